```python
import math
import jax, jax.numpy as jnp
from jax import lax
import numpy as np

D_MODEL = 1024
BATCH = 16
SEQ = 2048
DEPTH = 1

CHUNK = 64
GDN_HEADS = 4
GDN_DK = 128
GDN_DV = 128
GDN_QK = GDN_HEADS * GDN_DK
GDN_VW = GDN_HEADS * GDN_DV
CONV_K = 4
FX_HEADS = 8
FX_DH = 64
FX_W = FX_HEADS * FX_DH
Q_BLOCK = 128
PEER_HEADS = 8
PEER_NKEYS = 128
PEER_DKEY = 256
PEER_TOPK = 16
PEER_NEXPERTS = PEER_NKEYS * PEER_NKEYS
PEER_TOKEN_BLOCK = 128
DEEPNORM_ALPHA = (2.0 * DEPTH) ** 0.25
DEEPNORM_BETA = (8.0 * DEPTH) ** -0.25
IN_SIZES = (GDN_QK, GDN_QK, GDN_VW, GDN_VW, GDN_HEADS, GDN_HEADS, FX_W, FX_W, FX_W, FX_HEADS, D_MODEL, D_MODEL)
IN_WIDTH = sum(IN_SIZES)
IN_VALUE_SEGMENTS = (2, 8)

kernel_name = 'hybrid_gdn_fox_peer_deepnorm'


def _layernorm(x, g, b, eps=1e-5):
    xf = x.astype(jnp.float32)
    mu = jnp.mean(xf, axis=-1, keepdims=True)
    var = jnp.mean(jnp.square(xf - mu), axis=-1, keepdims=True)
    y = (xf - mu) * lax.rsqrt(var + eps) * g.astype(jnp.float32) + b.astype(jnp.float32)
    return y.astype(x.dtype)


def _l2norm(x, eps=1e-6):
    return x * lax.rsqrt(jnp.sum(jnp.square(x), axis=-1, keepdims=True) + eps)


def _causal_depthwise_conv(x, w):
    kw, c = w.shape
    return lax.conv_general_dilated(x, w[:, None, :].astype(x.dtype), window_strides=(1,), padding=[(kw - 1, 0)], dimension_numbers=('NWC', 'WIO', 'NWC'), feature_group_count=c)


def _gated_delta_rule(q, k, v, g, beta):
    b, h, s, dk = q.shape
    dv = v.shape[-1]
    n = s // CHUNK
    q = q.reshape(b, h, n, CHUNK, dk)
    k = k.reshape(b, h, n, CHUNK, dk)
    v = v.reshape(b, h, n, CHUNK, dv)
    g = g.reshape(b, h, n, CHUNK)
    beta = beta.reshape(b, h, n, CHUNK)
    gc = jnp.cumsum(g, axis=-1)
    tril = jnp.tril(jnp.ones((CHUNK, CHUNK), bool))
    strict = jnp.tril(jnp.ones((CHUNK, CHUNK), bool), -1)
    diff = gc[..., :, None] - gc[..., None, :]
    decay = jnp.where(tril, jnp.exp(jnp.where(tril, diff, 0.0)), 0.0)
    kb = k * beta[..., None]
    m = jnp.where(strict, jnp.einsum('bhnid,bhnjd->bhnij', kb, k) * decay, 0.0)
    a = m + jnp.eye(CHUNK, dtype=m.dtype)
    rhs = jnp.concatenate([v * beta[..., None], kb * jnp.exp(gc)[..., None]], axis=-1)
    sol = lax.linalg.triangular_solve(a, rhs, left_side=True, lower=True)
    u, w = sol[..., :dv], sol[..., dv:]
    qk = jnp.einsum('bhnid,bhnjd->bhnij', q, k) * decay
    qg = q * jnp.exp(gc)[..., None]
    glast = gc[..., -1]
    kd = k * jnp.exp(glast[..., None] - gc)[..., None]

    def step(state, inp):
        u_n, w_n, qk_n, qg_n, kd_n, gl_n = inp
        v_new = u_n - jnp.einsum('bhcd,bhde->bhce', w_n, state)
        o = jnp.einsum('bhcd,bhde->bhce', qg_n, state) + jnp.einsum('bhij,bhje->bhie', qk_n, v_new)
        state = state * jnp.exp(gl_n)[..., None, None] + jnp.einsum('bhcd,bhce->bhde', kd_n, v_new)
        return state, o

    xs = (jnp.moveaxis(u, 2, 0), jnp.moveaxis(w, 2, 0), jnp.moveaxis(qk, 2, 0), jnp.moveaxis(qg, 2, 0), jnp.moveaxis(kd, 2, 0), jnp.moveaxis(glast, 2, 0))
    s0 = jnp.zeros((b, h, dk, dv), jnp.float32)
    _, o = lax.scan(step, s0, xs)
    return jnp.moveaxis(o, 0, 2).reshape(b, h, s, dv)


def _gdn_branch(gq, gk, gv, gz, ga, gb, conv_w, a_log, dt_bias, norm_w, w_out):
    f32 = jnp.float32
    b, s, _ = gq.shape
    qkv = jax.nn.silu(_causal_depthwise_conv(jnp.concatenate([gq, gk, gv], axis=-1), conv_w))
    q, k, v = jnp.split(qkv, [GDN_QK, 2 * GDN_QK], axis=-1)
    q = _l2norm(q.reshape(b, s, GDN_HEADS, GDN_DK).transpose(0, 2, 1, 3).astype(f32)) * (GDN_DK ** -0.5)
    k = _l2norm(k.reshape(b, s, GDN_HEADS, GDN_DK).transpose(0, 2, 1, 3).astype(f32))
    v = v.reshape(b, s, GDN_HEADS, GDN_DV).transpose(0, 2, 1, 3).astype(f32)
    g = -jnp.exp(a_log.astype(f32)) * jax.nn.softplus(ga.astype(f32) + dt_bias.astype(f32))
    beta = jax.nn.sigmoid(gb.astype(f32))
    o = _gated_delta_rule(q, k, v, g.transpose(0, 2, 1), beta.transpose(0, 2, 1))
    o = o.transpose(0, 2, 1, 3)
    o = o * lax.rsqrt(jnp.mean(jnp.square(o), axis=-1, keepdims=True) + 1e-6) * norm_w.astype(f32)
    z = gz.reshape(b, s, GDN_HEADS, GDN_DV).astype(f32)
    o = (o * jax.nn.silu(z)).reshape(b, s, GDN_VW).astype(gq.dtype)
    return o @ w_out


def _forgetting_attention(q, k, v, log_f):
    s_len = q.shape[1]
    scale = FX_DH ** -0.5
    c = jnp.cumsum(log_f, axis=1).transpose(0, 2, 1)
    outs = []
    for i in range(s_len // Q_BLOCK):
        lo, hi = i * Q_BLOCK, (i + 1) * Q_BLOCK
        sc = jnp.einsum('bqhd,bkhd->bhqk', q[:, lo:hi], k[:, :hi]).astype(jnp.float32) * scale
        sc = sc + c[:, :, lo:hi, None] - c[:, :, None, :hi]
        causal = (lo + jnp.arange(Q_BLOCK))[:, None] >= jnp.arange(hi)[None, :]
        sc = jnp.where(causal, sc, -1e30)
        p = jax.nn.softmax(sc, axis=-1)
        outs.append(jnp.einsum('bhqk,bkhd->bqhd', p.astype(v.dtype), v[:, :hi]))
    return jnp.concatenate(outs, axis=1)


def _fox_branch(fq, fk, fv, ff, f_bias, w_out):
    b, s, _ = fq.shape
    q = fq.reshape(b, s, FX_HEADS, FX_DH)
    k = fk.reshape(b, s, FX_HEADS, FX_DH)
    v = fv.reshape(b, s, FX_HEADS, FX_DH)
    log_f = jax.nn.log_sigmoid(ff.astype(jnp.float32) + f_bias.astype(jnp.float32))
    o = _forgetting_attention(q, k, v, log_f).reshape(b, s, FX_W)
    return o @ w_out


def _peer(h, wq, keys, u_tab, v_tab):
    b, s, d = h.shape
    t = b * s
    hf = h.reshape(t, d)
    q = (hf @ wq).reshape(t, PEER_HEADS, 2, PEER_DKEY // 2)
    scores = jnp.einsum('thpd,hpnd->thpn', q, keys).astype(jnp.float32)
    top_s, top_i = lax.top_k(scores, PEER_TOPK)
    cand = top_s[:, :, 0, :, None] + top_s[:, :, 1, None, :]
    cand_idx = top_i[:, :, 0, :, None] * PEER_NKEYS + top_i[:, :, 1, None, :]
    best_s, best_pos = lax.top_k(cand.reshape(t, PEER_HEADS, PEER_TOPK * PEER_TOPK), PEER_TOPK)
    expert = jnp.take_along_axis(cand_idx.reshape(t, PEER_HEADS, PEER_TOPK * PEER_TOPK), best_pos, axis=-1)
    gate = jax.nn.softmax(best_s, axis=-1).astype(h.dtype)
    nb = t // PEER_TOKEN_BLOCK
    hk = PEER_HEADS * PEER_TOPK
    expert = expert.reshape(nb, PEER_TOKEN_BLOCK, hk)
    gate = gate.reshape(nb, PEER_TOKEN_BLOCK, hk)
    hb = hf.reshape(nb, PEER_TOKEN_BLOCK, d)

    def block(args):
        h_blk, e_blk, g_blk = args
        u = jnp.take(u_tab, e_blk, axis=0)
        act = jax.nn.gelu(jnp.einsum('td,tkd->tk', h_blk, u), approximate=False) * g_blk
        vv = jnp.take(v_tab, e_blk, axis=0)
        return jnp.einsum('tk,tkd->td', act, vv)

    y = lax.map(block, (hb, expert, gate))
    return y.reshape(b, s, d)


def _layer(h, w_in, gdn_conv_w, gdn_a_log, gdn_dt_bias, gdn_norm_w, fox_f_bias, w_out_gdn, w_out_fox, w_o, ln1_g, ln1_b, peer_wq, peer_keys, peer_u, peer_v, ln2_g, ln2_b):
    proj = h @ w_in
    splits = np.cumsum(IN_SIZES)[:-1].tolist()
    gq, gk, gv, gz, ga, gb, fq, fk, fv, ff, gate_a, gate_b = jnp.split(proj, splits, axis=-1)
    y_a = _gdn_branch(gq, gk, gv, gz, ga, gb, gdn_conv_w, gdn_a_log, gdn_dt_bias, gdn_norm_w, w_out_gdn)
    y_b = _fox_branch(fq, fk, fv, ff, fox_f_bias, w_out_fox)
    mix = jax.nn.sigmoid(gate_a) * y_a + jax.nn.sigmoid(gate_b) * y_b
    h = _layernorm(DEEPNORM_ALPHA * h + mix @ w_o, ln1_g, ln1_b)
    h = _layernorm(DEEPNORM_ALPHA * h + _peer(h, peer_wq, peer_keys, peer_u, peer_v), ln2_g, ln2_b)
    return h


def setup_inputs(seed: int = 0) -> dict:
    key = jax.random.key(seed)
    ks = jax.random.split(key, 20)
    f32 = jnp.float32
    L = DEPTH
    beta = DEEPNORM_BETA
    col_scale = np.ones((IN_WIDTH,), np.float32)
    off = np.cumsum((0,) + IN_SIZES)
    for seg in IN_VALUE_SEGMENTS:
        col_scale[off[seg]:off[seg + 1]] = beta
    nrm = jax.random.normal
    x = nrm(ks[0], (BATCH, SEQ, D_MODEL), f32)
    w_in = nrm(ks[1], (L, D_MODEL, IN_WIDTH), f32) * (D_MODEL ** -0.5) * jnp.asarray(col_scale)
    gdn_conv_w = nrm(ks[2], (L, CONV_K, 2 * GDN_QK + GDN_VW), f32) * (CONV_K ** -0.5)
    gdn_a_log = jnp.log(jax.random.uniform(ks[3], (L, GDN_HEADS), f32, 1.0, 16.0))
    dt = jnp.exp(jax.random.uniform(ks[4], (L, GDN_HEADS), f32, math.log(1e-3), math.log(1e-1)))
    gdn_dt_bias = dt + jnp.log(-jnp.expm1(-dt))
    gdn_norm_w = 1.0 + 0.02 * nrm(ks[5], (L, GDN_DV), f32)
    fox_f_bias = 2.0 + 0.1 * nrm(ks[6], (L, FX_HEADS), f32)
    w_out_gdn = nrm(ks[7], (L, GDN_VW, D_MODEL), f32) * (GDN_VW ** -0.5) * beta
    w_out_fox = nrm(ks[8], (L, FX_W, D_MODEL), f32) * (FX_W ** -0.5) * beta
    w_o = nrm(ks[9], (L, D_MODEL, D_MODEL), f32) * (D_MODEL ** -0.5) * beta
    ln1_g = 1.0 + 0.02 * nrm(ks[10], (L, D_MODEL), f32)
    ln1_b = 0.02 * nrm(ks[11], (L, D_MODEL), f32)
    peer_wq = nrm(ks[12], (L, D_MODEL, PEER_HEADS * PEER_DKEY), f32) * (D_MODEL ** -0.5)
    peer_keys = nrm(ks[13], (L, PEER_HEADS, 2, PEER_NKEYS, PEER_DKEY // 2), f32) * ((PEER_DKEY // 2) ** -0.5)
    peer_u = nrm(ks[14], (L, PEER_NEXPERTS, D_MODEL), f32) * (D_MODEL ** -0.5)
    peer_v = nrm(ks[15], (L, PEER_NEXPERTS, D_MODEL), f32) * beta
    ln2_g = 1.0 + 0.02 * nrm(ks[16], (L, D_MODEL), f32)
    ln2_b = 0.02 * nrm(ks[17], (L, D_MODEL), f32)
    return {'x': x, 'w_in': w_in, 'gdn_conv_w': gdn_conv_w, 'gdn_a_log': gdn_a_log, 'gdn_dt_bias': gdn_dt_bias, 'gdn_norm_w': gdn_norm_w, 'fox_f_bias': fox_f_bias, 'w_out_gdn': w_out_gdn, 'w_out_fox': w_out_fox, 'w_o': w_o, 'ln1_g': ln1_g, 'ln1_b': ln1_b, 'peer_wq': peer_wq, 'peer_keys': peer_keys, 'peer_u': peer_u, 'peer_v': peer_v, 'ln2_g': ln2_g, 'ln2_b': ln2_b}


def reference(x, w_in, gdn_conv_w, gdn_a_log, gdn_dt_bias, gdn_norm_w, fox_f_bias, w_out_gdn, w_out_fox, w_o, ln1_g, ln1_b, peer_wq, peer_keys, peer_u, peer_v, ln2_g, ln2_b):
    h = x
    for l in range(DEPTH):
        h = _layer(h, w_in[l], gdn_conv_w[l], gdn_a_log[l], gdn_dt_bias[l], gdn_norm_w[l], fox_f_bias[l], w_out_gdn[l], w_out_fox[l], w_o[l], ln1_g[l], ln1_b[l], peer_wq[l], peer_keys[l], peer_u[l], peer_v[l], ln2_g[l], ln2_b[l])
    return h
```

```python
import functools

import jax
import jax.numpy as jnp
from jax import lax
from jax.experimental import pallas as pl
from jax.experimental.pallas import tpu as pltpu

F32 = jnp.float32
BF16 = jnp.bfloat16
HI = lax.Precision.HIGHEST

LANES = 128
CHUNK = 64
GDN_HEADS = 4
GDN_DK = 128
FX_HEADS = 8
FX_DH = 64
PEER_HEADS = 8
PEER_NKEYS = 128
PEER_TOPK = 16
LN_EPS = 1e-5
VMEM_LIMIT = 48 * 1024 * 1024


def _dot(a, b, prec=None):
    return jnp.dot(a, b, preferred_element_type=F32, precision=prec)


def _dot_nt(a, b, prec=None):
    return lax.dot_general(a, b, (((1,), (1,)), ((), ())), preferred_element_type=F32, precision=prec)


def _dot_tn(a, b, prec=None):
    return lax.dot_general(a, b, (((0,), (0,)), ((), ())), preferred_element_type=F32, precision=prec)


def _softplus(x):
    return jnp.maximum(x, 0.0) + jnp.log1p(jnp.exp(-jnp.abs(x)))


def _layernorm(z, g, b):
    mu = jnp.mean(z, axis=-1, keepdims=True)
    zc = z - mu
    var = jnp.mean(zc * zc, axis=-1, keepdims=True)
    return zc * lax.rsqrt(var + LN_EPS) * g + b


def _mm_kernel(x_ref, w_ref, o_ref):
    o_ref[...] = _dot(x_ref[...].astype(BF16), w_ref[...])


def _in_proj(x2, w_big):
    t, d = x2.shape
    n = w_big.shape[1]
    tm = min(512, t)
    tn = 512
    return pl.pallas_call(
        _mm_kernel,
        grid=(t // tm, n // tn),
        in_specs=[pl.BlockSpec((tm, d), lambda i, j: (i, 0)),
                  pl.BlockSpec((d, tn), lambda i, j: (0, j))],
        out_specs=pl.BlockSpec((tm, tn), lambda i, j: (i, j)),
        out_shape=jax.ShapeDtypeStruct((t, n), F32),
        compiler_params=pltpu.CompilerParams(
            dimension_semantics=("parallel", "parallel"), vmem_limit_bytes=VMEM_LIMIT),
        name="in_proj",
    )(x2, w_big)


def _prep_kernel(x_ref, w_ref, par_ref, gexp_ref, bexp_ref, c_ref, ct_ref, carry_ref):
    ts = x_ref.shape[1]

    @pl.when(pl.program_id(1) == 0)
    def _():
        carry_ref[...] = jnp.zeros_like(carry_ref)

    small = _dot(x_ref[0], w_ref[...], HI)
    a_log = par_ref[0:1, :]
    dt_bias = par_ref[1:2, :]
    f_bias = par_ref[2:3, :]
    g = -jnp.exp(a_log) * _softplus(small + dt_bias)
    beta = jax.nn.sigmoid(small)
    lane = lax.broadcasted_iota(jnp.int32, (ts, LANES), 1)
    log_f = jnp.where((lane >= 8) & (lane < 16), -_softplus(-(small + f_bias)), 0.0)
    row = lax.broadcasted_iota(jnp.int32, (ts, ts), 0)
    col = lax.broadcasted_iota(jnp.int32, (ts, ts), 1)
    tril = (row >= col).astype(F32)
    c = _dot(tril, log_f, HI) + carry_ref[...]
    carry_ref[...] = c[ts - 1:ts, :]
    c_ref[0] = c
    ct_ref[0] = c.T[8:16, :]
    gexp_ref[0] = jnp.concatenate(
        [jnp.broadcast_to(g[:, h:h + 1], (ts, LANES)) for h in range(GDN_HEADS)], axis=1)
    bexp_ref[0] = jnp.concatenate(
        [jnp.broadcast_to(beta[:, GDN_HEADS + h:GDN_HEADS + h + 1], (ts, LANES)) for h in range(GDN_HEADS)], axis=1)


def _prep(x, w_small, params):
    b, s, d = x.shape
    ts = min(512, s)
    hw = GDN_HEADS * LANES
    return pl.pallas_call(
        _prep_kernel,
        grid=(b, s // ts),
        in_specs=[pl.BlockSpec((1, ts, d), lambda i, j: (i, j, 0)),
                  pl.BlockSpec((d, LANES), lambda i, j: (0, 0)),
                  pl.BlockSpec((8, LANES), lambda i, j: (0, 0))],
        out_specs=[pl.BlockSpec((1, ts, hw), lambda i, j: (i, j, 0)),
                   pl.BlockSpec((1, ts, hw), lambda i, j: (i, j, 0)),
                   pl.BlockSpec((1, ts, LANES), lambda i, j: (i, j, 0)),
                   pl.BlockSpec((1, 8, ts), lambda i, j: (i, 0, j))],
        out_shape=[jax.ShapeDtypeStruct((b, s, hw), F32),
                   jax.ShapeDtypeStruct((b, s, hw), F32),
                   jax.ShapeDtypeStruct((b, s, LANES), F32),
                   jax.ShapeDtypeStruct((b, 8, s), F32)],
        scratch_shapes=[pltpu.VMEM((1, LANES), F32)],
        compiler_params=pltpu.CompilerParams(
            dimension_semantics=("parallel", "arbitrary"), vmem_limit_bytes=VMEM_LIMIT),
        name="prep",
    )(x, w_small, params)


def _unit_lower_inverse(m, masks):
    eye, blk16, blk32 = masks
    md = jnp.where(blk16, m, 0.0)
    l1 = jnp.where(blk32 & jnp.logical_not(blk16), m, 0.0)
    l2 = jnp.where(blk32, 0.0, m)
    n1 = -md
    p = eye + n1
    n2 = _dot(n1, n1, HI)
    p = _dot(p, eye + n2, HI)
    n4 = _dot(n2, n2, HI)
    p = _dot(p, eye + n4, HI)
    n8 = _dot(n4, n4, HI)
    d_inv = _dot(p, eye + n8, HI)
    a32 = d_inv - _dot(_dot(d_inv, l1, HI), d_inv, HI)
    return a32 - _dot(_dot(a32, l2, HI), a32, HI)


def _gdn_kernel(q_ref, k_ref, v_ref, z_ref, g_ref, b_ref, cwq_ref, cwk_ref, cwv_ref, nw_ref,
                o_ref, qn, kn, vn, st):
    s = q_ref.shape[1]
    c = CHUNK
    row = lax.broadcasted_iota(jnp.int32, (s, LANES), 0)

    def conv_silu(x, w_ref):
        w = w_ref[...]
        y = x * w[3:4, :]
        for sh in (1, 2, 3):
            xs = jnp.where(row >= sh, pltpu.roll(x, sh, axis=0), 0.0)
            y = y + xs * w[3 - sh:4 - sh, :]
        return y * jax.nn.sigmoid(y)

    def l2norm(x):
        return x * lax.rsqrt(jnp.sum(x * x, axis=-1, keepdims=True) + 1e-6)

    qn[...] = l2norm(conv_silu(q_ref[0], cwq_ref)) * (GDN_DK ** -0.5)
    kn[...] = l2norm(conv_silu(k_ref[0], cwk_ref))
    vn[...] = conv_silu(v_ref[0], cwv_ref)
    st[...] = jnp.zeros_like(st)

    ri = lax.broadcasted_iota(jnp.int32, (c, c), 0)
    ci = lax.broadcasted_iota(jnp.int32, (c, c), 1)
    tril = ri >= ci
    strict = ri > ci
    t_inc = tril.astype(F32)
    eye = (ri == ci).astype(F32)
    blk16 = (ri >> 4) == (ci >> 4)
    blk32 = (ri >> 5) == (ci >> 5)
    l2 = lax.broadcasted_iota(jnp.int32, (c, 2 * LANES), 0)
    j2 = lax.broadcasted_iota(jnp.int32, (c, 2 * LANES), 1)
    ux = jnp.where((j2 >= c) | (l2 > j2), 1.0, 0.0).astype(F32)
    nw = nw_ref[...]

    def chunk(n, carry):
        r0 = pl.multiple_of(n * c, c)
        q = qn[pl.ds(r0, c), :]
        k = kn[pl.ds(r0, c), :]
        v = vn[pl.ds(r0, c), :]
        gb = g_ref[0, pl.ds(r0, c), :]
        bb = b_ref[0, pl.ds(r0, c), :]
        d = _dot(t_inc, jnp.concatenate([gb, gb], axis=1) * ux, HI)
        gc = d[:, LANES:]
        decay = jnp.where(tril, jnp.exp(d[:, :c]), 0.0)
        kb = k * bb
        m = jnp.where(strict, _dot_nt(kb, k, HI) * decay, 0.0)
        a_inv = _unit_lower_inverse(m, (eye, blk16, blk32))
        egc = jnp.exp(gc)
        sol = _dot(a_inv, jnp.concatenate([v * bb, kb * egc], axis=1), HI)
        u = sol[:, :LANES]
        w = sol[:, LANES:]
        qk = _dot_nt(q.astype(BF16), k.astype(BF16)) * decay
        qg = q * egc
        gl = gc[c - 1:c, :]
        kd = k * jnp.exp(gl - gc)
        state = st[...]
        state_b = state.astype(BF16)
        v_new = u - _dot(w.astype(BF16), state_b)
        v_new_b = v_new.astype(BF16)
        o = _dot(qg.astype(BF16), state_b) + _dot(qk.astype(BF16), v_new_b)
        st[...] = state * jnp.exp(gl) + _dot_tn(kd.astype(BF16), v_new_b)
        o = o * lax.rsqrt(jnp.mean(o * o, axis=-1, keepdims=True) + 1e-6) * nw
        z = z_ref[0, pl.ds(r0, c), :]
        o_ref[0, pl.ds(r0, c), :] = o * (z * jax.nn.sigmoid(z))
        return carry

    lax.fori_loop(0, s // c, chunk, 0)


def _gdn(proj3, gexp, bexp, conv_w, norm_w, cols):
    b, s, _ = proj3.shape
    cq, ck, cv, cz = cols

    def blk(c0):
        return pl.BlockSpec((1, s, LANES), lambda i, h: (i, 0, c0 + h))

    def cw(c0):
        return pl.BlockSpec((conv_w.shape[0], LANES), lambda i, h: (0, c0 + h))

    head = pl.BlockSpec((1, s, LANES), lambda i, h: (i, 0, h))
    return pl.pallas_call(
        _gdn_kernel,
        grid=(b, GDN_HEADS),
        in_specs=[blk(cq), blk(ck), blk(cv), blk(cz), head, head,
                  cw(0), cw(GDN_HEADS), cw(2 * GDN_HEADS),
                  pl.BlockSpec((1, LANES), lambda i, h: (0, 0))],
        out_specs=head,
        out_shape=jax.ShapeDtypeStruct((b, s, GDN_HEADS * LANES), F32),
        scratch_shapes=[pltpu.VMEM((s, LANES), F32), pltpu.VMEM((s, LANES), F32),
                        pltpu.VMEM((s, LANES), F32), pltpu.VMEM((GDN_DK, LANES), F32)],
        compiler_params=pltpu.CompilerParams(
            dimension_semantics=("parallel", "parallel"), vmem_limit_bytes=VMEM_LIMIT),
        name="gdn",
    )(proj3, proj3, proj3, proj3, gexp, bexp, conv_w, conv_w, conv_w, norm_w)


def _fox_kernel(q_ref, k_ref, v_ref, c_ref, ct_ref, o_ref):
    tq = q_ref.shape[1]
    hp = pl.program_id(1)
    qi = pl.program_id(2)
    q = q_ref[0]
    cblk = c_ref[0]
    lane = lax.broadcasted_iota(jnp.int32, (tq, LANES), 1)
    qpos = qi * tq + lax.broadcasted_iota(jnp.int32, (tq, tq), 0)
    kofs = lax.broadcasted_iota(jnp.int32, (tq, tq), 1)
    outs = []
    for j in range(LANES // FX_DH):
        hh = hp * (LANES // FX_DH) + j
        ccol = jnp.sum(jnp.where(lane == 8 + hh, cblk, 0.0), axis=-1, keepdims=True)
        qh = (q[:, j * FX_DH:(j + 1) * FX_DH] * (FX_DH ** -0.5)).astype(BF16)

        def body(kv, carry, j=j, hh=hh, ccol=ccol, qh=qh):
            m, l, acc = carry
            k0 = pl.multiple_of(kv * tq, tq)
            kh = k_ref[0, pl.ds(k0, tq), j * FX_DH:(j + 1) * FX_DH].astype(BF16)
            vh = v_ref[0, pl.ds(k0, tq), j * FX_DH:(j + 1) * FX_DH].astype(BF16)
            sc = _dot_nt(qh, kh) + ccol - ct_ref[0, pl.ds(hh, 1), pl.ds(k0, tq)]
            sc = jnp.where(qpos >= k0 + kofs, sc, -1e30)
            m_new = jnp.maximum(m, jnp.max(sc, axis=-1, keepdims=True))
            a = jnp.exp(m - m_new)
            p = jnp.exp(sc - m_new)
            l = a * l + jnp.sum(p, axis=-1, keepdims=True)
            acc = a * acc + _dot(p.astype(BF16), vh)
            return m_new, l, acc

        init = (jnp.full((tq, 1), -1e30, F32), jnp.zeros((tq, 1), F32), jnp.zeros((tq, FX_DH), F32))
        _, l, acc = lax.fori_loop(0, qi + 1, body, init)
        outs.append(acc / l)
    o_ref[0] = jnp.concatenate(outs, axis=1)


def _fox(proj3, c, ct, cols):
    b, s, _ = proj3.shape
    cq, ck, cv = cols
    tq = min(256, s)
    npair = FX_HEADS * FX_DH // LANES
    return pl.pallas_call(
        _fox_kernel,
        grid=(b, npair, s // tq),
        in_specs=[pl.BlockSpec((1, tq, LANES), lambda i, h, t: (i, t, cq + h)),
                  pl.BlockSpec((1, s, LANES), lambda i, h, t: (i, 0, ck + h)),
                  pl.BlockSpec((1, s, LANES), lambda i, h, t: (i, 0, cv + h)),
                  pl.BlockSpec((1, tq, LANES), lambda i, h, t: (i, t, 0)),
                  pl.BlockSpec((1, 8, s), lambda i, h, t: (i, 0, 0))],
        out_specs=pl.BlockSpec((1, tq, LANES), lambda i, h, t: (i, t, h)),
        out_shape=jax.ShapeDtypeStruct((b, s, npair * LANES), F32),
        compiler_params=pltpu.CompilerParams(
            dimension_semantics=("parallel", "parallel", "parallel"), vmem_limit_bytes=VMEM_LIMIT),
        name="fox",
    )(proj3, proj3, proj3, c, ct)


def _mix_kernel(oa_ref, ob_ref, ga_ref, gb_ref, x_ref, wa_ref, wb_ref, wo_ref, g1_ref, b1_ref, o_ref, *, alpha):
    ya = _dot(oa_ref[...].astype(BF16), wa_ref[...])
    yb = _dot(ob_ref[...].astype(BF16), wb_ref[...])
    mix = jax.nn.sigmoid(ga_ref[...]) * ya + jax.nn.sigmoid(gb_ref[...]) * yb
    z = alpha * x_ref[...] + _dot(mix.astype(BF16), wo_ref[...])
    o_ref[...] = _layernorm(z, g1_ref[...], b1_ref[...])


def _mix(oa, ob, proj, x2, wa, wb, wo, g1, b1, alpha):
    t, d = x2.shape
    tm = min(512, t)
    w = oa.shape[1]
    full = lambda r, c: pl.BlockSpec((r, c), lambda i: (0, 0))
    return pl.pallas_call(
        functools.partial(_mix_kernel, alpha=alpha),
        grid=(t // tm,),
        in_specs=[pl.BlockSpec((tm, w), lambda i: (i, 0)),
                  pl.BlockSpec((tm, w), lambda i: (i, 0)),
                  pl.BlockSpec((tm, d), lambda i: (i, 0)),
                  pl.BlockSpec((tm, d), lambda i: (i, 1)),
                  pl.BlockSpec((tm, d), lambda i: (i, 0)),
                  full(w, d), full(w, d), full(d, d), full(1, d), full(1, d)],
        out_specs=pl.BlockSpec((tm, d), lambda i: (i, 0)),
        out_shape=jax.ShapeDtypeStruct((t, d), F32),
        compiler_params=pltpu.CompilerParams(
            dimension_semantics=("parallel",), vmem_limit_bytes=VMEM_LIMIT),
        name="mix",
    )(oa, ob, proj, proj, x2, wa, wb, wo, g1, b1)


def _route_kernel(h_ref, wq_ref, keys_ref, e_ref, g_ref, q_scr, s_scr, i_scr, cand_scr, cidx_scr, best_scr, eh_scr, eall_scr):
    tm = h_ref.shape[0]
    nk = PEER_NKEYS
    kk = PEER_TOPK
    q = _dot(h_ref[...].astype(BF16), wq_ref[...])
    for j in range(2 * PEER_HEADS):
        q_scr[j] = q[:, j * LANES:(j + 1) * LANES].astype(BF16)
    iota_k = lax.broadcasted_iota(jnp.int32, (nk, tm), 0)
    iota_c = lax.broadcasted_iota(jnp.int32, (kk * kk, tm), 0)
    neg = jnp.float32(-jnp.inf)

    def head(hd, carry):
        for p in range(2):
            vals = _dot_nt(keys_ref[hd * 2 + p], q_scr[hd * 2 + p])
            for r in range(kk):
                m = jnp.max(vals, axis=0, keepdims=True)
                am = jnp.min(jnp.where(vals == m, iota_k, nk), axis=0, keepdims=True)
                s_scr[p * kk + r:p * kk + r + 1, :] = m
                i_scr[p * kk + r:p * kk + r + 1, :] = am
                vals = jnp.where(iota_k == am, neg, vals)
        s1 = s_scr[kk:2 * kk, :]
        i1 = i_scr[kk:2 * kk, :]
        for a in range(kk):
            cand_scr[a * kk:(a + 1) * kk, :] = s_scr[a:a + 1, :] + s1
            cidx_scr[a * kk:(a + 1) * kk, :] = i_scr[a:a + 1, :] * nk + i1
        vals = cand_scr[...]
        cidx = cidx_scr[...]
        for r in range(kk):
            m = jnp.max(vals, axis=0, keepdims=True)
            am = jnp.min(jnp.where(vals == m, iota_c, kk * kk), axis=0, keepdims=True)
            sel = iota_c == am
            best_scr[r:r + 1, :] = m
            eh_scr[r:r + 1, :] = jnp.max(jnp.where(sel, cidx, -1), axis=0, keepdims=True)
            vals = jnp.where(sel, neg, vals)
        bs = best_scr[...]
        ex = jnp.exp(bs - bs[0:1, :])
        r0 = pl.multiple_of(hd * kk, kk)
        g_ref[pl.ds(r0, kk), :] = ex / jnp.sum(ex, axis=0, keepdims=True)
        eall_scr[pl.ds(r0, kk), :] = eh_scr[...]
        return carry

    lax.fori_loop(0, PEER_HEADS, head, 0)
    e_ref[...] = eall_scr[...].T


def _route(h1, wq, keys):
    t, d = h1.shape
    tm = min(256, t)
    nq = wq.shape[1]
    hk = PEER_HEADS * PEER_TOPK
    kk = PEER_TOPK
    return pl.pallas_call(
        _route_kernel,
        grid=(t // tm,),
        in_specs=[pl.BlockSpec((tm, d), lambda i: (i, 0)),
                  pl.BlockSpec((d, nq), lambda i: (0, 0)),
                  pl.BlockSpec(keys.shape, lambda i: (0, 0, 0))],
        out_specs=[pl.BlockSpec((tm, hk), lambda i: (i, 0)),
                   pl.BlockSpec((hk, tm), lambda i: (0, i))],
        out_shape=[jax.ShapeDtypeStruct((t, hk), jnp.int32),
                   jax.ShapeDtypeStruct((hk, t), F32)],
        scratch_shapes=[pltpu.VMEM((2 * PEER_HEADS, tm, LANES), BF16),
                        pltpu.VMEM((2 * kk, tm), F32), pltpu.VMEM((2 * kk, tm), jnp.int32),
                        pltpu.VMEM((kk * kk, tm), F32), pltpu.VMEM((kk * kk, tm), jnp.int32),
                        pltpu.VMEM((kk, tm), F32), pltpu.VMEM((kk, tm), jnp.int32),
                        pltpu.VMEM((hk, tm), jnp.int32)],
        compiler_params=pltpu.CompilerParams(
            dimension_semantics=("parallel",), vmem_limit_bytes=VMEM_LIMIT),
        name="route",
    )(h1, wq, keys)


def _peer_kernel(ids_ref, idn_ref, h_ref, gt_ref, u_hbm, v_hbm, g2_ref, b2_ref, o_ref,
                 ubuf, vbuf, usem, vsem, y_scr, *, tt, alpha):
    i = pl.program_id(0)
    n = pl.num_programs(0)
    hk = ids_ref.shape[1]
    slot = i % 2

    def row_copy(tab, buf, sem, e, sl, r):
        return pltpu.make_async_copy(tab.at[pl.ds(e, 1)], buf.at[sl, pl.ds(r, 1)], sem.at[sl])

    def issue(idref, sl):
        def tok(t, carry):
            for k in range(hk):
                e = idref[t, k]
                r = t * hk + k
                row_copy(u_hbm, ubuf, usem, e, sl, r).start()
                row_copy(v_hbm, vbuf, vsem, e, sl, r).start()
            return carry
        lax.fori_loop(0, tt, tok, 0)

    @pl.when(i == 0)
    def _():
        issue(ids_ref, 0)

    @pl.when(i + 1 < n)
    def _():
        issue(idn_ref, 1 - slot)

    pltpu.make_async_copy(u_hbm.at[pl.ds(0, tt * hk)], ubuf.at[slot], usem.at[slot]).wait()
    pltpu.make_async_copy(v_hbm.at[pl.ds(0, tt * hk)], vbuf.at[slot], vsem.at[slot]).wait()

    d = h_ref.shape[1]
    gt = gt_ref[...]
    lane = lax.broadcasted_iota(jnp.int32, gt.shape, 1)
    tok0 = (i * tt) % LANES
    for t in range(tt):
        u = ubuf[slot, t * hk:(t + 1) * hk, :]
        prod = u * h_ref[t:t + 1, :]
        part = prod[:, 0:LANES]
        for c in range(1, d // LANES):
            part = part + prod[:, c * LANES:(c + 1) * LANES]
        pre = jnp.sum(part, axis=-1, keepdims=True)
        gate = jnp.sum(jnp.where(lane == tok0 + t, gt, 0.0), axis=-1, keepdims=True)
        act = 0.5 * pre * (1.0 + lax.erf(pre * (2.0 ** -0.5))) * gate
        v = vbuf[slot, t * hk:(t + 1) * hk, :]
        y_scr[t:t + 1, :] = jnp.sum(act * v, axis=0, keepdims=True)
    z = alpha * h_ref[...] + y_scr[...]
    o_ref[...] = _layernorm(z, g2_ref[...], b2_ref[...])


def _peer(ids, gt, h1, u_tab, v_tab, g2, b2, alpha):
    t, d = h1.shape
    hk = ids.shape[1]
    tt = 8
    nsteps = t // tt
    per_lane_blk = LANES // tt
    return pl.pallas_call(
        functools.partial(_peer_kernel, tt=tt, alpha=alpha),
        grid=(nsteps,),
        in_specs=[pl.BlockSpec((tt, hk), lambda i: (i, 0), memory_space=pltpu.SMEM),
                  pl.BlockSpec((tt, hk), lambda i: (jnp.minimum(i + 1, nsteps - 1), 0), memory_space=pltpu.SMEM),
                  pl.BlockSpec((tt, d), lambda i: (i, 0)),
                  pl.BlockSpec((hk, LANES), lambda i: (0, i // per_lane_blk)),
                  pl.BlockSpec(memory_space=pl.ANY),
                  pl.BlockSpec(memory_space=pl.ANY),
                  pl.BlockSpec((1, d), lambda i: (0, 0)),
                  pl.BlockSpec((1, d), lambda i: (0, 0))],
        out_specs=pl.BlockSpec((tt, d), lambda i: (i, 0)),
        out_shape=jax.ShapeDtypeStruct((t, d), F32),
        scratch_shapes=[pltpu.VMEM((2, tt * hk, d), F32), pltpu.VMEM((2, tt * hk, d), F32),
                        pltpu.SemaphoreType.DMA((2,)), pltpu.SemaphoreType.DMA((2,)),
                        pltpu.VMEM((tt, d), F32)],
        compiler_params=pltpu.CompilerParams(
            dimension_semantics=("arbitrary",), vmem_limit_bytes=VMEM_LIMIT),
        name="peer",
    )(ids, ids, h1, gt, u_tab, v_tab, g2, b2)


def _layer(h, w_in, conv_w, a_log, dt_bias, norm_w, f_bias, w_out_gdn, w_out_fox, w_o, ln1_g, ln1_b,
           peer_wq, peer_keys, peer_u, peer_v, ln2_g, ln2_b, alpha):
    b, s, d = h.shape
    t = b * s
    qk = GDN_HEADS * GDN_DK
    fw = FX_HEADS * FX_DH
    o_gz = 4 * qk
    o_ga = o_gz
    o_fq = o_ga + 2 * GDN_HEADS
    o_ff = o_fq + 3 * fw
    o_gate = o_ff + FX_HEADS
    w_big = jnp.concatenate([w_in[:, o_gate:], w_in[:, :o_gz], w_in[:, o_fq:o_ff]], axis=1).astype(BF16)
    n_small = 2 * GDN_HEADS + FX_HEADS
    w_small = jnp.concatenate([w_in[:, o_ga:o_fq], w_in[:, o_ff:o_gate],
                               jnp.zeros((d, LANES - n_small), F32)], axis=1)
    params = jnp.zeros((8, LANES), F32)
    params = params.at[0, :GDN_HEADS].set(a_log).at[1, :GDN_HEADS].set(dt_bias)
    params = params.at[2, 2 * GDN_HEADS:n_small].set(f_bias)

    x2 = h.reshape(t, d)
    proj = _in_proj(x2, w_big)
    gexp, bexp, c, ct = _prep(h, w_small, params)
    proj3 = proj.reshape(b, s, proj.shape[1])
    gdn0 = 2 * d // LANES
    oa = _gdn(proj3, gexp, bexp, conv_w, norm_w.reshape(1, LANES),
              (gdn0, gdn0 + GDN_HEADS, gdn0 + 2 * GDN_HEADS, gdn0 + 3 * GDN_HEADS))
    fox0 = gdn0 + 4 * GDN_HEADS
    npair = fw // LANES
    ob = _fox(proj3, c, ct, (fox0, fox0 + npair, fox0 + 2 * npair))
    h1 = _mix(oa.reshape(t, qk), ob.reshape(t, fw), proj, x2,
              w_out_gdn.astype(BF16), w_out_fox.astype(BF16), w_o.astype(BF16),
              ln1_g.reshape(1, d), ln1_b.reshape(1, d), alpha)
    keys = peer_keys.reshape(2 * PEER_HEADS, PEER_NKEYS, peer_keys.shape[-1]).astype(BF16)
    ids, gt = _route(h1, peer_wq.astype(BF16), keys)
    out = _peer(ids, gt, h1, peer_u, peer_v, ln2_g.reshape(1, d), ln2_b.reshape(1, d), alpha)
    return out.reshape(b, s, d)


def kernel(x, w_in, gdn_conv_w, gdn_a_log, gdn_dt_bias, gdn_norm_w, fox_f_bias, w_out_gdn, w_out_fox, w_o,
           ln1_g, ln1_b, peer_wq, peer_keys, peer_u, peer_v, ln2_g, ln2_b):
    depth = w_in.shape[0]
    alpha = (2.0 * depth) ** 0.25
    h = x
    for l in range(depth):
        h = _layer(h, w_in[l], gdn_conv_w[l], gdn_a_log[l], gdn_dt_bias[l], gdn_norm_w[l], fox_f_bias[l],
                   w_out_gdn[l], w_out_fox[l], w_o[l], ln1_g[l], ln1_b[l], peer_wq[l], peer_keys[l],
                   peer_u[l], peer_v[l], ln2_g[l], ln2_b[l], alpha)
    return h
```

```python
import functools

import jax
import jax.numpy as jnp
from jax import lax
from jax.experimental import pallas as pl
from jax.experimental.pallas import tpu as pltpu

F32 = jnp.float32
BF16 = jnp.bfloat16
HI = lax.Precision.HIGHEST

LANES = 128
CHUNK = 64
GDN_HEADS = 4
GDN_DK = 128
FX_HEADS = 8
FX_DH = 64
PEER_HEADS = 8
PEER_NKEYS = 128
PEER_TOPK = 16
LN_EPS = 1e-5
VMEM_LIMIT = 48 * 1024 * 1024


def _dot(a, b, prec=None):
    return jnp.dot(a, b, preferred_element_type=F32, precision=prec)


def _dot_nt(a, b, prec=None):
    return lax.dot_general(a, b, (((1,), (1,)), ((), ())), preferred_element_type=F32, precision=prec)


def _dot_tn(a, b, prec=None):
    return lax.dot_general(a, b, (((0,), (0,)), ((), ())), preferred_element_type=F32, precision=prec)


def _softplus(x):
    return jnp.maximum(x, 0.0) + jnp.log1p(jnp.exp(-jnp.abs(x)))


def _layernorm(z, g, b):
    mu = jnp.mean(z, axis=-1, keepdims=True)
    zc = z - mu
    var = jnp.mean(zc * zc, axis=-1, keepdims=True)
    return zc * lax.rsqrt(var + LN_EPS) * g + b


def _mm_kernel(x_ref, w_ref, o_ref):
    o_ref[...] = _dot(x_ref[...].astype(BF16), w_ref[...])


def _in_proj(x2, w_big):
    t, d = x2.shape
    n = w_big.shape[1]
    tm = min(512, t)
    tn = 512
    return pl.pallas_call(
        _mm_kernel,
        grid=(t // tm, n // tn),
        in_specs=[pl.BlockSpec((tm, d), lambda i, j: (i, 0)),
                  pl.BlockSpec((d, tn), lambda i, j: (0, j))],
        out_specs=pl.BlockSpec((tm, tn), lambda i, j: (i, j)),
        out_shape=jax.ShapeDtypeStruct((t, n), F32),
        compiler_params=pltpu.CompilerParams(
            dimension_semantics=("parallel", "parallel"), vmem_limit_bytes=VMEM_LIMIT),
        name="in_proj",
    )(x2, w_big)


def _prep_kernel(x_ref, w_ref, par_ref, gexp_ref, bexp_ref, c_ref, ct_ref, carry_ref):
    ts = x_ref.shape[1]

    @pl.when(pl.program_id(1) == 0)
    def _():
        carry_ref[...] = jnp.zeros_like(carry_ref)

    small = _dot(x_ref[0], w_ref[...], HI)
    a_log = par_ref[0:1, :]
    dt_bias = par_ref[1:2, :]
    f_bias = par_ref[2:3, :]
    g = -jnp.exp(a_log) * _softplus(small + dt_bias)
    beta = jax.nn.sigmoid(small)
    lane = lax.broadcasted_iota(jnp.int32, (ts, LANES), 1)
    log_f = jnp.where((lane >= 8) & (lane < 16), -_softplus(-(small + f_bias)), 0.0)
    row = lax.broadcasted_iota(jnp.int32, (ts, ts), 0)
    col = lax.broadcasted_iota(jnp.int32, (ts, ts), 1)
    tril = (row >= col).astype(F32)
    c = _dot(tril, log_f, HI) + carry_ref[...]
    carry_ref[...] = c[ts - 1:ts, :]
    c_ref[0] = c
    ct_ref[0] = c.T[8:16, :]
    gexp_ref[0] = jnp.concatenate(
        [jnp.broadcast_to(g[:, h:h + 1], (ts, LANES)) for h in range(GDN_HEADS)], axis=1)
    bexp_ref[0] = jnp.concatenate(
        [jnp.broadcast_to(beta[:, GDN_HEADS + h:GDN_HEADS + h + 1], (ts, LANES)) for h in range(GDN_HEADS)], axis=1)


def _prep(x, w_small, params):
    b, s, d = x.shape
    ts = min(512, s)
    hw = GDN_HEADS * LANES
    return pl.pallas_call(
        _prep_kernel,
        grid=(b, s // ts),
        in_specs=[pl.BlockSpec((1, ts, d), lambda i, j: (i, j, 0)),
                  pl.BlockSpec((d, LANES), lambda i, j: (0, 0)),
                  pl.BlockSpec((8, LANES), lambda i, j: (0, 0))],
        out_specs=[pl.BlockSpec((1, ts, hw), lambda i, j: (i, j, 0)),
                   pl.BlockSpec((1, ts, hw), lambda i, j: (i, j, 0)),
                   pl.BlockSpec((1, ts, LANES), lambda i, j: (i, j, 0)),
                   pl.BlockSpec((1, 8, ts), lambda i, j: (i, 0, j))],
        out_shape=[jax.ShapeDtypeStruct((b, s, hw), F32),
                   jax.ShapeDtypeStruct((b, s, hw), F32),
                   jax.ShapeDtypeStruct((b, s, LANES), F32),
                   jax.ShapeDtypeStruct((b, 8, s), F32)],
        scratch_shapes=[pltpu.VMEM((1, LANES), F32)],
        compiler_params=pltpu.CompilerParams(
            dimension_semantics=("parallel", "arbitrary"), vmem_limit_bytes=VMEM_LIMIT),
        name="prep",
    )(x, w_small, params)


def _unit_lower_inverse(m, masks):
    eye, blk16, blk32 = masks
    md = jnp.where(blk16, m, 0.0)
    l1 = jnp.where(blk32 & jnp.logical_not(blk16), m, 0.0)
    l2 = jnp.where(blk32, 0.0, m)
    n1 = -md
    p = eye + n1
    n2 = _dot(n1, n1, HI)
    p = _dot(p, eye + n2, HI)
    n4 = _dot(n2, n2, HI)
    p = _dot(p, eye + n4, HI)
    n8 = _dot(n4, n4, HI)
    d_inv = _dot(p, eye + n8, HI)
    a32 = d_inv - _dot(_dot(d_inv, l1, HI), d_inv, HI)
    return a32 - _dot(_dot(a32, l2, HI), a32, HI)


def _gdn_kernel(q_ref, k_ref, v_ref, z_ref, g_ref, b_ref, cwq_ref, cwk_ref, cwv_ref, nw_ref,
                o_ref, qn, kn, vn, st):
    s = q_ref.shape[1]
    c = CHUNK
    nh = q_ref.shape[2] // LANES
    row = lax.broadcasted_iota(jnp.int32, (s, LANES), 0)

    def conv_silu(x, w):
        y = x * w[3:4, :]
        for sh in (1, 2, 3):
            xs = jnp.where(row >= sh, pltpu.roll(x, sh, axis=0), 0.0)
            y = y + xs * w[3 - sh:4 - sh, :]
        return y * jax.nn.sigmoid(y)

    def l2norm(x):
        return x * lax.rsqrt(jnp.sum(x * x, axis=-1, keepdims=True) + 1e-6)

    for j in range(nh):
        hs = slice(j * LANES, (j + 1) * LANES)
        qn[:, hs] = l2norm(conv_silu(q_ref[0, :, hs], cwq_ref[:, hs])) * (GDN_DK ** -0.5)
        kn[:, hs] = l2norm(conv_silu(k_ref[0, :, hs], cwk_ref[:, hs]))
        vn[:, hs] = conv_silu(v_ref[0, :, hs], cwv_ref[:, hs])
    st[...] = jnp.zeros_like(st)

    ri = lax.broadcasted_iota(jnp.int32, (c, c), 0)
    ci = lax.broadcasted_iota(jnp.int32, (c, c), 1)
    tril = ri >= ci
    strict = ri > ci
    t_inc = tril.astype(F32)
    eye = (ri == ci).astype(F32)
    blk16 = (ri >> 4) == (ci >> 4)
    blk32 = (ri >> 5) == (ci >> 5)
    l2 = lax.broadcasted_iota(jnp.int32, (c, 2 * LANES), 0)
    j2 = lax.broadcasted_iota(jnp.int32, (c, 2 * LANES), 1)
    ux = jnp.where((j2 >= c) | (l2 > j2), 1.0, 0.0).astype(F32)
    nw = nw_ref[...]

    def chunk_head(r0, j):
        hs = slice(j * LANES, (j + 1) * LANES)
        q = qn[pl.ds(r0, c), hs]
        k = kn[pl.ds(r0, c), hs]
        v = vn[pl.ds(r0, c), hs]
        gb = g_ref[0, pl.ds(r0, c), hs]
        bb = b_ref[0, pl.ds(r0, c), hs]
        d = _dot(t_inc, jnp.concatenate([gb, gb], axis=1) * ux, HI)
        gc = d[:, LANES:]
        decay = jnp.where(tril, jnp.exp(d[:, :c]), 0.0)
        kb = k * bb
        m = jnp.where(strict, _dot_nt(kb, k, HI) * decay, 0.0)
        a_inv = _unit_lower_inverse(m, (eye, blk16, blk32))
        egc = jnp.exp(gc)
        sol = _dot(a_inv, jnp.concatenate([v * bb, kb * egc], axis=1), HI)
        u = sol[:, :LANES]
        w = sol[:, LANES:]
        qk = _dot_nt(q.astype(BF16), k.astype(BF16)) * decay
        qg = q * egc
        gl = gc[c - 1:c, :]
        kd = k * jnp.exp(gl - gc)
        state = st[j]
        state_b = state.astype(BF16)
        v_new = u - _dot(w.astype(BF16), state_b)
        v_new_b = v_new.astype(BF16)
        o = _dot(qg.astype(BF16), state_b) + _dot(qk.astype(BF16), v_new_b)
        st[j] = state * jnp.exp(gl) + _dot_tn(kd.astype(BF16), v_new_b)
        o = o * lax.rsqrt(jnp.mean(o * o, axis=-1, keepdims=True) + 1e-6) * nw
        z = z_ref[0, pl.ds(r0, c), hs]
        o_ref[0, pl.ds(r0, c), hs] = o * (z * jax.nn.sigmoid(z))

    def chunk(n, carry):
        r0 = pl.multiple_of(n * c, c)
        for j in range(nh):
            chunk_head(r0, j)
        return carry

    lax.fori_loop(0, s // c, chunk, 0)


def _gdn(proj3, gexp, bexp, conv_w, norm_w, cols):
    b, s, _ = proj3.shape
    nh = 2
    wd = nh * LANES
    cq, ck, cv, cz = (c0 // nh for c0 in cols)

    def blk(c0):
        return pl.BlockSpec((1, s, wd), lambda i, h: (i, 0, c0 + h))

    def cw(c0):
        return pl.BlockSpec((conv_w.shape[0], wd), lambda i, h: (0, c0 + h))

    head = pl.BlockSpec((1, s, wd), lambda i, h: (i, 0, h))
    return pl.pallas_call(
        _gdn_kernel,
        grid=(b, GDN_HEADS // nh),
        in_specs=[blk(cq), blk(ck), blk(cv), blk(cz), head, head,
                  cw(0), cw(GDN_HEADS // nh), cw(2 * GDN_HEADS // nh),
                  pl.BlockSpec((1, LANES), lambda i, h: (0, 0))],
        out_specs=head,
        out_shape=jax.ShapeDtypeStruct((b, s, GDN_HEADS * LANES), F32),
        scratch_shapes=[pltpu.VMEM((s, wd), F32), pltpu.VMEM((s, wd), F32),
                        pltpu.VMEM((s, wd), F32), pltpu.VMEM((nh, GDN_DK, LANES), F32)],
        compiler_params=pltpu.CompilerParams(
            dimension_semantics=("parallel", "parallel"), vmem_limit_bytes=VMEM_LIMIT),
        name="gdn",
    )(proj3, proj3, proj3, proj3, gexp, bexp, conv_w, conv_w, conv_w, norm_w)


def _fox_kernel(q_ref, k_ref, v_ref, c_ref, ct_ref, o_ref):
    tq = q_ref.shape[1]
    hp = pl.program_id(1)
    qi = pl.program_id(2)
    q = q_ref[0]
    cblk = c_ref[0]
    lane = lax.broadcasted_iota(jnp.int32, (tq, LANES), 1)
    qpos = qi * tq + lax.broadcasted_iota(jnp.int32, (tq, tq), 0)
    kofs = lax.broadcasted_iota(jnp.int32, (tq, tq), 1)
    nj = LANES // FX_DH
    ccol, qh = [], []
    for j in range(nj):
        hh = hp * nj + j
        ccol.append(jnp.sum(jnp.where(lane == 8 + hh, cblk, 0.0), axis=-1, keepdims=True))
        qh.append((q[:, j * FX_DH:(j + 1) * FX_DH] * (FX_DH ** -0.5)).astype(BF16))

    def body(kv, carry):
        k0 = pl.multiple_of(kv * tq, tq)
        causal = qpos >= k0 + kofs
        kblk = k_ref[0, pl.ds(k0, tq), :].astype(BF16)
        vblk = v_ref[0, pl.ds(k0, tq), :].astype(BF16)
        new = []
        for j in range(nj):
            m, l, acc = carry[j]
            hs = slice(j * FX_DH, (j + 1) * FX_DH)
            sc = _dot_nt(qh[j], kblk[:, hs]) + ccol[j] - ct_ref[0, pl.ds(hp * nj + j, 1), pl.ds(k0, tq)]
            sc = jnp.where(causal, sc, -1e30)
            m_new = jnp.maximum(m, jnp.max(sc, axis=-1, keepdims=True))
            a = jnp.exp(m - m_new)
            p = jnp.exp(sc - m_new)
            l = a * l + jnp.sum(p, axis=-1, keepdims=True)
            acc = a * acc + _dot(p.astype(BF16), vblk[:, hs])
            new.append((m_new, l, acc))
        return tuple(new)

    init = tuple((jnp.full((tq, 1), -1e30, F32), jnp.zeros((tq, 1), F32), jnp.zeros((tq, FX_DH), F32))
                 for _ in range(nj))
    res = lax.fori_loop(0, qi + 1, body, init)
    o_ref[0] = jnp.concatenate([acc / l for _, l, acc in res], axis=1)


def _fox(proj3, c, ct, cols):
    b, s, _ = proj3.shape
    cq, ck, cv = cols
    tq = min(256, s)
    npair = FX_HEADS * FX_DH // LANES
    return pl.pallas_call(
        _fox_kernel,
        grid=(b, npair, s // tq),
        in_specs=[pl.BlockSpec((1, tq, LANES), lambda i, h, t: (i, t, cq + h)),
                  pl.BlockSpec((1, s, LANES), lambda i, h, t: (i, 0, ck + h)),
                  pl.BlockSpec((1, s, LANES), lambda i, h, t: (i, 0, cv + h)),
                  pl.BlockSpec((1, tq, LANES), lambda i, h, t: (i, t, 0)),
                  pl.BlockSpec((1, 8, s), lambda i, h, t: (i, 0, 0))],
        out_specs=pl.BlockSpec((1, tq, LANES), lambda i, h, t: (i, t, h)),
        out_shape=jax.ShapeDtypeStruct((b, s, npair * LANES), F32),
        compiler_params=pltpu.CompilerParams(
            dimension_semantics=("parallel", "parallel", "parallel"), vmem_limit_bytes=VMEM_LIMIT),
        name="fox",
    )(proj3, proj3, proj3, c, ct)


def _mix_kernel(oa_ref, ob_ref, ga_ref, gb_ref, x_ref, wa_ref, wb_ref, wo_ref, g1_ref, b1_ref, o_ref, *, alpha):
    ya = _dot(oa_ref[...].astype(BF16), wa_ref[...])
    yb = _dot(ob_ref[...].astype(BF16), wb_ref[...])
    mix = jax.nn.sigmoid(ga_ref[...]) * ya + jax.nn.sigmoid(gb_ref[...]) * yb
    z = alpha * x_ref[...] + _dot(mix.astype(BF16), wo_ref[...])
    o_ref[...] = _layernorm(z, g1_ref[...], b1_ref[...])


def _mix(oa, ob, proj, x2, wa, wb, wo, g1, b1, alpha):
    t, d = x2.shape
    tm = min(512, t)
    w = oa.shape[1]
    full = lambda r, c: pl.BlockSpec((r, c), lambda i: (0, 0))
    return pl.pallas_call(
        functools.partial(_mix_kernel, alpha=alpha),
        grid=(t // tm,),
        in_specs=[pl.BlockSpec((tm, w), lambda i: (i, 0)),
                  pl.BlockSpec((tm, w), lambda i: (i, 0)),
                  pl.BlockSpec((tm, d), lambda i: (i, 0)),
                  pl.BlockSpec((tm, d), lambda i: (i, 1)),
                  pl.BlockSpec((tm, d), lambda i: (i, 0)),
                  full(w, d), full(w, d), full(d, d), full(1, d), full(1, d)],
        out_specs=pl.BlockSpec((tm, d), lambda i: (i, 0)),
        out_shape=jax.ShapeDtypeStruct((t, d), F32),
        compiler_params=pltpu.CompilerParams(
            dimension_semantics=("parallel",), vmem_limit_bytes=VMEM_LIMIT),
        name="mix",
    )(oa, ob, proj, proj, x2, wa, wb, wo, g1, b1)


def _route_kernel(h_ref, wq_ref, keys_ref, e_ref, g_ref, q_scr, s_scr, i_scr, cand_scr, cidx_scr, best_scr, eh_scr, eall_scr):
    tm = h_ref.shape[0]
    nk = PEER_NKEYS
    kk = PEER_TOPK
    q = _dot(h_ref[...].astype(BF16), wq_ref[...])
    for j in range(2 * PEER_HEADS):
        q_scr[j] = q[:, j * LANES:(j + 1) * LANES].astype(BF16)
    iota_k = lax.broadcasted_iota(jnp.int32, (nk, tm), 0)
    iota_c = lax.broadcasted_iota(jnp.int32, (kk * kk, tm), 0)
    neg = jnp.float32(-jnp.inf)

    def head(hd, carry):
        for p in range(2):
            vals = _dot_nt(keys_ref[hd * 2 + p], q_scr[hd * 2 + p])
            for r in range(kk):
                m = jnp.max(vals, axis=0, keepdims=True)
                am = jnp.min(jnp.where(vals == m, iota_k, nk), axis=0, keepdims=True)
                s_scr[p * kk + r:p * kk + r + 1, :] = m
                i_scr[p * kk + r:p * kk + r + 1, :] = am
                vals = jnp.where(iota_k == am, neg, vals)
        s1 = s_scr[kk:2 * kk, :]
        i1 = i_scr[kk:2 * kk, :]
        for a in range(kk):
            cand_scr[a * kk:(a + 1) * kk, :] = s_scr[a:a + 1, :] + s1
            cidx_scr[a * kk:(a + 1) * kk, :] = i_scr[a:a + 1, :] * nk + i1
        vals = cand_scr[...]
        cidx = cidx_scr[...]
        for r in range(kk):
            m = jnp.max(vals, axis=0, keepdims=True)
            am = jnp.min(jnp.where(vals == m, iota_c, kk * kk), axis=0, keepdims=True)
            sel = iota_c == am
            best_scr[r:r + 1, :] = m
            eh_scr[r:r + 1, :] = jnp.max(jnp.where(sel, cidx, -1), axis=0, keepdims=True)
            vals = jnp.where(sel, neg, vals)
        bs = best_scr[...]
        ex = jnp.exp(bs - bs[0:1, :])
        r0 = pl.multiple_of(hd * kk, kk)
        g_ref[pl.ds(r0, kk), :] = ex / jnp.sum(ex, axis=0, keepdims=True)
        eall_scr[pl.ds(r0, kk), :] = eh_scr[...]
        return carry

    lax.fori_loop(0, PEER_HEADS, head, 0)
    e_ref[...] = eall_scr[...].T


def _route(h1, wq, keys):
    t, d = h1.shape
    tm = min(256, t)
    nq = wq.shape[1]
    hk = PEER_HEADS * PEER_TOPK
    kk = PEER_TOPK
    return pl.pallas_call(
        _route_kernel,
        grid=(t // tm,),
        in_specs=[pl.BlockSpec((tm, d), lambda i: (i, 0)),
                  pl.BlockSpec((d, nq), lambda i: (0, 0)),
                  pl.BlockSpec(keys.shape, lambda i: (0, 0, 0))],
        out_specs=[pl.BlockSpec((tm, hk), lambda i: (i, 0)),
                   pl.BlockSpec((hk, tm), lambda i: (0, i))],
        out_shape=[jax.ShapeDtypeStruct((t, hk), jnp.int32),
                   jax.ShapeDtypeStruct((hk, t), F32)],
        scratch_shapes=[pltpu.VMEM((2 * PEER_HEADS, tm, LANES), BF16),
                        pltpu.VMEM((2 * kk, tm), F32), pltpu.VMEM((2 * kk, tm), jnp.int32),
                        pltpu.VMEM((kk * kk, tm), F32), pltpu.VMEM((kk * kk, tm), jnp.int32),
                        pltpu.VMEM((kk, tm), F32), pltpu.VMEM((kk, tm), jnp.int32),
                        pltpu.VMEM((hk, tm), jnp.int32)],
        compiler_params=pltpu.CompilerParams(
            dimension_semantics=("parallel",), vmem_limit_bytes=VMEM_LIMIT),
        name="route",
    )(h1, wq, keys)


def _peer_kernel(ids_ref, idn_ref, h_ref, gt_ref, uv_hbm, g2_ref, b2_ref, o_ref,
                 buf_a, buf_b, sem, y_scr, *, tt, alpha):
    i = pl.program_id(0)
    n = pl.num_programs(0)
    hk = ids_ref.shape[1]
    d = h_ref.shape[1]
    bufs = (buf_a, buf_b)

    def issue(idref, row0, slot):
        for t in range(tt):
            for k in range(hk):
                e = idref[row0 + t, k]
                pltpu.make_async_copy(uv_hbm.at[e], bufs[slot].at[pl.ds(t * hk + k, 1)],
                                      sem.at[slot]).start(priority=k % 2)

    def wait(slot):
        pltpu.make_async_copy(bufs[slot], bufs[slot], sem.at[slot]).wait()

    gt = gt_ref[...]
    lane = lax.broadcasted_iota(jnp.int32, gt.shape, 1)
    tok0 = (i * 2 * tt) % LANES

    def compute(slot):
        buf = bufs[slot]
        for t in range(tt):
            row = slot * tt + t
            prod = buf[t * hk:(t + 1) * hk, 0:d] * h_ref[row:row + 1, :]
            part = prod[:, 0:LANES]
            for c in range(1, d // LANES):
                part = part + prod[:, c * LANES:(c + 1) * LANES]
            pre = jnp.sum(part, axis=-1, keepdims=True)
            gate = jnp.sum(jnp.where(lane == tok0 + row, gt, 0.0), axis=-1, keepdims=True)
            act = 0.5 * pre * (1.0 + lax.erf(pre * (2.0 ** -0.5))) * gate
            y_scr[row:row + 1, :] = jnp.sum(act * buf[t * hk:(t + 1) * hk, d:2 * d], axis=0, keepdims=True)

    @pl.when(i == 0)
    def _():
        issue(ids_ref, 0, 0)

    wait(0)
    issue(ids_ref, tt, 1)
    compute(0)
    wait(1)
    issue(idn_ref, 0, 0)
    compute(1)

    @pl.when(i == n - 1)
    def _():
        wait(0)

    z = alpha * h_ref[...] + y_scr[...]
    o_ref[...] = _layernorm(z, g2_ref[...], b2_ref[...])


def _peer(ids, gt, h1, uv_tab, g2, b2, alpha):
    t, d = h1.shape
    hk = ids.shape[1]
    tt = 8
    nsteps = t // (2 * tt)
    steps_per_lane_blk = LANES // (2 * tt)
    return pl.pallas_call(
        functools.partial(_peer_kernel, tt=tt, alpha=alpha),
        grid=(nsteps,),
        in_specs=[pl.BlockSpec((2 * tt, hk), lambda i: (i, 0), memory_space=pltpu.SMEM),
                  pl.BlockSpec((2 * tt, hk), lambda i: (jnp.minimum(i + 1, nsteps - 1), 0), memory_space=pltpu.SMEM),
                  pl.BlockSpec((2 * tt, d), lambda i: (i, 0)),
                  pl.BlockSpec((hk, LANES), lambda i: (0, i // steps_per_lane_blk)),
                  pl.BlockSpec(memory_space=pl.ANY),
                  pl.BlockSpec((1, d), lambda i: (0, 0)),
                  pl.BlockSpec((1, d), lambda i: (0, 0))],
        out_specs=pl.BlockSpec((2 * tt, d), lambda i: (i, 0)),
        out_shape=jax.ShapeDtypeStruct((t, d), F32),
        scratch_shapes=[pltpu.VMEM((tt * hk, 2 * d), F32), pltpu.VMEM((tt * hk, 2 * d), F32),
                        pltpu.SemaphoreType.DMA((2,)),
                        pltpu.VMEM((2 * tt, d), F32)],
        compiler_params=pltpu.CompilerParams(
            dimension_semantics=("arbitrary",), vmem_limit_bytes=VMEM_LIMIT),
        name="peer",
    )(ids, ids, h1, gt, uv_tab, g2, b2)


def _layer(h, w_in, conv_w, a_log, dt_bias, norm_w, f_bias, w_out_gdn, w_out_fox, w_o, ln1_g, ln1_b,
           peer_wq, peer_keys, peer_u, peer_v, ln2_g, ln2_b, alpha):
    b, s, d = h.shape
    t = b * s
    qk = GDN_HEADS * GDN_DK
    fw = FX_HEADS * FX_DH
    o_gz = 4 * qk
    o_ga = o_gz
    o_fq = o_ga + 2 * GDN_HEADS
    o_ff = o_fq + 3 * fw
    o_gate = o_ff + FX_HEADS
    w_big = jnp.concatenate([w_in[:, o_gate:], w_in[:, :o_gz], w_in[:, o_fq:o_ff]], axis=1).astype(BF16)
    n_small = 2 * GDN_HEADS + FX_HEADS
    w_small = jnp.concatenate([w_in[:, o_ga:o_fq], w_in[:, o_ff:o_gate],
                               jnp.zeros((d, LANES - n_small), F32)], axis=1)
    params = jnp.zeros((8, LANES), F32)
    params = params.at[0, :GDN_HEADS].set(a_log).at[1, :GDN_HEADS].set(dt_bias)
    params = params.at[2, 2 * GDN_HEADS:n_small].set(f_bias)

    x2 = h.reshape(t, d)
    proj = _in_proj(x2, w_big)
    gexp, bexp, c, ct = _prep(h, w_small, params)
    proj3 = proj.reshape(b, s, proj.shape[1])
    gdn0 = 2 * d // LANES
    oa = _gdn(proj3, gexp, bexp, conv_w, norm_w.reshape(1, LANES),
              (gdn0, gdn0 + GDN_HEADS, gdn0 + 2 * GDN_HEADS, gdn0 + 3 * GDN_HEADS))
    fox0 = gdn0 + 4 * GDN_HEADS
    npair = fw // LANES
    ob = _fox(proj3, c, ct, (fox0, fox0 + npair, fox0 + 2 * npair))
    h1 = _mix(oa.reshape(t, qk), ob.reshape(t, fw), proj, x2,
              w_out_gdn.astype(BF16), w_out_fox.astype(BF16), w_o.astype(BF16),
              ln1_g.reshape(1, d), ln1_b.reshape(1, d), alpha)
    keys = peer_keys.reshape(2 * PEER_HEADS, PEER_NKEYS, peer_keys.shape[-1]).astype(BF16)
    ids, gt = _route(h1, peer_wq.astype(BF16), keys)
    uv_tab = jnp.concatenate([peer_u, peer_v], axis=1).reshape(peer_u.shape[0], 1, 2 * d)
    out = _peer(ids, gt, h1, uv_tab, ln2_g.reshape(1, d), ln2_b.reshape(1, d), alpha)
    return out.reshape(b, s, d)


def kernel(x, w_in, gdn_conv_w, gdn_a_log, gdn_dt_bias, gdn_norm_w, fox_f_bias, w_out_gdn, w_out_fox, w_o,
           ln1_g, ln1_b, peer_wq, peer_keys, peer_u, peer_v, ln2_g, ln2_b):
    depth = w_in.shape[0]
    alpha = (2.0 * depth) ** 0.25
    h = x
    for l in range(depth):
        h = _layer(h, w_in[l], gdn_conv_w[l], gdn_a_log[l], gdn_dt_bias[l], gdn_norm_w[l], fox_f_bias[l],
                   w_out_gdn[l], w_out_fox[l], w_o[l], ln1_g[l], ln1_b[l], peer_wq[l], peer_keys[l],
                   peer_u[l], peer_v[l], ln2_g[l], ln2_b[l], alpha)
    return h
```

```python
import functools

import jax
import jax.numpy as jnp
from jax import lax
from jax.experimental import pallas as pl
from jax.experimental.pallas import tpu as pltpu

F32 = jnp.float32
BF16 = jnp.bfloat16
HI = lax.Precision.HIGHEST

LANES = 128
CHUNK = 64
GDN_HEADS = 4
GDN_DK = 128
FX_HEADS = 8
FX_DH = 64
PEER_HEADS = 8
PEER_NKEYS = 128
PEER_TOPK = 16
PEER_SUB = 4
LN_EPS = 1e-5
VMEM_LIMIT = 48 * 1024 * 1024


def _dot(a, b, prec=None):
    return jnp.dot(a, b, preferred_element_type=F32, precision=prec)


def _dot_nt(a, b, prec=None):
    return lax.dot_general(a, b, (((1,), (1,)), ((), ())), preferred_element_type=F32, precision=prec)


def _dot_tn(a, b, prec=None):
    return lax.dot_general(a, b, (((0,), (0,)), ((), ())), preferred_element_type=F32, precision=prec)


def _softplus(x):
    return jnp.maximum(x, 0.0) + jnp.log1p(jnp.exp(-jnp.abs(x)))


def _layernorm(z, g, b):
    mu = jnp.mean(z, axis=-1, keepdims=True)
    zc = z - mu
    var = jnp.mean(zc * zc, axis=-1, keepdims=True)
    return zc * lax.rsqrt(var + LN_EPS) * g + b


def _mm_kernel(x_ref, w_ref, o_ref):
    o_ref[...] = _dot(x_ref[...].astype(BF16), w_ref[...])


def _in_proj(x2, w_big):
    t, d = x2.shape
    n = w_big.shape[1]
    tm = min(512, t)
    tn = 512
    return pl.pallas_call(
        _mm_kernel,
        grid=(t // tm, n // tn),
        in_specs=[pl.BlockSpec((tm, d), lambda i, j: (i, 0)),
                  pl.BlockSpec((d, tn), lambda i, j: (0, j))],
        out_specs=pl.BlockSpec((tm, tn), lambda i, j: (i, j)),
        out_shape=jax.ShapeDtypeStruct((t, n), F32),
        compiler_params=pltpu.CompilerParams(
            dimension_semantics=("parallel", "parallel"), vmem_limit_bytes=VMEM_LIMIT),
        name="in_proj",
    )(x2, w_big)


def _prep_kernel(x_ref, w_ref, par_ref, gexp_ref, bexp_ref, c_ref, ct_ref, carry_ref):
    ts = x_ref.shape[1]

    @pl.when(pl.program_id(1) == 0)
    def _():
        carry_ref[...] = jnp.zeros_like(carry_ref)

    small = _dot(x_ref[0], w_ref[...], HI)
    a_log = par_ref[0:1, :]
    dt_bias = par_ref[1:2, :]
    f_bias = par_ref[2:3, :]
    g = -jnp.exp(a_log) * _softplus(small + dt_bias)
    beta = jax.nn.sigmoid(small)
    lane = lax.broadcasted_iota(jnp.int32, (ts, LANES), 1)
    log_f = jnp.where((lane >= 8) & (lane < 16), -_softplus(-(small + f_bias)), 0.0)
    row = lax.broadcasted_iota(jnp.int32, (ts, ts), 0)
    col = lax.broadcasted_iota(jnp.int32, (ts, ts), 1)
    tril = (row >= col).astype(F32)
    c = _dot(tril, log_f, HI) + carry_ref[...]
    carry_ref[...] = c[ts - 1:ts, :]
    c_ref[0] = c
    ct_ref[0] = c.T[8:16, :]
    gexp_ref[0] = jnp.concatenate(
        [jnp.broadcast_to(g[:, h:h + 1], (ts, LANES)) for h in range(GDN_HEADS)], axis=1)
    bexp_ref[0] = jnp.concatenate(
        [jnp.broadcast_to(beta[:, GDN_HEADS + h:GDN_HEADS + h + 1], (ts, LANES)) for h in range(GDN_HEADS)], axis=1)


def _prep(x, w_small, params):
    b, s, d = x.shape
    ts = min(512, s)
    hw = GDN_HEADS * LANES
    return pl.pallas_call(
        _prep_kernel,
        grid=(b, s // ts),
        in_specs=[pl.BlockSpec((1, ts, d), lambda i, j: (i, j, 0)),
                  pl.BlockSpec((d, LANES), lambda i, j: (0, 0)),
                  pl.BlockSpec((8, LANES), lambda i, j: (0, 0))],
        out_specs=[pl.BlockSpec((1, ts, hw), lambda i, j: (i, j, 0)),
                   pl.BlockSpec((1, ts, hw), lambda i, j: (i, j, 0)),
                   pl.BlockSpec((1, ts, LANES), lambda i, j: (i, j, 0)),
                   pl.BlockSpec((1, 8, ts), lambda i, j: (i, 0, j))],
        out_shape=[jax.ShapeDtypeStruct((b, s, hw), F32),
                   jax.ShapeDtypeStruct((b, s, hw), F32),
                   jax.ShapeDtypeStruct((b, s, LANES), F32),
                   jax.ShapeDtypeStruct((b, 8, s), F32)],
        scratch_shapes=[pltpu.VMEM((1, LANES), F32)],
        compiler_params=pltpu.CompilerParams(
            dimension_semantics=("parallel", "arbitrary"), vmem_limit_bytes=VMEM_LIMIT),
        name="prep",
    )(x, w_small, params)


def _each(fn, *lists):
    return [fn(*args) for args in zip(*lists)]


def _unit_lower_inverse(ms, masks):
    eye, blk16, blk32 = masks
    hi = lambda a, b: _dot(a, b, HI)
    n1 = _each(lambda m: -jnp.where(blk16, m, 0.0), ms)
    l1 = _each(lambda m: jnp.where(blk32 & jnp.logical_not(blk16), m, 0.0), ms)
    l2 = _each(lambda m: jnp.where(blk32, 0.0, m), ms)
    n2 = _each(hi, n1, n1)
    p = _each(lambda a, b: hi(eye + a, eye + b), n1, n2)
    n4 = _each(hi, n2, n2)
    p = _each(lambda a, b: hi(a, eye + b), p, n4)
    n8 = _each(hi, n4, n4)
    d_inv = _each(lambda a, b: hi(a, eye + b), p, n8)
    dl = _each(hi, d_inv, l1)
    a32 = _each(lambda a, b: a - hi(b, a), d_inv, dl)
    al = _each(hi, a32, l2)
    return _each(lambda a, b: a - hi(b, a), a32, al)


def _gdn_kernel(q_ref, k_ref, v_ref, z_ref, g_ref, b_ref, cwq_ref, cwk_ref, cwv_ref, nw_ref,
                o_ref, qn, kn, vn, st):
    s = q_ref.shape[1]
    c = CHUNK
    nh = q_ref.shape[2] // LANES
    row = lax.broadcasted_iota(jnp.int32, (s, LANES), 0)

    def conv_silu(x, w):
        y = x * w[3:4, :]
        for sh in (1, 2, 3):
            xs = jnp.where(row >= sh, pltpu.roll(x, sh, axis=0), 0.0)
            y = y + xs * w[3 - sh:4 - sh, :]
        return y * jax.nn.sigmoid(y)

    def l2norm(x):
        return x * lax.rsqrt(jnp.sum(x * x, axis=-1, keepdims=True) + 1e-6)

    for j in range(nh):
        hs = slice(j * LANES, (j + 1) * LANES)
        qn[:, hs] = l2norm(conv_silu(q_ref[0, :, hs], cwq_ref[:, hs])) * (GDN_DK ** -0.5)
        kn[:, hs] = l2norm(conv_silu(k_ref[0, :, hs], cwk_ref[:, hs]))
        vn[:, hs] = conv_silu(v_ref[0, :, hs], cwv_ref[:, hs])
    st[...] = jnp.zeros_like(st)

    ri = lax.broadcasted_iota(jnp.int32, (c, c), 0)
    ci = lax.broadcasted_iota(jnp.int32, (c, c), 1)
    tril = ri >= ci
    strict = ri > ci
    t_inc = tril.astype(F32)
    eye = (ri == ci).astype(F32)
    blk16 = (ri >> 4) == (ci >> 4)
    blk32 = (ri >> 5) == (ci >> 5)
    l2 = lax.broadcasted_iota(jnp.int32, (c, 2 * LANES), 0)
    j2 = lax.broadcasted_iota(jnp.int32, (c, 2 * LANES), 1)
    ux = jnp.where((j2 >= c) | (l2 > j2), 1.0, 0.0).astype(F32)
    nw = nw_ref[...]

    heads = list(range(nh))
    lanes_of = [slice(j * LANES, (j + 1) * LANES) for j in heads]

    def chunk(n, carry):
        r0 = pl.multiple_of(n * c, c)
        rows = pl.ds(r0, c)
        q = [qn[rows, hs] for hs in lanes_of]
        k = [kn[rows, hs] for hs in lanes_of]
        v = [vn[rows, hs] for hs in lanes_of]
        gb = [g_ref[0, rows, hs] for hs in lanes_of]
        bb = [b_ref[0, rows, hs] for hs in lanes_of]
        d = _each(lambda g: _dot(t_inc, jnp.concatenate([g, g], axis=1) * ux, HI), gb)
        kb = _each(lambda a, b: a * b, k, bb)
        kk = _each(lambda a, b: _dot_nt(a, b, HI), kb, k)
        qk = _each(lambda a, b: _dot_nt(a.astype(BF16), b.astype(BF16)), q, k)
        gc = [x[:, LANES:] for x in d]
        decay = [jnp.where(tril, jnp.exp(x[:, :c]), 0.0) for x in d]
        m = _each(lambda a, b: jnp.where(strict, a * b, 0.0), kk, decay)
        a_inv = _unit_lower_inverse(m, (eye, blk16, blk32))
        egc = _each(jnp.exp, gc)
        rhs = _each(lambda vv, b, kbb, e: jnp.concatenate([vv * b, kbb * e], axis=1), v, bb, kb, egc)
        sol = _each(lambda a, r: _dot(a, r, HI), a_inv, rhs)
        qk = _each(lambda a, b: (a * b).astype(BF16), qk, decay)
        qg = _each(lambda a, e: (a * e).astype(BF16), q, egc)
        gl = [x[c - 1:c, :] for x in gc]
        kd = _each(lambda kk_, g1, g: (kk_ * jnp.exp(g1 - g)).astype(BF16), k, gl, gc)
        state = [st[j] for j in heads]
        state_b = [x.astype(BF16) for x in state]
        v_new = _each(lambda sl, sb: sl[:, :LANES] - _dot(sl[:, LANES:].astype(BF16), sb), sol, state_b)
        v_new_b = [x.astype(BF16) for x in v_new]
        o = _each(lambda a, sb, b, vb: _dot(a, sb) + _dot(b, vb), qg, state_b, qk, v_new_b)
        new_state = _each(lambda x, g1, a, vb: x * jnp.exp(g1) + _dot_tn(a, vb), state, gl, kd, v_new_b)
        for j in heads:
            st[j] = new_state[j]
            on = o[j] * lax.rsqrt(jnp.mean(o[j] * o[j], axis=-1, keepdims=True) + 1e-6) * nw
            z = z_ref[0, rows, lanes_of[j]]
            o_ref[0, rows, lanes_of[j]] = on * (z * jax.nn.sigmoid(z))
        return carry

    lax.fori_loop(0, s // c, chunk, 0)


def _gdn(proj3, gexp, bexp, conv_w, norm_w, cols):
    b, s, _ = proj3.shape
    nh = 2
    wd = nh * LANES
    cq, ck, cv, cz = (c0 // nh for c0 in cols)

    def blk(c0):
        return pl.BlockSpec((1, s, wd), lambda i, h: (i, 0, c0 + h))

    def cw(c0):
        return pl.BlockSpec((conv_w.shape[0], wd), lambda i, h: (0, c0 + h))

    head = pl.BlockSpec((1, s, wd), lambda i, h: (i, 0, h))
    return pl.pallas_call(
        _gdn_kernel,
        grid=(b, GDN_HEADS // nh),
        in_specs=[blk(cq), blk(ck), blk(cv), blk(cz), head, head,
                  cw(0), cw(GDN_HEADS // nh), cw(2 * GDN_HEADS // nh),
                  pl.BlockSpec((1, LANES), lambda i, h: (0, 0))],
        out_specs=head,
        out_shape=jax.ShapeDtypeStruct((b, s, GDN_HEADS * LANES), F32),
        scratch_shapes=[pltpu.VMEM((s, wd), F32), pltpu.VMEM((s, wd), F32),
                        pltpu.VMEM((s, wd), F32), pltpu.VMEM((nh, GDN_DK, LANES), F32)],
        compiler_params=pltpu.CompilerParams(
            dimension_semantics=("parallel", "parallel"), vmem_limit_bytes=VMEM_LIMIT),
        name="gdn",
    )(proj3, proj3, proj3, proj3, gexp, bexp, conv_w, conv_w, conv_w, norm_w)


def _fox_kernel(q_ref, k_ref, v_ref, c_ref, ct_ref, o_ref):
    tq = q_ref.shape[1]
    hp = pl.program_id(1)
    qi = pl.program_id(2)
    q = q_ref[0]
    cblk = c_ref[0]
    lane = lax.broadcasted_iota(jnp.int32, (tq, LANES), 1)
    qpos = qi * tq + lax.broadcasted_iota(jnp.int32, (tq, tq), 0)
    kofs = lax.broadcasted_iota(jnp.int32, (tq, tq), 1)
    nj = LANES // FX_DH
    ccol, qh = [], []
    for j in range(nj):
        hh = hp * nj + j
        ccol.append(jnp.sum(jnp.where(lane == 8 + hh, cblk, 0.0), axis=-1, keepdims=True))
        qh.append((q[:, j * FX_DH:(j + 1) * FX_DH] * (FX_DH ** -0.5)).astype(BF16))

    def body(kv, carry):
        k0 = pl.multiple_of(kv * tq, tq)
        causal = qpos >= k0 + kofs
        kblk = k_ref[0, pl.ds(k0, tq), :].astype(BF16)
        vblk = v_ref[0, pl.ds(k0, tq), :].astype(BF16)
        new = []
        for j in range(nj):
            m, l, acc = carry[j]
            hs = slice(j * FX_DH, (j + 1) * FX_DH)
            sc = _dot_nt(qh[j], kblk[:, hs]) + ccol[j] - ct_ref[0, pl.ds(hp * nj + j, 1), pl.ds(k0, tq)]
            sc = jnp.where(causal, sc, -1e30)
            m_new = jnp.maximum(m, jnp.max(sc, axis=-1, keepdims=True))
            a = jnp.exp(m - m_new)
            p = jnp.exp(sc - m_new)
            l = a * l + jnp.sum(p, axis=-1, keepdims=True)
            acc = a * acc + _dot(p.astype(BF16), vblk[:, hs])
            new.append((m_new, l, acc))
        return tuple(new)

    init = tuple((jnp.full((tq, 1), -1e30, F32), jnp.zeros((tq, 1), F32), jnp.zeros((tq, FX_DH), F32))
                 for _ in range(nj))
    res = lax.fori_loop(0, qi + 1, body, init)
    o_ref[0] = jnp.concatenate([acc / l for _, l, acc in res], axis=1)


def _fox(proj3, c, ct, cols):
    b, s, _ = proj3.shape
    cq, ck, cv = cols
    tq = min(256, s)
    npair = FX_HEADS * FX_DH // LANES
    return pl.pallas_call(
        _fox_kernel,
        grid=(b, npair, s // tq),
        in_specs=[pl.BlockSpec((1, tq, LANES), lambda i, h, t: (i, t, cq + h)),
                  pl.BlockSpec((1, s, LANES), lambda i, h, t: (i, 0, ck + h)),
                  pl.BlockSpec((1, s, LANES), lambda i, h, t: (i, 0, cv + h)),
                  pl.BlockSpec((1, tq, LANES), lambda i, h, t: (i, t, 0)),
                  pl.BlockSpec((1, 8, s), lambda i, h, t: (i, 0, 0))],
        out_specs=pl.BlockSpec((1, tq, LANES), lambda i, h, t: (i, t, h)),
        out_shape=jax.ShapeDtypeStruct((b, s, npair * LANES), F32),
        compiler_params=pltpu.CompilerParams(
            dimension_semantics=("parallel", "parallel", "parallel"), vmem_limit_bytes=VMEM_LIMIT),
        name="fox",
    )(proj3, proj3, proj3, c, ct)


def _mix_kernel(oa_ref, ob_ref, ga_ref, gb_ref, x_ref, wa_ref, wb_ref, wo_ref, g1_ref, b1_ref, o_ref, *, alpha):
    ya = _dot(oa_ref[...].astype(BF16), wa_ref[...])
    yb = _dot(ob_ref[...].astype(BF16), wb_ref[...])
    mix = jax.nn.sigmoid(ga_ref[...]) * ya + jax.nn.sigmoid(gb_ref[...]) * yb
    z = alpha * x_ref[...] + _dot(mix.astype(BF16), wo_ref[...])
    o_ref[...] = _layernorm(z, g1_ref[...], b1_ref[...])


def _mix(oa, ob, proj, x2, wa, wb, wo, g1, b1, alpha):
    t, d = x2.shape
    tm = min(512, t)
    w = oa.shape[1]
    full = lambda r, c: pl.BlockSpec((r, c), lambda i: (0, 0))
    return pl.pallas_call(
        functools.partial(_mix_kernel, alpha=alpha),
        grid=(t // tm,),
        in_specs=[pl.BlockSpec((tm, w), lambda i: (i, 0)),
                  pl.BlockSpec((tm, w), lambda i: (i, 0)),
                  pl.BlockSpec((tm, d), lambda i: (i, 0)),
                  pl.BlockSpec((tm, d), lambda i: (i, 1)),
                  pl.BlockSpec((tm, d), lambda i: (i, 0)),
                  full(w, d), full(w, d), full(d, d), full(1, d), full(1, d)],
        out_specs=pl.BlockSpec((tm, d), lambda i: (i, 0)),
        out_shape=jax.ShapeDtypeStruct((t, d), F32),
        compiler_params=pltpu.CompilerParams(
            dimension_semantics=("parallel",), vmem_limit_bytes=VMEM_LIMIT),
        name="mix",
    )(oa, ob, proj, proj, x2, wa, wb, wo, g1, b1)


def _route_kernel(h_ref, wq_ref, keys_ref, e_ref, g_ref, q_scr, s_scr, i_scr, cand_scr, cidx_scr, best_scr, eh_scr, eall_scr):
    tm = h_ref.shape[0]
    nk = PEER_NKEYS
    kk = PEER_TOPK
    q = _dot(h_ref[...].astype(BF16), wq_ref[...])
    for j in range(2 * PEER_HEADS):
        q_scr[j] = q[:, j * LANES:(j + 1) * LANES].astype(BF16)
    iota_k = lax.broadcasted_iota(jnp.int32, (nk, tm), 0)
    iota_c = lax.broadcasted_iota(jnp.int32, (kk * kk, tm), 0)
    neg = jnp.float32(-jnp.inf)

    def head(hd, carry):
        for p in range(2):
            vals = _dot_nt(keys_ref[hd * 2 + p], q_scr[hd * 2 + p])
            for r in range(kk):
                m = jnp.max(vals, axis=0, keepdims=True)
                am = jnp.min(jnp.where(vals == m, iota_k, nk), axis=0, keepdims=True)
                s_scr[p * kk + r:p * kk + r + 1, :] = m
                i_scr[p * kk + r:p * kk + r + 1, :] = am
                vals = jnp.where(iota_k == am, neg, vals)
        s1 = s_scr[kk:2 * kk, :]
        i1 = i_scr[kk:2 * kk, :]
        for a in range(kk):
            cand_scr[a * kk:(a + 1) * kk, :] = s_scr[a:a + 1, :] + s1
            cidx_scr[a * kk:(a + 1) * kk, :] = i_scr[a:a + 1, :] * nk + i1
        vals = cand_scr[...]
        cidx = cidx_scr[...]
        for r in range(kk):
            m = jnp.max(vals, axis=0, keepdims=True)
            am = jnp.min(jnp.where(vals == m, iota_c, kk * kk), axis=0, keepdims=True)
            sel = iota_c == am
            best_scr[r:r + 1, :] = m
            eh_scr[r:r + 1, :] = jnp.max(jnp.where(sel, cidx, -1), axis=0, keepdims=True)
            vals = jnp.where(sel, neg, vals)
        bs = best_scr[...]
        ex = jnp.exp(bs - bs[0:1, :])
        r0 = pl.multiple_of(hd * kk, kk)
        g_ref[pl.ds(r0, kk), :] = ex / jnp.sum(ex, axis=0, keepdims=True)
        eall_scr[pl.ds(r0, kk), :] = eh_scr[...]
        return carry

    lax.fori_loop(0, PEER_HEADS, head, 0)
    e_ref[...] = eall_scr[...].T


def _route(h1, wq, keys):
    t, d = h1.shape
    tm = min(256, t)
    nq = wq.shape[1]
    hk = PEER_HEADS * PEER_TOPK
    kk = PEER_TOPK
    return pl.pallas_call(
        _route_kernel,
        grid=(t // tm,),
        in_specs=[pl.BlockSpec((tm, d), lambda i: (i, 0)),
                  pl.BlockSpec((d, nq), lambda i: (0, 0)),
                  pl.BlockSpec(keys.shape, lambda i: (0, 0, 0))],
        out_specs=[pl.BlockSpec((tm, hk), lambda i: (i, 0)),
                   pl.BlockSpec((hk, tm), lambda i: (0, i))],
        out_shape=[jax.ShapeDtypeStruct((t, hk), jnp.int32),
                   jax.ShapeDtypeStruct((hk, t), F32)],
        scratch_shapes=[pltpu.VMEM((2 * PEER_HEADS, tm, LANES), BF16),
                        pltpu.VMEM((2 * kk, tm), F32), pltpu.VMEM((2 * kk, tm), jnp.int32),
                        pltpu.VMEM((kk * kk, tm), F32), pltpu.VMEM((kk * kk, tm), jnp.int32),
                        pltpu.VMEM((kk, tm), F32), pltpu.VMEM((kk, tm), jnp.int32),
                        pltpu.VMEM((hk, tm), jnp.int32)],
        compiler_params=pltpu.CompilerParams(
            dimension_semantics=("parallel",), vmem_limit_bytes=VMEM_LIMIT),
        name="route",
    )(h1, wq, keys)


def _peer_kernel(ids_ref, idn_ref, h_ref, gt_ref, uv_hbm, g2_ref, b2_ref, o_ref,
                 buf_a, buf_b, sem, y_scr, *, tt, alpha):
    i = pl.program_id(0)
    n = pl.num_programs(0)
    hk = ids_ref.shape[1]
    nslab, sub = uv_hbm.shape[1], uv_hbm.shape[2]
    half = sub // 2
    bufs = (buf_a, buf_b)

    def issue(idref, row0, slot):
        for t in range(tt):
            for k in range(hk):
                e = idref[row0 + t, k]
                pltpu.make_async_copy(uv_hbm.at[e], bufs[slot].at[:, pl.ds((t * hk + k) * sub, sub), :],
                                      sem.at[slot]).start(priority=k % 2)

    def wait(slot):
        pltpu.make_async_copy(bufs[slot], bufs[slot], sem.at[slot]).wait()

    gt = gt_ref[...]
    lane = lax.broadcasted_iota(jnp.int32, gt.shape, 1)
    tok0 = (i * 2 * tt) % LANES

    def compute(slot):
        buf = bufs[slot]
        for t in range(tt):
            row = slot * tt + t

            def rows(c, j):
                return buf[c, pl.ds(t * hk * sub + j, hk, stride=sub), :]

            part = None
            for j in range(half):
                for c in range(nslab):
                    seg = j * nslab + c
                    term = rows(c, j) * h_ref[row:row + 1, seg * LANES:(seg + 1) * LANES]
                    part = term if part is None else part + term
            pre = jnp.sum(part, axis=-1, keepdims=True)
            gate = jnp.sum(jnp.where(lane == tok0 + row, gt, 0.0), axis=-1, keepdims=True)
            act = 0.5 * pre * (1.0 + lax.erf(pre * (2.0 ** -0.5))) * gate
            for j in range(half):
                for c in range(nslab):
                    seg = j * nslab + c
                    y_scr[row:row + 1, seg * LANES:(seg + 1) * LANES] = jnp.sum(
                        act * rows(c, half + j), axis=0, keepdims=True)

    @pl.when(i == 0)
    def _():
        issue(ids_ref, 0, 0)

    wait(0)
    issue(ids_ref, tt, 1)
    compute(0)
    wait(1)
    issue(idn_ref, 0, 0)
    compute(1)

    @pl.when(i == n - 1)
    def _():
        wait(0)

    z = alpha * h_ref[...] + y_scr[...]
    o_ref[...] = _layernorm(z, g2_ref[...], b2_ref[...])


def _peer(ids, gt, h1, uv_tab, g2, b2, alpha):
    t, d = h1.shape
    hk = ids.shape[1]
    tt = 8
    nslab, sub = uv_tab.shape[1], uv_tab.shape[2]
    nsteps = t // (2 * tt)
    steps_per_lane_blk = LANES // (2 * tt)
    return pl.pallas_call(
        functools.partial(_peer_kernel, tt=tt, alpha=alpha),
        grid=(nsteps,),
        in_specs=[pl.BlockSpec((2 * tt, hk), lambda i: (i, 0), memory_space=pltpu.SMEM),
                  pl.BlockSpec((2 * tt, hk), lambda i: (jnp.minimum(i + 1, nsteps - 1), 0), memory_space=pltpu.SMEM),
                  pl.BlockSpec((2 * tt, d), lambda i: (i, 0)),
                  pl.BlockSpec((hk, LANES), lambda i: (0, i // steps_per_lane_blk)),
                  pl.BlockSpec(memory_space=pl.ANY),
                  pl.BlockSpec((1, d), lambda i: (0, 0)),
                  pl.BlockSpec((1, d), lambda i: (0, 0))],
        out_specs=pl.BlockSpec((2 * tt, d), lambda i: (i, 0)),
        out_shape=jax.ShapeDtypeStruct((t, d), F32),
        scratch_shapes=[pltpu.VMEM((nslab, tt * hk * sub, LANES), F32),
                        pltpu.VMEM((nslab, tt * hk * sub, LANES), F32),
                        pltpu.SemaphoreType.DMA((2,)),
                        pltpu.VMEM((2 * tt, d), F32)],
        compiler_params=pltpu.CompilerParams(
            dimension_semantics=("arbitrary",), vmem_limit_bytes=VMEM_LIMIT),
        name="peer",
    )(ids, ids, h1, gt, uv_tab, g2, b2)


def _layer(h, w_in, conv_w, a_log, dt_bias, norm_w, f_bias, w_out_gdn, w_out_fox, w_o, ln1_g, ln1_b,
           peer_wq, peer_keys, peer_u, peer_v, ln2_g, ln2_b, alpha):
    b, s, d = h.shape
    t = b * s
    qk = GDN_HEADS * GDN_DK
    fw = FX_HEADS * FX_DH
    o_gz = 4 * qk
    o_ga = o_gz
    o_fq = o_ga + 2 * GDN_HEADS
    o_ff = o_fq + 3 * fw
    o_gate = o_ff + FX_HEADS
    w_big = jnp.concatenate([w_in[:, o_gate:], w_in[:, :o_gz], w_in[:, o_fq:o_ff]], axis=1).astype(BF16)
    n_small = 2 * GDN_HEADS + FX_HEADS
    w_small = jnp.concatenate([w_in[:, o_ga:o_fq], w_in[:, o_ff:o_gate],
                               jnp.zeros((d, LANES - n_small), F32)], axis=1)
    params = jnp.zeros((8, LANES), F32)
    params = params.at[0, :GDN_HEADS].set(a_log).at[1, :GDN_HEADS].set(dt_bias)
    params = params.at[2, 2 * GDN_HEADS:n_small].set(f_bias)

    x2 = h.reshape(t, d)
    proj = _in_proj(x2, w_big)
    gexp, bexp, c, ct = _prep(h, w_small, params)
    proj3 = proj.reshape(b, s, proj.shape[1])
    gdn0 = 2 * d // LANES
    oa = _gdn(proj3, gexp, bexp, conv_w, norm_w.reshape(1, LANES),
              (gdn0, gdn0 + GDN_HEADS, gdn0 + 2 * GDN_HEADS, gdn0 + 3 * GDN_HEADS))
    fox0 = gdn0 + 4 * GDN_HEADS
    npair = fw // LANES
    ob = _fox(proj3, c, ct, (fox0, fox0 + npair, fox0 + 2 * npair))
    h1 = _mix(oa.reshape(t, qk), ob.reshape(t, fw), proj, x2,
              w_out_gdn.astype(BF16), w_out_fox.astype(BF16), w_o.astype(BF16),
              ln1_g.reshape(1, d), ln1_b.reshape(1, d), alpha)
    keys = peer_keys.reshape(2 * PEER_HEADS, PEER_NKEYS, peer_keys.shape[-1]).astype(BF16)
    ids, gt = _route(h1, peer_wq.astype(BF16), keys)
    ne = peer_u.shape[0]
    half = PEER_SUB // 2
    nslab = d // (half * LANES)
    uv_tab = jnp.concatenate([peer_u.reshape(ne, half, nslab, LANES),
                              peer_v.reshape(ne, half, nslab, LANES)], axis=1).transpose(0, 2, 1, 3)
    out = _peer(ids, gt, h1, uv_tab, ln2_g.reshape(1, d), ln2_b.reshape(1, d), alpha)
    return out.reshape(b, s, d)


def kernel(x, w_in, gdn_conv_w, gdn_a_log, gdn_dt_bias, gdn_norm_w, fox_f_bias, w_out_gdn, w_out_fox, w_o,
           ln1_g, ln1_b, peer_wq, peer_keys, peer_u, peer_v, ln2_g, ln2_b):
    depth = w_in.shape[0]
    alpha = (2.0 * depth) ** 0.25
    h = x
    for l in range(depth):
        h = _layer(h, w_in[l], gdn_conv_w[l], gdn_a_log[l], gdn_dt_bias[l], gdn_norm_w[l], fox_f_bias[l],
                   w_out_gdn[l], w_out_fox[l], w_o[l], ln1_g[l], ln1_b[l], peer_wq[l], peer_keys[l],
                   peer_u[l], peer_v[l], ln2_g[l], ln2_b[l], alpha)
    return h
```

```python
import functools

import jax
import jax.numpy as jnp
from jax import lax
from jax.experimental import pallas as pl
from jax.experimental.pallas import tpu as pltpu

F32 = jnp.float32
BF16 = jnp.bfloat16
HI = lax.Precision.HIGHEST

LANES = 128
CHUNK = 64
GDN_HEADS = 4
GDN_DK = 128
FX_HEADS = 8
FX_DH = 64
PEER_HEADS = 8
PEER_NKEYS = 128
PEER_TOPK = 16
PEER_SUB = 4
LN_EPS = 1e-5
VMEM_LIMIT = 48 * 1024 * 1024


def _dot(a, b, prec=None):
    return jnp.dot(a, b, preferred_element_type=F32, precision=prec)


def _dot_nt(a, b, prec=None):
    return lax.dot_general(a, b, (((1,), (1,)), ((), ())), preferred_element_type=F32, precision=prec)


def _dot_tn(a, b, prec=None):
    return lax.dot_general(a, b, (((0,), (0,)), ((), ())), preferred_element_type=F32, precision=prec)


def _softplus(x):
    return jnp.maximum(x, 0.0) + jnp.log1p(jnp.exp(-jnp.abs(x)))


def _layernorm(z, g, b):
    mu = jnp.mean(z, axis=-1, keepdims=True)
    zc = z - mu
    var = jnp.mean(zc * zc, axis=-1, keepdims=True)
    return zc * lax.rsqrt(var + LN_EPS) * g + b


def _mm_kernel(x_ref, w_ref, o_ref):
    o_ref[...] = _dot(x_ref[...].astype(BF16), w_ref[...]).astype(o_ref.dtype)


def _in_proj(x2, w_big, out_dtype, name):
    t, d = x2.shape
    n = w_big.shape[1]
    tm = min(1024, t)
    tn = 512
    return pl.pallas_call(
        _mm_kernel,
        grid=(t // tm, n // tn),
        in_specs=[pl.BlockSpec((tm, d), lambda i, j: (i, 0)),
                  pl.BlockSpec((d, tn), lambda i, j: (0, j))],
        out_specs=pl.BlockSpec((tm, tn), lambda i, j: (i, j)),
        out_shape=jax.ShapeDtypeStruct((t, n), out_dtype),
        compiler_params=pltpu.CompilerParams(
            dimension_semantics=("parallel", "parallel"), vmem_limit_bytes=VMEM_LIMIT),
        name=name,
    )(x2, w_big)


def _prep_kernel(x_ref, w_ref, par_ref, gexp_ref, bexp_ref, c_ref, ct_ref, carry_ref):
    ts = x_ref.shape[1]

    @pl.when(pl.program_id(1) == 0)
    def _():
        carry_ref[...] = jnp.zeros_like(carry_ref)

    small = _dot(x_ref[0], w_ref[...], HI)
    a_log = par_ref[0:1, :]
    dt_bias = par_ref[1:2, :]
    f_bias = par_ref[2:3, :]
    g = -jnp.exp(a_log) * _softplus(small + dt_bias)
    beta = jax.nn.sigmoid(small)
    lane = lax.broadcasted_iota(jnp.int32, (ts, LANES), 1)
    log_f = jnp.where((lane >= 8) & (lane < 16), -_softplus(-(small + f_bias)), 0.0)
    row = lax.broadcasted_iota(jnp.int32, (ts, ts), 0)
    col = lax.broadcasted_iota(jnp.int32, (ts, ts), 1)
    tril = (row >= col).astype(F32)
    c = _dot(tril, log_f, HI) + carry_ref[...]
    carry_ref[...] = c[ts - 1:ts, :]
    c_ref[0] = c
    ct_ref[0] = c.T[8:16, :]
    gexp_ref[0] = jnp.concatenate(
        [jnp.broadcast_to(g[:, h:h + 1], (ts, LANES)) for h in range(GDN_HEADS)], axis=1)
    bexp_ref[0] = jnp.concatenate(
        [jnp.broadcast_to(beta[:, GDN_HEADS + h:GDN_HEADS + h + 1], (ts, LANES)) for h in range(GDN_HEADS)], axis=1)


def _prep(x, w_small, params):
    b, s, d = x.shape
    ts = min(512, s)
    hw = GDN_HEADS * LANES
    return pl.pallas_call(
        _prep_kernel,
        grid=(b, s // ts),
        in_specs=[pl.BlockSpec((1, ts, d), lambda i, j: (i, j, 0)),
                  pl.BlockSpec((d, LANES), lambda i, j: (0, 0)),
                  pl.BlockSpec((8, LANES), lambda i, j: (0, 0))],
        out_specs=[pl.BlockSpec((1, ts, hw), lambda i, j: (i, j, 0)),
                   pl.BlockSpec((1, ts, hw), lambda i, j: (i, j, 0)),
                   pl.BlockSpec((1, ts, LANES), lambda i, j: (i, j, 0)),
                   pl.BlockSpec((1, 8, ts), lambda i, j: (i, 0, j))],
        out_shape=[jax.ShapeDtypeStruct((b, s, hw), F32),
                   jax.ShapeDtypeStruct((b, s, hw), F32),
                   jax.ShapeDtypeStruct((b, s, LANES), F32),
                   jax.ShapeDtypeStruct((b, 8, s), F32)],
        scratch_shapes=[pltpu.VMEM((1, LANES), F32)],
        compiler_params=pltpu.CompilerParams(
            dimension_semantics=("parallel", "arbitrary"), vmem_limit_bytes=VMEM_LIMIT),
        name="prep",
    )(x, w_small, params)


def _each(fn, *lists):
    return [fn(*args) for args in zip(*lists)]


def _unit_lower_inverse(ms, masks):
    eye, blk16, blk32 = masks
    hi = lambda a, b: _dot(a, b, HI)
    n1 = _each(lambda m: -jnp.where(blk16, m, 0.0), ms)
    l1 = _each(lambda m: jnp.where(blk32 & jnp.logical_not(blk16), m, 0.0), ms)
    l2 = _each(lambda m: jnp.where(blk32, 0.0, m), ms)
    n2 = _each(hi, n1, n1)
    p = _each(lambda a, b: hi(eye + a, eye + b), n1, n2)
    n4 = _each(hi, n2, n2)
    p = _each(lambda a, b: hi(a, eye + b), p, n4)
    n8 = _each(hi, n4, n4)
    d_inv = _each(lambda a, b: hi(a, eye + b), p, n8)
    dl = _each(hi, d_inv, l1)
    a32 = _each(lambda a, b: a - hi(b, a), d_inv, dl)
    al = _each(hi, a32, l2)
    return _each(lambda a, b: a - hi(b, a), a32, al)


def _gdn_kernel(q_ref, k_ref, v_ref, z_ref, g_ref, b_ref, cwq_ref, cwk_ref, cwv_ref, nw_ref,
                o_ref, qn, kn, vn, st):
    s = q_ref.shape[1]
    c = CHUNK
    nh = q_ref.shape[2] // LANES
    row = lax.broadcasted_iota(jnp.int32, (s, LANES), 0)

    def conv_silu(x, w):
        y = x * w[3:4, :]
        for sh in (1, 2, 3):
            xs = jnp.where(row >= sh, pltpu.roll(x, sh, axis=0), 0.0)
            y = y + xs * w[3 - sh:4 - sh, :]
        return y * jax.nn.sigmoid(y)

    def l2norm(x):
        return x * lax.rsqrt(jnp.sum(x * x, axis=-1, keepdims=True) + 1e-6)

    for j in range(nh):
        hs = slice(j * LANES, (j + 1) * LANES)
        qn[:, hs] = l2norm(conv_silu(q_ref[0, :, hs], cwq_ref[:, hs])) * (GDN_DK ** -0.5)
        kn[:, hs] = l2norm(conv_silu(k_ref[0, :, hs], cwk_ref[:, hs]))
        vn[:, hs] = conv_silu(v_ref[0, :, hs], cwv_ref[:, hs])
    st[...] = jnp.zeros_like(st)

    ri = lax.broadcasted_iota(jnp.int32, (c, c), 0)
    ci = lax.broadcasted_iota(jnp.int32, (c, c), 1)
    tril = ri >= ci
    strict = ri > ci
    t_inc = tril.astype(F32)
    eye = (ri == ci).astype(F32)
    blk16 = (ri >> 4) == (ci >> 4)
    blk32 = (ri >> 5) == (ci >> 5)
    l2 = lax.broadcasted_iota(jnp.int32, (c, 2 * LANES), 0)
    j2 = lax.broadcasted_iota(jnp.int32, (c, 2 * LANES), 1)
    ux = jnp.where((j2 >= c) | (l2 > j2), 1.0, 0.0).astype(F32)
    nw = nw_ref[...]

    heads = list(range(nh))
    lanes_of = [slice(j * LANES, (j + 1) * LANES) for j in heads]

    def chunk(n, carry):
        r0 = pl.multiple_of(n * c, c)
        rows = pl.ds(r0, c)
        q = [qn[rows, hs] for hs in lanes_of]
        k = [kn[rows, hs] for hs in lanes_of]
        v = [vn[rows, hs] for hs in lanes_of]
        gb = [g_ref[0, rows, hs] for hs in lanes_of]
        bb = [b_ref[0, rows, hs] for hs in lanes_of]
        d = _each(lambda g: _dot(t_inc, jnp.concatenate([g, g], axis=1) * ux, HI), gb)
        kb = _each(lambda a, b: a * b, k, bb)
        kk = _each(lambda a, b: _dot_nt(a, b, HI), kb, k)
        qk = _each(lambda a, b: _dot_nt(a.astype(BF16), b.astype(BF16)), q, k)
        gc = [x[:, LANES:] for x in d]
        decay = [jnp.where(tril, jnp.exp(x[:, :c]), 0.0) for x in d]
        m = _each(lambda a, b: jnp.where(strict, a * b, 0.0), kk, decay)
        a_inv = _unit_lower_inverse(m, (eye, blk16, blk32))
        egc = _each(jnp.exp, gc)
        rhs = _each(lambda vv, b, kbb, e: jnp.concatenate([vv * b, kbb * e], axis=1), v, bb, kb, egc)
        sol = _each(lambda a, r: _dot(a, r, HI), a_inv, rhs)
        qk = _each(lambda a, b: (a * b).astype(BF16), qk, decay)
        qg = _each(lambda a, e: (a * e).astype(BF16), q, egc)
        gl = [x[c - 1:c, :] for x in gc]
        kd = _each(lambda kk_, g1, g: (kk_ * jnp.exp(g1 - g)).astype(BF16), k, gl, gc)
        state = [st[j] for j in heads]
        state_b = [x.astype(BF16) for x in state]
        v_new = _each(lambda sl, sb: sl[:, :LANES] - _dot(sl[:, LANES:].astype(BF16), sb), sol, state_b)
        v_new_b = [x.astype(BF16) for x in v_new]
        o = _each(lambda a, sb, b, vb: _dot(a, sb) + _dot(b, vb), qg, state_b, qk, v_new_b)
        new_state = _each(lambda x, g1, a, vb: x * jnp.exp(g1) + _dot_tn(a, vb), state, gl, kd, v_new_b)
        for j in heads:
            st[j] = new_state[j]
            on = o[j] * lax.rsqrt(jnp.mean(o[j] * o[j], axis=-1, keepdims=True) + 1e-6) * nw
            z = z_ref[0, rows, lanes_of[j]]
            o_ref[0, rows, lanes_of[j]] = on * (z * jax.nn.sigmoid(z))
        return carry

    lax.fori_loop(0, s // c, chunk, 0)


def _gdn(proj3, gexp, bexp, conv_w, norm_w, cols):
    b, s, _ = proj3.shape
    nh = 2
    wd = nh * LANES
    cq, ck, cv, cz = (c0 // nh for c0 in cols)

    def blk(c0):
        return pl.BlockSpec((1, s, wd), lambda i, h: (i, 0, c0 + h))

    def cw(c0):
        return pl.BlockSpec((conv_w.shape[0], wd), lambda i, h: (0, c0 + h))

    head = pl.BlockSpec((1, s, wd), lambda i, h: (i, 0, h))
    return pl.pallas_call(
        _gdn_kernel,
        grid=(b, GDN_HEADS // nh),
        in_specs=[blk(cq), blk(ck), blk(cv), blk(cz), head, head,
                  cw(0), cw(GDN_HEADS // nh), cw(2 * GDN_HEADS // nh),
                  pl.BlockSpec((1, LANES), lambda i, h: (0, 0))],
        out_specs=head,
        out_shape=jax.ShapeDtypeStruct((b, s, GDN_HEADS * LANES), F32),
        scratch_shapes=[pltpu.VMEM((s, wd), F32), pltpu.VMEM((s, wd), F32),
                        pltpu.VMEM((s, wd), F32), pltpu.VMEM((nh, GDN_DK, LANES), F32)],
        compiler_params=pltpu.CompilerParams(
            dimension_semantics=("parallel", "parallel"), vmem_limit_bytes=VMEM_LIMIT),
        name="gdn",
    )(proj3, proj3, proj3, proj3, gexp, bexp, conv_w, conv_w, conv_w, norm_w)


def _fox_kernel(q_ref, k_ref, v_ref, c_ref, ct_ref, o_ref, *, tk):
    tq = q_ref.shape[1]
    nj = q_ref.shape[2] // FX_DH
    g = pl.program_id(1)
    qi = pl.program_id(2)
    q = q_ref[0]
    cblk = c_ref[0]
    lane = lax.broadcasted_iota(jnp.int32, (tq, LANES), 1)
    qpos = qi * tq + lax.broadcasted_iota(jnp.int32, (tq, tk), 0)
    kofs = lax.broadcasted_iota(jnp.int32, (tq, tk), 1)
    heads = list(range(nj))
    lanes_of = [slice(j * FX_DH, (j + 1) * FX_DH) for j in heads]
    ccol = [jnp.sum(jnp.where(lane == 8 + g * nj + j, cblk, 0.0), axis=-1, keepdims=True) for j in heads]
    qh = [q[:, hs] * jnp.asarray(FX_DH ** -0.5, BF16) for hs in lanes_of]

    def body(kv, carry):
        k0 = pl.multiple_of(kv * tk, tk)
        causal = qpos >= k0 + kofs
        kblk = k_ref[0, pl.ds(k0, tk), :]
        vblk = v_ref[0, pl.ds(k0, tk), :]
        sc = [_dot_nt(qh[j], kblk[:, lanes_of[j]]) for j in heads]
        crow = [ct_ref[0, pl.ds(g * nj + j, 1), pl.ds(k0, tk)] for j in heads]
        sc = [jnp.where(causal, sc[j] + ccol[j] - crow[j], -1e30) for j in heads]
        m_new = [jnp.maximum(carry[j][0], jnp.max(sc[j], axis=-1, keepdims=True)) for j in heads]
        a = [jnp.exp(carry[j][0] - m_new[j]) for j in heads]
        p = [jnp.exp(sc[j] - m_new[j]) for j in heads]
        l = [a[j] * carry[j][1] + jnp.sum(p[j], axis=-1, keepdims=True) for j in heads]
        acc = [a[j] * carry[j][2] + _dot(p[j].astype(BF16), vblk[:, lanes_of[j]]) for j in heads]
        return tuple((m_new[j], l[j], acc[j]) for j in heads)

    init = tuple((jnp.full((tq, 1), -1e30, F32), jnp.zeros((tq, 1), F32), jnp.zeros((tq, FX_DH), F32))
                 for _ in heads)
    nkv = (qi * tq + tq - 1) // tk + 1
    res = lax.fori_loop(0, nkv, body, init)
    o_ref[0] = jnp.concatenate([acc / l for _, l, acc in res], axis=1)


def _fox(pf3, c, ct):
    b, s, _ = pf3.shape
    wd = 2 * LANES
    tq = min(128, s)
    tk = min(256, s)
    ngrp = FX_HEADS * FX_DH // wd
    return pl.pallas_call(
        functools.partial(_fox_kernel, tk=tk),
        grid=(b, ngrp, s // tq),
        in_specs=[pl.BlockSpec((1, tq, wd), lambda i, h, t: (i, t, h)),
                  pl.BlockSpec((1, s, wd), lambda i, h, t: (i, 0, ngrp + h)),
                  pl.BlockSpec((1, s, wd), lambda i, h, t: (i, 0, 2 * ngrp + h)),
                  pl.BlockSpec((1, tq, LANES), lambda i, h, t: (i, t, 0)),
                  pl.BlockSpec((1, 8, s), lambda i, h, t: (i, 0, 0))],
        out_specs=pl.BlockSpec((1, tq, wd), lambda i, h, t: (i, t, h)),
        out_shape=jax.ShapeDtypeStruct((b, s, ngrp * wd), F32),
        compiler_params=pltpu.CompilerParams(
            dimension_semantics=("parallel", "parallel", "parallel"), vmem_limit_bytes=VMEM_LIMIT),
        name="fox",
    )(pf3, pf3, pf3, c, ct)


def _mix_kernel(oa_ref, ob_ref, ga_ref, gb_ref, x_ref, wa_ref, wb_ref, wo_ref, g1_ref, b1_ref, o_ref, *, alpha):
    ya = _dot(oa_ref[...].astype(BF16), wa_ref[...])
    yb = _dot(ob_ref[...].astype(BF16), wb_ref[...])
    mix = jax.nn.sigmoid(ga_ref[...]) * ya + jax.nn.sigmoid(gb_ref[...]) * yb
    z = alpha * x_ref[...] + _dot(mix.astype(BF16), wo_ref[...])
    o_ref[...] = _layernorm(z, g1_ref[...], b1_ref[...])


def _mix(oa, ob, proj, x2, wa, wb, wo, g1, b1, alpha):
    t, d = x2.shape
    tm = min(512, t)
    w = oa.shape[1]
    full = lambda r, c: pl.BlockSpec((r, c), lambda i: (0, 0))
    return pl.pallas_call(
        functools.partial(_mix_kernel, alpha=alpha),
        grid=(t // tm,),
        in_specs=[pl.BlockSpec((tm, w), lambda i: (i, 0)),
                  pl.BlockSpec((tm, w), lambda i: (i, 0)),
                  pl.BlockSpec((tm, d), lambda i: (i, 0)),
                  pl.BlockSpec((tm, d), lambda i: (i, 1)),
                  pl.BlockSpec((tm, d), lambda i: (i, 0)),
                  full(w, d), full(w, d), full(d, d), full(1, d), full(1, d)],
        out_specs=pl.BlockSpec((tm, d), lambda i: (i, 0)),
        out_shape=jax.ShapeDtypeStruct((t, d), F32),
        compiler_params=pltpu.CompilerParams(
            dimension_semantics=("parallel",), vmem_limit_bytes=VMEM_LIMIT),
        name="mix",
    )(oa, ob, proj, proj, x2, wa, wb, wo, g1, b1)


def _route_kernel(h_ref, wq_ref, keys_ref, flat_ref, cmask_ref, e_ref, g_ref,
                  q_scr, s_scr, i_scr, best_scr, eh_scr, eall_scr):
    tm = h_ref.shape[0]
    nk = PEER_NKEYS
    kk = PEER_TOPK
    q = _dot(h_ref[...].astype(BF16), wq_ref[...])
    for j in range(2 * PEER_HEADS):
        q_scr[j] = q[:, j * LANES:(j + 1) * LANES].astype(BF16)
    iota_k = lax.broadcasted_iota(jnp.int32, (nk, LANES), 0)
    flat = flat_ref[...]
    cmask = cmask_ref[...]
    neg = jnp.float32(-jnp.inf)

    def route_tokens(hd, lh):
        tok = slice(lh * LANES, (lh + 1) * LANES)
        for p in range(2):
            vals = _dot_nt(keys_ref[hd * 2 + p], q_scr[hd * 2 + p, tok, :])
            for r in range(kk):
                m = jnp.max(vals, axis=0, keepdims=True)
                am = jnp.min(jnp.where(vals == m, iota_k, nk), axis=0, keepdims=True)
                s_scr[p * kk + r:p * kk + r + 1, :] = m
                i_scr[p * kk + r:p * kk + r + 1, :] = am
                vals = jnp.where(iota_k == am, neg, vals)
        s1 = s_scr[kk:kk + 8, :]
        i1 = i_scr[kk:kk + 8, :]
        cand = [s_scr[0:1, :] + s_scr[kk:2 * kk, :]]
        cidx = [i_scr[0:1, :] * nk + i_scr[kk:2 * kk, :]]
        for a in range(1, 8):
            cand.append(s_scr[a:a + 1, :] + s1)
            cidx.append(i_scr[a:a + 1, :] * nk + i1)
        cand.append(s_scr[8:kk, :] + s_scr[kk:kk + 1, :])
        cidx.append(i_scr[8:kk, :] * nk + i_scr[kk:kk + 1, :])
        vals = jnp.concatenate(cand, axis=0) + cmask
        cidx = jnp.concatenate(cidx, axis=0)
        for r in range(kk):
            m = jnp.max(vals, axis=0, keepdims=True)
            am = jnp.min(jnp.where(vals == m, flat, 2 * kk * kk), axis=0, keepdims=True)
            sel = flat == am
            best_scr[r:r + 1, :] = m
            eh_scr[r:r + 1, :] = jnp.max(jnp.where(sel, cidx, -1), axis=0, keepdims=True)
            vals = jnp.where(sel, neg, vals)
        bs = best_scr[...]
        ex = jnp.exp(bs - bs[0:1, :])
        r0 = pl.multiple_of(hd * kk, kk)
        g_ref[pl.ds(r0, kk), tok] = ex / jnp.sum(ex, axis=0, keepdims=True)
        eall_scr[pl.ds(r0, kk), tok] = eh_scr[...]

    def head(hd, carry):
        for lh in range(tm // LANES):
            route_tokens(hd, lh)
        return carry

    lax.fori_loop(0, PEER_HEADS, head, 0)
    e_ref[...] = eall_scr[...].T


def _route(h1, wq, keys):
    t, d = h1.shape
    tm = min(256, t)
    nq = wq.shape[1]
    hk = PEER_HEADS * PEER_TOPK
    kk = PEER_TOPK
    pairs = [(0, bb) for bb in range(kk)]
    for a in range(1, 8):
        pairs += [(a, bb) for bb in range(8)]
    pairs += [(a, 0) for a in range(8, kk)]
    real = [(a + 1) * (bb + 1) <= kk for a, bb in pairs]
    flat = [a * kk + bb if ok else kk * kk + r for r, ((a, bb), ok) in enumerate(zip(pairs, real))]
    flat = jnp.broadcast_to(jnp.asarray(flat, jnp.int32)[:, None], (len(pairs), LANES))
    cmask = jnp.broadcast_to(jnp.asarray([0.0 if ok else -jnp.inf for ok in real], F32)[:, None],
                             (len(pairs), LANES))
    return pl.pallas_call(
        _route_kernel,
        grid=(t // tm,),
        in_specs=[pl.BlockSpec((tm, d), lambda i: (i, 0)),
                  pl.BlockSpec((d, nq), lambda i: (0, 0)),
                  pl.BlockSpec(keys.shape, lambda i: (0, 0, 0)),
                  pl.BlockSpec(flat.shape, lambda i: (0, 0)),
                  pl.BlockSpec(cmask.shape, lambda i: (0, 0))],
        out_specs=[pl.BlockSpec((tm, hk), lambda i: (i, 0)),
                   pl.BlockSpec((hk, tm), lambda i: (0, i))],
        out_shape=[jax.ShapeDtypeStruct((t, hk), jnp.int32),
                   jax.ShapeDtypeStruct((hk, t), F32)],
        scratch_shapes=[pltpu.VMEM((2 * PEER_HEADS, tm, LANES), BF16),
                        pltpu.VMEM((2 * kk, LANES), F32), pltpu.VMEM((2 * kk, LANES), jnp.int32),
                        pltpu.VMEM((kk, LANES), F32), pltpu.VMEM((kk, LANES), jnp.int32),
                        pltpu.VMEM((hk, tm), jnp.int32)],
        compiler_params=pltpu.CompilerParams(
            dimension_semantics=("parallel",), vmem_limit_bytes=VMEM_LIMIT),
        name="route",
    )(h1, wq, keys, flat, cmask)


def _peer_kernel(ids_ref, idn_ref, h_ref, gt_ref, uv_hbm, g2_ref, b2_ref, o_ref,
                 buf_a, buf_b, sem, y_scr, *, tt, alpha):
    i = pl.program_id(0)
    n = pl.num_programs(0)
    hk = ids_ref.shape[1]
    nslab, sub = uv_hbm.shape[1], uv_hbm.shape[2]
    half = sub // 2
    bufs = (buf_a, buf_b)

    def issue(idref, row0, slot):
        for t in range(tt):
            for k in range(hk):
                e = idref[row0 + t, k]
                pltpu.make_async_copy(uv_hbm.at[e], bufs[slot].at[:, pl.ds((t * hk + k) * sub, sub), :],
                                      sem.at[slot]).start(priority=k % 2)

    def wait(slot):
        pltpu.make_async_copy(bufs[slot], bufs[slot], sem.at[slot]).wait()

    gt = gt_ref[...]
    lane = lax.broadcasted_iota(jnp.int32, gt.shape, 1)
    tok0 = (i * 2 * tt) % LANES

    def compute(slot):
        buf = bufs[slot]
        for t in range(tt):
            row = slot * tt + t

            def rows(c, j):
                return buf[c, pl.ds(t * hk * sub + j, hk, stride=sub), :]

            part = None
            for j in range(half):
                for c in range(nslab):
                    seg = j * nslab + c
                    term = rows(c, j) * h_ref[row:row + 1, seg * LANES:(seg + 1) * LANES]
                    part = term if part is None else part + term
            pre = jnp.sum(part, axis=-1, keepdims=True)
            gate = jnp.sum(jnp.where(lane == tok0 + row, gt, 0.0), axis=-1, keepdims=True)
            act = 0.5 * pre * (1.0 + lax.erf(pre * (2.0 ** -0.5))) * gate
            for j in range(half):
                for c in range(nslab):
                    seg = j * nslab + c
                    y_scr[row:row + 1, seg * LANES:(seg + 1) * LANES] = jnp.sum(
                        act * rows(c, half + j), axis=0, keepdims=True)

    @pl.when(i == 0)
    def _():
        issue(ids_ref, 0, 0)

    wait(0)
    issue(ids_ref, tt, 1)
    compute(0)
    wait(1)
    issue(idn_ref, 0, 0)
    compute(1)

    @pl.when(i == n - 1)
    def _():
        wait(0)

    z = alpha * h_ref[...] + y_scr[...]
    o_ref[...] = _layernorm(z, g2_ref[...], b2_ref[...])


def _peer(ids, gt, h1, uv_tab, g2, b2, alpha):
    t, d = h1.shape
    hk = ids.shape[1]
    tt = 8
    nslab, sub = uv_tab.shape[1], uv_tab.shape[2]
    nsteps = t // (2 * tt)
    steps_per_lane_blk = LANES // (2 * tt)
    return pl.pallas_call(
        functools.partial(_peer_kernel, tt=tt, alpha=alpha),
        grid=(nsteps,),
        in_specs=[pl.BlockSpec((2 * tt, hk), lambda i: (i, 0), memory_space=pltpu.SMEM),
                  pl.BlockSpec((2 * tt, hk), lambda i: (jnp.minimum(i + 1, nsteps - 1), 0), memory_space=pltpu.SMEM),
                  pl.BlockSpec((2 * tt, d), lambda i: (i, 0)),
                  pl.BlockSpec((hk, LANES), lambda i: (0, i // steps_per_lane_blk)),
                  pl.BlockSpec(memory_space=pl.ANY),
                  pl.BlockSpec((1, d), lambda i: (0, 0)),
                  pl.BlockSpec((1, d), lambda i: (0, 0))],
        out_specs=pl.BlockSpec((2 * tt, d), lambda i: (i, 0)),
        out_shape=jax.ShapeDtypeStruct((t, d), F32),
        scratch_shapes=[pltpu.VMEM((nslab, tt * hk * sub, LANES), F32),
                        pltpu.VMEM((nslab, tt * hk * sub, LANES), F32),
                        pltpu.SemaphoreType.DMA((2,)),
                        pltpu.VMEM((2 * tt, d), F32)],
        compiler_params=pltpu.CompilerParams(
            dimension_semantics=("arbitrary",), vmem_limit_bytes=VMEM_LIMIT),
        name="peer",
    )(ids, ids, h1, gt, uv_tab, g2, b2)


def _layer(h, w_in, conv_w, a_log, dt_bias, norm_w, f_bias, w_out_gdn, w_out_fox, w_o, ln1_g, ln1_b,
           peer_wq, peer_keys, peer_u, peer_v, ln2_g, ln2_b, alpha):
    b, s, d = h.shape
    t = b * s
    qk = GDN_HEADS * GDN_DK
    fw = FX_HEADS * FX_DH
    o_gz = 4 * qk
    o_ga = o_gz
    o_fq = o_ga + 2 * GDN_HEADS
    o_ff = o_fq + 3 * fw
    o_gate = o_ff + FX_HEADS
    w_big = jnp.concatenate([w_in[:, o_gate:], w_in[:, :o_gz]], axis=1).astype(BF16)
    w_fox = w_in[:, o_fq:o_ff].astype(BF16)
    n_small = 2 * GDN_HEADS + FX_HEADS
    w_small = jnp.concatenate([w_in[:, o_ga:o_fq], w_in[:, o_ff:o_gate],
                               jnp.zeros((d, LANES - n_small), F32)], axis=1)
    params = jnp.zeros((8, LANES), F32)
    params = params.at[0, :GDN_HEADS].set(a_log).at[1, :GDN_HEADS].set(dt_bias)
    params = params.at[2, 2 * GDN_HEADS:n_small].set(f_bias)

    x2 = h.reshape(t, d)
    proj = _in_proj(x2, w_big, F32, "in_proj")
    pf = _in_proj(x2, w_fox, BF16, "in_proj_fox")
    gexp, bexp, c, ct = _prep(h, w_small, params)
    proj3 = proj.reshape(b, s, proj.shape[1])
    gdn0 = 2 * d // LANES
    oa = _gdn(proj3, gexp, bexp, conv_w, norm_w.reshape(1, LANES),
              (gdn0, gdn0 + GDN_HEADS, gdn0 + 2 * GDN_HEADS, gdn0 + 3 * GDN_HEADS))
    ob = _fox(pf.reshape(b, s, 3 * fw), c, ct)
    h1 = _mix(oa.reshape(t, qk), ob.reshape(t, fw), proj, x2,
              w_out_gdn.astype(BF16), w_out_fox.astype(BF16), w_o.astype(BF16),
              ln1_g.reshape(1, d), ln1_b.reshape(1, d), alpha)
    keys = peer_keys.reshape(2 * PEER_HEADS, PEER_NKEYS, peer_keys.shape[-1]).astype(BF16)
    ids, gt = _route(h1, peer_wq.astype(BF16), keys)
    ne = peer_u.shape[0]
    half = PEER_SUB // 2
    nslab = d // (half * LANES)
    uv_tab = jnp.concatenate([peer_u.reshape(ne, half, nslab, LANES),
                              peer_v.reshape(ne, half, nslab, LANES)], axis=1).transpose(0, 2, 1, 3)
    out = _peer(ids, gt, h1, uv_tab, ln2_g.reshape(1, d), ln2_b.reshape(1, d), alpha)
    return out.reshape(b, s, d)


def kernel(x, w_in, gdn_conv_w, gdn_a_log, gdn_dt_bias, gdn_norm_w, fox_f_bias, w_out_gdn, w_out_fox, w_o,
           ln1_g, ln1_b, peer_wq, peer_keys, peer_u, peer_v, ln2_g, ln2_b):
    depth = w_in.shape[0]
    alpha = (2.0 * depth) ** 0.25
    h = x
    for l in range(depth):
        h = _layer(h, w_in[l], gdn_conv_w[l], gdn_a_log[l], gdn_dt_bias[l], gdn_norm_w[l], fox_f_bias[l],
                   w_out_gdn[l], w_out_fox[l], w_o[l], ln1_g[l], ln1_b[l], peer_wq[l], peer_keys[l],
                   peer_u[l], peer_v[l], ln2_g[l], ln2_b[l], alpha)
    return h
```

```python
import functools

import jax
import jax.numpy as jnp
from jax import lax
from jax.experimental import pallas as pl
from jax.experimental.pallas import tpu as pltpu

F32 = jnp.float32
BF16 = jnp.bfloat16
HI = lax.Precision.HIGHEST

LANES = 128
CHUNK = 64
GDN_HEADS = 4
GDN_DK = 128
FX_HEADS = 8
FX_DH = 64
PEER_HEADS = 8
PEER_NKEYS = 128
PEER_TOPK = 16
PEER_SUB = 4
LN_EPS = 1e-5
VMEM_LIMIT = 48 * 1024 * 1024


def _dot(a, b, prec=None):
    return jnp.dot(a, b, preferred_element_type=F32, precision=prec)


def _dot_nt(a, b, prec=None):
    return lax.dot_general(a, b, (((1,), (1,)), ((), ())), preferred_element_type=F32, precision=prec)


def _dot_tn(a, b, prec=None):
    return lax.dot_general(a, b, (((0,), (0,)), ((), ())), preferred_element_type=F32, precision=prec)


def _softplus(x):
    return jnp.maximum(x, 0.0) + jnp.log1p(jnp.exp(-jnp.abs(x)))


def _layernorm(z, g, b):
    mu = jnp.mean(z, axis=-1, keepdims=True)
    zc = z - mu
    var = jnp.mean(zc * zc, axis=-1, keepdims=True)
    return zc * lax.rsqrt(var + LN_EPS) * g + b


def _mm_kernel(x_ref, w_ref, o_ref):
    o_ref[...] = _dot(x_ref[...].astype(BF16), w_ref[...]).astype(o_ref.dtype)


def _in_proj(x2, w_big, out_dtype, name):
    t, d = x2.shape
    n = w_big.shape[1]
    tm = min(1024, t)
    tn = 512
    return pl.pallas_call(
        _mm_kernel,
        grid=(t // tm, n // tn),
        in_specs=[pl.BlockSpec((tm, d), lambda i, j: (i, 0)),
                  pl.BlockSpec((d, tn), lambda i, j: (0, j))],
        out_specs=pl.BlockSpec((tm, tn), lambda i, j: (i, j)),
        out_shape=jax.ShapeDtypeStruct((t, n), out_dtype),
        compiler_params=pltpu.CompilerParams(
            dimension_semantics=("parallel", "parallel"), vmem_limit_bytes=VMEM_LIMIT),
        name=name,
    )(x2, w_big)


def _prep_kernel(x_ref, w_ref, par_ref, gexp_ref, bexp_ref, c_ref, ct_ref, carry_ref):
    ts = x_ref.shape[1]

    @pl.when(pl.program_id(1) == 0)
    def _():
        carry_ref[...] = jnp.zeros_like(carry_ref)

    small = _dot(x_ref[0], w_ref[...], HI)
    a_log = par_ref[0:1, :]
    dt_bias = par_ref[1:2, :]
    f_bias = par_ref[2:3, :]
    g = -jnp.exp(a_log) * _softplus(small + dt_bias)
    beta = jax.nn.sigmoid(small)
    lane = lax.broadcasted_iota(jnp.int32, (ts, LANES), 1)
    log_f = jnp.where((lane >= 8) & (lane < 16), -_softplus(-(small + f_bias)), 0.0)
    row = lax.broadcasted_iota(jnp.int32, (ts, ts), 0)
    col = lax.broadcasted_iota(jnp.int32, (ts, ts), 1)
    tril = (row >= col).astype(F32)
    c = _dot(tril, log_f, HI) + carry_ref[...]
    carry_ref[...] = c[ts - 1:ts, :]
    c_ref[0] = c
    ct_ref[0] = c.T[8:16, :]
    gexp_ref[0] = jnp.concatenate(
        [jnp.broadcast_to(g[:, h:h + 1], (ts, LANES)) for h in range(GDN_HEADS)], axis=1)
    bexp_ref[0] = jnp.concatenate(
        [jnp.broadcast_to(beta[:, GDN_HEADS + h:GDN_HEADS + h + 1], (ts, LANES)) for h in range(GDN_HEADS)], axis=1)


def _prep(x, w_small, params):
    b, s, d = x.shape
    ts = min(512, s)
    hw = GDN_HEADS * LANES
    return pl.pallas_call(
        _prep_kernel,
        grid=(b, s // ts),
        in_specs=[pl.BlockSpec((1, ts, d), lambda i, j: (i, j, 0)),
                  pl.BlockSpec((d, LANES), lambda i, j: (0, 0)),
                  pl.BlockSpec((8, LANES), lambda i, j: (0, 0))],
        out_specs=[pl.BlockSpec((1, ts, hw), lambda i, j: (i, j, 0)),
                   pl.BlockSpec((1, ts, hw), lambda i, j: (i, j, 0)),
                   pl.BlockSpec((1, ts, LANES), lambda i, j: (i, j, 0)),
                   pl.BlockSpec((1, 8, ts), lambda i, j: (i, 0, j))],
        out_shape=[jax.ShapeDtypeStruct((b, s, hw), F32),
                   jax.ShapeDtypeStruct((b, s, hw), F32),
                   jax.ShapeDtypeStruct((b, s, LANES), F32),
                   jax.ShapeDtypeStruct((b, 8, s), F32)],
        scratch_shapes=[pltpu.VMEM((1, LANES), F32)],
        compiler_params=pltpu.CompilerParams(
            dimension_semantics=("parallel", "arbitrary"), vmem_limit_bytes=VMEM_LIMIT),
        name="prep",
    )(x, w_small, params)


def _each(fn, *lists):
    return [fn(*args) for args in zip(*lists)]


def _unit_lower_inverse(ms, masks):
    eye, blk16, blk32 = masks
    hi = lambda a, b: _dot(a, b, HI)
    n1 = _each(lambda m: -jnp.where(blk16, m, 0.0), ms)
    l1 = _each(lambda m: jnp.where(blk32 & jnp.logical_not(blk16), m, 0.0), ms)
    l2 = _each(lambda m: jnp.where(blk32, 0.0, m), ms)
    n2 = _each(hi, n1, n1)
    p = _each(lambda a, b: hi(eye + a, eye + b), n1, n2)
    n4 = _each(hi, n2, n2)
    p = _each(lambda a, b: hi(a, eye + b), p, n4)
    n8 = _each(hi, n4, n4)
    d_inv = _each(lambda a, b: hi(a, eye + b), p, n8)
    dl = _each(hi, d_inv, l1)
    a32 = _each(lambda a, b: a - hi(b, a), d_inv, dl)
    al = _each(hi, a32, l2)
    return _each(lambda a, b: a - hi(b, a), a32, al)


def _gdn_kernel(q_ref, k_ref, v_ref, z_ref, g_ref, b_ref, cwq_ref, cwk_ref, cwv_ref, nw_ref,
                o_ref, qn, kn, vn, st, sol_s, qk_s, qg_s, kd_s, gl_s):
    s = q_ref.shape[1]
    c = CHUNK
    nh = q_ref.shape[2] // LANES
    row = lax.broadcasted_iota(jnp.int32, (s, LANES), 0)

    def conv_silu(x, w):
        y = x * w[3:4, :]
        for sh in (1, 2, 3):
            xs = jnp.where(row >= sh, pltpu.roll(x, sh, axis=0), 0.0)
            y = y + xs * w[3 - sh:4 - sh, :]
        return y * jax.nn.sigmoid(y)

    def l2norm(x):
        return x * lax.rsqrt(jnp.sum(x * x, axis=-1, keepdims=True) + 1e-6)

    for j in range(nh):
        hs = slice(j * LANES, (j + 1) * LANES)
        qn[:, hs] = l2norm(conv_silu(q_ref[0, :, hs], cwq_ref[:, hs])) * (GDN_DK ** -0.5)
        kn[:, hs] = l2norm(conv_silu(k_ref[0, :, hs], cwk_ref[:, hs]))
        vn[:, hs] = conv_silu(v_ref[0, :, hs], cwv_ref[:, hs])
    st[...] = jnp.zeros_like(st)

    ri = lax.broadcasted_iota(jnp.int32, (c, c), 0)
    ci = lax.broadcasted_iota(jnp.int32, (c, c), 1)
    tril = ri >= ci
    strict = ri > ci
    t_inc = tril.astype(F32)
    eye = (ri == ci).astype(F32)
    blk16 = (ri >> 4) == (ci >> 4)
    blk32 = (ri >> 5) == (ci >> 5)
    l2 = lax.broadcasted_iota(jnp.int32, (c, 2 * LANES), 0)
    j2 = lax.broadcasted_iota(jnp.int32, (c, 2 * LANES), 1)
    ux = jnp.where((j2 >= c) | (l2 > j2), 1.0, 0.0).astype(F32)
    nw = nw_ref[...]

    heads = list(range(nh))
    lanes_of = [slice(j * LANES, (j + 1) * LANES) for j in heads]

    def local_stage(ns):
        pairs = [(i, j) for i in range(len(ns)) for j in heads]
        rows = [pl.ds(pl.multiple_of(n * c, c), c) for n in ns]
        q = [qn[rows[i], lanes_of[j]] for i, j in pairs]
        k = [kn[rows[i], lanes_of[j]] for i, j in pairs]
        v = [vn[rows[i], lanes_of[j]] for i, j in pairs]
        gb = [g_ref[0, rows[i], lanes_of[j]] for i, j in pairs]
        bb = [b_ref[0, rows[i], lanes_of[j]] for i, j in pairs]
        d = _each(lambda g: _dot(t_inc, jnp.concatenate([g, g], axis=1) * ux, HI), gb)
        kb = _each(lambda a, b: a * b, k, bb)
        kk = _each(lambda a, b: _dot_nt(a, b, HI), kb, k)
        qk = _each(lambda a, b: _dot_nt(a.astype(BF16), b.astype(BF16)), q, k)
        gc = [x[:, LANES:] for x in d]
        decay = [jnp.where(tril, jnp.exp(x[:, :c]), 0.0) for x in d]
        m = _each(lambda a, b: jnp.where(strict, a * b, 0.0), kk, decay)
        a_inv = _unit_lower_inverse(m, (eye, blk16, blk32))
        egc = _each(jnp.exp, gc)
        rhs = _each(lambda vv, b, kbb, e: jnp.concatenate([vv * b, kbb * e], axis=1), v, bb, kb, egc)
        sol = _each(lambda a, r: _dot(a, r, HI), a_inv, rhs)
        gl = [x[c - 1:c, :] for x in gc]
        for p, (i, j) in enumerate(pairs):
            n = ns[i]
            qk_s[n, j] = (qk[p] * decay[p]).astype(BF16)
            qg_s[n, j] = (q[p] * egc[p]).astype(BF16)
            kd_s[n, j] = (k[p] * jnp.exp(gl[p] - gc[p])).astype(BF16)
            gl_s[n, j] = gl[p]
            sol_s[n, j] = sol[p]

    def state_stage(n):
        rows = pl.ds(pl.multiple_of(n * c, c), c)
        state = [st[j] for j in heads]
        state_b = [x.astype(BF16) for x in state]
        v_new = [sol_s[n, j, :, :LANES] - _dot(sol_s[n, j, :, LANES:].astype(BF16), state_b[j])
                 for j in heads]
        v_new_b = [x.astype(BF16) for x in v_new]
        o = [_dot(qg_s[n, j], state_b[j]) + _dot(qk_s[n, j], v_new_b[j]) for j in heads]
        new_state = [state[j] * jnp.exp(gl_s[n, j]) + _dot_tn(kd_s[n, j], v_new_b[j]) for j in heads]
        for j in heads:
            st[j] = new_state[j]
            on = o[j] * lax.rsqrt(jnp.mean(o[j] * o[j], axis=-1, keepdims=True) + 1e-6) * nw
            z = z_ref[0, rows, lanes_of[j]]
            o_ref[0, rows, lanes_of[j]] = on * (z * jax.nn.sigmoid(z))

    nchunks = s // c
    group = 4

    def local_group(gidx, carry):
        local_stage([gidx * group + i for i in range(group)])
        return carry

    def state_chunk(n, carry):
        state_stage(n)
        return carry

    lax.fori_loop(0, nchunks // group, local_group, 0)
    lax.fori_loop(0, nchunks, state_chunk, 0)


def _gdn(proj3, gexp, bexp, conv_w, norm_w, cols):
    b, s, _ = proj3.shape
    nh = 2
    nc = s // CHUNK
    wd = nh * LANES
    cq, ck, cv, cz = (c0 // nh for c0 in cols)

    def blk(c0):
        return pl.BlockSpec((1, s, wd), lambda i, h: (i, 0, c0 + h))

    def cw(c0):
        return pl.BlockSpec((conv_w.shape[0], wd), lambda i, h: (0, c0 + h))

    head = pl.BlockSpec((1, s, wd), lambda i, h: (i, 0, h))
    return pl.pallas_call(
        _gdn_kernel,
        grid=(b, GDN_HEADS // nh),
        in_specs=[blk(cq), blk(ck), blk(cv), blk(cz), head, head,
                  cw(0), cw(GDN_HEADS // nh), cw(2 * GDN_HEADS // nh),
                  pl.BlockSpec((1, LANES), lambda i, h: (0, 0))],
        out_specs=head,
        out_shape=jax.ShapeDtypeStruct((b, s, GDN_HEADS * LANES), F32),
        scratch_shapes=[pltpu.VMEM((s, wd), F32), pltpu.VMEM((s, wd), F32),
                        pltpu.VMEM((s, wd), F32), pltpu.VMEM((nh, GDN_DK, LANES), F32),
                        pltpu.VMEM((nc, nh, CHUNK, 2 * LANES), F32), pltpu.VMEM((nc, nh, CHUNK, CHUNK), BF16),
                        pltpu.VMEM((nc, nh, CHUNK, LANES), BF16), pltpu.VMEM((nc, nh, CHUNK, LANES), BF16),
                        pltpu.VMEM((nc, nh, 1, LANES), F32)],
        compiler_params=pltpu.CompilerParams(
            dimension_semantics=("parallel", "parallel"), vmem_limit_bytes=VMEM_LIMIT),
        name="gdn",
    )(proj3, proj3, proj3, proj3, gexp, bexp, conv_w, conv_w, conv_w, norm_w)


def _fox_kernel(q_ref, k_ref, v_ref, c_ref, ct_ref, o_ref, *, tk):
    tq = q_ref.shape[1]
    nj = q_ref.shape[2] // LANES
    g = pl.program_id(1)
    qi = pl.program_id(2)
    q = q_ref[0]
    cblk = c_ref[0]
    lane = lax.broadcasted_iota(jnp.int32, (tq, LANES), 1)
    qpos = qi * tq + lax.broadcasted_iota(jnp.int32, (tq, tk), 0)
    kofs = lax.broadcasted_iota(jnp.int32, (tq, tk), 1)
    heads = list(range(nj))
    lanes_of = [slice(j * LANES, (j + 1) * LANES) for j in heads]
    ccol = [jnp.sum(jnp.where(lane == 8 + g * nj + j, cblk, 0.0), axis=-1, keepdims=True) for j in heads]
    qh = [q[:, hs] * jnp.asarray(FX_DH ** -0.5, BF16) for hs in lanes_of]

    def body(kv, carry):
        k0 = pl.multiple_of(kv * tk, tk)
        causal = qpos >= k0 + kofs
        kblk = k_ref[0, pl.ds(k0, tk), :]
        vblk = v_ref[0, pl.ds(k0, tk), :]
        sc = [_dot_nt(qh[j], kblk[:, lanes_of[j]]) for j in heads]
        crow = [ct_ref[0, pl.ds(g * nj + j, 1), pl.ds(k0, tk)] for j in heads]
        sc = [jnp.where(causal, sc[j] + ccol[j] - crow[j], -1e30) for j in heads]
        m_new = [jnp.maximum(carry[j][0], jnp.max(sc[j], axis=-1, keepdims=True)) for j in heads]
        a = [jnp.exp(carry[j][0] - m_new[j]) for j in heads]
        p = [jnp.exp(sc[j] - m_new[j]) for j in heads]
        l = [a[j] * carry[j][1] + jnp.sum(p[j], axis=-1, keepdims=True) for j in heads]
        acc = [a[j] * carry[j][2] + _dot(p[j].astype(BF16), vblk[:, lanes_of[j]]) for j in heads]
        return tuple((m_new[j], l[j], acc[j]) for j in heads)

    init = tuple((jnp.full((tq, 1), -1e30, F32), jnp.zeros((tq, 1), F32), jnp.zeros((tq, LANES), F32))
                 for _ in heads)
    nkv = (qi * tq + tq - 1) // tk + 1
    res = lax.fori_loop(0, nkv, body, init)
    o_ref[0] = jnp.concatenate([acc / l for _, l, acc in res], axis=1)


def _fox(pf3, c, ct):
    b, s, _ = pf3.shape
    wd = 4 * LANES
    tq = min(128, s)
    tk = min(256, s)
    ngrp = FX_HEADS * LANES // wd
    return pl.pallas_call(
        functools.partial(_fox_kernel, tk=tk),
        grid=(b, ngrp, s // tq),
        in_specs=[pl.BlockSpec((1, tq, wd), lambda i, h, t: (i, t, h)),
                  pl.BlockSpec((1, s, wd), lambda i, h, t: (i, 0, ngrp + h)),
                  pl.BlockSpec((1, s, wd), lambda i, h, t: (i, 0, 2 * ngrp + h)),
                  pl.BlockSpec((1, tq, LANES), lambda i, h, t: (i, t, 0)),
                  pl.BlockSpec((1, 8, s), lambda i, h, t: (i, 0, 0))],
        out_specs=pl.BlockSpec((1, tq, wd), lambda i, h, t: (i, t, h)),
        out_shape=jax.ShapeDtypeStruct((b, s, ngrp * wd), F32),
        compiler_params=pltpu.CompilerParams(
            dimension_semantics=("parallel", "parallel", "parallel"), vmem_limit_bytes=VMEM_LIMIT),
        name="fox",
    )(pf3, pf3, pf3, c, ct)


def _mix_kernel(oa_ref, ob_ref, ga_ref, gb_ref, x_ref, wa_ref, wb_ref, wo_ref, g1_ref, b1_ref, o_ref, *, alpha):
    ya = _dot(oa_ref[...].astype(BF16), wa_ref[...])
    yb = _dot(ob_ref[...].astype(BF16), wb_ref[...])
    mix = jax.nn.sigmoid(ga_ref[...]) * ya + jax.nn.sigmoid(gb_ref[...]) * yb
    z = alpha * x_ref[...] + _dot(mix.astype(BF16), wo_ref[...])
    o_ref[...] = _layernorm(z, g1_ref[...], b1_ref[...])


def _mix(oa, ob, proj, x2, wa, wb, wo, g1, b1, alpha):
    t, d = x2.shape
    tm = min(512, t)
    w, w2 = oa.shape[1], ob.shape[1]
    full = lambda r, c: pl.BlockSpec((r, c), lambda i: (0, 0))
    return pl.pallas_call(
        functools.partial(_mix_kernel, alpha=alpha),
        grid=(t // tm,),
        in_specs=[pl.BlockSpec((tm, w), lambda i: (i, 0)),
                  pl.BlockSpec((tm, w2), lambda i: (i, 0)),
                  pl.BlockSpec((tm, d), lambda i: (i, 0)),
                  pl.BlockSpec((tm, d), lambda i: (i, 1)),
                  pl.BlockSpec((tm, d), lambda i: (i, 0)),
                  full(w, d), full(w2, d), full(d, d), full(1, d), full(1, d)],
        out_specs=pl.BlockSpec((tm, d), lambda i: (i, 0)),
        out_shape=jax.ShapeDtypeStruct((t, d), F32),
        compiler_params=pltpu.CompilerParams(
            dimension_semantics=("parallel",), vmem_limit_bytes=VMEM_LIMIT),
        name="mix",
    )(oa, ob, proj, proj, x2, wa, wb, wo, g1, b1)


def _route_kernel(h_ref, wq_ref, keys_ref, flat_ref, cmask_ref, e_ref, g_ref,
                  q_scr, s_scr, i_scr, best_scr, eh_scr, eall_scr):
    tm = h_ref.shape[0]
    nk = PEER_NKEYS
    kk = PEER_TOPK
    q = _dot(h_ref[...].astype(BF16), wq_ref[...])
    for j in range(2 * PEER_HEADS):
        q_scr[j] = q[:, j * LANES:(j + 1) * LANES].astype(BF16)
    iota_k = lax.broadcasted_iota(jnp.int32, (nk, LANES), 0)
    flat = flat_ref[...]
    cmask = cmask_ref[...]
    neg = jnp.float32(-jnp.inf)

    def route_tokens(hd, lh):
        tok = slice(lh * LANES, (lh + 1) * LANES)
        for p in range(2):
            vals = _dot_nt(keys_ref[hd * 2 + p], q_scr[hd * 2 + p, tok, :])
            for r in range(kk):
                m = jnp.max(vals, axis=0, keepdims=True)
                am = jnp.min(jnp.where(vals == m, iota_k, nk), axis=0, keepdims=True)
                s_scr[p * kk + r:p * kk + r + 1, :] = m
                i_scr[p * kk + r:p * kk + r + 1, :] = am
                vals = jnp.where(iota_k == am, neg, vals)
        s1 = s_scr[kk:kk + 8, :]
        i1 = i_scr[kk:kk + 8, :]
        cand = [s_scr[0:1, :] + s_scr[kk:2 * kk, :]]
        cidx = [i_scr[0:1, :] * nk + i_scr[kk:2 * kk, :]]
        for a in range(1, 8):
            cand.append(s_scr[a:a + 1, :] + s1)
            cidx.append(i_scr[a:a + 1, :] * nk + i1)
        cand.append(s_scr[8:kk, :] + s_scr[kk:kk + 1, :])
        cidx.append(i_scr[8:kk, :] * nk + i_scr[kk:kk + 1, :])
        vals = jnp.concatenate(cand, axis=0) + cmask
        cidx = jnp.concatenate(cidx, axis=0)
        for r in range(kk):
            m = jnp.max(vals, axis=0, keepdims=True)
            am = jnp.min(jnp.where(vals == m, flat, 2 * kk * kk), axis=0, keepdims=True)
            sel = flat == am
            best_scr[r:r + 1, :] = m
            eh_scr[r:r + 1, :] = jnp.max(jnp.where(sel, cidx, -1), axis=0, keepdims=True)
            vals = jnp.where(sel, neg, vals)
        bs = best_scr[...]
        ex = jnp.exp(bs - bs[0:1, :])
        r0 = pl.multiple_of(hd * kk, kk)
        g_ref[pl.ds(r0, kk), tok] = ex / jnp.sum(ex, axis=0, keepdims=True)
        eall_scr[pl.ds(r0, kk), tok] = eh_scr[...]

    def head(hd, carry):
        for lh in range(tm // LANES):
            route_tokens(hd, lh)
        return carry

    lax.fori_loop(0, PEER_HEADS, head, 0)
    e_ref[...] = eall_scr[...].T


def _route(h1, wq, keys):
    t, d = h1.shape
    tm = min(256, t)
    nq = wq.shape[1]
    hk = PEER_HEADS * PEER_TOPK
    kk = PEER_TOPK
    pairs = [(0, bb) for bb in range(kk)]
    for a in range(1, 8):
        pairs += [(a, bb) for bb in range(8)]
    pairs += [(a, 0) for a in range(8, kk)]
    real = [(a + 1) * (bb + 1) <= kk for a, bb in pairs]
    flat = [a * kk + bb if ok else kk * kk + r for r, ((a, bb), ok) in enumerate(zip(pairs, real))]
    flat = jnp.broadcast_to(jnp.asarray(flat, jnp.int32)[:, None], (len(pairs), LANES))
    cmask = jnp.broadcast_to(jnp.asarray([0.0 if ok else -jnp.inf for ok in real], F32)[:, None],
                             (len(pairs), LANES))
    return pl.pallas_call(
        _route_kernel,
        grid=(t // tm,),
        in_specs=[pl.BlockSpec((tm, d), lambda i: (i, 0)),
                  pl.BlockSpec((d, nq), lambda i: (0, 0)),
                  pl.BlockSpec(keys.shape, lambda i: (0, 0, 0)),
                  pl.BlockSpec(flat.shape, lambda i: (0, 0)),
                  pl.BlockSpec(cmask.shape, lambda i: (0, 0))],
        out_specs=[pl.BlockSpec((tm, hk), lambda i: (i, 0)),
                   pl.BlockSpec((hk, tm), lambda i: (0, i))],
        out_shape=[jax.ShapeDtypeStruct((t, hk), jnp.int32),
                   jax.ShapeDtypeStruct((hk, t), F32)],
        scratch_shapes=[pltpu.VMEM((2 * PEER_HEADS, tm, LANES), BF16),
                        pltpu.VMEM((2 * kk, LANES), F32), pltpu.VMEM((2 * kk, LANES), jnp.int32),
                        pltpu.VMEM((kk, LANES), F32), pltpu.VMEM((kk, LANES), jnp.int32),
                        pltpu.VMEM((hk, tm), jnp.int32)],
        compiler_params=pltpu.CompilerParams(
            dimension_semantics=("parallel",), vmem_limit_bytes=VMEM_LIMIT),
        name="route",
    )(h1, wq, keys, flat, cmask)


def _peer_kernel(ids_ref, idn_ref, h_ref, gt_ref, uv_hbm, g2_ref, b2_ref, o_ref,
                 buf_a, buf_b, sem, y_scr, *, tt, alpha):
    i = pl.program_id(0)
    n = pl.num_programs(0)
    hk = ids_ref.shape[1]
    nslab, sub = uv_hbm.shape[1], uv_hbm.shape[2]
    half = sub // 2
    bufs = (buf_a, buf_b)

    def issue(idref, row0, slot):
        for t in range(tt):
            for k in range(hk):
                e = idref[row0 + t, k]
                pltpu.make_async_copy(uv_hbm.at[e], bufs[slot].at[:, pl.ds((t * hk + k) * sub, sub), :],
                                      sem.at[slot]).start(priority=k % 2)

    def wait(slot):
        pltpu.make_async_copy(bufs[slot], bufs[slot], sem.at[slot]).wait()

    gt = gt_ref[...]
    lane = lax.broadcasted_iota(jnp.int32, gt.shape, 1)
    tok0 = (i * 2 * tt) % LANES

    def compute(slot):
        buf = bufs[slot]
        for t in range(tt):
            row = slot * tt + t

            def rows(c, j):
                return buf[c, pl.ds(t * hk * sub + j, hk, stride=sub), :]

            part = None
            for j in range(half):
                for c in range(nslab):
                    seg = j * nslab + c
                    term = rows(c, j) * h_ref[row:row + 1, seg * LANES:(seg + 1) * LANES]
                    part = term if part is None else part + term
            pre = jnp.sum(part, axis=-1, keepdims=True)
            gate = jnp.sum(jnp.where(lane == tok0 + row, gt, 0.0), axis=-1, keepdims=True)
            act = 0.5 * pre * (1.0 + lax.erf(pre * (2.0 ** -0.5))) * gate
            for j in range(half):
                for c in range(nslab):
                    seg = j * nslab + c
                    y_scr[row:row + 1, seg * LANES:(seg + 1) * LANES] = jnp.sum(
                        act * rows(c, half + j), axis=0, keepdims=True)

    @pl.when(i == 0)
    def _():
        issue(ids_ref, 0, 0)

    wait(0)
    issue(ids_ref, tt, 1)
    compute(0)
    wait(1)
    issue(idn_ref, 0, 0)
    compute(1)

    @pl.when(i == n - 1)
    def _():
        wait(0)

    z = alpha * h_ref[...] + y_scr[...]
    o_ref[...] = _layernorm(z, g2_ref[...], b2_ref[...])


def _peer(ids, gt, h1, uv_tab, g2, b2, alpha):
    t, d = h1.shape
    hk = ids.shape[1]
    tt = 8
    nslab, sub = uv_tab.shape[1], uv_tab.shape[2]
    nsteps = t // (2 * tt)
    steps_per_lane_blk = LANES // (2 * tt)
    return pl.pallas_call(
        functools.partial(_peer_kernel, tt=tt, alpha=alpha),
        grid=(nsteps,),
        in_specs=[pl.BlockSpec((2 * tt, hk), lambda i: (i, 0), memory_space=pltpu.SMEM),
                  pl.BlockSpec((2 * tt, hk), lambda i: (jnp.minimum(i + 1, nsteps - 1), 0), memory_space=pltpu.SMEM),
                  pl.BlockSpec((2 * tt, d), lambda i: (i, 0)),
                  pl.BlockSpec((hk, LANES), lambda i: (0, i // steps_per_lane_blk)),
                  pl.BlockSpec(memory_space=pl.ANY),
                  pl.BlockSpec((1, d), lambda i: (0, 0)),
                  pl.BlockSpec((1, d), lambda i: (0, 0))],
        out_specs=pl.BlockSpec((2 * tt, d), lambda i: (i, 0)),
        out_shape=jax.ShapeDtypeStruct((t, d), F32),
        scratch_shapes=[pltpu.VMEM((nslab, tt * hk * sub, LANES), F32),
                        pltpu.VMEM((nslab, tt * hk * sub, LANES), F32),
                        pltpu.SemaphoreType.DMA((2,)),
                        pltpu.VMEM((2 * tt, d), F32)],
        compiler_params=pltpu.CompilerParams(
            dimension_semantics=("arbitrary",), vmem_limit_bytes=VMEM_LIMIT),
        name="peer",
    )(ids, ids, h1, gt, uv_tab, g2, b2)


def _layer(h, w_in, conv_w, a_log, dt_bias, norm_w, f_bias, w_out_gdn, w_out_fox, w_o, ln1_g, ln1_b,
           peer_wq, peer_keys, peer_u, peer_v, ln2_g, ln2_b, alpha):
    b, s, d = h.shape
    t = b * s
    qk = GDN_HEADS * GDN_DK
    fw = FX_HEADS * FX_DH
    o_gz = 4 * qk
    o_ga = o_gz
    o_fq = o_ga + 2 * GDN_HEADS
    o_ff = o_fq + 3 * fw
    o_gate = o_ff + FX_HEADS
    w_big = jnp.concatenate([w_in[:, o_gate:], w_in[:, :o_gz]], axis=1).astype(BF16)
    pad = LANES - FX_DH
    w_fox = jnp.pad(w_in[:, o_fq:o_ff].reshape(d, 3 * FX_HEADS, FX_DH), ((0, 0), (0, 0), (0, pad)))
    w_fox = w_fox.reshape(d, 3 * FX_HEADS * LANES).astype(BF16)
    w_out_fox_p = jnp.pad(w_out_fox.reshape(FX_HEADS, FX_DH, d), ((0, 0), (0, pad), (0, 0)))
    w_out_fox_p = w_out_fox_p.reshape(FX_HEADS * LANES, d).astype(BF16)
    n_small = 2 * GDN_HEADS + FX_HEADS
    w_small = jnp.concatenate([w_in[:, o_ga:o_fq], w_in[:, o_ff:o_gate],
                               jnp.zeros((d, LANES - n_small), F32)], axis=1)
    params = jnp.zeros((8, LANES), F32)
    params = params.at[0, :GDN_HEADS].set(a_log).at[1, :GDN_HEADS].set(dt_bias)
    params = params.at[2, 2 * GDN_HEADS:n_small].set(f_bias)

    x2 = h.reshape(t, d)
    proj = _in_proj(x2, w_big, F32, "in_proj")
    pf = _in_proj(x2, w_fox, BF16, "in_proj_fox")
    gexp, bexp, c, ct = _prep(h, w_small, params)
    proj3 = proj.reshape(b, s, proj.shape[1])
    gdn0 = 2 * d // LANES
    oa = _gdn(proj3, gexp, bexp, conv_w, norm_w.reshape(1, LANES),
              (gdn0, gdn0 + GDN_HEADS, gdn0 + 2 * GDN_HEADS, gdn0 + 3 * GDN_HEADS))
    ob = _fox(pf.reshape(b, s, pf.shape[1]), c, ct)
    h1 = _mix(oa.reshape(t, qk), ob.reshape(t, FX_HEADS * LANES), proj, x2,
              w_out_gdn.astype(BF16), w_out_fox_p, w_o.astype(BF16),
              ln1_g.reshape(1, d), ln1_b.reshape(1, d), alpha)
    keys = peer_keys.reshape(2 * PEER_HEADS, PEER_NKEYS, peer_keys.shape[-1]).astype(BF16)
    ids, gt = _route(h1, peer_wq.astype(BF16), keys)
    ne = peer_u.shape[0]
    half = PEER_SUB // 2
    nslab = d // (half * LANES)
    uv_tab = jnp.concatenate([peer_u.reshape(ne, half, nslab, LANES),
                              peer_v.reshape(ne, half, nslab, LANES)], axis=1).transpose(0, 2, 1, 3)
    out = _peer(ids, gt, h1, uv_tab, ln2_g.reshape(1, d), ln2_b.reshape(1, d), alpha)
    return out.reshape(b, s, d)


def kernel(x, w_in, gdn_conv_w, gdn_a_log, gdn_dt_bias, gdn_norm_w, fox_f_bias, w_out_gdn, w_out_fox, w_o,
           ln1_g, ln1_b, peer_wq, peer_keys, peer_u, peer_v, ln2_g, ln2_b):
    depth = w_in.shape[0]
    alpha = (2.0 * depth) ** 0.25
    h = x
    for l in range(depth):
        h = _layer(h, w_in[l], gdn_conv_w[l], gdn_a_log[l], gdn_dt_bias[l], gdn_norm_w[l], fox_f_bias[l],
                   w_out_gdn[l], w_out_fox[l], w_o[l], ln1_g[l], ln1_b[l], peer_wq[l], peer_keys[l],
                   peer_u[l], peer_v[l], ln2_g[l], ln2_b[l], alpha)
    return h
```

```python
import functools

import jax
import jax.numpy as jnp
from jax import lax
from jax.experimental import pallas as pl
from jax.experimental.pallas import tpu as pltpu

F32 = jnp.float32
BF16 = jnp.bfloat16
HI = lax.Precision.HIGHEST

LANES = 128
CHUNK = 64
GDN_HEADS = 4
GDN_DK = 128
FX_HEADS = 8
FX_DH = 64
PEER_HEADS = 8
PEER_NKEYS = 128
PEER_TOPK = 16
PEER_SUB = 4
LN_EPS = 1e-5
VMEM_LIMIT = 48 * 1024 * 1024


def _dot(a, b, prec=None):
    return jnp.dot(a, b, preferred_element_type=F32, precision=prec)


def _dot_nt(a, b, prec=None):
    return lax.dot_general(a, b, (((1,), (1,)), ((), ())), preferred_element_type=F32, precision=prec)


def _dot_tn(a, b, prec=None):
    return lax.dot_general(a, b, (((0,), (0,)), ((), ())), preferred_element_type=F32, precision=prec)


def _softplus(x):
    return jnp.maximum(x, 0.0) + jnp.log1p(jnp.exp(-jnp.abs(x)))


def _layernorm(z, g, b):
    mu = jnp.mean(z, axis=-1, keepdims=True)
    zc = z - mu
    var = jnp.mean(zc * zc, axis=-1, keepdims=True)
    return zc * lax.rsqrt(var + LN_EPS) * g + b


def _mm_kernel(x_ref, w_ref, o_ref):
    o_ref[...] = _dot(x_ref[...].astype(BF16), w_ref[...]).astype(o_ref.dtype)


def _in_proj(x2, w_big, out_dtype, name):
    t, d = x2.shape
    n = w_big.shape[1]
    tm = min(1024, t)
    tn = 512
    return pl.pallas_call(
        _mm_kernel,
        grid=(t // tm, n // tn),
        in_specs=[pl.BlockSpec((tm, d), lambda i, j: (i, 0)),
                  pl.BlockSpec((d, tn), lambda i, j: (0, j))],
        out_specs=pl.BlockSpec((tm, tn), lambda i, j: (i, j)),
        out_shape=jax.ShapeDtypeStruct((t, n), out_dtype),
        compiler_params=pltpu.CompilerParams(
            dimension_semantics=("parallel", "parallel"), vmem_limit_bytes=VMEM_LIMIT),
        name=name,
    )(x2, w_big)


def _prep_kernel(x_ref, w_ref, par_ref, gexp_ref, bexp_ref, c_ref, ct_ref, carry_ref):
    ts = x_ref.shape[1]

    @pl.when(pl.program_id(1) == 0)
    def _():
        carry_ref[...] = jnp.zeros_like(carry_ref)

    small = _dot(x_ref[0], w_ref[...], HI)
    a_log = par_ref[0:1, :]
    dt_bias = par_ref[1:2, :]
    f_bias = par_ref[2:3, :]
    g = -jnp.exp(a_log) * _softplus(small + dt_bias)
    beta = jax.nn.sigmoid(small)
    lane = lax.broadcasted_iota(jnp.int32, (ts, LANES), 1)
    log_f = jnp.where((lane >= 8) & (lane < 16), -_softplus(-(small + f_bias)), 0.0)
    row = lax.broadcasted_iota(jnp.int32, (ts, ts), 0)
    col = lax.broadcasted_iota(jnp.int32, (ts, ts), 1)
    tril = (row >= col).astype(F32)
    c = _dot(tril, log_f, HI) + carry_ref[...]
    carry_ref[...] = c[ts - 1:ts, :]
    c_ref[0] = c
    ct_ref[0] = c.T[8:16, :]
    gexp_ref[0] = jnp.concatenate(
        [jnp.broadcast_to(g[:, h:h + 1], (ts, LANES)) for h in range(GDN_HEADS)], axis=1)
    bexp_ref[0] = jnp.concatenate(
        [jnp.broadcast_to(beta[:, GDN_HEADS + h:GDN_HEADS + h + 1], (ts, LANES)) for h in range(GDN_HEADS)], axis=1)


def _prep(x, w_small, params):
    b, s, d = x.shape
    ts = min(512, s)
    hw = GDN_HEADS * LANES
    return pl.pallas_call(
        _prep_kernel,
        grid=(b, s // ts),
        in_specs=[pl.BlockSpec((1, ts, d), lambda i, j: (i, j, 0)),
                  pl.BlockSpec((d, LANES), lambda i, j: (0, 0)),
                  pl.BlockSpec((8, LANES), lambda i, j: (0, 0))],
        out_specs=[pl.BlockSpec((1, ts, hw), lambda i, j: (i, j, 0)),
                   pl.BlockSpec((1, ts, hw), lambda i, j: (i, j, 0)),
                   pl.BlockSpec((1, ts, LANES), lambda i, j: (i, j, 0)),
                   pl.BlockSpec((1, 8, ts), lambda i, j: (i, 0, j))],
        out_shape=[jax.ShapeDtypeStruct((b, s, hw), F32),
                   jax.ShapeDtypeStruct((b, s, hw), F32),
                   jax.ShapeDtypeStruct((b, s, LANES), F32),
                   jax.ShapeDtypeStruct((b, 8, s), F32)],
        scratch_shapes=[pltpu.VMEM((1, LANES), F32)],
        compiler_params=pltpu.CompilerParams(
            dimension_semantics=("parallel", "arbitrary"), vmem_limit_bytes=VMEM_LIMIT),
        name="prep",
    )(x, w_small, params)


def _each(fn, *lists):
    return [fn(*args) for args in zip(*lists)]


def _unit_lower_inverse(ms, masks):
    eye, blk16, blk32 = masks
    hi = lambda a, b: _dot(a, b, HI)
    n1 = _each(lambda m: -jnp.where(blk16, m, 0.0), ms)
    l1 = _each(lambda m: jnp.where(blk32 & jnp.logical_not(blk16), m, 0.0), ms)
    l2 = _each(lambda m: jnp.where(blk32, 0.0, m), ms)
    n2 = _each(hi, n1, n1)
    p = _each(lambda a, b: hi(eye + a, eye + b), n1, n2)
    n4 = _each(hi, n2, n2)
    p = _each(lambda a, b: hi(a, eye + b), p, n4)
    n8 = _each(hi, n4, n4)
    d_inv = _each(lambda a, b: hi(a, eye + b), p, n8)
    dl = _each(hi, d_inv, l1)
    a32 = _each(lambda a, b: a - hi(b, a), d_inv, dl)
    al = _each(hi, a32, l2)
    return _each(lambda a, b: a - hi(b, a), a32, al)


def _gdn_kernel(q_ref, k_ref, v_ref, z_ref, g_ref, b_ref, cwq_ref, cwk_ref, cwv_ref, nw_ref,
                o_ref, qn, kn, vn, st, sol_s, qk_s, qg_s, kd_s, gl_s):
    s = q_ref.shape[1]
    c = CHUNK
    nh = q_ref.shape[2] // LANES
    row = lax.broadcasted_iota(jnp.int32, (s, LANES), 0)

    def conv_silu(x, w):
        y = x * w[3:4, :]
        for sh in (1, 2, 3):
            xs = jnp.where(row >= sh, pltpu.roll(x, sh, axis=0), 0.0)
            y = y + xs * w[3 - sh:4 - sh, :]
        return y * jax.nn.sigmoid(y)

    def l2norm(x):
        return x * lax.rsqrt(jnp.sum(x * x, axis=-1, keepdims=True) + 1e-6)

    for j in range(nh):
        hs = slice(j * LANES, (j + 1) * LANES)
        qn[:, hs] = l2norm(conv_silu(q_ref[0, :, hs], cwq_ref[:, hs])) * (GDN_DK ** -0.5)
        kn[:, hs] = l2norm(conv_silu(k_ref[0, :, hs], cwk_ref[:, hs]))
        vn[:, hs] = conv_silu(v_ref[0, :, hs], cwv_ref[:, hs])
    st[...] = jnp.zeros_like(st)

    ri = lax.broadcasted_iota(jnp.int32, (c, c), 0)
    ci = lax.broadcasted_iota(jnp.int32, (c, c), 1)
    tril = ri >= ci
    strict = ri > ci
    t_inc = tril.astype(F32)
    eye = (ri == ci).astype(F32)
    blk16 = (ri >> 4) == (ci >> 4)
    blk32 = (ri >> 5) == (ci >> 5)
    l2 = lax.broadcasted_iota(jnp.int32, (c, 2 * LANES), 0)
    j2 = lax.broadcasted_iota(jnp.int32, (c, 2 * LANES), 1)
    ux = jnp.where((j2 >= c) | (l2 > j2), 1.0, 0.0).astype(F32)
    nw = nw_ref[...]

    heads = list(range(nh))
    lanes_of = [slice(j * LANES, (j + 1) * LANES) for j in heads]

    def local_stage(ns):
        pairs = [(i, j) for i in range(len(ns)) for j in heads]
        rows = [pl.ds(pl.multiple_of(n * c, c), c) for n in ns]
        q = [qn[rows[i], lanes_of[j]] for i, j in pairs]
        k = [kn[rows[i], lanes_of[j]] for i, j in pairs]
        v = [vn[rows[i], lanes_of[j]] for i, j in pairs]
        gb = [g_ref[0, rows[i], lanes_of[j]] for i, j in pairs]
        bb = [b_ref[0, rows[i], lanes_of[j]] for i, j in pairs]
        d = _each(lambda g: _dot(t_inc, jnp.concatenate([g, g], axis=1) * ux, HI), gb)
        kb = _each(lambda a, b: a * b, k, bb)
        kk = _each(lambda a, b: _dot_nt(a, b, HI), kb, k)
        qk = _each(lambda a, b: _dot_nt(a.astype(BF16), b.astype(BF16)), q, k)
        gc = [x[:, LANES:] for x in d]
        decay = [jnp.where(tril, jnp.exp(x[:, :c]), 0.0) for x in d]
        m = _each(lambda a, b: jnp.where(strict, a * b, 0.0), kk, decay)
        a_inv = _unit_lower_inverse(m, (eye, blk16, blk32))
        egc = _each(jnp.exp, gc)
        rhs = _each(lambda vv, b, kbb, e: jnp.concatenate([vv * b, kbb * e], axis=1), v, bb, kb, egc)
        sol = _each(lambda a, r: _dot(a, r, HI), a_inv, rhs)
        gl = [x[c - 1:c, :] for x in gc]
        for p, (i, j) in enumerate(pairs):
            n = ns[i]
            qk_s[n, j] = (qk[p] * decay[p]).astype(BF16)
            qg_s[n, j] = (q[p] * egc[p]).astype(BF16)
            kd_s[n, j] = (k[p] * jnp.exp(gl[p] - gc[p])).astype(BF16)
            gl_s[n, j] = gl[p]
            sol_s[n, j] = sol[p]

    def state_stage(n):
        rows = pl.ds(pl.multiple_of(n * c, c), c)
        state = [st[j] for j in heads]
        state_b = [x.astype(BF16) for x in state]
        v_new = [sol_s[n, j, :, :LANES] - _dot(sol_s[n, j, :, LANES:].astype(BF16), state_b[j])
                 for j in heads]
        v_new_b = [x.astype(BF16) for x in v_new]
        o = [_dot(qg_s[n, j], state_b[j]) + _dot(qk_s[n, j], v_new_b[j]) for j in heads]
        new_state = [state[j] * jnp.exp(gl_s[n, j]) + _dot_tn(kd_s[n, j], v_new_b[j]) for j in heads]
        for j in heads:
            st[j] = new_state[j]
            on = o[j] * lax.rsqrt(jnp.mean(o[j] * o[j], axis=-1, keepdims=True) + 1e-6) * nw
            z = z_ref[0, rows, lanes_of[j]]
            o_ref[0, rows, lanes_of[j]] = on * (z * jax.nn.sigmoid(z))

    nchunks = s // c
    group = 4

    def local_group(gidx, carry):
        local_stage([gidx * group + i for i in range(group)])
        return carry

    def state_chunk(n, carry):
        state_stage(n)
        return carry

    lax.fori_loop(0, nchunks // group, local_group, 0)
    lax.fori_loop(0, nchunks, state_chunk, 0)


def _gdn(proj3, gexp, bexp, conv_w, norm_w, cols):
    b, s, _ = proj3.shape
    nh = 2
    nc = s // CHUNK
    wd = nh * LANES
    cq, ck, cv, cz = (c0 // nh for c0 in cols)

    def blk(c0):
        return pl.BlockSpec((1, s, wd), lambda i, h: (i, 0, c0 + h))

    def cw(c0):
        return pl.BlockSpec((conv_w.shape[0], wd), lambda i, h: (0, c0 + h))

    head = pl.BlockSpec((1, s, wd), lambda i, h: (i, 0, h))
    return pl.pallas_call(
        _gdn_kernel,
        grid=(b, GDN_HEADS // nh),
        in_specs=[blk(cq), blk(ck), blk(cv), blk(cz), head, head,
                  cw(0), cw(GDN_HEADS // nh), cw(2 * GDN_HEADS // nh),
                  pl.BlockSpec((1, LANES), lambda i, h: (0, 0))],
        out_specs=head,
        out_shape=jax.ShapeDtypeStruct((b, s, GDN_HEADS * LANES), F32),
        scratch_shapes=[pltpu.VMEM((s, wd), F32), pltpu.VMEM((s, wd), F32),
                        pltpu.VMEM((s, wd), F32), pltpu.VMEM((nh, GDN_DK, LANES), F32),
                        pltpu.VMEM((nc, nh, CHUNK, 2 * LANES), F32), pltpu.VMEM((nc, nh, CHUNK, CHUNK), BF16),
                        pltpu.VMEM((nc, nh, CHUNK, LANES), BF16), pltpu.VMEM((nc, nh, CHUNK, LANES), BF16),
                        pltpu.VMEM((nc, nh, 1, LANES), F32)],
        compiler_params=pltpu.CompilerParams(
            dimension_semantics=("parallel", "parallel"), vmem_limit_bytes=VMEM_LIMIT),
        name="gdn",
    )(proj3, proj3, proj3, proj3, gexp, bexp, conv_w, conv_w, conv_w, norm_w)


def _fox_kernel(q_ref, k_ref, v_ref, c_ref, ct_ref, o_ref, *, tk):
    tq = q_ref.shape[1]
    nj = q_ref.shape[2] // LANES
    g = pl.program_id(1)
    qi = pl.program_id(2)
    q = q_ref[0]
    cblk = c_ref[0]
    lane = lax.broadcasted_iota(jnp.int32, (tq, LANES), 1)
    qpos = qi * tq + lax.broadcasted_iota(jnp.int32, (tq, tk), 0)
    kofs = lax.broadcasted_iota(jnp.int32, (tq, tk), 1)
    heads = list(range(nj))
    lanes_of = [slice(j * LANES, (j + 1) * LANES) for j in heads]
    ccol = [jnp.sum(jnp.where(lane == 8 + g * nj + j, cblk, 0.0), axis=-1, keepdims=True) for j in heads]
    qh = [q[:, hs] * jnp.asarray(FX_DH ** -0.5, BF16) for hs in lanes_of]

    def body(kv, carry):
        k0 = pl.multiple_of(kv * tk, tk)
        causal = qpos >= k0 + kofs
        kblk = k_ref[0, pl.ds(k0, tk), :]
        vblk = v_ref[0, pl.ds(k0, tk), :]
        sc = [_dot_nt(qh[j], kblk[:, lanes_of[j]]) for j in heads]
        crow = [ct_ref[0, pl.ds(g * nj + j, 1), pl.ds(k0, tk)] for j in heads]
        sc = [jnp.where(causal, sc[j] + ccol[j] - crow[j], -1e30) for j in heads]
        m_new = [jnp.maximum(carry[j][0], jnp.max(sc[j], axis=-1, keepdims=True)) for j in heads]
        a = [jnp.exp(carry[j][0] - m_new[j]) for j in heads]
        p = [jnp.exp(sc[j] - m_new[j]) for j in heads]
        l = [a[j] * carry[j][1] + jnp.sum(p[j], axis=-1, keepdims=True) for j in heads]
        acc = [a[j] * carry[j][2] + _dot(p[j].astype(BF16), vblk[:, lanes_of[j]]) for j in heads]
        return tuple((m_new[j], l[j], acc[j]) for j in heads)

    init = tuple((jnp.full((tq, 1), -1e30, F32), jnp.zeros((tq, 1), F32), jnp.zeros((tq, LANES), F32))
                 for _ in heads)
    nkv = (qi * tq + tq - 1) // tk + 1
    res = lax.fori_loop(0, nkv, body, init)
    o_ref[0] = jnp.concatenate([acc / l for _, l, acc in res], axis=1)


def _fox(pf3, c, ct):
    b, s, _ = pf3.shape
    wd = 4 * LANES
    tq = min(128, s)
    tk = min(256, s)
    ngrp = FX_HEADS * LANES // wd
    return pl.pallas_call(
        functools.partial(_fox_kernel, tk=tk),
        grid=(b, ngrp, s // tq),
        in_specs=[pl.BlockSpec((1, tq, wd), lambda i, h, t: (i, t, h)),
                  pl.BlockSpec((1, s, wd), lambda i, h, t: (i, 0, ngrp + h)),
                  pl.BlockSpec((1, s, wd), lambda i, h, t: (i, 0, 2 * ngrp + h)),
                  pl.BlockSpec((1, tq, LANES), lambda i, h, t: (i, t, 0)),
                  pl.BlockSpec((1, 8, s), lambda i, h, t: (i, 0, 0))],
        out_specs=pl.BlockSpec((1, tq, wd), lambda i, h, t: (i, t, h)),
        out_shape=jax.ShapeDtypeStruct((b, s, ngrp * wd), F32),
        compiler_params=pltpu.CompilerParams(
            dimension_semantics=("parallel", "parallel", "parallel"), vmem_limit_bytes=VMEM_LIMIT),
        name="fox",
    )(pf3, pf3, pf3, c, ct)


def _mix_kernel(oa_ref, ob_ref, ga_ref, gb_ref, x_ref, wa_ref, wb_ref, wo_ref, g1_ref, b1_ref, o_ref, *, alpha):
    ya = _dot(oa_ref[...].astype(BF16), wa_ref[...])
    yb = _dot(ob_ref[...].astype(BF16), wb_ref[...])
    mix = jax.nn.sigmoid(ga_ref[...]) * ya + jax.nn.sigmoid(gb_ref[...]) * yb
    z = alpha * x_ref[...] + _dot(mix.astype(BF16), wo_ref[...])
    o_ref[...] = _layernorm(z, g1_ref[...], b1_ref[...])


def _mix(oa, ob, proj, x2, wa, wb, wo, g1, b1, alpha):
    t, d = x2.shape
    tm = min(512, t)
    w, w2 = oa.shape[1], ob.shape[1]
    full = lambda r, c: pl.BlockSpec((r, c), lambda i: (0, 0))
    return pl.pallas_call(
        functools.partial(_mix_kernel, alpha=alpha),
        grid=(t // tm,),
        in_specs=[pl.BlockSpec((tm, w), lambda i: (i, 0)),
                  pl.BlockSpec((tm, w2), lambda i: (i, 0)),
                  pl.BlockSpec((tm, d), lambda i: (i, 0)),
                  pl.BlockSpec((tm, d), lambda i: (i, 1)),
                  pl.BlockSpec((tm, d), lambda i: (i, 0)),
                  full(w, d), full(w2, d), full(d, d), full(1, d), full(1, d)],
        out_specs=pl.BlockSpec((tm, d), lambda i: (i, 0)),
        out_shape=jax.ShapeDtypeStruct((t, d), F32),
        compiler_params=pltpu.CompilerParams(
            dimension_semantics=("parallel",), vmem_limit_bytes=VMEM_LIMIT),
        name="mix",
    )(oa, ob, proj, proj, x2, wa, wb, wo, g1, b1)


def _peer_kernel(h_cur_ref, h_nxt_ref, wq_ref, keys_ref, flat_ref, cmask_ref, uv_hbm, g2_ref, b2_ref, o_ref,
                 q_scr, s_scr, i_scr, best_scr, eh_scr, eall_scr, idv_scr, ids_smem, gate_scr,
                 buf_a, buf_b, sem, idsem, y_scr, *, tt, alpha):
    s = pl.program_id(0)
    tg, d = h_cur_ref.shape
    nk, kk = PEER_NKEYS, PEER_TOPK
    hk = PEER_HEADS * kk
    nlh = tg // LANES
    nsub = tg // (2 * tt)
    assert nsub == PEER_HEADS * nlh
    nslab, sub = uv_hbm.shape[1], uv_hbm.shape[2]
    half = sub // 2
    bufs = (buf_a, buf_b)
    rslot = s % 2
    eslot = 1 - rslot

    def issue(row0, slot):
        for t in range(tt):
            for k in range(hk):
                e = ids_smem[eslot, row0 + t, k]
                pltpu.make_async_copy(uv_hbm.at[e], bufs[slot].at[:, pl.ds((t * hk + k) * sub, sub), :],
                                      sem.at[slot]).start(priority=k % 2)

    def wait(slot):
        pltpu.make_async_copy(bufs[slot], bufs[slot], sem.at[slot]).wait()

    @pl.when(s > 0)
    def _():
        issue(0, 0)

    q = _dot(h_nxt_ref[...].astype(BF16), wq_ref[...])
    for j in range(2 * PEER_HEADS):
        for lh in range(nlh):
            q_scr[j, lh] = q[lh * LANES:(lh + 1) * LANES, j * LANES:(j + 1) * LANES].astype(BF16)
    iota_k = lax.broadcasted_iota(jnp.int32, (nk, LANES), 0)
    flat = flat_ref[...]
    cmask = cmask_ref[...]
    neg = jnp.float32(-jnp.inf)

    def route_piece(piece):
        hd = piece // nlh
        lh = piece % nlh
        for p in range(2):
            vals = _dot_nt(keys_ref[hd * 2 + p], q_scr[hd * 2 + p, lh])
            for r in range(kk):
                m = jnp.max(vals, axis=0, keepdims=True)
                am = jnp.min(jnp.where(vals == m, iota_k, nk), axis=0, keepdims=True)
                s_scr[p * kk + r:p * kk + r + 1, :] = m
                i_scr[p * kk + r:p * kk + r + 1, :] = am
                vals = jnp.where(iota_k == am, neg, vals)
        s1 = s_scr[kk:kk + 8, :]
        i1 = i_scr[kk:kk + 8, :]
        cand = [s_scr[0:1, :] + s_scr[kk:2 * kk, :]]
        cidx = [i_scr[0:1, :] * nk + i_scr[kk:2 * kk, :]]
        for a in range(1, 8):
            cand.append(s_scr[a:a + 1, :] + s1)
            cidx.append(i_scr[a:a + 1, :] * nk + i1)
        cand.append(s_scr[8:kk, :] + s_scr[kk:kk + 1, :])
        cidx.append(i_scr[8:kk, :] * nk + i_scr[kk:kk + 1, :])
        vals = jnp.concatenate(cand, axis=0) + cmask
        cidx = jnp.concatenate(cidx, axis=0)
        for r in range(kk):
            m = jnp.max(vals, axis=0, keepdims=True)
            am = jnp.min(jnp.where(vals == m, flat, 2 * kk * kk), axis=0, keepdims=True)
            sel = flat == am
            best_scr[r:r + 1, :] = m
            eh_scr[r:r + 1, :] = jnp.max(jnp.where(sel, cidx, -1), axis=0, keepdims=True)
            vals = jnp.where(sel, neg, vals)
        bs = best_scr[...]
        ex = jnp.exp(bs - bs[0:1, :])
        r0 = pl.multiple_of(hd * kk, kk)
        gate_scr[rslot, lh, pl.ds(r0, kk), :] = ex / jnp.sum(ex, axis=0, keepdims=True)
        eall_scr[lh, pl.ds(r0, kk), :] = eh_scr[...]

    lane = lax.broadcasted_iota(jnp.int32, (hk, LANES), 1)

    def compute(row0, slot):
        buf = bufs[slot]
        gt = gate_scr[eslot, row0 // LANES]
        lane0 = row0 % LANES
        for t in range(tt):
            hrow = h_cur_ref[pl.ds(row0 + t, 1), :]

            def rows(c, j):
                return buf[c, pl.ds(t * hk * sub + j, hk, stride=sub), :]

            part = None
            for j in range(half):
                for c in range(nslab):
                    seg = j * nslab + c
                    term = rows(c, j) * hrow[:, seg * LANES:(seg + 1) * LANES]
                    part = term if part is None else part + term
            pre = jnp.sum(part, axis=-1, keepdims=True)
            gate = jnp.sum(jnp.where(lane == lane0 + t, gt, 0.0), axis=-1, keepdims=True)
            act = 0.5 * pre * (1.0 + lax.erf(pre * (2.0 ** -0.5))) * gate
            yrow = slot * tt + t
            for j in range(half):
                for c in range(nslab):
                    seg = j * nslab + c
                    y_scr[yrow:yrow + 1, seg * LANES:(seg + 1) * LANES] = jnp.sum(
                        act * rows(c, half + j), axis=0, keepdims=True)

    def substep(j, carry):
        row0 = pl.multiple_of(j * 2 * tt, 2 * tt)
        route_piece(j)
        wait(0)
        issue(row0 + tt, 1)
        compute(row0, 0)
        wait(1)
        issue(jnp.minimum(row0 + 2 * tt, tg - 2 * tt), 0)
        compute(row0 + tt, 1)
        z = alpha * h_cur_ref[pl.ds(row0, 2 * tt), :] + y_scr[...]
        o_ref[pl.ds(row0, 2 * tt), :] = _layernorm(z, g2_ref[...], b2_ref[...])
        return carry

    @pl.when(s == 0)
    def _():
        o_ref[...] = jnp.zeros_like(o_ref)

        def piece(p, carry):
            route_piece(p)
            return carry
        lax.fori_loop(0, nsub, piece, 0)

    @pl.when(s > 0)
    def _():
        lax.fori_loop(0, nsub, substep, 0)
        wait(0)

    for lh in range(nlh):
        idv_scr[lh * LANES:(lh + 1) * LANES, :] = eall_scr[lh].T
    publish = pltpu.make_async_copy(idv_scr, ids_smem.at[rslot], idsem)
    publish.start()
    publish.wait()


def _peer(h1, wq, keys, uv_tab, g2, b2, alpha):
    t, d = h1.shape
    tg = 256
    tt = 8
    ngrp = t // tg
    nq = wq.shape[1]
    kk = PEER_TOPK
    hk = PEER_HEADS * kk
    nlh = tg // LANES
    nslab, sub = uv_tab.shape[1], uv_tab.shape[2]
    pairs = [(0, bb) for bb in range(kk)]
    for a in range(1, 8):
        pairs += [(a, bb) for bb in range(8)]
    pairs += [(a, 0) for a in range(8, kk)]
    real = [(a + 1) * (bb + 1) <= kk for a, bb in pairs]
    flat = [a * kk + bb if ok else kk * kk + r for r, ((a, bb), ok) in enumerate(zip(pairs, real))]
    flat = jnp.broadcast_to(jnp.asarray(flat, jnp.int32)[:, None], (len(pairs), LANES))
    cmask = jnp.broadcast_to(jnp.asarray([0.0 if ok else -jnp.inf for ok in real], F32)[:, None],
                             (len(pairs), LANES))
    const = lambda shape: pl.BlockSpec(shape, lambda i: (0,) * len(shape))
    return pl.pallas_call(
        functools.partial(_peer_kernel, tt=tt, alpha=alpha),
        grid=(ngrp + 1,),
        in_specs=[pl.BlockSpec((tg, d), lambda i: (jnp.maximum(i - 1, 0), 0)),
                  pl.BlockSpec((tg, d), lambda i: (jnp.minimum(i, ngrp - 1), 0)),
                  const((d, nq)), const(keys.shape), const(flat.shape), const(cmask.shape),
                  pl.BlockSpec(memory_space=pl.ANY),
                  const((1, d)), const((1, d))],
        out_specs=pl.BlockSpec((tg, d), lambda i: (jnp.maximum(i - 1, 0), 0)),
        out_shape=jax.ShapeDtypeStruct((t, d), F32),
        scratch_shapes=[pltpu.VMEM((2 * PEER_HEADS, nlh, LANES, LANES), BF16),
                        pltpu.VMEM((2 * kk, LANES), F32), pltpu.VMEM((2 * kk, LANES), jnp.int32),
                        pltpu.VMEM((kk, LANES), F32), pltpu.VMEM((kk, LANES), jnp.int32),
                        pltpu.VMEM((nlh, hk, LANES), jnp.int32),
                        pltpu.VMEM((tg, hk), jnp.int32),
                        pltpu.SMEM((2, tg, hk), jnp.int32),
                        pltpu.VMEM((2, nlh, hk, LANES), F32),
                        pltpu.VMEM((nslab, tt * hk * sub, LANES), F32),
                        pltpu.VMEM((nslab, tt * hk * sub, LANES), F32),
                        pltpu.SemaphoreType.DMA((2,)), pltpu.SemaphoreType.DMA,
                        pltpu.VMEM((2 * tt, d), F32)],
        compiler_params=pltpu.CompilerParams(
            dimension_semantics=("arbitrary",), vmem_limit_bytes=VMEM_LIMIT),
        name="peer",
    )(h1, h1, wq, keys, flat, cmask, uv_tab, g2, b2)


def _layer(h, w_in, conv_w, a_log, dt_bias, norm_w, f_bias, w_out_gdn, w_out_fox, w_o, ln1_g, ln1_b,
           peer_wq, peer_keys, peer_u, peer_v, ln2_g, ln2_b, alpha):
    b, s, d = h.shape
    t = b * s
    qk = GDN_HEADS * GDN_DK
    fw = FX_HEADS * FX_DH
    o_gz = 4 * qk
    o_ga = o_gz
    o_fq = o_ga + 2 * GDN_HEADS
    o_ff = o_fq + 3 * fw
    o_gate = o_ff + FX_HEADS
    w_big = jnp.concatenate([w_in[:, o_gate:], w_in[:, :o_gz]], axis=1).astype(BF16)
    pad = LANES - FX_DH
    w_fox = jnp.pad(w_in[:, o_fq:o_ff].reshape(d, 3 * FX_HEADS, FX_DH), ((0, 0), (0, 0), (0, pad)))
    w_fox = w_fox.reshape(d, 3 * FX_HEADS * LANES).astype(BF16)
    w_out_fox_p = jnp.pad(w_out_fox.reshape(FX_HEADS, FX_DH, d), ((0, 0), (0, pad), (0, 0)))
    w_out_fox_p = w_out_fox_p.reshape(FX_HEADS * LANES, d).astype(BF16)
    n_small = 2 * GDN_HEADS + FX_HEADS
    w_small = jnp.concatenate([w_in[:, o_ga:o_fq], w_in[:, o_ff:o_gate],
                               jnp.zeros((d, LANES - n_small), F32)], axis=1)
    params = jnp.zeros((8, LANES), F32)
    params = params.at[0, :GDN_HEADS].set(a_log).at[1, :GDN_HEADS].set(dt_bias)
    params = params.at[2, 2 * GDN_HEADS:n_small].set(f_bias)

    x2 = h.reshape(t, d)
    proj = _in_proj(x2, w_big, F32, "in_proj")
    pf = _in_proj(x2, w_fox, BF16, "in_proj_fox")
    gexp, bexp, c, ct = _prep(h, w_small, params)
    proj3 = proj.reshape(b, s, proj.shape[1])
    gdn0 = 2 * d // LANES
    oa = _gdn(proj3, gexp, bexp, conv_w, norm_w.reshape(1, LANES),
              (gdn0, gdn0 + GDN_HEADS, gdn0 + 2 * GDN_HEADS, gdn0 + 3 * GDN_HEADS))
    ob = _fox(pf.reshape(b, s, pf.shape[1]), c, ct)
    h1 = _mix(oa.reshape(t, qk), ob.reshape(t, FX_HEADS * LANES), proj, x2,
              w_out_gdn.astype(BF16), w_out_fox_p, w_o.astype(BF16),
              ln1_g.reshape(1, d), ln1_b.reshape(1, d), alpha)
    keys = peer_keys.reshape(2 * PEER_HEADS, PEER_NKEYS, peer_keys.shape[-1]).astype(BF16)
    ne = peer_u.shape[0]
    half = PEER_SUB // 2
    nslab = d // (half * LANES)
    uv_tab = jnp.concatenate([peer_u.reshape(ne, half, nslab, LANES),
                              peer_v.reshape(ne, half, nslab, LANES)], axis=1).transpose(0, 2, 1, 3)
    out = _peer(h1, peer_wq.astype(BF16), keys, uv_tab, ln2_g.reshape(1, d), ln2_b.reshape(1, d), alpha)
    return out.reshape(b, s, d)


def kernel(x, w_in, gdn_conv_w, gdn_a_log, gdn_dt_bias, gdn_norm_w, fox_f_bias, w_out_gdn, w_out_fox, w_o,
           ln1_g, ln1_b, peer_wq, peer_keys, peer_u, peer_v, ln2_g, ln2_b):
    depth = w_in.shape[0]
    alpha = (2.0 * depth) ** 0.25
    h = x
    for l in range(depth):
        h = _layer(h, w_in[l], gdn_conv_w[l], gdn_a_log[l], gdn_dt_bias[l], gdn_norm_w[l], fox_f_bias[l],
                   w_out_gdn[l], w_out_fox[l], w_o[l], ln1_g[l], ln1_b[l], peer_wq[l], peer_keys[l],
                   peer_u[l], peer_v[l], ln2_g[l], ln2_b[l], alpha)
    return h
```

```python
import functools

import jax
import jax.numpy as jnp
from jax import lax
from jax.experimental import pallas as pl
from jax.experimental.pallas import tpu as pltpu

F32 = jnp.float32
BF16 = jnp.bfloat16
HI = lax.Precision.HIGHEST

LANES = 128
CHUNK = 64
GDN_HEADS = 4
GDN_DK = 128
FX_HEADS = 8
FX_DH = 64
PEER_HEADS = 8
PEER_NKEYS = 128
PEER_TOPK = 16
PEER_SUB = 4
LN_EPS = 1e-5
VMEM_LIMIT = 48 * 1024 * 1024


def _dot(a, b, prec=None):
    return jnp.dot(a, b, preferred_element_type=F32, precision=prec)


def _dot_nt(a, b, prec=None):
    return lax.dot_general(a, b, (((1,), (1,)), ((), ())), preferred_element_type=F32, precision=prec)


def _dot_tn(a, b, prec=None):
    return lax.dot_general(a, b, (((0,), (0,)), ((), ())), preferred_element_type=F32, precision=prec)


def _softplus(x):
    return jnp.maximum(x, 0.0) + jnp.log1p(jnp.exp(-jnp.abs(x)))


def _layernorm(z, g, b):
    mu = jnp.mean(z, axis=-1, keepdims=True)
    zc = z - mu
    var = jnp.mean(zc * zc, axis=-1, keepdims=True)
    return zc * lax.rsqrt(var + LN_EPS) * g + b


def _mm_kernel(x_ref, w_ref, o_ref):
    o_ref[...] = _dot(x_ref[...].astype(BF16), w_ref[...]).astype(o_ref.dtype)


def _in_proj(x2, w_big, out_dtype, name):
    t, d = x2.shape
    n = w_big.shape[1]
    tm = min(1024, t)
    tn = 512
    return pl.pallas_call(
        _mm_kernel,
        grid=(t // tm, n // tn),
        in_specs=[pl.BlockSpec((tm, d), lambda i, j: (i, 0)),
                  pl.BlockSpec((d, tn), lambda i, j: (0, j))],
        out_specs=pl.BlockSpec((tm, tn), lambda i, j: (i, j)),
        out_shape=jax.ShapeDtypeStruct((t, n), out_dtype),
        compiler_params=pltpu.CompilerParams(
            dimension_semantics=("parallel", "parallel"), vmem_limit_bytes=VMEM_LIMIT),
        name=name,
    )(x2, w_big)


def _prep_kernel(x_ref, w_ref, par_ref, gexp_ref, bexp_ref, c_ref, ct_ref, carry_ref):
    ts = x_ref.shape[1]

    @pl.when(pl.program_id(1) == 0)
    def _():
        carry_ref[...] = jnp.zeros_like(carry_ref)

    small = _dot(x_ref[0], w_ref[...], HI)
    a_log = par_ref[0:1, :]
    dt_bias = par_ref[1:2, :]
    f_bias = par_ref[2:3, :]
    g = -jnp.exp(a_log) * _softplus(small + dt_bias)
    beta = jax.nn.sigmoid(small)
    lane = lax.broadcasted_iota(jnp.int32, (ts, LANES), 1)
    log_f = jnp.where((lane >= 8) & (lane < 16), -_softplus(-(small + f_bias)), 0.0)
    row = lax.broadcasted_iota(jnp.int32, (ts, ts), 0)
    col = lax.broadcasted_iota(jnp.int32, (ts, ts), 1)
    tril = (row >= col).astype(F32)
    c = _dot(tril, log_f, HI) + carry_ref[...]
    carry_ref[...] = c[ts - 1:ts, :]
    c_ref[0] = c
    ct_ref[0] = c.T[8:16, :]
    gexp_ref[0] = jnp.concatenate(
        [jnp.broadcast_to(g[:, h:h + 1], (ts, LANES)) for h in range(GDN_HEADS)], axis=1)
    bexp_ref[0] = jnp.concatenate(
        [jnp.broadcast_to(beta[:, GDN_HEADS + h:GDN_HEADS + h + 1], (ts, LANES)) for h in range(GDN_HEADS)], axis=1)


def _prep(x, w_small, params):
    b, s, d = x.shape
    ts = min(512, s)
    hw = GDN_HEADS * LANES
    return pl.pallas_call(
        _prep_kernel,
        grid=(b, s // ts),
        in_specs=[pl.BlockSpec((1, ts, d), lambda i, j: (i, j, 0)),
                  pl.BlockSpec((d, LANES), lambda i, j: (0, 0)),
                  pl.BlockSpec((8, LANES), lambda i, j: (0, 0))],
        out_specs=[pl.BlockSpec((1, ts, hw), lambda i, j: (i, j, 0)),
                   pl.BlockSpec((1, ts, hw), lambda i, j: (i, j, 0)),
                   pl.BlockSpec((1, ts, LANES), lambda i, j: (i, j, 0)),
                   pl.BlockSpec((1, 8, ts), lambda i, j: (i, 0, j))],
        out_shape=[jax.ShapeDtypeStruct((b, s, hw), F32),
                   jax.ShapeDtypeStruct((b, s, hw), F32),
                   jax.ShapeDtypeStruct((b, s, LANES), F32),
                   jax.ShapeDtypeStruct((b, 8, s), F32)],
        scratch_shapes=[pltpu.VMEM((1, LANES), F32)],
        compiler_params=pltpu.CompilerParams(
            dimension_semantics=("parallel", "arbitrary"), vmem_limit_bytes=VMEM_LIMIT),
        name="prep",
    )(x, w_small, params)


def _each(fn, *lists):
    return [fn(*args) for args in zip(*lists)]


def _unit_lower_inverse(ms, masks):
    eye, blk16, blk32 = masks
    hi = lambda a, b: _dot(a, b, HI)
    n1 = _each(lambda m: -jnp.where(blk16, m, 0.0), ms)
    l1 = _each(lambda m: jnp.where(blk32 & jnp.logical_not(blk16), m, 0.0), ms)
    l2 = _each(lambda m: jnp.where(blk32, 0.0, m), ms)
    n2 = _each(hi, n1, n1)
    p = _each(lambda a, b: hi(eye + a, eye + b), n1, n2)
    n4 = _each(hi, n2, n2)
    p = _each(lambda a, b: hi(a, eye + b), p, n4)
    n8 = _each(hi, n4, n4)
    d_inv = _each(lambda a, b: hi(a, eye + b), p, n8)
    dl = _each(hi, d_inv, l1)
    a32 = _each(lambda a, b: a - hi(b, a), d_inv, dl)
    al = _each(hi, a32, l2)
    return _each(lambda a, b: a - hi(b, a), a32, al)


def _gdn_kernel(q_ref, k_ref, v_ref, z_ref, g_ref, b_ref, cwq_ref, cwk_ref, cwv_ref, nw_ref,
                o_ref, qn, kn, vn, st, sol_s, qk_s, qg_s, kd_s, gl_s):
    s = q_ref.shape[1]
    c = CHUNK
    nh = q_ref.shape[2] // LANES
    row = lax.broadcasted_iota(jnp.int32, (s, LANES), 0)

    def conv_silu(x, w):
        y = x * w[3:4, :]
        for sh in (1, 2, 3):
            xs = jnp.where(row >= sh, pltpu.roll(x, sh, axis=0), 0.0)
            y = y + xs * w[3 - sh:4 - sh, :]
        return y * jax.nn.sigmoid(y)

    def l2norm(x):
        return x * lax.rsqrt(jnp.sum(x * x, axis=-1, keepdims=True) + 1e-6)

    for j in range(nh):
        hs = slice(j * LANES, (j + 1) * LANES)
        qn[:, hs] = l2norm(conv_silu(q_ref[0, :, hs], cwq_ref[:, hs])) * (GDN_DK ** -0.5)
        kn[:, hs] = l2norm(conv_silu(k_ref[0, :, hs], cwk_ref[:, hs]))
        vn[:, hs] = conv_silu(v_ref[0, :, hs], cwv_ref[:, hs])
    st[...] = jnp.zeros_like(st)

    ri = lax.broadcasted_iota(jnp.int32, (c, c), 0)
    ci = lax.broadcasted_iota(jnp.int32, (c, c), 1)
    tril = ri >= ci
    strict = ri > ci
    t_inc = tril.astype(F32)
    eye = (ri == ci).astype(F32)
    blk16 = (ri >> 4) == (ci >> 4)
    blk32 = (ri >> 5) == (ci >> 5)
    l2 = lax.broadcasted_iota(jnp.int32, (c, 2 * LANES), 0)
    j2 = lax.broadcasted_iota(jnp.int32, (c, 2 * LANES), 1)
    ux = jnp.where((j2 >= c) | (l2 > j2), 1.0, 0.0).astype(F32)
    nw = nw_ref[...]

    heads = list(range(nh))
    lanes_of = [slice(j * LANES, (j + 1) * LANES) for j in heads]

    def local_stage(ns):
        pairs = [(i, j) for i in range(len(ns)) for j in heads]
        rows = [pl.ds(pl.multiple_of(n * c, c), c) for n in ns]
        q = [qn[rows[i], lanes_of[j]] for i, j in pairs]
        k = [kn[rows[i], lanes_of[j]] for i, j in pairs]
        v = [vn[rows[i], lanes_of[j]] for i, j in pairs]
        gb = [g_ref[0, rows[i], lanes_of[j]] for i, j in pairs]
        bb = [b_ref[0, rows[i], lanes_of[j]] for i, j in pairs]
        d = _each(lambda g: _dot(t_inc, jnp.concatenate([g, g], axis=1) * ux, HI), gb)
        kb = _each(lambda a, b: a * b, k, bb)
        kk = _each(lambda a, b: _dot_nt(a, b, HI), kb, k)
        qk = _each(lambda a, b: _dot_nt(a.astype(BF16), b.astype(BF16)), q, k)
        gc = [x[:, LANES:] for x in d]
        decay = [jnp.where(tril, jnp.exp(x[:, :c]), 0.0) for x in d]
        m = _each(lambda a, b: jnp.where(strict, a * b, 0.0), kk, decay)
        a_inv = _unit_lower_inverse(m, (eye, blk16, blk32))
        egc = _each(jnp.exp, gc)
        rhs = _each(lambda vv, b, kbb, e: jnp.concatenate([vv * b, kbb * e], axis=1), v, bb, kb, egc)
        sol = _each(lambda a, r: _dot(a, r, HI), a_inv, rhs)
        gl = [x[c - 1:c, :] for x in gc]
        for p, (i, j) in enumerate(pairs):
            n = ns[i]
            qk_s[n, j] = (qk[p] * decay[p]).astype(BF16)
            qg_s[n, j] = (q[p] * egc[p]).astype(BF16)
            kd_s[n, j] = (k[p] * jnp.exp(gl[p] - gc[p])).astype(BF16)
            gl_s[n, j] = gl[p]
            sol_s[n, j] = sol[p]

    def state_stage(n):
        rows = pl.ds(pl.multiple_of(n * c, c), c)
        state = [st[j] for j in heads]
        state_b = [x.astype(BF16) for x in state]
        v_new = [sol_s[n, j, :, :LANES] - _dot(sol_s[n, j, :, LANES:].astype(BF16), state_b[j])
                 for j in heads]
        v_new_b = [x.astype(BF16) for x in v_new]
        o = [_dot(qg_s[n, j], state_b[j]) + _dot(qk_s[n, j], v_new_b[j]) for j in heads]
        new_state = [state[j] * jnp.exp(gl_s[n, j]) + _dot_tn(kd_s[n, j], v_new_b[j]) for j in heads]
        for j in heads:
            st[j] = new_state[j]
            on = o[j] * lax.rsqrt(jnp.mean(o[j] * o[j], axis=-1, keepdims=True) + 1e-6) * nw
            z = z_ref[0, rows, lanes_of[j]]
            o_ref[0, rows, lanes_of[j]] = on * (z * jax.nn.sigmoid(z))

    nchunks = s // c
    group = 4

    def local_group(gidx, carry):
        local_stage([gidx * group + i for i in range(group)])
        return carry

    def state_chunk(n, carry):
        state_stage(n)
        return carry

    lax.fori_loop(0, nchunks // group, local_group, 0)
    lax.fori_loop(0, nchunks, state_chunk, 0)


def _gdn(proj3, gexp, bexp, conv_w, norm_w, cols):
    b, s, _ = proj3.shape
    nh = 2
    nc = s // CHUNK
    wd = nh * LANES
    cq, ck, cv, cz = (c0 // nh for c0 in cols)

    def blk(c0):
        return pl.BlockSpec((1, s, wd), lambda i, h: (i, 0, c0 + h))

    def cw(c0):
        return pl.BlockSpec((conv_w.shape[0], wd), lambda i, h: (0, c0 + h))

    head = pl.BlockSpec((1, s, wd), lambda i, h: (i, 0, h))
    return pl.pallas_call(
        _gdn_kernel,
        grid=(b, GDN_HEADS // nh),
        in_specs=[blk(cq), blk(ck), blk(cv), blk(cz), head, head,
                  cw(0), cw(GDN_HEADS // nh), cw(2 * GDN_HEADS // nh),
                  pl.BlockSpec((1, LANES), lambda i, h: (0, 0))],
        out_specs=head,
        out_shape=jax.ShapeDtypeStruct((b, s, GDN_HEADS * LANES), F32),
        scratch_shapes=[pltpu.VMEM((s, wd), F32), pltpu.VMEM((s, wd), F32),
                        pltpu.VMEM((s, wd), F32), pltpu.VMEM((nh, GDN_DK, LANES), F32),
                        pltpu.VMEM((nc, nh, CHUNK, 2 * LANES), F32), pltpu.VMEM((nc, nh, CHUNK, CHUNK), BF16),
                        pltpu.VMEM((nc, nh, CHUNK, LANES), BF16), pltpu.VMEM((nc, nh, CHUNK, LANES), BF16),
                        pltpu.VMEM((nc, nh, 1, LANES), F32)],
        compiler_params=pltpu.CompilerParams(
            dimension_semantics=("parallel", "parallel"), vmem_limit_bytes=VMEM_LIMIT),
        name="gdn",
    )(proj3, proj3, proj3, proj3, gexp, bexp, conv_w, conv_w, conv_w, norm_w)


def _fox_kernel(q_ref, k_ref, v_ref, c_ref, ct_ref, o_ref, *, tk):
    tq = q_ref.shape[1]
    nj = q_ref.shape[2] // LANES
    g = pl.program_id(1)
    qi = pl.program_id(2)
    q = q_ref[0]
    cblk = c_ref[0]
    lane = lax.broadcasted_iota(jnp.int32, (tq, LANES), 1)
    qpos = qi * tq + lax.broadcasted_iota(jnp.int32, (tq, tk), 0)
    kofs = lax.broadcasted_iota(jnp.int32, (tq, tk), 1)
    heads = list(range(nj))
    lanes_of = [slice(j * LANES, (j + 1) * LANES) for j in heads]
    ccol = [jnp.sum(jnp.where(lane == 8 + g * nj + j, cblk, 0.0), axis=-1, keepdims=True) for j in heads]
    qh = [q[:, hs] * jnp.asarray(FX_DH ** -0.5, BF16) for hs in lanes_of]

    def body(kv, carry):
        k0 = pl.multiple_of(kv * tk, tk)
        causal = qpos >= k0 + kofs
        kblk = k_ref[0, pl.ds(k0, tk), :]
        vblk = v_ref[0, pl.ds(k0, tk), :]
        sc = [_dot_nt(qh[j], kblk[:, lanes_of[j]]) for j in heads]
        crow = [ct_ref[0, pl.ds(g * nj + j, 1), pl.ds(k0, tk)] for j in heads]
        sc = [jnp.where(causal, sc[j] + ccol[j] - crow[j], -1e30) for j in heads]
        m_new = [jnp.maximum(carry[j][0], jnp.max(sc[j], axis=-1, keepdims=True)) for j in heads]
        a = [jnp.exp(carry[j][0] - m_new[j]) for j in heads]
        p = [jnp.exp(sc[j] - m_new[j]) for j in heads]
        l = [a[j] * carry[j][1] + jnp.sum(p[j], axis=-1, keepdims=True) for j in heads]
        acc = [a[j] * carry[j][2] + _dot(p[j].astype(BF16), vblk[:, lanes_of[j]]) for j in heads]
        return tuple((m_new[j], l[j], acc[j]) for j in heads)

    init = tuple((jnp.full((tq, 1), -1e30, F32), jnp.zeros((tq, 1), F32), jnp.zeros((tq, LANES), F32))
                 for _ in heads)
    nkv = (qi * tq + tq - 1) // tk + 1
    res = lax.fori_loop(0, nkv, body, init)
    o_ref[0] = jnp.concatenate([acc / l for _, l, acc in res], axis=1)


def _fox(pf3, c, ct):
    b, s, _ = pf3.shape
    wd = 4 * LANES
    tq = min(128, s)
    tk = min(256, s)
    ngrp = FX_HEADS * LANES // wd
    return pl.pallas_call(
        functools.partial(_fox_kernel, tk=tk),
        grid=(b, ngrp, s // tq),
        in_specs=[pl.BlockSpec((1, tq, wd), lambda i, h, t: (i, t, h)),
                  pl.BlockSpec((1, s, wd), lambda i, h, t: (i, 0, ngrp + h)),
                  pl.BlockSpec((1, s, wd), lambda i, h, t: (i, 0, 2 * ngrp + h)),
                  pl.BlockSpec((1, tq, LANES), lambda i, h, t: (i, t, 0)),
                  pl.BlockSpec((1, 8, s), lambda i, h, t: (i, 0, 0))],
        out_specs=pl.BlockSpec((1, tq, wd), lambda i, h, t: (i, t, h)),
        out_shape=jax.ShapeDtypeStruct((b, s, ngrp * wd), F32),
        compiler_params=pltpu.CompilerParams(
            dimension_semantics=("parallel", "parallel", "parallel"), vmem_limit_bytes=VMEM_LIMIT),
        name="fox",
    )(pf3, pf3, pf3, c, ct)


def _mix_kernel(oa_ref, ob_ref, ga_ref, gb_ref, x_ref, wa_ref, wb_ref, wo_ref, g1_ref, b1_ref, o_ref, *, alpha):
    ya = _dot(oa_ref[...].astype(BF16), wa_ref[...])
    yb = _dot(ob_ref[...].astype(BF16), wb_ref[...])
    mix = jax.nn.sigmoid(ga_ref[...]) * ya + jax.nn.sigmoid(gb_ref[...]) * yb
    z = alpha * x_ref[...] + _dot(mix.astype(BF16), wo_ref[...])
    o_ref[...] = _layernorm(z, g1_ref[...], b1_ref[...])


def _mix(oa, ob, proj, x2, wa, wb, wo, g1, b1, alpha):
    t, d = x2.shape
    tm = min(512, t)
    w, w2 = oa.shape[1], ob.shape[1]
    full = lambda r, c: pl.BlockSpec((r, c), lambda i: (0, 0))
    return pl.pallas_call(
        functools.partial(_mix_kernel, alpha=alpha),
        grid=(t // tm,),
        in_specs=[pl.BlockSpec((tm, w), lambda i: (i, 0)),
                  pl.BlockSpec((tm, w2), lambda i: (i, 0)),
                  pl.BlockSpec((tm, d), lambda i: (i, 0)),
                  pl.BlockSpec((tm, d), lambda i: (i, 1)),
                  pl.BlockSpec((tm, d), lambda i: (i, 0)),
                  full(w, d), full(w2, d), full(d, d), full(1, d), full(1, d)],
        out_specs=pl.BlockSpec((tm, d), lambda i: (i, 0)),
        out_shape=jax.ShapeDtypeStruct((t, d), F32),
        compiler_params=pltpu.CompilerParams(
            dimension_semantics=("parallel",), vmem_limit_bytes=VMEM_LIMIT),
        name="mix",
    )(oa, ob, proj, proj, x2, wa, wb, wo, g1, b1)


def _peer_kernel(h_cur_ref, h_nxt_ref, wq_ref, keys_ref, flat_ref, cmask_ref, uv_hbm, g2_ref, b2_ref, o_ref,
                 q_scr, s_scr, i_scr, best_scr, eh_scr, eall_scr, idv_scr, ids_smem, gate_scr,
                 buf_a, buf_b, sem, idsem, y_scr, *, tt, alpha):
    s = pl.program_id(0)
    tg, d = h_cur_ref.shape
    nk, kk = PEER_NKEYS, PEER_TOPK
    hk = PEER_HEADS * kk
    nlh = tg // LANES
    nsub = tg // (2 * tt)
    assert nsub == PEER_HEADS * nlh
    nslab, sub = uv_hbm.shape[1], uv_hbm.shape[2]
    half = sub // 2
    bufs = (buf_a, buf_b)
    last = pl.num_programs(0) - 1
    rslot = s % 3
    pslot = (s + 2) % 3
    eslot = (s + 1) % 3

    def issue(idslot, row0, slot):
        for t in range(tt):
            for k in range(hk):
                e = ids_smem[idslot, row0 + t, k]
                pltpu.make_async_copy(uv_hbm.at[e], bufs[slot].at[:, pl.ds((t * hk + k) * sub, sub), :],
                                      sem.at[slot]).start(priority=k % 2)

    def wait(slot):
        pltpu.make_async_copy(bufs[slot], bufs[slot], sem.at[slot]).wait()

    q = _dot(h_nxt_ref[...].astype(BF16), wq_ref[...])
    for j in range(2 * PEER_HEADS):
        for lh in range(nlh):
            q_scr[j, lh] = q[lh * LANES:(lh + 1) * LANES, j * LANES:(j + 1) * LANES].astype(BF16)
    iota_k = lax.broadcasted_iota(jnp.int32, (nk, LANES), 0)
    flat = flat_ref[...]
    cmask = cmask_ref[...]
    neg = jnp.float32(-jnp.inf)

    def route_piece(piece):
        hd = piece // nlh
        lh = piece % nlh
        for p in range(2):
            vals = _dot_nt(keys_ref[hd * 2 + p], q_scr[hd * 2 + p, lh])
            for r in range(kk):
                m = jnp.max(vals, axis=0, keepdims=True)
                am = jnp.min(jnp.where(vals == m, iota_k, nk), axis=0, keepdims=True)
                s_scr[p * kk + r:p * kk + r + 1, :] = m
                i_scr[p * kk + r:p * kk + r + 1, :] = am
                vals = jnp.where(iota_k == am, neg, vals)
        s1 = s_scr[kk:kk + 8, :]
        i1 = i_scr[kk:kk + 8, :]
        cand = [s_scr[0:1, :] + s_scr[kk:2 * kk, :]]
        cidx = [i_scr[0:1, :] * nk + i_scr[kk:2 * kk, :]]
        for a in range(1, 8):
            cand.append(s_scr[a:a + 1, :] + s1)
            cidx.append(i_scr[a:a + 1, :] * nk + i1)
        cand.append(s_scr[8:kk, :] + s_scr[kk:kk + 1, :])
        cidx.append(i_scr[8:kk, :] * nk + i_scr[kk:kk + 1, :])
        vals = jnp.concatenate(cand, axis=0) + cmask
        cidx = jnp.concatenate(cidx, axis=0)
        for r in range(kk):
            m = jnp.max(vals, axis=0, keepdims=True)
            am = jnp.min(jnp.where(vals == m, flat, 2 * kk * kk), axis=0, keepdims=True)
            sel = flat == am
            best_scr[r:r + 1, :] = m
            eh_scr[r:r + 1, :] = jnp.max(jnp.where(sel, cidx, -1), axis=0, keepdims=True)
            vals = jnp.where(sel, neg, vals)
        bs = best_scr[...]
        ex = jnp.exp(bs - bs[0:1, :])
        r0 = pl.multiple_of(hd * kk, kk)
        gate_scr[rslot, lh, pl.ds(r0, kk), :] = ex / jnp.sum(ex, axis=0, keepdims=True)
        eall_scr[lh, pl.ds(r0, kk), :] = eh_scr[...]

    lane = lax.broadcasted_iota(jnp.int32, (hk, LANES), 1)

    def compute(row0, slot):
        buf = bufs[slot]
        gt = gate_scr[eslot, row0 // LANES]
        lane0 = row0 % LANES
        for t in range(tt):
            hrow = h_cur_ref[pl.ds(row0 + t, 1), :]

            def rows(c, j):
                return buf[c, pl.ds(t * hk * sub + j, hk, stride=sub), :]

            part = None
            for j in range(half):
                for c in range(nslab):
                    seg = j * nslab + c
                    term = rows(c, j) * hrow[:, seg * LANES:(seg + 1) * LANES]
                    part = term if part is None else part + term
            pre = jnp.sum(part, axis=-1, keepdims=True)
            gate = jnp.sum(jnp.where(lane == lane0 + t, gt, 0.0), axis=-1, keepdims=True)
            act = 0.5 * pre * (1.0 + lax.erf(pre * (2.0 ** -0.5))) * gate
            yrow = slot * tt + t
            for j in range(half):
                for c in range(nslab):
                    seg = j * nslab + c
                    y_scr[yrow:yrow + 1, seg * LANES:(seg + 1) * LANES] = jnp.sum(
                        act * rows(c, half + j), axis=0, keepdims=True)

    def substep(j, carry):
        row0 = pl.multiple_of(j * 2 * tt, 2 * tt)
        route_piece(j)
        wait(0)
        issue(eslot, row0 + tt, 1)
        compute(row0, 0)
        wait(1)
        wrap = j == nsub - 1
        issue(jnp.where(wrap, pslot, eslot), jnp.where(wrap, 0, row0 + 2 * tt), 0)
        compute(row0 + tt, 1)
        z = alpha * h_cur_ref[pl.ds(row0, 2 * tt), :] + y_scr[...]
        o_ref[pl.ds(row0, 2 * tt), :] = _layernorm(z, g2_ref[...], b2_ref[...])
        return carry

    @pl.when(s < 2)
    def _():
        o_ref[...] = jnp.zeros_like(o_ref)

        def piece(p, carry):
            route_piece(p)
            return carry
        lax.fori_loop(0, nsub, piece, 0)

    @pl.when(s == 1)
    def _():
        issue(pslot, 0, 0)

    @pl.when(s >= 2)
    def _():
        lax.fori_loop(0, nsub, substep, 0)

    @pl.when(s == last)
    def _():
        wait(0)

    for lh in range(nlh):
        idv_scr[lh * LANES:(lh + 1) * LANES, :] = eall_scr[lh].T
    publish = pltpu.make_async_copy(idv_scr, ids_smem.at[rslot], idsem)
    publish.start()
    publish.wait()


def _peer(h1, wq, keys, uv_tab, g2, b2, alpha):
    t, d = h1.shape
    tg = 256
    tt = 8
    ngrp = t // tg
    nq = wq.shape[1]
    kk = PEER_TOPK
    hk = PEER_HEADS * kk
    nlh = tg // LANES
    nslab, sub = uv_tab.shape[1], uv_tab.shape[2]
    pairs = [(0, bb) for bb in range(kk)]
    for a in range(1, 8):
        pairs += [(a, bb) for bb in range(8)]
    pairs += [(a, 0) for a in range(8, kk)]
    real = [(a + 1) * (bb + 1) <= kk for a, bb in pairs]
    flat = [a * kk + bb if ok else kk * kk + r for r, ((a, bb), ok) in enumerate(zip(pairs, real))]
    flat = jnp.broadcast_to(jnp.asarray(flat, jnp.int32)[:, None], (len(pairs), LANES))
    cmask = jnp.broadcast_to(jnp.asarray([0.0 if ok else -jnp.inf for ok in real], F32)[:, None],
                             (len(pairs), LANES))
    const = lambda shape: pl.BlockSpec(shape, lambda i: (0,) * len(shape))
    return pl.pallas_call(
        functools.partial(_peer_kernel, tt=tt, alpha=alpha),
        grid=(ngrp + 2,),
        in_specs=[pl.BlockSpec((tg, d), lambda i: (jnp.maximum(i - 2, 0), 0)),
                  pl.BlockSpec((tg, d), lambda i: (jnp.minimum(i, ngrp - 1), 0)),
                  const((d, nq)), const(keys.shape), const(flat.shape), const(cmask.shape),
                  pl.BlockSpec(memory_space=pl.ANY),
                  const((1, d)), const((1, d))],
        out_specs=pl.BlockSpec((tg, d), lambda i: (jnp.maximum(i - 2, 0), 0)),
        out_shape=jax.ShapeDtypeStruct((t, d), F32),
        scratch_shapes=[pltpu.VMEM((2 * PEER_HEADS, nlh, LANES, LANES), BF16),
                        pltpu.VMEM((2 * kk, LANES), F32), pltpu.VMEM((2 * kk, LANES), jnp.int32),
                        pltpu.VMEM((kk, LANES), F32), pltpu.VMEM((kk, LANES), jnp.int32),
                        pltpu.VMEM((nlh, hk, LANES), jnp.int32),
                        pltpu.VMEM((tg, hk), jnp.int32),
                        pltpu.SMEM((3, tg, hk), jnp.int32),
                        pltpu.VMEM((3, nlh, hk, LANES), F32),
                        pltpu.VMEM((nslab, tt * hk * sub, LANES), F32),
                        pltpu.VMEM((nslab, tt * hk * sub, LANES), F32),
                        pltpu.SemaphoreType.DMA((2,)), pltpu.SemaphoreType.DMA,
                        pltpu.VMEM((2 * tt, d), F32)],
        compiler_params=pltpu.CompilerParams(
            dimension_semantics=("arbitrary",), vmem_limit_bytes=VMEM_LIMIT),
        name="peer",
    )(h1, h1, wq, keys, flat, cmask, uv_tab, g2, b2)


def _layer(h, w_in, conv_w, a_log, dt_bias, norm_w, f_bias, w_out_gdn, w_out_fox, w_o, ln1_g, ln1_b,
           peer_wq, peer_keys, peer_u, peer_v, ln2_g, ln2_b, alpha):
    b, s, d = h.shape
    t = b * s
    qk = GDN_HEADS * GDN_DK
    fw = FX_HEADS * FX_DH
    o_gz = 4 * qk
    o_ga = o_gz
    o_fq = o_ga + 2 * GDN_HEADS
    o_ff = o_fq + 3 * fw
    o_gate = o_ff + FX_HEADS
    w_big = jnp.concatenate([w_in[:, o_gate:], w_in[:, :o_gz]], axis=1).astype(BF16)
    pad = LANES - FX_DH
    w_fox = jnp.pad(w_in[:, o_fq:o_ff].reshape(d, 3 * FX_HEADS, FX_DH), ((0, 0), (0, 0), (0, pad)))
    w_fox = w_fox.reshape(d, 3 * FX_HEADS * LANES).astype(BF16)
    w_out_fox_p = jnp.pad(w_out_fox.reshape(FX_HEADS, FX_DH, d), ((0, 0), (0, pad), (0, 0)))
    w_out_fox_p = w_out_fox_p.reshape(FX_HEADS * LANES, d).astype(BF16)
    n_small = 2 * GDN_HEADS + FX_HEADS
    w_small = jnp.concatenate([w_in[:, o_ga:o_fq], w_in[:, o_ff:o_gate],
                               jnp.zeros((d, LANES - n_small), F32)], axis=1)
    params = jnp.zeros((8, LANES), F32)
    params = params.at[0, :GDN_HEADS].set(a_log).at[1, :GDN_HEADS].set(dt_bias)
    params = params.at[2, 2 * GDN_HEADS:n_small].set(f_bias)

    x2 = h.reshape(t, d)
    proj = _in_proj(x2, w_big, F32, "in_proj")
    pf = _in_proj(x2, w_fox, BF16, "in_proj_fox")
    gexp, bexp, c, ct = _prep(h, w_small, params)
    proj3 = proj.reshape(b, s, proj.shape[1])
    gdn0 = 2 * d // LANES
    oa = _gdn(proj3, gexp, bexp, conv_w, norm_w.reshape(1, LANES),
              (gdn0, gdn0 + GDN_HEADS, gdn0 + 2 * GDN_HEADS, gdn0 + 3 * GDN_HEADS))
    ob = _fox(pf.reshape(b, s, pf.shape[1]), c, ct)
    h1 = _mix(oa.reshape(t, qk), ob.reshape(t, FX_HEADS * LANES), proj, x2,
              w_out_gdn.astype(BF16), w_out_fox_p, w_o.astype(BF16),
              ln1_g.reshape(1, d), ln1_b.reshape(1, d), alpha)
    keys = peer_keys.reshape(2 * PEER_HEADS, PEER_NKEYS, peer_keys.shape[-1]).astype(BF16)
    ne = peer_u.shape[0]
    half = PEER_SUB // 2
    nslab = d // (half * LANES)
    uv_tab = jnp.concatenate([peer_u.reshape(ne, half, nslab, LANES),
                              peer_v.reshape(ne, half, nslab, LANES)], axis=1).transpose(0, 2, 1, 3)
    out = _peer(h1, peer_wq.astype(BF16), keys, uv_tab, ln2_g.reshape(1, d), ln2_b.reshape(1, d), alpha)
    return out.reshape(b, s, d)


def kernel(x, w_in, gdn_conv_w, gdn_a_log, gdn_dt_bias, gdn_norm_w, fox_f_bias, w_out_gdn, w_out_fox, w_o,
           ln1_g, ln1_b, peer_wq, peer_keys, peer_u, peer_v, ln2_g, ln2_b):
    depth = w_in.shape[0]
    alpha = (2.0 * depth) ** 0.25
    h = x
    for l in range(depth):
        h = _layer(h, w_in[l], gdn_conv_w[l], gdn_a_log[l], gdn_dt_bias[l], gdn_norm_w[l], fox_f_bias[l],
                   w_out_gdn[l], w_out_fox[l], w_o[l], ln1_g[l], ln1_b[l], peer_wq[l], peer_keys[l],
                   peer_u[l], peer_v[l], ln2_g[l], ln2_b[l], alpha)
    return h
```

```python
import functools

import jax
import jax.numpy as jnp
from jax import lax
from jax.experimental import pallas as pl
from jax.experimental.pallas import tpu as pltpu

F32 = jnp.float32
BF16 = jnp.bfloat16
HI = lax.Precision.HIGHEST

LANES = 128
CHUNK = 64
GDN_HEADS = 4
GDN_DK = 128
FX_HEADS = 8
FX_DH = 64
PEER_HEADS = 8
PEER_NKEYS = 128
PEER_TOPK = 16
PEER_SUB = 4
LN_EPS = 1e-5
VMEM_LIMIT = 48 * 1024 * 1024


def _dot(a, b, prec=None):
    return jnp.dot(a, b, preferred_element_type=F32, precision=prec)


def _dot_nt(a, b, prec=None):
    return lax.dot_general(a, b, (((1,), (1,)), ((), ())), preferred_element_type=F32, precision=prec)


def _dot_tn(a, b, prec=None):
    return lax.dot_general(a, b, (((0,), (0,)), ((), ())), preferred_element_type=F32, precision=prec)


def _softplus(x):
    return jnp.maximum(x, 0.0) + jnp.log1p(jnp.exp(-jnp.abs(x)))


def _layernorm(z, g, b):
    mu = jnp.mean(z, axis=-1, keepdims=True)
    zc = z - mu
    var = jnp.mean(zc * zc, axis=-1, keepdims=True)
    return zc * lax.rsqrt(var + LN_EPS) * g + b


def _mm_kernel(x_ref, w_ref, o_ref):
    o_ref[...] = _dot(x_ref[...].astype(BF16), w_ref[...]).astype(o_ref.dtype)


def _in_proj(x2, w_big, out_dtype, name):
    t, d = x2.shape
    n = w_big.shape[1]
    tm = min(1024, t)
    tn = 512
    return pl.pallas_call(
        _mm_kernel,
        grid=(t // tm, n // tn),
        in_specs=[pl.BlockSpec((tm, d), lambda i, j: (i, 0)),
                  pl.BlockSpec((d, tn), lambda i, j: (0, j))],
        out_specs=pl.BlockSpec((tm, tn), lambda i, j: (i, j)),
        out_shape=jax.ShapeDtypeStruct((t, n), out_dtype),
        compiler_params=pltpu.CompilerParams(
            dimension_semantics=("parallel", "parallel"), vmem_limit_bytes=VMEM_LIMIT),
        name=name,
    )(x2, w_big)


def _prep_kernel(x_ref, w_ref, par_ref, gexp_ref, bexp_ref, c_ref, ct_ref, carry_ref):
    ts = x_ref.shape[1]

    @pl.when(pl.program_id(1) == 0)
    def _():
        carry_ref[...] = jnp.zeros_like(carry_ref)

    small = _dot(x_ref[0], w_ref[...], HI)
    a_log = par_ref[0:1, :]
    dt_bias = par_ref[1:2, :]
    f_bias = par_ref[2:3, :]
    g = -jnp.exp(a_log) * _softplus(small + dt_bias)
    beta = jax.nn.sigmoid(small)
    lane = lax.broadcasted_iota(jnp.int32, (ts, LANES), 1)
    log_f = jnp.where((lane >= 8) & (lane < 16), -_softplus(-(small + f_bias)), 0.0)
    row = lax.broadcasted_iota(jnp.int32, (ts, ts), 0)
    col = lax.broadcasted_iota(jnp.int32, (ts, ts), 1)
    tril = (row >= col).astype(F32)
    c = _dot(tril, log_f, HI) + carry_ref[...]
    carry_ref[...] = c[ts - 1:ts, :]
    c_ref[0] = c
    ct_ref[0] = c.T[8:16, :]
    gexp_ref[0] = jnp.concatenate(
        [jnp.broadcast_to(g[:, h:h + 1], (ts, LANES)) for h in range(GDN_HEADS)], axis=1)
    bexp_ref[0] = jnp.concatenate(
        [jnp.broadcast_to(beta[:, GDN_HEADS + h:GDN_HEADS + h + 1], (ts, LANES)) for h in range(GDN_HEADS)], axis=1)


def _prep(x, w_small, params):
    b, s, d = x.shape
    ts = min(512, s)
    hw = GDN_HEADS * LANES
    return pl.pallas_call(
        _prep_kernel,
        grid=(b, s // ts),
        in_specs=[pl.BlockSpec((1, ts, d), lambda i, j: (i, j, 0)),
                  pl.BlockSpec((d, LANES), lambda i, j: (0, 0)),
                  pl.BlockSpec((8, LANES), lambda i, j: (0, 0))],
        out_specs=[pl.BlockSpec((1, ts, hw), lambda i, j: (i, j, 0)),
                   pl.BlockSpec((1, ts, hw), lambda i, j: (i, j, 0)),
                   pl.BlockSpec((1, ts, LANES), lambda i, j: (i, j, 0)),
                   pl.BlockSpec((1, 8, ts), lambda i, j: (i, 0, j))],
        out_shape=[jax.ShapeDtypeStruct((b, s, hw), F32),
                   jax.ShapeDtypeStruct((b, s, hw), F32),
                   jax.ShapeDtypeStruct((b, s, LANES), F32),
                   jax.ShapeDtypeStruct((b, 8, s), F32)],
        scratch_shapes=[pltpu.VMEM((1, LANES), F32)],
        compiler_params=pltpu.CompilerParams(
            dimension_semantics=("parallel", "arbitrary"), vmem_limit_bytes=VMEM_LIMIT),
        name="prep",
    )(x, w_small, params)


def _each(fn, *lists):
    return [fn(*args) for args in zip(*lists)]


def _unit_lower_inverse(ms, masks):
    eye, blk16, blk32 = masks
    hi = lambda a, b: _dot(a, b, HI)
    n1 = _each(lambda m: -jnp.where(blk16, m, 0.0), ms)
    l1 = _each(lambda m: jnp.where(blk32 & jnp.logical_not(blk16), m, 0.0), ms)
    l2 = _each(lambda m: jnp.where(blk32, 0.0, m), ms)
    n2 = _each(hi, n1, n1)
    p = _each(lambda a, b: hi(eye + a, eye + b), n1, n2)
    n4 = _each(hi, n2, n2)
    p = _each(lambda a, b: hi(a, eye + b), p, n4)
    n8 = _each(hi, n4, n4)
    d_inv = _each(lambda a, b: hi(a, eye + b), p, n8)
    dl = _each(hi, d_inv, l1)
    a32 = _each(lambda a, b: a - hi(b, a), d_inv, dl)
    al = _each(hi, a32, l2)
    return _each(lambda a, b: a - hi(b, a), a32, al)


def _gdn_kernel(q_ref, k_ref, v_ref, z_ref, g_ref, b_ref, cwq_ref, cwk_ref, cwv_ref, nw_ref,
                o_ref, qn, kn, vn, st, sol_s, qk_s, qg_s, kd_s, gl_s):
    s = q_ref.shape[1]
    c = CHUNK
    nh = q_ref.shape[2] // LANES
    row = lax.broadcasted_iota(jnp.int32, (s, LANES), 0)

    def conv_silu(x, w):
        y = x * w[3:4, :]
        for sh in (1, 2, 3):
            xs = jnp.where(row >= sh, pltpu.roll(x, sh, axis=0), 0.0)
            y = y + xs * w[3 - sh:4 - sh, :]
        return y * jax.nn.sigmoid(y)

    def l2norm(x):
        return x * lax.rsqrt(jnp.sum(x * x, axis=-1, keepdims=True) + 1e-6)

    for j in range(nh):
        hs = slice(j * LANES, (j + 1) * LANES)
        qn[:, hs] = l2norm(conv_silu(q_ref[0, :, hs], cwq_ref[:, hs])) * (GDN_DK ** -0.5)
        kn[:, hs] = l2norm(conv_silu(k_ref[0, :, hs], cwk_ref[:, hs]))
        vn[:, hs] = conv_silu(v_ref[0, :, hs], cwv_ref[:, hs])
    st[...] = jnp.zeros_like(st)

    ri = lax.broadcasted_iota(jnp.int32, (c, c), 0)
    ci = lax.broadcasted_iota(jnp.int32, (c, c), 1)
    tril = ri >= ci
    strict = ri > ci
    t_inc = tril.astype(F32)
    eye = (ri == ci).astype(F32)
    blk16 = (ri >> 4) == (ci >> 4)
    blk32 = (ri >> 5) == (ci >> 5)
    l2 = lax.broadcasted_iota(jnp.int32, (c, 2 * LANES), 0)
    j2 = lax.broadcasted_iota(jnp.int32, (c, 2 * LANES), 1)
    ux = jnp.where((j2 >= c) | (l2 > j2), 1.0, 0.0).astype(F32)
    nw = nw_ref[...]

    heads = list(range(nh))
    lanes_of = [slice(j * LANES, (j + 1) * LANES) for j in heads]

    def local_stage(ns):
        pairs = [(i, j) for i in range(len(ns)) for j in heads]
        rows = [pl.ds(pl.multiple_of(n * c, c), c) for n in ns]
        q = [qn[rows[i], lanes_of[j]] for i, j in pairs]
        k = [kn[rows[i], lanes_of[j]] for i, j in pairs]
        v = [vn[rows[i], lanes_of[j]] for i, j in pairs]
        gb = [g_ref[0, rows[i], lanes_of[j]] for i, j in pairs]
        bb = [b_ref[0, rows[i], lanes_of[j]] for i, j in pairs]
        d = _each(lambda g: _dot(t_inc, jnp.concatenate([g, g], axis=1) * ux, HI), gb)
        kb = _each(lambda a, b: a * b, k, bb)
        kk = _each(lambda a, b: _dot_nt(a, b, HI), kb, k)
        qk = _each(lambda a, b: _dot_nt(a.astype(BF16), b.astype(BF16)), q, k)
        gc = [x[:, LANES:] for x in d]
        decay = [jnp.where(tril, jnp.exp(x[:, :c]), 0.0) for x in d]
        m = _each(lambda a, b: jnp.where(strict, a * b, 0.0), kk, decay)
        a_inv = _unit_lower_inverse(m, (eye, blk16, blk32))
        egc = _each(jnp.exp, gc)
        rhs = _each(lambda vv, b, kbb, e: jnp.concatenate([vv * b, kbb * e], axis=1), v, bb, kb, egc)
        sol = _each(lambda a, r: _dot(a, r, HI), a_inv, rhs)
        gl = [x[c - 1:c, :] for x in gc]
        for p, (i, j) in enumerate(pairs):
            n = ns[i]
            qk_s[n, j] = (qk[p] * decay[p]).astype(BF16)
            qg_s[n, j] = (q[p] * egc[p]).astype(BF16)
            kd_s[n, j] = (k[p] * jnp.exp(gl[p] - gc[p])).astype(BF16)
            gl_s[n, j] = gl[p]
            sol_s[n, j] = sol[p]

    def state_stage(n):
        rows = pl.ds(pl.multiple_of(n * c, c), c)
        state = [st[j] for j in heads]
        state_b = [x.astype(BF16) for x in state]
        v_new = [sol_s[n, j, :, :LANES] - _dot(sol_s[n, j, :, LANES:].astype(BF16), state_b[j])
                 for j in heads]
        v_new_b = [x.astype(BF16) for x in v_new]
        o = [_dot(qg_s[n, j], state_b[j]) + _dot(qk_s[n, j], v_new_b[j]) for j in heads]
        new_state = [state[j] * jnp.exp(gl_s[n, j]) + _dot_tn(kd_s[n, j], v_new_b[j]) for j in heads]
        for j in heads:
            st[j] = new_state[j]
            on = o[j] * lax.rsqrt(jnp.mean(o[j] * o[j], axis=-1, keepdims=True) + 1e-6) * nw
            z = z_ref[0, rows, lanes_of[j]]
            o_ref[0, rows, lanes_of[j]] = on * (z * jax.nn.sigmoid(z))

    nchunks = s // c
    group = 4

    def local_group(gidx, carry):
        local_stage([gidx * group + i for i in range(group)])
        return carry

    def state_chunk(n, carry):
        state_stage(n)
        return carry

    lax.fori_loop(0, nchunks // group, local_group, 0)
    lax.fori_loop(0, nchunks, state_chunk, 0)


def _gdn(proj3, gexp, bexp, conv_w, norm_w, cols):
    b, s, _ = proj3.shape
    nh = 2
    nc = s // CHUNK
    wd = nh * LANES
    cq, ck, cv, cz = (c0 // nh for c0 in cols)

    def blk(c0):
        return pl.BlockSpec((1, s, wd), lambda i, h: (i, 0, c0 + h))

    def cw(c0):
        return pl.BlockSpec((conv_w.shape[0], wd), lambda i, h: (0, c0 + h))

    head = pl.BlockSpec((1, s, wd), lambda i, h: (i, 0, h))
    return pl.pallas_call(
        _gdn_kernel,
        grid=(b, GDN_HEADS // nh),
        in_specs=[blk(cq), blk(ck), blk(cv), blk(cz), head, head,
                  cw(0), cw(GDN_HEADS // nh), cw(2 * GDN_HEADS // nh),
                  pl.BlockSpec((1, LANES), lambda i, h: (0, 0))],
        out_specs=head,
        out_shape=jax.ShapeDtypeStruct((b, s, GDN_HEADS * LANES), F32),
        scratch_shapes=[pltpu.VMEM((s, wd), F32), pltpu.VMEM((s, wd), F32),
                        pltpu.VMEM((s, wd), F32), pltpu.VMEM((nh, GDN_DK, LANES), F32),
                        pltpu.VMEM((nc, nh, CHUNK, 2 * LANES), F32), pltpu.VMEM((nc, nh, CHUNK, CHUNK), BF16),
                        pltpu.VMEM((nc, nh, CHUNK, LANES), BF16), pltpu.VMEM((nc, nh, CHUNK, LANES), BF16),
                        pltpu.VMEM((nc, nh, 1, LANES), F32)],
        compiler_params=pltpu.CompilerParams(
            dimension_semantics=("parallel", "parallel"), vmem_limit_bytes=VMEM_LIMIT),
        name="gdn",
    )(proj3, proj3, proj3, proj3, gexp, bexp, conv_w, conv_w, conv_w, norm_w)


def _fox_kernel(q_ref, k_ref, v_ref, c_ref, ct_ref, o_ref, *, tk):
    tq = q_ref.shape[1]
    nj = q_ref.shape[2] // LANES
    g = pl.program_id(1)
    qi = pl.program_id(2)
    q = q_ref[0]
    cblk = c_ref[0]
    lane = lax.broadcasted_iota(jnp.int32, (tq, LANES), 1)
    qpos = qi * tq + lax.broadcasted_iota(jnp.int32, (tq, tk), 0)
    kofs = lax.broadcasted_iota(jnp.int32, (tq, tk), 1)
    heads = list(range(nj))
    lanes_of = [slice(j * LANES, (j + 1) * LANES) for j in heads]
    ccol = [jnp.sum(jnp.where(lane == 8 + g * nj + j, cblk, 0.0), axis=-1, keepdims=True) for j in heads]
    qh = [q[:, hs] * jnp.asarray(FX_DH ** -0.5, BF16) for hs in lanes_of]

    def body(kv, carry):
        k0 = pl.multiple_of(kv * tk, tk)
        causal = qpos >= k0 + kofs
        kblk = k_ref[0, pl.ds(k0, tk), :]
        vblk = v_ref[0, pl.ds(k0, tk), :]
        sc = [_dot_nt(qh[j], kblk[:, lanes_of[j]]) for j in heads]
        crow = [ct_ref[0, pl.ds(g * nj + j, 1), pl.ds(k0, tk)] for j in heads]
        sc = [jnp.where(causal, sc[j] + ccol[j] - crow[j], -1e30) for j in heads]
        m_new = [jnp.maximum(carry[j][0], jnp.max(sc[j], axis=-1, keepdims=True)) for j in heads]
        a = [jnp.exp(carry[j][0] - m_new[j]) for j in heads]
        p = [jnp.exp(sc[j] - m_new[j]) for j in heads]
        l = [a[j] * carry[j][1] + jnp.sum(p[j], axis=-1, keepdims=True) for j in heads]
        acc = [a[j] * carry[j][2] + _dot(p[j].astype(BF16), vblk[:, lanes_of[j]]) for j in heads]
        return tuple((m_new[j], l[j], acc[j]) for j in heads)

    init = tuple((jnp.full((tq, 1), -1e30, F32), jnp.zeros((tq, 1), F32), jnp.zeros((tq, LANES), F32))
                 for _ in heads)
    nkv = (qi * tq + tq - 1) // tk + 1
    res = lax.fori_loop(0, nkv, body, init)
    o_ref[0] = jnp.concatenate([acc / l for _, l, acc in res], axis=1)


def _fox(pf3, c, ct):
    b, s, _ = pf3.shape
    wd = 4 * LANES
    tq = min(128, s)
    tk = min(256, s)
    ngrp = FX_HEADS * LANES // wd
    return pl.pallas_call(
        functools.partial(_fox_kernel, tk=tk),
        grid=(b, ngrp, s // tq),
        in_specs=[pl.BlockSpec((1, tq, wd), lambda i, h, t: (i, t, h)),
                  pl.BlockSpec((1, s, wd), lambda i, h, t: (i, 0, ngrp + h)),
                  pl.BlockSpec((1, s, wd), lambda i, h, t: (i, 0, 2 * ngrp + h)),
                  pl.BlockSpec((1, tq, LANES), lambda i, h, t: (i, t, 0)),
                  pl.BlockSpec((1, 8, s), lambda i, h, t: (i, 0, 0))],
        out_specs=pl.BlockSpec((1, tq, wd), lambda i, h, t: (i, t, h)),
        out_shape=jax.ShapeDtypeStruct((b, s, ngrp * wd), F32),
        compiler_params=pltpu.CompilerParams(
            dimension_semantics=("parallel", "parallel", "parallel"), vmem_limit_bytes=VMEM_LIMIT),
        name="fox",
    )(pf3, pf3, pf3, c, ct)


def _mix_kernel(oa_ref, ob_ref, ga_ref, gb_ref, x_ref, wa_ref, wb_ref, wo_ref, g1_ref, b1_ref, o_ref, *, alpha):
    ya = _dot(oa_ref[...].astype(BF16), wa_ref[...])
    yb = _dot(ob_ref[...].astype(BF16), wb_ref[...])
    mix = jax.nn.sigmoid(ga_ref[...]) * ya + jax.nn.sigmoid(gb_ref[...]) * yb
    z = alpha * x_ref[...] + _dot(mix.astype(BF16), wo_ref[...])
    o_ref[...] = _layernorm(z, g1_ref[...], b1_ref[...])


def _mix(oa, ob, proj, x2, wa, wb, wo, g1, b1, alpha):
    t, d = x2.shape
    tm = min(512, t)
    w, w2 = oa.shape[1], ob.shape[1]
    full = lambda r, c: pl.BlockSpec((r, c), lambda i: (0, 0))
    return pl.pallas_call(
        functools.partial(_mix_kernel, alpha=alpha),
        grid=(t // tm,),
        in_specs=[pl.BlockSpec((tm, w), lambda i: (i, 0)),
                  pl.BlockSpec((tm, w2), lambda i: (i, 0)),
                  pl.BlockSpec((tm, d), lambda i: (i, 0)),
                  pl.BlockSpec((tm, d), lambda i: (i, 1)),
                  pl.BlockSpec((tm, d), lambda i: (i, 0)),
                  full(w, d), full(w2, d), full(d, d), full(1, d), full(1, d)],
        out_specs=pl.BlockSpec((tm, d), lambda i: (i, 0)),
        out_shape=jax.ShapeDtypeStruct((t, d), F32),
        compiler_params=pltpu.CompilerParams(
            dimension_semantics=("parallel",), vmem_limit_bytes=VMEM_LIMIT),
        name="mix",
    )(oa, ob, proj, proj, x2, wa, wb, wo, g1, b1)


def _peer_kernel(h_cur_ref, h_nxt_ref, wq_ref, keys_ref, flat_ref, cmask_ref, uv_hbm, g2_ref, b2_ref, o_ref,
                 q_scr, s_scr, i_scr, best_scr, eh_scr, eall_scr, idv_scr, ids_smem, gate_scr,
                 buf_a, buf_b, sem, idsem, y_scr, *, tt, alpha):
    s = pl.program_id(0)
    tg, d = h_cur_ref.shape
    nk, kk = PEER_NKEYS, PEER_TOPK
    hk = PEER_HEADS * kk
    nlh = tg // LANES
    nsub = tg // (2 * tt)
    assert nsub == PEER_HEADS * nlh
    nslab, sub = uv_hbm.shape[1], uv_hbm.shape[2]
    half = sub // 2
    bufs = (buf_a, buf_b)
    nsem = sem.shape[1]
    last = pl.num_programs(0) - 1
    rslot = s % 3
    pslot = (s + 2) % 3
    eslot = (s + 1) % 3

    def issue(idslot, row0, slot):
        for t in range(tt):
            for k in range(hk):
                e = ids_smem[idslot, row0 + t, k]
                pltpu.make_async_copy(uv_hbm.at[e], bufs[slot].at[:, pl.ds((t * hk + k) * sub, sub), :],
                                      sem.at[slot, k % nsem]).start(priority=k % 2)

    def wait(slot):
        share = bufs[slot].at[:, pl.ds(0, tt * hk * sub // nsem), :]
        for i in range(nsem):
            pltpu.make_async_copy(share, share, sem.at[slot, i]).wait()

    q = _dot(h_nxt_ref[...].astype(BF16), wq_ref[...])
    for j in range(2 * PEER_HEADS):
        for lh in range(nlh):
            q_scr[j, lh] = q[lh * LANES:(lh + 1) * LANES, j * LANES:(j + 1) * LANES].astype(BF16)
    iota_k = lax.broadcasted_iota(jnp.int32, (nk, LANES), 0)
    flat = flat_ref[...]
    cmask = cmask_ref[...]
    neg = jnp.float32(-jnp.inf)

    def route_piece(piece):
        hd = piece // nlh
        lh = piece % nlh
        for p in range(2):
            vals = _dot_nt(keys_ref[hd * 2 + p], q_scr[hd * 2 + p, lh])
            for r in range(kk):
                m = jnp.max(vals, axis=0, keepdims=True)
                am = jnp.min(jnp.where(vals == m, iota_k, nk), axis=0, keepdims=True)
                s_scr[p * kk + r:p * kk + r + 1, :] = m
                i_scr[p * kk + r:p * kk + r + 1, :] = am
                vals = jnp.where(iota_k == am, neg, vals)
        s1 = s_scr[kk:kk + 8, :]
        i1 = i_scr[kk:kk + 8, :]
        cand = [s_scr[0:1, :] + s_scr[kk:2 * kk, :]]
        cidx = [i_scr[0:1, :] * nk + i_scr[kk:2 * kk, :]]
        for a in range(1, 8):
            cand.append(s_scr[a:a + 1, :] + s1)
            cidx.append(i_scr[a:a + 1, :] * nk + i1)
        cand.append(s_scr[8:kk, :] + s_scr[kk:kk + 1, :])
        cidx.append(i_scr[8:kk, :] * nk + i_scr[kk:kk + 1, :])
        vals = jnp.concatenate(cand, axis=0) + cmask
        cidx = jnp.concatenate(cidx, axis=0)
        for r in range(kk):
            m = jnp.max(vals, axis=0, keepdims=True)
            am = jnp.min(jnp.where(vals == m, flat, 2 * kk * kk), axis=0, keepdims=True)
            sel = flat == am
            best_scr[r:r + 1, :] = m
            eh_scr[r:r + 1, :] = jnp.max(jnp.where(sel, cidx, -1), axis=0, keepdims=True)
            vals = jnp.where(sel, neg, vals)
        bs = best_scr[...]
        ex = jnp.exp(bs - bs[0:1, :])
        r0 = pl.multiple_of(hd * kk, kk)
        gate_scr[rslot, lh, pl.ds(r0, kk), :] = ex / jnp.sum(ex, axis=0, keepdims=True)
        eall_scr[lh, pl.ds(r0, kk), :] = eh_scr[...]

    lane = lax.broadcasted_iota(jnp.int32, (hk, LANES), 1)

    def compute(row0, slot):
        buf = bufs[slot]
        gt = gate_scr[eslot, row0 // LANES]
        lane0 = row0 % LANES
        for t in range(tt):
            hrow = h_cur_ref[pl.ds(row0 + t, 1), :]

            def rows(c, j):
                return buf[c, pl.ds(t * hk * sub + j, hk, stride=sub), :]

            part = None
            for j in range(half):
                for c in range(nslab):
                    seg = j * nslab + c
                    term = rows(c, j) * hrow[:, seg * LANES:(seg + 1) * LANES]
                    part = term if part is None else part + term
            pre = jnp.sum(part, axis=-1, keepdims=True)
            gate = jnp.sum(jnp.where(lane == lane0 + t, gt, 0.0), axis=-1, keepdims=True)
            act = 0.5 * pre * (1.0 + lax.erf(pre * (2.0 ** -0.5))) * gate
            yrow = slot * tt + t
            for j in range(half):
                for c in range(nslab):
                    seg = j * nslab + c
                    y_scr[yrow:yrow + 1, seg * LANES:(seg + 1) * LANES] = jnp.sum(
                        act * rows(c, half + j), axis=0, keepdims=True)

    def substep(j, carry):
        row0 = pl.multiple_of(j * 2 * tt, 2 * tt)
        issue(eslot, row0 + tt, 1)
        route_piece(j)
        wait(0)
        compute(row0, 0)
        wait(1)
        wrap = j == nsub - 1
        issue(jnp.where(wrap, pslot, eslot), jnp.where(wrap, 0, row0 + 2 * tt), 0)
        compute(row0 + tt, 1)
        z = alpha * h_cur_ref[pl.ds(row0, 2 * tt), :] + y_scr[...]
        o_ref[pl.ds(row0, 2 * tt), :] = _layernorm(z, g2_ref[...], b2_ref[...])
        return carry

    @pl.when(s < 2)
    def _():
        o_ref[...] = jnp.zeros_like(o_ref)

        def piece(p, carry):
            route_piece(p)
            return carry
        lax.fori_loop(0, nsub, piece, 0)

    @pl.when(s == 1)
    def _():
        issue(pslot, 0, 0)

    @pl.when(s >= 2)
    def _():
        lax.fori_loop(0, nsub, substep, 0)

    @pl.when(s == last)
    def _():
        wait(0)

    for lh in range(nlh):
        idv_scr[lh * LANES:(lh + 1) * LANES, :] = eall_scr[lh].T
    publish = pltpu.make_async_copy(idv_scr, ids_smem.at[rslot], idsem)
    publish.start()
    publish.wait()


def _peer(h1, wq, keys, uv_tab, g2, b2, alpha):
    t, d = h1.shape
    tg = 256
    tt = 8
    ngrp = t // tg
    nq = wq.shape[1]
    kk = PEER_TOPK
    hk = PEER_HEADS * kk
    nlh = tg // LANES
    nslab, sub = uv_tab.shape[1], uv_tab.shape[2]
    pairs = [(0, bb) for bb in range(kk)]
    for a in range(1, 8):
        pairs += [(a, bb) for bb in range(8)]
    pairs += [(a, 0) for a in range(8, kk)]
    real = [(a + 1) * (bb + 1) <= kk for a, bb in pairs]
    flat = [a * kk + bb if ok else kk * kk + r for r, ((a, bb), ok) in enumerate(zip(pairs, real))]
    flat = jnp.broadcast_to(jnp.asarray(flat, jnp.int32)[:, None], (len(pairs), LANES))
    cmask = jnp.broadcast_to(jnp.asarray([0.0 if ok else -jnp.inf for ok in real], F32)[:, None],
                             (len(pairs), LANES))
    const = lambda shape: pl.BlockSpec(shape, lambda i: (0,) * len(shape))
    return pl.pallas_call(
        functools.partial(_peer_kernel, tt=tt, alpha=alpha),
        grid=(ngrp + 2,),
        in_specs=[pl.BlockSpec((tg, d), lambda i: (jnp.maximum(i - 2, 0), 0)),
                  pl.BlockSpec((tg, d), lambda i: (jnp.minimum(i, ngrp - 1), 0)),
                  const((d, nq)), const(keys.shape), const(flat.shape), const(cmask.shape),
                  pl.BlockSpec(memory_space=pl.ANY),
                  const((1, d)), const((1, d))],
        out_specs=pl.BlockSpec((tg, d), lambda i: (jnp.maximum(i - 2, 0), 0)),
        out_shape=jax.ShapeDtypeStruct((t, d), F32),
        scratch_shapes=[pltpu.VMEM((2 * PEER_HEADS, nlh, LANES, LANES), BF16),
                        pltpu.VMEM((2 * kk, LANES), F32), pltpu.VMEM((2 * kk, LANES), jnp.int32),
                        pltpu.VMEM((kk, LANES), F32), pltpu.VMEM((kk, LANES), jnp.int32),
                        pltpu.VMEM((nlh, hk, LANES), jnp.int32),
                        pltpu.VMEM((tg, hk), jnp.int32),
                        pltpu.SMEM((3, tg, hk), jnp.int32),
                        pltpu.VMEM((3, nlh, hk, LANES), F32),
                        pltpu.VMEM((nslab, tt * hk * sub, LANES), F32),
                        pltpu.VMEM((nslab, tt * hk * sub, LANES), F32),
                        pltpu.SemaphoreType.DMA((2, 8)), pltpu.SemaphoreType.DMA,
                        pltpu.VMEM((2 * tt, d), F32)],
        compiler_params=pltpu.CompilerParams(
            dimension_semantics=("arbitrary",), vmem_limit_bytes=VMEM_LIMIT),
        name="peer",
    )(h1, h1, wq, keys, flat, cmask, uv_tab, g2, b2)


def _layer(h, w_in, conv_w, a_log, dt_bias, norm_w, f_bias, w_out_gdn, w_out_fox, w_o, ln1_g, ln1_b,
           peer_wq, peer_keys, peer_u, peer_v, ln2_g, ln2_b, alpha):
    b, s, d = h.shape
    t = b * s
    qk = GDN_HEADS * GDN_DK
    fw = FX_HEADS * FX_DH
    o_gz = 4 * qk
    o_ga = o_gz
    o_fq = o_ga + 2 * GDN_HEADS
    o_ff = o_fq + 3 * fw
    o_gate = o_ff + FX_HEADS
    w_big = jnp.concatenate([w_in[:, o_gate:], w_in[:, :o_gz]], axis=1).astype(BF16)
    pad = LANES - FX_DH
    w_fox = jnp.pad(w_in[:, o_fq:o_ff].reshape(d, 3 * FX_HEADS, FX_DH), ((0, 0), (0, 0), (0, pad)))
    w_fox = w_fox.reshape(d, 3 * FX_HEADS * LANES).astype(BF16)
    w_out_fox_p = jnp.pad(w_out_fox.reshape(FX_HEADS, FX_DH, d), ((0, 0), (0, pad), (0, 0)))
    w_out_fox_p = w_out_fox_p.reshape(FX_HEADS * LANES, d).astype(BF16)
    n_small = 2 * GDN_HEADS + FX_HEADS
    w_small = jnp.concatenate([w_in[:, o_ga:o_fq], w_in[:, o_ff:o_gate],
                               jnp.zeros((d, LANES - n_small), F32)], axis=1)
    params = jnp.zeros((8, LANES), F32)
    params = params.at[0, :GDN_HEADS].set(a_log).at[1, :GDN_HEADS].set(dt_bias)
    params = params.at[2, 2 * GDN_HEADS:n_small].set(f_bias)

    x2 = h.reshape(t, d)
    proj = _in_proj(x2, w_big, F32, "in_proj")
    pf = _in_proj(x2, w_fox, BF16, "in_proj_fox")
    gexp, bexp, c, ct = _prep(h, w_small, params)
    proj3 = proj.reshape(b, s, proj.shape[1])
    gdn0 = 2 * d // LANES
    oa = _gdn(proj3, gexp, bexp, conv_w, norm_w.reshape(1, LANES),
              (gdn0, gdn0 + GDN_HEADS, gdn0 + 2 * GDN_HEADS, gdn0 + 3 * GDN_HEADS))
    ob = _fox(pf.reshape(b, s, pf.shape[1]), c, ct)
    h1 = _mix(oa.reshape(t, qk), ob.reshape(t, FX_HEADS * LANES), proj, x2,
              w_out_gdn.astype(BF16), w_out_fox_p, w_o.astype(BF16),
              ln1_g.reshape(1, d), ln1_b.reshape(1, d), alpha)
    keys = peer_keys.reshape(2 * PEER_HEADS, PEER_NKEYS, peer_keys.shape[-1]).astype(BF16)
    ne = peer_u.shape[0]
    half = PEER_SUB // 2
    nslab = d // (half * LANES)
    uv_tab = jnp.concatenate([peer_u.reshape(ne, half, nslab, LANES),
                              peer_v.reshape(ne, half, nslab, LANES)], axis=1).transpose(0, 2, 1, 3)
    out = _peer(h1, peer_wq.astype(BF16), keys, uv_tab, ln2_g.reshape(1, d), ln2_b.reshape(1, d), alpha)
    return out.reshape(b, s, d)


def kernel(x, w_in, gdn_conv_w, gdn_a_log, gdn_dt_bias, gdn_norm_w, fox_f_bias, w_out_gdn, w_out_fox, w_o,
           ln1_g, ln1_b, peer_wq, peer_keys, peer_u, peer_v, ln2_g, ln2_b):
    depth = w_in.shape[0]
    alpha = (2.0 * depth) ** 0.25
    h = x
    for l in range(depth):
        h = _layer(h, w_in[l], gdn_conv_w[l], gdn_a_log[l], gdn_dt_bias[l], gdn_norm_w[l], fox_f_bias[l],
                   w_out_gdn[l], w_out_fox[l], w_o[l], ln1_g[l], ln1_b[l], peer_wq[l], peer_keys[l],
                   peer_u[l], peer_v[l], ln2_g[l], ln2_b[l], alpha)
    return h
```

```python
import functools

import jax
import jax.numpy as jnp
from jax import lax
from jax.experimental import pallas as pl
from jax.experimental.pallas import tpu as pltpu

F32 = jnp.float32
BF16 = jnp.bfloat16
HI = lax.Precision.HIGHEST

LANES = 128
CHUNK = 64
GDN_HEADS = 4
GDN_DK = 128
FX_HEADS = 8
FX_DH = 64
PEER_HEADS = 8
PEER_NKEYS = 128
PEER_TOPK = 16
PEER_SUB = 4
LN_EPS = 1e-5
VMEM_LIMIT = 48 * 1024 * 1024


def _dot(a, b, prec=None):
    return jnp.dot(a, b, preferred_element_type=F32, precision=prec)


def _dot_nt(a, b, prec=None):
    return lax.dot_general(a, b, (((1,), (1,)), ((), ())), preferred_element_type=F32, precision=prec)


def _dot_tn(a, b, prec=None):
    return lax.dot_general(a, b, (((0,), (0,)), ((), ())), preferred_element_type=F32, precision=prec)


def _softplus(x):
    return jnp.maximum(x, 0.0) + jnp.log1p(jnp.exp(-jnp.abs(x)))


def _layernorm(z, g, b):
    mu = jnp.mean(z, axis=-1, keepdims=True)
    zc = z - mu
    var = jnp.mean(zc * zc, axis=-1, keepdims=True)
    return zc * lax.rsqrt(var + LN_EPS) * g + b


def _mm_kernel(x_ref, w_ref, o_ref):
    o_ref[...] = _dot(x_ref[...].astype(BF16), w_ref[...]).astype(o_ref.dtype)


def _in_proj(x2, w_big, out_dtype, name):
    t, d = x2.shape
    n = w_big.shape[1]
    tm = min(1024, t)
    tn = 512
    return pl.pallas_call(
        _mm_kernel,
        grid=(t // tm, n // tn),
        in_specs=[pl.BlockSpec((tm, d), lambda i, j: (i, 0)),
                  pl.BlockSpec((d, tn), lambda i, j: (0, j))],
        out_specs=pl.BlockSpec((tm, tn), lambda i, j: (i, j)),
        out_shape=jax.ShapeDtypeStruct((t, n), out_dtype),
        compiler_params=pltpu.CompilerParams(
            dimension_semantics=("parallel", "parallel"), vmem_limit_bytes=VMEM_LIMIT),
        name=name,
    )(x2, w_big)


def _prep_kernel(x_ref, w_ref, par_ref, gexp_ref, bexp_ref, c_ref, ct_ref, carry_ref):
    ts = x_ref.shape[1]

    @pl.when(pl.program_id(1) == 0)
    def _():
        carry_ref[...] = jnp.zeros_like(carry_ref)

    small = _dot(x_ref[0], w_ref[...], HI)
    a_log = par_ref[0:1, :]
    dt_bias = par_ref[1:2, :]
    f_bias = par_ref[2:3, :]
    g = -jnp.exp(a_log) * _softplus(small + dt_bias)
    beta = jax.nn.sigmoid(small)
    lane = lax.broadcasted_iota(jnp.int32, (ts, LANES), 1)
    log_f = jnp.where((lane >= 8) & (lane < 16), -_softplus(-(small + f_bias)), 0.0)
    row = lax.broadcasted_iota(jnp.int32, (ts, ts), 0)
    col = lax.broadcasted_iota(jnp.int32, (ts, ts), 1)
    tril = (row >= col).astype(F32)
    c = _dot(tril, log_f, HI) + carry_ref[...]
    carry_ref[...] = c[ts - 1:ts, :]
    c_ref[0] = c
    ct_ref[0] = c.T[8:16, :]
    gexp_ref[0] = jnp.concatenate(
        [jnp.broadcast_to(g[:, h:h + 1], (ts, LANES)) for h in range(GDN_HEADS)], axis=1)
    bexp_ref[0] = jnp.concatenate(
        [jnp.broadcast_to(beta[:, GDN_HEADS + h:GDN_HEADS + h + 1], (ts, LANES)) for h in range(GDN_HEADS)], axis=1)


def _prep(x, w_small, params):
    b, s, d = x.shape
    ts = min(512, s)
    hw = GDN_HEADS * LANES
    return pl.pallas_call(
        _prep_kernel,
        grid=(b, s // ts),
        in_specs=[pl.BlockSpec((1, ts, d), lambda i, j: (i, j, 0)),
                  pl.BlockSpec((d, LANES), lambda i, j: (0, 0)),
                  pl.BlockSpec((8, LANES), lambda i, j: (0, 0))],
        out_specs=[pl.BlockSpec((1, ts, hw), lambda i, j: (i, j, 0)),
                   pl.BlockSpec((1, ts, hw), lambda i, j: (i, j, 0)),
                   pl.BlockSpec((1, ts, LANES), lambda i, j: (i, j, 0)),
                   pl.BlockSpec((1, 8, ts), lambda i, j: (i, 0, j))],
        out_shape=[jax.ShapeDtypeStruct((b, s, hw), F32),
                   jax.ShapeDtypeStruct((b, s, hw), F32),
                   jax.ShapeDtypeStruct((b, s, LANES), F32),
                   jax.ShapeDtypeStruct((b, 8, s), F32)],
        scratch_shapes=[pltpu.VMEM((1, LANES), F32)],
        compiler_params=pltpu.CompilerParams(
            dimension_semantics=("parallel", "arbitrary"), vmem_limit_bytes=VMEM_LIMIT),
        name="prep",
    )(x, w_small, params)


def _each(fn, *lists):
    return [fn(*args) for args in zip(*lists)]


def _unit_lower_inverse(ms, masks):
    eye, blk16, blk32 = masks
    hi = lambda a, b: _dot(a, b, HI)
    n1 = _each(lambda m: -jnp.where(blk16, m, 0.0), ms)
    l1 = _each(lambda m: jnp.where(blk32 & jnp.logical_not(blk16), m, 0.0), ms)
    l2 = _each(lambda m: jnp.where(blk32, 0.0, m), ms)
    n2 = _each(hi, n1, n1)
    p = _each(lambda a, b: hi(eye + a, eye + b), n1, n2)
    n4 = _each(hi, n2, n2)
    p = _each(lambda a, b: hi(a, eye + b), p, n4)
    n8 = _each(hi, n4, n4)
    d_inv = _each(lambda a, b: hi(a, eye + b), p, n8)
    dl = _each(hi, d_inv, l1)
    a32 = _each(lambda a, b: a - hi(b, a), d_inv, dl)
    al = _each(hi, a32, l2)
    return _each(lambda a, b: a - hi(b, a), a32, al)


def _gdn_kernel(q_ref, k_ref, v_ref, z_ref, g_ref, b_ref, cwq_ref, cwk_ref, cwv_ref, nw_ref,
                o_ref, qn, kn, vn, st, sol_s, qk_s, qg_s, kd_s, gl_s):
    s = q_ref.shape[1]
    c = CHUNK
    nh = q_ref.shape[2] // LANES
    row = lax.broadcasted_iota(jnp.int32, (s, LANES), 0)

    def conv_silu(x, w):
        y = x * w[3:4, :]
        for sh in (1, 2, 3):
            xs = jnp.where(row >= sh, pltpu.roll(x, sh, axis=0), 0.0)
            y = y + xs * w[3 - sh:4 - sh, :]
        return y * jax.nn.sigmoid(y)

    def l2norm(x):
        return x * lax.rsqrt(jnp.sum(x * x, axis=-1, keepdims=True) + 1e-6)

    for j in range(nh):
        hs = slice(j * LANES, (j + 1) * LANES)
        qn[:, hs] = l2norm(conv_silu(q_ref[0, :, hs], cwq_ref[:, hs])) * (GDN_DK ** -0.5)
        kn[:, hs] = l2norm(conv_silu(k_ref[0, :, hs], cwk_ref[:, hs]))
        vn[:, hs] = conv_silu(v_ref[0, :, hs], cwv_ref[:, hs])
    st[...] = jnp.zeros_like(st)

    ri = lax.broadcasted_iota(jnp.int32, (c, c), 0)
    ci = lax.broadcasted_iota(jnp.int32, (c, c), 1)
    tril = ri >= ci
    strict = ri > ci
    t_inc = tril.astype(F32)
    eye = (ri == ci).astype(F32)
    blk16 = (ri >> 4) == (ci >> 4)
    blk32 = (ri >> 5) == (ci >> 5)
    l2 = lax.broadcasted_iota(jnp.int32, (c, 2 * LANES), 0)
    j2 = lax.broadcasted_iota(jnp.int32, (c, 2 * LANES), 1)
    ux = jnp.where((j2 >= c) | (l2 > j2), 1.0, 0.0).astype(F32)
    nw = nw_ref[...]

    heads = list(range(nh))
    lanes_of = [slice(j * LANES, (j + 1) * LANES) for j in heads]

    def local_stage(ns):
        pairs = [(i, j) for i in range(len(ns)) for j in heads]
        rows = [pl.ds(pl.multiple_of(n * c, c), c) for n in ns]
        q = [qn[rows[i], lanes_of[j]] for i, j in pairs]
        k = [kn[rows[i], lanes_of[j]] for i, j in pairs]
        v = [vn[rows[i], lanes_of[j]] for i, j in pairs]
        gb = [g_ref[0, rows[i], lanes_of[j]] for i, j in pairs]
        bb = [b_ref[0, rows[i], lanes_of[j]] for i, j in pairs]
        d = _each(lambda g: _dot(t_inc, jnp.concatenate([g, g], axis=1) * ux, HI), gb)
        kb = _each(lambda a, b: a * b, k, bb)
        kk = _each(lambda a, b: _dot_nt(a, b, HI), kb, k)
        qk = _each(lambda a, b: _dot_nt(a.astype(BF16), b.astype(BF16)), q, k)
        gc = [x[:, LANES:] for x in d]
        decay = [jnp.where(tril, jnp.exp(x[:, :c]), 0.0) for x in d]
        m = _each(lambda a, b: jnp.where(strict, a * b, 0.0), kk, decay)
        a_inv = _unit_lower_inverse(m, (eye, blk16, blk32))
        egc = _each(jnp.exp, gc)
        rhs = _each(lambda vv, b, kbb, e: jnp.concatenate([vv * b, kbb * e], axis=1), v, bb, kb, egc)
        sol = _each(lambda a, r: _dot(a, r, HI), a_inv, rhs)
        gl = [x[c - 1:c, :] for x in gc]
        for p, (i, j) in enumerate(pairs):
            n = ns[i]
            qk_s[n, j] = (qk[p] * decay[p]).astype(BF16)
            qg_s[n, j] = (q[p] * egc[p]).astype(BF16)
            kd_s[n, j] = (k[p] * jnp.exp(gl[p] - gc[p])).astype(BF16)
            gl_s[n, j] = gl[p]
            sol_s[n, j] = sol[p]

    def state_stage(n):
        rows = pl.ds(pl.multiple_of(n * c, c), c)
        state = [st[j] for j in heads]
        state_b = [x.astype(BF16) for x in state]
        v_new = [sol_s[n, j, :, :LANES] - _dot(sol_s[n, j, :, LANES:].astype(BF16), state_b[j])
                 for j in heads]
        v_new_b = [x.astype(BF16) for x in v_new]
        o = [_dot(qg_s[n, j], state_b[j]) + _dot(qk_s[n, j], v_new_b[j]) for j in heads]
        new_state = [state[j] * jnp.exp(gl_s[n, j]) + _dot_tn(kd_s[n, j], v_new_b[j]) for j in heads]
        for j in heads:
            st[j] = new_state[j]
            on = o[j] * lax.rsqrt(jnp.mean(o[j] * o[j], axis=-1, keepdims=True) + 1e-6) * nw
            z = z_ref[0, rows, lanes_of[j]]
            o_ref[0, rows, lanes_of[j]] = on * (z * jax.nn.sigmoid(z))

    nchunks = s // c
    group = 4

    def local_group(gidx, carry):
        local_stage([gidx * group + i for i in range(group)])
        return carry

    def state_chunk(n, carry):
        state_stage(n)
        return carry

    lax.fori_loop(0, nchunks // group, local_group, 0)
    lax.fori_loop(0, nchunks, state_chunk, 0)


def _gdn(proj3, gexp, bexp, conv_w, norm_w, cols):
    b, s, _ = proj3.shape
    nh = 2
    nc = s // CHUNK
    wd = nh * LANES
    cq, ck, cv, cz = (c0 // nh for c0 in cols)

    def blk(c0):
        return pl.BlockSpec((1, s, wd), lambda i, h: (i, 0, c0 + h))

    def cw(c0):
        return pl.BlockSpec((conv_w.shape[0], wd), lambda i, h: (0, c0 + h))

    head = pl.BlockSpec((1, s, wd), lambda i, h: (i, 0, h))
    return pl.pallas_call(
        _gdn_kernel,
        grid=(b, GDN_HEADS // nh),
        in_specs=[blk(cq), blk(ck), blk(cv), blk(cz), head, head,
                  cw(0), cw(GDN_HEADS // nh), cw(2 * GDN_HEADS // nh),
                  pl.BlockSpec((1, LANES), lambda i, h: (0, 0))],
        out_specs=head,
        out_shape=jax.ShapeDtypeStruct((b, s, GDN_HEADS * LANES), F32),
        scratch_shapes=[pltpu.VMEM((s, wd), F32), pltpu.VMEM((s, wd), F32),
                        pltpu.VMEM((s, wd), F32), pltpu.VMEM((nh, GDN_DK, LANES), F32),
                        pltpu.VMEM((nc, nh, CHUNK, 2 * LANES), F32), pltpu.VMEM((nc, nh, CHUNK, CHUNK), BF16),
                        pltpu.VMEM((nc, nh, CHUNK, LANES), BF16), pltpu.VMEM((nc, nh, CHUNK, LANES), BF16),
                        pltpu.VMEM((nc, nh, 1, LANES), F32)],
        compiler_params=pltpu.CompilerParams(
            dimension_semantics=("parallel", "parallel"), vmem_limit_bytes=VMEM_LIMIT),
        name="gdn",
    )(proj3, proj3, proj3, proj3, gexp, bexp, conv_w, conv_w, conv_w, norm_w)


def _fox_kernel(q_ref, k_ref, v_ref, c_ref, ct_ref, o_ref, *, tk):
    tq = q_ref.shape[1]
    nj = q_ref.shape[2] // LANES
    g = pl.program_id(1)
    qi = pl.program_id(2)
    q = q_ref[0]
    cblk = c_ref[0]
    lane = lax.broadcasted_iota(jnp.int32, (tq, LANES), 1)
    qpos = qi * tq + lax.broadcasted_iota(jnp.int32, (tq, tk), 0)
    kofs = lax.broadcasted_iota(jnp.int32, (tq, tk), 1)
    heads = list(range(nj))
    lanes_of = [slice(j * LANES, (j + 1) * LANES) for j in heads]
    ccol = [jnp.sum(jnp.where(lane == 8 + g * nj + j, cblk, 0.0), axis=-1, keepdims=True) for j in heads]
    qh = [q[:, hs] * jnp.asarray(FX_DH ** -0.5, BF16) for hs in lanes_of]

    def body(kv, carry):
        k0 = pl.multiple_of(kv * tk, tk)
        causal = qpos >= k0 + kofs
        kblk = k_ref[0, pl.ds(k0, tk), :]
        vblk = v_ref[0, pl.ds(k0, tk), :]
        sc = [_dot_nt(qh[j], kblk[:, lanes_of[j]]) for j in heads]
        crow = [ct_ref[0, pl.ds(g * nj + j, 1), pl.ds(k0, tk)] for j in heads]
        sc = [jnp.where(causal, sc[j] + ccol[j] - crow[j], -1e30) for j in heads]
        m_new = [jnp.maximum(carry[j][0], jnp.max(sc[j], axis=-1, keepdims=True)) for j in heads]
        a = [jnp.exp(carry[j][0] - m_new[j]) for j in heads]
        p = [jnp.exp(sc[j] - m_new[j]) for j in heads]
        l = [a[j] * carry[j][1] + jnp.sum(p[j], axis=-1, keepdims=True) for j in heads]
        acc = [a[j] * carry[j][2] + _dot(p[j].astype(BF16), vblk[:, lanes_of[j]]) for j in heads]
        return tuple((m_new[j], l[j], acc[j]) for j in heads)

    init = tuple((jnp.full((tq, 1), -1e30, F32), jnp.zeros((tq, 1), F32), jnp.zeros((tq, LANES), F32))
                 for _ in heads)
    nkv = (qi * tq + tq - 1) // tk + 1
    res = lax.fori_loop(0, nkv, body, init)
    o_ref[0] = jnp.concatenate([acc / l for _, l, acc in res], axis=1)


def _fox(pf3, c, ct):
    b, s, _ = pf3.shape
    wd = 4 * LANES
    tq = min(128, s)
    tk = min(256, s)
    ngrp = FX_HEADS * LANES // wd
    return pl.pallas_call(
        functools.partial(_fox_kernel, tk=tk),
        grid=(b, ngrp, s // tq),
        in_specs=[pl.BlockSpec((1, tq, wd), lambda i, h, t: (i, t, h)),
                  pl.BlockSpec((1, s, wd), lambda i, h, t: (i, 0, ngrp + h)),
                  pl.BlockSpec((1, s, wd), lambda i, h, t: (i, 0, 2 * ngrp + h)),
                  pl.BlockSpec((1, tq, LANES), lambda i, h, t: (i, t, 0)),
                  pl.BlockSpec((1, 8, s), lambda i, h, t: (i, 0, 0))],
        out_specs=pl.BlockSpec((1, tq, wd), lambda i, h, t: (i, t, h)),
        out_shape=jax.ShapeDtypeStruct((b, s, ngrp * wd), F32),
        compiler_params=pltpu.CompilerParams(
            dimension_semantics=("parallel", "parallel", "parallel"), vmem_limit_bytes=VMEM_LIMIT),
        name="fox",
    )(pf3, pf3, pf3, c, ct)


def _mix_kernel(oa_ref, ob_ref, ga_ref, gb_ref, x_ref, wa_ref, wb_ref, wo_ref, g1_ref, b1_ref, o_ref, *, alpha):
    ya = _dot(oa_ref[...].astype(BF16), wa_ref[...])
    yb = _dot(ob_ref[...].astype(BF16), wb_ref[...])
    mix = jax.nn.sigmoid(ga_ref[...]) * ya + jax.nn.sigmoid(gb_ref[...]) * yb
    z = alpha * x_ref[...] + _dot(mix.astype(BF16), wo_ref[...])
    o_ref[...] = _layernorm(z, g1_ref[...], b1_ref[...])


def _mix(oa, ob, proj, x2, wa, wb, wo, g1, b1, alpha):
    t, d = x2.shape
    tm = min(512, t)
    w, w2 = oa.shape[1], ob.shape[1]
    full = lambda r, c: pl.BlockSpec((r, c), lambda i: (0, 0))
    return pl.pallas_call(
        functools.partial(_mix_kernel, alpha=alpha),
        grid=(t // tm,),
        in_specs=[pl.BlockSpec((tm, w), lambda i: (i, 0)),
                  pl.BlockSpec((tm, w2), lambda i: (i, 0)),
                  pl.BlockSpec((tm, d), lambda i: (i, 0)),
                  pl.BlockSpec((tm, d), lambda i: (i, 1)),
                  pl.BlockSpec((tm, d), lambda i: (i, 0)),
                  full(w, d), full(w2, d), full(d, d), full(1, d), full(1, d)],
        out_specs=pl.BlockSpec((tm, d), lambda i: (i, 0)),
        out_shape=jax.ShapeDtypeStruct((t, d), F32),
        compiler_params=pltpu.CompilerParams(
            dimension_semantics=("parallel",), vmem_limit_bytes=VMEM_LIMIT),
        name="mix",
    )(oa, ob, proj, proj, x2, wa, wb, wo, g1, b1)


def _peer_kernel(h_cur_ref, h_nxt_ref, wq_ref, keys_ref, flat_ref, cmask_ref, uv_hbm, g2_ref, b2_ref, o_ref,
                 q_scr, s_scr, i_scr, best_scr, eh_scr, eall_scr, idv_scr, ids_smem, gate_scr,
                 buf_a, buf_b, sem, idsem, y_scr, *, tt, alpha):
    s = pl.program_id(0)
    tg, d = h_cur_ref.shape
    nk, kk = PEER_NKEYS, PEER_TOPK
    hk = PEER_HEADS * kk
    nlh = tg // LANES
    nsub = tg // (2 * tt)
    assert nsub == PEER_HEADS * nlh
    nslab, sub = uv_hbm.shape[1], uv_hbm.shape[2]
    half = sub // 2
    bufs = (buf_a, buf_b)
    nsem = sem.shape[1]
    last = pl.num_programs(0) - 1
    rslot = s % 3
    pslot = (s + 2) % 3
    eslot = (s + 1) % 3

    def issue(idslot, row0, slot):
        for t in range(tt):
            for k in range(hk):
                e = ids_smem[idslot, row0 + t, k]
                pltpu.make_async_copy(uv_hbm.at[e], bufs[slot].at[:, pl.ds((t * hk + k) * sub, sub), :],
                                      sem.at[slot, k % nsem]).start(priority=k % 2)

    def wait(slot):
        share = bufs[slot].at[:, pl.ds(0, tt * hk * sub // nsem), :]
        for i in range(nsem):
            pltpu.make_async_copy(share, share, sem.at[slot, i]).wait()

    q = _dot(h_nxt_ref[...].astype(BF16), wq_ref[...])
    for j in range(2 * PEER_HEADS):
        for lh in range(nlh):
            q_scr[j, lh] = q[lh * LANES:(lh + 1) * LANES, j * LANES:(j + 1) * LANES].astype(BF16)
    iota_k = lax.broadcasted_iota(jnp.int32, (nk, LANES), 0)
    flat = flat_ref[...]
    cmask = cmask_ref[...]
    neg = jnp.float32(-jnp.inf)

    def route_piece(piece):
        hd = piece // nlh
        lh = piece % nlh
        for p in range(2):
            vals = _dot_nt(keys_ref[hd * 2 + p], q_scr[hd * 2 + p, lh])
            for r in range(kk):
                m = jnp.max(vals, axis=0, keepdims=True)
                am = jnp.min(jnp.where(vals == m, iota_k, nk), axis=0, keepdims=True)
                s_scr[p * kk + r:p * kk + r + 1, :] = m
                i_scr[p * kk + r:p * kk + r + 1, :] = am
                vals = jnp.where(iota_k == am, neg, vals)
        s1 = s_scr[kk:kk + 8, :]
        i1 = i_scr[kk:kk + 8, :]
        cand = [s_scr[0:1, :] + s_scr[kk:2 * kk, :]]
        cidx = [i_scr[0:1, :] * nk + i_scr[kk:2 * kk, :]]
        for a in range(1, 8):
            cand.append(s_scr[a:a + 1, :] + s1)
            cidx.append(i_scr[a:a + 1, :] * nk + i1)
        cand.append(s_scr[8:kk, :] + s_scr[kk:kk + 1, :])
        cidx.append(i_scr[8:kk, :] * nk + i_scr[kk:kk + 1, :])
        vals = jnp.concatenate(cand, axis=0) + cmask
        cidx = jnp.concatenate(cidx, axis=0)
        for r in range(kk):
            m = jnp.max(vals, axis=0, keepdims=True)
            am = jnp.min(jnp.where(vals == m, flat, 2 * kk * kk), axis=0, keepdims=True)
            sel = flat == am
            best_scr[r:r + 1, :] = m
            eh_scr[r:r + 1, :] = jnp.max(jnp.where(sel, cidx, -1), axis=0, keepdims=True)
            vals = jnp.where(sel, neg, vals)
        bs = best_scr[...]
        ex = jnp.exp(bs - bs[0:1, :])
        r0 = pl.multiple_of(hd * kk, kk)
        gate_scr[rslot, lh, pl.ds(r0, kk), :] = ex / jnp.sum(ex, axis=0, keepdims=True)
        eall_scr[lh, pl.ds(r0, kk), :] = eh_scr[...]

    lane = lax.broadcasted_iota(jnp.int32, (hk, LANES), 1)

    def compute(row0, slot):
        buf = bufs[slot]
        gt = gate_scr[eslot, row0 // LANES]
        lane0 = row0 % LANES
        for t in range(tt):
            hrow = h_cur_ref[pl.ds(row0 + t, 1), :]

            def rows(c, j):
                return buf[c, pl.ds(t * hk * sub + j, hk, stride=sub), :]

            part = None
            for j in range(half):
                for c in range(nslab):
                    seg = j * nslab + c
                    term = rows(c, j) * hrow[:, seg * LANES:(seg + 1) * LANES]
                    part = term if part is None else part + term
            pre = jnp.sum(part, axis=-1, keepdims=True)
            gate = jnp.sum(jnp.where(lane == lane0 + t, gt, 0.0), axis=-1, keepdims=True)
            act = 0.5 * pre * (1.0 + lax.erf(pre * (2.0 ** -0.5))) * gate
            yrow = slot * tt + t
            for j in range(half):
                for c in range(nslab):
                    seg = j * nslab + c
                    y_scr[yrow:yrow + 1, seg * LANES:(seg + 1) * LANES] = jnp.sum(
                        act * rows(c, half + j), axis=0, keepdims=True)

    def substep(j, carry):
        row0 = pl.multiple_of(j * 2 * tt, 2 * tt)
        issue(eslot, row0 + tt, 1)
        route_piece(j)
        wait(0)
        compute(row0, 0)
        wait(1)
        wrap = j == nsub - 1
        issue(jnp.where(wrap, pslot, eslot), jnp.where(wrap, 0, row0 + 2 * tt), 0)
        compute(row0 + tt, 1)
        z = alpha * h_cur_ref[pl.ds(row0, 2 * tt), :] + y_scr[...]
        o_ref[pl.ds(row0, 2 * tt), :] = _layernorm(z, g2_ref[...], b2_ref[...])
        return carry

    @pl.when(s < 2)
    def _():
        o_ref[...] = jnp.zeros_like(o_ref)

        def piece(p, carry):
            route_piece(p)
            return carry
        lax.fori_loop(0, nsub, piece, 0)

    @pl.when(s == 1)
    def _():
        issue(pslot, 0, 0)

    @pl.when(s >= 2)
    def _():
        lax.fori_loop(0, nsub, substep, 0)

    @pl.when(s == last)
    def _():
        wait(0)

    for lh in range(nlh):
        idv_scr[lh * LANES:(lh + 1) * LANES, :] = eall_scr[lh].T
    publish = pltpu.make_async_copy(idv_scr, ids_smem.at[rslot], idsem)
    publish.start()
    publish.wait()


def _peer(h1, wq, keys, uv_tab, g2, b2, alpha):
    t, d = h1.shape
    tg = 256
    tt = 8
    ngrp = t // tg
    nq = wq.shape[1]
    kk = PEER_TOPK
    hk = PEER_HEADS * kk
    nlh = tg // LANES
    nslab, sub = uv_tab.shape[1], uv_tab.shape[2]
    pairs = [(0, bb) for bb in range(kk)]
    for a in range(1, 8):
        pairs += [(a, bb) for bb in range(8)]
    pairs += [(a, 0) for a in range(8, kk)]
    real = [(a + 1) * (bb + 1) <= kk for a, bb in pairs]
    flat = [a * kk + bb if ok else kk * kk + r for r, ((a, bb), ok) in enumerate(zip(pairs, real))]
    flat = jnp.broadcast_to(jnp.asarray(flat, jnp.int32)[:, None], (len(pairs), LANES))
    cmask = jnp.broadcast_to(jnp.asarray([0.0 if ok else -jnp.inf for ok in real], F32)[:, None],
                             (len(pairs), LANES))
    const = lambda shape: pl.BlockSpec(shape, lambda i: (0,) * len(shape))
    return pl.pallas_call(
        functools.partial(_peer_kernel, tt=tt, alpha=alpha),
        grid=(ngrp + 2,),
        in_specs=[pl.BlockSpec((tg, d), lambda i: (jnp.maximum(i - 2, 0), 0)),
                  pl.BlockSpec((tg, d), lambda i: (jnp.minimum(i, ngrp - 1), 0)),
                  const((d, nq)), const(keys.shape), const(flat.shape), const(cmask.shape),
                  pl.BlockSpec(memory_space=pl.ANY),
                  const((1, d)), const((1, d))],
        out_specs=pl.BlockSpec((tg, d), lambda i: (jnp.maximum(i - 2, 0), 0)),
        out_shape=jax.ShapeDtypeStruct((t, d), F32),
        scratch_shapes=[pltpu.VMEM((2 * PEER_HEADS, nlh, LANES, LANES), BF16),
                        pltpu.VMEM((2 * kk, LANES), F32), pltpu.VMEM((2 * kk, LANES), jnp.int32),
                        pltpu.VMEM((kk, LANES), F32), pltpu.VMEM((kk, LANES), jnp.int32),
                        pltpu.VMEM((nlh, hk, LANES), jnp.int32),
                        pltpu.VMEM((tg, hk), jnp.int32),
                        pltpu.SMEM((3, tg, hk), jnp.int32),
                        pltpu.VMEM((3, nlh, hk, LANES), F32),
                        pltpu.VMEM((nslab, tt * hk * sub, LANES), F32),
                        pltpu.VMEM((nslab, tt * hk * sub, LANES), F32),
                        pltpu.SemaphoreType.DMA((2, 32)), pltpu.SemaphoreType.DMA,
                        pltpu.VMEM((2 * tt, d), F32)],
        compiler_params=pltpu.CompilerParams(
            dimension_semantics=("arbitrary",), vmem_limit_bytes=VMEM_LIMIT),
        name="peer",
    )(h1, h1, wq, keys, flat, cmask, uv_tab, g2, b2)


def _layer(h, w_in, conv_w, a_log, dt_bias, norm_w, f_bias, w_out_gdn, w_out_fox, w_o, ln1_g, ln1_b,
           peer_wq, peer_keys, peer_u, peer_v, ln2_g, ln2_b, alpha):
    b, s, d = h.shape
    t = b * s
    qk = GDN_HEADS * GDN_DK
    fw = FX_HEADS * FX_DH
    o_gz = 4 * qk
    o_ga = o_gz
    o_fq = o_ga + 2 * GDN_HEADS
    o_ff = o_fq + 3 * fw
    o_gate = o_ff + FX_HEADS
    w_big = jnp.concatenate([w_in[:, o_gate:], w_in[:, :o_gz]], axis=1).astype(BF16)
    pad = LANES - FX_DH
    w_fox = jnp.pad(w_in[:, o_fq:o_ff].reshape(d, 3 * FX_HEADS, FX_DH), ((0, 0), (0, 0), (0, pad)))
    w_fox = w_fox.reshape(d, 3 * FX_HEADS * LANES).astype(BF16)
    w_out_fox_p = jnp.pad(w_out_fox.reshape(FX_HEADS, FX_DH, d), ((0, 0), (0, pad), (0, 0)))
    w_out_fox_p = w_out_fox_p.reshape(FX_HEADS * LANES, d).astype(BF16)
    n_small = 2 * GDN_HEADS + FX_HEADS
    w_small = jnp.concatenate([w_in[:, o_ga:o_fq], w_in[:, o_ff:o_gate],
                               jnp.zeros((d, LANES - n_small), F32)], axis=1)
    params = jnp.zeros((8, LANES), F32)
    params = params.at[0, :GDN_HEADS].set(a_log).at[1, :GDN_HEADS].set(dt_bias)
    params = params.at[2, 2 * GDN_HEADS:n_small].set(f_bias)

    x2 = h.reshape(t, d)
    proj = _in_proj(x2, w_big, F32, "in_proj")
    pf = _in_proj(x2, w_fox, BF16, "in_proj_fox")
    gexp, bexp, c, ct = _prep(h, w_small, params)
    proj3 = proj.reshape(b, s, proj.shape[1])
    gdn0 = 2 * d // LANES
    oa = _gdn(proj3, gexp, bexp, conv_w, norm_w.reshape(1, LANES),
              (gdn0, gdn0 + GDN_HEADS, gdn0 + 2 * GDN_HEADS, gdn0 + 3 * GDN_HEADS))
    ob = _fox(pf.reshape(b, s, pf.shape[1]), c, ct)
    h1 = _mix(oa.reshape(t, qk), ob.reshape(t, FX_HEADS * LANES), proj, x2,
              w_out_gdn.astype(BF16), w_out_fox_p, w_o.astype(BF16),
              ln1_g.reshape(1, d), ln1_b.reshape(1, d), alpha)
    keys = peer_keys.reshape(2 * PEER_HEADS, PEER_NKEYS, peer_keys.shape[-1]).astype(BF16)
    ne = peer_u.shape[0]
    half = PEER_SUB // 2
    nslab = d // (half * LANES)
    uv_tab = jnp.concatenate([peer_u.reshape(ne, half, nslab, LANES),
                              peer_v.reshape(ne, half, nslab, LANES)], axis=1).transpose(0, 2, 1, 3)
    out = _peer(h1, peer_wq.astype(BF16), keys, uv_tab, ln2_g.reshape(1, d), ln2_b.reshape(1, d), alpha)
    return out.reshape(b, s, d)


def kernel(x, w_in, gdn_conv_w, gdn_a_log, gdn_dt_bias, gdn_norm_w, fox_f_bias, w_out_gdn, w_out_fox, w_o,
           ln1_g, ln1_b, peer_wq, peer_keys, peer_u, peer_v, ln2_g, ln2_b):
    depth = w_in.shape[0]
    alpha = (2.0 * depth) ** 0.25
    h = x
    for l in range(depth):
        h = _layer(h, w_in[l], gdn_conv_w[l], gdn_a_log[l], gdn_dt_bias[l], gdn_norm_w[l], fox_f_bias[l],
                   w_out_gdn[l], w_out_fox[l], w_o[l], ln1_g[l], ln1_b[l], peer_wq[l], peer_keys[l],
                   peer_u[l], peer_v[l], ln2_g[l], ln2_b[l], alpha)
    return h
```

```python
import functools

import jax
import jax.numpy as jnp
from jax import lax
from jax.experimental import pallas as pl
from jax.experimental.pallas import tpu as pltpu
from jax.experimental.pallas import tpu_sc as plsc

F32 = jnp.float32
BF16 = jnp.bfloat16
HI = lax.Precision.HIGHEST

LANES = 128
CHUNK = 64
GDN_HEADS = 4
GDN_DK = 128
FX_HEADS = 8
FX_DH = 64
PEER_HEADS = 8
PEER_NKEYS = 128
PEER_TOPK = 16
PEER_SUB = 4
LN_EPS = 1e-5
VMEM_LIMIT = 48 * 1024 * 1024


def _dot(a, b, prec=None):
    return jnp.dot(a, b, preferred_element_type=F32, precision=prec)


def _dot_nt(a, b, prec=None):
    return lax.dot_general(a, b, (((1,), (1,)), ((), ())), preferred_element_type=F32, precision=prec)


def _dot_tn(a, b, prec=None):
    return lax.dot_general(a, b, (((0,), (0,)), ((), ())), preferred_element_type=F32, precision=prec)


def _softplus(x):
    return jnp.maximum(x, 0.0) + jnp.log1p(jnp.exp(-jnp.abs(x)))


def _layernorm(z, g, b):
    mu = jnp.mean(z, axis=-1, keepdims=True)
    zc = z - mu
    var = jnp.mean(zc * zc, axis=-1, keepdims=True)
    return zc * lax.rsqrt(var + LN_EPS) * g + b


def _mm_kernel(x_ref, w_ref, o_ref):
    o_ref[...] = _dot(x_ref[...].astype(BF16), w_ref[...]).astype(o_ref.dtype)


def _in_proj(x2, w_big, out_dtype, name):
    t, d = x2.shape
    n = w_big.shape[1]
    tm = min(1024, t)
    tn = 512
    return pl.pallas_call(
        _mm_kernel,
        grid=(t // tm, n // tn),
        in_specs=[pl.BlockSpec((tm, d), lambda i, j: (i, 0)),
                  pl.BlockSpec((d, tn), lambda i, j: (0, j))],
        out_specs=pl.BlockSpec((tm, tn), lambda i, j: (i, j)),
        out_shape=jax.ShapeDtypeStruct((t, n), out_dtype),
        compiler_params=pltpu.CompilerParams(
            dimension_semantics=("parallel", "parallel"), vmem_limit_bytes=VMEM_LIMIT),
        name=name,
    )(x2, w_big)


def _prep_kernel(x_ref, w_ref, par_ref, gexp_ref, bexp_ref, c_ref, ct_ref, carry_ref):
    ts = x_ref.shape[1]

    @pl.when(pl.program_id(1) == 0)
    def _():
        carry_ref[...] = jnp.zeros_like(carry_ref)

    small = _dot(x_ref[0], w_ref[...], HI)
    a_log = par_ref[0:1, :]
    dt_bias = par_ref[1:2, :]
    f_bias = par_ref[2:3, :]
    g = -jnp.exp(a_log) * _softplus(small + dt_bias)
    beta = jax.nn.sigmoid(small)
    lane = lax.broadcasted_iota(jnp.int32, (ts, LANES), 1)
    log_f = jnp.where((lane >= 8) & (lane < 16), -_softplus(-(small + f_bias)), 0.0)
    row = lax.broadcasted_iota(jnp.int32, (ts, ts), 0)
    col = lax.broadcasted_iota(jnp.int32, (ts, ts), 1)
    tril = (row >= col).astype(F32)
    c = _dot(tril, log_f, HI) + carry_ref[...]
    carry_ref[...] = c[ts - 1:ts, :]
    c_ref[0] = c
    ct_ref[0] = c.T[8:16, :]
    gexp_ref[0] = jnp.concatenate(
        [jnp.broadcast_to(g[:, h:h + 1], (ts, LANES)) for h in range(GDN_HEADS)], axis=1)
    bexp_ref[0] = jnp.concatenate(
        [jnp.broadcast_to(beta[:, GDN_HEADS + h:GDN_HEADS + h + 1], (ts, LANES)) for h in range(GDN_HEADS)], axis=1)


def _prep(x, w_small, params):
    b, s, d = x.shape
    ts = min(512, s)
    hw = GDN_HEADS * LANES
    return pl.pallas_call(
        _prep_kernel,
        grid=(b, s // ts),
        in_specs=[pl.BlockSpec((1, ts, d), lambda i, j: (i, j, 0)),
                  pl.BlockSpec((d, LANES), lambda i, j: (0, 0)),
                  pl.BlockSpec((8, LANES), lambda i, j: (0, 0))],
        out_specs=[pl.BlockSpec((1, ts, hw), lambda i, j: (i, j, 0)),
                   pl.BlockSpec((1, ts, hw), lambda i, j: (i, j, 0)),
                   pl.BlockSpec((1, ts, LANES), lambda i, j: (i, j, 0)),
                   pl.BlockSpec((1, 8, ts), lambda i, j: (i, 0, j))],
        out_shape=[jax.ShapeDtypeStruct((b, s, hw), F32),
                   jax.ShapeDtypeStruct((b, s, hw), F32),
                   jax.ShapeDtypeStruct((b, s, LANES), F32),
                   jax.ShapeDtypeStruct((b, 8, s), F32)],
        scratch_shapes=[pltpu.VMEM((1, LANES), F32)],
        compiler_params=pltpu.CompilerParams(
            dimension_semantics=("parallel", "arbitrary"), vmem_limit_bytes=VMEM_LIMIT),
        name="prep",
    )(x, w_small, params)


def _each(fn, *lists):
    return [fn(*args) for args in zip(*lists)]


def _unit_lower_inverse(ms, masks):
    eye, blk16, blk32 = masks
    hi = lambda a, b: _dot(a, b, HI)
    n1 = _each(lambda m: -jnp.where(blk16, m, 0.0), ms)
    l1 = _each(lambda m: jnp.where(blk32 & jnp.logical_not(blk16), m, 0.0), ms)
    l2 = _each(lambda m: jnp.where(blk32, 0.0, m), ms)
    n2 = _each(hi, n1, n1)
    p = _each(lambda a, b: hi(eye + a, eye + b), n1, n2)
    n4 = _each(hi, n2, n2)
    p = _each(lambda a, b: hi(a, eye + b), p, n4)
    n8 = _each(hi, n4, n4)
    d_inv = _each(lambda a, b: hi(a, eye + b), p, n8)
    dl = _each(hi, d_inv, l1)
    a32 = _each(lambda a, b: a - hi(b, a), d_inv, dl)
    al = _each(hi, a32, l2)
    return _each(lambda a, b: a - hi(b, a), a32, al)


def _gdn_kernel(q_ref, k_ref, v_ref, z_ref, g_ref, b_ref, cwq_ref, cwk_ref, cwv_ref, nw_ref,
                o_ref, qn, kn, vn, st, sol_s, qk_s, qg_s, kd_s, gl_s):
    s = q_ref.shape[1]
    c = CHUNK
    nh = q_ref.shape[2] // LANES
    row = lax.broadcasted_iota(jnp.int32, (s, LANES), 0)

    def conv_silu(x, w):
        y = x * w[3:4, :]
        for sh in (1, 2, 3):
            xs = jnp.where(row >= sh, pltpu.roll(x, sh, axis=0), 0.0)
            y = y + xs * w[3 - sh:4 - sh, :]
        return y * jax.nn.sigmoid(y)

    def l2norm(x):
        return x * lax.rsqrt(jnp.sum(x * x, axis=-1, keepdims=True) + 1e-6)

    for j in range(nh):
        hs = slice(j * LANES, (j + 1) * LANES)
        qn[:, hs] = l2norm(conv_silu(q_ref[0, :, hs], cwq_ref[:, hs])) * (GDN_DK ** -0.5)
        kn[:, hs] = l2norm(conv_silu(k_ref[0, :, hs], cwk_ref[:, hs]))
        vn[:, hs] = conv_silu(v_ref[0, :, hs], cwv_ref[:, hs])
    st[...] = jnp.zeros_like(st)

    ri = lax.broadcasted_iota(jnp.int32, (c, c), 0)
    ci = lax.broadcasted_iota(jnp.int32, (c, c), 1)
    tril = ri >= ci
    strict = ri > ci
    t_inc = tril.astype(F32)
    eye = (ri == ci).astype(F32)
    blk16 = (ri >> 4) == (ci >> 4)
    blk32 = (ri >> 5) == (ci >> 5)
    l2 = lax.broadcasted_iota(jnp.int32, (c, 2 * LANES), 0)
    j2 = lax.broadcasted_iota(jnp.int32, (c, 2 * LANES), 1)
    ux = jnp.where((j2 >= c) | (l2 > j2), 1.0, 0.0).astype(F32)
    nw = nw_ref[...]

    heads = list(range(nh))
    lanes_of = [slice(j * LANES, (j + 1) * LANES) for j in heads]

    def local_stage(ns):
        pairs = [(i, j) for i in range(len(ns)) for j in heads]
        rows = [pl.ds(pl.multiple_of(n * c, c), c) for n in ns]
        q = [qn[rows[i], lanes_of[j]] for i, j in pairs]
        k = [kn[rows[i], lanes_of[j]] for i, j in pairs]
        v = [vn[rows[i], lanes_of[j]] for i, j in pairs]
        gb = [g_ref[0, rows[i], lanes_of[j]] for i, j in pairs]
        bb = [b_ref[0, rows[i], lanes_of[j]] for i, j in pairs]
        d = _each(lambda g: _dot(t_inc, jnp.concatenate([g, g], axis=1) * ux, HI), gb)
        kb = _each(lambda a, b: a * b, k, bb)
        kk = _each(lambda a, b: _dot_nt(a, b, HI), kb, k)
        qk = _each(lambda a, b: _dot_nt(a.astype(BF16), b.astype(BF16)), q, k)
        gc = [x[:, LANES:] for x in d]
        decay = [jnp.where(tril, jnp.exp(x[:, :c]), 0.0) for x in d]
        m = _each(lambda a, b: jnp.where(strict, a * b, 0.0), kk, decay)
        a_inv = _unit_lower_inverse(m, (eye, blk16, blk32))
        egc = _each(jnp.exp, gc)
        rhs = _each(lambda vv, b, kbb, e: jnp.concatenate([vv * b, kbb * e], axis=1), v, bb, kb, egc)
        sol = _each(lambda a, r: _dot(a, r, HI), a_inv, rhs)
        gl = [x[c - 1:c, :] for x in gc]
        for p, (i, j) in enumerate(pairs):
            n = ns[i]
            qk_s[n, j] = (qk[p] * decay[p]).astype(BF16)
            qg_s[n, j] = (q[p] * egc[p]).astype(BF16)
            kd_s[n, j] = (k[p] * jnp.exp(gl[p] - gc[p])).astype(BF16)
            gl_s[n, j] = gl[p]
            sol_s[n, j] = sol[p]

    def state_stage(n):
        rows = pl.ds(pl.multiple_of(n * c, c), c)
        state = [st[j] for j in heads]
        state_b = [x.astype(BF16) for x in state]
        v_new = [sol_s[n, j, :, :LANES] - _dot(sol_s[n, j, :, LANES:].astype(BF16), state_b[j])
                 for j in heads]
        v_new_b = [x.astype(BF16) for x in v_new]
        o = [_dot(qg_s[n, j], state_b[j]) + _dot(qk_s[n, j], v_new_b[j]) for j in heads]
        new_state = [state[j] * jnp.exp(gl_s[n, j]) + _dot_tn(kd_s[n, j], v_new_b[j]) for j in heads]
        for j in heads:
            st[j] = new_state[j]
            on = o[j] * lax.rsqrt(jnp.mean(o[j] * o[j], axis=-1, keepdims=True) + 1e-6) * nw
            z = z_ref[0, rows, lanes_of[j]]
            o_ref[0, rows, lanes_of[j]] = on * (z * jax.nn.sigmoid(z))

    nchunks = s // c
    group = 4

    def local_group(gidx, carry):
        local_stage([gidx * group + i for i in range(group)])
        return carry

    def state_chunk(n, carry):
        state_stage(n)
        return carry

    lax.fori_loop(0, nchunks // group, local_group, 0)
    lax.fori_loop(0, nchunks, state_chunk, 0)


def _gdn(proj3, gexp, bexp, conv_w, norm_w, cols):
    b, s, _ = proj3.shape
    nh = 2
    nc = s // CHUNK
    wd = nh * LANES
    cq, ck, cv, cz = (c0 // nh for c0 in cols)

    def blk(c0):
        return pl.BlockSpec((1, s, wd), lambda i, h: (i, 0, c0 + h))

    def cw(c0):
        return pl.BlockSpec((conv_w.shape[0], wd), lambda i, h: (0, c0 + h))

    head = pl.BlockSpec((1, s, wd), lambda i, h: (i, 0, h))
    return pl.pallas_call(
        _gdn_kernel,
        grid=(b, GDN_HEADS // nh),
        in_specs=[blk(cq), blk(ck), blk(cv), blk(cz), head, head,
                  cw(0), cw(GDN_HEADS // nh), cw(2 * GDN_HEADS // nh),
                  pl.BlockSpec((1, LANES), lambda i, h: (0, 0))],
        out_specs=head,
        out_shape=jax.ShapeDtypeStruct((b, s, GDN_HEADS * LANES), F32),
        scratch_shapes=[pltpu.VMEM((s, wd), F32), pltpu.VMEM((s, wd), F32),
                        pltpu.VMEM((s, wd), F32), pltpu.VMEM((nh, GDN_DK, LANES), F32),
                        pltpu.VMEM((nc, nh, CHUNK, 2 * LANES), F32), pltpu.VMEM((nc, nh, CHUNK, CHUNK), BF16),
                        pltpu.VMEM((nc, nh, CHUNK, LANES), BF16), pltpu.VMEM((nc, nh, CHUNK, LANES), BF16),
                        pltpu.VMEM((nc, nh, 1, LANES), F32)],
        compiler_params=pltpu.CompilerParams(
            dimension_semantics=("parallel", "parallel"), vmem_limit_bytes=VMEM_LIMIT),
        name="gdn",
    )(proj3, proj3, proj3, proj3, gexp, bexp, conv_w, conv_w, conv_w, norm_w)


def _fox_kernel(q_ref, k_ref, v_ref, c_ref, ct_ref, o_ref, *, tk):
    tq = q_ref.shape[1]
    nj = q_ref.shape[2] // LANES
    g = pl.program_id(1)
    qi = pl.program_id(2)
    q = q_ref[0]
    cblk = c_ref[0]
    lane = lax.broadcasted_iota(jnp.int32, (tq, LANES), 1)
    qpos = qi * tq + lax.broadcasted_iota(jnp.int32, (tq, tk), 0)
    kofs = lax.broadcasted_iota(jnp.int32, (tq, tk), 1)
    heads = list(range(nj))
    lanes_of = [slice(j * LANES, (j + 1) * LANES) for j in heads]
    ccol = [jnp.sum(jnp.where(lane == 8 + g * nj + j, cblk, 0.0), axis=-1, keepdims=True) for j in heads]
    qh = [q[:, hs] * jnp.asarray(FX_DH ** -0.5, BF16) for hs in lanes_of]

    def body(kv, carry):
        k0 = pl.multiple_of(kv * tk, tk)
        causal = qpos >= k0 + kofs
        kblk = k_ref[0, pl.ds(k0, tk), :]
        vblk = v_ref[0, pl.ds(k0, tk), :]
        sc = [_dot_nt(qh[j], kblk[:, lanes_of[j]]) for j in heads]
        crow = [ct_ref[0, pl.ds(g * nj + j, 1), pl.ds(k0, tk)] for j in heads]
        sc = [jnp.where(causal, sc[j] + ccol[j] - crow[j], -1e30) for j in heads]
        m_new = [jnp.maximum(carry[j][0], jnp.max(sc[j], axis=-1, keepdims=True)) for j in heads]
        a = [jnp.exp(carry[j][0] - m_new[j]) for j in heads]
        p = [jnp.exp(sc[j] - m_new[j]) for j in heads]
        l = [a[j] * carry[j][1] + jnp.sum(p[j], axis=-1, keepdims=True) for j in heads]
        acc = [a[j] * carry[j][2] + _dot(p[j].astype(BF16), vblk[:, lanes_of[j]]) for j in heads]
        return tuple((m_new[j], l[j], acc[j]) for j in heads)

    init = tuple((jnp.full((tq, 1), -1e30, F32), jnp.zeros((tq, 1), F32), jnp.zeros((tq, LANES), F32))
                 for _ in heads)
    nkv = (qi * tq + tq - 1) // tk + 1
    res = lax.fori_loop(0, nkv, body, init)
    o_ref[0] = jnp.concatenate([acc / l for _, l, acc in res], axis=1)


def _fox(pf3, c, ct):
    b, s, _ = pf3.shape
    wd = 4 * LANES
    tq = min(128, s)
    tk = min(256, s)
    ngrp = FX_HEADS * LANES // wd
    return pl.pallas_call(
        functools.partial(_fox_kernel, tk=tk),
        grid=(b, ngrp, s // tq),
        in_specs=[pl.BlockSpec((1, tq, wd), lambda i, h, t: (i, t, h)),
                  pl.BlockSpec((1, s, wd), lambda i, h, t: (i, 0, ngrp + h)),
                  pl.BlockSpec((1, s, wd), lambda i, h, t: (i, 0, 2 * ngrp + h)),
                  pl.BlockSpec((1, tq, LANES), lambda i, h, t: (i, t, 0)),
                  pl.BlockSpec((1, 8, s), lambda i, h, t: (i, 0, 0))],
        out_specs=pl.BlockSpec((1, tq, wd), lambda i, h, t: (i, t, h)),
        out_shape=jax.ShapeDtypeStruct((b, s, ngrp * wd), F32),
        compiler_params=pltpu.CompilerParams(
            dimension_semantics=("parallel", "parallel", "parallel"), vmem_limit_bytes=VMEM_LIMIT),
        name="fox",
    )(pf3, pf3, pf3, c, ct)


def _mix_kernel(oa_ref, ob_ref, ga_ref, gb_ref, x_ref, wa_ref, wb_ref, wo_ref, g1_ref, b1_ref, o_ref, *, alpha):
    ya = _dot(oa_ref[...].astype(BF16), wa_ref[...])
    yb = _dot(ob_ref[...].astype(BF16), wb_ref[...])
    mix = jax.nn.sigmoid(ga_ref[...]) * ya + jax.nn.sigmoid(gb_ref[...]) * yb
    z = alpha * x_ref[...] + _dot(mix.astype(BF16), wo_ref[...])
    o_ref[...] = _layernorm(z, g1_ref[...], b1_ref[...])


def _mix(oa, ob, proj, x2, wa, wb, wo, g1, b1, alpha):
    t, d = x2.shape
    tm = min(512, t)
    w, w2 = oa.shape[1], ob.shape[1]
    full = lambda r, c: pl.BlockSpec((r, c), lambda i: (0, 0))
    return pl.pallas_call(
        functools.partial(_mix_kernel, alpha=alpha),
        grid=(t // tm,),
        in_specs=[pl.BlockSpec((tm, w), lambda i: (i, 0)),
                  pl.BlockSpec((tm, w2), lambda i: (i, 0)),
                  pl.BlockSpec((tm, d), lambda i: (i, 0)),
                  pl.BlockSpec((tm, d), lambda i: (i, 1)),
                  pl.BlockSpec((tm, d), lambda i: (i, 0)),
                  full(w, d), full(w2, d), full(d, d), full(1, d), full(1, d)],
        out_specs=pl.BlockSpec((tm, d), lambda i: (i, 0)),
        out_shape=jax.ShapeDtypeStruct((t, d), F32),
        compiler_params=pltpu.CompilerParams(
            dimension_semantics=("parallel",), vmem_limit_bytes=VMEM_LIMIT),
        name="mix",
    )(oa, ob, proj, proj, x2, wa, wb, wo, g1, b1)


def _route_head(keys_ref, q_scr, hd, lh, scr, flat, cmask):
    s_scr, i_scr, best_scr, eh_scr = scr
    nk, kk = PEER_NKEYS, PEER_TOPK
    iota_k = lax.broadcasted_iota(jnp.int32, (nk, LANES), 0)
    neg = jnp.float32(-jnp.inf)
    for p in range(2):
        vals = _dot_nt(keys_ref[hd * 2 + p], q_scr[hd * 2 + p, lh])
        for r in range(kk):
            m = jnp.max(vals, axis=0, keepdims=True)
            am = jnp.min(jnp.where(vals == m, iota_k, nk), axis=0, keepdims=True)
            s_scr[p * kk + r:p * kk + r + 1, :] = m
            i_scr[p * kk + r:p * kk + r + 1, :] = am
            vals = jnp.where(iota_k == am, neg, vals)
    s1 = s_scr[kk:kk + 8, :]
    i1 = i_scr[kk:kk + 8, :]
    cand = [s_scr[0:1, :] + s_scr[kk:2 * kk, :]]
    cidx = [i_scr[0:1, :] * nk + i_scr[kk:2 * kk, :]]
    for a in range(1, 8):
        cand.append(s_scr[a:a + 1, :] + s1)
        cidx.append(i_scr[a:a + 1, :] * nk + i1)
    cand.append(s_scr[8:kk, :] + s_scr[kk:kk + 1, :])
    cidx.append(i_scr[8:kk, :] * nk + i_scr[kk:kk + 1, :])
    vals = jnp.concatenate(cand, axis=0) + cmask
    cidx = jnp.concatenate(cidx, axis=0)
    for r in range(kk):
        m = jnp.max(vals, axis=0, keepdims=True)
        am = jnp.min(jnp.where(vals == m, flat, 2 * kk * kk), axis=0, keepdims=True)
        sel = flat == am
        best_scr[r:r + 1, :] = m
        eh_scr[r:r + 1, :] = jnp.max(jnp.where(sel, cidx, -1), axis=0, keepdims=True)
        vals = jnp.where(sel, neg, vals)
    bs = best_scr[...]
    ex = jnp.exp(bs - bs[0:1, :])
    return ex / jnp.sum(ex, axis=0, keepdims=True), eh_scr[...]


def _candidate_tables():
    kk = PEER_TOPK
    pairs = [(0, bb) for bb in range(kk)]
    for a in range(1, 8):
        pairs += [(a, bb) for bb in range(8)]
    pairs += [(a, 0) for a in range(8, kk)]
    real = [(a + 1) * (bb + 1) <= kk for a, bb in pairs]
    flat = [a * kk + bb if ok else kk * kk + r for r, ((a, bb), ok) in enumerate(zip(pairs, real))]
    flat = jnp.broadcast_to(jnp.asarray(flat, jnp.int32)[:, None], (len(pairs), LANES))
    cmask = jnp.broadcast_to(jnp.asarray([0.0 if ok else -jnp.inf for ok in real], F32)[:, None],
                             (len(pairs), LANES))
    return flat, cmask


def _route_kernel(h_ref, wq_ref, keys_ref, flat_ref, cmask_ref, e_ref, g_ref,
                  q_scr, s_scr, i_scr, best_scr, eh_scr, eall_scr, gall_scr):
    tg = h_ref.shape[0]
    kk = PEER_TOPK
    nlh = tg // LANES
    q = _dot(h_ref[...].astype(BF16), wq_ref[...])
    for j in range(2 * PEER_HEADS):
        for lh in range(nlh):
            q_scr[j, lh] = q[lh * LANES:(lh + 1) * LANES, j * LANES:(j + 1) * LANES].astype(BF16)
    flat = flat_ref[...]
    cmask = cmask_ref[...]

    def piece(p, carry):
        hd = p // nlh
        lh = p % nlh
        gates, ids = _route_head(keys_ref, q_scr, hd, lh, (s_scr, i_scr, best_scr, eh_scr), flat, cmask)
        r0 = pl.multiple_of(hd * kk, kk)
        gall_scr[lh, pl.ds(r0, kk), :] = gates
        eall_scr[lh, pl.ds(r0, kk), :] = ids
        return carry

    lax.fori_loop(0, PEER_HEADS * nlh, piece, 0)
    for lh in range(nlh):
        e_ref[lh * LANES:(lh + 1) * LANES, :] = eall_scr[lh].T
        g_ref[lh * LANES:(lh + 1) * LANES, :] = gall_scr[lh].T


def _route(h, wq, keys):
    t, d = h.shape
    tg = 256
    nq = wq.shape[1]
    kk = PEER_TOPK
    hk = PEER_HEADS * kk
    nlh = tg // LANES
    flat, cmask = _candidate_tables()
    const = lambda shape: pl.BlockSpec(shape, lambda i: (0,) * len(shape))
    return pl.pallas_call(
        _route_kernel,
        grid=(t // tg,),
        in_specs=[pl.BlockSpec((tg, d), lambda i: (i, 0)),
                  const((d, nq)), const(keys.shape), const(flat.shape), const(cmask.shape)],
        out_specs=[pl.BlockSpec((tg, hk), lambda i: (i, 0)), pl.BlockSpec((tg, hk), lambda i: (i, 0))],
        out_shape=[jax.ShapeDtypeStruct((t, hk), jnp.int32), jax.ShapeDtypeStruct((t, hk), F32)],
        scratch_shapes=[pltpu.VMEM((2 * PEER_HEADS, nlh, LANES, LANES), BF16),
                        pltpu.VMEM((2 * kk, LANES), F32), pltpu.VMEM((2 * kk, LANES), jnp.int32),
                        pltpu.VMEM((kk, LANES), F32), pltpu.VMEM((kk, LANES), jnp.int32),
                        pltpu.VMEM((nlh, hk, LANES), jnp.int32), pltpu.VMEM((nlh, hk, LANES), F32)],
        compiler_params=pltpu.CompilerParams(
            dimension_semantics=("parallel",), vmem_limit_bytes=VMEM_LIMIT),
        name="route",
    )(h, wq, keys, flat, cmask)


PEER_SC_SHARE_PCT = 25
SC_WORKERS = 32
SC_LANES = 16
SC_CHUNK = 32


def _sc_mesh():
    return plsc.VectorSubcoreMesh(core_axis_name="c", subcore_axis_name="s")


def _sc_worker():
    return lax.axis_index("s") * 2 + lax.axis_index("c")


def _sc_pre(ids, h, u_tab):
    ts, hk = ids.shape
    d = h.shape[1]
    per_w = ts // SC_WORKERS
    nch = hk // SC_CHUNK

    def body(ids_hbm, h_hbm, u_hbm, out_hbm, idx_v, h_v, rows_a, rows_b, pre_v, sem_a, sem_b):
        base = _sc_worker() * per_w
        lane = lax.iota(jnp.int32, SC_LANES)
        bufs = (rows_a, rows_b)
        sems = (sem_a, sem_b)

        def gather(c):
            return pltpu.async_copy(u_hbm.at[idx_v.at[pl.ds(c * SC_CHUNK, SC_CHUNK)]], bufs[c % 2], sems[c % 2])

        def token(i, carry):
            t = base + i
            pltpu.sync_copy(ids_hbm.at[t], idx_v)
            pltpu.sync_copy(h_hbm.at[t], h_v)
            cps = {0: gather(0)}
            for c in range(nch):
                if c + 1 < nch:
                    cps[c + 1] = gather(c + 1)
                cps[c].wait()
                rows = bufs[c % 2]
                for g in range(SC_CHUNK // SC_LANES):
                    def expert(r, vec, g=g, rows=rows):
                        acc = jnp.zeros((SC_LANES,), F32)
                        for j in range(d // SC_LANES):
                            acc = acc + rows[g * SC_LANES + r, pl.ds(j * SC_LANES, SC_LANES)] * h_v[pl.ds(j * SC_LANES, SC_LANES)]
                        return jnp.where(lane == r, jnp.sum(acc), vec)
                    vec = lax.fori_loop(0, SC_LANES, expert, jnp.zeros((SC_LANES,), F32))
                    pre_v[pl.ds(c * SC_CHUNK + g * SC_LANES, SC_LANES)] = vec
            pltpu.sync_copy(pre_v, out_hbm.at[t])
            return carry

        lax.fori_loop(0, per_w, token, 0)

    return pl.kernel(
        body, out_type=jax.ShapeDtypeStruct((ts, hk), F32), mesh=_sc_mesh(),
        scratch_types=[pltpu.VMEM((hk,), jnp.int32), pltpu.VMEM((d,), F32),
                       pltpu.VMEM((SC_CHUNK, d), F32), pltpu.VMEM((SC_CHUNK, d), F32),
                       pltpu.VMEM((hk,), F32), pltpu.SemaphoreType.DMA, pltpu.SemaphoreType.DMA],
        compiler_params=pltpu.CompilerParams(needs_layout_passes=False),
        name="sc_pre",
    )(ids, h, u_tab)


def _sc_out(ids, act, v_tab):
    ts, hk = ids.shape
    d = v_tab.shape[1]
    per_w = ts // SC_WORKERS
    nch = hk // SC_CHUNK

    def body(ids_hbm, act_hbm, v_hbm, out_hbm, idx_v, act_v, rows_a, rows_b, y_v, sem_a, sem_b):
        base = _sc_worker() * per_w
        lane = lax.iota(jnp.int32, SC_LANES)
        bufs = (rows_a, rows_b)
        sems = (sem_a, sem_b)

        def gather(c):
            return pltpu.async_copy(v_hbm.at[idx_v.at[pl.ds(c * SC_CHUNK, SC_CHUNK)]], bufs[c % 2], sems[c % 2])

        def token(i, carry):
            t = base + i
            pltpu.sync_copy(ids_hbm.at[t], idx_v)
            pltpu.sync_copy(act_hbm.at[t], act_v)
            for j in range(d // SC_LANES):
                y_v[pl.ds(j * SC_LANES, SC_LANES)] = jnp.zeros((SC_LANES,), F32)
            cps = {0: gather(0)}
            for c in range(nch):
                if c + 1 < nch:
                    cps[c + 1] = gather(c + 1)
                cps[c].wait()
                rows = bufs[c % 2]
                for g in range(SC_CHUNK // SC_LANES):
                    coefs = act_v[pl.ds(c * SC_CHUNK + g * SC_LANES, SC_LANES)]

                    def expert(r, carry2, g=g, rows=rows, coefs=coefs):
                        a = jnp.sum(jnp.where(lane == r, coefs, 0.0))
                        for j in range(d // SC_LANES):
                            sl = pl.ds(j * SC_LANES, SC_LANES)
                            plsc.addupdate(y_v.at[sl], a * rows[g * SC_LANES + r, sl])
                        return carry2
                    lax.fori_loop(0, SC_LANES, expert, 0)
            pltpu.sync_copy(y_v, out_hbm.at[t])
            return carry

        lax.fori_loop(0, per_w, token, 0)

    return pl.kernel(
        body, out_type=jax.ShapeDtypeStruct((ts, d), F32), mesh=_sc_mesh(),
        scratch_types=[pltpu.VMEM((hk,), jnp.int32), pltpu.VMEM((hk,), F32),
                       pltpu.VMEM((SC_CHUNK, d), F32), pltpu.VMEM((SC_CHUNK, d), F32),
                       pltpu.VMEM((d,), F32), pltpu.SemaphoreType.DMA, pltpu.SemaphoreType.DMA],
        compiler_params=pltpu.CompilerParams(needs_layout_passes=False),
        name="sc_out",
    )(ids, act, v_tab)


def _act_kernel(pre_ref, gate_ref, o_ref):
    pre = pre_ref[...]
    o_ref[...] = 0.5 * pre * (1.0 + lax.erf(pre * (2.0 ** -0.5))) * gate_ref[...]


def _act(pre, gate):
    t, hk = pre.shape
    tm = min(1024, t)
    spec = pl.BlockSpec((tm, hk), lambda i: (i, 0))
    return pl.pallas_call(
        _act_kernel, grid=(t // tm,), in_specs=[spec, spec], out_specs=spec,
        out_shape=jax.ShapeDtypeStruct((t, hk), F32),
        compiler_params=pltpu.CompilerParams(dimension_semantics=("parallel",)),
        name="peer_act",
    )(pre, gate)


def _ln2_kernel(h_ref, y_ref, g_ref, b_ref, o_ref, *, alpha):
    o_ref[...] = _layernorm(alpha * h_ref[...] + y_ref[...], g_ref[...], b_ref[...])


def _ln2(h, y, g2, b2, alpha):
    t, d = h.shape
    tm = min(512, t)
    spec = pl.BlockSpec((tm, d), lambda i: (i, 0))
    vec = pl.BlockSpec((1, d), lambda i: (0, 0))
    return pl.pallas_call(
        functools.partial(_ln2_kernel, alpha=alpha), grid=(t // tm,),
        in_specs=[spec, spec, vec, vec], out_specs=spec,
        out_shape=jax.ShapeDtypeStruct((t, d), F32),
        compiler_params=pltpu.CompilerParams(dimension_semantics=("parallel",)),
        name="peer_ln2",
    )(h, y, g2, b2)


def _peer_kernel(h_cur_ref, h_nxt_ref, wq_ref, keys_ref, flat_ref, cmask_ref, uv_hbm, g2_ref, b2_ref, o_ref,
                 q_scr, s_scr, i_scr, best_scr, eh_scr, eall_scr, idv_scr, ids_smem, gate_scr,
                 buf_a, buf_b, sem, idsem, y_scr, *, tt, alpha):
    s = pl.program_id(0)
    tg, d = h_cur_ref.shape
    nk, kk = PEER_NKEYS, PEER_TOPK
    hk = PEER_HEADS * kk
    nlh = tg // LANES
    nsub = tg // (2 * tt)
    assert nsub == PEER_HEADS * nlh
    nslab, sub = uv_hbm.shape[1], uv_hbm.shape[2]
    half = sub // 2
    bufs = (buf_a, buf_b)
    last = pl.num_programs(0) - 1
    rslot = s % 3
    pslot = (s + 2) % 3
    eslot = (s + 1) % 3

    def issue(idslot, row0, slot):
        for t in range(tt):
            for k in range(hk):
                e = ids_smem[idslot, row0 + t, k]
                pltpu.make_async_copy(uv_hbm.at[e], bufs[slot].at[:, pl.ds((t * hk + k) * sub, sub), :],
                                      sem.at[slot]).start(priority=k % 2)

    def wait(slot):
        pltpu.make_async_copy(bufs[slot], bufs[slot], sem.at[slot]).wait()

    q = _dot(h_nxt_ref[...].astype(BF16), wq_ref[...])
    for j in range(2 * PEER_HEADS):
        for lh in range(nlh):
            q_scr[j, lh] = q[lh * LANES:(lh + 1) * LANES, j * LANES:(j + 1) * LANES].astype(BF16)
    flat = flat_ref[...]
    cmask = cmask_ref[...]

    def route_piece(piece):
        hd = piece // nlh
        lh = piece % nlh
        gates, ids = _route_head(keys_ref, q_scr, hd, lh, (s_scr, i_scr, best_scr, eh_scr), flat, cmask)
        r0 = pl.multiple_of(hd * kk, kk)
        gate_scr[rslot, lh, pl.ds(r0, kk), :] = gates
        eall_scr[lh, pl.ds(r0, kk), :] = ids

    lane = lax.broadcasted_iota(jnp.int32, (hk, LANES), 1)

    def compute(row0, slot):
        buf = bufs[slot]
        gt = gate_scr[eslot, row0 // LANES]
        lane0 = row0 % LANES
        for t in range(tt):
            hrow = h_cur_ref[pl.ds(row0 + t, 1), :]

            def rows(c, j):
                return buf[c, pl.ds(t * hk * sub + j, hk, stride=sub), :]

            part = None
            for j in range(half):
                for c in range(nslab):
                    seg = j * nslab + c
                    term = rows(c, j) * hrow[:, seg * LANES:(seg + 1) * LANES]
                    part = term if part is None else part + term
            pre = jnp.sum(part, axis=-1, keepdims=True)
            gate = jnp.sum(jnp.where(lane == lane0 + t, gt, 0.0), axis=-1, keepdims=True)
            act = 0.5 * pre * (1.0 + lax.erf(pre * (2.0 ** -0.5))) * gate
            yrow = slot * tt + t
            for j in range(half):
                for c in range(nslab):
                    seg = j * nslab + c
                    y_scr[yrow:yrow + 1, seg * LANES:(seg + 1) * LANES] = jnp.sum(
                        act * rows(c, half + j), axis=0, keepdims=True)

    def substep(j, carry):
        row0 = pl.multiple_of(j * 2 * tt, 2 * tt)
        issue(eslot, row0 + tt, 1)
        route_piece(j)
        wait(0)
        compute(row0, 0)
        wait(1)
        wrap = j == nsub - 1
        issue(jnp.where(wrap, pslot, eslot), jnp.where(wrap, 0, row0 + 2 * tt), 0)
        compute(row0 + tt, 1)
        z = alpha * h_cur_ref[pl.ds(row0, 2 * tt), :] + y_scr[...]
        o_ref[pl.ds(row0, 2 * tt), :] = _layernorm(z, g2_ref[...], b2_ref[...])
        return carry

    @pl.when(s < 2)
    def _():
        o_ref[...] = jnp.zeros_like(o_ref)

        def piece(p, carry):
            route_piece(p)
            return carry
        lax.fori_loop(0, nsub, piece, 0)

    @pl.when(s == 1)
    def _():
        issue(pslot, 0, 0)

    @pl.when(s >= 2)
    def _():
        lax.fori_loop(0, nsub, substep, 0)

    @pl.when(s == last)
    def _():
        wait(0)

    for lh in range(nlh):
        idv_scr[lh * LANES:(lh + 1) * LANES, :] = eall_scr[lh].T
    publish = pltpu.make_async_copy(idv_scr, ids_smem.at[rslot], idsem)
    publish.start()
    publish.wait()


def _peer(h1, wq, keys, uv_tab, g2, b2, alpha):
    t, d = h1.shape
    tg = 256
    tt = 8
    ngrp = t // tg
    nq = wq.shape[1]
    kk = PEER_TOPK
    hk = PEER_HEADS * kk
    nlh = tg // LANES
    nslab, sub = uv_tab.shape[1], uv_tab.shape[2]
    flat, cmask = _candidate_tables()
    const = lambda shape: pl.BlockSpec(shape, lambda i: (0,) * len(shape))
    return pl.pallas_call(
        functools.partial(_peer_kernel, tt=tt, alpha=alpha),
        grid=(ngrp + 2,),
        in_specs=[pl.BlockSpec((tg, d), lambda i: (jnp.maximum(i - 2, 0), 0)),
                  pl.BlockSpec((tg, d), lambda i: (jnp.minimum(i, ngrp - 1), 0)),
                  const((d, nq)), const(keys.shape), const(flat.shape), const(cmask.shape),
                  pl.BlockSpec(memory_space=pl.ANY),
                  const((1, d)), const((1, d))],
        out_specs=pl.BlockSpec((tg, d), lambda i: (jnp.maximum(i - 2, 0), 0)),
        out_shape=jax.ShapeDtypeStruct((t, d), F32),
        scratch_shapes=[pltpu.VMEM((2 * PEER_HEADS, nlh, LANES, LANES), BF16),
                        pltpu.VMEM((2 * kk, LANES), F32), pltpu.VMEM((2 * kk, LANES), jnp.int32),
                        pltpu.VMEM((kk, LANES), F32), pltpu.VMEM((kk, LANES), jnp.int32),
                        pltpu.VMEM((nlh, hk, LANES), jnp.int32),
                        pltpu.VMEM((tg, hk), jnp.int32),
                        pltpu.SMEM((3, tg, hk), jnp.int32),
                        pltpu.VMEM((3, nlh, hk, LANES), F32),
                        pltpu.VMEM((nslab, tt * hk * sub, LANES), F32),
                        pltpu.VMEM((nslab, tt * hk * sub, LANES), F32),
                        pltpu.SemaphoreType.DMA((2,)), pltpu.SemaphoreType.DMA,
                        pltpu.VMEM((2 * tt, d), F32)],
        compiler_params=pltpu.CompilerParams(
            dimension_semantics=("arbitrary",), vmem_limit_bytes=VMEM_LIMIT),
        name="peer",
    )(h1, h1, wq, keys, flat, cmask, uv_tab, g2, b2)


def _layer(h, w_in, conv_w, a_log, dt_bias, norm_w, f_bias, w_out_gdn, w_out_fox, w_o, ln1_g, ln1_b,
           peer_wq, peer_keys, peer_u, peer_v, ln2_g, ln2_b, alpha):
    b, s, d = h.shape
    t = b * s
    qk = GDN_HEADS * GDN_DK
    fw = FX_HEADS * FX_DH
    o_gz = 4 * qk
    o_ga = o_gz
    o_fq = o_ga + 2 * GDN_HEADS
    o_ff = o_fq + 3 * fw
    o_gate = o_ff + FX_HEADS
    w_big = jnp.concatenate([w_in[:, o_gate:], w_in[:, :o_gz]], axis=1).astype(BF16)
    pad = LANES - FX_DH
    w_fox = jnp.pad(w_in[:, o_fq:o_ff].reshape(d, 3 * FX_HEADS, FX_DH), ((0, 0), (0, 0), (0, pad)))
    w_fox = w_fox.reshape(d, 3 * FX_HEADS * LANES).astype(BF16)
    w_out_fox_p = jnp.pad(w_out_fox.reshape(FX_HEADS, FX_DH, d), ((0, 0), (0, pad), (0, 0)))
    w_out_fox_p = w_out_fox_p.reshape(FX_HEADS * LANES, d).astype(BF16)
    n_small = 2 * GDN_HEADS + FX_HEADS
    w_small = jnp.concatenate([w_in[:, o_ga:o_fq], w_in[:, o_ff:o_gate],
                               jnp.zeros((d, LANES - n_small), F32)], axis=1)
    params = jnp.zeros((8, LANES), F32)
    params = params.at[0, :GDN_HEADS].set(a_log).at[1, :GDN_HEADS].set(dt_bias)
    params = params.at[2, 2 * GDN_HEADS:n_small].set(f_bias)

    x2 = h.reshape(t, d)
    proj = _in_proj(x2, w_big, F32, "in_proj")
    pf = _in_proj(x2, w_fox, BF16, "in_proj_fox")
    gexp, bexp, c, ct = _prep(h, w_small, params)
    proj3 = proj.reshape(b, s, proj.shape[1])
    gdn0 = 2 * d // LANES
    oa = _gdn(proj3, gexp, bexp, conv_w, norm_w.reshape(1, LANES),
              (gdn0, gdn0 + GDN_HEADS, gdn0 + 2 * GDN_HEADS, gdn0 + 3 * GDN_HEADS))
    ob = _fox(pf.reshape(b, s, pf.shape[1]), c, ct)
    h1 = _mix(oa.reshape(t, qk), ob.reshape(t, FX_HEADS * LANES), proj, x2,
              w_out_gdn.astype(BF16), w_out_fox_p, w_o.astype(BF16),
              ln1_g.reshape(1, d), ln1_b.reshape(1, d), alpha)
    keys = peer_keys.reshape(2 * PEER_HEADS, PEER_NKEYS, peer_keys.shape[-1]).astype(BF16)
    ne = peer_u.shape[0]
    half = PEER_SUB // 2
    nslab = d // (half * LANES)
    uv_tab = jnp.concatenate([peer_u.reshape(ne, half, nslab, LANES),
                              peer_v.reshape(ne, half, nslab, LANES)], axis=1).transpose(0, 2, 1, 3)
    wq = peer_wq.astype(BF16)
    g2, b2 = ln2_g.reshape(1, d), ln2_b.reshape(1, d)
    t_sc = (t * PEER_SC_SHARE_PCT // 100) // 256 * 256
    t_tc = t - t_sc
    if t_sc:
        h_sc = h1[t_tc:]
        ids_sc, gate_sc = _route(h_sc, wq, keys)
        pre = _sc_pre(ids_sc, h_sc, peer_u)
        out_tc = _peer(h1[:t_tc], wq, keys, uv_tab, g2, b2, alpha)
        y_sc = _sc_out(ids_sc, _act(pre, gate_sc), peer_v)
        out = jnp.concatenate([out_tc, _ln2(h_sc, y_sc, g2, b2, alpha)], axis=0)
    else:
        out = _peer(h1, wq, keys, uv_tab, g2, b2, alpha)
    return out.reshape(b, s, d)


def kernel(x, w_in, gdn_conv_w, gdn_a_log, gdn_dt_bias, gdn_norm_w, fox_f_bias, w_out_gdn, w_out_fox, w_o,
           ln1_g, ln1_b, peer_wq, peer_keys, peer_u, peer_v, ln2_g, ln2_b):
    depth = w_in.shape[0]
    alpha = (2.0 * depth) ** 0.25
    h = x
    for l in range(depth):
        h = _layer(h, w_in[l], gdn_conv_w[l], gdn_a_log[l], gdn_dt_bias[l], gdn_norm_w[l], fox_f_bias[l],
                   w_out_gdn[l], w_out_fox[l], w_o[l], ln1_g[l], ln1_b[l], peer_wq[l], peer_keys[l],
                   peer_u[l], peer_v[l], ln2_g[l], ln2_b[l], alpha)
    return h
```

```python
import functools

import jax
import jax.numpy as jnp
from jax import lax
from jax.experimental import pallas as pl
from jax.experimental.pallas import tpu as pltpu
from jax.experimental.pallas import tpu_sc as plsc

F32 = jnp.float32
BF16 = jnp.bfloat16
HI = lax.Precision.HIGHEST

LANES = 128
CHUNK = 64
GDN_HEADS = 4
GDN_DK = 128
FX_HEADS = 8
FX_DH = 64
PEER_HEADS = 8
PEER_NKEYS = 128
PEER_TOPK = 16
PEER_SUB = 4
LN_EPS = 1e-5
VMEM_LIMIT = 48 * 1024 * 1024


def _dot(a, b, prec=None):
    return jnp.dot(a, b, preferred_element_type=F32, precision=prec)


def _dot_nt(a, b, prec=None):
    return lax.dot_general(a, b, (((1,), (1,)), ((), ())), preferred_element_type=F32, precision=prec)


def _dot_tn(a, b, prec=None):
    return lax.dot_general(a, b, (((0,), (0,)), ((), ())), preferred_element_type=F32, precision=prec)


def _softplus(x):
    return jnp.maximum(x, 0.0) + jnp.log1p(jnp.exp(-jnp.abs(x)))


def _layernorm(z, g, b):
    mu = jnp.mean(z, axis=-1, keepdims=True)
    zc = z - mu
    var = jnp.mean(zc * zc, axis=-1, keepdims=True)
    return zc * lax.rsqrt(var + LN_EPS) * g + b


def _mm_kernel(x_ref, w_ref, o_ref):
    o_ref[...] = _dot(x_ref[...].astype(BF16), w_ref[...]).astype(o_ref.dtype)


def _in_proj(x2, w_big, out_dtype, name):
    t, d = x2.shape
    n = w_big.shape[1]
    tm = min(1024, t)
    tn = 512
    return pl.pallas_call(
        _mm_kernel,
        grid=(t // tm, n // tn),
        in_specs=[pl.BlockSpec((tm, d), lambda i, j: (i, 0)),
                  pl.BlockSpec((d, tn), lambda i, j: (0, j))],
        out_specs=pl.BlockSpec((tm, tn), lambda i, j: (i, j)),
        out_shape=jax.ShapeDtypeStruct((t, n), out_dtype),
        compiler_params=pltpu.CompilerParams(
            dimension_semantics=("parallel", "parallel"), vmem_limit_bytes=VMEM_LIMIT),
        name=name,
    )(x2, w_big)


def _prep_kernel(x_ref, w_ref, par_ref, gexp_ref, bexp_ref, c_ref, ct_ref, carry_ref):
    ts = x_ref.shape[1]

    @pl.when(pl.program_id(1) == 0)
    def _():
        carry_ref[...] = jnp.zeros_like(carry_ref)

    small = _dot(x_ref[0], w_ref[...], HI)
    a_log = par_ref[0:1, :]
    dt_bias = par_ref[1:2, :]
    f_bias = par_ref[2:3, :]
    g = -jnp.exp(a_log) * _softplus(small + dt_bias)
    beta = jax.nn.sigmoid(small)
    lane = lax.broadcasted_iota(jnp.int32, (ts, LANES), 1)
    log_f = jnp.where((lane >= 8) & (lane < 16), -_softplus(-(small + f_bias)), 0.0)
    row = lax.broadcasted_iota(jnp.int32, (ts, ts), 0)
    col = lax.broadcasted_iota(jnp.int32, (ts, ts), 1)
    tril = (row >= col).astype(F32)
    c = _dot(tril, log_f, HI) + carry_ref[...]
    carry_ref[...] = c[ts - 1:ts, :]
    c_ref[0] = c
    ct_ref[0] = c.T[8:16, :]
    gexp_ref[0] = jnp.concatenate(
        [jnp.broadcast_to(g[:, h:h + 1], (ts, LANES)) for h in range(GDN_HEADS)], axis=1)
    bexp_ref[0] = jnp.concatenate(
        [jnp.broadcast_to(beta[:, GDN_HEADS + h:GDN_HEADS + h + 1], (ts, LANES)) for h in range(GDN_HEADS)], axis=1)


def _prep(x, w_small, params):
    b, s, d = x.shape
    ts = min(512, s)
    hw = GDN_HEADS * LANES
    return pl.pallas_call(
        _prep_kernel,
        grid=(b, s // ts),
        in_specs=[pl.BlockSpec((1, ts, d), lambda i, j: (i, j, 0)),
                  pl.BlockSpec((d, LANES), lambda i, j: (0, 0)),
                  pl.BlockSpec((8, LANES), lambda i, j: (0, 0))],
        out_specs=[pl.BlockSpec((1, ts, hw), lambda i, j: (i, j, 0)),
                   pl.BlockSpec((1, ts, hw), lambda i, j: (i, j, 0)),
                   pl.BlockSpec((1, ts, LANES), lambda i, j: (i, j, 0)),
                   pl.BlockSpec((1, 8, ts), lambda i, j: (i, 0, j))],
        out_shape=[jax.ShapeDtypeStruct((b, s, hw), F32),
                   jax.ShapeDtypeStruct((b, s, hw), F32),
                   jax.ShapeDtypeStruct((b, s, LANES), F32),
                   jax.ShapeDtypeStruct((b, 8, s), F32)],
        scratch_shapes=[pltpu.VMEM((1, LANES), F32)],
        compiler_params=pltpu.CompilerParams(
            dimension_semantics=("parallel", "arbitrary"), vmem_limit_bytes=VMEM_LIMIT),
        name="prep",
    )(x, w_small, params)


def _each(fn, *lists):
    return [fn(*args) for args in zip(*lists)]


def _unit_lower_inverse(ms, masks):
    eye, blk16, blk32 = masks
    hi = lambda a, b: _dot(a, b, HI)
    n1 = _each(lambda m: -jnp.where(blk16, m, 0.0), ms)
    l1 = _each(lambda m: jnp.where(blk32 & jnp.logical_not(blk16), m, 0.0), ms)
    l2 = _each(lambda m: jnp.where(blk32, 0.0, m), ms)
    n2 = _each(hi, n1, n1)
    p = _each(lambda a, b: hi(eye + a, eye + b), n1, n2)
    n4 = _each(hi, n2, n2)
    p = _each(lambda a, b: hi(a, eye + b), p, n4)
    n8 = _each(hi, n4, n4)
    d_inv = _each(lambda a, b: hi(a, eye + b), p, n8)
    dl = _each(hi, d_inv, l1)
    a32 = _each(lambda a, b: a - hi(b, a), d_inv, dl)
    al = _each(hi, a32, l2)
    return _each(lambda a, b: a - hi(b, a), a32, al)


def _gdn_kernel(q_ref, k_ref, v_ref, z_ref, g_ref, b_ref, cwq_ref, cwk_ref, cwv_ref, nw_ref,
                o_ref, qn, kn, vn, st, sol_s, qk_s, qg_s, kd_s, gl_s):
    s = q_ref.shape[1]
    c = CHUNK
    nh = q_ref.shape[2] // LANES
    row = lax.broadcasted_iota(jnp.int32, (s, LANES), 0)

    def conv_silu(x, w):
        y = x * w[3:4, :]
        for sh in (1, 2, 3):
            xs = jnp.where(row >= sh, pltpu.roll(x, sh, axis=0), 0.0)
            y = y + xs * w[3 - sh:4 - sh, :]
        return y * jax.nn.sigmoid(y)

    def l2norm(x):
        return x * lax.rsqrt(jnp.sum(x * x, axis=-1, keepdims=True) + 1e-6)

    for j in range(nh):
        hs = slice(j * LANES, (j + 1) * LANES)
        qn[:, hs] = l2norm(conv_silu(q_ref[0, :, hs], cwq_ref[:, hs])) * (GDN_DK ** -0.5)
        kn[:, hs] = l2norm(conv_silu(k_ref[0, :, hs], cwk_ref[:, hs]))
        vn[:, hs] = conv_silu(v_ref[0, :, hs], cwv_ref[:, hs])
    st[...] = jnp.zeros_like(st)

    ri = lax.broadcasted_iota(jnp.int32, (c, c), 0)
    ci = lax.broadcasted_iota(jnp.int32, (c, c), 1)
    tril = ri >= ci
    strict = ri > ci
    t_inc = tril.astype(F32)
    eye = (ri == ci).astype(F32)
    blk16 = (ri >> 4) == (ci >> 4)
    blk32 = (ri >> 5) == (ci >> 5)
    l2 = lax.broadcasted_iota(jnp.int32, (c, 2 * LANES), 0)
    j2 = lax.broadcasted_iota(jnp.int32, (c, 2 * LANES), 1)
    ux = jnp.where((j2 >= c) | (l2 > j2), 1.0, 0.0).astype(F32)
    nw = nw_ref[...]

    heads = list(range(nh))
    lanes_of = [slice(j * LANES, (j + 1) * LANES) for j in heads]

    def local_stage(ns):
        pairs = [(i, j) for i in range(len(ns)) for j in heads]
        rows = [pl.ds(pl.multiple_of(n * c, c), c) for n in ns]
        q = [qn[rows[i], lanes_of[j]] for i, j in pairs]
        k = [kn[rows[i], lanes_of[j]] for i, j in pairs]
        v = [vn[rows[i], lanes_of[j]] for i, j in pairs]
        gb = [g_ref[0, rows[i], lanes_of[j]] for i, j in pairs]
        bb = [b_ref[0, rows[i], lanes_of[j]] for i, j in pairs]
        d = _each(lambda g: _dot(t_inc, jnp.concatenate([g, g], axis=1) * ux, HI), gb)
        kb = _each(lambda a, b: a * b, k, bb)
        kk = _each(lambda a, b: _dot_nt(a, b, HI), kb, k)
        qk = _each(lambda a, b: _dot_nt(a.astype(BF16), b.astype(BF16)), q, k)
        gc = [x[:, LANES:] for x in d]
        decay = [jnp.where(tril, jnp.exp(x[:, :c]), 0.0) for x in d]
        m = _each(lambda a, b: jnp.where(strict, a * b, 0.0), kk, decay)
        a_inv = _unit_lower_inverse(m, (eye, blk16, blk32))
        egc = _each(jnp.exp, gc)
        rhs = _each(lambda vv, b, kbb, e: jnp.concatenate([vv * b, kbb * e], axis=1), v, bb, kb, egc)
        sol = _each(lambda a, r: _dot(a, r, HI), a_inv, rhs)
        gl = [x[c - 1:c, :] for x in gc]
        for p, (i, j) in enumerate(pairs):
            n = ns[i]
            qk_s[n, j] = (qk[p] * decay[p]).astype(BF16)
            qg_s[n, j] = (q[p] * egc[p]).astype(BF16)
            kd_s[n, j] = (k[p] * jnp.exp(gl[p] - gc[p])).astype(BF16)
            gl_s[n, j] = gl[p]
            sol_s[n, j] = sol[p]

    def state_stage(n):
        rows = pl.ds(pl.multiple_of(n * c, c), c)
        state = [st[j] for j in heads]
        state_b = [x.astype(BF16) for x in state]
        v_new = [sol_s[n, j, :, :LANES] - _dot(sol_s[n, j, :, LANES:].astype(BF16), state_b[j])
                 for j in heads]
        v_new_b = [x.astype(BF16) for x in v_new]
        o = [_dot(qg_s[n, j], state_b[j]) + _dot(qk_s[n, j], v_new_b[j]) for j in heads]
        new_state = [state[j] * jnp.exp(gl_s[n, j]) + _dot_tn(kd_s[n, j], v_new_b[j]) for j in heads]
        for j in heads:
            st[j] = new_state[j]
            on = o[j] * lax.rsqrt(jnp.mean(o[j] * o[j], axis=-1, keepdims=True) + 1e-6) * nw
            z = z_ref[0, rows, lanes_of[j]]
            o_ref[0, rows, lanes_of[j]] = on * (z * jax.nn.sigmoid(z))

    nchunks = s // c
    group = 4

    def local_group(gidx, carry):
        local_stage([gidx * group + i for i in range(group)])
        return carry

    def state_chunk(n, carry):
        state_stage(n)
        return carry

    lax.fori_loop(0, nchunks // group, local_group, 0)
    lax.fori_loop(0, nchunks, state_chunk, 0)


def _gdn(proj3, gexp, bexp, conv_w, norm_w, cols):
    b, s, _ = proj3.shape
    nh = 2
    nc = s // CHUNK
    wd = nh * LANES
    cq, ck, cv, cz = (c0 // nh for c0 in cols)

    def blk(c0):
        return pl.BlockSpec((1, s, wd), lambda i, h: (i, 0, c0 + h))

    def cw(c0):
        return pl.BlockSpec((conv_w.shape[0], wd), lambda i, h: (0, c0 + h))

    head = pl.BlockSpec((1, s, wd), lambda i, h: (i, 0, h))
    return pl.pallas_call(
        _gdn_kernel,
        grid=(b, GDN_HEADS // nh),
        in_specs=[blk(cq), blk(ck), blk(cv), blk(cz), head, head,
                  cw(0), cw(GDN_HEADS // nh), cw(2 * GDN_HEADS // nh),
                  pl.BlockSpec((1, LANES), lambda i, h: (0, 0))],
        out_specs=head,
        out_shape=jax.ShapeDtypeStruct((b, s, GDN_HEADS * LANES), F32),
        scratch_shapes=[pltpu.VMEM((s, wd), F32), pltpu.VMEM((s, wd), F32),
                        pltpu.VMEM((s, wd), F32), pltpu.VMEM((nh, GDN_DK, LANES), F32),
                        pltpu.VMEM((nc, nh, CHUNK, 2 * LANES), F32), pltpu.VMEM((nc, nh, CHUNK, CHUNK), BF16),
                        pltpu.VMEM((nc, nh, CHUNK, LANES), BF16), pltpu.VMEM((nc, nh, CHUNK, LANES), BF16),
                        pltpu.VMEM((nc, nh, 1, LANES), F32)],
        compiler_params=pltpu.CompilerParams(
            dimension_semantics=("parallel", "parallel"), vmem_limit_bytes=VMEM_LIMIT),
        name="gdn",
    )(proj3, proj3, proj3, proj3, gexp, bexp, conv_w, conv_w, conv_w, norm_w)


def _fox_kernel(q_ref, k_ref, v_ref, c_ref, ct_ref, o_ref, *, tk):
    tq = q_ref.shape[1]
    nj = q_ref.shape[2] // LANES
    g = pl.program_id(1)
    qi = pl.program_id(2)
    q = q_ref[0]
    cblk = c_ref[0]
    lane = lax.broadcasted_iota(jnp.int32, (tq, LANES), 1)
    qpos = qi * tq + lax.broadcasted_iota(jnp.int32, (tq, tk), 0)
    kofs = lax.broadcasted_iota(jnp.int32, (tq, tk), 1)
    heads = list(range(nj))
    lanes_of = [slice(j * LANES, (j + 1) * LANES) for j in heads]
    ccol = [jnp.sum(jnp.where(lane == 8 + g * nj + j, cblk, 0.0), axis=-1, keepdims=True) for j in heads]
    qh = [q[:, hs] * jnp.asarray(FX_DH ** -0.5, BF16) for hs in lanes_of]

    def body(kv, carry):
        k0 = pl.multiple_of(kv * tk, tk)
        causal = qpos >= k0 + kofs
        kblk = k_ref[0, pl.ds(k0, tk), :]
        vblk = v_ref[0, pl.ds(k0, tk), :]
        sc = [_dot_nt(qh[j], kblk[:, lanes_of[j]]) for j in heads]
        crow = [ct_ref[0, pl.ds(g * nj + j, 1), pl.ds(k0, tk)] for j in heads]
        sc = [jnp.where(causal, sc[j] + ccol[j] - crow[j], -1e30) for j in heads]
        m_new = [jnp.maximum(carry[j][0], jnp.max(sc[j], axis=-1, keepdims=True)) for j in heads]
        a = [jnp.exp(carry[j][0] - m_new[j]) for j in heads]
        p = [jnp.exp(sc[j] - m_new[j]) for j in heads]
        l = [a[j] * carry[j][1] + jnp.sum(p[j], axis=-1, keepdims=True) for j in heads]
        acc = [a[j] * carry[j][2] + _dot(p[j].astype(BF16), vblk[:, lanes_of[j]]) for j in heads]
        return tuple((m_new[j], l[j], acc[j]) for j in heads)

    init = tuple((jnp.full((tq, 1), -1e30, F32), jnp.zeros((tq, 1), F32), jnp.zeros((tq, LANES), F32))
                 for _ in heads)
    nkv = (qi * tq + tq - 1) // tk + 1
    res = lax.fori_loop(0, nkv, body, init)
    o_ref[0] = jnp.concatenate([acc / l for _, l, acc in res], axis=1)


def _fox(pf3, c, ct):
    b, s, _ = pf3.shape
    wd = 4 * LANES
    tq = min(128, s)
    tk = min(256, s)
    ngrp = FX_HEADS * LANES // wd
    return pl.pallas_call(
        functools.partial(_fox_kernel, tk=tk),
        grid=(b, ngrp, s // tq),
        in_specs=[pl.BlockSpec((1, tq, wd), lambda i, h, t: (i, t, h)),
                  pl.BlockSpec((1, s, wd), lambda i, h, t: (i, 0, ngrp + h)),
                  pl.BlockSpec((1, s, wd), lambda i, h, t: (i, 0, 2 * ngrp + h)),
                  pl.BlockSpec((1, tq, LANES), lambda i, h, t: (i, t, 0)),
                  pl.BlockSpec((1, 8, s), lambda i, h, t: (i, 0, 0))],
        out_specs=pl.BlockSpec((1, tq, wd), lambda i, h, t: (i, t, h)),
        out_shape=jax.ShapeDtypeStruct((b, s, ngrp * wd), F32),
        compiler_params=pltpu.CompilerParams(
            dimension_semantics=("parallel", "parallel", "parallel"), vmem_limit_bytes=VMEM_LIMIT),
        name="fox",
    )(pf3, pf3, pf3, c, ct)


def _mix_kernel(oa_ref, ob_ref, ga_ref, gb_ref, x_ref, wa_ref, wb_ref, wo_ref, g1_ref, b1_ref, o_ref, *, alpha):
    ya = _dot(oa_ref[...].astype(BF16), wa_ref[...])
    yb = _dot(ob_ref[...].astype(BF16), wb_ref[...])
    mix = jax.nn.sigmoid(ga_ref[...]) * ya + jax.nn.sigmoid(gb_ref[...]) * yb
    z = alpha * x_ref[...] + _dot(mix.astype(BF16), wo_ref[...])
    o_ref[...] = _layernorm(z, g1_ref[...], b1_ref[...])


def _mix(oa, ob, proj, x2, wa, wb, wo, g1, b1, alpha):
    t, d = x2.shape
    tm = min(512, t)
    w, w2 = oa.shape[1], ob.shape[1]
    full = lambda r, c: pl.BlockSpec((r, c), lambda i: (0, 0))
    return pl.pallas_call(
        functools.partial(_mix_kernel, alpha=alpha),
        grid=(t // tm,),
        in_specs=[pl.BlockSpec((tm, w), lambda i: (i, 0)),
                  pl.BlockSpec((tm, w2), lambda i: (i, 0)),
                  pl.BlockSpec((tm, d), lambda i: (i, 0)),
                  pl.BlockSpec((tm, d), lambda i: (i, 1)),
                  pl.BlockSpec((tm, d), lambda i: (i, 0)),
                  full(w, d), full(w2, d), full(d, d), full(1, d), full(1, d)],
        out_specs=pl.BlockSpec((tm, d), lambda i: (i, 0)),
        out_shape=jax.ShapeDtypeStruct((t, d), F32),
        compiler_params=pltpu.CompilerParams(
            dimension_semantics=("parallel",), vmem_limit_bytes=VMEM_LIMIT),
        name="mix",
    )(oa, ob, proj, proj, x2, wa, wb, wo, g1, b1)


def _route_head(keys_ref, q_scr, hd, lh, scr, flat, cmask):
    s_scr, i_scr, best_scr, eh_scr = scr
    nk, kk = PEER_NKEYS, PEER_TOPK
    iota_k = lax.broadcasted_iota(jnp.int32, (nk, LANES), 0)
    neg = jnp.float32(-jnp.inf)
    for p in range(2):
        vals = _dot_nt(keys_ref[hd * 2 + p], q_scr[hd * 2 + p, lh])
        for r in range(kk):
            m = jnp.max(vals, axis=0, keepdims=True)
            am = jnp.min(jnp.where(vals == m, iota_k, nk), axis=0, keepdims=True)
            s_scr[p * kk + r:p * kk + r + 1, :] = m
            i_scr[p * kk + r:p * kk + r + 1, :] = am
            vals = jnp.where(iota_k == am, neg, vals)
    s1 = s_scr[kk:kk + 8, :]
    i1 = i_scr[kk:kk + 8, :]
    cand = [s_scr[0:1, :] + s_scr[kk:2 * kk, :]]
    cidx = [i_scr[0:1, :] * nk + i_scr[kk:2 * kk, :]]
    for a in range(1, 8):
        cand.append(s_scr[a:a + 1, :] + s1)
        cidx.append(i_scr[a:a + 1, :] * nk + i1)
    cand.append(s_scr[8:kk, :] + s_scr[kk:kk + 1, :])
    cidx.append(i_scr[8:kk, :] * nk + i_scr[kk:kk + 1, :])
    vals = jnp.concatenate(cand, axis=0) + cmask
    cidx = jnp.concatenate(cidx, axis=0)
    for r in range(kk):
        m = jnp.max(vals, axis=0, keepdims=True)
        am = jnp.min(jnp.where(vals == m, flat, 2 * kk * kk), axis=0, keepdims=True)
        sel = flat == am
        best_scr[r:r + 1, :] = m
        eh_scr[r:r + 1, :] = jnp.max(jnp.where(sel, cidx, -1), axis=0, keepdims=True)
        vals = jnp.where(sel, neg, vals)
    bs = best_scr[...]
    ex = jnp.exp(bs - bs[0:1, :])
    return ex / jnp.sum(ex, axis=0, keepdims=True), eh_scr[...]


def _candidate_tables():
    kk = PEER_TOPK
    pairs = [(0, bb) for bb in range(kk)]
    for a in range(1, 8):
        pairs += [(a, bb) for bb in range(8)]
    pairs += [(a, 0) for a in range(8, kk)]
    real = [(a + 1) * (bb + 1) <= kk for a, bb in pairs]
    flat = [a * kk + bb if ok else kk * kk + r for r, ((a, bb), ok) in enumerate(zip(pairs, real))]
    flat = jnp.broadcast_to(jnp.asarray(flat, jnp.int32)[:, None], (len(pairs), LANES))
    cmask = jnp.broadcast_to(jnp.asarray([0.0 if ok else -jnp.inf for ok in real], F32)[:, None],
                             (len(pairs), LANES))
    return flat, cmask


def _route_kernel(h_ref, wq_ref, keys_ref, flat_ref, cmask_ref, e_ref, g_ref,
                  q_scr, s_scr, i_scr, best_scr, eh_scr, eall_scr, gall_scr):
    tg = h_ref.shape[0]
    kk = PEER_TOPK
    nlh = tg // LANES
    q = _dot(h_ref[...].astype(BF16), wq_ref[...])
    for j in range(2 * PEER_HEADS):
        for lh in range(nlh):
            q_scr[j, lh] = q[lh * LANES:(lh + 1) * LANES, j * LANES:(j + 1) * LANES].astype(BF16)
    flat = flat_ref[...]
    cmask = cmask_ref[...]

    def piece(p, carry):
        hd = p // nlh
        lh = p % nlh
        gates, ids = _route_head(keys_ref, q_scr, hd, lh, (s_scr, i_scr, best_scr, eh_scr), flat, cmask)
        r0 = pl.multiple_of(hd * kk, kk)
        gall_scr[lh, pl.ds(r0, kk), :] = gates
        eall_scr[lh, pl.ds(r0, kk), :] = ids
        return carry

    lax.fori_loop(0, PEER_HEADS * nlh, piece, 0)
    for lh in range(nlh):
        e_ref[lh * LANES:(lh + 1) * LANES, :] = eall_scr[lh].T
        g_ref[lh * LANES:(lh + 1) * LANES, :] = gall_scr[lh].T


def _route(h, wq, keys):
    t, d = h.shape
    tg = 256
    nq = wq.shape[1]
    kk = PEER_TOPK
    hk = PEER_HEADS * kk
    nlh = tg // LANES
    flat, cmask = _candidate_tables()
    const = lambda shape: pl.BlockSpec(shape, lambda i: (0,) * len(shape))
    return pl.pallas_call(
        _route_kernel,
        grid=(t // tg,),
        in_specs=[pl.BlockSpec((tg, d), lambda i: (i, 0)),
                  const((d, nq)), const(keys.shape), const(flat.shape), const(cmask.shape)],
        out_specs=[pl.BlockSpec((tg, hk), lambda i: (i, 0)), pl.BlockSpec((tg, hk), lambda i: (i, 0))],
        out_shape=[jax.ShapeDtypeStruct((t, hk), jnp.int32), jax.ShapeDtypeStruct((t, hk), F32)],
        scratch_shapes=[pltpu.VMEM((2 * PEER_HEADS, nlh, LANES, LANES), BF16),
                        pltpu.VMEM((2 * kk, LANES), F32), pltpu.VMEM((2 * kk, LANES), jnp.int32),
                        pltpu.VMEM((kk, LANES), F32), pltpu.VMEM((kk, LANES), jnp.int32),
                        pltpu.VMEM((nlh, hk, LANES), jnp.int32), pltpu.VMEM((nlh, hk, LANES), F32)],
        compiler_params=pltpu.CompilerParams(
            dimension_semantics=("parallel",), vmem_limit_bytes=VMEM_LIMIT),
        name="route",
    )(h, wq, keys, flat, cmask)


PEER_SC_SHARE_PCT = 25
SC_WORKERS = 32
SC_LANES = 16
SC_CHUNK = 32
SC_DBLK = 16


def _sc_mesh():
    return plsc.VectorSubcoreMesh(core_axis_name="c", subcore_axis_name="s")


def _sc_worker():
    return lax.axis_index("s") * 2 + lax.axis_index("c")


def _sc_pre(ids, h, u_tab):
    ts, hk = ids.shape
    d = h.shape[1]
    per_w = ts // SC_WORKERS
    nch = hk // SC_CHUNK

    def body(ids_hbm, h_hbm, u_hbm, out_hbm, idx_v, h_v, rows_a, rows_b, acc_v, pre_v, sem_a, sem_b):
        base = _sc_worker() * per_w
        lane = lax.iota(jnp.int32, SC_LANES)
        bufs = (rows_a, rows_b)
        sems = (sem_a, sem_b)

        def gather(c):
            return pltpu.async_copy(u_hbm.at[idx_v.at[pl.ds(c * SC_CHUNK, SC_CHUNK)]], bufs[c % 2], sems[c % 2])

        def token(i, carry):
            t = base + i
            pltpu.sync_copy(ids_hbm.at[t], idx_v)
            pltpu.sync_copy(h_hbm.at[t], h_v)
            cps = {0: gather(0)}
            for c in range(nch):
                if c + 1 < nch:
                    cps[c + 1] = gather(c + 1)
                cps[c].wait()
                rows = bufs[c % 2]
                for db in range(ndb):
                    hregs = [h_v[pl.ds((db * SC_DBLK + j) * SC_LANES, SC_LANES)] for j in range(SC_DBLK)]
                    for g in range(SC_CHUNK // SC_LANES):
                        def expert(r, vec, g=g, rows=rows, db=db, hregs=hregs):
                            e = g * SC_LANES + r
                            acc = acc_v[e] if db else jnp.zeros((SC_LANES,), F32)
                            for j in range(SC_DBLK):
                                acc = acc + rows[e, pl.ds((db * SC_DBLK + j) * SC_LANES, SC_LANES)] * hregs[j]
                            if db + 1 < ndb:
                                acc_v[e] = acc
                                return vec
                            return jnp.where(lane == r, jnp.sum(acc), vec)
                        vec = lax.fori_loop(0, SC_LANES, expert, jnp.zeros((SC_LANES,), F32))
                        if db + 1 == ndb:
                            pre_v[pl.ds(c * SC_CHUNK + g * SC_LANES, SC_LANES)] = vec
            pltpu.sync_copy(pre_v, out_hbm.at[t])
            return carry

        lax.fori_loop(0, per_w, token, 0)

    ndb = d // (SC_DBLK * SC_LANES)
    return pl.kernel(
        body, out_type=jax.ShapeDtypeStruct((ts, hk), F32), mesh=_sc_mesh(),
        scratch_types=[pltpu.VMEM((hk,), jnp.int32), pltpu.VMEM((d,), F32),
                       pltpu.VMEM((SC_CHUNK, d), F32), pltpu.VMEM((SC_CHUNK, d), F32),
                       pltpu.VMEM((SC_CHUNK, SC_LANES), F32),
                       pltpu.VMEM((hk,), F32), pltpu.SemaphoreType.DMA, pltpu.SemaphoreType.DMA],
        compiler_params=pltpu.CompilerParams(needs_layout_passes=False),
        name="sc_pre",
    )(ids, h, u_tab)


def _sc_out(ids, act, v_tab):
    ts, hk = ids.shape
    d = v_tab.shape[1]
    per_w = ts // SC_WORKERS
    nch = hk // SC_CHUNK

    def body(ids_hbm, act_hbm, v_hbm, out_hbm, idx_v, act_v, rows_a, rows_b, y_v, sem_a, sem_b):
        base = _sc_worker() * per_w
        lane = lax.iota(jnp.int32, SC_LANES)
        bufs = (rows_a, rows_b)
        sems = (sem_a, sem_b)

        def gather(c):
            return pltpu.async_copy(v_hbm.at[idx_v.at[pl.ds(c * SC_CHUNK, SC_CHUNK)]], bufs[c % 2], sems[c % 2])

        def token(i, carry):
            t = base + i
            pltpu.sync_copy(ids_hbm.at[t], idx_v)
            pltpu.sync_copy(act_hbm.at[t], act_v)
            for j in range(d // SC_LANES):
                y_v[pl.ds(j * SC_LANES, SC_LANES)] = jnp.zeros((SC_LANES,), F32)
            cps = {0: gather(0)}
            for c in range(nch):
                if c + 1 < nch:
                    cps[c + 1] = gather(c + 1)
                cps[c].wait()
                rows = bufs[c % 2]
                for db in range(d // (SC_DBLK * SC_LANES)):
                    def expert(r, accs, c=c, rows=rows, db=db):
                        a = plsc.load_gather(act_v, [jnp.zeros((SC_LANES,), jnp.int32) + (c * SC_CHUNK + r)])
                        return tuple(
                            accs[j] + a * rows[r, pl.ds((db * SC_DBLK + j) * SC_LANES, SC_LANES)]
                            for j in range(SC_DBLK))
                    accs = lax.fori_loop(0, SC_CHUNK, expert,
                                         tuple(jnp.zeros((SC_LANES,), F32) for _ in range(SC_DBLK)))
                    for j in range(SC_DBLK):
                        plsc.addupdate(y_v.at[pl.ds((db * SC_DBLK + j) * SC_LANES, SC_LANES)], accs[j])
            pltpu.sync_copy(y_v, out_hbm.at[t])
            return carry

        lax.fori_loop(0, per_w, token, 0)

    return pl.kernel(
        body, out_type=jax.ShapeDtypeStruct((ts, d), F32), mesh=_sc_mesh(),
        scratch_types=[pltpu.VMEM((hk,), jnp.int32), pltpu.VMEM((hk,), F32),
                       pltpu.VMEM((SC_CHUNK, d), F32), pltpu.VMEM((SC_CHUNK, d), F32),
                       pltpu.VMEM((d,), F32), pltpu.SemaphoreType.DMA, pltpu.SemaphoreType.DMA],
        compiler_params=pltpu.CompilerParams(needs_layout_passes=False),
        name="sc_out",
    )(ids, act, v_tab)


def _act_kernel(pre_ref, gate_ref, o_ref):
    pre = pre_ref[...]
    o_ref[...] = 0.5 * pre * (1.0 + lax.erf(pre * (2.0 ** -0.5))) * gate_ref[...]


def _act(pre, gate):
    t, hk = pre.shape
    tm = min(1024, t)
    spec = pl.BlockSpec((tm, hk), lambda i: (i, 0))
    return pl.pallas_call(
        _act_kernel, grid=(t // tm,), in_specs=[spec, spec], out_specs=spec,
        out_shape=jax.ShapeDtypeStruct((t, hk), F32),
        compiler_params=pltpu.CompilerParams(dimension_semantics=("parallel",)),
        name="peer_act",
    )(pre, gate)


def _ln2_kernel(h_ref, y_ref, g_ref, b_ref, o_ref, *, alpha):
    o_ref[...] = _layernorm(alpha * h_ref[...] + y_ref[...], g_ref[...], b_ref[...])


def _ln2(h, y, g2, b2, alpha):
    t, d = h.shape
    tm = min(512, t)
    spec = pl.BlockSpec((tm, d), lambda i: (i, 0))
    vec = pl.BlockSpec((1, d), lambda i: (0, 0))
    return pl.pallas_call(
        functools.partial(_ln2_kernel, alpha=alpha), grid=(t // tm,),
        in_specs=[spec, spec, vec, vec], out_specs=spec,
        out_shape=jax.ShapeDtypeStruct((t, d), F32),
        compiler_params=pltpu.CompilerParams(dimension_semantics=("parallel",)),
        name="peer_ln2",
    )(h, y, g2, b2)


def _peer_kernel(h_cur_ref, h_nxt_ref, wq_ref, keys_ref, flat_ref, cmask_ref, uv_hbm, g2_ref, b2_ref, o_ref,
                 q_scr, s_scr, i_scr, best_scr, eh_scr, eall_scr, idv_scr, ids_smem, gate_scr,
                 buf_a, buf_b, sem, idsem, y_scr, *, tt, alpha):
    s = pl.program_id(0)
    tg, d = h_cur_ref.shape
    nk, kk = PEER_NKEYS, PEER_TOPK
    hk = PEER_HEADS * kk
    nlh = tg // LANES
    nsub = tg // (2 * tt)
    assert nsub == PEER_HEADS * nlh
    nslab, sub = uv_hbm.shape[1], uv_hbm.shape[2]
    half = sub // 2
    bufs = (buf_a, buf_b)
    last = pl.num_programs(0) - 1
    rslot = s % 3
    pslot = (s + 2) % 3
    eslot = (s + 1) % 3

    def issue(idslot, row0, slot):
        for t in range(tt):
            for k in range(hk):
                e = ids_smem[idslot, row0 + t, k]
                pltpu.make_async_copy(uv_hbm.at[e], bufs[slot].at[:, pl.ds((t * hk + k) * sub, sub), :],
                                      sem.at[slot]).start(priority=k % 2)

    def wait(slot):
        pltpu.make_async_copy(bufs[slot], bufs[slot], sem.at[slot]).wait()

    q = _dot(h_nxt_ref[...].astype(BF16), wq_ref[...])
    for j in range(2 * PEER_HEADS):
        for lh in range(nlh):
            q_scr[j, lh] = q[lh * LANES:(lh + 1) * LANES, j * LANES:(j + 1) * LANES].astype(BF16)
    flat = flat_ref[...]
    cmask = cmask_ref[...]

    def route_piece(piece):
        hd = piece // nlh
        lh = piece % nlh
        gates, ids = _route_head(keys_ref, q_scr, hd, lh, (s_scr, i_scr, best_scr, eh_scr), flat, cmask)
        r0 = pl.multiple_of(hd * kk, kk)
        gate_scr[rslot, lh, pl.ds(r0, kk), :] = gates
        eall_scr[lh, pl.ds(r0, kk), :] = ids

    lane = lax.broadcasted_iota(jnp.int32, (hk, LANES), 1)

    def compute(row0, slot):
        buf = bufs[slot]
        gt = gate_scr[eslot, row0 // LANES]
        lane0 = row0 % LANES
        for t in range(tt):
            hrow = h_cur_ref[pl.ds(row0 + t, 1), :]

            def rows(c, j):
                return buf[c, pl.ds(t * hk * sub + j, hk, stride=sub), :]

            part = None
            for j in range(half):
                for c in range(nslab):
                    seg = j * nslab + c
                    term = rows(c, j) * hrow[:, seg * LANES:(seg + 1) * LANES]
                    part = term if part is None else part + term
            pre = jnp.sum(part, axis=-1, keepdims=True)
            gate = jnp.sum(jnp.where(lane == lane0 + t, gt, 0.0), axis=-1, keepdims=True)
            act = 0.5 * pre * (1.0 + lax.erf(pre * (2.0 ** -0.5))) * gate
            yrow = slot * tt + t
            for j in range(half):
                for c in range(nslab):
                    seg = j * nslab + c
                    y_scr[yrow:yrow + 1, seg * LANES:(seg + 1) * LANES] = jnp.sum(
                        act * rows(c, half + j), axis=0, keepdims=True)

    def substep(j, carry):
        row0 = pl.multiple_of(j * 2 * tt, 2 * tt)
        issue(eslot, row0 + tt, 1)
        route_piece(j)
        wait(0)
        compute(row0, 0)
        wait(1)
        wrap = j == nsub - 1
        issue(jnp.where(wrap, pslot, eslot), jnp.where(wrap, 0, row0 + 2 * tt), 0)
        compute(row0 + tt, 1)
        z = alpha * h_cur_ref[pl.ds(row0, 2 * tt), :] + y_scr[...]
        o_ref[pl.ds(row0, 2 * tt), :] = _layernorm(z, g2_ref[...], b2_ref[...])
        return carry

    @pl.when(s < 2)
    def _():
        o_ref[...] = jnp.zeros_like(o_ref)

        def piece(p, carry):
            route_piece(p)
            return carry
        lax.fori_loop(0, nsub, piece, 0)

    @pl.when(s == 1)
    def _():
        issue(pslot, 0, 0)

    @pl.when(s >= 2)
    def _():
        lax.fori_loop(0, nsub, substep, 0)

    @pl.when(s == last)
    def _():
        wait(0)

    for lh in range(nlh):
        idv_scr[lh * LANES:(lh + 1) * LANES, :] = eall_scr[lh].T
    publish = pltpu.make_async_copy(idv_scr, ids_smem.at[rslot], idsem)
    publish.start()
    publish.wait()


def _peer(h1, wq, keys, uv_tab, g2, b2, alpha):
    t, d = h1.shape
    tg = 256
    tt = 8
    ngrp = t // tg
    nq = wq.shape[1]
    kk = PEER_TOPK
    hk = PEER_HEADS * kk
    nlh = tg // LANES
    nslab, sub = uv_tab.shape[1], uv_tab.shape[2]
    flat, cmask = _candidate_tables()
    const = lambda shape: pl.BlockSpec(shape, lambda i: (0,) * len(shape))
    return pl.pallas_call(
        functools.partial(_peer_kernel, tt=tt, alpha=alpha),
        grid=(ngrp + 2,),
        in_specs=[pl.BlockSpec((tg, d), lambda i: (jnp.maximum(i - 2, 0), 0)),
                  pl.BlockSpec((tg, d), lambda i: (jnp.minimum(i, ngrp - 1), 0)),
                  const((d, nq)), const(keys.shape), const(flat.shape), const(cmask.shape),
                  pl.BlockSpec(memory_space=pl.ANY),
                  const((1, d)), const((1, d))],
        out_specs=pl.BlockSpec((tg, d), lambda i: (jnp.maximum(i - 2, 0), 0)),
        out_shape=jax.ShapeDtypeStruct((t, d), F32),
        scratch_shapes=[pltpu.VMEM((2 * PEER_HEADS, nlh, LANES, LANES), BF16),
                        pltpu.VMEM((2 * kk, LANES), F32), pltpu.VMEM((2 * kk, LANES), jnp.int32),
                        pltpu.VMEM((kk, LANES), F32), pltpu.VMEM((kk, LANES), jnp.int32),
                        pltpu.VMEM((nlh, hk, LANES), jnp.int32),
                        pltpu.VMEM((tg, hk), jnp.int32),
                        pltpu.SMEM((3, tg, hk), jnp.int32),
                        pltpu.VMEM((3, nlh, hk, LANES), F32),
                        pltpu.VMEM((nslab, tt * hk * sub, LANES), F32),
                        pltpu.VMEM((nslab, tt * hk * sub, LANES), F32),
                        pltpu.SemaphoreType.DMA((2,)), pltpu.SemaphoreType.DMA,
                        pltpu.VMEM((2 * tt, d), F32)],
        compiler_params=pltpu.CompilerParams(
            dimension_semantics=("arbitrary",), vmem_limit_bytes=VMEM_LIMIT),
        name="peer",
    )(h1, h1, wq, keys, flat, cmask, uv_tab, g2, b2)


def _layer(h, w_in, conv_w, a_log, dt_bias, norm_w, f_bias, w_out_gdn, w_out_fox, w_o, ln1_g, ln1_b,
           peer_wq, peer_keys, peer_u, peer_v, ln2_g, ln2_b, alpha):
    b, s, d = h.shape
    t = b * s
    qk = GDN_HEADS * GDN_DK
    fw = FX_HEADS * FX_DH
    o_gz = 4 * qk
    o_ga = o_gz
    o_fq = o_ga + 2 * GDN_HEADS
    o_ff = o_fq + 3 * fw
    o_gate = o_ff + FX_HEADS
    w_big = jnp.concatenate([w_in[:, o_gate:], w_in[:, :o_gz]], axis=1).astype(BF16)
    pad = LANES - FX_DH
    w_fox = jnp.pad(w_in[:, o_fq:o_ff].reshape(d, 3 * FX_HEADS, FX_DH), ((0, 0), (0, 0), (0, pad)))
    w_fox = w_fox.reshape(d, 3 * FX_HEADS * LANES).astype(BF16)
    w_out_fox_p = jnp.pad(w_out_fox.reshape(FX_HEADS, FX_DH, d), ((0, 0), (0, pad), (0, 0)))
    w_out_fox_p = w_out_fox_p.reshape(FX_HEADS * LANES, d).astype(BF16)
    n_small = 2 * GDN_HEADS + FX_HEADS
    w_small = jnp.concatenate([w_in[:, o_ga:o_fq], w_in[:, o_ff:o_gate],
                               jnp.zeros((d, LANES - n_small), F32)], axis=1)
    params = jnp.zeros((8, LANES), F32)
    params = params.at[0, :GDN_HEADS].set(a_log).at[1, :GDN_HEADS].set(dt_bias)
    params = params.at[2, 2 * GDN_HEADS:n_small].set(f_bias)

    x2 = h.reshape(t, d)
    proj = _in_proj(x2, w_big, F32, "in_proj")
    pf = _in_proj(x2, w_fox, BF16, "in_proj_fox")
    gexp, bexp, c, ct = _prep(h, w_small, params)
    proj3 = proj.reshape(b, s, proj.shape[1])
    gdn0 = 2 * d // LANES
    oa = _gdn(proj3, gexp, bexp, conv_w, norm_w.reshape(1, LANES),
              (gdn0, gdn0 + GDN_HEADS, gdn0 + 2 * GDN_HEADS, gdn0 + 3 * GDN_HEADS))
    ob = _fox(pf.reshape(b, s, pf.shape[1]), c, ct)
    h1 = _mix(oa.reshape(t, qk), ob.reshape(t, FX_HEADS * LANES), proj, x2,
              w_out_gdn.astype(BF16), w_out_fox_p, w_o.astype(BF16),
              ln1_g.reshape(1, d), ln1_b.reshape(1, d), alpha)
    keys = peer_keys.reshape(2 * PEER_HEADS, PEER_NKEYS, peer_keys.shape[-1]).astype(BF16)
    ne = peer_u.shape[0]
    half = PEER_SUB // 2
    nslab = d // (half * LANES)
    uv_tab = jnp.concatenate([peer_u.reshape(ne, half, nslab, LANES),
                              peer_v.reshape(ne, half, nslab, LANES)], axis=1).transpose(0, 2, 1, 3)
    wq = peer_wq.astype(BF16)
    g2, b2 = ln2_g.reshape(1, d), ln2_b.reshape(1, d)
    t_sc = (t * PEER_SC_SHARE_PCT // 100) // 256 * 256
    t_tc = t - t_sc
    if t_sc:
        h_sc = h1[t_tc:]
        ids_sc, gate_sc = _route(h_sc, wq, keys)
        t_a = (t_tc // 2) // 256 * 256
        pre = _sc_pre(ids_sc, h_sc, peer_u)
        out_a = _peer(h1[:t_a], wq, keys, uv_tab, g2, b2, alpha)
        y_sc = _sc_out(ids_sc, _act(pre, gate_sc), peer_v)
        out_b = _peer(h1[t_a:t_tc], wq, keys, uv_tab, g2, b2, alpha)
        out = jnp.concatenate([out_a, out_b, _ln2(h_sc, y_sc, g2, b2, alpha)], axis=0)
    else:
        out = _peer(h1, wq, keys, uv_tab, g2, b2, alpha)
    return out.reshape(b, s, d)


def kernel(x, w_in, gdn_conv_w, gdn_a_log, gdn_dt_bias, gdn_norm_w, fox_f_bias, w_out_gdn, w_out_fox, w_o,
           ln1_g, ln1_b, peer_wq, peer_keys, peer_u, peer_v, ln2_g, ln2_b):
    depth = w_in.shape[0]
    alpha = (2.0 * depth) ** 0.25
    h = x
    for l in range(depth):
        h = _layer(h, w_in[l], gdn_conv_w[l], gdn_a_log[l], gdn_dt_bias[l], gdn_norm_w[l], fox_f_bias[l],
                   w_out_gdn[l], w_out_fox[l], w_o[l], ln1_g[l], ln1_b[l], peer_wq[l], peer_keys[l],
                   peer_u[l], peer_v[l], ln2_g[l], ln2_b[l], alpha)
    return h
```

```python
import functools

import jax
import jax.numpy as jnp
from jax import lax
from jax.experimental import pallas as pl
from jax.experimental.pallas import tpu as pltpu
from jax.experimental.pallas import tpu_sc as plsc

F32 = jnp.float32
BF16 = jnp.bfloat16
HI = lax.Precision.HIGHEST

LANES = 128
CHUNK = 64
GDN_HEADS = 4
GDN_DK = 128
FX_HEADS = 8
FX_DH = 64
PEER_HEADS = 8
PEER_NKEYS = 128
PEER_TOPK = 16
PEER_SUB = 4
LN_EPS = 1e-5
VMEM_LIMIT = 48 * 1024 * 1024


def _dot(a, b, prec=None):
    return jnp.dot(a, b, preferred_element_type=F32, precision=prec)


def _dot_nt(a, b, prec=None):
    return lax.dot_general(a, b, (((1,), (1,)), ((), ())), preferred_element_type=F32, precision=prec)


def _dot_tn(a, b, prec=None):
    return lax.dot_general(a, b, (((0,), (0,)), ((), ())), preferred_element_type=F32, precision=prec)


def _softplus(x):
    return jnp.maximum(x, 0.0) + jnp.log1p(jnp.exp(-jnp.abs(x)))


def _layernorm(z, g, b):
    mu = jnp.mean(z, axis=-1, keepdims=True)
    zc = z - mu
    var = jnp.mean(zc * zc, axis=-1, keepdims=True)
    return zc * lax.rsqrt(var + LN_EPS) * g + b


def _mm_kernel(x_ref, w_ref, o_ref):
    o_ref[...] = _dot(x_ref[...].astype(BF16), w_ref[...]).astype(o_ref.dtype)


def _in_proj(x2, w_big, out_dtype, name):
    t, d = x2.shape
    n = w_big.shape[1]
    tm = min(1024, t)
    tn = 512
    return pl.pallas_call(
        _mm_kernel,
        grid=(t // tm, n // tn),
        in_specs=[pl.BlockSpec((tm, d), lambda i, j: (i, 0)),
                  pl.BlockSpec((d, tn), lambda i, j: (0, j))],
        out_specs=pl.BlockSpec((tm, tn), lambda i, j: (i, j)),
        out_shape=jax.ShapeDtypeStruct((t, n), out_dtype),
        compiler_params=pltpu.CompilerParams(
            dimension_semantics=("parallel", "parallel"), vmem_limit_bytes=VMEM_LIMIT),
        name=name,
    )(x2, w_big)


def _prep_kernel(x_ref, w_ref, par_ref, gexp_ref, bexp_ref, c_ref, ct_ref, carry_ref):
    ts = x_ref.shape[1]

    @pl.when(pl.program_id(1) == 0)
    def _():
        carry_ref[...] = jnp.zeros_like(carry_ref)

    small = _dot(x_ref[0], w_ref[...], HI)
    a_log = par_ref[0:1, :]
    dt_bias = par_ref[1:2, :]
    f_bias = par_ref[2:3, :]
    g = -jnp.exp(a_log) * _softplus(small + dt_bias)
    beta = jax.nn.sigmoid(small)
    lane = lax.broadcasted_iota(jnp.int32, (ts, LANES), 1)
    log_f = jnp.where((lane >= 8) & (lane < 16), -_softplus(-(small + f_bias)), 0.0)
    row = lax.broadcasted_iota(jnp.int32, (ts, ts), 0)
    col = lax.broadcasted_iota(jnp.int32, (ts, ts), 1)
    tril = (row >= col).astype(F32)
    c = _dot(tril, log_f, HI) + carry_ref[...]
    carry_ref[...] = c[ts - 1:ts, :]
    c_ref[0] = c
    ct_ref[0] = c.T[8:16, :]
    gexp_ref[0] = jnp.concatenate(
        [jnp.broadcast_to(g[:, h:h + 1], (ts, LANES)) for h in range(GDN_HEADS)], axis=1)
    bexp_ref[0] = jnp.concatenate(
        [jnp.broadcast_to(beta[:, GDN_HEADS + h:GDN_HEADS + h + 1], (ts, LANES)) for h in range(GDN_HEADS)], axis=1)


def _prep(x, w_small, params):
    b, s, d = x.shape
    ts = min(512, s)
    hw = GDN_HEADS * LANES
    return pl.pallas_call(
        _prep_kernel,
        grid=(b, s // ts),
        in_specs=[pl.BlockSpec((1, ts, d), lambda i, j: (i, j, 0)),
                  pl.BlockSpec((d, LANES), lambda i, j: (0, 0)),
                  pl.BlockSpec((8, LANES), lambda i, j: (0, 0))],
        out_specs=[pl.BlockSpec((1, ts, hw), lambda i, j: (i, j, 0)),
                   pl.BlockSpec((1, ts, hw), lambda i, j: (i, j, 0)),
                   pl.BlockSpec((1, ts, LANES), lambda i, j: (i, j, 0)),
                   pl.BlockSpec((1, 8, ts), lambda i, j: (i, 0, j))],
        out_shape=[jax.ShapeDtypeStruct((b, s, hw), F32),
                   jax.ShapeDtypeStruct((b, s, hw), F32),
                   jax.ShapeDtypeStruct((b, s, LANES), F32),
                   jax.ShapeDtypeStruct((b, 8, s), F32)],
        scratch_shapes=[pltpu.VMEM((1, LANES), F32)],
        compiler_params=pltpu.CompilerParams(
            dimension_semantics=("parallel", "arbitrary"), vmem_limit_bytes=VMEM_LIMIT),
        name="prep",
    )(x, w_small, params)


def _each(fn, *lists):
    return [fn(*args) for args in zip(*lists)]


def _unit_lower_inverse(ms, masks):
    eye, blk16, blk32 = masks
    hi = lambda a, b: _dot(a, b, HI)
    n1 = _each(lambda m: -jnp.where(blk16, m, 0.0), ms)
    l1 = _each(lambda m: jnp.where(blk32 & jnp.logical_not(blk16), m, 0.0), ms)
    l2 = _each(lambda m: jnp.where(blk32, 0.0, m), ms)
    n2 = _each(hi, n1, n1)
    p = _each(lambda a, b: hi(eye + a, eye + b), n1, n2)
    n4 = _each(hi, n2, n2)
    p = _each(lambda a, b: hi(a, eye + b), p, n4)
    n8 = _each(hi, n4, n4)
    d_inv = _each(lambda a, b: hi(a, eye + b), p, n8)
    dl = _each(hi, d_inv, l1)
    a32 = _each(lambda a, b: a - hi(b, a), d_inv, dl)
    al = _each(hi, a32, l2)
    return _each(lambda a, b: a - hi(b, a), a32, al)


def _gdn_kernel(q_ref, k_ref, v_ref, z_ref, g_ref, b_ref, cwq_ref, cwk_ref, cwv_ref, nw_ref,
                o_ref, qn, kn, vn, st, sol_s, qk_s, qg_s, kd_s, gl_s):
    s = q_ref.shape[1]
    c = CHUNK
    nh = q_ref.shape[2] // LANES
    row = lax.broadcasted_iota(jnp.int32, (s, LANES), 0)

    def conv_silu(x, w):
        y = x * w[3:4, :]
        for sh in (1, 2, 3):
            xs = jnp.where(row >= sh, pltpu.roll(x, sh, axis=0), 0.0)
            y = y + xs * w[3 - sh:4 - sh, :]
        return y * jax.nn.sigmoid(y)

    def l2norm(x):
        return x * lax.rsqrt(jnp.sum(x * x, axis=-1, keepdims=True) + 1e-6)

    for j in range(nh):
        hs = slice(j * LANES, (j + 1) * LANES)
        qn[:, hs] = l2norm(conv_silu(q_ref[0, :, hs], cwq_ref[:, hs])) * (GDN_DK ** -0.5)
        kn[:, hs] = l2norm(conv_silu(k_ref[0, :, hs], cwk_ref[:, hs]))
        vn[:, hs] = conv_silu(v_ref[0, :, hs], cwv_ref[:, hs])
    st[...] = jnp.zeros_like(st)

    ri = lax.broadcasted_iota(jnp.int32, (c, c), 0)
    ci = lax.broadcasted_iota(jnp.int32, (c, c), 1)
    tril = ri >= ci
    strict = ri > ci
    t_inc = tril.astype(F32)
    eye = (ri == ci).astype(F32)
    blk16 = (ri >> 4) == (ci >> 4)
    blk32 = (ri >> 5) == (ci >> 5)
    l2 = lax.broadcasted_iota(jnp.int32, (c, 2 * LANES), 0)
    j2 = lax.broadcasted_iota(jnp.int32, (c, 2 * LANES), 1)
    ux = jnp.where((j2 >= c) | (l2 > j2), 1.0, 0.0).astype(F32)
    nw = nw_ref[...]

    heads = list(range(nh))
    lanes_of = [slice(j * LANES, (j + 1) * LANES) for j in heads]

    def local_stage(ns):
        pairs = [(i, j) for i in range(len(ns)) for j in heads]
        rows = [pl.ds(pl.multiple_of(n * c, c), c) for n in ns]
        q = [qn[rows[i], lanes_of[j]] for i, j in pairs]
        k = [kn[rows[i], lanes_of[j]] for i, j in pairs]
        v = [vn[rows[i], lanes_of[j]] for i, j in pairs]
        gb = [g_ref[0, rows[i], lanes_of[j]] for i, j in pairs]
        bb = [b_ref[0, rows[i], lanes_of[j]] for i, j in pairs]
        d = _each(lambda g: _dot(t_inc, jnp.concatenate([g, g], axis=1) * ux, HI), gb)
        kb = _each(lambda a, b: a * b, k, bb)
        kk = _each(lambda a, b: _dot_nt(a, b, HI), kb, k)
        qk = _each(lambda a, b: _dot_nt(a.astype(BF16), b.astype(BF16)), q, k)
        gc = [x[:, LANES:] for x in d]
        decay = [jnp.where(tril, jnp.exp(x[:, :c]), 0.0) for x in d]
        m = _each(lambda a, b: jnp.where(strict, a * b, 0.0), kk, decay)
        a_inv = _unit_lower_inverse(m, (eye, blk16, blk32))
        egc = _each(jnp.exp, gc)
        rhs = _each(lambda vv, b, kbb, e: jnp.concatenate([vv * b, kbb * e], axis=1), v, bb, kb, egc)
        sol = _each(lambda a, r: _dot(a, r, HI), a_inv, rhs)
        gl = [x[c - 1:c, :] for x in gc]
        for p, (i, j) in enumerate(pairs):
            n = ns[i]
            qk_s[n, j] = (qk[p] * decay[p]).astype(BF16)
            qg_s[n, j] = (q[p] * egc[p]).astype(BF16)
            kd_s[n, j] = (k[p] * jnp.exp(gl[p] - gc[p])).astype(BF16)
            gl_s[n, j] = gl[p]
            sol_s[n, j] = sol[p]

    def state_stage(n):
        rows = pl.ds(pl.multiple_of(n * c, c), c)
        state = [st[j] for j in heads]
        state_b = [x.astype(BF16) for x in state]
        v_new = [sol_s[n, j, :, :LANES] - _dot(sol_s[n, j, :, LANES:].astype(BF16), state_b[j])
                 for j in heads]
        v_new_b = [x.astype(BF16) for x in v_new]
        o = [_dot(qg_s[n, j], state_b[j]) + _dot(qk_s[n, j], v_new_b[j]) for j in heads]
        new_state = [state[j] * jnp.exp(gl_s[n, j]) + _dot_tn(kd_s[n, j], v_new_b[j]) for j in heads]
        for j in heads:
            st[j] = new_state[j]
            on = o[j] * lax.rsqrt(jnp.mean(o[j] * o[j], axis=-1, keepdims=True) + 1e-6) * nw
            z = z_ref[0, rows, lanes_of[j]]
            o_ref[0, rows, lanes_of[j]] = on * (z * jax.nn.sigmoid(z))

    nchunks = s // c
    group = 4

    def local_group(gidx, carry):
        local_stage([gidx * group + i for i in range(group)])
        return carry

    def state_chunk(n, carry):
        state_stage(n)
        return carry

    lax.fori_loop(0, nchunks // group, local_group, 0)
    lax.fori_loop(0, nchunks, state_chunk, 0)


def _gdn(proj3, gexp, bexp, conv_w, norm_w, cols):
    b, s, _ = proj3.shape
    nh = 2
    nc = s // CHUNK
    wd = nh * LANES
    cq, ck, cv, cz = (c0 // nh for c0 in cols)

    def blk(c0):
        return pl.BlockSpec((1, s, wd), lambda i, h: (i, 0, c0 + h))

    def cw(c0):
        return pl.BlockSpec((conv_w.shape[0], wd), lambda i, h: (0, c0 + h))

    head = pl.BlockSpec((1, s, wd), lambda i, h: (i, 0, h))
    return pl.pallas_call(
        _gdn_kernel,
        grid=(b, GDN_HEADS // nh),
        in_specs=[blk(cq), blk(ck), blk(cv), blk(cz), head, head,
                  cw(0), cw(GDN_HEADS // nh), cw(2 * GDN_HEADS // nh),
                  pl.BlockSpec((1, LANES), lambda i, h: (0, 0))],
        out_specs=head,
        out_shape=jax.ShapeDtypeStruct((b, s, GDN_HEADS * LANES), F32),
        scratch_shapes=[pltpu.VMEM((s, wd), F32), pltpu.VMEM((s, wd), F32),
                        pltpu.VMEM((s, wd), F32), pltpu.VMEM((nh, GDN_DK, LANES), F32),
                        pltpu.VMEM((nc, nh, CHUNK, 2 * LANES), F32), pltpu.VMEM((nc, nh, CHUNK, CHUNK), BF16),
                        pltpu.VMEM((nc, nh, CHUNK, LANES), BF16), pltpu.VMEM((nc, nh, CHUNK, LANES), BF16),
                        pltpu.VMEM((nc, nh, 1, LANES), F32)],
        compiler_params=pltpu.CompilerParams(
            dimension_semantics=("parallel", "parallel"), vmem_limit_bytes=VMEM_LIMIT),
        name="gdn",
    )(proj3, proj3, proj3, proj3, gexp, bexp, conv_w, conv_w, conv_w, norm_w)


def _fox_kernel(q_ref, k_ref, v_ref, c_ref, ct_ref, o_ref, *, tk):
    tq = q_ref.shape[1]
    nj = q_ref.shape[2] // LANES
    g = pl.program_id(1)
    qi = pl.program_id(2)
    q = q_ref[0]
    cblk = c_ref[0]
    lane = lax.broadcasted_iota(jnp.int32, (tq, LANES), 1)
    qpos = qi * tq + lax.broadcasted_iota(jnp.int32, (tq, tk), 0)
    kofs = lax.broadcasted_iota(jnp.int32, (tq, tk), 1)
    heads = list(range(nj))
    lanes_of = [slice(j * LANES, (j + 1) * LANES) for j in heads]
    ccol = [jnp.sum(jnp.where(lane == 8 + g * nj + j, cblk, 0.0), axis=-1, keepdims=True) for j in heads]
    qh = [q[:, hs] * jnp.asarray(FX_DH ** -0.5, BF16) for hs in lanes_of]

    def body(kv, carry):
        k0 = pl.multiple_of(kv * tk, tk)
        causal = qpos >= k0 + kofs
        kblk = k_ref[0, pl.ds(k0, tk), :]
        vblk = v_ref[0, pl.ds(k0, tk), :]
        sc = [_dot_nt(qh[j], kblk[:, lanes_of[j]]) for j in heads]
        crow = [ct_ref[0, pl.ds(g * nj + j, 1), pl.ds(k0, tk)] for j in heads]
        sc = [jnp.where(causal, sc[j] + ccol[j] - crow[j], -1e30) for j in heads]
        m_new = [jnp.maximum(carry[j][0], jnp.max(sc[j], axis=-1, keepdims=True)) for j in heads]
        a = [jnp.exp(carry[j][0] - m_new[j]) for j in heads]
        p = [jnp.exp(sc[j] - m_new[j]) for j in heads]
        l = [a[j] * carry[j][1] + jnp.sum(p[j], axis=-1, keepdims=True) for j in heads]
        acc = [a[j] * carry[j][2] + _dot(p[j].astype(BF16), vblk[:, lanes_of[j]]) for j in heads]
        return tuple((m_new[j], l[j], acc[j]) for j in heads)

    init = tuple((jnp.full((tq, 1), -1e30, F32), jnp.zeros((tq, 1), F32), jnp.zeros((tq, LANES), F32))
                 for _ in heads)
    nkv = (qi * tq + tq - 1) // tk + 1
    res = lax.fori_loop(0, nkv, body, init)
    o_ref[0] = jnp.concatenate([acc / l for _, l, acc in res], axis=1)


def _fox(pf3, c, ct):
    b, s, _ = pf3.shape
    wd = 4 * LANES
    tq = min(128, s)
    tk = min(256, s)
    ngrp = FX_HEADS * LANES // wd
    return pl.pallas_call(
        functools.partial(_fox_kernel, tk=tk),
        grid=(b, ngrp, s // tq),
        in_specs=[pl.BlockSpec((1, tq, wd), lambda i, h, t: (i, t, h)),
                  pl.BlockSpec((1, s, wd), lambda i, h, t: (i, 0, ngrp + h)),
                  pl.BlockSpec((1, s, wd), lambda i, h, t: (i, 0, 2 * ngrp + h)),
                  pl.BlockSpec((1, tq, LANES), lambda i, h, t: (i, t, 0)),
                  pl.BlockSpec((1, 8, s), lambda i, h, t: (i, 0, 0))],
        out_specs=pl.BlockSpec((1, tq, wd), lambda i, h, t: (i, t, h)),
        out_shape=jax.ShapeDtypeStruct((b, s, ngrp * wd), F32),
        compiler_params=pltpu.CompilerParams(
            dimension_semantics=("parallel", "parallel", "parallel"), vmem_limit_bytes=VMEM_LIMIT),
        name="fox",
    )(pf3, pf3, pf3, c, ct)


def _mix_kernel(oa_ref, ob_ref, ga_ref, gb_ref, x_ref, wa_ref, wb_ref, wo_ref, g1_ref, b1_ref, o_ref, *, alpha):
    ya = _dot(oa_ref[...].astype(BF16), wa_ref[...])
    yb = _dot(ob_ref[...].astype(BF16), wb_ref[...])
    mix = jax.nn.sigmoid(ga_ref[...]) * ya + jax.nn.sigmoid(gb_ref[...]) * yb
    z = alpha * x_ref[...] + _dot(mix.astype(BF16), wo_ref[...])
    o_ref[...] = _layernorm(z, g1_ref[...], b1_ref[...])


def _mix(oa, ob, proj, x2, wa, wb, wo, g1, b1, alpha):
    t, d = x2.shape
    tm = min(512, t)
    w, w2 = oa.shape[1], ob.shape[1]
    full = lambda r, c: pl.BlockSpec((r, c), lambda i: (0, 0))
    return pl.pallas_call(
        functools.partial(_mix_kernel, alpha=alpha),
        grid=(t // tm,),
        in_specs=[pl.BlockSpec((tm, w), lambda i: (i, 0)),
                  pl.BlockSpec((tm, w2), lambda i: (i, 0)),
                  pl.BlockSpec((tm, d), lambda i: (i, 0)),
                  pl.BlockSpec((tm, d), lambda i: (i, 1)),
                  pl.BlockSpec((tm, d), lambda i: (i, 0)),
                  full(w, d), full(w2, d), full(d, d), full(1, d), full(1, d)],
        out_specs=pl.BlockSpec((tm, d), lambda i: (i, 0)),
        out_shape=jax.ShapeDtypeStruct((t, d), F32),
        compiler_params=pltpu.CompilerParams(
            dimension_semantics=("parallel",), vmem_limit_bytes=VMEM_LIMIT),
        name="mix",
    )(oa, ob, proj, proj, x2, wa, wb, wo, g1, b1)


def _route_head(keys_ref, q_scr, hd, lh, scr, flat, cmask):
    s_scr, i_scr, best_scr, eh_scr = scr
    nk, kk = PEER_NKEYS, PEER_TOPK
    iota_k = lax.broadcasted_iota(jnp.int32, (nk, LANES), 0)
    neg = jnp.float32(-jnp.inf)
    for p in range(2):
        vals = _dot_nt(keys_ref[hd * 2 + p], q_scr[hd * 2 + p, lh])
        for r in range(kk):
            m = jnp.max(vals, axis=0, keepdims=True)
            am = jnp.min(jnp.where(vals == m, iota_k, nk), axis=0, keepdims=True)
            s_scr[p * kk + r:p * kk + r + 1, :] = m
            i_scr[p * kk + r:p * kk + r + 1, :] = am
            vals = jnp.where(iota_k == am, neg, vals)
    s1 = s_scr[kk:kk + 8, :]
    i1 = i_scr[kk:kk + 8, :]
    cand = [s_scr[0:1, :] + s_scr[kk:2 * kk, :]]
    cidx = [i_scr[0:1, :] * nk + i_scr[kk:2 * kk, :]]
    for a in range(1, 8):
        cand.append(s_scr[a:a + 1, :] + s1)
        cidx.append(i_scr[a:a + 1, :] * nk + i1)
    cand.append(s_scr[8:kk, :] + s_scr[kk:kk + 1, :])
    cidx.append(i_scr[8:kk, :] * nk + i_scr[kk:kk + 1, :])
    vals = jnp.concatenate(cand, axis=0) + cmask
    cidx = jnp.concatenate(cidx, axis=0)
    for r in range(kk):
        m = jnp.max(vals, axis=0, keepdims=True)
        am = jnp.min(jnp.where(vals == m, flat, 2 * kk * kk), axis=0, keepdims=True)
        sel = flat == am
        best_scr[r:r + 1, :] = m
        eh_scr[r:r + 1, :] = jnp.max(jnp.where(sel, cidx, -1), axis=0, keepdims=True)
        vals = jnp.where(sel, neg, vals)
    bs = best_scr[...]
    ex = jnp.exp(bs - bs[0:1, :])
    return ex / jnp.sum(ex, axis=0, keepdims=True), eh_scr[...]


def _candidate_tables():
    kk = PEER_TOPK
    pairs = [(0, bb) for bb in range(kk)]
    for a in range(1, 8):
        pairs += [(a, bb) for bb in range(8)]
    pairs += [(a, 0) for a in range(8, kk)]
    real = [(a + 1) * (bb + 1) <= kk for a, bb in pairs]
    flat = [a * kk + bb if ok else kk * kk + r for r, ((a, bb), ok) in enumerate(zip(pairs, real))]
    flat = jnp.broadcast_to(jnp.asarray(flat, jnp.int32)[:, None], (len(pairs), LANES))
    cmask = jnp.broadcast_to(jnp.asarray([0.0 if ok else -jnp.inf for ok in real], F32)[:, None],
                             (len(pairs), LANES))
    return flat, cmask


def _route_kernel(h_ref, wq_ref, keys_ref, flat_ref, cmask_ref, e_ref, g_ref,
                  q_scr, s_scr, i_scr, best_scr, eh_scr, eall_scr, gall_scr):
    tg = h_ref.shape[0]
    kk = PEER_TOPK
    nlh = tg // LANES
    q = _dot(h_ref[...].astype(BF16), wq_ref[...])
    for j in range(2 * PEER_HEADS):
        for lh in range(nlh):
            q_scr[j, lh] = q[lh * LANES:(lh + 1) * LANES, j * LANES:(j + 1) * LANES].astype(BF16)
    flat = flat_ref[...]
    cmask = cmask_ref[...]

    def piece(p, carry):
        hd = p // nlh
        lh = p % nlh
        gates, ids = _route_head(keys_ref, q_scr, hd, lh, (s_scr, i_scr, best_scr, eh_scr), flat, cmask)
        r0 = pl.multiple_of(hd * kk, kk)
        gall_scr[lh, pl.ds(r0, kk), :] = gates
        eall_scr[lh, pl.ds(r0, kk), :] = ids
        return carry

    lax.fori_loop(0, PEER_HEADS * nlh, piece, 0)
    for lh in range(nlh):
        e_ref[lh * LANES:(lh + 1) * LANES, :] = eall_scr[lh].T
        g_ref[lh * LANES:(lh + 1) * LANES, :] = gall_scr[lh].T


def _route(h, wq, keys):
    t, d = h.shape
    tg = 256
    nq = wq.shape[1]
    kk = PEER_TOPK
    hk = PEER_HEADS * kk
    nlh = tg // LANES
    flat, cmask = _candidate_tables()
    const = lambda shape: pl.BlockSpec(shape, lambda i: (0,) * len(shape))
    return pl.pallas_call(
        _route_kernel,
        grid=(t // tg,),
        in_specs=[pl.BlockSpec((tg, d), lambda i: (i, 0)),
                  const((d, nq)), const(keys.shape), const(flat.shape), const(cmask.shape)],
        out_specs=[pl.BlockSpec((tg, hk), lambda i: (i, 0)), pl.BlockSpec((tg, hk), lambda i: (i, 0))],
        out_shape=[jax.ShapeDtypeStruct((t, hk), jnp.int32), jax.ShapeDtypeStruct((t, hk), F32)],
        scratch_shapes=[pltpu.VMEM((2 * PEER_HEADS, nlh, LANES, LANES), BF16),
                        pltpu.VMEM((2 * kk, LANES), F32), pltpu.VMEM((2 * kk, LANES), jnp.int32),
                        pltpu.VMEM((kk, LANES), F32), pltpu.VMEM((kk, LANES), jnp.int32),
                        pltpu.VMEM((nlh, hk, LANES), jnp.int32), pltpu.VMEM((nlh, hk, LANES), F32)],
        compiler_params=pltpu.CompilerParams(
            dimension_semantics=("parallel",), vmem_limit_bytes=VMEM_LIMIT),
        name="route",
    )(h, wq, keys, flat, cmask)


PEER_SC_SHARE_PCT = 35
SC_WORKERS = 32
SC_LANES = 16
SC_CHUNK = 32
SC_DBLK = 16


def _sc_mesh():
    return plsc.VectorSubcoreMesh(core_axis_name="c", subcore_axis_name="s")


def _sc_worker():
    return lax.axis_index("s") * 2 + lax.axis_index("c")


def _sc_pre(ids, h, u_tab):
    ts, hk = ids.shape
    d = h.shape[1]
    per_w = ts // SC_WORKERS
    nch = hk // SC_CHUNK

    def body(ids_hbm, h_hbm, u_hbm, out_hbm, idx_v, h_v, rows_a, rows_b, pre_v, sem_a, sem_b):
        base = _sc_worker() * per_w
        lane = lax.iota(jnp.int32, SC_LANES)
        bufs = (rows_a, rows_b)
        sems = (sem_a, sem_b)

        def gather(c):
            return pltpu.async_copy(u_hbm.at[idx_v.at[pl.ds(c * SC_CHUNK, SC_CHUNK)]], bufs[c % 2], sems[c % 2])

        def token(i, carry):
            t = base + i
            pltpu.sync_copy(ids_hbm.at[t], idx_v)
            pltpu.sync_copy(h_hbm.at[t], h_v)
            cps = {0: gather(0)}
            for c in range(nch):
                if c + 1 < nch:
                    cps[c + 1] = gather(c + 1)
                cps[c].wait()
                rows = bufs[c % 2]
                for g in range(SC_CHUNK // SC_LANES):
                    def expert(r, vec, g=g, rows=rows):
                        acc = jnp.zeros((SC_LANES,), F32)
                        for j in range(d // SC_LANES):
                            sl = pl.ds(j * SC_LANES, SC_LANES)
                            acc = acc + rows[g * SC_LANES + r, sl] * h_v[sl]
                        return jnp.where(lane == r, jnp.sum(acc), vec)
                    vec = lax.fori_loop(0, SC_LANES, expert, jnp.zeros((SC_LANES,), F32))
                    pre_v[pl.ds(c * SC_CHUNK + g * SC_LANES, SC_LANES)] = vec
            pltpu.sync_copy(pre_v, out_hbm.at[t])
            return carry

        lax.fori_loop(0, per_w, token, 0)

    return pl.kernel(
        body, out_type=jax.ShapeDtypeStruct((ts, hk), F32), mesh=_sc_mesh(),
        scratch_types=[pltpu.VMEM((hk,), jnp.int32), pltpu.VMEM((d,), F32),
                       pltpu.VMEM((SC_CHUNK, d), F32), pltpu.VMEM((SC_CHUNK, d), F32),
                       pltpu.VMEM((hk,), F32), pltpu.SemaphoreType.DMA, pltpu.SemaphoreType.DMA],
        compiler_params=pltpu.CompilerParams(needs_layout_passes=False),
        name="sc_pre",
    )(ids, h, u_tab)


def _sc_out(ids, act, v_tab):
    ts, hk = ids.shape
    d = v_tab.shape[1]
    per_w = ts // SC_WORKERS
    nch = hk // SC_CHUNK

    def body(ids_hbm, act_hbm, v_hbm, out_hbm, idx_v, act_v, rows_a, rows_b, y_v, sem_a, sem_b):
        base = _sc_worker() * per_w
        lane = lax.iota(jnp.int32, SC_LANES)
        bufs = (rows_a, rows_b)
        sems = (sem_a, sem_b)

        def gather(c):
            return pltpu.async_copy(v_hbm.at[idx_v.at[pl.ds(c * SC_CHUNK, SC_CHUNK)]], bufs[c % 2], sems[c % 2])

        def token(i, carry):
            t = base + i
            pltpu.sync_copy(ids_hbm.at[t], idx_v)
            pltpu.sync_copy(act_hbm.at[t], act_v)
            for j in range(d // SC_LANES):
                y_v[pl.ds(j * SC_LANES, SC_LANES)] = jnp.zeros((SC_LANES,), F32)
            cps = {0: gather(0)}
            for c in range(nch):
                if c + 1 < nch:
                    cps[c + 1] = gather(c + 1)
                cps[c].wait()
                rows = bufs[c % 2]
                for db in range(d // (SC_DBLK * SC_LANES)):
                    def expert(r, accs, c=c, rows=rows, db=db):
                        a = plsc.load_gather(act_v, [jnp.zeros((SC_LANES,), jnp.int32) + (c * SC_CHUNK + r)])
                        return tuple(
                            accs[j] + a * rows[r, pl.ds((db * SC_DBLK + j) * SC_LANES, SC_LANES)]
                            for j in range(SC_DBLK))
                    accs = lax.fori_loop(0, SC_CHUNK, expert,
                                         tuple(jnp.zeros((SC_LANES,), F32) for _ in range(SC_DBLK)))
                    for j in range(SC_DBLK):
                        plsc.addupdate(y_v.at[pl.ds((db * SC_DBLK + j) * SC_LANES, SC_LANES)], accs[j])
            pltpu.sync_copy(y_v, out_hbm.at[t])
            return carry

        lax.fori_loop(0, per_w, token, 0)

    return pl.kernel(
        body, out_type=jax.ShapeDtypeStruct((ts, d), F32), mesh=_sc_mesh(),
        scratch_types=[pltpu.VMEM((hk,), jnp.int32), pltpu.VMEM((hk,), F32),
                       pltpu.VMEM((SC_CHUNK, d), F32), pltpu.VMEM((SC_CHUNK, d), F32),
                       pltpu.VMEM((d,), F32), pltpu.SemaphoreType.DMA, pltpu.SemaphoreType.DMA],
        compiler_params=pltpu.CompilerParams(needs_layout_passes=False),
        name="sc_out",
    )(ids, act, v_tab)


def _act_kernel(pre_ref, gate_ref, after_ref, o_ref):
    del after_ref
    pre = pre_ref[...]
    o_ref[...] = 0.5 * pre * (1.0 + lax.erf(pre * (2.0 ** -0.5))) * gate_ref[...]


def _act(pre, gate, after):
    t, hk = pre.shape
    tm = min(1024, t)
    spec = pl.BlockSpec((tm, hk), lambda i: (i, 0))
    return pl.pallas_call(
        _act_kernel, grid=(t // tm,),
        in_specs=[spec, spec, pl.BlockSpec((8, LANES), lambda i: (0, 0))], out_specs=spec,
        out_shape=jax.ShapeDtypeStruct((t, hk), F32),
        compiler_params=pltpu.CompilerParams(dimension_semantics=("parallel",)),
        name="peer_act",
    )(pre, gate, after)


def _ln2_kernel(h_ref, y_ref, g_ref, b_ref, o_ref, *, alpha):
    o_ref[...] = _layernorm(alpha * h_ref[...] + y_ref[...], g_ref[...], b_ref[...])


def _ln2(h, y, g2, b2, alpha):
    t, d = h.shape
    tm = min(512, t)
    spec = pl.BlockSpec((tm, d), lambda i: (i, 0))
    vec = pl.BlockSpec((1, d), lambda i: (0, 0))
    return pl.pallas_call(
        functools.partial(_ln2_kernel, alpha=alpha), grid=(t // tm,),
        in_specs=[spec, spec, vec, vec], out_specs=spec,
        out_shape=jax.ShapeDtypeStruct((t, d), F32),
        compiler_params=pltpu.CompilerParams(dimension_semantics=("parallel",)),
        name="peer_ln2",
    )(h, y, g2, b2)


def _peer_kernel(h_cur_ref, h_nxt_ref, wq_ref, keys_ref, flat_ref, cmask_ref, uv_hbm, g2_ref, b2_ref, o_ref,
                 q_scr, s_scr, i_scr, best_scr, eh_scr, eall_scr, idv_scr, ids_smem, gate_scr,
                 buf_a, buf_b, sem, idsem, y_scr, *, tt, alpha):
    s = pl.program_id(0)
    tg, d = h_cur_ref.shape
    nk, kk = PEER_NKEYS, PEER_TOPK
    hk = PEER_HEADS * kk
    nlh = tg // LANES
    nsub = tg // (2 * tt)
    assert nsub == PEER_HEADS * nlh
    nslab, sub = uv_hbm.shape[1], uv_hbm.shape[2]
    half = sub // 2
    bufs = (buf_a, buf_b)
    last = pl.num_programs(0) - 1
    rslot = s % 3
    pslot = (s + 2) % 3
    eslot = (s + 1) % 3

    def issue(idslot, row0, slot):
        for t in range(tt):
            for k in range(hk):
                e = ids_smem[idslot, row0 + t, k]
                pltpu.make_async_copy(uv_hbm.at[e], bufs[slot].at[:, pl.ds((t * hk + k) * sub, sub), :],
                                      sem.at[slot]).start(priority=k % 2)

    def wait(slot):
        pltpu.make_async_copy(bufs[slot], bufs[slot], sem.at[slot]).wait()

    q = _dot(h_nxt_ref[...].astype(BF16), wq_ref[...])
    for j in range(2 * PEER_HEADS):
        for lh in range(nlh):
            q_scr[j, lh] = q[lh * LANES:(lh + 1) * LANES, j * LANES:(j + 1) * LANES].astype(BF16)
    flat = flat_ref[...]
    cmask = cmask_ref[...]

    def route_piece(piece):
        hd = piece // nlh
        lh = piece % nlh
        gates, ids = _route_head(keys_ref, q_scr, hd, lh, (s_scr, i_scr, best_scr, eh_scr), flat, cmask)
        r0 = pl.multiple_of(hd * kk, kk)
        gate_scr[rslot, lh, pl.ds(r0, kk), :] = gates
        eall_scr[lh, pl.ds(r0, kk), :] = ids

    lane = lax.broadcasted_iota(jnp.int32, (hk, LANES), 1)

    def compute(row0, slot):
        buf = bufs[slot]
        gt = gate_scr[eslot, row0 // LANES]
        lane0 = row0 % LANES
        for t in range(tt):
            hrow = h_cur_ref[pl.ds(row0 + t, 1), :]

            def rows(c, j):
                return buf[c, pl.ds(t * hk * sub + j, hk, stride=sub), :]

            part = None
            for j in range(half):
                for c in range(nslab):
                    seg = j * nslab + c
                    term = rows(c, j) * hrow[:, seg * LANES:(seg + 1) * LANES]
                    part = term if part is None else part + term
            pre = jnp.sum(part, axis=-1, keepdims=True)
            gate = jnp.sum(jnp.where(lane == lane0 + t, gt, 0.0), axis=-1, keepdims=True)
            act = 0.5 * pre * (1.0 + lax.erf(pre * (2.0 ** -0.5))) * gate
            yrow = slot * tt + t
            for j in range(half):
                for c in range(nslab):
                    seg = j * nslab + c
                    y_scr[yrow:yrow + 1, seg * LANES:(seg + 1) * LANES] = jnp.sum(
                        act * rows(c, half + j), axis=0, keepdims=True)

    def substep(j, carry):
        row0 = pl.multiple_of(j * 2 * tt, 2 * tt)
        issue(eslot, row0 + tt, 1)
        route_piece(j)
        wait(0)
        compute(row0, 0)
        wait(1)
        wrap = j == nsub - 1
        issue(jnp.where(wrap, pslot, eslot), jnp.where(wrap, 0, row0 + 2 * tt), 0)
        compute(row0 + tt, 1)
        z = alpha * h_cur_ref[pl.ds(row0, 2 * tt), :] + y_scr[...]
        o_ref[pl.ds(row0, 2 * tt), :] = _layernorm(z, g2_ref[...], b2_ref[...])
        return carry

    @pl.when(s < 2)
    def _():
        o_ref[...] = jnp.zeros_like(o_ref)

        def piece(p, carry):
            route_piece(p)
            return carry
        lax.fori_loop(0, nsub, piece, 0)

    @pl.when(s == 1)
    def _():
        issue(pslot, 0, 0)

    @pl.when(s >= 2)
    def _():
        lax.fori_loop(0, nsub, substep, 0)

    @pl.when(s == last)
    def _():
        wait(0)

    for lh in range(nlh):
        idv_scr[lh * LANES:(lh + 1) * LANES, :] = eall_scr[lh].T
    publish = pltpu.make_async_copy(idv_scr, ids_smem.at[rslot], idsem)
    publish.start()
    publish.wait()


def _peer(h1, wq, keys, uv_tab, g2, b2, alpha):
    t, d = h1.shape
    tg = 256
    tt = 8
    ngrp = t // tg
    nq = wq.shape[1]
    kk = PEER_TOPK
    hk = PEER_HEADS * kk
    nlh = tg // LANES
    nslab, sub = uv_tab.shape[1], uv_tab.shape[2]
    flat, cmask = _candidate_tables()
    const = lambda shape: pl.BlockSpec(shape, lambda i: (0,) * len(shape))
    return pl.pallas_call(
        functools.partial(_peer_kernel, tt=tt, alpha=alpha),
        grid=(ngrp + 2,),
        in_specs=[pl.BlockSpec((tg, d), lambda i: (jnp.maximum(i - 2, 0), 0)),
                  pl.BlockSpec((tg, d), lambda i: (jnp.minimum(i, ngrp - 1), 0)),
                  const((d, nq)), const(keys.shape), const(flat.shape), const(cmask.shape),
                  pl.BlockSpec(memory_space=pl.ANY),
                  const((1, d)), const((1, d))],
        out_specs=pl.BlockSpec((tg, d), lambda i: (jnp.maximum(i - 2, 0), 0)),
        out_shape=jax.ShapeDtypeStruct((t, d), F32),
        scratch_shapes=[pltpu.VMEM((2 * PEER_HEADS, nlh, LANES, LANES), BF16),
                        pltpu.VMEM((2 * kk, LANES), F32), pltpu.VMEM((2 * kk, LANES), jnp.int32),
                        pltpu.VMEM((kk, LANES), F32), pltpu.VMEM((kk, LANES), jnp.int32),
                        pltpu.VMEM((nlh, hk, LANES), jnp.int32),
                        pltpu.VMEM((tg, hk), jnp.int32),
                        pltpu.SMEM((3, tg, hk), jnp.int32),
                        pltpu.VMEM((3, nlh, hk, LANES), F32),
                        pltpu.VMEM((nslab, tt * hk * sub, LANES), F32),
                        pltpu.VMEM((nslab, tt * hk * sub, LANES), F32),
                        pltpu.SemaphoreType.DMA((2,)), pltpu.SemaphoreType.DMA,
                        pltpu.VMEM((2 * tt, d), F32)],
        compiler_params=pltpu.CompilerParams(
            dimension_semantics=("arbitrary",), vmem_limit_bytes=VMEM_LIMIT),
        name="peer",
    )(h1, h1, wq, keys, flat, cmask, uv_tab, g2, b2)


def _layer(h, w_in, conv_w, a_log, dt_bias, norm_w, f_bias, w_out_gdn, w_out_fox, w_o, ln1_g, ln1_b,
           peer_wq, peer_keys, peer_u, peer_v, ln2_g, ln2_b, alpha):
    b, s, d = h.shape
    t = b * s
    qk = GDN_HEADS * GDN_DK
    fw = FX_HEADS * FX_DH
    o_gz = 4 * qk
    o_ga = o_gz
    o_fq = o_ga + 2 * GDN_HEADS
    o_ff = o_fq + 3 * fw
    o_gate = o_ff + FX_HEADS
    w_big = jnp.concatenate([w_in[:, o_gate:], w_in[:, :o_gz]], axis=1).astype(BF16)
    pad = LANES - FX_DH
    w_fox = jnp.pad(w_in[:, o_fq:o_ff].reshape(d, 3 * FX_HEADS, FX_DH), ((0, 0), (0, 0), (0, pad)))
    w_fox = w_fox.reshape(d, 3 * FX_HEADS * LANES).astype(BF16)
    w_out_fox_p = jnp.pad(w_out_fox.reshape(FX_HEADS, FX_DH, d), ((0, 0), (0, pad), (0, 0)))
    w_out_fox_p = w_out_fox_p.reshape(FX_HEADS * LANES, d).astype(BF16)
    n_small = 2 * GDN_HEADS + FX_HEADS
    w_small = jnp.concatenate([w_in[:, o_ga:o_fq], w_in[:, o_ff:o_gate],
                               jnp.zeros((d, LANES - n_small), F32)], axis=1)
    params = jnp.zeros((8, LANES), F32)
    params = params.at[0, :GDN_HEADS].set(a_log).at[1, :GDN_HEADS].set(dt_bias)
    params = params.at[2, 2 * GDN_HEADS:n_small].set(f_bias)

    x2 = h.reshape(t, d)
    proj = _in_proj(x2, w_big, F32, "in_proj")
    pf = _in_proj(x2, w_fox, BF16, "in_proj_fox")
    gexp, bexp, c, ct = _prep(h, w_small, params)
    proj3 = proj.reshape(b, s, proj.shape[1])
    gdn0 = 2 * d // LANES
    oa = _gdn(proj3, gexp, bexp, conv_w, norm_w.reshape(1, LANES),
              (gdn0, gdn0 + GDN_HEADS, gdn0 + 2 * GDN_HEADS, gdn0 + 3 * GDN_HEADS))
    ob = _fox(pf.reshape(b, s, pf.shape[1]), c, ct)
    h1 = _mix(oa.reshape(t, qk), ob.reshape(t, FX_HEADS * LANES), proj, x2,
              w_out_gdn.astype(BF16), w_out_fox_p, w_o.astype(BF16),
              ln1_g.reshape(1, d), ln1_b.reshape(1, d), alpha)
    keys = peer_keys.reshape(2 * PEER_HEADS, PEER_NKEYS, peer_keys.shape[-1]).astype(BF16)
    ne = peer_u.shape[0]
    half = PEER_SUB // 2
    nslab = d // (half * LANES)
    uv_tab = jnp.concatenate([peer_u.reshape(ne, half, nslab, LANES),
                              peer_v.reshape(ne, half, nslab, LANES)], axis=1).transpose(0, 2, 1, 3)
    wq = peer_wq.astype(BF16)
    g2, b2 = ln2_g.reshape(1, d), ln2_b.reshape(1, d)
    t_sc = (t * PEER_SC_SHARE_PCT // 100) // 256 * 256
    t_tc = t - t_sc
    if t_sc:
        h_sc = h1[t_tc:]
        ids_sc, gate_sc = _route(h_sc, wq, keys)
        t_a = (t_tc // 2) // 256 * 256
        pre = _sc_pre(ids_sc, h_sc, peer_u)
        out_a = _peer(h1[:t_a], wq, keys, uv_tab, g2, b2, alpha)
        y_sc = _sc_out(ids_sc, _act(pre, gate_sc, out_a), peer_v)
        out_b = _peer(h1[t_a:t_tc], wq, keys, uv_tab, g2, b2, alpha)
        out = jnp.concatenate([out_a, out_b, _ln2(h_sc, y_sc, g2, b2, alpha)], axis=0)
    else:
        out = _peer(h1, wq, keys, uv_tab, g2, b2, alpha)
    return out.reshape(b, s, d)


def kernel(x, w_in, gdn_conv_w, gdn_a_log, gdn_dt_bias, gdn_norm_w, fox_f_bias, w_out_gdn, w_out_fox, w_o,
           ln1_g, ln1_b, peer_wq, peer_keys, peer_u, peer_v, ln2_g, ln2_b):
    depth = w_in.shape[0]
    alpha = (2.0 * depth) ** 0.25
    h = x
    for l in range(depth):
        h = _layer(h, w_in[l], gdn_conv_w[l], gdn_a_log[l], gdn_dt_bias[l], gdn_norm_w[l], fox_f_bias[l],
                   w_out_gdn[l], w_out_fox[l], w_o[l], ln1_g[l], ln1_b[l], peer_wq[l], peer_keys[l],
                   peer_u[l], peer_v[l], ln2_g[l], ln2_b[l], alpha)
    return h
```

```python
import functools

import jax
import jax.numpy as jnp
from jax import lax
from jax.experimental import pallas as pl
from jax.experimental.pallas import tpu as pltpu
from jax.experimental.pallas import tpu_sc as plsc

F32 = jnp.float32
BF16 = jnp.bfloat16
HI = lax.Precision.HIGHEST

LANES = 128
CHUNK = 64
GDN_HEADS = 4
GDN_DK = 128
FX_HEADS = 8
FX_DH = 64
PEER_HEADS = 8
PEER_NKEYS = 128
PEER_TOPK = 16
PEER_SUB = 4
LN_EPS = 1e-5
VMEM_LIMIT = 48 * 1024 * 1024


def _dot(a, b, prec=None):
    return jnp.dot(a, b, preferred_element_type=F32, precision=prec)


def _dot_nt(a, b, prec=None):
    return lax.dot_general(a, b, (((1,), (1,)), ((), ())), preferred_element_type=F32, precision=prec)


def _dot_tn(a, b, prec=None):
    return lax.dot_general(a, b, (((0,), (0,)), ((), ())), preferred_element_type=F32, precision=prec)


def _softplus(x):
    return jnp.maximum(x, 0.0) + jnp.log1p(jnp.exp(-jnp.abs(x)))


def _layernorm(z, g, b):
    mu = jnp.mean(z, axis=-1, keepdims=True)
    zc = z - mu
    var = jnp.mean(zc * zc, axis=-1, keepdims=True)
    return zc * lax.rsqrt(var + LN_EPS) * g + b


def _mm_kernel(x_ref, w_ref, o_ref):
    o_ref[...] = _dot(x_ref[...].astype(BF16), w_ref[...]).astype(o_ref.dtype)


def _in_proj(x2, w_big, out_dtype, name):
    t, d = x2.shape
    n = w_big.shape[1]
    tm = min(1024, t)
    tn = 512
    return pl.pallas_call(
        _mm_kernel,
        grid=(t // tm, n // tn),
        in_specs=[pl.BlockSpec((tm, d), lambda i, j: (i, 0)),
                  pl.BlockSpec((d, tn), lambda i, j: (0, j))],
        out_specs=pl.BlockSpec((tm, tn), lambda i, j: (i, j)),
        out_shape=jax.ShapeDtypeStruct((t, n), out_dtype),
        compiler_params=pltpu.CompilerParams(
            dimension_semantics=("parallel", "parallel"), vmem_limit_bytes=VMEM_LIMIT),
        name=name,
    )(x2, w_big)


def _prep_kernel(x_ref, w_ref, par_ref, gexp_ref, bexp_ref, c_ref, ct_ref, carry_ref):
    ts = x_ref.shape[1]

    @pl.when(pl.program_id(1) == 0)
    def _():
        carry_ref[...] = jnp.zeros_like(carry_ref)

    small = _dot(x_ref[0], w_ref[...], HI)
    a_log = par_ref[0:1, :]
    dt_bias = par_ref[1:2, :]
    f_bias = par_ref[2:3, :]
    g = -jnp.exp(a_log) * _softplus(small + dt_bias)
    beta = jax.nn.sigmoid(small)
    lane = lax.broadcasted_iota(jnp.int32, (ts, LANES), 1)
    log_f = jnp.where((lane >= 8) & (lane < 16), -_softplus(-(small + f_bias)), 0.0)
    row = lax.broadcasted_iota(jnp.int32, (ts, ts), 0)
    col = lax.broadcasted_iota(jnp.int32, (ts, ts), 1)
    tril = (row >= col).astype(F32)
    c = _dot(tril, log_f, HI) + carry_ref[...]
    carry_ref[...] = c[ts - 1:ts, :]
    c_ref[0] = c
    ct_ref[0] = c.T[8:16, :]
    gexp_ref[0] = jnp.concatenate(
        [jnp.broadcast_to(g[:, h:h + 1], (ts, LANES)) for h in range(GDN_HEADS)], axis=1)
    bexp_ref[0] = jnp.concatenate(
        [jnp.broadcast_to(beta[:, GDN_HEADS + h:GDN_HEADS + h + 1], (ts, LANES)) for h in range(GDN_HEADS)], axis=1)


def _prep(x, w_small, params):
    b, s, d = x.shape
    ts = min(512, s)
    hw = GDN_HEADS * LANES
    return pl.pallas_call(
        _prep_kernel,
        grid=(b, s // ts),
        in_specs=[pl.BlockSpec((1, ts, d), lambda i, j: (i, j, 0)),
                  pl.BlockSpec((d, LANES), lambda i, j: (0, 0)),
                  pl.BlockSpec((8, LANES), lambda i, j: (0, 0))],
        out_specs=[pl.BlockSpec((1, ts, hw), lambda i, j: (i, j, 0)),
                   pl.BlockSpec((1, ts, hw), lambda i, j: (i, j, 0)),
                   pl.BlockSpec((1, ts, LANES), lambda i, j: (i, j, 0)),
                   pl.BlockSpec((1, 8, ts), lambda i, j: (i, 0, j))],
        out_shape=[jax.ShapeDtypeStruct((b, s, hw), F32),
                   jax.ShapeDtypeStruct((b, s, hw), F32),
                   jax.ShapeDtypeStruct((b, s, LANES), F32),
                   jax.ShapeDtypeStruct((b, 8, s), F32)],
        scratch_shapes=[pltpu.VMEM((1, LANES), F32)],
        compiler_params=pltpu.CompilerParams(
            dimension_semantics=("parallel", "arbitrary"), vmem_limit_bytes=VMEM_LIMIT),
        name="prep",
    )(x, w_small, params)


def _each(fn, *lists):
    return [fn(*args) for args in zip(*lists)]


def _unit_lower_inverse(ms, masks):
    eye, blk16, blk32 = masks
    hi = lambda a, b: _dot(a, b, HI)
    n1 = _each(lambda m: -jnp.where(blk16, m, 0.0), ms)
    l1 = _each(lambda m: jnp.where(blk32 & jnp.logical_not(blk16), m, 0.0), ms)
    l2 = _each(lambda m: jnp.where(blk32, 0.0, m), ms)
    n2 = _each(hi, n1, n1)
    p = _each(lambda a, b: hi(eye + a, eye + b), n1, n2)
    n4 = _each(hi, n2, n2)
    p = _each(lambda a, b: hi(a, eye + b), p, n4)
    n8 = _each(hi, n4, n4)
    d_inv = _each(lambda a, b: hi(a, eye + b), p, n8)
    dl = _each(hi, d_inv, l1)
    a32 = _each(lambda a, b: a - hi(b, a), d_inv, dl)
    al = _each(hi, a32, l2)
    return _each(lambda a, b: a - hi(b, a), a32, al)


def _gdn_kernel(q_ref, k_ref, v_ref, z_ref, g_ref, b_ref, cwq_ref, cwk_ref, cwv_ref, nw_ref,
                o_ref, qn, kn, vn, st, sol_s, qk_s, qg_s, kd_s, gl_s):
    s = q_ref.shape[1]
    c = CHUNK
    nh = q_ref.shape[2] // LANES
    row = lax.broadcasted_iota(jnp.int32, (s, LANES), 0)

    def conv_silu(x, w):
        y = x * w[3:4, :]
        for sh in (1, 2, 3):
            xs = jnp.where(row >= sh, pltpu.roll(x, sh, axis=0), 0.0)
            y = y + xs * w[3 - sh:4 - sh, :]
        return y * jax.nn.sigmoid(y)

    def l2norm(x):
        return x * lax.rsqrt(jnp.sum(x * x, axis=-1, keepdims=True) + 1e-6)

    for j in range(nh):
        hs = slice(j * LANES, (j + 1) * LANES)
        qn[:, hs] = l2norm(conv_silu(q_ref[0, :, hs], cwq_ref[:, hs])) * (GDN_DK ** -0.5)
        kn[:, hs] = l2norm(conv_silu(k_ref[0, :, hs], cwk_ref[:, hs]))
        vn[:, hs] = conv_silu(v_ref[0, :, hs], cwv_ref[:, hs])
    st[...] = jnp.zeros_like(st)

    ri = lax.broadcasted_iota(jnp.int32, (c, c), 0)
    ci = lax.broadcasted_iota(jnp.int32, (c, c), 1)
    tril = ri >= ci
    strict = ri > ci
    t_inc = tril.astype(F32)
    eye = (ri == ci).astype(F32)
    blk16 = (ri >> 4) == (ci >> 4)
    blk32 = (ri >> 5) == (ci >> 5)
    l2 = lax.broadcasted_iota(jnp.int32, (c, 2 * LANES), 0)
    j2 = lax.broadcasted_iota(jnp.int32, (c, 2 * LANES), 1)
    ux = jnp.where((j2 >= c) | (l2 > j2), 1.0, 0.0).astype(F32)
    nw = nw_ref[...]

    heads = list(range(nh))
    lanes_of = [slice(j * LANES, (j + 1) * LANES) for j in heads]

    def local_stage(ns):
        pairs = [(i, j) for i in range(len(ns)) for j in heads]
        rows = [pl.ds(pl.multiple_of(n * c, c), c) for n in ns]
        q = [qn[rows[i], lanes_of[j]] for i, j in pairs]
        k = [kn[rows[i], lanes_of[j]] for i, j in pairs]
        v = [vn[rows[i], lanes_of[j]] for i, j in pairs]
        gb = [g_ref[0, rows[i], lanes_of[j]] for i, j in pairs]
        bb = [b_ref[0, rows[i], lanes_of[j]] for i, j in pairs]
        d = _each(lambda g: _dot(t_inc, jnp.concatenate([g, g], axis=1) * ux, HI), gb)
        kb = _each(lambda a, b: a * b, k, bb)
        kk = _each(lambda a, b: _dot_nt(a, b, HI), kb, k)
        qk = _each(lambda a, b: _dot_nt(a.astype(BF16), b.astype(BF16)), q, k)
        gc = [x[:, LANES:] for x in d]
        decay = [jnp.where(tril, jnp.exp(x[:, :c]), 0.0) for x in d]
        m = _each(lambda a, b: jnp.where(strict, a * b, 0.0), kk, decay)
        a_inv = _unit_lower_inverse(m, (eye, blk16, blk32))
        egc = _each(jnp.exp, gc)
        rhs = _each(lambda vv, b, kbb, e: jnp.concatenate([vv * b, kbb * e], axis=1), v, bb, kb, egc)
        sol = _each(lambda a, r: _dot(a, r, HI), a_inv, rhs)
        gl = [x[c - 1:c, :] for x in gc]
        for p, (i, j) in enumerate(pairs):
            n = ns[i]
            qk_s[n, j] = (qk[p] * decay[p]).astype(BF16)
            qg_s[n, j] = (q[p] * egc[p]).astype(BF16)
            kd_s[n, j] = (k[p] * jnp.exp(gl[p] - gc[p])).astype(BF16)
            gl_s[n, j] = gl[p]
            sol_s[n, j] = sol[p]

    def state_stage(n):
        rows = pl.ds(pl.multiple_of(n * c, c), c)
        state = [st[j] for j in heads]
        state_b = [x.astype(BF16) for x in state]
        v_new = [sol_s[n, j, :, :LANES] - _dot(sol_s[n, j, :, LANES:].astype(BF16), state_b[j])
                 for j in heads]
        v_new_b = [x.astype(BF16) for x in v_new]
        o = [_dot(qg_s[n, j], state_b[j]) + _dot(qk_s[n, j], v_new_b[j]) for j in heads]
        new_state = [state[j] * jnp.exp(gl_s[n, j]) + _dot_tn(kd_s[n, j], v_new_b[j]) for j in heads]
        for j in heads:
            st[j] = new_state[j]
            on = o[j] * lax.rsqrt(jnp.mean(o[j] * o[j], axis=-1, keepdims=True) + 1e-6) * nw
            z = z_ref[0, rows, lanes_of[j]]
            o_ref[0, rows, lanes_of[j]] = on * (z * jax.nn.sigmoid(z))

    nchunks = s // c
    group = 4

    def local_group(gidx, carry):
        local_stage([gidx * group + i for i in range(group)])
        return carry

    def state_chunk(n, carry):
        state_stage(n)
        return carry

    lax.fori_loop(0, nchunks // group, local_group, 0)
    lax.fori_loop(0, nchunks, state_chunk, 0)


def _gdn(proj3, gexp, bexp, conv_w, norm_w, cols):
    b, s, _ = proj3.shape
    nh = 2
    nc = s // CHUNK
    wd = nh * LANES
    cq, ck, cv, cz = (c0 // nh for c0 in cols)

    def blk(c0):
        return pl.BlockSpec((1, s, wd), lambda i, h: (i, 0, c0 + h))

    def cw(c0):
        return pl.BlockSpec((conv_w.shape[0], wd), lambda i, h: (0, c0 + h))

    head = pl.BlockSpec((1, s, wd), lambda i, h: (i, 0, h))
    return pl.pallas_call(
        _gdn_kernel,
        grid=(b, GDN_HEADS // nh),
        in_specs=[blk(cq), blk(ck), blk(cv), blk(cz), head, head,
                  cw(0), cw(GDN_HEADS // nh), cw(2 * GDN_HEADS // nh),
                  pl.BlockSpec((1, LANES), lambda i, h: (0, 0))],
        out_specs=head,
        out_shape=jax.ShapeDtypeStruct((b, s, GDN_HEADS * LANES), F32),
        scratch_shapes=[pltpu.VMEM((s, wd), F32), pltpu.VMEM((s, wd), F32),
                        pltpu.VMEM((s, wd), F32), pltpu.VMEM((nh, GDN_DK, LANES), F32),
                        pltpu.VMEM((nc, nh, CHUNK, 2 * LANES), F32), pltpu.VMEM((nc, nh, CHUNK, CHUNK), BF16),
                        pltpu.VMEM((nc, nh, CHUNK, LANES), BF16), pltpu.VMEM((nc, nh, CHUNK, LANES), BF16),
                        pltpu.VMEM((nc, nh, 1, LANES), F32)],
        compiler_params=pltpu.CompilerParams(
            dimension_semantics=("parallel", "parallel"), vmem_limit_bytes=VMEM_LIMIT),
        name="gdn",
    )(proj3, proj3, proj3, proj3, gexp, bexp, conv_w, conv_w, conv_w, norm_w)


def _fox_kernel(q_ref, k_ref, v_ref, c_ref, ct_ref, o_ref, *, tk):
    tq = q_ref.shape[1]
    nj = q_ref.shape[2] // LANES
    g = pl.program_id(1)
    qi = pl.program_id(2)
    q = q_ref[0]
    cblk = c_ref[0]
    lane = lax.broadcasted_iota(jnp.int32, (tq, LANES), 1)
    qpos = qi * tq + lax.broadcasted_iota(jnp.int32, (tq, tk), 0)
    kofs = lax.broadcasted_iota(jnp.int32, (tq, tk), 1)
    heads = list(range(nj))
    lanes_of = [slice(j * LANES, (j + 1) * LANES) for j in heads]
    ccol = [jnp.sum(jnp.where(lane == 8 + g * nj + j, cblk, 0.0), axis=-1, keepdims=True) for j in heads]
    qh = [q[:, hs] * jnp.asarray(FX_DH ** -0.5, BF16) for hs in lanes_of]

    def body(kv, carry):
        k0 = pl.multiple_of(kv * tk, tk)
        causal = qpos >= k0 + kofs
        kblk = k_ref[0, pl.ds(k0, tk), :]
        vblk = v_ref[0, pl.ds(k0, tk), :]
        sc = [_dot_nt(qh[j], kblk[:, lanes_of[j]]) for j in heads]
        crow = [ct_ref[0, pl.ds(g * nj + j, 1), pl.ds(k0, tk)] for j in heads]
        sc = [jnp.where(causal, sc[j] + ccol[j] - crow[j], -1e30) for j in heads]
        m_new = [jnp.maximum(carry[j][0], jnp.max(sc[j], axis=-1, keepdims=True)) for j in heads]
        a = [jnp.exp(carry[j][0] - m_new[j]) for j in heads]
        p = [jnp.exp(sc[j] - m_new[j]) for j in heads]
        l = [a[j] * carry[j][1] + jnp.sum(p[j], axis=-1, keepdims=True) for j in heads]
        acc = [a[j] * carry[j][2] + _dot(p[j].astype(BF16), vblk[:, lanes_of[j]]) for j in heads]
        return tuple((m_new[j], l[j], acc[j]) for j in heads)

    init = tuple((jnp.full((tq, 1), -1e30, F32), jnp.zeros((tq, 1), F32), jnp.zeros((tq, LANES), F32))
                 for _ in heads)
    nkv = (qi * tq + tq - 1) // tk + 1
    res = lax.fori_loop(0, nkv, body, init)
    o_ref[0] = jnp.concatenate([acc / l for _, l, acc in res], axis=1)


def _fox(pf3, c, ct):
    b, s, _ = pf3.shape
    wd = 4 * LANES
    tq = min(128, s)
    tk = min(256, s)
    ngrp = FX_HEADS * LANES // wd
    return pl.pallas_call(
        functools.partial(_fox_kernel, tk=tk),
        grid=(b, ngrp, s // tq),
        in_specs=[pl.BlockSpec((1, tq, wd), lambda i, h, t: (i, t, h)),
                  pl.BlockSpec((1, s, wd), lambda i, h, t: (i, 0, ngrp + h)),
                  pl.BlockSpec((1, s, wd), lambda i, h, t: (i, 0, 2 * ngrp + h)),
                  pl.BlockSpec((1, tq, LANES), lambda i, h, t: (i, t, 0)),
                  pl.BlockSpec((1, 8, s), lambda i, h, t: (i, 0, 0))],
        out_specs=pl.BlockSpec((1, tq, wd), lambda i, h, t: (i, t, h)),
        out_shape=jax.ShapeDtypeStruct((b, s, ngrp * wd), F32),
        compiler_params=pltpu.CompilerParams(
            dimension_semantics=("parallel", "parallel", "parallel"), vmem_limit_bytes=VMEM_LIMIT),
        name="fox",
    )(pf3, pf3, pf3, c, ct)


def _mix_kernel(oa_ref, ob_ref, ga_ref, gb_ref, x_ref, wa_ref, wb_ref, wo_ref, g1_ref, b1_ref, o_ref, *, alpha):
    ya = _dot(oa_ref[...].astype(BF16), wa_ref[...])
    yb = _dot(ob_ref[...].astype(BF16), wb_ref[...])
    mix = jax.nn.sigmoid(ga_ref[...]) * ya + jax.nn.sigmoid(gb_ref[...]) * yb
    z = alpha * x_ref[...] + _dot(mix.astype(BF16), wo_ref[...])
    o_ref[...] = _layernorm(z, g1_ref[...], b1_ref[...])


def _mix(oa, ob, proj, x2, wa, wb, wo, g1, b1, alpha):
    t, d = x2.shape
    tm = min(512, t)
    w, w2 = oa.shape[1], ob.shape[1]
    full = lambda r, c: pl.BlockSpec((r, c), lambda i: (0, 0))
    return pl.pallas_call(
        functools.partial(_mix_kernel, alpha=alpha),
        grid=(t // tm,),
        in_specs=[pl.BlockSpec((tm, w), lambda i: (i, 0)),
                  pl.BlockSpec((tm, w2), lambda i: (i, 0)),
                  pl.BlockSpec((tm, d), lambda i: (i, 0)),
                  pl.BlockSpec((tm, d), lambda i: (i, 1)),
                  pl.BlockSpec((tm, d), lambda i: (i, 0)),
                  full(w, d), full(w2, d), full(d, d), full(1, d), full(1, d)],
        out_specs=pl.BlockSpec((tm, d), lambda i: (i, 0)),
        out_shape=jax.ShapeDtypeStruct((t, d), F32),
        compiler_params=pltpu.CompilerParams(
            dimension_semantics=("parallel",), vmem_limit_bytes=VMEM_LIMIT),
        name="mix",
    )(oa, ob, proj, proj, x2, wa, wb, wo, g1, b1)


def _route_head(keys_ref, q_scr, hd, lh, scr, flat, cmask):
    s_scr, i_scr, best_scr, eh_scr = scr
    nk, kk = PEER_NKEYS, PEER_TOPK
    iota_k = lax.broadcasted_iota(jnp.int32, (nk, LANES), 0)
    neg = jnp.float32(-jnp.inf)
    for p in range(2):
        vals = _dot_nt(keys_ref[hd * 2 + p], q_scr[hd * 2 + p, lh])
        for r in range(kk):
            m = jnp.max(vals, axis=0, keepdims=True)
            am = jnp.min(jnp.where(vals == m, iota_k, nk), axis=0, keepdims=True)
            s_scr[p * kk + r:p * kk + r + 1, :] = m
            i_scr[p * kk + r:p * kk + r + 1, :] = am
            vals = jnp.where(iota_k == am, neg, vals)
    s1 = s_scr[kk:kk + 8, :]
    i1 = i_scr[kk:kk + 8, :]
    cand = [s_scr[0:1, :] + s_scr[kk:2 * kk, :]]
    cidx = [i_scr[0:1, :] * nk + i_scr[kk:2 * kk, :]]
    for a in range(1, 8):
        cand.append(s_scr[a:a + 1, :] + s1)
        cidx.append(i_scr[a:a + 1, :] * nk + i1)
    cand.append(s_scr[8:kk, :] + s_scr[kk:kk + 1, :])
    cidx.append(i_scr[8:kk, :] * nk + i_scr[kk:kk + 1, :])
    vals = jnp.concatenate(cand, axis=0) + cmask
    cidx = jnp.concatenate(cidx, axis=0)
    for r in range(kk):
        m = jnp.max(vals, axis=0, keepdims=True)
        am = jnp.min(jnp.where(vals == m, flat, 2 * kk * kk), axis=0, keepdims=True)
        sel = flat == am
        best_scr[r:r + 1, :] = m
        eh_scr[r:r + 1, :] = jnp.max(jnp.where(sel, cidx, -1), axis=0, keepdims=True)
        vals = jnp.where(sel, neg, vals)
    bs = best_scr[...]
    ex = jnp.exp(bs - bs[0:1, :])
    return ex / jnp.sum(ex, axis=0, keepdims=True), eh_scr[...]


def _candidate_tables():
    kk = PEER_TOPK
    pairs = [(0, bb) for bb in range(kk)]
    for a in range(1, 8):
        pairs += [(a, bb) for bb in range(8)]
    pairs += [(a, 0) for a in range(8, kk)]
    real = [(a + 1) * (bb + 1) <= kk for a, bb in pairs]
    flat = [a * kk + bb if ok else kk * kk + r for r, ((a, bb), ok) in enumerate(zip(pairs, real))]
    flat = jnp.broadcast_to(jnp.asarray(flat, jnp.int32)[:, None], (len(pairs), LANES))
    cmask = jnp.broadcast_to(jnp.asarray([0.0 if ok else -jnp.inf for ok in real], F32)[:, None],
                             (len(pairs), LANES))
    return flat, cmask


def _route_kernel(h_ref, wq_ref, keys_ref, flat_ref, cmask_ref, e_ref, g_ref,
                  q_scr, s_scr, i_scr, best_scr, eh_scr, eall_scr, gall_scr):
    tg = h_ref.shape[0]
    kk = PEER_TOPK
    nlh = tg // LANES
    q = _dot(h_ref[...].astype(BF16), wq_ref[...])
    for j in range(2 * PEER_HEADS):
        for lh in range(nlh):
            q_scr[j, lh] = q[lh * LANES:(lh + 1) * LANES, j * LANES:(j + 1) * LANES].astype(BF16)
    flat = flat_ref[...]
    cmask = cmask_ref[...]

    def piece(p, carry):
        hd = p // nlh
        lh = p % nlh
        gates, ids = _route_head(keys_ref, q_scr, hd, lh, (s_scr, i_scr, best_scr, eh_scr), flat, cmask)
        r0 = pl.multiple_of(hd * kk, kk)
        gall_scr[lh, pl.ds(r0, kk), :] = gates
        eall_scr[lh, pl.ds(r0, kk), :] = ids
        return carry

    lax.fori_loop(0, PEER_HEADS * nlh, piece, 0)
    for lh in range(nlh):
        e_ref[lh * LANES:(lh + 1) * LANES, :] = eall_scr[lh].T
        g_ref[lh * LANES:(lh + 1) * LANES, :] = gall_scr[lh].T


def _route(h, wq, keys):
    t, d = h.shape
    tg = 256
    nq = wq.shape[1]
    kk = PEER_TOPK
    hk = PEER_HEADS * kk
    nlh = tg // LANES
    flat, cmask = _candidate_tables()
    const = lambda shape: pl.BlockSpec(shape, lambda i: (0,) * len(shape))
    return pl.pallas_call(
        _route_kernel,
        grid=(t // tg,),
        in_specs=[pl.BlockSpec((tg, d), lambda i: (i, 0)),
                  const((d, nq)), const(keys.shape), const(flat.shape), const(cmask.shape)],
        out_specs=[pl.BlockSpec((tg, hk), lambda i: (i, 0)), pl.BlockSpec((tg, hk), lambda i: (i, 0))],
        out_shape=[jax.ShapeDtypeStruct((t, hk), jnp.int32), jax.ShapeDtypeStruct((t, hk), F32)],
        scratch_shapes=[pltpu.VMEM((2 * PEER_HEADS, nlh, LANES, LANES), BF16),
                        pltpu.VMEM((2 * kk, LANES), F32), pltpu.VMEM((2 * kk, LANES), jnp.int32),
                        pltpu.VMEM((kk, LANES), F32), pltpu.VMEM((kk, LANES), jnp.int32),
                        pltpu.VMEM((nlh, hk, LANES), jnp.int32), pltpu.VMEM((nlh, hk, LANES), F32)],
        compiler_params=pltpu.CompilerParams(
            dimension_semantics=("parallel",), vmem_limit_bytes=VMEM_LIMIT),
        name="route",
    )(h, wq, keys, flat, cmask)


PEER_SC_SHARE_PCT = 40
SC_WORKERS = 32
SC_LANES = 16
SC_CHUNK = 32
SC_DBLK = 16
SC_EBLK = 2


def _sc_mesh():
    return plsc.VectorSubcoreMesh(core_axis_name="c", subcore_axis_name="s")


def _sc_worker():
    return lax.axis_index("s") * 2 + lax.axis_index("c")


def _sc_pre(ids, h, u_tab):
    ts, hk = ids.shape
    d = h.shape[1]
    per_w = ts // SC_WORKERS
    nch = hk // SC_CHUNK

    def body(ids_hbm, h_hbm, u_hbm, out_hbm, idx_v, h_v, rows_a, rows_b, pre_v, sem_a, sem_b):
        base = _sc_worker() * per_w
        lane = lax.iota(jnp.int32, SC_LANES)
        bufs = (rows_a, rows_b)
        sems = (sem_a, sem_b)

        def gather(c):
            return pltpu.async_copy(u_hbm.at[idx_v.at[pl.ds(c * SC_CHUNK, SC_CHUNK)]], bufs[c % 2], sems[c % 2])

        def token(i, carry):
            t = base + i
            pltpu.sync_copy(ids_hbm.at[t], idx_v)
            pltpu.sync_copy(h_hbm.at[t], h_v)
            cps = {0: gather(0)}
            for c in range(nch):
                if c + 1 < nch:
                    cps[c + 1] = gather(c + 1)
                cps[c].wait()
                rows = bufs[c % 2]
                for g in range(SC_CHUNK // SC_LANES):
                    def experts(q, vec, g=g, rows=rows):
                        e0 = g * SC_LANES + q * SC_EBLK
                        accs = [jnp.zeros((SC_LANES,), F32) for _ in range(SC_EBLK)]
                        for j in range(d // SC_LANES):
                            sl = pl.ds(j * SC_LANES, SC_LANES)
                            hv = h_v[sl]
                            for i in range(SC_EBLK):
                                accs[i] = accs[i] + rows[e0 + i, sl] * hv
                        for i in range(SC_EBLK):
                            vec = jnp.where(lane == q * SC_EBLK + i, jnp.sum(accs[i]), vec)
                        return vec
                    vec = lax.fori_loop(0, SC_LANES // SC_EBLK, experts, jnp.zeros((SC_LANES,), F32))
                    pre_v[pl.ds(c * SC_CHUNK + g * SC_LANES, SC_LANES)] = vec
            pltpu.sync_copy(pre_v, out_hbm.at[t])
            return carry

        lax.fori_loop(0, per_w, token, 0)

    return pl.kernel(
        body, out_type=jax.ShapeDtypeStruct((ts, hk), F32), mesh=_sc_mesh(),
        scratch_types=[pltpu.VMEM((hk,), jnp.int32), pltpu.VMEM((d,), F32),
                       pltpu.VMEM((SC_CHUNK, d), F32), pltpu.VMEM((SC_CHUNK, d), F32),
                       pltpu.VMEM((hk,), F32), pltpu.SemaphoreType.DMA, pltpu.SemaphoreType.DMA],
        compiler_params=pltpu.CompilerParams(needs_layout_passes=False),
        name="sc_pre",
    )(ids, h, u_tab)


def _sc_out(ids, act, v_tab):
    ts, hk = ids.shape
    d = v_tab.shape[1]
    per_w = ts // SC_WORKERS
    nch = hk // SC_CHUNK

    def body(ids_hbm, act_hbm, v_hbm, out_hbm, idx_v, act_v, rows_a, rows_b, y_v, sem_a, sem_b):
        base = _sc_worker() * per_w
        lane = lax.iota(jnp.int32, SC_LANES)
        bufs = (rows_a, rows_b)
        sems = (sem_a, sem_b)

        def gather(c):
            return pltpu.async_copy(v_hbm.at[idx_v.at[pl.ds(c * SC_CHUNK, SC_CHUNK)]], bufs[c % 2], sems[c % 2])

        def token(i, carry):
            t = base + i
            pltpu.sync_copy(ids_hbm.at[t], idx_v)
            pltpu.sync_copy(act_hbm.at[t], act_v)
            for j in range(d // SC_LANES):
                y_v[pl.ds(j * SC_LANES, SC_LANES)] = jnp.zeros((SC_LANES,), F32)
            cps = {0: gather(0)}
            for c in range(nch):
                if c + 1 < nch:
                    cps[c + 1] = gather(c + 1)
                cps[c].wait()
                rows = bufs[c % 2]
                for db in range(d // (SC_DBLK * SC_LANES)):
                    def expert(r, accs, c=c, rows=rows, db=db):
                        a = plsc.load_gather(act_v, [jnp.zeros((SC_LANES,), jnp.int32) + (c * SC_CHUNK + r)])
                        return tuple(
                            accs[j] + a * rows[r, pl.ds((db * SC_DBLK + j) * SC_LANES, SC_LANES)]
                            for j in range(SC_DBLK))
                    accs = lax.fori_loop(0, SC_CHUNK, expert,
                                         tuple(jnp.zeros((SC_LANES,), F32) for _ in range(SC_DBLK)))
                    for j in range(SC_DBLK):
                        plsc.addupdate(y_v.at[pl.ds((db * SC_DBLK + j) * SC_LANES, SC_LANES)], accs[j])
            pltpu.sync_copy(y_v, out_hbm.at[t])
            return carry

        lax.fori_loop(0, per_w, token, 0)

    return pl.kernel(
        body, out_type=jax.ShapeDtypeStruct((ts, d), F32), mesh=_sc_mesh(),
        scratch_types=[pltpu.VMEM((hk,), jnp.int32), pltpu.VMEM((hk,), F32),
                       pltpu.VMEM((SC_CHUNK, d), F32), pltpu.VMEM((SC_CHUNK, d), F32),
                       pltpu.VMEM((d,), F32), pltpu.SemaphoreType.DMA, pltpu.SemaphoreType.DMA],
        compiler_params=pltpu.CompilerParams(needs_layout_passes=False),
        name="sc_out",
    )(ids, act, v_tab)


def _act_kernel(pre_ref, gate_ref, after_ref, o_ref):
    del after_ref
    pre = pre_ref[...]
    o_ref[...] = 0.5 * pre * (1.0 + lax.erf(pre * (2.0 ** -0.5))) * gate_ref[...]


def _act(pre, gate, after):
    t, hk = pre.shape
    tm = 256
    assert t % tm == 0
    spec = pl.BlockSpec((tm, hk), lambda i: (i, 0))
    return pl.pallas_call(
        _act_kernel, grid=(t // tm,),
        in_specs=[spec, spec, pl.BlockSpec((8, LANES), lambda i: (0, 0))], out_specs=spec,
        out_shape=jax.ShapeDtypeStruct((t, hk), F32),
        compiler_params=pltpu.CompilerParams(dimension_semantics=("parallel",)),
        name="peer_act",
    )(pre, gate, after)


def _ln2_kernel(h_ref, y_ref, g_ref, b_ref, o_ref, *, alpha):
    o_ref[...] = _layernorm(alpha * h_ref[...] + y_ref[...], g_ref[...], b_ref[...])


def _ln2(h, y, g2, b2, alpha):
    t, d = h.shape
    tm = 256
    assert t % tm == 0
    spec = pl.BlockSpec((tm, d), lambda i: (i, 0))
    vec = pl.BlockSpec((1, d), lambda i: (0, 0))
    return pl.pallas_call(
        functools.partial(_ln2_kernel, alpha=alpha), grid=(t // tm,),
        in_specs=[spec, spec, vec, vec], out_specs=spec,
        out_shape=jax.ShapeDtypeStruct((t, d), F32),
        compiler_params=pltpu.CompilerParams(dimension_semantics=("parallel",)),
        name="peer_ln2",
    )(h, y, g2, b2)


def _peer_kernel(h_cur_ref, h_nxt_ref, wq_ref, keys_ref, flat_ref, cmask_ref, uv_hbm, g2_ref, b2_ref, o_ref,
                 q_scr, s_scr, i_scr, best_scr, eh_scr, eall_scr, idv_scr, ids_smem, gate_scr,
                 buf_a, buf_b, sem, idsem, y_scr, *, tt, alpha):
    s = pl.program_id(0)
    tg, d = h_cur_ref.shape
    nk, kk = PEER_NKEYS, PEER_TOPK
    hk = PEER_HEADS * kk
    nlh = tg // LANES
    nsub = tg // (2 * tt)
    assert nsub == PEER_HEADS * nlh
    nslab, sub = uv_hbm.shape[1], uv_hbm.shape[2]
    half = sub // 2
    bufs = (buf_a, buf_b)
    last = pl.num_programs(0) - 1
    rslot = s % 3
    pslot = (s + 2) % 3
    eslot = (s + 1) % 3

    def issue(idslot, row0, slot):
        for t in range(tt):
            for k in range(hk):
                e = ids_smem[idslot, row0 + t, k]
                pltpu.make_async_copy(uv_hbm.at[e], bufs[slot].at[:, pl.ds((t * hk + k) * sub, sub), :],
                                      sem.at[slot]).start(priority=k % 2)

    def wait(slot):
        pltpu.make_async_copy(bufs[slot], bufs[slot], sem.at[slot]).wait()

    q = _dot(h_nxt_ref[...].astype(BF16), wq_ref[...])
    for j in range(2 * PEER_HEADS):
        for lh in range(nlh):
            q_scr[j, lh] = q[lh * LANES:(lh + 1) * LANES, j * LANES:(j + 1) * LANES].astype(BF16)
    flat = flat_ref[...]
    cmask = cmask_ref[...]

    def route_piece(piece):
        hd = piece // nlh
        lh = piece % nlh
        gates, ids = _route_head(keys_ref, q_scr, hd, lh, (s_scr, i_scr, best_scr, eh_scr), flat, cmask)
        r0 = pl.multiple_of(hd * kk, kk)
        gate_scr[rslot, lh, pl.ds(r0, kk), :] = gates
        eall_scr[lh, pl.ds(r0, kk), :] = ids

    lane = lax.broadcasted_iota(jnp.int32, (hk, LANES), 1)

    def compute(row0, slot):
        buf = bufs[slot]
        gt = gate_scr[eslot, row0 // LANES]
        lane0 = row0 % LANES
        for t in range(tt):
            hrow = h_cur_ref[pl.ds(row0 + t, 1), :]

            def rows(c, j):
                return buf[c, pl.ds(t * hk * sub + j, hk, stride=sub), :]

            part = None
            for j in range(half):
                for c in range(nslab):
                    seg = j * nslab + c
                    term = rows(c, j) * hrow[:, seg * LANES:(seg + 1) * LANES]
                    part = term if part is None else part + term
            pre = jnp.sum(part, axis=-1, keepdims=True)
            gate = jnp.sum(jnp.where(lane == lane0 + t, gt, 0.0), axis=-1, keepdims=True)
            act = 0.5 * pre * (1.0 + lax.erf(pre * (2.0 ** -0.5))) * gate
            yrow = slot * tt + t
            for j in range(half):
                for c in range(nslab):
                    seg = j * nslab + c
                    y_scr[yrow:yrow + 1, seg * LANES:(seg + 1) * LANES] = jnp.sum(
                        act * rows(c, half + j), axis=0, keepdims=True)

    def substep(j, carry):
        row0 = pl.multiple_of(j * 2 * tt, 2 * tt)
        issue(eslot, row0 + tt, 1)
        route_piece(j)
        wait(0)
        compute(row0, 0)
        wait(1)
        wrap = j == nsub - 1
        issue(jnp.where(wrap, pslot, eslot), jnp.where(wrap, 0, row0 + 2 * tt), 0)
        compute(row0 + tt, 1)
        z = alpha * h_cur_ref[pl.ds(row0, 2 * tt), :] + y_scr[...]
        o_ref[pl.ds(row0, 2 * tt), :] = _layernorm(z, g2_ref[...], b2_ref[...])
        return carry

    @pl.when(s < 2)
    def _():
        o_ref[...] = jnp.zeros_like(o_ref)

        def piece(p, carry):
            route_piece(p)
            return carry
        lax.fori_loop(0, nsub, piece, 0)

    @pl.when(s == 1)
    def _():
        issue(pslot, 0, 0)

    @pl.when(s >= 2)
    def _():
        lax.fori_loop(0, nsub, substep, 0)

    @pl.when(s == last)
    def _():
        wait(0)

    for lh in range(nlh):
        idv_scr[lh * LANES:(lh + 1) * LANES, :] = eall_scr[lh].T
    publish = pltpu.make_async_copy(idv_scr, ids_smem.at[rslot], idsem)
    publish.start()
    publish.wait()


def _peer(h1, wq, keys, uv_tab, g2, b2, alpha):
    t, d = h1.shape
    tg = 256
    tt = 8
    ngrp = t // tg
    nq = wq.shape[1]
    kk = PEER_TOPK
    hk = PEER_HEADS * kk
    nlh = tg // LANES
    nslab, sub = uv_tab.shape[1], uv_tab.shape[2]
    flat, cmask = _candidate_tables()
    const = lambda shape: pl.BlockSpec(shape, lambda i: (0,) * len(shape))
    return pl.pallas_call(
        functools.partial(_peer_kernel, tt=tt, alpha=alpha),
        grid=(ngrp + 2,),
        in_specs=[pl.BlockSpec((tg, d), lambda i: (jnp.maximum(i - 2, 0), 0)),
                  pl.BlockSpec((tg, d), lambda i: (jnp.minimum(i, ngrp - 1), 0)),
                  const((d, nq)), const(keys.shape), const(flat.shape), const(cmask.shape),
                  pl.BlockSpec(memory_space=pl.ANY),
                  const((1, d)), const((1, d))],
        out_specs=pl.BlockSpec((tg, d), lambda i: (jnp.maximum(i - 2, 0), 0)),
        out_shape=jax.ShapeDtypeStruct((t, d), F32),
        scratch_shapes=[pltpu.VMEM((2 * PEER_HEADS, nlh, LANES, LANES), BF16),
                        pltpu.VMEM((2 * kk, LANES), F32), pltpu.VMEM((2 * kk, LANES), jnp.int32),
                        pltpu.VMEM((kk, LANES), F32), pltpu.VMEM((kk, LANES), jnp.int32),
                        pltpu.VMEM((nlh, hk, LANES), jnp.int32),
                        pltpu.VMEM((tg, hk), jnp.int32),
                        pltpu.SMEM((3, tg, hk), jnp.int32),
                        pltpu.VMEM((3, nlh, hk, LANES), F32),
                        pltpu.VMEM((nslab, tt * hk * sub, LANES), F32),
                        pltpu.VMEM((nslab, tt * hk * sub, LANES), F32),
                        pltpu.SemaphoreType.DMA((2,)), pltpu.SemaphoreType.DMA,
                        pltpu.VMEM((2 * tt, d), F32)],
        compiler_params=pltpu.CompilerParams(
            dimension_semantics=("arbitrary",), vmem_limit_bytes=VMEM_LIMIT),
        name="peer",
    )(h1, h1, wq, keys, flat, cmask, uv_tab, g2, b2)


def _layer(h, w_in, conv_w, a_log, dt_bias, norm_w, f_bias, w_out_gdn, w_out_fox, w_o, ln1_g, ln1_b,
           peer_wq, peer_keys, peer_u, peer_v, ln2_g, ln2_b, alpha):
    b, s, d = h.shape
    t = b * s
    qk = GDN_HEADS * GDN_DK
    fw = FX_HEADS * FX_DH
    o_gz = 4 * qk
    o_ga = o_gz
    o_fq = o_ga + 2 * GDN_HEADS
    o_ff = o_fq + 3 * fw
    o_gate = o_ff + FX_HEADS
    w_big = jnp.concatenate([w_in[:, o_gate:], w_in[:, :o_gz]], axis=1).astype(BF16)
    pad = LANES - FX_DH
    w_fox = jnp.pad(w_in[:, o_fq:o_ff].reshape(d, 3 * FX_HEADS, FX_DH), ((0, 0), (0, 0), (0, pad)))
    w_fox = w_fox.reshape(d, 3 * FX_HEADS * LANES).astype(BF16)
    w_out_fox_p = jnp.pad(w_out_fox.reshape(FX_HEADS, FX_DH, d), ((0, 0), (0, pad), (0, 0)))
    w_out_fox_p = w_out_fox_p.reshape(FX_HEADS * LANES, d).astype(BF16)
    n_small = 2 * GDN_HEADS + FX_HEADS
    w_small = jnp.concatenate([w_in[:, o_ga:o_fq], w_in[:, o_ff:o_gate],
                               jnp.zeros((d, LANES - n_small), F32)], axis=1)
    params = jnp.zeros((8, LANES), F32)
    params = params.at[0, :GDN_HEADS].set(a_log).at[1, :GDN_HEADS].set(dt_bias)
    params = params.at[2, 2 * GDN_HEADS:n_small].set(f_bias)

    x2 = h.reshape(t, d)
    proj = _in_proj(x2, w_big, F32, "in_proj")
    pf = _in_proj(x2, w_fox, BF16, "in_proj_fox")
    gexp, bexp, c, ct = _prep(h, w_small, params)
    proj3 = proj.reshape(b, s, proj.shape[1])
    gdn0 = 2 * d // LANES
    oa = _gdn(proj3, gexp, bexp, conv_w, norm_w.reshape(1, LANES),
              (gdn0, gdn0 + GDN_HEADS, gdn0 + 2 * GDN_HEADS, gdn0 + 3 * GDN_HEADS))
    ob = _fox(pf.reshape(b, s, pf.shape[1]), c, ct)
    h1 = _mix(oa.reshape(t, qk), ob.reshape(t, FX_HEADS * LANES), proj, x2,
              w_out_gdn.astype(BF16), w_out_fox_p, w_o.astype(BF16),
              ln1_g.reshape(1, d), ln1_b.reshape(1, d), alpha)
    keys = peer_keys.reshape(2 * PEER_HEADS, PEER_NKEYS, peer_keys.shape[-1]).astype(BF16)
    ne = peer_u.shape[0]
    half = PEER_SUB // 2
    nslab = d // (half * LANES)
    uv_tab = jnp.concatenate([peer_u.reshape(ne, half, nslab, LANES),
                              peer_v.reshape(ne, half, nslab, LANES)], axis=1).transpose(0, 2, 1, 3)
    wq = peer_wq.astype(BF16)
    g2, b2 = ln2_g.reshape(1, d), ln2_b.reshape(1, d)
    t_sc = (t * PEER_SC_SHARE_PCT // 100) // 256 * 256
    t_tc = t - t_sc
    if t_sc:
        h_sc = h1[t_tc:]
        ids_sc, gate_sc = _route(h_sc, wq, keys)
        t_a = (t_tc * 55 // 100) // 256 * 256
        pre = _sc_pre(ids_sc, h_sc, peer_u)
        out_a = _peer(h1[:t_a], wq, keys, uv_tab, g2, b2, alpha)
        y_sc = _sc_out(ids_sc, _act(pre, gate_sc, out_a), peer_v)
        out_b = _peer(h1[t_a:t_tc], wq, keys, uv_tab, g2, b2, alpha)
        out = jnp.concatenate([out_a, out_b, _ln2(h_sc, y_sc, g2, b2, alpha)], axis=0)
    else:
        out = _peer(h1, wq, keys, uv_tab, g2, b2, alpha)
    return out.reshape(b, s, d)


def kernel(x, w_in, gdn_conv_w, gdn_a_log, gdn_dt_bias, gdn_norm_w, fox_f_bias, w_out_gdn, w_out_fox, w_o,
           ln1_g, ln1_b, peer_wq, peer_keys, peer_u, peer_v, ln2_g, ln2_b):
    depth = w_in.shape[0]
    alpha = (2.0 * depth) ** 0.25
    h = x
    for l in range(depth):
        h = _layer(h, w_in[l], gdn_conv_w[l], gdn_a_log[l], gdn_dt_bias[l], gdn_norm_w[l], fox_f_bias[l],
                   w_out_gdn[l], w_out_fox[l], w_o[l], ln1_g[l], ln1_b[l], peer_wq[l], peer_keys[l],
                   peer_u[l], peer_v[l], ln2_g[l], ln2_b[l], alpha)
    return h
```

```python
import functools

import jax
import jax.numpy as jnp
from jax import lax
from jax.experimental import pallas as pl
from jax.experimental.pallas import tpu as pltpu
from jax.experimental.pallas import tpu_sc as plsc

F32 = jnp.float32
BF16 = jnp.bfloat16
HI = lax.Precision.HIGHEST

LANES = 128
CHUNK = 64
GDN_HEADS = 4
GDN_DK = 128
FX_HEADS = 8
FX_DH = 64
PEER_HEADS = 8
PEER_NKEYS = 128
PEER_TOPK = 16
PEER_SUB = 4
LN_EPS = 1e-5
VMEM_LIMIT = 48 * 1024 * 1024


def _dot(a, b, prec=None):
    return jnp.dot(a, b, preferred_element_type=F32, precision=prec)


def _dot_nt(a, b, prec=None):
    return lax.dot_general(a, b, (((1,), (1,)), ((), ())), preferred_element_type=F32, precision=prec)


def _dot_tn(a, b, prec=None):
    return lax.dot_general(a, b, (((0,), (0,)), ((), ())), preferred_element_type=F32, precision=prec)


def _softplus(x):
    return jnp.maximum(x, 0.0) + jnp.log1p(jnp.exp(-jnp.abs(x)))


def _layernorm(z, g, b):
    mu = jnp.mean(z, axis=-1, keepdims=True)
    zc = z - mu
    var = jnp.mean(zc * zc, axis=-1, keepdims=True)
    return zc * lax.rsqrt(var + LN_EPS) * g + b


def _mm_kernel(x_ref, w_ref, o_ref):
    o_ref[...] = _dot(x_ref[...].astype(BF16), w_ref[...]).astype(o_ref.dtype)


def _in_proj(x2, w_big, out_dtype, name):
    t, d = x2.shape
    n = w_big.shape[1]
    tm = min(1024, t)
    tn = 512
    return pl.pallas_call(
        _mm_kernel,
        grid=(t // tm, n // tn),
        in_specs=[pl.BlockSpec((tm, d), lambda i, j: (i, 0)),
                  pl.BlockSpec((d, tn), lambda i, j: (0, j))],
        out_specs=pl.BlockSpec((tm, tn), lambda i, j: (i, j)),
        out_shape=jax.ShapeDtypeStruct((t, n), out_dtype),
        compiler_params=pltpu.CompilerParams(
            dimension_semantics=("parallel", "parallel"), vmem_limit_bytes=VMEM_LIMIT),
        name=name,
    )(x2, w_big)


def _prep_kernel(x_ref, w_ref, par_ref, gexp_ref, bexp_ref, c_ref, ct_ref, carry_ref):
    ts = x_ref.shape[1]

    @pl.when(pl.program_id(1) == 0)
    def _():
        carry_ref[...] = jnp.zeros_like(carry_ref)

    small = _dot(x_ref[0], w_ref[...], HI)
    a_log = par_ref[0:1, :]
    dt_bias = par_ref[1:2, :]
    f_bias = par_ref[2:3, :]
    g = -jnp.exp(a_log) * _softplus(small + dt_bias)
    beta = jax.nn.sigmoid(small)
    lane = lax.broadcasted_iota(jnp.int32, (ts, LANES), 1)
    log_f = jnp.where((lane >= 8) & (lane < 16), -_softplus(-(small + f_bias)), 0.0)
    row = lax.broadcasted_iota(jnp.int32, (ts, ts), 0)
    col = lax.broadcasted_iota(jnp.int32, (ts, ts), 1)
    tril = (row >= col).astype(F32)
    c = _dot(tril, log_f, HI) + carry_ref[...]
    carry_ref[...] = c[ts - 1:ts, :]
    c_ref[0] = c
    ct_ref[0] = c.T[8:16, :]
    gexp_ref[0] = jnp.concatenate(
        [jnp.broadcast_to(g[:, h:h + 1], (ts, LANES)) for h in range(GDN_HEADS)], axis=1)
    bexp_ref[0] = jnp.concatenate(
        [jnp.broadcast_to(beta[:, GDN_HEADS + h:GDN_HEADS + h + 1], (ts, LANES)) for h in range(GDN_HEADS)], axis=1)


def _prep(x, w_small, params):
    b, s, d = x.shape
    ts = min(512, s)
    hw = GDN_HEADS * LANES
    return pl.pallas_call(
        _prep_kernel,
        grid=(b, s // ts),
        in_specs=[pl.BlockSpec((1, ts, d), lambda i, j: (i, j, 0)),
                  pl.BlockSpec((d, LANES), lambda i, j: (0, 0)),
                  pl.BlockSpec((8, LANES), lambda i, j: (0, 0))],
        out_specs=[pl.BlockSpec((1, ts, hw), lambda i, j: (i, j, 0)),
                   pl.BlockSpec((1, ts, hw), lambda i, j: (i, j, 0)),
                   pl.BlockSpec((1, ts, LANES), lambda i, j: (i, j, 0)),
                   pl.BlockSpec((1, 8, ts), lambda i, j: (i, 0, j))],
        out_shape=[jax.ShapeDtypeStruct((b, s, hw), F32),
                   jax.ShapeDtypeStruct((b, s, hw), F32),
                   jax.ShapeDtypeStruct((b, s, LANES), F32),
                   jax.ShapeDtypeStruct((b, 8, s), F32)],
        scratch_shapes=[pltpu.VMEM((1, LANES), F32)],
        compiler_params=pltpu.CompilerParams(
            dimension_semantics=("parallel", "arbitrary"), vmem_limit_bytes=VMEM_LIMIT),
        name="prep",
    )(x, w_small, params)


def _each(fn, *lists):
    return [fn(*args) for args in zip(*lists)]


def _unit_lower_inverse(ms, masks):
    eye, blk16, blk32 = masks
    hi = lambda a, b: _dot(a, b, HI)
    n1 = _each(lambda m: -jnp.where(blk16, m, 0.0), ms)
    l1 = _each(lambda m: jnp.where(blk32 & jnp.logical_not(blk16), m, 0.0), ms)
    l2 = _each(lambda m: jnp.where(blk32, 0.0, m), ms)
    n2 = _each(hi, n1, n1)
    p = _each(lambda a, b: hi(eye + a, eye + b), n1, n2)
    n4 = _each(hi, n2, n2)
    p = _each(lambda a, b: hi(a, eye + b), p, n4)
    n8 = _each(hi, n4, n4)
    d_inv = _each(lambda a, b: hi(a, eye + b), p, n8)
    dl = _each(hi, d_inv, l1)
    a32 = _each(lambda a, b: a - hi(b, a), d_inv, dl)
    al = _each(hi, a32, l2)
    return _each(lambda a, b: a - hi(b, a), a32, al)


def _gdn_kernel(q_ref, k_ref, v_ref, z_ref, g_ref, b_ref, cwq_ref, cwk_ref, cwv_ref, nw_ref,
                o_ref, qn, kn, vn, st, sol_s, qk_s, qg_s, kd_s, gl_s):
    s = q_ref.shape[1]
    c = CHUNK
    nh = q_ref.shape[2] // LANES
    row = lax.broadcasted_iota(jnp.int32, (s, LANES), 0)

    def conv_silu(x, w):
        y = x * w[3:4, :]
        for sh in (1, 2, 3):
            xs = jnp.where(row >= sh, pltpu.roll(x, sh, axis=0), 0.0)
            y = y + xs * w[3 - sh:4 - sh, :]
        return y * jax.nn.sigmoid(y)

    def l2norm(x):
        return x * lax.rsqrt(jnp.sum(x * x, axis=-1, keepdims=True) + 1e-6)

    for j in range(nh):
        hs = slice(j * LANES, (j + 1) * LANES)
        qn[:, hs] = l2norm(conv_silu(q_ref[0, :, hs], cwq_ref[:, hs])) * (GDN_DK ** -0.5)
        kn[:, hs] = l2norm(conv_silu(k_ref[0, :, hs], cwk_ref[:, hs]))
        vn[:, hs] = conv_silu(v_ref[0, :, hs], cwv_ref[:, hs])
    st[...] = jnp.zeros_like(st)

    ri = lax.broadcasted_iota(jnp.int32, (c, c), 0)
    ci = lax.broadcasted_iota(jnp.int32, (c, c), 1)
    tril = ri >= ci
    strict = ri > ci
    t_inc = tril.astype(F32)
    eye = (ri == ci).astype(F32)
    blk16 = (ri >> 4) == (ci >> 4)
    blk32 = (ri >> 5) == (ci >> 5)
    l2 = lax.broadcasted_iota(jnp.int32, (c, 2 * LANES), 0)
    j2 = lax.broadcasted_iota(jnp.int32, (c, 2 * LANES), 1)
    ux = jnp.where((j2 >= c) | (l2 > j2), 1.0, 0.0).astype(F32)
    nw = nw_ref[...]

    heads = list(range(nh))
    lanes_of = [slice(j * LANES, (j + 1) * LANES) for j in heads]

    def local_stage(ns):
        pairs = [(i, j) for i in range(len(ns)) for j in heads]
        rows = [pl.ds(pl.multiple_of(n * c, c), c) for n in ns]
        q = [qn[rows[i], lanes_of[j]] for i, j in pairs]
        k = [kn[rows[i], lanes_of[j]] for i, j in pairs]
        v = [vn[rows[i], lanes_of[j]] for i, j in pairs]
        gb = [g_ref[0, rows[i], lanes_of[j]] for i, j in pairs]
        bb = [b_ref[0, rows[i], lanes_of[j]] for i, j in pairs]
        d = _each(lambda g: _dot(t_inc, jnp.concatenate([g, g], axis=1) * ux, HI), gb)
        kb = _each(lambda a, b: a * b, k, bb)
        kk = _each(lambda a, b: _dot_nt(a, b, HI), kb, k)
        qk = _each(lambda a, b: _dot_nt(a.astype(BF16), b.astype(BF16)), q, k)
        gc = [x[:, LANES:] for x in d]
        decay = [jnp.where(tril, jnp.exp(x[:, :c]), 0.0) for x in d]
        m = _each(lambda a, b: jnp.where(strict, a * b, 0.0), kk, decay)
        a_inv = _unit_lower_inverse(m, (eye, blk16, blk32))
        egc = _each(jnp.exp, gc)
        rhs = _each(lambda vv, b, kbb, e: jnp.concatenate([vv * b, kbb * e], axis=1), v, bb, kb, egc)
        sol = _each(lambda a, r: _dot(a, r, HI), a_inv, rhs)
        gl = [x[c - 1:c, :] for x in gc]
        for p, (i, j) in enumerate(pairs):
            n = ns[i]
            qk_s[n, j] = (qk[p] * decay[p]).astype(BF16)
            qg_s[n, j] = (q[p] * egc[p]).astype(BF16)
            kd_s[n, j] = (k[p] * jnp.exp(gl[p] - gc[p])).astype(BF16)
            gl_s[n, j] = gl[p]
            sol_s[n, j] = sol[p]

    def state_stage(n):
        rows = pl.ds(pl.multiple_of(n * c, c), c)
        state = [st[j] for j in heads]
        state_b = [x.astype(BF16) for x in state]
        v_new = [sol_s[n, j, :, :LANES] - _dot(sol_s[n, j, :, LANES:].astype(BF16), state_b[j])
                 for j in heads]
        v_new_b = [x.astype(BF16) for x in v_new]
        o = [_dot(qg_s[n, j], state_b[j]) + _dot(qk_s[n, j], v_new_b[j]) for j in heads]
        new_state = [state[j] * jnp.exp(gl_s[n, j]) + _dot_tn(kd_s[n, j], v_new_b[j]) for j in heads]
        for j in heads:
            st[j] = new_state[j]
            on = o[j] * lax.rsqrt(jnp.mean(o[j] * o[j], axis=-1, keepdims=True) + 1e-6) * nw
            z = z_ref[0, rows, lanes_of[j]]
            o_ref[0, rows, lanes_of[j]] = on * (z * jax.nn.sigmoid(z))

    nchunks = s // c
    group = 4

    def local_group(gidx, carry):
        local_stage([gidx * group + i for i in range(group)])
        return carry

    def state_chunk(n, carry):
        state_stage(n)
        return carry

    lax.fori_loop(0, nchunks // group, local_group, 0)
    lax.fori_loop(0, nchunks, state_chunk, 0)


def _gdn(proj3, gexp, bexp, conv_w, norm_w, cols):
    b, s, _ = proj3.shape
    nh = 2
    nc = s // CHUNK
    wd = nh * LANES
    cq, ck, cv, cz = (c0 // nh for c0 in cols)

    def blk(c0):
        return pl.BlockSpec((1, s, wd), lambda i, h: (i, 0, c0 + h))

    def cw(c0):
        return pl.BlockSpec((conv_w.shape[0], wd), lambda i, h: (0, c0 + h))

    head = pl.BlockSpec((1, s, wd), lambda i, h: (i, 0, h))
    return pl.pallas_call(
        _gdn_kernel,
        grid=(b, GDN_HEADS // nh),
        in_specs=[blk(cq), blk(ck), blk(cv), blk(cz), head, head,
                  cw(0), cw(GDN_HEADS // nh), cw(2 * GDN_HEADS // nh),
                  pl.BlockSpec((1, LANES), lambda i, h: (0, 0))],
        out_specs=head,
        out_shape=jax.ShapeDtypeStruct((b, s, GDN_HEADS * LANES), F32),
        scratch_shapes=[pltpu.VMEM((s, wd), F32), pltpu.VMEM((s, wd), F32),
                        pltpu.VMEM((s, wd), F32), pltpu.VMEM((nh, GDN_DK, LANES), F32),
                        pltpu.VMEM((nc, nh, CHUNK, 2 * LANES), F32), pltpu.VMEM((nc, nh, CHUNK, CHUNK), BF16),
                        pltpu.VMEM((nc, nh, CHUNK, LANES), BF16), pltpu.VMEM((nc, nh, CHUNK, LANES), BF16),
                        pltpu.VMEM((nc, nh, 1, LANES), F32)],
        compiler_params=pltpu.CompilerParams(
            dimension_semantics=("parallel", "parallel"), vmem_limit_bytes=VMEM_LIMIT),
        name="gdn",
    )(proj3, proj3, proj3, proj3, gexp, bexp, conv_w, conv_w, conv_w, norm_w)


def _fox_kernel(q_ref, k_ref, v_ref, c_ref, ct_ref, o_ref, *, tk):
    tq = q_ref.shape[1]
    nj = q_ref.shape[2] // LANES
    g = pl.program_id(1)
    qi = pl.program_id(2)
    q = q_ref[0]
    cblk = c_ref[0]
    lane = lax.broadcasted_iota(jnp.int32, (tq, LANES), 1)
    qpos = qi * tq + lax.broadcasted_iota(jnp.int32, (tq, tk), 0)
    kofs = lax.broadcasted_iota(jnp.int32, (tq, tk), 1)
    heads = list(range(nj))
    lanes_of = [slice(j * LANES, (j + 1) * LANES) for j in heads]
    ccol = [jnp.sum(jnp.where(lane == 8 + g * nj + j, cblk, 0.0), axis=-1, keepdims=True) for j in heads]
    qh = [q[:, hs] * jnp.asarray(FX_DH ** -0.5, BF16) for hs in lanes_of]

    def body(kv, carry):
        k0 = pl.multiple_of(kv * tk, tk)
        causal = qpos >= k0 + kofs
        kblk = k_ref[0, pl.ds(k0, tk), :]
        vblk = v_ref[0, pl.ds(k0, tk), :]
        sc = [_dot_nt(qh[j], kblk[:, lanes_of[j]]) for j in heads]
        crow = [ct_ref[0, pl.ds(g * nj + j, 1), pl.ds(k0, tk)] for j in heads]
        sc = [jnp.where(causal, sc[j] + ccol[j] - crow[j], -1e30) for j in heads]
        m_new = [jnp.maximum(carry[j][0], jnp.max(sc[j], axis=-1, keepdims=True)) for j in heads]
        a = [jnp.exp(carry[j][0] - m_new[j]) for j in heads]
        p = [jnp.exp(sc[j] - m_new[j]) for j in heads]
        l = [a[j] * carry[j][1] + jnp.sum(p[j], axis=-1, keepdims=True) for j in heads]
        acc = [a[j] * carry[j][2] + _dot(p[j].astype(BF16), vblk[:, lanes_of[j]]) for j in heads]
        return tuple((m_new[j], l[j], acc[j]) for j in heads)

    init = tuple((jnp.full((tq, 1), -1e30, F32), jnp.zeros((tq, 1), F32), jnp.zeros((tq, LANES), F32))
                 for _ in heads)
    nkv = (qi * tq + tq - 1) // tk + 1
    res = lax.fori_loop(0, nkv, body, init)
    o_ref[0] = jnp.concatenate([acc / l for _, l, acc in res], axis=1)


def _fox(pf3, c, ct):
    b, s, _ = pf3.shape
    wd = 4 * LANES
    tq = min(128, s)
    tk = min(256, s)
    ngrp = FX_HEADS * LANES // wd
    return pl.pallas_call(
        functools.partial(_fox_kernel, tk=tk),
        grid=(b, ngrp, s // tq),
        in_specs=[pl.BlockSpec((1, tq, wd), lambda i, h, t: (i, t, h)),
                  pl.BlockSpec((1, s, wd), lambda i, h, t: (i, 0, ngrp + h)),
                  pl.BlockSpec((1, s, wd), lambda i, h, t: (i, 0, 2 * ngrp + h)),
                  pl.BlockSpec((1, tq, LANES), lambda i, h, t: (i, t, 0)),
                  pl.BlockSpec((1, 8, s), lambda i, h, t: (i, 0, 0))],
        out_specs=pl.BlockSpec((1, tq, wd), lambda i, h, t: (i, t, h)),
        out_shape=jax.ShapeDtypeStruct((b, s, ngrp * wd), F32),
        compiler_params=pltpu.CompilerParams(
            dimension_semantics=("parallel", "parallel", "parallel"), vmem_limit_bytes=VMEM_LIMIT),
        name="fox",
    )(pf3, pf3, pf3, c, ct)


def _mix_kernel(oa_ref, ob_ref, ga_ref, gb_ref, x_ref, wa_ref, wb_ref, wo_ref, g1_ref, b1_ref, o_ref, *, alpha):
    ya = _dot(oa_ref[...].astype(BF16), wa_ref[...])
    yb = _dot(ob_ref[...].astype(BF16), wb_ref[...])
    mix = jax.nn.sigmoid(ga_ref[...]) * ya + jax.nn.sigmoid(gb_ref[...]) * yb
    z = alpha * x_ref[...] + _dot(mix.astype(BF16), wo_ref[...])
    o_ref[...] = _layernorm(z, g1_ref[...], b1_ref[...])


def _mix(oa, ob, proj, x2, wa, wb, wo, g1, b1, alpha):
    t, d = x2.shape
    tm = min(512, t)
    w, w2 = oa.shape[1], ob.shape[1]
    full = lambda r, c: pl.BlockSpec((r, c), lambda i: (0, 0))
    return pl.pallas_call(
        functools.partial(_mix_kernel, alpha=alpha),
        grid=(t // tm,),
        in_specs=[pl.BlockSpec((tm, w), lambda i: (i, 0)),
                  pl.BlockSpec((tm, w2), lambda i: (i, 0)),
                  pl.BlockSpec((tm, d), lambda i: (i, 0)),
                  pl.BlockSpec((tm, d), lambda i: (i, 1)),
                  pl.BlockSpec((tm, d), lambda i: (i, 0)),
                  full(w, d), full(w2, d), full(d, d), full(1, d), full(1, d)],
        out_specs=pl.BlockSpec((tm, d), lambda i: (i, 0)),
        out_shape=jax.ShapeDtypeStruct((t, d), F32),
        compiler_params=pltpu.CompilerParams(
            dimension_semantics=("parallel",), vmem_limit_bytes=VMEM_LIMIT),
        name="mix",
    )(oa, ob, proj, proj, x2, wa, wb, wo, g1, b1)


def _route_head(keys_ref, q_scr, hd, lh, scr, flat, cmask):
    s_scr, i_scr, best_scr, eh_scr = scr
    nk, kk = PEER_NKEYS, PEER_TOPK
    iota_k = lax.broadcasted_iota(jnp.int32, (nk, LANES), 0)
    neg = jnp.float32(-jnp.inf)
    for p in range(2):
        vals = _dot_nt(keys_ref[hd * 2 + p], q_scr[hd * 2 + p, lh])
        for r in range(kk):
            m = jnp.max(vals, axis=0, keepdims=True)
            am = jnp.min(jnp.where(vals == m, iota_k, nk), axis=0, keepdims=True)
            s_scr[p * kk + r:p * kk + r + 1, :] = m
            i_scr[p * kk + r:p * kk + r + 1, :] = am
            vals = jnp.where(iota_k == am, neg, vals)
    s1 = s_scr[kk:kk + 8, :]
    i1 = i_scr[kk:kk + 8, :]
    cand = [s_scr[0:1, :] + s_scr[kk:2 * kk, :]]
    cidx = [i_scr[0:1, :] * nk + i_scr[kk:2 * kk, :]]
    for a in range(1, 8):
        cand.append(s_scr[a:a + 1, :] + s1)
        cidx.append(i_scr[a:a + 1, :] * nk + i1)
    cand.append(s_scr[8:kk, :] + s_scr[kk:kk + 1, :])
    cidx.append(i_scr[8:kk, :] * nk + i_scr[kk:kk + 1, :])
    vals = jnp.concatenate(cand, axis=0) + cmask
    cidx = jnp.concatenate(cidx, axis=0)
    for r in range(kk):
        m = jnp.max(vals, axis=0, keepdims=True)
        am = jnp.min(jnp.where(vals == m, flat, 2 * kk * kk), axis=0, keepdims=True)
        sel = flat == am
        best_scr[r:r + 1, :] = m
        eh_scr[r:r + 1, :] = jnp.max(jnp.where(sel, cidx, -1), axis=0, keepdims=True)
        vals = jnp.where(sel, neg, vals)
    bs = best_scr[...]
    ex = jnp.exp(bs - bs[0:1, :])
    return ex / jnp.sum(ex, axis=0, keepdims=True), eh_scr[...]


def _candidate_tables():
    kk = PEER_TOPK
    pairs = [(0, bb) for bb in range(kk)]
    for a in range(1, 8):
        pairs += [(a, bb) for bb in range(8)]
    pairs += [(a, 0) for a in range(8, kk)]
    real = [(a + 1) * (bb + 1) <= kk for a, bb in pairs]
    flat = [a * kk + bb if ok else kk * kk + r for r, ((a, bb), ok) in enumerate(zip(pairs, real))]
    flat = jnp.broadcast_to(jnp.asarray(flat, jnp.int32)[:, None], (len(pairs), LANES))
    cmask = jnp.broadcast_to(jnp.asarray([0.0 if ok else -jnp.inf for ok in real], F32)[:, None],
                             (len(pairs), LANES))
    return flat, cmask


def _route_kernel(h_ref, wq_ref, keys_ref, flat_ref, cmask_ref, e_ref, g_ref,
                  q_scr, s_scr, i_scr, best_scr, eh_scr, eall_scr, gall_scr):
    tg = h_ref.shape[0]
    kk = PEER_TOPK
    nlh = tg // LANES
    q = _dot(h_ref[...].astype(BF16), wq_ref[...])
    for j in range(2 * PEER_HEADS):
        for lh in range(nlh):
            q_scr[j, lh] = q[lh * LANES:(lh + 1) * LANES, j * LANES:(j + 1) * LANES].astype(BF16)
    flat = flat_ref[...]
    cmask = cmask_ref[...]

    def piece(p, carry):
        hd = p // nlh
        lh = p % nlh
        gates, ids = _route_head(keys_ref, q_scr, hd, lh, (s_scr, i_scr, best_scr, eh_scr), flat, cmask)
        r0 = pl.multiple_of(hd * kk, kk)
        gall_scr[lh, pl.ds(r0, kk), :] = gates
        eall_scr[lh, pl.ds(r0, kk), :] = ids
        return carry

    lax.fori_loop(0, PEER_HEADS * nlh, piece, 0)
    for lh in range(nlh):
        e_ref[lh * LANES:(lh + 1) * LANES, :] = eall_scr[lh].T
        g_ref[lh * LANES:(lh + 1) * LANES, :] = gall_scr[lh].T


def _route(h, wq, keys):
    t, d = h.shape
    tg = 256
    nq = wq.shape[1]
    kk = PEER_TOPK
    hk = PEER_HEADS * kk
    nlh = tg // LANES
    flat, cmask = _candidate_tables()
    const = lambda shape: pl.BlockSpec(shape, lambda i: (0,) * len(shape))
    return pl.pallas_call(
        _route_kernel,
        grid=(t // tg,),
        in_specs=[pl.BlockSpec((tg, d), lambda i: (i, 0)),
                  const((d, nq)), const(keys.shape), const(flat.shape), const(cmask.shape)],
        out_specs=[pl.BlockSpec((tg, hk), lambda i: (i, 0)), pl.BlockSpec((tg, hk), lambda i: (i, 0))],
        out_shape=[jax.ShapeDtypeStruct((t, hk), jnp.int32), jax.ShapeDtypeStruct((t, hk), F32)],
        scratch_shapes=[pltpu.VMEM((2 * PEER_HEADS, nlh, LANES, LANES), BF16),
                        pltpu.VMEM((2 * kk, LANES), F32), pltpu.VMEM((2 * kk, LANES), jnp.int32),
                        pltpu.VMEM((kk, LANES), F32), pltpu.VMEM((kk, LANES), jnp.int32),
                        pltpu.VMEM((nlh, hk, LANES), jnp.int32), pltpu.VMEM((nlh, hk, LANES), F32)],
        compiler_params=pltpu.CompilerParams(
            dimension_semantics=("parallel",), vmem_limit_bytes=VMEM_LIMIT),
        name="route",
    )(h, wq, keys, flat, cmask)


PEER_SC_SHARE_PCT = 40
SC_WORKERS = 32
SC_LANES = 16
SC_CHUNK = 32
SC_DBLK = 16
SC_EBLK = 4
SC_JUNROLL = 8


def _sc_mesh():
    return plsc.VectorSubcoreMesh(core_axis_name="c", subcore_axis_name="s")


def _sc_worker():
    return lax.axis_index("s") * 2 + lax.axis_index("c")


def _sc_pre(ids, h, u_tab):
    ts, hk = ids.shape
    d = h.shape[1]
    per_w = ts // SC_WORKERS
    nch = hk // SC_CHUNK

    def body(ids_hbm, h_hbm, u_hbm, out_hbm, idx_v, h_v, rows_a, rows_b, pre_v, sem_a, sem_b):
        base = _sc_worker() * per_w
        lane = lax.iota(jnp.int32, SC_LANES)
        bufs = (rows_a, rows_b)
        sems = (sem_a, sem_b)

        def gather(c):
            return pltpu.async_copy(u_hbm.at[idx_v.at[pl.ds(c * SC_CHUNK, SC_CHUNK)]], bufs[c % 2], sems[c % 2])

        def token(i, carry):
            t = base + i
            pltpu.sync_copy(ids_hbm.at[t], idx_v)
            pltpu.sync_copy(h_hbm.at[t], h_v)
            cps = {0: gather(0)}
            for c in range(nch):
                if c + 1 < nch:
                    cps[c + 1] = gather(c + 1)
                cps[c].wait()
                rows = bufs[c % 2]
                for g in range(SC_CHUNK // SC_LANES):
                    def experts(q, vec, g=g, rows=rows):
                        e0 = g * SC_LANES + q * SC_EBLK

                        def span(jb, accs, rows=rows, e0=e0):
                            accs = list(accs)
                            for jj in range(SC_JUNROLL):
                                sl = pl.ds((jb * SC_JUNROLL + jj) * SC_LANES, SC_LANES)
                                hv = h_v[sl]
                                for i in range(SC_EBLK):
                                    accs[i] = accs[i] + rows[e0 + i, sl] * hv
                            return tuple(accs)
                        accs = lax.fori_loop(0, d // (SC_LANES * SC_JUNROLL), span,
                                             tuple(jnp.zeros((SC_LANES,), F32) for _ in range(SC_EBLK)))
                        for i in range(SC_EBLK):
                            vec = jnp.where(lane == q * SC_EBLK + i, jnp.sum(accs[i]), vec)
                        return vec
                    vec = lax.fori_loop(0, SC_LANES // SC_EBLK, experts, jnp.zeros((SC_LANES,), F32))
                    pre_v[pl.ds(c * SC_CHUNK + g * SC_LANES, SC_LANES)] = vec
            pltpu.sync_copy(pre_v, out_hbm.at[t])
            return carry

        lax.fori_loop(0, per_w, token, 0)

    return pl.kernel(
        body, out_type=jax.ShapeDtypeStruct((ts, hk), F32), mesh=_sc_mesh(),
        scratch_types=[pltpu.VMEM((hk,), jnp.int32), pltpu.VMEM((d,), F32),
                       pltpu.VMEM((SC_CHUNK, d), F32), pltpu.VMEM((SC_CHUNK, d), F32),
                       pltpu.VMEM((hk,), F32), pltpu.SemaphoreType.DMA, pltpu.SemaphoreType.DMA],
        compiler_params=pltpu.CompilerParams(needs_layout_passes=False),
        name="sc_pre",
    )(ids, h, u_tab)


def _sc_out(ids, act, v_tab):
    ts, hk = ids.shape
    d = v_tab.shape[1]
    per_w = ts // SC_WORKERS
    nch = hk // SC_CHUNK

    def body(ids_hbm, act_hbm, v_hbm, out_hbm, idx_v, act_v, rows_a, rows_b, y_v, sem_a, sem_b):
        base = _sc_worker() * per_w
        lane = lax.iota(jnp.int32, SC_LANES)
        bufs = (rows_a, rows_b)
        sems = (sem_a, sem_b)

        def gather(c):
            return pltpu.async_copy(v_hbm.at[idx_v.at[pl.ds(c * SC_CHUNK, SC_CHUNK)]], bufs[c % 2], sems[c % 2])

        def token(i, carry):
            t = base + i
            pltpu.sync_copy(ids_hbm.at[t], idx_v)
            pltpu.sync_copy(act_hbm.at[t], act_v)
            for j in range(d // SC_LANES):
                y_v[pl.ds(j * SC_LANES, SC_LANES)] = jnp.zeros((SC_LANES,), F32)
            cps = {0: gather(0)}
            for c in range(nch):
                if c + 1 < nch:
                    cps[c + 1] = gather(c + 1)
                cps[c].wait()
                rows = bufs[c % 2]
                for db in range(d // (SC_DBLK * SC_LANES)):
                    def expert(r, accs, c=c, rows=rows, db=db):
                        a = plsc.load_gather(act_v, [jnp.zeros((SC_LANES,), jnp.int32) + (c * SC_CHUNK + r)])
                        return tuple(
                            accs[j] + a * rows[r, pl.ds((db * SC_DBLK + j) * SC_LANES, SC_LANES)]
                            for j in range(SC_DBLK))
                    accs = lax.fori_loop(0, SC_CHUNK, expert,
                                         tuple(jnp.zeros((SC_LANES,), F32) for _ in range(SC_DBLK)))
                    for j in range(SC_DBLK):
                        plsc.addupdate(y_v.at[pl.ds((db * SC_DBLK + j) * SC_LANES, SC_LANES)], accs[j])
            pltpu.sync_copy(y_v, out_hbm.at[t])
            return carry

        lax.fori_loop(0, per_w, token, 0)

    return pl.kernel(
        body, out_type=jax.ShapeDtypeStruct((ts, d), F32), mesh=_sc_mesh(),
        scratch_types=[pltpu.VMEM((hk,), jnp.int32), pltpu.VMEM((hk,), F32),
                       pltpu.VMEM((SC_CHUNK, d), F32), pltpu.VMEM((SC_CHUNK, d), F32),
                       pltpu.VMEM((d,), F32), pltpu.SemaphoreType.DMA, pltpu.SemaphoreType.DMA],
        compiler_params=pltpu.CompilerParams(needs_layout_passes=False),
        name="sc_out",
    )(ids, act, v_tab)


def _act_kernel(pre_ref, gate_ref, after_ref, o_ref):
    del after_ref
    pre = pre_ref[...]
    o_ref[...] = 0.5 * pre * (1.0 + lax.erf(pre * (2.0 ** -0.5))) * gate_ref[...]


def _act(pre, gate, after):
    t, hk = pre.shape
    tm = 256
    assert t % tm == 0
    spec = pl.BlockSpec((tm, hk), lambda i: (i, 0))
    return pl.pallas_call(
        _act_kernel, grid=(t // tm,),
        in_specs=[spec, spec, pl.BlockSpec((8, LANES), lambda i: (0, 0))], out_specs=spec,
        out_shape=jax.ShapeDtypeStruct((t, hk), F32),
        compiler_params=pltpu.CompilerParams(dimension_semantics=("parallel",)),
        name="peer_act",
    )(pre, gate, after)


def _ln2_kernel(h_ref, y_ref, g_ref, b_ref, o_ref, *, alpha):
    o_ref[...] = _layernorm(alpha * h_ref[...] + y_ref[...], g_ref[...], b_ref[...])


def _ln2(h, y, g2, b2, alpha):
    t, d = h.shape
    tm = 256
    assert t % tm == 0
    spec = pl.BlockSpec((tm, d), lambda i: (i, 0))
    vec = pl.BlockSpec((1, d), lambda i: (0, 0))
    return pl.pallas_call(
        functools.partial(_ln2_kernel, alpha=alpha), grid=(t // tm,),
        in_specs=[spec, spec, vec, vec], out_specs=spec,
        out_shape=jax.ShapeDtypeStruct((t, d), F32),
        compiler_params=pltpu.CompilerParams(dimension_semantics=("parallel",)),
        name="peer_ln2",
    )(h, y, g2, b2)


def _peer_kernel(h_cur_ref, h_nxt_ref, wq_ref, keys_ref, flat_ref, cmask_ref, uv_hbm, g2_ref, b2_ref, o_ref,
                 q_scr, s_scr, i_scr, best_scr, eh_scr, eall_scr, idv_scr, ids_smem, gate_scr,
                 buf_a, buf_b, sem, idsem, y_scr, *, tt, alpha):
    s = pl.program_id(0)
    tg, d = h_cur_ref.shape
    nk, kk = PEER_NKEYS, PEER_TOPK
    hk = PEER_HEADS * kk
    nlh = tg // LANES
    nsub = tg // (2 * tt)
    assert nsub == PEER_HEADS * nlh
    nslab, sub = uv_hbm.shape[1], uv_hbm.shape[2]
    half = sub // 2
    bufs = (buf_a, buf_b)
    last = pl.num_programs(0) - 1
    rslot = s % 3
    pslot = (s + 2) % 3
    eslot = (s + 1) % 3

    def issue(idslot, row0, slot):
        for t in range(tt):
            for k in range(hk):
                e = ids_smem[idslot, row0 + t, k]
                pltpu.make_async_copy(uv_hbm.at[e], bufs[slot].at[:, pl.ds((t * hk + k) * sub, sub), :],
                                      sem.at[slot]).start(priority=k % 2)

    def wait(slot):
        pltpu.make_async_copy(bufs[slot], bufs[slot], sem.at[slot]).wait()

    q = _dot(h_nxt_ref[...].astype(BF16), wq_ref[...])
    for j in range(2 * PEER_HEADS):
        for lh in range(nlh):
            q_scr[j, lh] = q[lh * LANES:(lh + 1) * LANES, j * LANES:(j + 1) * LANES].astype(BF16)
    flat = flat_ref[...]
    cmask = cmask_ref[...]

    def route_piece(piece):
        hd = piece // nlh
        lh = piece % nlh
        gates, ids = _route_head(keys_ref, q_scr, hd, lh, (s_scr, i_scr, best_scr, eh_scr), flat, cmask)
        r0 = pl.multiple_of(hd * kk, kk)
        gate_scr[rslot, lh, pl.ds(r0, kk), :] = gates
        eall_scr[lh, pl.ds(r0, kk), :] = ids

    lane = lax.broadcasted_iota(jnp.int32, (hk, LANES), 1)

    def compute(row0, slot):
        buf = bufs[slot]
        gt = gate_scr[eslot, row0 // LANES]
        lane0 = row0 % LANES
        for t in range(tt):
            hrow = h_cur_ref[pl.ds(row0 + t, 1), :]

            def rows(c, j):
                return buf[c, pl.ds(t * hk * sub + j, hk, stride=sub), :]

            part = None
            for j in range(half):
                for c in range(nslab):
                    seg = j * nslab + c
                    term = rows(c, j) * hrow[:, seg * LANES:(seg + 1) * LANES]
                    part = term if part is None else part + term
            pre = jnp.sum(part, axis=-1, keepdims=True)
            gate = jnp.sum(jnp.where(lane == lane0 + t, gt, 0.0), axis=-1, keepdims=True)
            act = 0.5 * pre * (1.0 + lax.erf(pre * (2.0 ** -0.5))) * gate
            yrow = slot * tt + t
            for j in range(half):
                for c in range(nslab):
                    seg = j * nslab + c
                    y_scr[yrow:yrow + 1, seg * LANES:(seg + 1) * LANES] = jnp.sum(
                        act * rows(c, half + j), axis=0, keepdims=True)

    def substep(j, carry):
        row0 = pl.multiple_of(j * 2 * tt, 2 * tt)
        issue(eslot, row0 + tt, 1)
        route_piece(j)
        wait(0)
        compute(row0, 0)
        wait(1)
        wrap = j == nsub - 1
        issue(jnp.where(wrap, pslot, eslot), jnp.where(wrap, 0, row0 + 2 * tt), 0)
        compute(row0 + tt, 1)
        z = alpha * h_cur_ref[pl.ds(row0, 2 * tt), :] + y_scr[...]
        o_ref[pl.ds(row0, 2 * tt), :] = _layernorm(z, g2_ref[...], b2_ref[...])
        return carry

    @pl.when(s < 2)
    def _():
        o_ref[...] = jnp.zeros_like(o_ref)

        def piece(p, carry):
            route_piece(p)
            return carry
        lax.fori_loop(0, nsub, piece, 0)

    @pl.when(s == 1)
    def _():
        issue(pslot, 0, 0)

    @pl.when(s >= 2)
    def _():
        lax.fori_loop(0, nsub, substep, 0)

    @pl.when(s == last)
    def _():
        wait(0)

    for lh in range(nlh):
        idv_scr[lh * LANES:(lh + 1) * LANES, :] = eall_scr[lh].T
    publish = pltpu.make_async_copy(idv_scr, ids_smem.at[rslot], idsem)
    publish.start()
    publish.wait()


def _peer(h1, wq, keys, uv_tab, g2, b2, alpha):
    t, d = h1.shape
    tg = 256
    tt = 8
    ngrp = t // tg
    nq = wq.shape[1]
    kk = PEER_TOPK
    hk = PEER_HEADS * kk
    nlh = tg // LANES
    nslab, sub = uv_tab.shape[1], uv_tab.shape[2]
    flat, cmask = _candidate_tables()
    const = lambda shape: pl.BlockSpec(shape, lambda i: (0,) * len(shape))
    return pl.pallas_call(
        functools.partial(_peer_kernel, tt=tt, alpha=alpha),
        grid=(ngrp + 2,),
        in_specs=[pl.BlockSpec((tg, d), lambda i: (jnp.maximum(i - 2, 0), 0)),
                  pl.BlockSpec((tg, d), lambda i: (jnp.minimum(i, ngrp - 1), 0)),
                  const((d, nq)), const(keys.shape), const(flat.shape), const(cmask.shape),
                  pl.BlockSpec(memory_space=pl.ANY),
                  const((1, d)), const((1, d))],
        out_specs=pl.BlockSpec((tg, d), lambda i: (jnp.maximum(i - 2, 0), 0)),
        out_shape=jax.ShapeDtypeStruct((t, d), F32),
        scratch_shapes=[pltpu.VMEM((2 * PEER_HEADS, nlh, LANES, LANES), BF16),
                        pltpu.VMEM((2 * kk, LANES), F32), pltpu.VMEM((2 * kk, LANES), jnp.int32),
                        pltpu.VMEM((kk, LANES), F32), pltpu.VMEM((kk, LANES), jnp.int32),
                        pltpu.VMEM((nlh, hk, LANES), jnp.int32),
                        pltpu.VMEM((tg, hk), jnp.int32),
                        pltpu.SMEM((3, tg, hk), jnp.int32),
                        pltpu.VMEM((3, nlh, hk, LANES), F32),
                        pltpu.VMEM((nslab, tt * hk * sub, LANES), F32),
                        pltpu.VMEM((nslab, tt * hk * sub, LANES), F32),
                        pltpu.SemaphoreType.DMA((2,)), pltpu.SemaphoreType.DMA,
                        pltpu.VMEM((2 * tt, d), F32)],
        compiler_params=pltpu.CompilerParams(
            dimension_semantics=("arbitrary",), vmem_limit_bytes=VMEM_LIMIT),
        name="peer",
    )(h1, h1, wq, keys, flat, cmask, uv_tab, g2, b2)


def _layer(h, w_in, conv_w, a_log, dt_bias, norm_w, f_bias, w_out_gdn, w_out_fox, w_o, ln1_g, ln1_b,
           peer_wq, peer_keys, peer_u, peer_v, ln2_g, ln2_b, alpha):
    b, s, d = h.shape
    t = b * s
    qk = GDN_HEADS * GDN_DK
    fw = FX_HEADS * FX_DH
    o_gz = 4 * qk
    o_ga = o_gz
    o_fq = o_ga + 2 * GDN_HEADS
    o_ff = o_fq + 3 * fw
    o_gate = o_ff + FX_HEADS
    w_big = jnp.concatenate([w_in[:, o_gate:], w_in[:, :o_gz]], axis=1).astype(BF16)
    pad = LANES - FX_DH
    w_fox = jnp.pad(w_in[:, o_fq:o_ff].reshape(d, 3 * FX_HEADS, FX_DH), ((0, 0), (0, 0), (0, pad)))
    w_fox = w_fox.reshape(d, 3 * FX_HEADS * LANES).astype(BF16)
    w_out_fox_p = jnp.pad(w_out_fox.reshape(FX_HEADS, FX_DH, d), ((0, 0), (0, pad), (0, 0)))
    w_out_fox_p = w_out_fox_p.reshape(FX_HEADS * LANES, d).astype(BF16)
    n_small = 2 * GDN_HEADS + FX_HEADS
    w_small = jnp.concatenate([w_in[:, o_ga:o_fq], w_in[:, o_ff:o_gate],
                               jnp.zeros((d, LANES - n_small), F32)], axis=1)
    params = jnp.zeros((8, LANES), F32)
    params = params.at[0, :GDN_HEADS].set(a_log).at[1, :GDN_HEADS].set(dt_bias)
    params = params.at[2, 2 * GDN_HEADS:n_small].set(f_bias)

    x2 = h.reshape(t, d)
    proj = _in_proj(x2, w_big, F32, "in_proj")
    pf = _in_proj(x2, w_fox, BF16, "in_proj_fox")
    gexp, bexp, c, ct = _prep(h, w_small, params)
    proj3 = proj.reshape(b, s, proj.shape[1])
    gdn0 = 2 * d // LANES
    oa = _gdn(proj3, gexp, bexp, conv_w, norm_w.reshape(1, LANES),
              (gdn0, gdn0 + GDN_HEADS, gdn0 + 2 * GDN_HEADS, gdn0 + 3 * GDN_HEADS))
    ob = _fox(pf.reshape(b, s, pf.shape[1]), c, ct)
    h1 = _mix(oa.reshape(t, qk), ob.reshape(t, FX_HEADS * LANES), proj, x2,
              w_out_gdn.astype(BF16), w_out_fox_p, w_o.astype(BF16),
              ln1_g.reshape(1, d), ln1_b.reshape(1, d), alpha)
    keys = peer_keys.reshape(2 * PEER_HEADS, PEER_NKEYS, peer_keys.shape[-1]).astype(BF16)
    ne = peer_u.shape[0]
    half = PEER_SUB // 2
    nslab = d // (half * LANES)
    uv_tab = jnp.concatenate([peer_u.reshape(ne, half, nslab, LANES),
                              peer_v.reshape(ne, half, nslab, LANES)], axis=1).transpose(0, 2, 1, 3)
    wq = peer_wq.astype(BF16)
    g2, b2 = ln2_g.reshape(1, d), ln2_b.reshape(1, d)
    t_sc = (t * PEER_SC_SHARE_PCT // 100) // 256 * 256
    t_tc = t - t_sc
    if t_sc:
        h_sc = h1[t_tc:]
        ids_sc, gate_sc = _route(h_sc, wq, keys)
        t_a = (t_tc * 55 // 100) // 256 * 256
        pre = _sc_pre(ids_sc, h_sc, peer_u)
        out_a = _peer(h1[:t_a], wq, keys, uv_tab, g2, b2, alpha)
        y_sc = _sc_out(ids_sc, _act(pre, gate_sc, out_a), peer_v)
        out_b = _peer(h1[t_a:t_tc], wq, keys, uv_tab, g2, b2, alpha)
        out = jnp.concatenate([out_a, out_b, _ln2(h_sc, y_sc, g2, b2, alpha)], axis=0)
    else:
        out = _peer(h1, wq, keys, uv_tab, g2, b2, alpha)
    return out.reshape(b, s, d)


def kernel(x, w_in, gdn_conv_w, gdn_a_log, gdn_dt_bias, gdn_norm_w, fox_f_bias, w_out_gdn, w_out_fox, w_o,
           ln1_g, ln1_b, peer_wq, peer_keys, peer_u, peer_v, ln2_g, ln2_b):
    depth = w_in.shape[0]
    alpha = (2.0 * depth) ** 0.25
    h = x
    for l in range(depth):
        h = _layer(h, w_in[l], gdn_conv_w[l], gdn_a_log[l], gdn_dt_bias[l], gdn_norm_w[l], fox_f_bias[l],
                   w_out_gdn[l], w_out_fox[l], w_o[l], ln1_g[l], ln1_b[l], peer_wq[l], peer_keys[l],
                   peer_u[l], peer_v[l], ln2_g[l], ln2_b[l], alpha)
    return h
```

```python
import functools

import jax
import jax.numpy as jnp
from jax import lax
from jax.experimental import pallas as pl
from jax.experimental.pallas import tpu as pltpu
from jax.experimental.pallas import tpu_sc as plsc

F32 = jnp.float32
BF16 = jnp.bfloat16
HI = lax.Precision.HIGHEST

LANES = 128
CHUNK = 64
GDN_HEADS = 4
GDN_DK = 128
FX_HEADS = 8
FX_DH = 64
PEER_HEADS = 8
PEER_NKEYS = 128
PEER_TOPK = 16
PEER_SUB = 4
LN_EPS = 1e-5
VMEM_LIMIT = 48 * 1024 * 1024


def _dot(a, b, prec=None):
    return jnp.dot(a, b, preferred_element_type=F32, precision=prec)


def _dot_nt(a, b, prec=None):
    return lax.dot_general(a, b, (((1,), (1,)), ((), ())), preferred_element_type=F32, precision=prec)


def _dot_tn(a, b, prec=None):
    return lax.dot_general(a, b, (((0,), (0,)), ((), ())), preferred_element_type=F32, precision=prec)


def _softplus(x):
    return jnp.maximum(x, 0.0) + jnp.log1p(jnp.exp(-jnp.abs(x)))


def _layernorm(z, g, b):
    mu = jnp.mean(z, axis=-1, keepdims=True)
    zc = z - mu
    var = jnp.mean(zc * zc, axis=-1, keepdims=True)
    return zc * lax.rsqrt(var + LN_EPS) * g + b


def _mm_kernel(x_ref, w_ref, o_ref):
    o_ref[...] = _dot(x_ref[...].astype(BF16), w_ref[...]).astype(o_ref.dtype)


def _in_proj(x2, w_big, out_dtype, name):
    t, d = x2.shape
    n = w_big.shape[1]
    tm = min(1024, t)
    tn = 512
    return pl.pallas_call(
        _mm_kernel,
        grid=(t // tm, n // tn),
        in_specs=[pl.BlockSpec((tm, d), lambda i, j: (i, 0)),
                  pl.BlockSpec((d, tn), lambda i, j: (0, j))],
        out_specs=pl.BlockSpec((tm, tn), lambda i, j: (i, j)),
        out_shape=jax.ShapeDtypeStruct((t, n), out_dtype),
        compiler_params=pltpu.CompilerParams(
            dimension_semantics=("parallel", "parallel"), vmem_limit_bytes=VMEM_LIMIT),
        name=name,
    )(x2, w_big)


def _prep_kernel(x_ref, w_ref, par_ref, gexp_ref, bexp_ref, c_ref, ct_ref, carry_ref):
    ts = x_ref.shape[1]

    @pl.when(pl.program_id(1) == 0)
    def _():
        carry_ref[...] = jnp.zeros_like(carry_ref)

    small = _dot(x_ref[0], w_ref[...], HI)
    a_log = par_ref[0:1, :]
    dt_bias = par_ref[1:2, :]
    f_bias = par_ref[2:3, :]
    g = -jnp.exp(a_log) * _softplus(small + dt_bias)
    beta = jax.nn.sigmoid(small)
    lane = lax.broadcasted_iota(jnp.int32, (ts, LANES), 1)
    log_f = jnp.where((lane >= 8) & (lane < 16), -_softplus(-(small + f_bias)), 0.0)
    row = lax.broadcasted_iota(jnp.int32, (ts, ts), 0)
    col = lax.broadcasted_iota(jnp.int32, (ts, ts), 1)
    tril = (row >= col).astype(F32)
    c = _dot(tril, log_f, HI) + carry_ref[...]
    carry_ref[...] = c[ts - 1:ts, :]
    c_ref[0] = c
    ct_ref[0] = c.T[8:16, :]
    gexp_ref[0] = jnp.concatenate(
        [jnp.broadcast_to(g[:, h:h + 1], (ts, LANES)) for h in range(GDN_HEADS)], axis=1)
    bexp_ref[0] = jnp.concatenate(
        [jnp.broadcast_to(beta[:, GDN_HEADS + h:GDN_HEADS + h + 1], (ts, LANES)) for h in range(GDN_HEADS)], axis=1)


def _prep(x, w_small, params):
    b, s, d = x.shape
    ts = min(512, s)
    hw = GDN_HEADS * LANES
    return pl.pallas_call(
        _prep_kernel,
        grid=(b, s // ts),
        in_specs=[pl.BlockSpec((1, ts, d), lambda i, j: (i, j, 0)),
                  pl.BlockSpec((d, LANES), lambda i, j: (0, 0)),
                  pl.BlockSpec((8, LANES), lambda i, j: (0, 0))],
        out_specs=[pl.BlockSpec((1, ts, hw), lambda i, j: (i, j, 0)),
                   pl.BlockSpec((1, ts, hw), lambda i, j: (i, j, 0)),
                   pl.BlockSpec((1, ts, LANES), lambda i, j: (i, j, 0)),
                   pl.BlockSpec((1, 8, ts), lambda i, j: (i, 0, j))],
        out_shape=[jax.ShapeDtypeStruct((b, s, hw), F32),
                   jax.ShapeDtypeStruct((b, s, hw), F32),
                   jax.ShapeDtypeStruct((b, s, LANES), F32),
                   jax.ShapeDtypeStruct((b, 8, s), F32)],
        scratch_shapes=[pltpu.VMEM((1, LANES), F32)],
        compiler_params=pltpu.CompilerParams(
            dimension_semantics=("parallel", "arbitrary"), vmem_limit_bytes=VMEM_LIMIT),
        name="prep",
    )(x, w_small, params)


def _each(fn, *lists):
    return [fn(*args) for args in zip(*lists)]


def _unit_lower_inverse(ms, masks):
    eye, blk16, blk32 = masks
    hi = lambda a, b: _dot(a, b, HI)
    n1 = _each(lambda m: -jnp.where(blk16, m, 0.0), ms)
    l1 = _each(lambda m: jnp.where(blk32 & jnp.logical_not(blk16), m, 0.0), ms)
    l2 = _each(lambda m: jnp.where(blk32, 0.0, m), ms)
    n2 = _each(hi, n1, n1)
    p = _each(lambda a, b: hi(eye + a, eye + b), n1, n2)
    n4 = _each(hi, n2, n2)
    p = _each(lambda a, b: hi(a, eye + b), p, n4)
    n8 = _each(hi, n4, n4)
    d_inv = _each(lambda a, b: hi(a, eye + b), p, n8)
    dl = _each(hi, d_inv, l1)
    a32 = _each(lambda a, b: a - hi(b, a), d_inv, dl)
    al = _each(hi, a32, l2)
    return _each(lambda a, b: a - hi(b, a), a32, al)


def _gdn_kernel(q_ref, k_ref, v_ref, z_ref, g_ref, b_ref, cwq_ref, cwk_ref, cwv_ref, nw_ref,
                o_ref, qn, kn, vn, st, sol_s, qk_s, qg_s, kd_s, gl_s):
    s = q_ref.shape[1]
    c = CHUNK
    nh = q_ref.shape[2] // LANES
    row = lax.broadcasted_iota(jnp.int32, (s, LANES), 0)

    def conv_silu(x, w):
        y = x * w[3:4, :]
        for sh in (1, 2, 3):
            xs = jnp.where(row >= sh, pltpu.roll(x, sh, axis=0), 0.0)
            y = y + xs * w[3 - sh:4 - sh, :]
        return y * jax.nn.sigmoid(y)

    def l2norm(x):
        return x * lax.rsqrt(jnp.sum(x * x, axis=-1, keepdims=True) + 1e-6)

    for j in range(nh):
        hs = slice(j * LANES, (j + 1) * LANES)
        qn[:, hs] = l2norm(conv_silu(q_ref[0, :, hs], cwq_ref[:, hs])) * (GDN_DK ** -0.5)
        kn[:, hs] = l2norm(conv_silu(k_ref[0, :, hs], cwk_ref[:, hs]))
        vn[:, hs] = conv_silu(v_ref[0, :, hs], cwv_ref[:, hs])
    st[...] = jnp.zeros_like(st)

    ri = lax.broadcasted_iota(jnp.int32, (c, c), 0)
    ci = lax.broadcasted_iota(jnp.int32, (c, c), 1)
    tril = ri >= ci
    strict = ri > ci
    t_inc = tril.astype(F32)
    eye = (ri == ci).astype(F32)
    blk16 = (ri >> 4) == (ci >> 4)
    blk32 = (ri >> 5) == (ci >> 5)
    l2 = lax.broadcasted_iota(jnp.int32, (c, 2 * LANES), 0)
    j2 = lax.broadcasted_iota(jnp.int32, (c, 2 * LANES), 1)
    ux = jnp.where((j2 >= c) | (l2 > j2), 1.0, 0.0).astype(F32)
    nw = nw_ref[...]

    heads = list(range(nh))
    lanes_of = [slice(j * LANES, (j + 1) * LANES) for j in heads]

    def local_stage(ns):
        pairs = [(i, j) for i in range(len(ns)) for j in heads]
        rows = [pl.ds(pl.multiple_of(n * c, c), c) for n in ns]
        q = [qn[rows[i], lanes_of[j]] for i, j in pairs]
        k = [kn[rows[i], lanes_of[j]] for i, j in pairs]
        v = [vn[rows[i], lanes_of[j]] for i, j in pairs]
        gb = [g_ref[0, rows[i], lanes_of[j]] for i, j in pairs]
        bb = [b_ref[0, rows[i], lanes_of[j]] for i, j in pairs]
        d = _each(lambda g: _dot(t_inc, jnp.concatenate([g, g], axis=1) * ux, HI), gb)
        kb = _each(lambda a, b: a * b, k, bb)
        kk = _each(lambda a, b: _dot_nt(a, b, HI), kb, k)
        qk = _each(lambda a, b: _dot_nt(a.astype(BF16), b.astype(BF16)), q, k)
        gc = [x[:, LANES:] for x in d]
        decay = [jnp.where(tril, jnp.exp(x[:, :c]), 0.0) for x in d]
        m = _each(lambda a, b: jnp.where(strict, a * b, 0.0), kk, decay)
        a_inv = _unit_lower_inverse(m, (eye, blk16, blk32))
        egc = _each(jnp.exp, gc)
        rhs = _each(lambda vv, b, kbb, e: jnp.concatenate([vv * b, kbb * e], axis=1), v, bb, kb, egc)
        sol = _each(lambda a, r: _dot(a, r, HI), a_inv, rhs)
        gl = [x[c - 1:c, :] for x in gc]
        for p, (i, j) in enumerate(pairs):
            n = ns[i]
            qk_s[n, j] = (qk[p] * decay[p]).astype(BF16)
            qg_s[n, j] = (q[p] * egc[p]).astype(BF16)
            kd_s[n, j] = (k[p] * jnp.exp(gl[p] - gc[p])).astype(BF16)
            gl_s[n, j] = gl[p]
            sol_s[n, j] = sol[p]

    def state_stage(n):
        rows = pl.ds(pl.multiple_of(n * c, c), c)
        state = [st[j] for j in heads]
        state_b = [x.astype(BF16) for x in state]
        v_new = [sol_s[n, j, :, :LANES] - _dot(sol_s[n, j, :, LANES:].astype(BF16), state_b[j])
                 for j in heads]
        v_new_b = [x.astype(BF16) for x in v_new]
        o = [_dot(qg_s[n, j], state_b[j]) + _dot(qk_s[n, j], v_new_b[j]) for j in heads]
        new_state = [state[j] * jnp.exp(gl_s[n, j]) + _dot_tn(kd_s[n, j], v_new_b[j]) for j in heads]
        for j in heads:
            st[j] = new_state[j]
            on = o[j] * lax.rsqrt(jnp.mean(o[j] * o[j], axis=-1, keepdims=True) + 1e-6) * nw
            z = z_ref[0, rows, lanes_of[j]]
            o_ref[0, rows, lanes_of[j]] = on * (z * jax.nn.sigmoid(z))

    nchunks = s // c
    group = 4

    def local_group(gidx, carry):
        local_stage([gidx * group + i for i in range(group)])
        return carry

    def state_chunk(n, carry):
        state_stage(n)
        return carry

    lax.fori_loop(0, nchunks // group, local_group, 0)
    lax.fori_loop(0, nchunks, state_chunk, 0)


def _gdn(proj3, gexp, bexp, conv_w, norm_w, cols):
    b, s, _ = proj3.shape
    nh = 2
    nc = s // CHUNK
    wd = nh * LANES
    cq, ck, cv, cz = (c0 // nh for c0 in cols)

    def blk(c0):
        return pl.BlockSpec((1, s, wd), lambda i, h: (i, 0, c0 + h))

    def cw(c0):
        return pl.BlockSpec((conv_w.shape[0], wd), lambda i, h: (0, c0 + h))

    head = pl.BlockSpec((1, s, wd), lambda i, h: (i, 0, h))
    return pl.pallas_call(
        _gdn_kernel,
        grid=(b, GDN_HEADS // nh),
        in_specs=[blk(cq), blk(ck), blk(cv), blk(cz), head, head,
                  cw(0), cw(GDN_HEADS // nh), cw(2 * GDN_HEADS // nh),
                  pl.BlockSpec((1, LANES), lambda i, h: (0, 0))],
        out_specs=head,
        out_shape=jax.ShapeDtypeStruct((b, s, GDN_HEADS * LANES), F32),
        scratch_shapes=[pltpu.VMEM((s, wd), F32), pltpu.VMEM((s, wd), F32),
                        pltpu.VMEM((s, wd), F32), pltpu.VMEM((nh, GDN_DK, LANES), F32),
                        pltpu.VMEM((nc, nh, CHUNK, 2 * LANES), F32), pltpu.VMEM((nc, nh, CHUNK, CHUNK), BF16),
                        pltpu.VMEM((nc, nh, CHUNK, LANES), BF16), pltpu.VMEM((nc, nh, CHUNK, LANES), BF16),
                        pltpu.VMEM((nc, nh, 1, LANES), F32)],
        compiler_params=pltpu.CompilerParams(
            dimension_semantics=("parallel", "parallel"), vmem_limit_bytes=VMEM_LIMIT),
        name="gdn",
    )(proj3, proj3, proj3, proj3, gexp, bexp, conv_w, conv_w, conv_w, norm_w)


def _fox_kernel(q_ref, k_ref, v_ref, c_ref, ct_ref, o_ref, *, tk):
    tq = q_ref.shape[1]
    nj = q_ref.shape[2] // LANES
    g = pl.program_id(1)
    qi = pl.program_id(2)
    q = q_ref[0]
    cblk = c_ref[0]
    lane = lax.broadcasted_iota(jnp.int32, (tq, LANES), 1)
    qpos = qi * tq + lax.broadcasted_iota(jnp.int32, (tq, tk), 0)
    kofs = lax.broadcasted_iota(jnp.int32, (tq, tk), 1)
    heads = list(range(nj))
    lanes_of = [slice(j * LANES, (j + 1) * LANES) for j in heads]
    ccol = [jnp.sum(jnp.where(lane == 8 + g * nj + j, cblk, 0.0), axis=-1, keepdims=True) for j in heads]
    qh = [q[:, hs] * jnp.asarray(FX_DH ** -0.5, BF16) for hs in lanes_of]

    def body(kv, carry):
        k0 = pl.multiple_of(kv * tk, tk)
        causal = qpos >= k0 + kofs
        kblk = k_ref[0, pl.ds(k0, tk), :]
        vblk = v_ref[0, pl.ds(k0, tk), :]
        sc = [_dot_nt(qh[j], kblk[:, lanes_of[j]]) for j in heads]
        crow = [ct_ref[0, pl.ds(g * nj + j, 1), pl.ds(k0, tk)] for j in heads]
        sc = [jnp.where(causal, sc[j] + ccol[j] - crow[j], -1e30) for j in heads]
        m_new = [jnp.maximum(carry[j][0], jnp.max(sc[j], axis=-1, keepdims=True)) for j in heads]
        a = [jnp.exp(carry[j][0] - m_new[j]) for j in heads]
        p = [jnp.exp(sc[j] - m_new[j]) for j in heads]
        l = [a[j] * carry[j][1] + jnp.sum(p[j], axis=-1, keepdims=True) for j in heads]
        acc = [a[j] * carry[j][2] + _dot(p[j].astype(BF16), vblk[:, lanes_of[j]]) for j in heads]
        return tuple((m_new[j], l[j], acc[j]) for j in heads)

    init = tuple((jnp.full((tq, 1), -1e30, F32), jnp.zeros((tq, 1), F32), jnp.zeros((tq, LANES), F32))
                 for _ in heads)
    nkv = (qi * tq + tq - 1) // tk + 1
    res = lax.fori_loop(0, nkv, body, init)
    o_ref[0] = jnp.concatenate([acc / l for _, l, acc in res], axis=1)


def _fox(pf3, c, ct):
    b, s, _ = pf3.shape
    wd = 4 * LANES
    tq = min(128, s)
    tk = min(256, s)
    ngrp = FX_HEADS * LANES // wd
    return pl.pallas_call(
        functools.partial(_fox_kernel, tk=tk),
        grid=(b, ngrp, s // tq),
        in_specs=[pl.BlockSpec((1, tq, wd), lambda i, h, t: (i, t, h)),
                  pl.BlockSpec((1, s, wd), lambda i, h, t: (i, 0, ngrp + h)),
                  pl.BlockSpec((1, s, wd), lambda i, h, t: (i, 0, 2 * ngrp + h)),
                  pl.BlockSpec((1, tq, LANES), lambda i, h, t: (i, t, 0)),
                  pl.BlockSpec((1, 8, s), lambda i, h, t: (i, 0, 0))],
        out_specs=pl.BlockSpec((1, tq, wd), lambda i, h, t: (i, t, h)),
        out_shape=jax.ShapeDtypeStruct((b, s, ngrp * wd), F32),
        compiler_params=pltpu.CompilerParams(
            dimension_semantics=("parallel", "parallel", "parallel"), vmem_limit_bytes=VMEM_LIMIT),
        name="fox",
    )(pf3, pf3, pf3, c, ct)


def _mix_kernel(oa_ref, ob_ref, ga_ref, gb_ref, x_ref, wa_ref, wb_ref, wo_ref, g1_ref, b1_ref, o_ref, *, alpha):
    ya = _dot(oa_ref[...].astype(BF16), wa_ref[...])
    yb = _dot(ob_ref[...].astype(BF16), wb_ref[...])
    mix = jax.nn.sigmoid(ga_ref[...]) * ya + jax.nn.sigmoid(gb_ref[...]) * yb
    z = alpha * x_ref[...] + _dot(mix.astype(BF16), wo_ref[...])
    o_ref[...] = _layernorm(z, g1_ref[...], b1_ref[...])


def _mix(oa, ob, proj, x2, wa, wb, wo, g1, b1, alpha):
    t, d = x2.shape
    tm = min(512, t)
    w, w2 = oa.shape[1], ob.shape[1]
    full = lambda r, c: pl.BlockSpec((r, c), lambda i: (0, 0))
    return pl.pallas_call(
        functools.partial(_mix_kernel, alpha=alpha),
        grid=(t // tm,),
        in_specs=[pl.BlockSpec((tm, w), lambda i: (i, 0)),
                  pl.BlockSpec((tm, w2), lambda i: (i, 0)),
                  pl.BlockSpec((tm, d), lambda i: (i, 0)),
                  pl.BlockSpec((tm, d), lambda i: (i, 1)),
                  pl.BlockSpec((tm, d), lambda i: (i, 0)),
                  full(w, d), full(w2, d), full(d, d), full(1, d), full(1, d)],
        out_specs=pl.BlockSpec((tm, d), lambda i: (i, 0)),
        out_shape=jax.ShapeDtypeStruct((t, d), F32),
        compiler_params=pltpu.CompilerParams(
            dimension_semantics=("parallel",), vmem_limit_bytes=VMEM_LIMIT),
        name="mix",
    )(oa, ob, proj, proj, x2, wa, wb, wo, g1, b1)


def _route_head(keys_ref, q_scr, hd, lh, scr, flat, cmask):
    s_scr, i_scr, best_scr, eh_scr = scr
    nk, kk = PEER_NKEYS, PEER_TOPK
    iota_k = lax.broadcasted_iota(jnp.int32, (nk, LANES), 0)
    neg = jnp.float32(-jnp.inf)
    for p in range(2):
        vals = _dot_nt(keys_ref[hd * 2 + p], q_scr[hd * 2 + p, lh])
        for r in range(kk):
            m = jnp.max(vals, axis=0, keepdims=True)
            am = jnp.min(jnp.where(vals == m, iota_k, nk), axis=0, keepdims=True)
            s_scr[p * kk + r:p * kk + r + 1, :] = m
            i_scr[p * kk + r:p * kk + r + 1, :] = am
            vals = jnp.where(iota_k == am, neg, vals)
    s1 = s_scr[kk:kk + 8, :]
    i1 = i_scr[kk:kk + 8, :]
    cand = [s_scr[0:1, :] + s_scr[kk:2 * kk, :]]
    cidx = [i_scr[0:1, :] * nk + i_scr[kk:2 * kk, :]]
    for a in range(1, 8):
        cand.append(s_scr[a:a + 1, :] + s1)
        cidx.append(i_scr[a:a + 1, :] * nk + i1)
    cand.append(s_scr[8:kk, :] + s_scr[kk:kk + 1, :])
    cidx.append(i_scr[8:kk, :] * nk + i_scr[kk:kk + 1, :])
    vals = jnp.concatenate(cand, axis=0) + cmask
    cidx = jnp.concatenate(cidx, axis=0)
    for r in range(kk):
        m = jnp.max(vals, axis=0, keepdims=True)
        am = jnp.min(jnp.where(vals == m, flat, 2 * kk * kk), axis=0, keepdims=True)
        sel = flat == am
        best_scr[r:r + 1, :] = m
        eh_scr[r:r + 1, :] = jnp.max(jnp.where(sel, cidx, -1), axis=0, keepdims=True)
        vals = jnp.where(sel, neg, vals)
    bs = best_scr[...]
    ex = jnp.exp(bs - bs[0:1, :])
    return ex / jnp.sum(ex, axis=0, keepdims=True), eh_scr[...]


def _candidate_tables():
    kk = PEER_TOPK
    pairs = [(0, bb) for bb in range(kk)]
    for a in range(1, 8):
        pairs += [(a, bb) for bb in range(8)]
    pairs += [(a, 0) for a in range(8, kk)]
    real = [(a + 1) * (bb + 1) <= kk for a, bb in pairs]
    flat = [a * kk + bb if ok else kk * kk + r for r, ((a, bb), ok) in enumerate(zip(pairs, real))]
    flat = jnp.broadcast_to(jnp.asarray(flat, jnp.int32)[:, None], (len(pairs), LANES))
    cmask = jnp.broadcast_to(jnp.asarray([0.0 if ok else -jnp.inf for ok in real], F32)[:, None],
                             (len(pairs), LANES))
    return flat, cmask


def _route_kernel(h_ref, wq_ref, keys_ref, flat_ref, cmask_ref, e_ref, g_ref,
                  q_scr, s_scr, i_scr, best_scr, eh_scr, eall_scr, gall_scr):
    tg = h_ref.shape[0]
    kk = PEER_TOPK
    nlh = tg // LANES
    q = _dot(h_ref[...].astype(BF16), wq_ref[...])
    for j in range(2 * PEER_HEADS):
        for lh in range(nlh):
            q_scr[j, lh] = q[lh * LANES:(lh + 1) * LANES, j * LANES:(j + 1) * LANES].astype(BF16)
    flat = flat_ref[...]
    cmask = cmask_ref[...]

    def piece(p, carry):
        hd = p // nlh
        lh = p % nlh
        gates, ids = _route_head(keys_ref, q_scr, hd, lh, (s_scr, i_scr, best_scr, eh_scr), flat, cmask)
        r0 = pl.multiple_of(hd * kk, kk)
        gall_scr[lh, pl.ds(r0, kk), :] = gates
        eall_scr[lh, pl.ds(r0, kk), :] = ids
        return carry

    lax.fori_loop(0, PEER_HEADS * nlh, piece, 0)
    for lh in range(nlh):
        e_ref[lh * LANES:(lh + 1) * LANES, :] = eall_scr[lh].T
        g_ref[lh * LANES:(lh + 1) * LANES, :] = gall_scr[lh].T


def _route(h, wq, keys):
    t, d = h.shape
    tg = 256
    nq = wq.shape[1]
    kk = PEER_TOPK
    hk = PEER_HEADS * kk
    nlh = tg // LANES
    flat, cmask = _candidate_tables()
    const = lambda shape: pl.BlockSpec(shape, lambda i: (0,) * len(shape))
    return pl.pallas_call(
        _route_kernel,
        grid=(t // tg,),
        in_specs=[pl.BlockSpec((tg, d), lambda i: (i, 0)),
                  const((d, nq)), const(keys.shape), const(flat.shape), const(cmask.shape)],
        out_specs=[pl.BlockSpec((tg, hk), lambda i: (i, 0)), pl.BlockSpec((tg, hk), lambda i: (i, 0))],
        out_shape=[jax.ShapeDtypeStruct((t, hk), jnp.int32), jax.ShapeDtypeStruct((t, hk), F32)],
        scratch_shapes=[pltpu.VMEM((2 * PEER_HEADS, nlh, LANES, LANES), BF16),
                        pltpu.VMEM((2 * kk, LANES), F32), pltpu.VMEM((2 * kk, LANES), jnp.int32),
                        pltpu.VMEM((kk, LANES), F32), pltpu.VMEM((kk, LANES), jnp.int32),
                        pltpu.VMEM((nlh, hk, LANES), jnp.int32), pltpu.VMEM((nlh, hk, LANES), F32)],
        compiler_params=pltpu.CompilerParams(
            dimension_semantics=("parallel",), vmem_limit_bytes=VMEM_LIMIT),
        name="route",
    )(h, wq, keys, flat, cmask)


PEER_SC_SHARE_PCT = 44
PEER_TC_FIRST_PCT = 53
SC_WORKERS = 32
SC_LANES = 16
SC_CHUNK = 32
SC_DBLK = 16
SC_EBLK = 4
SC_JUNROLL = 8


def _sc_mesh():
    return plsc.VectorSubcoreMesh(core_axis_name="c", subcore_axis_name="s")


def _sc_worker():
    return lax.axis_index("s") * 2 + lax.axis_index("c")


def _sc_pre(ids, h, u_tab):
    ts, hk = ids.shape
    d = h.shape[1]
    per_w = ts // SC_WORKERS
    nch = hk // SC_CHUNK

    def body(ids_hbm, h_hbm, u_hbm, out_hbm, idx_v, h_v, rows_a, rows_b, pre_v, sem_a, sem_b):
        base = _sc_worker() * per_w
        lane = lax.iota(jnp.int32, SC_LANES)
        bufs = (rows_a, rows_b)
        sems = (sem_a, sem_b)

        def gather(c):
            return pltpu.async_copy(u_hbm.at[idx_v.at[pl.ds(c * SC_CHUNK, SC_CHUNK)]], bufs[c % 2], sems[c % 2])

        def token(i, carry):
            t = base + i
            pltpu.sync_copy(ids_hbm.at[t], idx_v)
            pltpu.sync_copy(h_hbm.at[t], h_v)
            cps = {0: gather(0)}
            for c in range(nch):
                if c + 1 < nch:
                    cps[c + 1] = gather(c + 1)
                cps[c].wait()
                rows = bufs[c % 2]
                for g in range(SC_CHUNK // SC_LANES):
                    def experts(q, vec, g=g, rows=rows):
                        e0 = g * SC_LANES + q * SC_EBLK

                        def span(jb, accs, rows=rows, e0=e0):
                            accs = list(accs)
                            for jj in range(SC_JUNROLL):
                                sl = pl.ds((jb * SC_JUNROLL + jj) * SC_LANES, SC_LANES)
                                hv = h_v[sl]
                                for i in range(SC_EBLK):
                                    accs[i] = accs[i] + rows[e0 + i, sl] * hv
                            return tuple(accs)
                        accs = lax.fori_loop(0, d // (SC_LANES * SC_JUNROLL), span,
                                             tuple(jnp.zeros((SC_LANES,), F32) for _ in range(SC_EBLK)))
                        for i in range(SC_EBLK):
                            vec = jnp.where(lane == q * SC_EBLK + i, jnp.sum(accs[i]), vec)
                        return vec
                    vec = lax.fori_loop(0, SC_LANES // SC_EBLK, experts, jnp.zeros((SC_LANES,), F32))
                    pre_v[pl.ds(c * SC_CHUNK + g * SC_LANES, SC_LANES)] = vec
            pltpu.sync_copy(pre_v, out_hbm.at[t])
            return carry

        lax.fori_loop(0, per_w, token, 0)

    return pl.kernel(
        body, out_type=jax.ShapeDtypeStruct((ts, hk), F32), mesh=_sc_mesh(),
        scratch_types=[pltpu.VMEM((hk,), jnp.int32), pltpu.VMEM((d,), F32),
                       pltpu.VMEM((SC_CHUNK, d), F32), pltpu.VMEM((SC_CHUNK, d), F32),
                       pltpu.VMEM((hk,), F32), pltpu.SemaphoreType.DMA, pltpu.SemaphoreType.DMA],
        compiler_params=pltpu.CompilerParams(needs_layout_passes=False),
        name="sc_pre",
    )(ids, h, u_tab)


def _sc_out(ids, act, v_tab):
    ts, hk = ids.shape
    d = v_tab.shape[1]
    per_w = ts // SC_WORKERS
    nch = hk // SC_CHUNK

    def body(ids_hbm, act_hbm, v_hbm, out_hbm, idx_v, act_v, rows_a, rows_b, y_v, sem_a, sem_b):
        base = _sc_worker() * per_w
        lane = lax.iota(jnp.int32, SC_LANES)
        bufs = (rows_a, rows_b)
        sems = (sem_a, sem_b)

        def gather(c):
            return pltpu.async_copy(v_hbm.at[idx_v.at[pl.ds(c * SC_CHUNK, SC_CHUNK)]], bufs[c % 2], sems[c % 2])

        def token(i, carry):
            t = base + i
            pltpu.sync_copy(ids_hbm.at[t], idx_v)
            pltpu.sync_copy(act_hbm.at[t], act_v)
            for j in range(d // SC_LANES):
                y_v[pl.ds(j * SC_LANES, SC_LANES)] = jnp.zeros((SC_LANES,), F32)
            cps = {0: gather(0)}
            for c in range(nch):
                if c + 1 < nch:
                    cps[c + 1] = gather(c + 1)
                cps[c].wait()
                rows = bufs[c % 2]
                for db in range(d // (SC_DBLK * SC_LANES)):
                    def expert(r, accs, c=c, rows=rows, db=db):
                        a = plsc.load_gather(act_v, [jnp.zeros((SC_LANES,), jnp.int32) + (c * SC_CHUNK + r)])
                        return tuple(
                            accs[j] + a * rows[r, pl.ds((db * SC_DBLK + j) * SC_LANES, SC_LANES)]
                            for j in range(SC_DBLK))
                    accs = lax.fori_loop(0, SC_CHUNK, expert,
                                         tuple(jnp.zeros((SC_LANES,), F32) for _ in range(SC_DBLK)))
                    for j in range(SC_DBLK):
                        plsc.addupdate(y_v.at[pl.ds((db * SC_DBLK + j) * SC_LANES, SC_LANES)], accs[j])
            pltpu.sync_copy(y_v, out_hbm.at[t])
            return carry

        lax.fori_loop(0, per_w, token, 0)

    return pl.kernel(
        body, out_type=jax.ShapeDtypeStruct((ts, d), F32), mesh=_sc_mesh(),
        scratch_types=[pltpu.VMEM((hk,), jnp.int32), pltpu.VMEM((hk,), F32),
                       pltpu.VMEM((SC_CHUNK, d), F32), pltpu.VMEM((SC_CHUNK, d), F32),
                       pltpu.VMEM((d,), F32), pltpu.SemaphoreType.DMA, pltpu.SemaphoreType.DMA],
        compiler_params=pltpu.CompilerParams(needs_layout_passes=False),
        name="sc_out",
    )(ids, act, v_tab)


def _act_kernel(pre_ref, gate_ref, after_ref, o_ref):
    del after_ref
    pre = pre_ref[...]
    o_ref[...] = 0.5 * pre * (1.0 + lax.erf(pre * (2.0 ** -0.5))) * gate_ref[...]


def _act(pre, gate, after):
    t, hk = pre.shape
    tm = 256
    assert t % tm == 0
    spec = pl.BlockSpec((tm, hk), lambda i: (i, 0))
    return pl.pallas_call(
        _act_kernel, grid=(t // tm,),
        in_specs=[spec, spec, pl.BlockSpec((8, LANES), lambda i: (0, 0))], out_specs=spec,
        out_shape=jax.ShapeDtypeStruct((t, hk), F32),
        compiler_params=pltpu.CompilerParams(dimension_semantics=("parallel",)),
        name="peer_act",
    )(pre, gate, after)


def _ln2_kernel(h_ref, y_ref, g_ref, b_ref, o_ref, *, alpha):
    o_ref[...] = _layernorm(alpha * h_ref[...] + y_ref[...], g_ref[...], b_ref[...])


def _ln2(h, y, g2, b2, alpha):
    t, d = h.shape
    tm = 256
    assert t % tm == 0
    spec = pl.BlockSpec((tm, d), lambda i: (i, 0))
    vec = pl.BlockSpec((1, d), lambda i: (0, 0))
    return pl.pallas_call(
        functools.partial(_ln2_kernel, alpha=alpha), grid=(t // tm,),
        in_specs=[spec, spec, vec, vec], out_specs=spec,
        out_shape=jax.ShapeDtypeStruct((t, d), F32),
        compiler_params=pltpu.CompilerParams(dimension_semantics=("parallel",)),
        name="peer_ln2",
    )(h, y, g2, b2)


def _peer_kernel(h_cur_ref, h_nxt_ref, wq_ref, keys_ref, flat_ref, cmask_ref, uv_hbm, g2_ref, b2_ref, o_ref,
                 q_scr, s_scr, i_scr, best_scr, eh_scr, eall_scr, idv_scr, ids_smem, gate_scr,
                 buf_a, buf_b, sem, idsem, y_scr, *, tt, alpha):
    s = pl.program_id(0)
    tg, d = h_cur_ref.shape
    nk, kk = PEER_NKEYS, PEER_TOPK
    hk = PEER_HEADS * kk
    nlh = tg // LANES
    nsub = tg // (2 * tt)
    assert nsub == PEER_HEADS * nlh
    nslab, sub = uv_hbm.shape[1], uv_hbm.shape[2]
    half = sub // 2
    bufs = (buf_a, buf_b)
    last = pl.num_programs(0) - 1
    rslot = s % 3
    pslot = (s + 2) % 3
    eslot = (s + 1) % 3

    def issue(idslot, row0, slot):
        for t in range(tt):
            for k in range(hk):
                e = ids_smem[idslot, row0 + t, k]
                pltpu.make_async_copy(uv_hbm.at[e], bufs[slot].at[:, pl.ds((t * hk + k) * sub, sub), :],
                                      sem.at[slot]).start(priority=k % 2)

    def wait(slot):
        pltpu.make_async_copy(bufs[slot], bufs[slot], sem.at[slot]).wait()

    q = _dot(h_nxt_ref[...].astype(BF16), wq_ref[...])
    for j in range(2 * PEER_HEADS):
        for lh in range(nlh):
            q_scr[j, lh] = q[lh * LANES:(lh + 1) * LANES, j * LANES:(j + 1) * LANES].astype(BF16)
    flat = flat_ref[...]
    cmask = cmask_ref[...]

    def route_piece(piece):
        hd = piece // nlh
        lh = piece % nlh
        gates, ids = _route_head(keys_ref, q_scr, hd, lh, (s_scr, i_scr, best_scr, eh_scr), flat, cmask)
        r0 = pl.multiple_of(hd * kk, kk)
        gate_scr[rslot, lh, pl.ds(r0, kk), :] = gates
        eall_scr[lh, pl.ds(r0, kk), :] = ids

    lane = lax.broadcasted_iota(jnp.int32, (hk, LANES), 1)

    def compute(row0, slot):
        buf = bufs[slot]
        gt = gate_scr[eslot, row0 // LANES]
        lane0 = row0 % LANES
        for t in range(tt):
            hrow = h_cur_ref[pl.ds(row0 + t, 1), :]

            def rows(c, j):
                return buf[c, pl.ds(t * hk * sub + j, hk, stride=sub), :]

            part = None
            for j in range(half):
                for c in range(nslab):
                    seg = j * nslab + c
                    term = rows(c, j) * hrow[:, seg * LANES:(seg + 1) * LANES]
                    part = term if part is None else part + term
            pre = jnp.sum(part, axis=-1, keepdims=True)
            gate = jnp.sum(jnp.where(lane == lane0 + t, gt, 0.0), axis=-1, keepdims=True)
            act = 0.5 * pre * (1.0 + lax.erf(pre * (2.0 ** -0.5))) * gate
            yrow = slot * tt + t
            for j in range(half):
                for c in range(nslab):
                    seg = j * nslab + c
                    y_scr[yrow:yrow + 1, seg * LANES:(seg + 1) * LANES] = jnp.sum(
                        act * rows(c, half + j), axis=0, keepdims=True)

    def substep(j, carry):
        row0 = pl.multiple_of(j * 2 * tt, 2 * tt)
        issue(eslot, row0 + tt, 1)
        route_piece(j)
        wait(0)
        compute(row0, 0)
        wait(1)
        wrap = j == nsub - 1
        issue(jnp.where(wrap, pslot, eslot), jnp.where(wrap, 0, row0 + 2 * tt), 0)
        compute(row0 + tt, 1)
        z = alpha * h_cur_ref[pl.ds(row0, 2 * tt), :] + y_scr[...]
        o_ref[pl.ds(row0, 2 * tt), :] = _layernorm(z, g2_ref[...], b2_ref[...])
        return carry

    @pl.when(s < 2)
    def _():
        o_ref[...] = jnp.zeros_like(o_ref)

        def piece(p, carry):
            route_piece(p)
            return carry
        lax.fori_loop(0, nsub, piece, 0)

    @pl.when(s == 1)
    def _():
        issue(pslot, 0, 0)

    @pl.when(s >= 2)
    def _():
        lax.fori_loop(0, nsub, substep, 0)

    @pl.when(s == last)
    def _():
        wait(0)

    for lh in range(nlh):
        idv_scr[lh * LANES:(lh + 1) * LANES, :] = eall_scr[lh].T
    publish = pltpu.make_async_copy(idv_scr, ids_smem.at[rslot], idsem)
    publish.start()
    publish.wait()


def _peer(h1, wq, keys, uv_tab, g2, b2, alpha):
    t, d = h1.shape
    tg = 256
    tt = 8
    ngrp = t // tg
    nq = wq.shape[1]
    kk = PEER_TOPK
    hk = PEER_HEADS * kk
    nlh = tg // LANES
    nslab, sub = uv_tab.shape[1], uv_tab.shape[2]
    flat, cmask = _candidate_tables()
    const = lambda shape: pl.BlockSpec(shape, lambda i: (0,) * len(shape))
    return pl.pallas_call(
        functools.partial(_peer_kernel, tt=tt, alpha=alpha),
        grid=(ngrp + 2,),
        in_specs=[pl.BlockSpec((tg, d), lambda i: (jnp.maximum(i - 2, 0), 0)),
                  pl.BlockSpec((tg, d), lambda i: (jnp.minimum(i, ngrp - 1), 0)),
                  const((d, nq)), const(keys.shape), const(flat.shape), const(cmask.shape),
                  pl.BlockSpec(memory_space=pl.ANY),
                  const((1, d)), const((1, d))],
        out_specs=pl.BlockSpec((tg, d), lambda i: (jnp.maximum(i - 2, 0), 0)),
        out_shape=jax.ShapeDtypeStruct((t, d), F32),
        scratch_shapes=[pltpu.VMEM((2 * PEER_HEADS, nlh, LANES, LANES), BF16),
                        pltpu.VMEM((2 * kk, LANES), F32), pltpu.VMEM((2 * kk, LANES), jnp.int32),
                        pltpu.VMEM((kk, LANES), F32), pltpu.VMEM((kk, LANES), jnp.int32),
                        pltpu.VMEM((nlh, hk, LANES), jnp.int32),
                        pltpu.VMEM((tg, hk), jnp.int32),
                        pltpu.SMEM((3, tg, hk), jnp.int32),
                        pltpu.VMEM((3, nlh, hk, LANES), F32),
                        pltpu.VMEM((nslab, tt * hk * sub, LANES), F32),
                        pltpu.VMEM((nslab, tt * hk * sub, LANES), F32),
                        pltpu.SemaphoreType.DMA((2,)), pltpu.SemaphoreType.DMA,
                        pltpu.VMEM((2 * tt, d), F32)],
        compiler_params=pltpu.CompilerParams(
            dimension_semantics=("arbitrary",), vmem_limit_bytes=VMEM_LIMIT),
        name="peer",
    )(h1, h1, wq, keys, flat, cmask, uv_tab, g2, b2)


def _layer(h, w_in, conv_w, a_log, dt_bias, norm_w, f_bias, w_out_gdn, w_out_fox, w_o, ln1_g, ln1_b,
           peer_wq, peer_keys, peer_u, peer_v, ln2_g, ln2_b, alpha):
    b, s, d = h.shape
    t = b * s
    qk = GDN_HEADS * GDN_DK
    fw = FX_HEADS * FX_DH
    o_gz = 4 * qk
    o_ga = o_gz
    o_fq = o_ga + 2 * GDN_HEADS
    o_ff = o_fq + 3 * fw
    o_gate = o_ff + FX_HEADS
    w_big = jnp.concatenate([w_in[:, o_gate:], w_in[:, :o_gz]], axis=1).astype(BF16)
    pad = LANES - FX_DH
    w_fox = jnp.pad(w_in[:, o_fq:o_ff].reshape(d, 3 * FX_HEADS, FX_DH), ((0, 0), (0, 0), (0, pad)))
    w_fox = w_fox.reshape(d, 3 * FX_HEADS * LANES).astype(BF16)
    w_out_fox_p = jnp.pad(w_out_fox.reshape(FX_HEADS, FX_DH, d), ((0, 0), (0, pad), (0, 0)))
    w_out_fox_p = w_out_fox_p.reshape(FX_HEADS * LANES, d).astype(BF16)
    n_small = 2 * GDN_HEADS + FX_HEADS
    w_small = jnp.concatenate([w_in[:, o_ga:o_fq], w_in[:, o_ff:o_gate],
                               jnp.zeros((d, LANES - n_small), F32)], axis=1)
    params = jnp.zeros((8, LANES), F32)
    params = params.at[0, :GDN_HEADS].set(a_log).at[1, :GDN_HEADS].set(dt_bias)
    params = params.at[2, 2 * GDN_HEADS:n_small].set(f_bias)

    x2 = h.reshape(t, d)
    proj = _in_proj(x2, w_big, F32, "in_proj")
    pf = _in_proj(x2, w_fox, BF16, "in_proj_fox")
    gexp, bexp, c, ct = _prep(h, w_small, params)
    proj3 = proj.reshape(b, s, proj.shape[1])
    gdn0 = 2 * d // LANES
    oa = _gdn(proj3, gexp, bexp, conv_w, norm_w.reshape(1, LANES),
              (gdn0, gdn0 + GDN_HEADS, gdn0 + 2 * GDN_HEADS, gdn0 + 3 * GDN_HEADS))
    ob = _fox(pf.reshape(b, s, pf.shape[1]), c, ct)
    h1 = _mix(oa.reshape(t, qk), ob.reshape(t, FX_HEADS * LANES), proj, x2,
              w_out_gdn.astype(BF16), w_out_fox_p, w_o.astype(BF16),
              ln1_g.reshape(1, d), ln1_b.reshape(1, d), alpha)
    keys = peer_keys.reshape(2 * PEER_HEADS, PEER_NKEYS, peer_keys.shape[-1]).astype(BF16)
    ne = peer_u.shape[0]
    half = PEER_SUB // 2
    nslab = d // (half * LANES)
    uv_tab = jnp.concatenate([peer_u.reshape(ne, half, nslab, LANES),
                              peer_v.reshape(ne, half, nslab, LANES)], axis=1).transpose(0, 2, 1, 3)
    wq = peer_wq.astype(BF16)
    g2, b2 = ln2_g.reshape(1, d), ln2_b.reshape(1, d)
    t_sc = (t * PEER_SC_SHARE_PCT // 100) // 256 * 256
    t_tc = t - t_sc
    if t_sc:
        h_sc = h1[t_tc:]
        ids_sc, gate_sc = _route(h_sc, wq, keys)
        t_a = (t_tc * PEER_TC_FIRST_PCT // 100) // 256 * 256
        pre = _sc_pre(ids_sc, h_sc, peer_u)
        out_a = _peer(h1[:t_a], wq, keys, uv_tab, g2, b2, alpha)
        y_sc = _sc_out(ids_sc, _act(pre, gate_sc, out_a), peer_v)
        out_b = _peer(h1[t_a:t_tc], wq, keys, uv_tab, g2, b2, alpha)
        out = jnp.concatenate([out_a, out_b, _ln2(h_sc, y_sc, g2, b2, alpha)], axis=0)
    else:
        out = _peer(h1, wq, keys, uv_tab, g2, b2, alpha)
    return out.reshape(b, s, d)


def kernel(x, w_in, gdn_conv_w, gdn_a_log, gdn_dt_bias, gdn_norm_w, fox_f_bias, w_out_gdn, w_out_fox, w_o,
           ln1_g, ln1_b, peer_wq, peer_keys, peer_u, peer_v, ln2_g, ln2_b):
    depth = w_in.shape[0]
    alpha = (2.0 * depth) ** 0.25
    h = x
    for l in range(depth):
        h = _layer(h, w_in[l], gdn_conv_w[l], gdn_a_log[l], gdn_dt_bias[l], gdn_norm_w[l], fox_f_bias[l],
                   w_out_gdn[l], w_out_fox[l], w_o[l], ln1_g[l], ln1_b[l], peer_wq[l], peer_keys[l],
                   peer_u[l], peer_v[l], ln2_g[l], ln2_b[l], alpha)
    return h
```

```python
import functools

import jax
import jax.numpy as jnp
from jax import lax
from jax.experimental import pallas as pl
from jax.experimental.pallas import tpu as pltpu
from jax.experimental.pallas import tpu_sc as plsc

F32 = jnp.float32
BF16 = jnp.bfloat16
HI = lax.Precision.HIGHEST

LANES = 128
CHUNK = 64
GDN_HEADS = 4
GDN_DK = 128
FX_HEADS = 8
FX_DH = 64
PEER_HEADS = 8
PEER_NKEYS = 128
PEER_TOPK = 16
PEER_SUB = 4
LN_EPS = 1e-5
VMEM_LIMIT = 48 * 1024 * 1024


def _dot(a, b, prec=None):
    return jnp.dot(a, b, preferred_element_type=F32, precision=prec)


def _dot_nt(a, b, prec=None):
    return lax.dot_general(a, b, (((1,), (1,)), ((), ())), preferred_element_type=F32, precision=prec)


def _dot_tn(a, b, prec=None):
    return lax.dot_general(a, b, (((0,), (0,)), ((), ())), preferred_element_type=F32, precision=prec)


def _softplus(x):
    return jnp.maximum(x, 0.0) + jnp.log1p(jnp.exp(-jnp.abs(x)))


def _layernorm(z, g, b):
    mu = jnp.mean(z, axis=-1, keepdims=True)
    zc = z - mu
    var = jnp.mean(zc * zc, axis=-1, keepdims=True)
    return zc * lax.rsqrt(var + LN_EPS) * g + b


def _mm_kernel(x_ref, w_ref, o_ref):
    o_ref[...] = _dot(x_ref[...].astype(BF16), w_ref[...]).astype(o_ref.dtype)


def _in_proj(x2, w_big, out_dtype, name):
    t, d = x2.shape
    n = w_big.shape[1]
    tm = min(1024, t)
    tn = 512
    return pl.pallas_call(
        _mm_kernel,
        grid=(t // tm, n // tn),
        in_specs=[pl.BlockSpec((tm, d), lambda i, j: (i, 0)),
                  pl.BlockSpec((d, tn), lambda i, j: (0, j))],
        out_specs=pl.BlockSpec((tm, tn), lambda i, j: (i, j)),
        out_shape=jax.ShapeDtypeStruct((t, n), out_dtype),
        compiler_params=pltpu.CompilerParams(
            dimension_semantics=("parallel", "parallel"), vmem_limit_bytes=VMEM_LIMIT),
        name=name,
    )(x2, w_big)


def _prep_kernel(x_ref, w_ref, par_ref, gexp_ref, bexp_ref, c_ref, ct_ref, carry_ref):
    ts = x_ref.shape[1]

    @pl.when(pl.program_id(1) == 0)
    def _():
        carry_ref[...] = jnp.zeros_like(carry_ref)

    small = _dot(x_ref[0], w_ref[...], HI)
    a_log = par_ref[0:1, :]
    dt_bias = par_ref[1:2, :]
    f_bias = par_ref[2:3, :]
    g = -jnp.exp(a_log) * _softplus(small + dt_bias)
    beta = jax.nn.sigmoid(small)
    lane = lax.broadcasted_iota(jnp.int32, (ts, LANES), 1)
    log_f = jnp.where((lane >= 8) & (lane < 16), -_softplus(-(small + f_bias)), 0.0)
    row = lax.broadcasted_iota(jnp.int32, (ts, ts), 0)
    col = lax.broadcasted_iota(jnp.int32, (ts, ts), 1)
    tril = (row >= col).astype(F32)
    c = _dot(tril, log_f, HI) + carry_ref[...]
    carry_ref[...] = c[ts - 1:ts, :]
    c_ref[0] = c
    ct_ref[0] = c.T[8:16, :]
    gexp_ref[0] = jnp.concatenate(
        [jnp.broadcast_to(g[:, h:h + 1], (ts, LANES)) for h in range(GDN_HEADS)], axis=1)
    bexp_ref[0] = jnp.concatenate(
        [jnp.broadcast_to(beta[:, GDN_HEADS + h:GDN_HEADS + h + 1], (ts, LANES)) for h in range(GDN_HEADS)], axis=1)


def _prep(x, w_small, params):
    b, s, d = x.shape
    ts = min(512, s)
    hw = GDN_HEADS * LANES
    return pl.pallas_call(
        _prep_kernel,
        grid=(b, s // ts),
        in_specs=[pl.BlockSpec((1, ts, d), lambda i, j: (i, j, 0)),
                  pl.BlockSpec((d, LANES), lambda i, j: (0, 0)),
                  pl.BlockSpec((8, LANES), lambda i, j: (0, 0))],
        out_specs=[pl.BlockSpec((1, ts, hw), lambda i, j: (i, j, 0)),
                   pl.BlockSpec((1, ts, hw), lambda i, j: (i, j, 0)),
                   pl.BlockSpec((1, ts, LANES), lambda i, j: (i, j, 0)),
                   pl.BlockSpec((1, 8, ts), lambda i, j: (i, 0, j))],
        out_shape=[jax.ShapeDtypeStruct((b, s, hw), F32),
                   jax.ShapeDtypeStruct((b, s, hw), F32),
                   jax.ShapeDtypeStruct((b, s, LANES), F32),
                   jax.ShapeDtypeStruct((b, 8, s), F32)],
        scratch_shapes=[pltpu.VMEM((1, LANES), F32)],
        compiler_params=pltpu.CompilerParams(
            dimension_semantics=("parallel", "arbitrary"), vmem_limit_bytes=VMEM_LIMIT),
        name="prep",
    )(x, w_small, params)


def _each(fn, *lists):
    return [fn(*args) for args in zip(*lists)]


def _unit_lower_inverse(ms, masks):
    eye, blk16, blk32 = masks
    hi = lambda a, b: _dot(a, b, HI)
    n1 = _each(lambda m: -jnp.where(blk16, m, 0.0), ms)
    l1 = _each(lambda m: jnp.where(blk32 & jnp.logical_not(blk16), m, 0.0), ms)
    l2 = _each(lambda m: jnp.where(blk32, 0.0, m), ms)
    n2 = _each(hi, n1, n1)
    yield
    p = _each(lambda a, b: hi(eye + a, eye + b), n1, n2)
    n4 = _each(hi, n2, n2)
    yield
    p = _each(lambda a, b: hi(a, eye + b), p, n4)
    n8 = _each(hi, n4, n4)
    yield
    d_inv = _each(lambda a, b: hi(a, eye + b), p, n8)
    yield
    dl = _each(hi, d_inv, l1)
    yield
    a32 = _each(lambda a, b: a - hi(b, a), d_inv, dl)
    yield
    al = _each(hi, a32, l2)
    yield
    return _each(lambda a, b: a - hi(b, a), a32, al)


def _emit_zipped(main, side):
    live = [main, side]
    while live:
        for gen in list(live):
            try:
                next(gen)
            except StopIteration:
                live.remove(gen)


def _gdn_kernel(q_ref, k_ref, v_ref, z_ref, g_ref, b_ref, cwq_ref, cwk_ref, cwv_ref, nw_ref,
                o_ref, qn, kn, vn, st, sol_s, qk_s, qg_s, kd_s, gl_s):
    s = q_ref.shape[1]
    c = CHUNK
    nh = q_ref.shape[2] // LANES
    row = lax.broadcasted_iota(jnp.int32, (s, LANES), 0)

    def conv_silu(x, w):
        y = x * w[3:4, :]
        for sh in (1, 2, 3):
            xs = jnp.where(row >= sh, pltpu.roll(x, sh, axis=0), 0.0)
            y = y + xs * w[3 - sh:4 - sh, :]
        return y * jax.nn.sigmoid(y)

    def l2norm(x):
        return x * lax.rsqrt(jnp.sum(x * x, axis=-1, keepdims=True) + 1e-6)

    for j in range(nh):
        hs = slice(j * LANES, (j + 1) * LANES)
        qn[:, hs] = l2norm(conv_silu(q_ref[0, :, hs], cwq_ref[:, hs])) * (GDN_DK ** -0.5)
        kn[:, hs] = l2norm(conv_silu(k_ref[0, :, hs], cwk_ref[:, hs]))
        vn[:, hs] = conv_silu(v_ref[0, :, hs], cwv_ref[:, hs])
    st[...] = jnp.zeros_like(st)

    ri = lax.broadcasted_iota(jnp.int32, (c, c), 0)
    ci = lax.broadcasted_iota(jnp.int32, (c, c), 1)
    tril = ri >= ci
    strict = ri > ci
    t_inc = tril.astype(F32)
    eye = (ri == ci).astype(F32)
    blk16 = (ri >> 4) == (ci >> 4)
    blk32 = (ri >> 5) == (ci >> 5)
    l2 = lax.broadcasted_iota(jnp.int32, (c, 2 * LANES), 0)
    j2 = lax.broadcasted_iota(jnp.int32, (c, 2 * LANES), 1)
    ux = jnp.where((j2 >= c) | (l2 > j2), 1.0, 0.0).astype(F32)
    nw = nw_ref[...]

    heads = list(range(nh))
    lanes_of = [slice(j * LANES, (j + 1) * LANES) for j in heads]

    def local_stage(ns):
        pairs = [(i, j) for i in range(len(ns)) for j in heads]
        rows = [pl.ds(pl.multiple_of(n * c, c), c) for n in ns]
        q = [qn[rows[i], lanes_of[j]] for i, j in pairs]
        k = [kn[rows[i], lanes_of[j]] for i, j in pairs]
        v = [vn[rows[i], lanes_of[j]] for i, j in pairs]
        gb = [g_ref[0, rows[i], lanes_of[j]] for i, j in pairs]
        bb = [b_ref[0, rows[i], lanes_of[j]] for i, j in pairs]
        d = _each(lambda g: _dot(t_inc, jnp.concatenate([g, g], axis=1) * ux, HI), gb)
        kb = _each(lambda a, b: a * b, k, bb)
        kk = _each(lambda a, b: _dot_nt(a, b, HI), kb, k)
        qk = _each(lambda a, b: _dot_nt(a.astype(BF16), b.astype(BF16)), q, k)
        yield
        gc = [x[:, LANES:] for x in d]
        decay = [jnp.where(tril, jnp.exp(x[:, :c]), 0.0) for x in d]
        m = _each(lambda a, b: jnp.where(strict, a * b, 0.0), kk, decay)
        a_inv = yield from _unit_lower_inverse(m, (eye, blk16, blk32))
        yield
        egc = _each(jnp.exp, gc)
        rhs = _each(lambda vv, b, kbb, e: jnp.concatenate([vv * b, kbb * e], axis=1), v, bb, kb, egc)
        sol = _each(lambda a, r: _dot(a, r, HI), a_inv, rhs)
        gl = [x[c - 1:c, :] for x in gc]
        yield
        for p, (i, j) in enumerate(pairs):
            n = ns[i]
            qk_s[n, j] = (qk[p] * decay[p]).astype(BF16)
            qg_s[n, j] = (q[p] * egc[p]).astype(BF16)
            kd_s[n, j] = (k[p] * jnp.exp(gl[p] - gc[p])).astype(BF16)
            gl_s[n, j] = gl[p]
            sol_s[n, j] = sol[p]

    def state_stage(ns):
        for n in ns:
            rows = pl.ds(pl.multiple_of(n * c, c), c)
            state = [st[j] for j in heads]
            state_b = [x.astype(BF16) for x in state]
            v_new = [sol_s[n, j, :, :LANES] - _dot(sol_s[n, j, :, LANES:].astype(BF16), state_b[j])
                     for j in heads]
            yield
            v_new_b = [x.astype(BF16) for x in v_new]
            o = [_dot(qg_s[n, j], state_b[j]) + _dot(qk_s[n, j], v_new_b[j]) for j in heads]
            new_state = [state[j] * jnp.exp(gl_s[n, j]) + _dot_tn(kd_s[n, j], v_new_b[j]) for j in heads]
            yield
            for j in heads:
                st[j] = new_state[j]
                on = o[j] * lax.rsqrt(jnp.mean(o[j] * o[j], axis=-1, keepdims=True) + 1e-6) * nw
                z = z_ref[0, rows, lanes_of[j]]
                o_ref[0, rows, lanes_of[j]] = on * (z * jax.nn.sigmoid(z))
            yield

    nchunks = s // c
    group = 4
    ngroups = nchunks // group

    def chunks_of(gidx):
        return [gidx * group + i for i in range(group)]

    def run(gen):
        for _ in gen:
            pass

    run(local_stage(chunks_of(jnp.int32(0))))

    def both(gidx, carry):
        _emit_zipped(local_stage(chunks_of(gidx)), state_stage(chunks_of(gidx - 1)))
        return carry

    lax.fori_loop(1, ngroups, both, 0)
    run(state_stage(chunks_of(jnp.int32(ngroups - 1))))


def _gdn(proj3, gexp, bexp, conv_w, norm_w, cols):
    b, s, _ = proj3.shape
    nh = 2
    nc = s // CHUNK
    wd = nh * LANES
    cq, ck, cv, cz = (c0 // nh for c0 in cols)

    def blk(c0):
        return pl.BlockSpec((1, s, wd), lambda i, h: (i, 0, c0 + h))

    def cw(c0):
        return pl.BlockSpec((conv_w.shape[0], wd), lambda i, h: (0, c0 + h))

    head = pl.BlockSpec((1, s, wd), lambda i, h: (i, 0, h))
    return pl.pallas_call(
        _gdn_kernel,
        grid=(b, GDN_HEADS // nh),
        in_specs=[blk(cq), blk(ck), blk(cv), blk(cz), head, head,
                  cw(0), cw(GDN_HEADS // nh), cw(2 * GDN_HEADS // nh),
                  pl.BlockSpec((1, LANES), lambda i, h: (0, 0))],
        out_specs=head,
        out_shape=jax.ShapeDtypeStruct((b, s, GDN_HEADS * LANES), F32),
        scratch_shapes=[pltpu.VMEM((s, wd), F32), pltpu.VMEM((s, wd), F32),
                        pltpu.VMEM((s, wd), F32), pltpu.VMEM((nh, GDN_DK, LANES), F32),
                        pltpu.VMEM((nc, nh, CHUNK, 2 * LANES), F32), pltpu.VMEM((nc, nh, CHUNK, CHUNK), BF16),
                        pltpu.VMEM((nc, nh, CHUNK, LANES), BF16), pltpu.VMEM((nc, nh, CHUNK, LANES), BF16),
                        pltpu.VMEM((nc, nh, 1, LANES), F32)],
        compiler_params=pltpu.CompilerParams(
            dimension_semantics=("parallel", "parallel"), vmem_limit_bytes=VMEM_LIMIT),
        name="gdn",
    )(proj3, proj3, proj3, proj3, gexp, bexp, conv_w, conv_w, conv_w, norm_w)


def _fox_kernel(q_ref, k_ref, v_ref, c_ref, ct_ref, o_ref, *, tk):
    tq = q_ref.shape[1]
    nj = q_ref.shape[2] // LANES
    g = pl.program_id(1)
    qi = pl.program_id(2)
    q = q_ref[0]
    cblk = c_ref[0]
    lane = lax.broadcasted_iota(jnp.int32, (tq, LANES), 1)
    qpos = qi * tq + lax.broadcasted_iota(jnp.int32, (tq, tk), 0)
    kofs = lax.broadcasted_iota(jnp.int32, (tq, tk), 1)
    heads = list(range(nj))
    lanes_of = [slice(j * LANES, (j + 1) * LANES) for j in heads]
    ccol = [jnp.sum(jnp.where(lane == 8 + g * nj + j, cblk, 0.0), axis=-1, keepdims=True) for j in heads]
    qh = [q[:, hs] * jnp.asarray(FX_DH ** -0.5, BF16) for hs in lanes_of]

    def body(kv, carry):
        k0 = pl.multiple_of(kv * tk, tk)
        causal = qpos >= k0 + kofs
        kblk = k_ref[0, pl.ds(k0, tk), :]
        vblk = v_ref[0, pl.ds(k0, tk), :]
        sc = [_dot_nt(qh[j], kblk[:, lanes_of[j]]) for j in heads]
        crow = [ct_ref[0, pl.ds(g * nj + j, 1), pl.ds(k0, tk)] for j in heads]
        sc = [jnp.where(causal, sc[j] + ccol[j] - crow[j], -1e30) for j in heads]
        m_new = [jnp.maximum(carry[j][0], jnp.max(sc[j], axis=-1, keepdims=True)) for j in heads]
        a = [jnp.exp(carry[j][0] - m_new[j]) for j in heads]
        p = [jnp.exp(sc[j] - m_new[j]) for j in heads]
        l = [a[j] * carry[j][1] + jnp.sum(p[j], axis=-1, keepdims=True) for j in heads]
        acc = [a[j] * carry[j][2] + _dot(p[j].astype(BF16), vblk[:, lanes_of[j]]) for j in heads]
        return tuple((m_new[j], l[j], acc[j]) for j in heads)

    init = tuple((jnp.full((tq, 1), -1e30, F32), jnp.zeros((tq, 1), F32), jnp.zeros((tq, LANES), F32))
                 for _ in heads)
    nkv = (qi * tq + tq - 1) // tk + 1
    res = lax.fori_loop(0, nkv, body, init)
    o_ref[0] = jnp.concatenate([acc / l for _, l, acc in res], axis=1)


def _fox(pf3, c, ct):
    b, s, _ = pf3.shape
    wd = 4 * LANES
    tq = min(128, s)
    tk = min(256, s)
    ngrp = FX_HEADS * LANES // wd
    return pl.pallas_call(
        functools.partial(_fox_kernel, tk=tk),
        grid=(b, ngrp, s // tq),
        in_specs=[pl.BlockSpec((1, tq, wd), lambda i, h, t: (i, t, h)),
                  pl.BlockSpec((1, s, wd), lambda i, h, t: (i, 0, ngrp + h)),
                  pl.BlockSpec((1, s, wd), lambda i, h, t: (i, 0, 2 * ngrp + h)),
                  pl.BlockSpec((1, tq, LANES), lambda i, h, t: (i, t, 0)),
                  pl.BlockSpec((1, 8, s), lambda i, h, t: (i, 0, 0))],
        out_specs=pl.BlockSpec((1, tq, wd), lambda i, h, t: (i, t, h)),
        out_shape=jax.ShapeDtypeStruct((b, s, ngrp * wd), F32),
        compiler_params=pltpu.CompilerParams(
            dimension_semantics=("parallel", "parallel", "parallel"), vmem_limit_bytes=VMEM_LIMIT),
        name="fox",
    )(pf3, pf3, pf3, c, ct)


def _mix_kernel(oa_ref, ob_ref, ga_ref, gb_ref, x_ref, wa_ref, wb_ref, wo_ref, g1_ref, b1_ref, o_ref, *, alpha):
    ya = _dot(oa_ref[...].astype(BF16), wa_ref[...])
    yb = _dot(ob_ref[...].astype(BF16), wb_ref[...])
    mix = jax.nn.sigmoid(ga_ref[...]) * ya + jax.nn.sigmoid(gb_ref[...]) * yb
    z = alpha * x_ref[...] + _dot(mix.astype(BF16), wo_ref[...])
    o_ref[...] = _layernorm(z, g1_ref[...], b1_ref[...])


def _mix(oa, ob, proj, x2, wa, wb, wo, g1, b1, alpha):
    t, d = x2.shape
    tm = min(512, t)
    w, w2 = oa.shape[1], ob.shape[1]
    full = lambda r, c: pl.BlockSpec((r, c), lambda i: (0, 0))
    return pl.pallas_call(
        functools.partial(_mix_kernel, alpha=alpha),
        grid=(t // tm,),
        in_specs=[pl.BlockSpec((tm, w), lambda i: (i, 0)),
                  pl.BlockSpec((tm, w2), lambda i: (i, 0)),
                  pl.BlockSpec((tm, d), lambda i: (i, 0)),
                  pl.BlockSpec((tm, d), lambda i: (i, 1)),
                  pl.BlockSpec((tm, d), lambda i: (i, 0)),
                  full(w, d), full(w2, d), full(d, d), full(1, d), full(1, d)],
        out_specs=pl.BlockSpec((tm, d), lambda i: (i, 0)),
        out_shape=jax.ShapeDtypeStruct((t, d), F32),
        compiler_params=pltpu.CompilerParams(
            dimension_semantics=("parallel",), vmem_limit_bytes=VMEM_LIMIT),
        name="mix",
    )(oa, ob, proj, proj, x2, wa, wb, wo, g1, b1)


def _route_head(keys_ref, q_scr, hd, lh, scr, flat, cmask):
    s_scr, i_scr, best_scr, eh_scr = scr
    nk, kk = PEER_NKEYS, PEER_TOPK
    iota_k = lax.broadcasted_iota(jnp.int32, (nk, LANES), 0)
    neg = jnp.float32(-jnp.inf)
    for p in range(2):
        vals = _dot_nt(keys_ref[hd * 2 + p], q_scr[hd * 2 + p, lh])
        for r in range(kk):
            m = jnp.max(vals, axis=0, keepdims=True)
            am = jnp.min(jnp.where(vals == m, iota_k, nk), axis=0, keepdims=True)
            s_scr[p * kk + r:p * kk + r + 1, :] = m
            i_scr[p * kk + r:p * kk + r + 1, :] = am
            vals = jnp.where(iota_k == am, neg, vals)
    s1 = s_scr[kk:kk + 8, :]
    i1 = i_scr[kk:kk + 8, :]
    cand = [s_scr[0:1, :] + s_scr[kk:2 * kk, :]]
    cidx = [i_scr[0:1, :] * nk + i_scr[kk:2 * kk, :]]
    for a in range(1, 8):
        cand.append(s_scr[a:a + 1, :] + s1)
        cidx.append(i_scr[a:a + 1, :] * nk + i1)
    cand.append(s_scr[8:kk, :] + s_scr[kk:kk + 1, :])
    cidx.append(i_scr[8:kk, :] * nk + i_scr[kk:kk + 1, :])
    vals = jnp.concatenate(cand, axis=0) + cmask
    cidx = jnp.concatenate(cidx, axis=0)
    for r in range(kk):
        m = jnp.max(vals, axis=0, keepdims=True)
        am = jnp.min(jnp.where(vals == m, flat, 2 * kk * kk), axis=0, keepdims=True)
        sel = flat == am
        best_scr[r:r + 1, :] = m
        eh_scr[r:r + 1, :] = jnp.max(jnp.where(sel, cidx, -1), axis=0, keepdims=True)
        vals = jnp.where(sel, neg, vals)
    bs = best_scr[...]
    ex = jnp.exp(bs - bs[0:1, :])
    return ex / jnp.sum(ex, axis=0, keepdims=True), eh_scr[...]


def _candidate_tables():
    kk = PEER_TOPK
    pairs = [(0, bb) for bb in range(kk)]
    for a in range(1, 8):
        pairs += [(a, bb) for bb in range(8)]
    pairs += [(a, 0) for a in range(8, kk)]
    real = [(a + 1) * (bb + 1) <= kk for a, bb in pairs]
    flat = [a * kk + bb if ok else kk * kk + r for r, ((a, bb), ok) in enumerate(zip(pairs, real))]
    flat = jnp.broadcast_to(jnp.asarray(flat, jnp.int32)[:, None], (len(pairs), LANES))
    cmask = jnp.broadcast_to(jnp.asarray([0.0 if ok else -jnp.inf for ok in real], F32)[:, None],
                             (len(pairs), LANES))
    return flat, cmask


def _route_kernel(h_ref, wq_ref, keys_ref, flat_ref, cmask_ref, e_ref, g_ref,
                  q_scr, s_scr, i_scr, best_scr, eh_scr, eall_scr, gall_scr):
    tg = h_ref.shape[0]
    kk = PEER_TOPK
    nlh = tg // LANES
    q = _dot(h_ref[...].astype(BF16), wq_ref[...])
    for j in range(2 * PEER_HEADS):
        for lh in range(nlh):
            q_scr[j, lh] = q[lh * LANES:(lh + 1) * LANES, j * LANES:(j + 1) * LANES].astype(BF16)
    flat = flat_ref[...]
    cmask = cmask_ref[...]

    def piece(p, carry):
        hd = p // nlh
        lh = p % nlh
        gates, ids = _route_head(keys_ref, q_scr, hd, lh, (s_scr, i_scr, best_scr, eh_scr), flat, cmask)
        r0 = pl.multiple_of(hd * kk, kk)
        gall_scr[lh, pl.ds(r0, kk), :] = gates
        eall_scr[lh, pl.ds(r0, kk), :] = ids
        return carry

    lax.fori_loop(0, PEER_HEADS * nlh, piece, 0)
    for lh in range(nlh):
        e_ref[lh * LANES:(lh + 1) * LANES, :] = eall_scr[lh].T
        g_ref[lh * LANES:(lh + 1) * LANES, :] = gall_scr[lh].T


def _route(h, wq, keys):
    t, d = h.shape
    tg = 256
    nq = wq.shape[1]
    kk = PEER_TOPK
    hk = PEER_HEADS * kk
    nlh = tg // LANES
    flat, cmask = _candidate_tables()
    const = lambda shape: pl.BlockSpec(shape, lambda i: (0,) * len(shape))
    return pl.pallas_call(
        _route_kernel,
        grid=(t // tg,),
        in_specs=[pl.BlockSpec((tg, d), lambda i: (i, 0)),
                  const((d, nq)), const(keys.shape), const(flat.shape), const(cmask.shape)],
        out_specs=[pl.BlockSpec((tg, hk), lambda i: (i, 0)), pl.BlockSpec((tg, hk), lambda i: (i, 0))],
        out_shape=[jax.ShapeDtypeStruct((t, hk), jnp.int32), jax.ShapeDtypeStruct((t, hk), F32)],
        scratch_shapes=[pltpu.VMEM((2 * PEER_HEADS, nlh, LANES, LANES), BF16),
                        pltpu.VMEM((2 * kk, LANES), F32), pltpu.VMEM((2 * kk, LANES), jnp.int32),
                        pltpu.VMEM((kk, LANES), F32), pltpu.VMEM((kk, LANES), jnp.int32),
                        pltpu.VMEM((nlh, hk, LANES), jnp.int32), pltpu.VMEM((nlh, hk, LANES), F32)],
        compiler_params=pltpu.CompilerParams(
            dimension_semantics=("parallel",), vmem_limit_bytes=VMEM_LIMIT),
        name="route",
    )(h, wq, keys, flat, cmask)


PEER_SC_SHARE_PCT = 44
PEER_TC_FIRST_PCT = 53
SC_WORKERS = 32
SC_LANES = 16
SC_CHUNK = 32
SC_DBLK = 16
SC_EBLK = 4
SC_JUNROLL = 8


def _sc_mesh():
    return plsc.VectorSubcoreMesh(core_axis_name="c", subcore_axis_name="s")


def _sc_worker():
    return lax.axis_index("s") * 2 + lax.axis_index("c")


def _sc_pre(ids, h, u_tab):
    ts, hk = ids.shape
    d = h.shape[1]
    per_w = ts // SC_WORKERS
    nch = hk // SC_CHUNK

    def body(ids_hbm, h_hbm, u_hbm, out_hbm, idx_v, h_v, rows_a, rows_b, pre_v, sem_a, sem_b):
        base = _sc_worker() * per_w
        lane = lax.iota(jnp.int32, SC_LANES)
        bufs = (rows_a, rows_b)
        sems = (sem_a, sem_b)

        def gather(c):
            return pltpu.async_copy(u_hbm.at[idx_v.at[pl.ds(c * SC_CHUNK, SC_CHUNK)]], bufs[c % 2], sems[c % 2])

        def token(i, carry):
            t = base + i
            pltpu.sync_copy(ids_hbm.at[t], idx_v)
            pltpu.sync_copy(h_hbm.at[t], h_v)
            cps = {0: gather(0)}
            for c in range(nch):
                if c + 1 < nch:
                    cps[c + 1] = gather(c + 1)
                cps[c].wait()
                rows = bufs[c % 2]
                for g in range(SC_CHUNK // SC_LANES):
                    def experts(q, vec, g=g, rows=rows):
                        e0 = g * SC_LANES + q * SC_EBLK

                        def span(jb, accs, rows=rows, e0=e0):
                            accs = list(accs)
                            for jj in range(SC_JUNROLL):
                                sl = pl.ds((jb * SC_JUNROLL + jj) * SC_LANES, SC_LANES)
                                hv = h_v[sl]
                                for i in range(SC_EBLK):
                                    accs[i] = accs[i] + rows[e0 + i, sl] * hv
                            return tuple(accs)
                        accs = lax.fori_loop(0, d // (SC_LANES * SC_JUNROLL), span,
                                             tuple(jnp.zeros((SC_LANES,), F32) for _ in range(SC_EBLK)))
                        for i in range(SC_EBLK):
                            vec = jnp.where(lane == q * SC_EBLK + i, jnp.sum(accs[i]), vec)
                        return vec
                    vec = lax.fori_loop(0, SC_LANES // SC_EBLK, experts, jnp.zeros((SC_LANES,), F32))
                    pre_v[pl.ds(c * SC_CHUNK + g * SC_LANES, SC_LANES)] = vec
            pltpu.sync_copy(pre_v, out_hbm.at[t])
            return carry

        lax.fori_loop(0, per_w, token, 0)

    return pl.kernel(
        body, out_type=jax.ShapeDtypeStruct((ts, hk), F32), mesh=_sc_mesh(),
        scratch_types=[pltpu.VMEM((hk,), jnp.int32), pltpu.VMEM((d,), F32),
                       pltpu.VMEM((SC_CHUNK, d), F32), pltpu.VMEM((SC_CHUNK, d), F32),
                       pltpu.VMEM((hk,), F32), pltpu.SemaphoreType.DMA, pltpu.SemaphoreType.DMA],
        compiler_params=pltpu.CompilerParams(needs_layout_passes=False),
        name="sc_pre",
    )(ids, h, u_tab)


def _sc_out(ids, act, v_tab):
    ts, hk = ids.shape
    d = v_tab.shape[1]
    per_w = ts // SC_WORKERS
    nch = hk // SC_CHUNK

    def body(ids_hbm, act_hbm, v_hbm, out_hbm, idx_v, act_v, rows_a, rows_b, y_v, sem_a, sem_b):
        base = _sc_worker() * per_w
        lane = lax.iota(jnp.int32, SC_LANES)
        bufs = (rows_a, rows_b)
        sems = (sem_a, sem_b)

        def gather(c):
            return pltpu.async_copy(v_hbm.at[idx_v.at[pl.ds(c * SC_CHUNK, SC_CHUNK)]], bufs[c % 2], sems[c % 2])

        def token(i, carry):
            t = base + i
            pltpu.sync_copy(ids_hbm.at[t], idx_v)
            pltpu.sync_copy(act_hbm.at[t], act_v)
            for j in range(d // SC_LANES):
                y_v[pl.ds(j * SC_LANES, SC_LANES)] = jnp.zeros((SC_LANES,), F32)
            cps = {0: gather(0)}
            for c in range(nch):
                if c + 1 < nch:
                    cps[c + 1] = gather(c + 1)
                cps[c].wait()
                rows = bufs[c % 2]
                for db in range(d // (SC_DBLK * SC_LANES)):
                    def expert(r, accs, c=c, rows=rows, db=db):
                        a = plsc.load_gather(act_v, [jnp.zeros((SC_LANES,), jnp.int32) + (c * SC_CHUNK + r)])
                        return tuple(
                            accs[j] + a * rows[r, pl.ds((db * SC_DBLK + j) * SC_LANES, SC_LANES)]
                            for j in range(SC_DBLK))
                    accs = lax.fori_loop(0, SC_CHUNK, expert,
                                         tuple(jnp.zeros((SC_LANES,), F32) for _ in range(SC_DBLK)))
                    for j in range(SC_DBLK):
                        plsc.addupdate(y_v.at[pl.ds((db * SC_DBLK + j) * SC_LANES, SC_LANES)], accs[j])
            pltpu.sync_copy(y_v, out_hbm.at[t])
            return carry

        lax.fori_loop(0, per_w, token, 0)

    return pl.kernel(
        body, out_type=jax.ShapeDtypeStruct((ts, d), F32), mesh=_sc_mesh(),
        scratch_types=[pltpu.VMEM((hk,), jnp.int32), pltpu.VMEM((hk,), F32),
                       pltpu.VMEM((SC_CHUNK, d), F32), pltpu.VMEM((SC_CHUNK, d), F32),
                       pltpu.VMEM((d,), F32), pltpu.SemaphoreType.DMA, pltpu.SemaphoreType.DMA],
        compiler_params=pltpu.CompilerParams(needs_layout_passes=False),
        name="sc_out",
    )(ids, act, v_tab)


def _act_kernel(pre_ref, gate_ref, after_ref, o_ref):
    del after_ref
    pre = pre_ref[...]
    o_ref[...] = 0.5 * pre * (1.0 + lax.erf(pre * (2.0 ** -0.5))) * gate_ref[...]


def _act(pre, gate, after):
    t, hk = pre.shape
    tm = 256
    assert t % tm == 0
    spec = pl.BlockSpec((tm, hk), lambda i: (i, 0))
    return pl.pallas_call(
        _act_kernel, grid=(t // tm,),
        in_specs=[spec, spec, pl.BlockSpec((8, LANES), lambda i: (0, 0))], out_specs=spec,
        out_shape=jax.ShapeDtypeStruct((t, hk), F32),
        compiler_params=pltpu.CompilerParams(dimension_semantics=("parallel",)),
        name="peer_act",
    )(pre, gate, after)


def _ln2_kernel(h_ref, y_ref, g_ref, b_ref, o_ref, *, alpha):
    o_ref[...] = _layernorm(alpha * h_ref[...] + y_ref[...], g_ref[...], b_ref[...])


def _ln2(h, y, g2, b2, alpha):
    t, d = h.shape
    tm = 256
    assert t % tm == 0
    spec = pl.BlockSpec((tm, d), lambda i: (i, 0))
    vec = pl.BlockSpec((1, d), lambda i: (0, 0))
    return pl.pallas_call(
        functools.partial(_ln2_kernel, alpha=alpha), grid=(t // tm,),
        in_specs=[spec, spec, vec, vec], out_specs=spec,
        out_shape=jax.ShapeDtypeStruct((t, d), F32),
        compiler_params=pltpu.CompilerParams(dimension_semantics=("parallel",)),
        name="peer_ln2",
    )(h, y, g2, b2)


def _peer_kernel(h_cur_ref, h_nxt_ref, wq_ref, keys_ref, flat_ref, cmask_ref, uv_hbm, g2_ref, b2_ref, o_ref,
                 q_scr, s_scr, i_scr, best_scr, eh_scr, eall_scr, idv_scr, ids_smem, gate_scr,
                 buf_a, buf_b, sem, idsem, y_scr, *, tt, alpha):
    s = pl.program_id(0)
    tg, d = h_cur_ref.shape
    nk, kk = PEER_NKEYS, PEER_TOPK
    hk = PEER_HEADS * kk
    nlh = tg // LANES
    nsub = tg // (2 * tt)
    assert nsub == PEER_HEADS * nlh
    nslab, sub = uv_hbm.shape[1], uv_hbm.shape[2]
    half = sub // 2
    bufs = (buf_a, buf_b)
    last = pl.num_programs(0) - 1
    rslot = s % 3
    pslot = (s + 2) % 3
    eslot = (s + 1) % 3

    def issue(idslot, row0, slot):
        for t in range(tt):
            for k in range(hk):
                e = ids_smem[idslot, row0 + t, k]
                pltpu.make_async_copy(uv_hbm.at[e], bufs[slot].at[:, pl.ds((t * hk + k) * sub, sub), :],
                                      sem.at[slot]).start(priority=k % 2)

    def wait(slot):
        pltpu.make_async_copy(bufs[slot], bufs[slot], sem.at[slot]).wait()

    q = _dot(h_nxt_ref[...].astype(BF16), wq_ref[...])
    for j in range(2 * PEER_HEADS):
        for lh in range(nlh):
            q_scr[j, lh] = q[lh * LANES:(lh + 1) * LANES, j * LANES:(j + 1) * LANES].astype(BF16)
    flat = flat_ref[...]
    cmask = cmask_ref[...]

    def route_piece(piece):
        hd = piece // nlh
        lh = piece % nlh
        gates, ids = _route_head(keys_ref, q_scr, hd, lh, (s_scr, i_scr, best_scr, eh_scr), flat, cmask)
        r0 = pl.multiple_of(hd * kk, kk)
        gate_scr[rslot, lh, pl.ds(r0, kk), :] = gates
        eall_scr[lh, pl.ds(r0, kk), :] = ids

    lane = lax.broadcasted_iota(jnp.int32, (hk, LANES), 1)

    def compute(row0, slot):
        buf = bufs[slot]
        gt = gate_scr[eslot, row0 // LANES]
        lane0 = row0 % LANES
        for t in range(tt):
            hrow = h_cur_ref[pl.ds(row0 + t, 1), :]

            def rows(c, j):
                return buf[c, pl.ds(t * hk * sub + j, hk, stride=sub), :]

            part = None
            for j in range(half):
                for c in range(nslab):
                    seg = j * nslab + c
                    term = rows(c, j) * hrow[:, seg * LANES:(seg + 1) * LANES]
                    part = term if part is None else part + term
            pre = jnp.sum(part, axis=-1, keepdims=True)
            gate = jnp.sum(jnp.where(lane == lane0 + t, gt, 0.0), axis=-1, keepdims=True)
            act = 0.5 * pre * (1.0 + lax.erf(pre * (2.0 ** -0.5))) * gate
            yrow = slot * tt + t
            for j in range(half):
                for c in range(nslab):
                    seg = j * nslab + c
                    y_scr[yrow:yrow + 1, seg * LANES:(seg + 1) * LANES] = jnp.sum(
                        act * rows(c, half + j), axis=0, keepdims=True)

    def substep(j, carry):
        row0 = pl.multiple_of(j * 2 * tt, 2 * tt)
        issue(eslot, row0 + tt, 1)
        route_piece(j)
        wait(0)
        compute(row0, 0)
        wait(1)
        wrap = j == nsub - 1
        issue(jnp.where(wrap, pslot, eslot), jnp.where(wrap, 0, row0 + 2 * tt), 0)
        compute(row0 + tt, 1)
        z = alpha * h_cur_ref[pl.ds(row0, 2 * tt), :] + y_scr[...]
        o_ref[pl.ds(row0, 2 * tt), :] = _layernorm(z, g2_ref[...], b2_ref[...])
        return carry

    @pl.when(s < 2)
    def _():
        o_ref[...] = jnp.zeros_like(o_ref)

        def piece(p, carry):
            route_piece(p)
            return carry
        lax.fori_loop(0, nsub, piece, 0)

    @pl.when(s == 1)
    def _():
        issue(pslot, 0, 0)

    @pl.when(s >= 2)
    def _():
        lax.fori_loop(0, nsub, substep, 0)

    @pl.when(s == last)
    def _():
        wait(0)

    for lh in range(nlh):
        idv_scr[lh * LANES:(lh + 1) * LANES, :] = eall_scr[lh].T
    publish = pltpu.make_async_copy(idv_scr, ids_smem.at[rslot], idsem)
    publish.start()
    publish.wait()


def _peer(h1, wq, keys, uv_tab, g2, b2, alpha):
    t, d = h1.shape
    tg = 256
    tt = 8
    ngrp = t // tg
    nq = wq.shape[1]
    kk = PEER_TOPK
    hk = PEER_HEADS * kk
    nlh = tg // LANES
    nslab, sub = uv_tab.shape[1], uv_tab.shape[2]
    flat, cmask = _candidate_tables()
    const = lambda shape: pl.BlockSpec(shape, lambda i: (0,) * len(shape))
    return pl.pallas_call(
        functools.partial(_peer_kernel, tt=tt, alpha=alpha),
        grid=(ngrp + 2,),
        in_specs=[pl.BlockSpec((tg, d), lambda i: (jnp.maximum(i - 2, 0), 0)),
                  pl.BlockSpec((tg, d), lambda i: (jnp.minimum(i, ngrp - 1), 0)),
                  const((d, nq)), const(keys.shape), const(flat.shape), const(cmask.shape),
                  pl.BlockSpec(memory_space=pl.ANY),
                  const((1, d)), const((1, d))],
        out_specs=pl.BlockSpec((tg, d), lambda i: (jnp.maximum(i - 2, 0), 0)),
        out_shape=jax.ShapeDtypeStruct((t, d), F32),
        scratch_shapes=[pltpu.VMEM((2 * PEER_HEADS, nlh, LANES, LANES), BF16),
                        pltpu.VMEM((2 * kk, LANES), F32), pltpu.VMEM((2 * kk, LANES), jnp.int32),
                        pltpu.VMEM((kk, LANES), F32), pltpu.VMEM((kk, LANES), jnp.int32),
                        pltpu.VMEM((nlh, hk, LANES), jnp.int32),
                        pltpu.VMEM((tg, hk), jnp.int32),
                        pltpu.SMEM((3, tg, hk), jnp.int32),
                        pltpu.VMEM((3, nlh, hk, LANES), F32),
                        pltpu.VMEM((nslab, tt * hk * sub, LANES), F32),
                        pltpu.VMEM((nslab, tt * hk * sub, LANES), F32),
                        pltpu.SemaphoreType.DMA((2,)), pltpu.SemaphoreType.DMA,
                        pltpu.VMEM((2 * tt, d), F32)],
        compiler_params=pltpu.CompilerParams(
            dimension_semantics=("arbitrary",), vmem_limit_bytes=VMEM_LIMIT),
        name="peer",
    )(h1, h1, wq, keys, flat, cmask, uv_tab, g2, b2)


def _layer(h, w_in, conv_w, a_log, dt_bias, norm_w, f_bias, w_out_gdn, w_out_fox, w_o, ln1_g, ln1_b,
           peer_wq, peer_keys, peer_u, peer_v, ln2_g, ln2_b, alpha):
    b, s, d = h.shape
    t = b * s
    qk = GDN_HEADS * GDN_DK
    fw = FX_HEADS * FX_DH
    o_gz = 4 * qk
    o_ga = o_gz
    o_fq = o_ga + 2 * GDN_HEADS
    o_ff = o_fq + 3 * fw
    o_gate = o_ff + FX_HEADS
    w_big = jnp.concatenate([w_in[:, o_gate:], w_in[:, :o_gz]], axis=1).astype(BF16)
    pad = LANES - FX_DH
    w_fox = jnp.pad(w_in[:, o_fq:o_ff].reshape(d, 3 * FX_HEADS, FX_DH), ((0, 0), (0, 0), (0, pad)))
    w_fox = w_fox.reshape(d, 3 * FX_HEADS * LANES).astype(BF16)
    w_out_fox_p = jnp.pad(w_out_fox.reshape(FX_HEADS, FX_DH, d), ((0, 0), (0, pad), (0, 0)))
    w_out_fox_p = w_out_fox_p.reshape(FX_HEADS * LANES, d).astype(BF16)
    n_small = 2 * GDN_HEADS + FX_HEADS
    w_small = jnp.concatenate([w_in[:, o_ga:o_fq], w_in[:, o_ff:o_gate],
                               jnp.zeros((d, LANES - n_small), F32)], axis=1)
    params = jnp.zeros((8, LANES), F32)
    params = params.at[0, :GDN_HEADS].set(a_log).at[1, :GDN_HEADS].set(dt_bias)
    params = params.at[2, 2 * GDN_HEADS:n_small].set(f_bias)

    x2 = h.reshape(t, d)
    proj = _in_proj(x2, w_big, F32, "in_proj")
    pf = _in_proj(x2, w_fox, BF16, "in_proj_fox")
    gexp, bexp, c, ct = _prep(h, w_small, params)
    proj3 = proj.reshape(b, s, proj.shape[1])
    gdn0 = 2 * d // LANES
    oa = _gdn(proj3, gexp, bexp, conv_w, norm_w.reshape(1, LANES),
              (gdn0, gdn0 + GDN_HEADS, gdn0 + 2 * GDN_HEADS, gdn0 + 3 * GDN_HEADS))
    ob = _fox(pf.reshape(b, s, pf.shape[1]), c, ct)
    h1 = _mix(oa.reshape(t, qk), ob.reshape(t, FX_HEADS * LANES), proj, x2,
              w_out_gdn.astype(BF16), w_out_fox_p, w_o.astype(BF16),
              ln1_g.reshape(1, d), ln1_b.reshape(1, d), alpha)
    keys = peer_keys.reshape(2 * PEER_HEADS, PEER_NKEYS, peer_keys.shape[-1]).astype(BF16)
    ne = peer_u.shape[0]
    half = PEER_SUB // 2
    nslab = d // (half * LANES)
    uv_tab = jnp.concatenate([peer_u.reshape(ne, half, nslab, LANES),
                              peer_v.reshape(ne, half, nslab, LANES)], axis=1).transpose(0, 2, 1, 3)
    wq = peer_wq.astype(BF16)
    g2, b2 = ln2_g.reshape(1, d), ln2_b.reshape(1, d)
    t_sc = (t * PEER_SC_SHARE_PCT // 100) // 256 * 256
    t_tc = t - t_sc
    if t_sc:
        h_sc = h1[t_tc:]
        ids_sc, gate_sc = _route(h_sc, wq, keys)
        t_a = (t_tc * PEER_TC_FIRST_PCT // 100) // 256 * 256
        pre = _sc_pre(ids_sc, h_sc, peer_u)
        out_a = _peer(h1[:t_a], wq, keys, uv_tab, g2, b2, alpha)
        y_sc = _sc_out(ids_sc, _act(pre, gate_sc, out_a), peer_v)
        out_b = _peer(h1[t_a:t_tc], wq, keys, uv_tab, g2, b2, alpha)
        out = jnp.concatenate([out_a, out_b, _ln2(h_sc, y_sc, g2, b2, alpha)], axis=0)
    else:
        out = _peer(h1, wq, keys, uv_tab, g2, b2, alpha)
    return out.reshape(b, s, d)


def kernel(x, w_in, gdn_conv_w, gdn_a_log, gdn_dt_bias, gdn_norm_w, fox_f_bias, w_out_gdn, w_out_fox, w_o,
           ln1_g, ln1_b, peer_wq, peer_keys, peer_u, peer_v, ln2_g, ln2_b):
    depth = w_in.shape[0]
    alpha = (2.0 * depth) ** 0.25
    h = x
    for l in range(depth):
        h = _layer(h, w_in[l], gdn_conv_w[l], gdn_a_log[l], gdn_dt_bias[l], gdn_norm_w[l], fox_f_bias[l],
                   w_out_gdn[l], w_out_fox[l], w_o[l], ln1_g[l], ln1_b[l], peer_wq[l], peer_keys[l],
                   peer_u[l], peer_v[l], ln2_g[l], ln2_b[l], alpha)
    return h
```

```python
import functools

import jax
import jax.numpy as jnp
from jax import lax
from jax.experimental import pallas as pl
from jax.experimental.pallas import tpu as pltpu
from jax.experimental.pallas import tpu_sc as plsc

F32 = jnp.float32
BF16 = jnp.bfloat16
HI = lax.Precision.HIGHEST

LANES = 128
CHUNK = 64
GDN_HEADS = 4
GDN_DK = 128
FX_HEADS = 8
FX_DH = 64
PEER_HEADS = 8
PEER_NKEYS = 128
PEER_TOPK = 16
PEER_SUB = 4
LN_EPS = 1e-5
VMEM_LIMIT = 48 * 1024 * 1024


def _dot(a, b, prec=None):
    return jnp.dot(a, b, preferred_element_type=F32, precision=prec)


def _dot_nt(a, b, prec=None):
    return lax.dot_general(a, b, (((1,), (1,)), ((), ())), preferred_element_type=F32, precision=prec)


def _dot_tn(a, b, prec=None):
    return lax.dot_general(a, b, (((0,), (0,)), ((), ())), preferred_element_type=F32, precision=prec)


def _softplus(x):
    return jnp.maximum(x, 0.0) + jnp.log1p(jnp.exp(-jnp.abs(x)))


def _layernorm(z, g, b):
    mu = jnp.mean(z, axis=-1, keepdims=True)
    zc = z - mu
    var = jnp.mean(zc * zc, axis=-1, keepdims=True)
    return zc * lax.rsqrt(var + LN_EPS) * g + b


def _mm_kernel(x_ref, w_ref, o_ref):
    o_ref[...] = _dot(x_ref[...].astype(BF16), w_ref[...]).astype(o_ref.dtype)


def _in_proj(x2, w_big, out_dtype, name):
    t, d = x2.shape
    n = w_big.shape[1]
    tm = min(1024, t)
    tn = 512
    return pl.pallas_call(
        _mm_kernel,
        grid=(t // tm, n // tn),
        in_specs=[pl.BlockSpec((tm, d), lambda i, j: (i, 0)),
                  pl.BlockSpec((d, tn), lambda i, j: (0, j))],
        out_specs=pl.BlockSpec((tm, tn), lambda i, j: (i, j)),
        out_shape=jax.ShapeDtypeStruct((t, n), out_dtype),
        compiler_params=pltpu.CompilerParams(
            dimension_semantics=("parallel", "parallel"), vmem_limit_bytes=VMEM_LIMIT),
        name=name,
    )(x2, w_big)


def _prep_kernel(x_ref, w_ref, par_ref, gexp_ref, bexp_ref, c_ref, ct_ref, carry_ref):
    ts = x_ref.shape[1]

    @pl.when(pl.program_id(1) == 0)
    def _():
        carry_ref[...] = jnp.zeros_like(carry_ref)

    small = _dot(x_ref[0], w_ref[...], HI)
    a_log = par_ref[0:1, :]
    dt_bias = par_ref[1:2, :]
    f_bias = par_ref[2:3, :]
    g = -jnp.exp(a_log) * _softplus(small + dt_bias)
    beta = jax.nn.sigmoid(small)
    lane = lax.broadcasted_iota(jnp.int32, (ts, LANES), 1)
    log_f = jnp.where((lane >= 8) & (lane < 16), -_softplus(-(small + f_bias)), 0.0)
    row = lax.broadcasted_iota(jnp.int32, (ts, ts), 0)
    col = lax.broadcasted_iota(jnp.int32, (ts, ts), 1)
    tril = (row >= col).astype(F32)
    c = _dot(tril, log_f, HI) + carry_ref[...]
    carry_ref[...] = c[ts - 1:ts, :]
    c_ref[0] = c
    ct_ref[0] = c.T[8:16, :]
    gexp_ref[0] = jnp.concatenate(
        [jnp.broadcast_to(g[:, h:h + 1], (ts, LANES)) for h in range(GDN_HEADS)], axis=1)
    bexp_ref[0] = jnp.concatenate(
        [jnp.broadcast_to(beta[:, GDN_HEADS + h:GDN_HEADS + h + 1], (ts, LANES)) for h in range(GDN_HEADS)], axis=1)


def _prep(x, w_small, params):
    b, s, d = x.shape
    ts = min(512, s)
    hw = GDN_HEADS * LANES
    return pl.pallas_call(
        _prep_kernel,
        grid=(b, s // ts),
        in_specs=[pl.BlockSpec((1, ts, d), lambda i, j: (i, j, 0)),
                  pl.BlockSpec((d, LANES), lambda i, j: (0, 0)),
                  pl.BlockSpec((8, LANES), lambda i, j: (0, 0))],
        out_specs=[pl.BlockSpec((1, ts, hw), lambda i, j: (i, j, 0)),
                   pl.BlockSpec((1, ts, hw), lambda i, j: (i, j, 0)),
                   pl.BlockSpec((1, ts, LANES), lambda i, j: (i, j, 0)),
                   pl.BlockSpec((1, 8, ts), lambda i, j: (i, 0, j))],
        out_shape=[jax.ShapeDtypeStruct((b, s, hw), F32),
                   jax.ShapeDtypeStruct((b, s, hw), F32),
                   jax.ShapeDtypeStruct((b, s, LANES), F32),
                   jax.ShapeDtypeStruct((b, 8, s), F32)],
        scratch_shapes=[pltpu.VMEM((1, LANES), F32)],
        compiler_params=pltpu.CompilerParams(
            dimension_semantics=("parallel", "arbitrary"), vmem_limit_bytes=VMEM_LIMIT),
        name="prep",
    )(x, w_small, params)


def _each(fn, *lists):
    return [fn(*args) for args in zip(*lists)]


def _unit_lower_inverse(ms, masks):
    eye, blk16, blk32 = masks
    hi = lambda a, b: _dot(a, b, HI)
    n1 = _each(lambda m: -jnp.where(blk16, m, 0.0), ms)
    l1 = _each(lambda m: jnp.where(blk32 & jnp.logical_not(blk16), m, 0.0), ms)
    l2 = _each(lambda m: jnp.where(blk32, 0.0, m), ms)
    n2 = _each(hi, n1, n1)
    yield
    p = _each(lambda a, b: hi(eye + a, eye + b), n1, n2)
    n4 = _each(hi, n2, n2)
    yield
    p = _each(lambda a, b: hi(a, eye + b), p, n4)
    n8 = _each(hi, n4, n4)
    yield
    d_inv = _each(lambda a, b: hi(a, eye + b), p, n8)
    yield
    dl = _each(hi, d_inv, l1)
    yield
    a32 = _each(lambda a, b: a - hi(b, a), d_inv, dl)
    yield
    al = _each(hi, a32, l2)
    yield
    return _each(lambda a, b: a - hi(b, a), a32, al)


def _emit_zipped(main, side):
    live = [main, side]
    while live:
        for gen in list(live):
            try:
                next(gen)
            except StopIteration:
                live.remove(gen)


def _gdn_kernel(q_ref, k_ref, v_ref, z_ref, g_ref, b_ref, cwq_ref, cwk_ref, cwv_ref, nw_ref,
                o_ref, qn, kn, vn, st, sol_s, qk_s, qg_s, kd_s, gl_s):
    s = q_ref.shape[1]
    c = CHUNK
    nh = q_ref.shape[2] // LANES
    row = lax.broadcasted_iota(jnp.int32, (s, LANES), 0)

    def conv_silu(x, w):
        y = x * w[3:4, :]
        for sh in (1, 2, 3):
            xs = jnp.where(row >= sh, pltpu.roll(x, sh, axis=0), 0.0)
            y = y + xs * w[3 - sh:4 - sh, :]
        return y * jax.nn.sigmoid(y)

    def l2norm(x):
        return x * lax.rsqrt(jnp.sum(x * x, axis=-1, keepdims=True) + 1e-6)

    for j in range(nh):
        hs = slice(j * LANES, (j + 1) * LANES)
        qn[:, hs] = l2norm(conv_silu(q_ref[0, :, hs], cwq_ref[:, hs])) * (GDN_DK ** -0.5)
        kn[:, hs] = l2norm(conv_silu(k_ref[0, :, hs], cwk_ref[:, hs]))
        vn[:, hs] = conv_silu(v_ref[0, :, hs], cwv_ref[:, hs])
    st[...] = jnp.zeros_like(st)

    ri = lax.broadcasted_iota(jnp.int32, (c, c), 0)
    ci = lax.broadcasted_iota(jnp.int32, (c, c), 1)
    tril = ri >= ci
    strict = ri > ci
    t_inc = tril.astype(F32)
    eye = (ri == ci).astype(F32)
    blk16 = (ri >> 4) == (ci >> 4)
    blk32 = (ri >> 5) == (ci >> 5)
    l2 = lax.broadcasted_iota(jnp.int32, (c, 2 * LANES), 0)
    j2 = lax.broadcasted_iota(jnp.int32, (c, 2 * LANES), 1)
    ux = jnp.where((j2 >= c) | (l2 > j2), 1.0, 0.0).astype(F32)
    nw = nw_ref[...]

    heads = list(range(nh))
    lanes_of = [slice(j * LANES, (j + 1) * LANES) for j in heads]

    def local_stage(ns):
        pairs = [(i, j) for i in range(len(ns)) for j in heads]
        rows = [pl.ds(pl.multiple_of(n * c, c), c) for n in ns]
        q = [qn[rows[i], lanes_of[j]] for i, j in pairs]
        k = [kn[rows[i], lanes_of[j]] for i, j in pairs]
        v = [vn[rows[i], lanes_of[j]] for i, j in pairs]
        gb = [g_ref[0, rows[i], lanes_of[j]] for i, j in pairs]
        bb = [b_ref[0, rows[i], lanes_of[j]] for i, j in pairs]
        d = _each(lambda g: _dot(t_inc, jnp.concatenate([g, g], axis=1) * ux, HI), gb)
        kb = _each(lambda a, b: a * b, k, bb)
        kk = _each(lambda a, b: _dot_nt(a, b, HI), kb, k)
        qk = _each(lambda a, b: _dot_nt(a.astype(BF16), b.astype(BF16)), q, k)
        yield
        gc = [x[:, LANES:] for x in d]
        decay = [jnp.where(tril, jnp.exp(x[:, :c]), 0.0) for x in d]
        m = _each(lambda a, b: jnp.where(strict, a * b, 0.0), kk, decay)
        a_inv = yield from _unit_lower_inverse(m, (eye, blk16, blk32))
        yield
        egc = _each(jnp.exp, gc)
        rhs = _each(lambda vv, b, kbb, e: jnp.concatenate([vv * b, kbb * e], axis=1), v, bb, kb, egc)
        sol = _each(lambda a, r: _dot(a, r, HI), a_inv, rhs)
        gl = [x[c - 1:c, :] for x in gc]
        yield
        for p, (i, j) in enumerate(pairs):
            n = ns[i]
            qk_s[n, j] = (qk[p] * decay[p]).astype(BF16)
            qg_s[n, j] = (q[p] * egc[p]).astype(BF16)
            kd_s[n, j] = (k[p] * jnp.exp(gl[p] - gc[p])).astype(BF16)
            gl_s[n, j] = gl[p]
            sol_s[n, j] = sol[p]

    def state_stage(ns):
        for n in ns:
            rows = pl.ds(pl.multiple_of(n * c, c), c)
            state = [st[j] for j in heads]
            state_b = [x.astype(BF16) for x in state]
            v_new = [sol_s[n, j, :, :LANES] - _dot(sol_s[n, j, :, LANES:].astype(BF16), state_b[j])
                     for j in heads]
            yield
            v_new_b = [x.astype(BF16) for x in v_new]
            o = [_dot(qg_s[n, j], state_b[j]) + _dot(qk_s[n, j], v_new_b[j]) for j in heads]
            new_state = [state[j] * jnp.exp(gl_s[n, j]) + _dot_tn(kd_s[n, j], v_new_b[j]) for j in heads]
            yield
            for j in heads:
                st[j] = new_state[j]
                on = o[j] * lax.rsqrt(jnp.mean(o[j] * o[j], axis=-1, keepdims=True) + 1e-6) * nw
                z = z_ref[0, rows, lanes_of[j]]
                o_ref[0, rows, lanes_of[j]] = on * (z * jax.nn.sigmoid(z))
            yield

    nchunks = s // c
    group = 4
    ngroups = nchunks // group

    def chunks_of(gidx):
        return [gidx * group + i for i in range(group)]

    def run(gen):
        for _ in gen:
            pass

    run(local_stage(chunks_of(jnp.int32(0))))

    def both(gidx, carry):
        _emit_zipped(local_stage(chunks_of(gidx)), state_stage(chunks_of(gidx - 1)))
        return carry

    lax.fori_loop(1, ngroups, both, 0)
    run(state_stage(chunks_of(jnp.int32(ngroups - 1))))


def _gdn(proj3, gexp, bexp, conv_w, norm_w, cols):
    b, s, _ = proj3.shape
    nh = 2
    nc = s // CHUNK
    wd = nh * LANES
    cq, ck, cv, cz = (c0 // nh for c0 in cols)

    def blk(c0):
        return pl.BlockSpec((1, s, wd), lambda i, h: (i, 0, c0 + h))

    def cw(c0):
        return pl.BlockSpec((conv_w.shape[0], wd), lambda i, h: (0, c0 + h))

    head = pl.BlockSpec((1, s, wd), lambda i, h: (i, 0, h))
    return pl.pallas_call(
        _gdn_kernel,
        grid=(b, GDN_HEADS // nh),
        in_specs=[blk(cq), blk(ck), blk(cv), blk(cz), head, head,
                  cw(0), cw(GDN_HEADS // nh), cw(2 * GDN_HEADS // nh),
                  pl.BlockSpec((1, LANES), lambda i, h: (0, 0))],
        out_specs=head,
        out_shape=jax.ShapeDtypeStruct((b, s, GDN_HEADS * LANES), F32),
        scratch_shapes=[pltpu.VMEM((s, wd), F32), pltpu.VMEM((s, wd), F32),
                        pltpu.VMEM((s, wd), F32), pltpu.VMEM((nh, GDN_DK, LANES), F32),
                        pltpu.VMEM((nc, nh, CHUNK, 2 * LANES), F32), pltpu.VMEM((nc, nh, CHUNK, CHUNK), BF16),
                        pltpu.VMEM((nc, nh, CHUNK, LANES), BF16), pltpu.VMEM((nc, nh, CHUNK, LANES), BF16),
                        pltpu.VMEM((nc, nh, 1, LANES), F32)],
        compiler_params=pltpu.CompilerParams(
            dimension_semantics=("parallel", "parallel"), vmem_limit_bytes=VMEM_LIMIT),
        name="gdn",
    )(proj3, proj3, proj3, proj3, gexp, bexp, conv_w, conv_w, conv_w, norm_w)


def _fox_kernel(q_ref, k_ref, v_ref, c_ref, ct_ref, o_ref, *, tk):
    tq = q_ref.shape[1]
    nj = q_ref.shape[2] // LANES
    g = pl.program_id(1)
    qi = pl.program_id(2)
    q = q_ref[0]
    cblk = c_ref[0]
    lane = lax.broadcasted_iota(jnp.int32, (tq, LANES), 1)
    qpos = qi * tq + lax.broadcasted_iota(jnp.int32, (tq, tk), 0)
    kofs = lax.broadcasted_iota(jnp.int32, (tq, tk), 1)
    heads = list(range(nj))
    lanes_of = [slice(j * LANES, (j + 1) * LANES) for j in heads]
    ccol = [jnp.sum(jnp.where(lane == 8 + g * nj + j, cblk, 0.0), axis=-1, keepdims=True) for j in heads]
    qh = [q[:, hs] * jnp.asarray(FX_DH ** -0.5, BF16) for hs in lanes_of]

    def body(kv, carry):
        k0 = pl.multiple_of(kv * tk, tk)
        causal = qpos >= k0 + kofs
        kblk = k_ref[0, pl.ds(k0, tk), :]
        vblk = v_ref[0, pl.ds(k0, tk), :]
        sc = [_dot_nt(qh[j], kblk[:, lanes_of[j]]) for j in heads]
        crow = [ct_ref[0, pl.ds(g * nj + j, 1), pl.ds(k0, tk)] for j in heads]
        sc = [jnp.where(causal, sc[j] + ccol[j] - crow[j], -1e30) for j in heads]
        m_new = [jnp.maximum(carry[j][0], jnp.max(sc[j], axis=-1, keepdims=True)) for j in heads]
        a = [jnp.exp(carry[j][0] - m_new[j]) for j in heads]
        p = [jnp.exp(sc[j] - m_new[j]) for j in heads]
        l = [a[j] * carry[j][1] + jnp.sum(p[j], axis=-1, keepdims=True) for j in heads]
        acc = [a[j] * carry[j][2] + _dot(p[j].astype(BF16), vblk[:, lanes_of[j]]) for j in heads]
        return tuple((m_new[j], l[j], acc[j]) for j in heads)

    init = tuple((jnp.full((tq, 1), -1e30, F32), jnp.zeros((tq, 1), F32), jnp.zeros((tq, LANES), F32))
                 for _ in heads)
    nkv = (qi * tq + tq - 1) // tk + 1
    res = lax.fori_loop(0, nkv, body, init)
    o_ref[0] = jnp.concatenate([acc / l for _, l, acc in res], axis=1)


def _fox(pf3, c, ct):
    b, s, _ = pf3.shape
    wd = 4 * LANES
    tq = min(128, s)
    tk = min(256, s)
    ngrp = FX_HEADS * LANES // wd
    return pl.pallas_call(
        functools.partial(_fox_kernel, tk=tk),
        grid=(b, ngrp, s // tq),
        in_specs=[pl.BlockSpec((1, tq, wd), lambda i, h, t: (i, t, h)),
                  pl.BlockSpec((1, s, wd), lambda i, h, t: (i, 0, ngrp + h)),
                  pl.BlockSpec((1, s, wd), lambda i, h, t: (i, 0, 2 * ngrp + h)),
                  pl.BlockSpec((1, tq, LANES), lambda i, h, t: (i, t, 0)),
                  pl.BlockSpec((1, 8, s), lambda i, h, t: (i, 0, 0))],
        out_specs=pl.BlockSpec((1, tq, wd), lambda i, h, t: (i, t, h)),
        out_shape=jax.ShapeDtypeStruct((b, s, ngrp * wd), F32),
        compiler_params=pltpu.CompilerParams(
            dimension_semantics=("parallel", "parallel", "parallel"), vmem_limit_bytes=VMEM_LIMIT),
        name="fox",
    )(pf3, pf3, pf3, c, ct)


def _mix_kernel(oa_ref, ob_ref, ga_ref, gb_ref, x_ref, wa_ref, wb_ref, wo_ref, g1_ref, b1_ref, o_ref, *, alpha):
    ya = _dot(oa_ref[...].astype(BF16), wa_ref[...])
    yb = _dot(ob_ref[...].astype(BF16), wb_ref[...])
    mix = jax.nn.sigmoid(ga_ref[...]) * ya + jax.nn.sigmoid(gb_ref[...]) * yb
    z = alpha * x_ref[...] + _dot(mix.astype(BF16), wo_ref[...])
    o_ref[...] = _layernorm(z, g1_ref[...], b1_ref[...])


def _mix(oa, ob, proj, x2, wa, wb, wo, g1, b1, alpha):
    t, d = x2.shape
    tm = min(512, t)
    w, w2 = oa.shape[1], ob.shape[1]
    full = lambda r, c: pl.BlockSpec((r, c), lambda i: (0, 0))
    return pl.pallas_call(
        functools.partial(_mix_kernel, alpha=alpha),
        grid=(t // tm,),
        in_specs=[pl.BlockSpec((tm, w), lambda i: (i, 0)),
                  pl.BlockSpec((tm, w2), lambda i: (i, 0)),
                  pl.BlockSpec((tm, d), lambda i: (i, 0)),
                  pl.BlockSpec((tm, d), lambda i: (i, 1)),
                  pl.BlockSpec((tm, d), lambda i: (i, 0)),
                  full(w, d), full(w2, d), full(d, d), full(1, d), full(1, d)],
        out_specs=pl.BlockSpec((tm, d), lambda i: (i, 0)),
        out_shape=jax.ShapeDtypeStruct((t, d), F32),
        compiler_params=pltpu.CompilerParams(
            dimension_semantics=("parallel",), vmem_limit_bytes=VMEM_LIMIT),
        name="mix",
    )(oa, ob, proj, proj, x2, wa, wb, wo, g1, b1)


def _route_head(keys_ref, q_scr, hd, lh, scr, flat, cmask):
    s_scr, i_scr, best_scr, eh_scr = scr
    nk, kk = PEER_NKEYS, PEER_TOPK
    iota_k = lax.broadcasted_iota(jnp.int32, (nk, LANES), 0)
    neg = jnp.float32(-jnp.inf)
    for p in range(2):
        vals = _dot_nt(keys_ref[hd * 2 + p], q_scr[hd * 2 + p, lh])
        for r in range(kk):
            m = jnp.max(vals, axis=0, keepdims=True)
            am = jnp.min(jnp.where(vals == m, iota_k, nk), axis=0, keepdims=True)
            s_scr[p * kk + r:p * kk + r + 1, :] = m
            i_scr[p * kk + r:p * kk + r + 1, :] = am
            vals = jnp.where(iota_k == am, neg, vals)
    s1 = s_scr[kk:kk + 8, :]
    i1 = i_scr[kk:kk + 8, :]
    cand = [s_scr[0:1, :] + s_scr[kk:2 * kk, :]]
    cidx = [i_scr[0:1, :] * nk + i_scr[kk:2 * kk, :]]
    for a in range(1, 8):
        cand.append(s_scr[a:a + 1, :] + s1)
        cidx.append(i_scr[a:a + 1, :] * nk + i1)
    cand.append(s_scr[8:kk, :] + s_scr[kk:kk + 1, :])
    cidx.append(i_scr[8:kk, :] * nk + i_scr[kk:kk + 1, :])
    vals = jnp.concatenate(cand, axis=0) + cmask
    cidx = jnp.concatenate(cidx, axis=0)
    for r in range(kk):
        m = jnp.max(vals, axis=0, keepdims=True)
        am = jnp.min(jnp.where(vals == m, flat, 2 * kk * kk), axis=0, keepdims=True)
        sel = flat == am
        best_scr[r:r + 1, :] = m
        eh_scr[r:r + 1, :] = jnp.max(jnp.where(sel, cidx, -1), axis=0, keepdims=True)
        vals = jnp.where(sel, neg, vals)
    bs = best_scr[...]
    ex = jnp.exp(bs - bs[0:1, :])
    return ex / jnp.sum(ex, axis=0, keepdims=True), eh_scr[...]


def _candidate_tables():
    kk = PEER_TOPK
    pairs = [(0, bb) for bb in range(kk)]
    for a in range(1, 8):
        pairs += [(a, bb) for bb in range(8)]
    pairs += [(a, 0) for a in range(8, kk)]
    real = [(a + 1) * (bb + 1) <= kk for a, bb in pairs]
    flat = [a * kk + bb if ok else kk * kk + r for r, ((a, bb), ok) in enumerate(zip(pairs, real))]
    flat = jnp.broadcast_to(jnp.asarray(flat, jnp.int32)[:, None], (len(pairs), LANES))
    cmask = jnp.broadcast_to(jnp.asarray([0.0 if ok else -jnp.inf for ok in real], F32)[:, None],
                             (len(pairs), LANES))
    return flat, cmask


def _route_kernel(h_ref, wq_ref, keys_ref, flat_ref, cmask_ref, e_ref, g_ref,
                  q_scr, s_scr, i_scr, best_scr, eh_scr, eall_scr, gall_scr):
    tg = h_ref.shape[0]
    kk = PEER_TOPK
    nlh = tg // LANES
    q = _dot(h_ref[...].astype(BF16), wq_ref[...])
    for j in range(2 * PEER_HEADS):
        for lh in range(nlh):
            q_scr[j, lh] = q[lh * LANES:(lh + 1) * LANES, j * LANES:(j + 1) * LANES].astype(BF16)
    flat = flat_ref[...]
    cmask = cmask_ref[...]

    def piece(p, carry):
        hd = p // nlh
        lh = p % nlh
        gates, ids = _route_head(keys_ref, q_scr, hd, lh, (s_scr, i_scr, best_scr, eh_scr), flat, cmask)
        r0 = pl.multiple_of(hd * kk, kk)
        gall_scr[lh, pl.ds(r0, kk), :] = gates
        eall_scr[lh, pl.ds(r0, kk), :] = ids
        return carry

    lax.fori_loop(0, PEER_HEADS * nlh, piece, 0)
    for lh in range(nlh):
        e_ref[lh * LANES:(lh + 1) * LANES, :] = eall_scr[lh].T
        g_ref[lh * LANES:(lh + 1) * LANES, :] = gall_scr[lh].T


def _route(h, wq, keys):
    t, d = h.shape
    tg = 256
    nq = wq.shape[1]
    kk = PEER_TOPK
    hk = PEER_HEADS * kk
    nlh = tg // LANES
    flat, cmask = _candidate_tables()
    const = lambda shape: pl.BlockSpec(shape, lambda i: (0,) * len(shape))
    return pl.pallas_call(
        _route_kernel,
        grid=(t // tg,),
        in_specs=[pl.BlockSpec((tg, d), lambda i: (i, 0)),
                  const((d, nq)), const(keys.shape), const(flat.shape), const(cmask.shape)],
        out_specs=[pl.BlockSpec((tg, hk), lambda i: (i, 0)), pl.BlockSpec((tg, hk), lambda i: (i, 0))],
        out_shape=[jax.ShapeDtypeStruct((t, hk), jnp.int32), jax.ShapeDtypeStruct((t, hk), F32)],
        scratch_shapes=[pltpu.VMEM((2 * PEER_HEADS, nlh, LANES, LANES), BF16),
                        pltpu.VMEM((2 * kk, LANES), F32), pltpu.VMEM((2 * kk, LANES), jnp.int32),
                        pltpu.VMEM((kk, LANES), F32), pltpu.VMEM((kk, LANES), jnp.int32),
                        pltpu.VMEM((nlh, hk, LANES), jnp.int32), pltpu.VMEM((nlh, hk, LANES), F32)],
        compiler_params=pltpu.CompilerParams(
            dimension_semantics=("parallel",), vmem_limit_bytes=VMEM_LIMIT),
        name="route",
    )(h, wq, keys, flat, cmask)


PEER_SC_SHARE_PCT = 44
PEER_TC_FIRST_PCT = 53
SC_WORKERS = 32
SC_LANES = 16
SC_CHUNK = 32
SC_DBLK = 16
SC_EBLK = 4
SC_JUNROLL = 8


def _sc_mesh():
    return plsc.VectorSubcoreMesh(core_axis_name="c", subcore_axis_name="s")


def _sc_worker():
    return lax.axis_index("s") * 2 + lax.axis_index("c")


def _sc_pre(ids, h, u_tab):
    ts, hk = ids.shape
    d = h.shape[1]
    per_w = ts // SC_WORKERS
    nch = hk // SC_CHUNK

    def body(ids_hbm, h_hbm, u_hbm, out_hbm, idx_v, h_v, rows_a, rows_b, pre_v, sem_a, sem_b):
        base = _sc_worker() * per_w
        lane = lax.iota(jnp.int32, SC_LANES)
        bufs = (rows_a, rows_b)
        sems = (sem_a, sem_b)

        def gather(c):
            return pltpu.async_copy(u_hbm.at[idx_v.at[pl.ds(c * SC_CHUNK, SC_CHUNK)]], bufs[c % 2], sems[c % 2])

        def token(i, carry):
            t = base + i
            pltpu.sync_copy(ids_hbm.at[t], idx_v)
            pltpu.sync_copy(h_hbm.at[t], h_v)
            cps = {0: gather(0)}
            for c in range(nch):
                if c + 1 < nch:
                    cps[c + 1] = gather(c + 1)
                cps[c].wait()
                rows = bufs[c % 2]
                for g in range(SC_CHUNK // SC_LANES):
                    def experts(q, vec, g=g, rows=rows):
                        e0 = g * SC_LANES + q * SC_EBLK

                        def span(jb, accs, rows=rows, e0=e0):
                            accs = list(accs)
                            for jj in range(SC_JUNROLL):
                                sl = pl.ds((jb * SC_JUNROLL + jj) * SC_LANES, SC_LANES)
                                hv = h_v[sl]
                                for i in range(SC_EBLK):
                                    accs[i] = accs[i] + rows[e0 + i, sl] * hv
                            return tuple(accs)
                        accs = lax.fori_loop(0, d // (SC_LANES * SC_JUNROLL), span,
                                             tuple(jnp.zeros((SC_LANES,), F32) for _ in range(SC_EBLK)))
                        for i in range(SC_EBLK):
                            vec = jnp.where(lane == q * SC_EBLK + i, jnp.sum(accs[i]), vec)
                        return vec
                    vec = lax.fori_loop(0, SC_LANES // SC_EBLK, experts, jnp.zeros((SC_LANES,), F32))
                    pre_v[pl.ds(c * SC_CHUNK + g * SC_LANES, SC_LANES)] = vec
            pltpu.sync_copy(pre_v, out_hbm.at[t])
            return carry

        lax.fori_loop(0, per_w, token, 0)

    return pl.kernel(
        body, out_type=jax.ShapeDtypeStruct((ts, hk), F32), mesh=_sc_mesh(),
        scratch_types=[pltpu.VMEM((hk,), jnp.int32), pltpu.VMEM((d,), F32),
                       pltpu.VMEM((SC_CHUNK, d), F32), pltpu.VMEM((SC_CHUNK, d), F32),
                       pltpu.VMEM((hk,), F32), pltpu.SemaphoreType.DMA, pltpu.SemaphoreType.DMA],
        compiler_params=pltpu.CompilerParams(needs_layout_passes=False),
        name="sc_pre",
    )(ids, h, u_tab)


def _sc_out(ids, act, v_tab):
    ts, hk = ids.shape
    d = v_tab.shape[1]
    per_w = ts // SC_WORKERS
    nch = hk // SC_CHUNK

    def body(ids_hbm, act_hbm, v_hbm, out_hbm, idx_v, act_v, rows_a, rows_b, y_v, sem_a, sem_b):
        base = _sc_worker() * per_w
        lane = lax.iota(jnp.int32, SC_LANES)
        bufs = (rows_a, rows_b)
        sems = (sem_a, sem_b)

        def gather(c):
            return pltpu.async_copy(v_hbm.at[idx_v.at[pl.ds(c * SC_CHUNK, SC_CHUNK)]], bufs[c % 2], sems[c % 2])

        def token(i, carry):
            t = base + i
            pltpu.sync_copy(ids_hbm.at[t], idx_v)
            pltpu.sync_copy(act_hbm.at[t], act_v)
            for j in range(d // SC_LANES):
                y_v[pl.ds(j * SC_LANES, SC_LANES)] = jnp.zeros((SC_LANES,), F32)
            cps = {0: gather(0)}
            for c in range(nch):
                if c + 1 < nch:
                    cps[c + 1] = gather(c + 1)
                cps[c].wait()
                rows = bufs[c % 2]
                for db in range(d // (SC_DBLK * SC_LANES)):
                    def expert(r, accs, c=c, rows=rows, db=db):
                        a = plsc.load_gather(act_v, [jnp.zeros((SC_LANES,), jnp.int32) + (c * SC_CHUNK + r)])
                        return tuple(
                            accs[j] + a * rows[r, pl.ds((db * SC_DBLK + j) * SC_LANES, SC_LANES)]
                            for j in range(SC_DBLK))
                    accs = lax.fori_loop(0, SC_CHUNK, expert,
                                         tuple(jnp.zeros((SC_LANES,), F32) for _ in range(SC_DBLK)))
                    for j in range(SC_DBLK):
                        plsc.addupdate(y_v.at[pl.ds((db * SC_DBLK + j) * SC_LANES, SC_LANES)], accs[j])
            pltpu.sync_copy(y_v, out_hbm.at[t])
            return carry

        lax.fori_loop(0, per_w, token, 0)

    return pl.kernel(
        body, out_type=jax.ShapeDtypeStruct((ts, d), F32), mesh=_sc_mesh(),
        scratch_types=[pltpu.VMEM((hk,), jnp.int32), pltpu.VMEM((hk,), F32),
                       pltpu.VMEM((SC_CHUNK, d), F32), pltpu.VMEM((SC_CHUNK, d), F32),
                       pltpu.VMEM((d,), F32), pltpu.SemaphoreType.DMA, pltpu.SemaphoreType.DMA],
        compiler_params=pltpu.CompilerParams(needs_layout_passes=False),
        name="sc_out",
    )(ids, act, v_tab)


def _act_kernel(pre_ref, gate_ref, after_ref, o_ref):
    del after_ref
    pre = pre_ref[...]
    o_ref[...] = 0.5 * pre * (1.0 + lax.erf(pre * (2.0 ** -0.5))) * gate_ref[...]


def _act(pre, gate, after):
    t, hk = pre.shape
    tm = 256
    assert t % tm == 0
    spec = pl.BlockSpec((tm, hk), lambda i: (i, 0))
    return pl.pallas_call(
        _act_kernel, grid=(t // tm,),
        in_specs=[spec, spec, pl.BlockSpec((8, LANES), lambda i: (0, 0))], out_specs=spec,
        out_shape=jax.ShapeDtypeStruct((t, hk), F32),
        compiler_params=pltpu.CompilerParams(dimension_semantics=("parallel",)),
        name="peer_act",
    )(pre, gate, after)


def _ln2_kernel(h_ref, y_ref, g_ref, b_ref, o_ref, *, alpha):
    o_ref[...] = _layernorm(alpha * h_ref[...] + y_ref[...], g_ref[...], b_ref[...])


def _ln2(h, y, g2, b2, alpha):
    t, d = h.shape
    tm = 256
    assert t % tm == 0
    spec = pl.BlockSpec((tm, d), lambda i: (i, 0))
    vec = pl.BlockSpec((1, d), lambda i: (0, 0))
    return pl.pallas_call(
        functools.partial(_ln2_kernel, alpha=alpha), grid=(t // tm,),
        in_specs=[spec, spec, vec, vec], out_specs=spec,
        out_shape=jax.ShapeDtypeStruct((t, d), F32),
        compiler_params=pltpu.CompilerParams(dimension_semantics=("parallel",)),
        name="peer_ln2",
    )(h, y, g2, b2)


def _peer_kernel(h_cur_ref, h_nxt_ref, wq_ref, keys_ref, flat_ref, cmask_ref, uv_hbm, g2_ref, b2_ref, o_ref,
                 q_scr, s_scr, i_scr, best_scr, eh_scr, eall_scr, idv_scr, ids_smem, gate_scr,
                 buf_a, buf_b, sem, idsem, y_scr, *, tt, alpha):
    s = pl.program_id(0)
    tg, d = h_cur_ref.shape
    nk, kk = PEER_NKEYS, PEER_TOPK
    hk = PEER_HEADS * kk
    nlh = tg // LANES
    nsub = tg // (2 * tt)
    assert nsub == PEER_HEADS * nlh
    nslab, sub = uv_hbm.shape[1], uv_hbm.shape[2]
    half = nslab // 2
    bufs = (buf_a, buf_b)
    last = pl.num_programs(0) - 1
    rslot = s % 3
    pslot = (s + 2) % 3
    eslot = (s + 1) % 3

    def issue(idslot, row0, slot):
        for t in range(tt):
            for k in range(hk):
                e = ids_smem[idslot, row0 + t, k]
                pltpu.make_async_copy(uv_hbm.at[e], bufs[slot].at[:, pl.ds((t * hk + k) * sub, sub), :],
                                      sem.at[slot]).start(priority=k % 2)

    def wait(slot):
        pltpu.make_async_copy(bufs[slot], bufs[slot], sem.at[slot]).wait()

    q = _dot(h_nxt_ref[...].astype(BF16), wq_ref[...])
    for j in range(2 * PEER_HEADS):
        for lh in range(nlh):
            q_scr[j, lh] = q[lh * LANES:(lh + 1) * LANES, j * LANES:(j + 1) * LANES].astype(BF16)
    flat = flat_ref[...]
    cmask = cmask_ref[...]

    def route_piece(piece):
        hd = piece // nlh
        lh = piece % nlh
        gates, ids = _route_head(keys_ref, q_scr, hd, lh, (s_scr, i_scr, best_scr, eh_scr), flat, cmask)
        r0 = pl.multiple_of(hd * kk, kk)
        gate_scr[rslot, lh, pl.ds(r0, kk), :] = gates
        eall_scr[lh, pl.ds(r0, kk), :] = ids

    lane = lax.broadcasted_iota(jnp.int32, (hk, LANES), 1)

    def compute(row0, slot):
        buf = bufs[slot]
        gt = gate_scr[eslot, row0 // LANES]
        lane0 = row0 % LANES
        for t in range(tt):
            hrow = h_cur_ref[pl.ds(row0 + t, 1), :]

            def rows(a, c):
                return buf[a, pl.ds(t * hk * sub + c, hk, stride=sub), :]

            part = None
            for j in range(half):
                for c in range(sub):
                    seg = j * sub + c
                    term = rows(j, c) * hrow[:, seg * LANES:(seg + 1) * LANES]
                    part = term if part is None else part + term
            pre = jnp.sum(part, axis=-1, keepdims=True)
            gate = jnp.sum(jnp.where(lane == lane0 + t, gt, 0.0), axis=-1, keepdims=True)
            act = 0.5 * pre * (1.0 + lax.erf(pre * (2.0 ** -0.5))) * gate
            yrow = slot * tt + t
            for j in range(half):
                for c in range(sub):
                    seg = j * sub + c
                    y_scr[yrow:yrow + 1, seg * LANES:(seg + 1) * LANES] = jnp.sum(
                        act * rows(half + j, c), axis=0, keepdims=True)

    def substep(j, carry):
        row0 = pl.multiple_of(j * 2 * tt, 2 * tt)
        issue(eslot, row0 + tt, 1)
        route_piece(j)
        wait(0)
        compute(row0, 0)
        wait(1)
        wrap = j == nsub - 1
        issue(jnp.where(wrap, pslot, eslot), jnp.where(wrap, 0, row0 + 2 * tt), 0)
        compute(row0 + tt, 1)
        z = alpha * h_cur_ref[pl.ds(row0, 2 * tt), :] + y_scr[...]
        o_ref[pl.ds(row0, 2 * tt), :] = _layernorm(z, g2_ref[...], b2_ref[...])
        return carry

    @pl.when(s < 2)
    def _():
        o_ref[...] = jnp.zeros_like(o_ref)

        def piece(p, carry):
            route_piece(p)
            return carry
        lax.fori_loop(0, nsub, piece, 0)

    @pl.when(s == 1)
    def _():
        issue(pslot, 0, 0)

    @pl.when(s >= 2)
    def _():
        lax.fori_loop(0, nsub, substep, 0)

    @pl.when(s == last)
    def _():
        wait(0)

    for lh in range(nlh):
        idv_scr[lh * LANES:(lh + 1) * LANES, :] = eall_scr[lh].T
    publish = pltpu.make_async_copy(idv_scr, ids_smem.at[rslot], idsem)
    publish.start()
    publish.wait()


def _peer(h1, wq, keys, uv_tab, g2, b2, alpha):
    t, d = h1.shape
    tg = 256
    tt = 8
    ngrp = t // tg
    nq = wq.shape[1]
    kk = PEER_TOPK
    hk = PEER_HEADS * kk
    nlh = tg // LANES
    nslab, sub = uv_tab.shape[1], uv_tab.shape[2]
    flat, cmask = _candidate_tables()
    const = lambda shape: pl.BlockSpec(shape, lambda i: (0,) * len(shape))
    return pl.pallas_call(
        functools.partial(_peer_kernel, tt=tt, alpha=alpha),
        grid=(ngrp + 2,),
        in_specs=[pl.BlockSpec((tg, d), lambda i: (jnp.maximum(i - 2, 0), 0)),
                  pl.BlockSpec((tg, d), lambda i: (jnp.minimum(i, ngrp - 1), 0)),
                  const((d, nq)), const(keys.shape), const(flat.shape), const(cmask.shape),
                  pl.BlockSpec(memory_space=pl.ANY),
                  const((1, d)), const((1, d))],
        out_specs=pl.BlockSpec((tg, d), lambda i: (jnp.maximum(i - 2, 0), 0)),
        out_shape=jax.ShapeDtypeStruct((t, d), F32),
        scratch_shapes=[pltpu.VMEM((2 * PEER_HEADS, nlh, LANES, LANES), BF16),
                        pltpu.VMEM((2 * kk, LANES), F32), pltpu.VMEM((2 * kk, LANES), jnp.int32),
                        pltpu.VMEM((kk, LANES), F32), pltpu.VMEM((kk, LANES), jnp.int32),
                        pltpu.VMEM((nlh, hk, LANES), jnp.int32),
                        pltpu.VMEM((tg, hk), jnp.int32),
                        pltpu.SMEM((3, tg, hk), jnp.int32),
                        pltpu.VMEM((3, nlh, hk, LANES), F32),
                        pltpu.VMEM((nslab, tt * hk * sub, LANES), F32),
                        pltpu.VMEM((nslab, tt * hk * sub, LANES), F32),
                        pltpu.SemaphoreType.DMA((2,)), pltpu.SemaphoreType.DMA,
                        pltpu.VMEM((2 * tt, d), F32)],
        compiler_params=pltpu.CompilerParams(
            dimension_semantics=("arbitrary",), vmem_limit_bytes=VMEM_LIMIT),
        name="peer",
    )(h1, h1, wq, keys, flat, cmask, uv_tab, g2, b2)


def _layer(h, w_in, conv_w, a_log, dt_bias, norm_w, f_bias, w_out_gdn, w_out_fox, w_o, ln1_g, ln1_b,
           peer_wq, peer_keys, peer_u, peer_v, ln2_g, ln2_b, alpha):
    b, s, d = h.shape
    t = b * s
    qk = GDN_HEADS * GDN_DK
    fw = FX_HEADS * FX_DH
    o_gz = 4 * qk
    o_ga = o_gz
    o_fq = o_ga + 2 * GDN_HEADS
    o_ff = o_fq + 3 * fw
    o_gate = o_ff + FX_HEADS
    w_big = jnp.concatenate([w_in[:, o_gate:], w_in[:, :o_gz]], axis=1).astype(BF16)
    pad = LANES - FX_DH
    w_fox = jnp.pad(w_in[:, o_fq:o_ff].reshape(d, 3 * FX_HEADS, FX_DH), ((0, 0), (0, 0), (0, pad)))
    w_fox = w_fox.reshape(d, 3 * FX_HEADS * LANES).astype(BF16)
    w_out_fox_p = jnp.pad(w_out_fox.reshape(FX_HEADS, FX_DH, d), ((0, 0), (0, pad), (0, 0)))
    w_out_fox_p = w_out_fox_p.reshape(FX_HEADS * LANES, d).astype(BF16)
    n_small = 2 * GDN_HEADS + FX_HEADS
    w_small = jnp.concatenate([w_in[:, o_ga:o_fq], w_in[:, o_ff:o_gate],
                               jnp.zeros((d, LANES - n_small), F32)], axis=1)
    params = jnp.zeros((8, LANES), F32)
    params = params.at[0, :GDN_HEADS].set(a_log).at[1, :GDN_HEADS].set(dt_bias)
    params = params.at[2, 2 * GDN_HEADS:n_small].set(f_bias)

    x2 = h.reshape(t, d)
    proj = _in_proj(x2, w_big, F32, "in_proj")
    pf = _in_proj(x2, w_fox, BF16, "in_proj_fox")
    gexp, bexp, c, ct = _prep(h, w_small, params)
    proj3 = proj.reshape(b, s, proj.shape[1])
    gdn0 = 2 * d // LANES
    oa = _gdn(proj3, gexp, bexp, conv_w, norm_w.reshape(1, LANES),
              (gdn0, gdn0 + GDN_HEADS, gdn0 + 2 * GDN_HEADS, gdn0 + 3 * GDN_HEADS))
    ob = _fox(pf.reshape(b, s, pf.shape[1]), c, ct)
    h1 = _mix(oa.reshape(t, qk), ob.reshape(t, FX_HEADS * LANES), proj, x2,
              w_out_gdn.astype(BF16), w_out_fox_p, w_o.astype(BF16),
              ln1_g.reshape(1, d), ln1_b.reshape(1, d), alpha)
    keys = peer_keys.reshape(2 * PEER_HEADS, PEER_NKEYS, peer_keys.shape[-1]).astype(BF16)
    ne = peer_u.shape[0]
    uslabs = d // (PEER_SUB * LANES)
    uv_tab = jnp.concatenate([peer_u.reshape(ne, uslabs, PEER_SUB, LANES),
                              peer_v.reshape(ne, uslabs, PEER_SUB, LANES)], axis=1)
    wq = peer_wq.astype(BF16)
    g2, b2 = ln2_g.reshape(1, d), ln2_b.reshape(1, d)
    t_sc = (t * PEER_SC_SHARE_PCT // 100) // 256 * 256
    t_tc = t - t_sc
    if t_sc:
        h_sc = h1[t_tc:]
        ids_sc, gate_sc = _route(h_sc, wq, keys)
        t_a = (t_tc * PEER_TC_FIRST_PCT // 100) // 256 * 256
        pre = _sc_pre(ids_sc, h_sc, peer_u)
        out_a = _peer(h1[:t_a], wq, keys, uv_tab, g2, b2, alpha)
        y_sc = _sc_out(ids_sc, _act(pre, gate_sc, out_a), peer_v)
        out_b = _peer(h1[t_a:t_tc], wq, keys, uv_tab, g2, b2, alpha)
        out = jnp.concatenate([out_a, out_b, _ln2(h_sc, y_sc, g2, b2, alpha)], axis=0)
    else:
        out = _peer(h1, wq, keys, uv_tab, g2, b2, alpha)
    return out.reshape(b, s, d)


def kernel(x, w_in, gdn_conv_w, gdn_a_log, gdn_dt_bias, gdn_norm_w, fox_f_bias, w_out_gdn, w_out_fox, w_o,
           ln1_g, ln1_b, peer_wq, peer_keys, peer_u, peer_v, ln2_g, ln2_b):
    depth = w_in.shape[0]
    alpha = (2.0 * depth) ** 0.25
    h = x
    for l in range(depth):
        h = _layer(h, w_in[l], gdn_conv_w[l], gdn_a_log[l], gdn_dt_bias[l], gdn_norm_w[l], fox_f_bias[l],
                   w_out_gdn[l], w_out_fox[l], w_o[l], ln1_g[l], ln1_b[l], peer_wq[l], peer_keys[l],
                   peer_u[l], peer_v[l], ln2_g[l], ln2_b[l], alpha)
    return h
```

```python
import functools

import jax
import jax.numpy as jnp
from jax import lax
from jax.experimental import pallas as pl
from jax.experimental.pallas import tpu as pltpu
from jax.experimental.pallas import tpu_sc as plsc

F32 = jnp.float32
BF16 = jnp.bfloat16
HI = lax.Precision.HIGHEST

LANES = 128
CHUNK = 64
GDN_HEADS = 4
GDN_DK = 128
FX_HEADS = 8
FX_DH = 64
PEER_HEADS = 8
PEER_NKEYS = 128
PEER_TOPK = 16
PEER_SUB = 4
LN_EPS = 1e-5
VMEM_LIMIT = 48 * 1024 * 1024


def _dot(a, b, prec=None):
    return jnp.dot(a, b, preferred_element_type=F32, precision=prec)


def _dot_nt(a, b, prec=None):
    return lax.dot_general(a, b, (((1,), (1,)), ((), ())), preferred_element_type=F32, precision=prec)


def _dot_tn(a, b, prec=None):
    return lax.dot_general(a, b, (((0,), (0,)), ((), ())), preferred_element_type=F32, precision=prec)


def _softplus(x):
    return jnp.maximum(x, 0.0) + jnp.log1p(jnp.exp(-jnp.abs(x)))


def _layernorm(z, g, b):
    mu = jnp.mean(z, axis=-1, keepdims=True)
    zc = z - mu
    var = jnp.mean(zc * zc, axis=-1, keepdims=True)
    return zc * lax.rsqrt(var + LN_EPS) * g + b


def _mm_kernel(x_ref, w_ref, o_ref):
    o_ref[...] = _dot(x_ref[...].astype(BF16), w_ref[...]).astype(o_ref.dtype)


def _in_proj(x2, w_big, out_dtype, name):
    t, d = x2.shape
    n = w_big.shape[1]
    tm = min(1024, t)
    tn = 512
    return pl.pallas_call(
        _mm_kernel,
        grid=(t // tm, n // tn),
        in_specs=[pl.BlockSpec((tm, d), lambda i, j: (i, 0)),
                  pl.BlockSpec((d, tn), lambda i, j: (0, j))],
        out_specs=pl.BlockSpec((tm, tn), lambda i, j: (i, j)),
        out_shape=jax.ShapeDtypeStruct((t, n), out_dtype),
        compiler_params=pltpu.CompilerParams(
            dimension_semantics=("parallel", "parallel"), vmem_limit_bytes=VMEM_LIMIT),
        name=name,
    )(x2, w_big)


def _prep_kernel(x_ref, w_ref, par_ref, gexp_ref, bexp_ref, c_ref, ct_ref, carry_ref):
    ts = x_ref.shape[1]

    @pl.when(pl.program_id(1) == 0)
    def _():
        carry_ref[...] = jnp.zeros_like(carry_ref)

    small = _dot(x_ref[0], w_ref[...], HI)
    a_log = par_ref[0:1, :]
    dt_bias = par_ref[1:2, :]
    f_bias = par_ref[2:3, :]
    g = -jnp.exp(a_log) * _softplus(small + dt_bias)
    beta = jax.nn.sigmoid(small)
    lane = lax.broadcasted_iota(jnp.int32, (ts, LANES), 1)
    log_f = jnp.where((lane >= 8) & (lane < 16), -_softplus(-(small + f_bias)), 0.0)
    row = lax.broadcasted_iota(jnp.int32, (ts, ts), 0)
    col = lax.broadcasted_iota(jnp.int32, (ts, ts), 1)
    tril = (row >= col).astype(F32)
    c = _dot(tril, log_f, HI) + carry_ref[...]
    carry_ref[...] = c[ts - 1:ts, :]
    c_ref[0] = c
    ct_ref[0] = c.T[8:16, :]
    gexp_ref[0] = jnp.concatenate(
        [jnp.broadcast_to(g[:, h:h + 1], (ts, LANES)) for h in range(GDN_HEADS)], axis=1)
    bexp_ref[0] = jnp.concatenate(
        [jnp.broadcast_to(beta[:, GDN_HEADS + h:GDN_HEADS + h + 1], (ts, LANES)) for h in range(GDN_HEADS)], axis=1)


def _prep(x, w_small, params):
    b, s, d = x.shape
    ts = min(512, s)
    hw = GDN_HEADS * LANES
    return pl.pallas_call(
        _prep_kernel,
        grid=(b, s // ts),
        in_specs=[pl.BlockSpec((1, ts, d), lambda i, j: (i, j, 0)),
                  pl.BlockSpec((d, LANES), lambda i, j: (0, 0)),
                  pl.BlockSpec((8, LANES), lambda i, j: (0, 0))],
        out_specs=[pl.BlockSpec((1, ts, hw), lambda i, j: (i, j, 0)),
                   pl.BlockSpec((1, ts, hw), lambda i, j: (i, j, 0)),
                   pl.BlockSpec((1, ts, LANES), lambda i, j: (i, j, 0)),
                   pl.BlockSpec((1, 8, ts), lambda i, j: (i, 0, j))],
        out_shape=[jax.ShapeDtypeStruct((b, s, hw), F32),
                   jax.ShapeDtypeStruct((b, s, hw), F32),
                   jax.ShapeDtypeStruct((b, s, LANES), F32),
                   jax.ShapeDtypeStruct((b, 8, s), F32)],
        scratch_shapes=[pltpu.VMEM((1, LANES), F32)],
        compiler_params=pltpu.CompilerParams(
            dimension_semantics=("parallel", "arbitrary"), vmem_limit_bytes=VMEM_LIMIT),
        name="prep",
    )(x, w_small, params)


def _each(fn, *lists):
    return [fn(*args) for args in zip(*lists)]


def _unit_lower_inverse(ms, masks):
    eye, blk16, blk32 = masks
    hi = lambda a, b: _dot(a, b, HI)
    n1 = _each(lambda m: -jnp.where(blk16, m, 0.0), ms)
    l1 = _each(lambda m: jnp.where(blk32 & jnp.logical_not(blk16), m, 0.0), ms)
    l2 = _each(lambda m: jnp.where(blk32, 0.0, m), ms)
    n2 = _each(hi, n1, n1)
    yield
    p = _each(lambda a, b: hi(eye + a, eye + b), n1, n2)
    n4 = _each(hi, n2, n2)
    yield
    p = _each(lambda a, b: hi(a, eye + b), p, n4)
    n8 = _each(hi, n4, n4)
    yield
    d_inv = _each(lambda a, b: hi(a, eye + b), p, n8)
    yield
    dl = _each(hi, d_inv, l1)
    yield
    a32 = _each(lambda a, b: a - hi(b, a), d_inv, dl)
    yield
    al = _each(hi, a32, l2)
    yield
    return _each(lambda a, b: a - hi(b, a), a32, al)


def _emit_zipped(main, side):
    live = [main, side]
    while live:
        for gen in list(live):
            try:
                next(gen)
            except StopIteration:
                live.remove(gen)


def _gdn_kernel(q_ref, k_ref, v_ref, z_ref, g_ref, b_ref, cwq_ref, cwk_ref, cwv_ref, nw_ref,
                o_ref, qn, kn, vn, st, sol_s, qk_s, qg_s, kd_s, gl_s):
    s = q_ref.shape[1]
    c = CHUNK
    nh = q_ref.shape[2] // LANES
    row = lax.broadcasted_iota(jnp.int32, (s, LANES), 0)

    def conv_silu(x, w):
        y = x * w[3:4, :]
        for sh in (1, 2, 3):
            xs = jnp.where(row >= sh, pltpu.roll(x, sh, axis=0), 0.0)
            y = y + xs * w[3 - sh:4 - sh, :]
        return y * jax.nn.sigmoid(y)

    def l2norm(x):
        return x * lax.rsqrt(jnp.sum(x * x, axis=-1, keepdims=True) + 1e-6)

    for j in range(nh):
        hs = slice(j * LANES, (j + 1) * LANES)
        qn[:, hs] = l2norm(conv_silu(q_ref[0, :, hs], cwq_ref[:, hs])) * (GDN_DK ** -0.5)
        kn[:, hs] = l2norm(conv_silu(k_ref[0, :, hs], cwk_ref[:, hs]))
        vn[:, hs] = conv_silu(v_ref[0, :, hs], cwv_ref[:, hs])
    st[...] = jnp.zeros_like(st)

    ri = lax.broadcasted_iota(jnp.int32, (c, c), 0)
    ci = lax.broadcasted_iota(jnp.int32, (c, c), 1)
    tril = ri >= ci
    strict = ri > ci
    t_inc = tril.astype(F32)
    eye = (ri == ci).astype(F32)
    blk16 = (ri >> 4) == (ci >> 4)
    blk32 = (ri >> 5) == (ci >> 5)
    l2 = lax.broadcasted_iota(jnp.int32, (c, 2 * LANES), 0)
    j2 = lax.broadcasted_iota(jnp.int32, (c, 2 * LANES), 1)
    ux = jnp.where((j2 >= c) | (l2 > j2), 1.0, 0.0).astype(F32)
    nw = nw_ref[...]

    heads = list(range(nh))
    lanes_of = [slice(j * LANES, (j + 1) * LANES) for j in heads]

    def local_stage(ns):
        pairs = [(i, j) for i in range(len(ns)) for j in heads]
        rows = [pl.ds(pl.multiple_of(n * c, c), c) for n in ns]
        q = [qn[rows[i], lanes_of[j]] for i, j in pairs]
        k = [kn[rows[i], lanes_of[j]] for i, j in pairs]
        v = [vn[rows[i], lanes_of[j]] for i, j in pairs]
        gb = [g_ref[0, rows[i], lanes_of[j]] for i, j in pairs]
        bb = [b_ref[0, rows[i], lanes_of[j]] for i, j in pairs]
        d = _each(lambda g: _dot(t_inc, jnp.concatenate([g, g], axis=1) * ux, HI), gb)
        kb = _each(lambda a, b: a * b, k, bb)
        kk = _each(lambda a, b: _dot_nt(a, b, HI), kb, k)
        qk = _each(lambda a, b: _dot_nt(a.astype(BF16), b.astype(BF16)), q, k)
        yield
        gc = [x[:, LANES:] for x in d]
        decay = [jnp.where(tril, jnp.exp(x[:, :c]), 0.0) for x in d]
        m = _each(lambda a, b: jnp.where(strict, a * b, 0.0), kk, decay)
        a_inv = yield from _unit_lower_inverse(m, (eye, blk16, blk32))
        yield
        egc = _each(jnp.exp, gc)
        rhs = _each(lambda vv, b, kbb, e: jnp.concatenate([vv * b, kbb * e], axis=1), v, bb, kb, egc)
        sol = _each(lambda a, r: _dot(a, r, HI), a_inv, rhs)
        gl = [x[c - 1:c, :] for x in gc]
        yield
        for p, (i, j) in enumerate(pairs):
            n = ns[i]
            qk_s[n, j] = (qk[p] * decay[p]).astype(BF16)
            qg_s[n, j] = (q[p] * egc[p]).astype(BF16)
            kd_s[n, j] = (k[p] * jnp.exp(gl[p] - gc[p])).astype(BF16)
            gl_s[n, j] = gl[p]
            sol_s[n, j] = sol[p]

    def state_stage(ns):
        for n in ns:
            rows = pl.ds(pl.multiple_of(n * c, c), c)
            state = [st[j] for j in heads]
            state_b = [x.astype(BF16) for x in state]
            v_new = [sol_s[n, j, :, :LANES] - _dot(sol_s[n, j, :, LANES:].astype(BF16), state_b[j])
                     for j in heads]
            yield
            v_new_b = [x.astype(BF16) for x in v_new]
            o = [_dot(qg_s[n, j], state_b[j]) + _dot(qk_s[n, j], v_new_b[j]) for j in heads]
            new_state = [state[j] * jnp.exp(gl_s[n, j]) + _dot_tn(kd_s[n, j], v_new_b[j]) for j in heads]
            yield
            for j in heads:
                st[j] = new_state[j]
                on = o[j] * lax.rsqrt(jnp.mean(o[j] * o[j], axis=-1, keepdims=True) + 1e-6) * nw
                z = z_ref[0, rows, lanes_of[j]]
                o_ref[0, rows, lanes_of[j]] = on * (z * jax.nn.sigmoid(z))
            yield

    nchunks = s // c
    group = 4
    ngroups = nchunks // group

    def chunks_of(gidx):
        return [gidx * group + i for i in range(group)]

    def run(gen):
        for _ in gen:
            pass

    run(local_stage(chunks_of(jnp.int32(0))))

    def both(gidx, carry):
        _emit_zipped(local_stage(chunks_of(gidx)), state_stage(chunks_of(gidx - 1)))
        return carry

    lax.fori_loop(1, ngroups, both, 0)
    run(state_stage(chunks_of(jnp.int32(ngroups - 1))))


def _gdn(proj3, gexp, bexp, conv_w, norm_w, cols):
    b, s, _ = proj3.shape
    nh = 2
    nc = s // CHUNK
    wd = nh * LANES
    cq, ck, cv, cz = (c0 // nh for c0 in cols)

    def blk(c0):
        return pl.BlockSpec((1, s, wd), lambda i, h: (i, 0, c0 + h))

    def cw(c0):
        return pl.BlockSpec((conv_w.shape[0], wd), lambda i, h: (0, c0 + h))

    head = pl.BlockSpec((1, s, wd), lambda i, h: (i, 0, h))
    return pl.pallas_call(
        _gdn_kernel,
        grid=(b, GDN_HEADS // nh),
        in_specs=[blk(cq), blk(ck), blk(cv), blk(cz), head, head,
                  cw(0), cw(GDN_HEADS // nh), cw(2 * GDN_HEADS // nh),
                  pl.BlockSpec((1, LANES), lambda i, h: (0, 0))],
        out_specs=head,
        out_shape=jax.ShapeDtypeStruct((b, s, GDN_HEADS * LANES), F32),
        scratch_shapes=[pltpu.VMEM((s, wd), F32), pltpu.VMEM((s, wd), F32),
                        pltpu.VMEM((s, wd), F32), pltpu.VMEM((nh, GDN_DK, LANES), F32),
                        pltpu.VMEM((nc, nh, CHUNK, 2 * LANES), F32), pltpu.VMEM((nc, nh, CHUNK, CHUNK), BF16),
                        pltpu.VMEM((nc, nh, CHUNK, LANES), BF16), pltpu.VMEM((nc, nh, CHUNK, LANES), BF16),
                        pltpu.VMEM((nc, nh, 1, LANES), F32)],
        compiler_params=pltpu.CompilerParams(
            dimension_semantics=("parallel", "parallel"), vmem_limit_bytes=VMEM_LIMIT),
        name="gdn",
    )(proj3, proj3, proj3, proj3, gexp, bexp, conv_w, conv_w, conv_w, norm_w)


def _fox_kernel(q_ref, k_ref, v_ref, c_ref, ct_ref, o_ref, *, tk):
    tq = q_ref.shape[1]
    nj = q_ref.shape[2] // LANES
    g = pl.program_id(1)
    qi = pl.program_id(2)
    q = q_ref[0]
    cblk = c_ref[0]
    lane = lax.broadcasted_iota(jnp.int32, (tq, LANES), 1)
    qpos = qi * tq + lax.broadcasted_iota(jnp.int32, (tq, tk), 0)
    kofs = lax.broadcasted_iota(jnp.int32, (tq, tk), 1)
    heads = list(range(nj))
    lanes_of = [slice(j * LANES, (j + 1) * LANES) for j in heads]
    ccol = [jnp.sum(jnp.where(lane == 8 + g * nj + j, cblk, 0.0), axis=-1, keepdims=True) for j in heads]
    qh = [q[:, hs] * jnp.asarray(FX_DH ** -0.5, BF16) for hs in lanes_of]

    def body(kv, carry):
        k0 = pl.multiple_of(kv * tk, tk)
        causal = qpos >= k0 + kofs
        kblk = k_ref[0, pl.ds(k0, tk), :]
        vblk = v_ref[0, pl.ds(k0, tk), :]
        sc = [_dot_nt(qh[j], kblk[:, lanes_of[j]]) for j in heads]
        crow = [ct_ref[0, pl.ds(g * nj + j, 1), pl.ds(k0, tk)] for j in heads]
        sc = [jnp.where(causal, sc[j] + ccol[j] - crow[j], -1e30) for j in heads]
        m_new = [jnp.maximum(carry[j][0], jnp.max(sc[j], axis=-1, keepdims=True)) for j in heads]
        a = [jnp.exp(carry[j][0] - m_new[j]) for j in heads]
        p = [jnp.exp(sc[j] - m_new[j]) for j in heads]
        l = [a[j] * carry[j][1] + jnp.sum(p[j], axis=-1, keepdims=True) for j in heads]
        acc = [a[j] * carry[j][2] + _dot(p[j].astype(BF16), vblk[:, lanes_of[j]]) for j in heads]
        return tuple((m_new[j], l[j], acc[j]) for j in heads)

    init = tuple((jnp.full((tq, 1), -1e30, F32), jnp.zeros((tq, 1), F32), jnp.zeros((tq, LANES), F32))
                 for _ in heads)
    nkv = (qi * tq + tq - 1) // tk + 1
    res = lax.fori_loop(0, nkv, body, init)
    o_ref[0] = jnp.concatenate([acc / l for _, l, acc in res], axis=1)


def _fox(pf3, c, ct):
    b, s, _ = pf3.shape
    wd = 4 * LANES
    tq = min(128, s)
    tk = min(256, s)
    ngrp = FX_HEADS * LANES // wd
    return pl.pallas_call(
        functools.partial(_fox_kernel, tk=tk),
        grid=(b, ngrp, s // tq),
        in_specs=[pl.BlockSpec((1, tq, wd), lambda i, h, t: (i, t, h)),
                  pl.BlockSpec((1, s, wd), lambda i, h, t: (i, 0, ngrp + h)),
                  pl.BlockSpec((1, s, wd), lambda i, h, t: (i, 0, 2 * ngrp + h)),
                  pl.BlockSpec((1, tq, LANES), lambda i, h, t: (i, t, 0)),
                  pl.BlockSpec((1, 8, s), lambda i, h, t: (i, 0, 0))],
        out_specs=pl.BlockSpec((1, tq, wd), lambda i, h, t: (i, t, h)),
        out_shape=jax.ShapeDtypeStruct((b, s, ngrp * wd), F32),
        compiler_params=pltpu.CompilerParams(
            dimension_semantics=("parallel", "parallel", "parallel"), vmem_limit_bytes=VMEM_LIMIT),
        name="fox",
    )(pf3, pf3, pf3, c, ct)


def _mix_kernel(oa_ref, ob_ref, ga_ref, gb_ref, x_ref, wa_ref, wb_ref, wo_ref, g1_ref, b1_ref, o_ref, *, alpha):
    ya = _dot(oa_ref[...].astype(BF16), wa_ref[...])
    yb = _dot(ob_ref[...].astype(BF16), wb_ref[...])
    mix = jax.nn.sigmoid(ga_ref[...]) * ya + jax.nn.sigmoid(gb_ref[...]) * yb
    z = alpha * x_ref[...] + _dot(mix.astype(BF16), wo_ref[...])
    o_ref[...] = _layernorm(z, g1_ref[...], b1_ref[...])


def _mix(oa, ob, proj, x2, wa, wb, wo, g1, b1, alpha):
    t, d = x2.shape
    tm = min(512, t)
    w, w2 = oa.shape[1], ob.shape[1]
    full = lambda r, c: pl.BlockSpec((r, c), lambda i: (0, 0))
    return pl.pallas_call(
        functools.partial(_mix_kernel, alpha=alpha),
        grid=(t // tm,),
        in_specs=[pl.BlockSpec((tm, w), lambda i: (i, 0)),
                  pl.BlockSpec((tm, w2), lambda i: (i, 0)),
                  pl.BlockSpec((tm, d), lambda i: (i, 0)),
                  pl.BlockSpec((tm, d), lambda i: (i, 1)),
                  pl.BlockSpec((tm, d), lambda i: (i, 0)),
                  full(w, d), full(w2, d), full(d, d), full(1, d), full(1, d)],
        out_specs=pl.BlockSpec((tm, d), lambda i: (i, 0)),
        out_shape=jax.ShapeDtypeStruct((t, d), F32),
        compiler_params=pltpu.CompilerParams(
            dimension_semantics=("parallel",), vmem_limit_bytes=VMEM_LIMIT),
        name="mix",
    )(oa, ob, proj, proj, x2, wa, wb, wo, g1, b1)


def _route_head(keys_ref, q_scr, hd, lh, scr, flat, cmask):
    s_scr, i_scr, best_scr, eh_scr = scr
    nk, kk = PEER_NKEYS, PEER_TOPK
    iota_k = lax.broadcasted_iota(jnp.int32, (nk, LANES), 0)
    neg = jnp.float32(-jnp.inf)
    for p in range(2):
        vals = _dot_nt(keys_ref[hd * 2 + p], q_scr[hd * 2 + p, lh])
        for r in range(kk):
            m = jnp.max(vals, axis=0, keepdims=True)
            am = jnp.min(jnp.where(vals == m, iota_k, nk), axis=0, keepdims=True)
            s_scr[p * kk + r:p * kk + r + 1, :] = m
            i_scr[p * kk + r:p * kk + r + 1, :] = am
            vals = jnp.where(iota_k == am, neg, vals)
    s1 = s_scr[kk:kk + 8, :]
    i1 = i_scr[kk:kk + 8, :]
    cand = [s_scr[0:1, :] + s_scr[kk:2 * kk, :]]
    cidx = [i_scr[0:1, :] * nk + i_scr[kk:2 * kk, :]]
    for a in range(1, 8):
        cand.append(s_scr[a:a + 1, :] + s1)
        cidx.append(i_scr[a:a + 1, :] * nk + i1)
    cand.append(s_scr[8:kk, :] + s_scr[kk:kk + 1, :])
    cidx.append(i_scr[8:kk, :] * nk + i_scr[kk:kk + 1, :])
    vals = jnp.concatenate(cand, axis=0) + cmask
    cidx = jnp.concatenate(cidx, axis=0)
    for r in range(kk):
        m = jnp.max(vals, axis=0, keepdims=True)
        am = jnp.min(jnp.where(vals == m, flat, 2 * kk * kk), axis=0, keepdims=True)
        sel = flat == am
        best_scr[r:r + 1, :] = m
        eh_scr[r:r + 1, :] = jnp.max(jnp.where(sel, cidx, -1), axis=0, keepdims=True)
        vals = jnp.where(sel, neg, vals)
    bs = best_scr[...]
    ex = jnp.exp(bs - bs[0:1, :])
    return ex / jnp.sum(ex, axis=0, keepdims=True), eh_scr[...]


def _candidate_tables():
    kk = PEER_TOPK
    pairs = [(0, bb) for bb in range(kk)]
    for a in range(1, 8):
        pairs += [(a, bb) for bb in range(8)]
    pairs += [(a, 0) for a in range(8, kk)]
    real = [(a + 1) * (bb + 1) <= kk for a, bb in pairs]
    flat = [a * kk + bb if ok else kk * kk + r for r, ((a, bb), ok) in enumerate(zip(pairs, real))]
    flat = jnp.broadcast_to(jnp.asarray(flat, jnp.int32)[:, None], (len(pairs), LANES))
    cmask = jnp.broadcast_to(jnp.asarray([0.0 if ok else -jnp.inf for ok in real], F32)[:, None],
                             (len(pairs), LANES))
    return flat, cmask


def _route_kernel(h_ref, wq_ref, keys_ref, flat_ref, cmask_ref, e_ref, g_ref,
                  q_scr, s_scr, i_scr, best_scr, eh_scr, eall_scr, gall_scr):
    tg = h_ref.shape[0]
    kk = PEER_TOPK
    nlh = tg // LANES
    q = _dot(h_ref[...].astype(BF16), wq_ref[...])
    for j in range(2 * PEER_HEADS):
        for lh in range(nlh):
            q_scr[j, lh] = q[lh * LANES:(lh + 1) * LANES, j * LANES:(j + 1) * LANES].astype(BF16)
    flat = flat_ref[...]
    cmask = cmask_ref[...]

    def piece(p, carry):
        hd = p // nlh
        lh = p % nlh
        gates, ids = _route_head(keys_ref, q_scr, hd, lh, (s_scr, i_scr, best_scr, eh_scr), flat, cmask)
        r0 = pl.multiple_of(hd * kk, kk)
        gall_scr[lh, pl.ds(r0, kk), :] = gates
        eall_scr[lh, pl.ds(r0, kk), :] = ids
        return carry

    lax.fori_loop(0, PEER_HEADS * nlh, piece, 0)
    for lh in range(nlh):
        e_ref[lh * LANES:(lh + 1) * LANES, :] = eall_scr[lh].T
        g_ref[lh * LANES:(lh + 1) * LANES, :] = gall_scr[lh].T


def _route(h, wq, keys, row0, t):
    d = h.shape[1]
    tg = 256
    g0 = row0 // tg
    nq = wq.shape[1]
    kk = PEER_TOPK
    hk = PEER_HEADS * kk
    nlh = tg // LANES
    flat, cmask = _candidate_tables()
    const = lambda shape: pl.BlockSpec(shape, lambda i: (0,) * len(shape))
    return pl.pallas_call(
        _route_kernel,
        grid=(t // tg,),
        in_specs=[pl.BlockSpec((tg, d), lambda i: (g0 + i, 0)),
                  const((d, nq)), const(keys.shape), const(flat.shape), const(cmask.shape)],
        out_specs=[pl.BlockSpec((tg, hk), lambda i: (i, 0)), pl.BlockSpec((tg, hk), lambda i: (i, 0))],
        out_shape=[jax.ShapeDtypeStruct((t, hk), jnp.int32), jax.ShapeDtypeStruct((t, hk), F32)],
        scratch_shapes=[pltpu.VMEM((2 * PEER_HEADS, nlh, LANES, LANES), BF16),
                        pltpu.VMEM((2 * kk, LANES), F32), pltpu.VMEM((2 * kk, LANES), jnp.int32),
                        pltpu.VMEM((kk, LANES), F32), pltpu.VMEM((kk, LANES), jnp.int32),
                        pltpu.VMEM((nlh, hk, LANES), jnp.int32), pltpu.VMEM((nlh, hk, LANES), F32)],
        compiler_params=pltpu.CompilerParams(
            dimension_semantics=("parallel",), vmem_limit_bytes=VMEM_LIMIT),
        name="route",
    )(h, wq, keys, flat, cmask)


PEER_SC_SHARE_PCT = 44
PEER_TC_FIRST_PCT = 53
SC_WORKERS = 32
SC_LANES = 16
SC_CHUNK = 32
SC_DBLK = 16
SC_EBLK = 4
SC_JUNROLL = 8


def _sc_mesh():
    return plsc.VectorSubcoreMesh(core_axis_name="c", subcore_axis_name="s")


def _sc_worker():
    return lax.axis_index("s") * 2 + lax.axis_index("c")


def _sc_pre(ids, h, u_tab, row0):
    ts, hk = ids.shape
    d = h.shape[1]
    per_w = ts // SC_WORKERS
    nch = hk // SC_CHUNK

    def body(ids_hbm, h_hbm, u_hbm, out_hbm, idx_v, h_v, rows_a, rows_b, pre_v, sem_a, sem_b):
        base = _sc_worker() * per_w
        lane = lax.iota(jnp.int32, SC_LANES)
        bufs = (rows_a, rows_b)
        sems = (sem_a, sem_b)

        def gather(c):
            return pltpu.async_copy(u_hbm.at[idx_v.at[pl.ds(c * SC_CHUNK, SC_CHUNK)]], bufs[c % 2], sems[c % 2])

        def token(i, carry):
            t = base + i
            pltpu.sync_copy(ids_hbm.at[t], idx_v)
            pltpu.sync_copy(h_hbm.at[row0 + t], h_v)
            cps = {0: gather(0)}
            for c in range(nch):
                if c + 1 < nch:
                    cps[c + 1] = gather(c + 1)
                cps[c].wait()
                rows = bufs[c % 2]
                for g in range(SC_CHUNK // SC_LANES):
                    def experts(q, vec, g=g, rows=rows):
                        e0 = g * SC_LANES + q * SC_EBLK

                        def span(jb, accs, rows=rows, e0=e0):
                            accs = list(accs)
                            for jj in range(SC_JUNROLL):
                                sl = pl.ds((jb * SC_JUNROLL + jj) * SC_LANES, SC_LANES)
                                hv = h_v[sl]
                                for i in range(SC_EBLK):
                                    accs[i] = accs[i] + rows[e0 + i, sl] * hv
                            return tuple(accs)
                        accs = lax.fori_loop(0, d // (SC_LANES * SC_JUNROLL), span,
                                             tuple(jnp.zeros((SC_LANES,), F32) for _ in range(SC_EBLK)))
                        for i in range(SC_EBLK):
                            vec = jnp.where(lane == q * SC_EBLK + i, jnp.sum(accs[i]), vec)
                        return vec
                    vec = lax.fori_loop(0, SC_LANES // SC_EBLK, experts, jnp.zeros((SC_LANES,), F32))
                    pre_v[pl.ds(c * SC_CHUNK + g * SC_LANES, SC_LANES)] = vec
            pltpu.sync_copy(pre_v, out_hbm.at[t])
            return carry

        lax.fori_loop(0, per_w, token, 0)

    return pl.kernel(
        body, out_type=jax.ShapeDtypeStruct((ts, hk), F32), mesh=_sc_mesh(),
        scratch_types=[pltpu.VMEM((hk,), jnp.int32), pltpu.VMEM((d,), F32),
                       pltpu.VMEM((SC_CHUNK, d), F32), pltpu.VMEM((SC_CHUNK, d), F32),
                       pltpu.VMEM((hk,), F32), pltpu.SemaphoreType.DMA, pltpu.SemaphoreType.DMA],
        compiler_params=pltpu.CompilerParams(needs_layout_passes=False),
        name="sc_pre",
    )(ids, h, u_tab)


def _sc_out(ids, act, v_tab):
    ts, hk = ids.shape
    d = v_tab.shape[1]
    per_w = ts // SC_WORKERS
    nch = hk // SC_CHUNK

    def body(ids_hbm, act_hbm, v_hbm, out_hbm, idx_v, act_v, rows_a, rows_b, y_v, sem_a, sem_b):
        base = _sc_worker() * per_w
        lane = lax.iota(jnp.int32, SC_LANES)
        bufs = (rows_a, rows_b)
        sems = (sem_a, sem_b)

        def gather(c):
            return pltpu.async_copy(v_hbm.at[idx_v.at[pl.ds(c * SC_CHUNK, SC_CHUNK)]], bufs[c % 2], sems[c % 2])

        def token(i, carry):
            t = base + i
            pltpu.sync_copy(ids_hbm.at[t], idx_v)
            pltpu.sync_copy(act_hbm.at[t], act_v)
            for j in range(d // SC_LANES):
                y_v[pl.ds(j * SC_LANES, SC_LANES)] = jnp.zeros((SC_LANES,), F32)
            cps = {0: gather(0)}
            for c in range(nch):
                if c + 1 < nch:
                    cps[c + 1] = gather(c + 1)
                cps[c].wait()
                rows = bufs[c % 2]
                for db in range(d // (SC_DBLK * SC_LANES)):
                    def expert(r, accs, c=c, rows=rows, db=db):
                        a = plsc.load_gather(act_v, [jnp.zeros((SC_LANES,), jnp.int32) + (c * SC_CHUNK + r)])
                        return tuple(
                            accs[j] + a * rows[r, pl.ds((db * SC_DBLK + j) * SC_LANES, SC_LANES)]
                            for j in range(SC_DBLK))
                    accs = lax.fori_loop(0, SC_CHUNK, expert,
                                         tuple(jnp.zeros((SC_LANES,), F32) for _ in range(SC_DBLK)))
                    for j in range(SC_DBLK):
                        plsc.addupdate(y_v.at[pl.ds((db * SC_DBLK + j) * SC_LANES, SC_LANES)], accs[j])
            pltpu.sync_copy(y_v, out_hbm.at[t])
            return carry

        lax.fori_loop(0, per_w, token, 0)

    return pl.kernel(
        body, out_type=jax.ShapeDtypeStruct((ts, d), F32), mesh=_sc_mesh(),
        scratch_types=[pltpu.VMEM((hk,), jnp.int32), pltpu.VMEM((hk,), F32),
                       pltpu.VMEM((SC_CHUNK, d), F32), pltpu.VMEM((SC_CHUNK, d), F32),
                       pltpu.VMEM((d,), F32), pltpu.SemaphoreType.DMA, pltpu.SemaphoreType.DMA],
        compiler_params=pltpu.CompilerParams(needs_layout_passes=False),
        name="sc_out",
    )(ids, act, v_tab)


def _act_kernel(pre_ref, gate_ref, after_ref, o_ref):
    del after_ref
    pre = pre_ref[...]
    o_ref[...] = 0.5 * pre * (1.0 + lax.erf(pre * (2.0 ** -0.5))) * gate_ref[...]


def _act(pre, gate, after):
    t, hk = pre.shape
    tm = 256
    assert t % tm == 0
    spec = pl.BlockSpec((tm, hk), lambda i: (i, 0))
    return pl.pallas_call(
        _act_kernel, grid=(t // tm,),
        in_specs=[spec, spec, pl.BlockSpec((8, LANES), lambda i: (0, 0))], out_specs=spec,
        out_shape=jax.ShapeDtypeStruct((t, hk), F32),
        compiler_params=pltpu.CompilerParams(dimension_semantics=("parallel",)),
        name="peer_act",
    )(pre, gate, after)


def _ln2_kernel(h_ref, y_ref, g_ref, b_ref, prev_ref, o_ref, *, alpha):
    del prev_ref
    o_ref[...] = _layernorm(alpha * h_ref[...] + y_ref[...], g_ref[...], b_ref[...])


def _ln2(h, y, g2, b2, alpha, row0, out_prev):
    t, d = y.shape
    tm = 256
    assert t % tm == 0 and row0 % tm == 0
    g0 = row0 // tm
    spec_full = pl.BlockSpec((tm, d), lambda i: (g0 + i, 0))
    vec = pl.BlockSpec((1, d), lambda i: (0, 0))
    return pl.pallas_call(
        functools.partial(_ln2_kernel, alpha=alpha), grid=(t // tm,),
        in_specs=[spec_full, pl.BlockSpec((tm, d), lambda i: (i, 0)), vec, vec,
                  pl.BlockSpec(memory_space=pl.ANY)],
        out_specs=spec_full,
        out_shape=jax.ShapeDtypeStruct(out_prev.shape, F32),
        input_output_aliases={4: 0},
        compiler_params=pltpu.CompilerParams(dimension_semantics=("parallel",)),
        name="peer_ln2",
    )(h, y, g2, b2, out_prev)


def _peer_kernel(h_cur_ref, h_nxt_ref, wq_ref, keys_ref, flat_ref, cmask_ref, uv_hbm, g2_ref, b2_ref, *rest,
                 tt, alpha, has_prev):
    (o_ref, q_scr, s_scr, i_scr, best_scr, eh_scr, eall_scr, idv_scr, ids_smem, gate_scr,
     buf_a, buf_b, sem, idsem, y_scr) = rest[1:] if has_prev else rest
    _peer_body(h_cur_ref, h_nxt_ref, wq_ref, keys_ref, flat_ref, cmask_ref, uv_hbm, g2_ref, b2_ref, o_ref,
               q_scr, s_scr, i_scr, best_scr, eh_scr, eall_scr, idv_scr, ids_smem, gate_scr,
               buf_a, buf_b, sem, idsem, y_scr, tt=tt, alpha=alpha)


def _peer_body(h_cur_ref, h_nxt_ref, wq_ref, keys_ref, flat_ref, cmask_ref, uv_hbm, g2_ref, b2_ref, o_ref,
               q_scr, s_scr, i_scr, best_scr, eh_scr, eall_scr, idv_scr, ids_smem, gate_scr,
               buf_a, buf_b, sem, idsem, y_scr, *, tt, alpha):
    s = pl.program_id(0)
    tg, d = h_cur_ref.shape
    nk, kk = PEER_NKEYS, PEER_TOPK
    hk = PEER_HEADS * kk
    nlh = tg // LANES
    nsub = tg // (2 * tt)
    assert nsub == PEER_HEADS * nlh
    nslab, sub = uv_hbm.shape[1], uv_hbm.shape[2]
    half = nslab // 2
    bufs = (buf_a, buf_b)
    last = pl.num_programs(0) - 1
    rslot = s % 3
    pslot = (s + 2) % 3
    eslot = (s + 1) % 3

    def issue(idslot, row0, slot):
        for t in range(tt):
            for k in range(hk):
                e = ids_smem[idslot, row0 + t, k]
                pltpu.make_async_copy(uv_hbm.at[e], bufs[slot].at[:, pl.ds((t * hk + k) * sub, sub), :],
                                      sem.at[slot]).start(priority=k % 2)

    def wait(slot):
        pltpu.make_async_copy(bufs[slot], bufs[slot], sem.at[slot]).wait()

    q = _dot(h_nxt_ref[...].astype(BF16), wq_ref[...])
    for j in range(2 * PEER_HEADS):
        for lh in range(nlh):
            q_scr[j, lh] = q[lh * LANES:(lh + 1) * LANES, j * LANES:(j + 1) * LANES].astype(BF16)
    flat = flat_ref[...]
    cmask = cmask_ref[...]

    def route_piece(piece):
        hd = piece // nlh
        lh = piece % nlh
        gates, ids = _route_head(keys_ref, q_scr, hd, lh, (s_scr, i_scr, best_scr, eh_scr), flat, cmask)
        r0 = pl.multiple_of(hd * kk, kk)
        gate_scr[rslot, lh, pl.ds(r0, kk), :] = gates
        eall_scr[lh, pl.ds(r0, kk), :] = ids

    lane = lax.broadcasted_iota(jnp.int32, (hk, LANES), 1)

    def compute(row0, slot):
        buf = bufs[slot]
        gt = gate_scr[eslot, row0 // LANES]
        lane0 = row0 % LANES
        for t in range(tt):
            hrow = h_cur_ref[pl.ds(row0 + t, 1), :]

            def rows(a, c):
                return buf[a, pl.ds(t * hk * sub + c, hk, stride=sub), :]

            part = None
            for j in range(half):
                for c in range(sub):
                    seg = j * sub + c
                    term = rows(j, c) * hrow[:, seg * LANES:(seg + 1) * LANES]
                    part = term if part is None else part + term
            pre = jnp.sum(part, axis=-1, keepdims=True)
            gate = jnp.sum(jnp.where(lane == lane0 + t, gt, 0.0), axis=-1, keepdims=True)
            act = 0.5 * pre * (1.0 + lax.erf(pre * (2.0 ** -0.5))) * gate
            yrow = slot * tt + t
            for j in range(half):
                for c in range(sub):
                    seg = j * sub + c
                    y_scr[yrow:yrow + 1, seg * LANES:(seg + 1) * LANES] = jnp.sum(
                        act * rows(half + j, c), axis=0, keepdims=True)

    def substep(j, carry):
        row0 = pl.multiple_of(j * 2 * tt, 2 * tt)
        issue(eslot, row0 + tt, 1)
        route_piece(j)
        wait(0)
        compute(row0, 0)
        wait(1)
        wrap = j == nsub - 1
        issue(jnp.where(wrap, pslot, eslot), jnp.where(wrap, 0, row0 + 2 * tt), 0)
        compute(row0 + tt, 1)
        z = alpha * h_cur_ref[pl.ds(row0, 2 * tt), :] + y_scr[...]
        o_ref[pl.ds(row0, 2 * tt), :] = _layernorm(z, g2_ref[...], b2_ref[...])
        return carry

    @pl.when(s < 2)
    def _():
        o_ref[...] = jnp.zeros_like(o_ref)

        def piece(p, carry):
            route_piece(p)
            return carry
        lax.fori_loop(0, nsub, piece, 0)

    @pl.when(s == 1)
    def _():
        issue(pslot, 0, 0)

    @pl.when(s >= 2)
    def _():
        lax.fori_loop(0, nsub, substep, 0)

    @pl.when(s == last)
    def _():
        wait(0)

    for lh in range(nlh):
        idv_scr[lh * LANES:(lh + 1) * LANES, :] = eall_scr[lh].T
    publish = pltpu.make_async_copy(idv_scr, ids_smem.at[rslot], idsem)
    publish.start()
    publish.wait()


def _peer(h1, wq, keys, uv_tab, g2, b2, alpha, row0=0, t=None, out_prev=None):
    d = h1.shape[1]
    t = h1.shape[0] if t is None else t
    tg = 256
    tt = 8
    ngrp = t // tg
    g0 = row0 // tg
    has_prev = out_prev is not None
    nq = wq.shape[1]
    kk = PEER_TOPK
    hk = PEER_HEADS * kk
    nlh = tg // LANES
    nslab, sub = uv_tab.shape[1], uv_tab.shape[2]
    flat, cmask = _candidate_tables()
    const = lambda shape: pl.BlockSpec(shape, lambda i: (0,) * len(shape))
    prev_specs = [pl.BlockSpec(memory_space=pl.ANY)] if has_prev else []
    prev_args = [out_prev] if has_prev else []
    return pl.pallas_call(
        functools.partial(_peer_kernel, tt=tt, alpha=alpha, has_prev=has_prev),
        grid=(ngrp + 2,),
        in_specs=[pl.BlockSpec((tg, d), lambda i: (g0 + jnp.maximum(i - 2, 0), 0)),
                  pl.BlockSpec((tg, d), lambda i: (g0 + jnp.minimum(i, ngrp - 1), 0)),
                  const((d, nq)), const(keys.shape), const(flat.shape), const(cmask.shape),
                  pl.BlockSpec(memory_space=pl.ANY),
                  const((1, d)), const((1, d))] + prev_specs,
        out_specs=pl.BlockSpec((tg, d), lambda i: (g0 + jnp.maximum(i - 2, 0), 0)),
        out_shape=jax.ShapeDtypeStruct(h1.shape, F32),
        input_output_aliases={9: 0} if has_prev else {},
        scratch_shapes=[pltpu.VMEM((2 * PEER_HEADS, nlh, LANES, LANES), BF16),
                        pltpu.VMEM((2 * kk, LANES), F32), pltpu.VMEM((2 * kk, LANES), jnp.int32),
                        pltpu.VMEM((kk, LANES), F32), pltpu.VMEM((kk, LANES), jnp.int32),
                        pltpu.VMEM((nlh, hk, LANES), jnp.int32),
                        pltpu.VMEM((tg, hk), jnp.int32),
                        pltpu.SMEM((3, tg, hk), jnp.int32),
                        pltpu.VMEM((3, nlh, hk, LANES), F32),
                        pltpu.VMEM((nslab, tt * hk * sub, LANES), F32),
                        pltpu.VMEM((nslab, tt * hk * sub, LANES), F32),
                        pltpu.SemaphoreType.DMA((2,)), pltpu.SemaphoreType.DMA,
                        pltpu.VMEM((2 * tt, d), F32)],
        compiler_params=pltpu.CompilerParams(
            dimension_semantics=("arbitrary",), vmem_limit_bytes=VMEM_LIMIT),
        name="peer",
    )(h1, h1, wq, keys, flat, cmask, uv_tab, g2, b2, *prev_args)


def _layer(h, w_in, conv_w, a_log, dt_bias, norm_w, f_bias, w_out_gdn, w_out_fox, w_o, ln1_g, ln1_b,
           peer_wq, peer_keys, peer_u, peer_v, ln2_g, ln2_b, alpha):
    b, s, d = h.shape
    t = b * s
    qk = GDN_HEADS * GDN_DK
    fw = FX_HEADS * FX_DH
    o_gz = 4 * qk
    o_ga = o_gz
    o_fq = o_ga + 2 * GDN_HEADS
    o_ff = o_fq + 3 * fw
    o_gate = o_ff + FX_HEADS
    w_big = jnp.concatenate([w_in[:, o_gate:], w_in[:, :o_gz]], axis=1).astype(BF16)
    pad = LANES - FX_DH
    w_fox = jnp.pad(w_in[:, o_fq:o_ff].reshape(d, 3 * FX_HEADS, FX_DH), ((0, 0), (0, 0), (0, pad)))
    w_fox = w_fox.reshape(d, 3 * FX_HEADS * LANES).astype(BF16)
    w_out_fox_p = jnp.pad(w_out_fox.reshape(FX_HEADS, FX_DH, d), ((0, 0), (0, pad), (0, 0)))
    w_out_fox_p = w_out_fox_p.reshape(FX_HEADS * LANES, d).astype(BF16)
    n_small = 2 * GDN_HEADS + FX_HEADS
    w_small = jnp.concatenate([w_in[:, o_ga:o_fq], w_in[:, o_ff:o_gate],
                               jnp.zeros((d, LANES - n_small), F32)], axis=1)
    params = jnp.zeros((8, LANES), F32)
    params = params.at[0, :GDN_HEADS].set(a_log).at[1, :GDN_HEADS].set(dt_bias)
    params = params.at[2, 2 * GDN_HEADS:n_small].set(f_bias)

    x2 = h.reshape(t, d)
    proj = _in_proj(x2, w_big, F32, "in_proj")
    pf = _in_proj(x2, w_fox, BF16, "in_proj_fox")
    gexp, bexp, c, ct = _prep(h, w_small, params)
    proj3 = proj.reshape(b, s, proj.shape[1])
    gdn0 = 2 * d // LANES
    oa = _gdn(proj3, gexp, bexp, conv_w, norm_w.reshape(1, LANES),
              (gdn0, gdn0 + GDN_HEADS, gdn0 + 2 * GDN_HEADS, gdn0 + 3 * GDN_HEADS))
    ob = _fox(pf.reshape(b, s, pf.shape[1]), c, ct)
    h1 = _mix(oa.reshape(t, qk), ob.reshape(t, FX_HEADS * LANES), proj, x2,
              w_out_gdn.astype(BF16), w_out_fox_p, w_o.astype(BF16),
              ln1_g.reshape(1, d), ln1_b.reshape(1, d), alpha)
    keys = peer_keys.reshape(2 * PEER_HEADS, PEER_NKEYS, peer_keys.shape[-1]).astype(BF16)
    ne = peer_u.shape[0]
    uslabs = d // (PEER_SUB * LANES)
    uv_tab = jnp.concatenate([peer_u.reshape(ne, uslabs, PEER_SUB, LANES),
                              peer_v.reshape(ne, uslabs, PEER_SUB, LANES)], axis=1)
    wq = peer_wq.astype(BF16)
    g2, b2 = ln2_g.reshape(1, d), ln2_b.reshape(1, d)
    t_sc = (t * PEER_SC_SHARE_PCT // 100) // 256 * 256
    t_tc = t - t_sc
    if t_sc:
        ids_sc, gate_sc = _route(h1, wq, keys, t_tc, t_sc)
        t_a = (t_tc * PEER_TC_FIRST_PCT // 100) // 256 * 256
        pre = _sc_pre(ids_sc, h1, peer_u, t_tc)
        out = _peer(h1, wq, keys, uv_tab, g2, b2, alpha, 0, t_a)
        y_sc = _sc_out(ids_sc, _act(pre, gate_sc, out), peer_v)
        out = _peer(h1, wq, keys, uv_tab, g2, b2, alpha, t_a, t_tc - t_a, out)
        out = _ln2(h1, y_sc, g2, b2, alpha, t_tc, out)
    else:
        out = _peer(h1, wq, keys, uv_tab, g2, b2, alpha)
    return out.reshape(b, s, d)


def kernel(x, w_in, gdn_conv_w, gdn_a_log, gdn_dt_bias, gdn_norm_w, fox_f_bias, w_out_gdn, w_out_fox, w_o,
           ln1_g, ln1_b, peer_wq, peer_keys, peer_u, peer_v, ln2_g, ln2_b):
    depth = w_in.shape[0]
    alpha = (2.0 * depth) ** 0.25
    params = (w_in, gdn_conv_w, gdn_a_log, gdn_dt_bias, gdn_norm_w, fox_f_bias, w_out_gdn, w_out_fox, w_o,
              ln1_g, ln1_b, peer_wq, peer_keys, peer_u, peer_v, ln2_g, ln2_b)

    def layer_slice(p, l):
        return p.reshape(p.shape[1:]) if depth == 1 else p[l]

    h = x
    for l in range(depth):
        h = _layer(h, *(layer_slice(p, l) for p in params), alpha)
    return h
```

```python
import functools

import jax
import jax.numpy as jnp
from jax import lax
from jax.experimental import pallas as pl
from jax.experimental.pallas import tpu as pltpu
from jax.experimental.pallas import tpu_sc as plsc

F32 = jnp.float32
BF16 = jnp.bfloat16
HI = lax.Precision.HIGHEST

LANES = 128
CHUNK = 64
GDN_HEADS = 4
GDN_DK = 128
FX_HEADS = 8
FX_DH = 64
PEER_HEADS = 8
PEER_NKEYS = 128
PEER_TOPK = 16
PEER_SUB = 4
LN_EPS = 1e-5
VMEM_LIMIT = 48 * 1024 * 1024


def _dot(a, b, prec=None):
    return jnp.dot(a, b, preferred_element_type=F32, precision=prec)


def _dot_nt(a, b, prec=None):
    return lax.dot_general(a, b, (((1,), (1,)), ((), ())), preferred_element_type=F32, precision=prec)


def _dot_tn(a, b, prec=None):
    return lax.dot_general(a, b, (((0,), (0,)), ((), ())), preferred_element_type=F32, precision=prec)


def _softplus(x):
    return jnp.maximum(x, 0.0) + jnp.log1p(jnp.exp(-jnp.abs(x)))


def _layernorm(z, g, b):
    mu = jnp.mean(z, axis=-1, keepdims=True)
    zc = z - mu
    var = jnp.mean(zc * zc, axis=-1, keepdims=True)
    return zc * lax.rsqrt(var + LN_EPS) * g + b


def _mm_kernel(x_ref, w_ref, o_ref):
    o_ref[...] = _dot(x_ref[...].astype(BF16), w_ref[...]).astype(o_ref.dtype)


def _in_proj(x2, w_big, out_dtype, name):
    t, d = x2.shape
    n = w_big.shape[1]
    tm = min(1024, t)
    tn = 512
    return pl.pallas_call(
        _mm_kernel,
        grid=(t // tm, n // tn),
        in_specs=[pl.BlockSpec((tm, d), lambda i, j: (i, 0)),
                  pl.BlockSpec((d, tn), lambda i, j: (0, j))],
        out_specs=pl.BlockSpec((tm, tn), lambda i, j: (i, j)),
        out_shape=jax.ShapeDtypeStruct((t, n), out_dtype),
        compiler_params=pltpu.CompilerParams(
            dimension_semantics=("parallel", "parallel"), vmem_limit_bytes=VMEM_LIMIT),
        name=name,
    )(x2, w_big)


def _prep_kernel(x_ref, w_ref, par_ref, gexp_ref, bexp_ref, c_ref, ct_ref, carry_ref):
    ts = x_ref.shape[1]

    @pl.when(pl.program_id(1) == 0)
    def _():
        carry_ref[...] = jnp.zeros_like(carry_ref)

    small = _dot(x_ref[0], w_ref[...], HI)
    a_log = par_ref[0:1, :]
    dt_bias = par_ref[1:2, :]
    f_bias = par_ref[2:3, :]
    g = -jnp.exp(a_log) * _softplus(small + dt_bias)
    beta = jax.nn.sigmoid(small)
    lane = lax.broadcasted_iota(jnp.int32, (ts, LANES), 1)
    log_f = jnp.where((lane >= 8) & (lane < 16), -_softplus(-(small + f_bias)), 0.0)
    row = lax.broadcasted_iota(jnp.int32, (ts, ts), 0)
    col = lax.broadcasted_iota(jnp.int32, (ts, ts), 1)
    tril = (row >= col).astype(F32)
    c = _dot(tril, log_f, HI) + carry_ref[...]
    carry_ref[...] = c[ts - 1:ts, :]
    c_ref[0] = c
    ct_ref[0] = c.T[8:16, :]
    gexp_ref[0] = jnp.concatenate(
        [jnp.broadcast_to(g[:, h:h + 1], (ts, LANES)) for h in range(GDN_HEADS)], axis=1)
    bexp_ref[0] = jnp.concatenate(
        [jnp.broadcast_to(beta[:, GDN_HEADS + h:GDN_HEADS + h + 1], (ts, LANES)) for h in range(GDN_HEADS)], axis=1)


def _prep(x, w_small, params):
    b, s, d = x.shape
    ts = min(512, s)
    hw = GDN_HEADS * LANES
    return pl.pallas_call(
        _prep_kernel,
        grid=(b, s // ts),
        in_specs=[pl.BlockSpec((1, ts, d), lambda i, j: (i, j, 0)),
                  pl.BlockSpec((d, LANES), lambda i, j: (0, 0)),
                  pl.BlockSpec((8, LANES), lambda i, j: (0, 0))],
        out_specs=[pl.BlockSpec((1, ts, hw), lambda i, j: (i, j, 0)),
                   pl.BlockSpec((1, ts, hw), lambda i, j: (i, j, 0)),
                   pl.BlockSpec((1, ts, LANES), lambda i, j: (i, j, 0)),
                   pl.BlockSpec((1, 8, ts), lambda i, j: (i, 0, j))],
        out_shape=[jax.ShapeDtypeStruct((b, s, hw), F32),
                   jax.ShapeDtypeStruct((b, s, hw), F32),
                   jax.ShapeDtypeStruct((b, s, LANES), F32),
                   jax.ShapeDtypeStruct((b, 8, s), F32)],
        scratch_shapes=[pltpu.VMEM((1, LANES), F32)],
        compiler_params=pltpu.CompilerParams(
            dimension_semantics=("parallel", "arbitrary"), vmem_limit_bytes=VMEM_LIMIT),
        name="prep",
    )(x, w_small, params)


def _each(fn, *lists):
    return [fn(*args) for args in zip(*lists)]


def _unit_lower_inverse(ms, masks):
    eye, blk16, blk32 = masks
    hi = lambda a, b: _dot(a, b, HI)
    n1 = _each(lambda m: -jnp.where(blk16, m, 0.0), ms)
    l1 = _each(lambda m: jnp.where(blk32 & jnp.logical_not(blk16), m, 0.0), ms)
    l2 = _each(lambda m: jnp.where(blk32, 0.0, m), ms)
    n2 = _each(hi, n1, n1)
    yield
    p = _each(lambda a, b: hi(eye + a, eye + b), n1, n2)
    n4 = _each(hi, n2, n2)
    yield
    p = _each(lambda a, b: hi(a, eye + b), p, n4)
    n8 = _each(hi, n4, n4)
    yield
    d_inv = _each(lambda a, b: hi(a, eye + b), p, n8)
    yield
    dl = _each(hi, d_inv, l1)
    yield
    a32 = _each(lambda a, b: a - hi(b, a), d_inv, dl)
    yield
    al = _each(hi, a32, l2)
    yield
    return _each(lambda a, b: a - hi(b, a), a32, al)


def _emit_zipped(main, side):
    live = [main, side]
    while live:
        for gen in list(live):
            try:
                next(gen)
            except StopIteration:
                live.remove(gen)


def _gdn_kernel(q_ref, k_ref, v_ref, z_ref, g_ref, b_ref, cwq_ref, cwk_ref, cwv_ref, nw_ref,
                o_ref, qn, kn, vn, st, sol_s, qk_s, qg_s, kd_s, gl_s):
    s = q_ref.shape[1]
    c = CHUNK
    nh = q_ref.shape[2] // LANES
    row = lax.broadcasted_iota(jnp.int32, (s, LANES), 0)

    def conv_silu(x, w):
        y = x * w[3:4, :]
        for sh in (1, 2, 3):
            xs = jnp.where(row >= sh, pltpu.roll(x, sh, axis=0), 0.0)
            y = y + xs * w[3 - sh:4 - sh, :]
        return y * jax.nn.sigmoid(y)

    def l2norm(x):
        return x * lax.rsqrt(jnp.sum(x * x, axis=-1, keepdims=True) + 1e-6)

    for j in range(nh):
        hs = slice(j * LANES, (j + 1) * LANES)
        qn[:, hs] = l2norm(conv_silu(q_ref[0, :, hs], cwq_ref[:, hs])) * (GDN_DK ** -0.5)
        kn[:, hs] = l2norm(conv_silu(k_ref[0, :, hs], cwk_ref[:, hs]))
        vn[:, hs] = conv_silu(v_ref[0, :, hs], cwv_ref[:, hs])
    st[...] = jnp.zeros_like(st)

    ri = lax.broadcasted_iota(jnp.int32, (c, c), 0)
    ci = lax.broadcasted_iota(jnp.int32, (c, c), 1)
    tril = ri >= ci
    strict = ri > ci
    t_inc = tril.astype(F32)
    eye = (ri == ci).astype(F32)
    blk16 = (ri >> 4) == (ci >> 4)
    blk32 = (ri >> 5) == (ci >> 5)
    l2 = lax.broadcasted_iota(jnp.int32, (c, 2 * LANES), 0)
    j2 = lax.broadcasted_iota(jnp.int32, (c, 2 * LANES), 1)
    ux = jnp.where((j2 >= c) | (l2 > j2), 1.0, 0.0).astype(F32)
    nw = nw_ref[...]

    heads = list(range(nh))
    lanes_of = [slice(j * LANES, (j + 1) * LANES) for j in heads]

    def local_stage(ns):
        pairs = [(i, j) for i in range(len(ns)) for j in heads]
        rows = [pl.ds(pl.multiple_of(n * c, c), c) for n in ns]
        q = [qn[rows[i], lanes_of[j]] for i, j in pairs]
        k = [kn[rows[i], lanes_of[j]] for i, j in pairs]
        v = [vn[rows[i], lanes_of[j]] for i, j in pairs]
        gb = [g_ref[0, rows[i], lanes_of[j]] for i, j in pairs]
        bb = [b_ref[0, rows[i], lanes_of[j]] for i, j in pairs]
        d = _each(lambda g: _dot(t_inc, jnp.concatenate([g, g], axis=1) * ux, HI), gb)
        kb = _each(lambda a, b: a * b, k, bb)
        kk = _each(lambda a, b: _dot_nt(a, b, HI), kb, k)
        qk = _each(lambda a, b: _dot_nt(a.astype(BF16), b.astype(BF16)), q, k)
        yield
        gc = [x[:, LANES:] for x in d]
        decay = [jnp.where(tril, jnp.exp(x[:, :c]), 0.0) for x in d]
        m = _each(lambda a, b: jnp.where(strict, a * b, 0.0), kk, decay)
        a_inv = yield from _unit_lower_inverse(m, (eye, blk16, blk32))
        yield
        egc = _each(jnp.exp, gc)
        rhs = _each(lambda vv, b, kbb, e: jnp.concatenate([vv * b, kbb * e], axis=1), v, bb, kb, egc)
        sol = _each(lambda a, r: _dot(a, r, HI), a_inv, rhs)
        gl = [x[c - 1:c, :] for x in gc]
        yield
        for p, (i, j) in enumerate(pairs):
            n = ns[i]
            qk_s[n, j] = (qk[p] * decay[p]).astype(BF16)
            qg_s[n, j] = (q[p] * egc[p]).astype(BF16)
            kd_s[n, j] = (k[p] * jnp.exp(gl[p] - gc[p])).astype(BF16)
            gl_s[n, j] = gl[p]
            sol_s[n, j] = sol[p]

    def state_stage(ns):
        for n in ns:
            rows = pl.ds(pl.multiple_of(n * c, c), c)
            state = [st[j] for j in heads]
            state_b = [x.astype(BF16) for x in state]
            v_new = [sol_s[n, j, :, :LANES] - _dot(sol_s[n, j, :, LANES:].astype(BF16), state_b[j])
                     for j in heads]
            yield
            v_new_b = [x.astype(BF16) for x in v_new]
            o = [_dot(qg_s[n, j], state_b[j]) + _dot(qk_s[n, j], v_new_b[j]) for j in heads]
            new_state = [state[j] * jnp.exp(gl_s[n, j]) + _dot_tn(kd_s[n, j], v_new_b[j]) for j in heads]
            yield
            for j in heads:
                st[j] = new_state[j]
                on = o[j] * lax.rsqrt(jnp.mean(o[j] * o[j], axis=-1, keepdims=True) + 1e-6) * nw
                z = z_ref[0, rows, lanes_of[j]]
                o_ref[0, rows, lanes_of[j]] = on * (z * jax.nn.sigmoid(z))
            yield

    nchunks = s // c
    group = 4
    ngroups = nchunks // group

    def chunks_of(gidx):
        return [gidx * group + i for i in range(group)]

    def run(gen):
        for _ in gen:
            pass

    run(local_stage(chunks_of(jnp.int32(0))))

    def both(gidx, carry):
        _emit_zipped(local_stage(chunks_of(gidx)), state_stage(chunks_of(gidx - 1)))
        return carry

    lax.fori_loop(1, ngroups, both, 0)
    run(state_stage(chunks_of(jnp.int32(ngroups - 1))))


def _gdn(proj3, gexp, bexp, conv_w, norm_w, cols):
    b, s, _ = proj3.shape
    nh = 2
    nc = s // CHUNK
    wd = nh * LANES
    cq, ck, cv, cz = (c0 // nh for c0 in cols)

    def blk(c0):
        return pl.BlockSpec((1, s, wd), lambda i, h: (i, 0, c0 + h))

    def cw(c0):
        return pl.BlockSpec((conv_w.shape[0], wd), lambda i, h: (0, c0 + h))

    head = pl.BlockSpec((1, s, wd), lambda i, h: (i, 0, h))
    return pl.pallas_call(
        _gdn_kernel,
        grid=(b, GDN_HEADS // nh),
        in_specs=[blk(cq), blk(ck), blk(cv), blk(cz), head, head,
                  cw(0), cw(GDN_HEADS // nh), cw(2 * GDN_HEADS // nh),
                  pl.BlockSpec((1, LANES), lambda i, h: (0, 0))],
        out_specs=head,
        out_shape=jax.ShapeDtypeStruct((b, s, GDN_HEADS * LANES), F32),
        scratch_shapes=[pltpu.VMEM((s, wd), F32), pltpu.VMEM((s, wd), F32),
                        pltpu.VMEM((s, wd), F32), pltpu.VMEM((nh, GDN_DK, LANES), F32),
                        pltpu.VMEM((nc, nh, CHUNK, 2 * LANES), F32), pltpu.VMEM((nc, nh, CHUNK, CHUNK), BF16),
                        pltpu.VMEM((nc, nh, CHUNK, LANES), BF16), pltpu.VMEM((nc, nh, CHUNK, LANES), BF16),
                        pltpu.VMEM((nc, nh, 1, LANES), F32)],
        compiler_params=pltpu.CompilerParams(
            dimension_semantics=("parallel", "parallel"), vmem_limit_bytes=VMEM_LIMIT),
        name="gdn",
    )(proj3, proj3, proj3, proj3, gexp, bexp, conv_w, conv_w, conv_w, norm_w)


def _fox_kernel(q_ref, k_ref, v_ref, c_ref, ct_ref, o_ref, *, tk):
    tq = q_ref.shape[1]
    nj = q_ref.shape[2] // LANES
    g = pl.program_id(1)
    qi = pl.program_id(2)
    q = q_ref[0]
    cblk = c_ref[0]
    lane = lax.broadcasted_iota(jnp.int32, (tq, LANES), 1)
    qpos = qi * tq + lax.broadcasted_iota(jnp.int32, (tq, tk), 0)
    kofs = lax.broadcasted_iota(jnp.int32, (tq, tk), 1)
    heads = list(range(nj))
    lanes_of = [slice(j * LANES, (j + 1) * LANES) for j in heads]
    ccol = [jnp.sum(jnp.where(lane == 8 + g * nj + j, cblk, 0.0), axis=-1, keepdims=True) for j in heads]
    qh = [q[:, hs] * jnp.asarray(FX_DH ** -0.5, BF16) for hs in lanes_of]

    def body(kv, carry):
        k0 = pl.multiple_of(kv * tk, tk)
        causal = qpos >= k0 + kofs
        kblk = k_ref[0, pl.ds(k0, tk), :]
        vblk = v_ref[0, pl.ds(k0, tk), :]
        sc = [_dot_nt(qh[j], kblk[:, lanes_of[j]]) for j in heads]
        crow = [ct_ref[0, pl.ds(g * nj + j, 1), pl.ds(k0, tk)] for j in heads]
        sc = [jnp.where(causal, sc[j] + ccol[j] - crow[j], -1e30) for j in heads]
        m_new = [jnp.maximum(carry[j][0], jnp.max(sc[j], axis=-1, keepdims=True)) for j in heads]
        a = [jnp.exp(carry[j][0] - m_new[j]) for j in heads]
        p = [jnp.exp(sc[j] - m_new[j]) for j in heads]
        l = [a[j] * carry[j][1] + jnp.sum(p[j], axis=-1, keepdims=True) for j in heads]
        acc = [a[j] * carry[j][2] + _dot(p[j].astype(BF16), vblk[:, lanes_of[j]]) for j in heads]
        return tuple((m_new[j], l[j], acc[j]) for j in heads)

    init = tuple((jnp.full((tq, 1), -1e30, F32), jnp.zeros((tq, 1), F32), jnp.zeros((tq, LANES), F32))
                 for _ in heads)
    nkv = (qi * tq + tq - 1) // tk + 1
    res = lax.fori_loop(0, nkv, body, init)
    o_ref[0] = jnp.concatenate([acc / l for _, l, acc in res], axis=1)


def _fox(pf3, c, ct):
    b, s, _ = pf3.shape
    wd = 4 * LANES
    tq = min(128, s)
    tk = min(256, s)
    ngrp = FX_HEADS * LANES // wd
    return pl.pallas_call(
        functools.partial(_fox_kernel, tk=tk),
        grid=(b, ngrp, s // tq),
        in_specs=[pl.BlockSpec((1, tq, wd), lambda i, h, t: (i, t, h)),
                  pl.BlockSpec((1, s, wd), lambda i, h, t: (i, 0, ngrp + h)),
                  pl.BlockSpec((1, s, wd), lambda i, h, t: (i, 0, 2 * ngrp + h)),
                  pl.BlockSpec((1, tq, LANES), lambda i, h, t: (i, t, 0)),
                  pl.BlockSpec((1, 8, s), lambda i, h, t: (i, 0, 0))],
        out_specs=pl.BlockSpec((1, tq, wd), lambda i, h, t: (i, t, h)),
        out_shape=jax.ShapeDtypeStruct((b, s, ngrp * wd), F32),
        compiler_params=pltpu.CompilerParams(
            dimension_semantics=("parallel", "parallel", "parallel"), vmem_limit_bytes=VMEM_LIMIT),
        name="fox",
    )(pf3, pf3, pf3, c, ct)


def _mix_kernel(oa_ref, ob_ref, ga_ref, gb_ref, x_ref, wa_ref, wb_ref, wo_ref, g1_ref, b1_ref, o_ref, *, alpha):
    ya = _dot(oa_ref[...].astype(BF16), wa_ref[...])
    yb = _dot(ob_ref[...].astype(BF16), wb_ref[...])
    mix = jax.nn.sigmoid(ga_ref[...]) * ya + jax.nn.sigmoid(gb_ref[...]) * yb
    z = alpha * x_ref[...] + _dot(mix.astype(BF16), wo_ref[...])
    o_ref[...] = _layernorm(z, g1_ref[...], b1_ref[...])


def _mix(oa, ob, proj, x2, wa, wb, wo, g1, b1, alpha):
    t, d = x2.shape
    tm = min(512, t)
    w, w2 = oa.shape[1], ob.shape[1]
    full = lambda r, c: pl.BlockSpec((r, c), lambda i: (0, 0))
    return pl.pallas_call(
        functools.partial(_mix_kernel, alpha=alpha),
        grid=(t // tm,),
        in_specs=[pl.BlockSpec((tm, w), lambda i: (i, 0)),
                  pl.BlockSpec((tm, w2), lambda i: (i, 0)),
                  pl.BlockSpec((tm, d), lambda i: (i, 0)),
                  pl.BlockSpec((tm, d), lambda i: (i, 1)),
                  pl.BlockSpec((tm, d), lambda i: (i, 0)),
                  full(w, d), full(w2, d), full(d, d), full(1, d), full(1, d)],
        out_specs=pl.BlockSpec((tm, d), lambda i: (i, 0)),
        out_shape=jax.ShapeDtypeStruct((t, d), F32),
        compiler_params=pltpu.CompilerParams(
            dimension_semantics=("parallel",), vmem_limit_bytes=VMEM_LIMIT),
        name="mix",
    )(oa, ob, proj, proj, x2, wa, wb, wo, g1, b1)


def _route_head(keys_ref, q_scr, hd, lh, scr, flat, cmask):
    s_scr, i_scr, best_scr, eh_scr = scr
    nk, kk = PEER_NKEYS, PEER_TOPK
    iota_k = lax.broadcasted_iota(jnp.int32, (nk, LANES), 0)
    neg = jnp.float32(-jnp.inf)
    for p in range(2):
        vals = _dot_nt(keys_ref[hd * 2 + p], q_scr[hd * 2 + p, lh])
        for r in range(kk):
            m = jnp.max(vals, axis=0, keepdims=True)
            am = jnp.min(jnp.where(vals == m, iota_k, nk), axis=0, keepdims=True)
            s_scr[p * kk + r:p * kk + r + 1, :] = m
            i_scr[p * kk + r:p * kk + r + 1, :] = am
            vals = jnp.where(iota_k == am, neg, vals)
    s1 = s_scr[kk:kk + 8, :]
    i1 = i_scr[kk:kk + 8, :]
    cand = [s_scr[0:1, :] + s_scr[kk:2 * kk, :]]
    cidx = [i_scr[0:1, :] * nk + i_scr[kk:2 * kk, :]]
    for a in range(1, 8):
        cand.append(s_scr[a:a + 1, :] + s1)
        cidx.append(i_scr[a:a + 1, :] * nk + i1)
    cand.append(s_scr[8:kk, :] + s_scr[kk:kk + 1, :])
    cidx.append(i_scr[8:kk, :] * nk + i_scr[kk:kk + 1, :])
    vals = jnp.concatenate(cand, axis=0) + cmask
    cidx = jnp.concatenate(cidx, axis=0)
    for r in range(kk):
        m = jnp.max(vals, axis=0, keepdims=True)
        am = jnp.min(jnp.where(vals == m, flat, 2 * kk * kk), axis=0, keepdims=True)
        sel = flat == am
        best_scr[r:r + 1, :] = m
        eh_scr[r:r + 1, :] = jnp.max(jnp.where(sel, cidx, -1), axis=0, keepdims=True)
        vals = jnp.where(sel, neg, vals)
    bs = best_scr[...]
    ex = jnp.exp(bs - bs[0:1, :])
    return ex / jnp.sum(ex, axis=0, keepdims=True), eh_scr[...]


def _candidate_tables():
    kk = PEER_TOPK
    pairs = [(0, bb) for bb in range(kk)]
    for a in range(1, 8):
        pairs += [(a, bb) for bb in range(8)]
    pairs += [(a, 0) for a in range(8, kk)]
    real = [(a + 1) * (bb + 1) <= kk for a, bb in pairs]
    flat = [a * kk + bb if ok else kk * kk + r for r, ((a, bb), ok) in enumerate(zip(pairs, real))]
    flat = jnp.broadcast_to(jnp.asarray(flat, jnp.int32)[:, None], (len(pairs), LANES))
    cmask = jnp.broadcast_to(jnp.asarray([0.0 if ok else -jnp.inf for ok in real], F32)[:, None],
                             (len(pairs), LANES))
    return flat, cmask


def _route_kernel(h_ref, wq_ref, keys_ref, flat_ref, cmask_ref, e_ref, g_ref,
                  q_scr, s_scr, i_scr, best_scr, eh_scr, eall_scr, gall_scr):
    tg = h_ref.shape[0]
    kk = PEER_TOPK
    nlh = tg // LANES
    q = _dot(h_ref[...].astype(BF16), wq_ref[...])
    for j in range(2 * PEER_HEADS):
        for lh in range(nlh):
            q_scr[j, lh] = q[lh * LANES:(lh + 1) * LANES, j * LANES:(j + 1) * LANES].astype(BF16)
    flat = flat_ref[...]
    cmask = cmask_ref[...]

    def piece(p, carry):
        hd = p // nlh
        lh = p % nlh
        gates, ids = _route_head(keys_ref, q_scr, hd, lh, (s_scr, i_scr, best_scr, eh_scr), flat, cmask)
        r0 = pl.multiple_of(hd * kk, kk)
        gall_scr[lh, pl.ds(r0, kk), :] = gates
        eall_scr[lh, pl.ds(r0, kk), :] = ids
        return carry

    lax.fori_loop(0, PEER_HEADS * nlh, piece, 0)
    for lh in range(nlh):
        e_ref[lh * LANES:(lh + 1) * LANES, :] = eall_scr[lh].T
        g_ref[lh * LANES:(lh + 1) * LANES, :] = gall_scr[lh].T


def _route(h, wq, keys, row0, t):
    d = h.shape[1]
    tg = 256
    g0 = row0 // tg
    nq = wq.shape[1]
    kk = PEER_TOPK
    hk = PEER_HEADS * kk
    nlh = tg // LANES
    flat, cmask = _candidate_tables()
    const = lambda shape: pl.BlockSpec(shape, lambda i: (0,) * len(shape))
    return pl.pallas_call(
        _route_kernel,
        grid=(t // tg,),
        in_specs=[pl.BlockSpec((tg, d), lambda i: (g0 + i, 0)),
                  const((d, nq)), const(keys.shape), const(flat.shape), const(cmask.shape)],
        out_specs=[pl.BlockSpec((tg, hk), lambda i: (i, 0)), pl.BlockSpec((tg, hk), lambda i: (i, 0))],
        out_shape=[jax.ShapeDtypeStruct((t, hk), jnp.int32), jax.ShapeDtypeStruct((t, hk), F32)],
        scratch_shapes=[pltpu.VMEM((2 * PEER_HEADS, nlh, LANES, LANES), BF16),
                        pltpu.VMEM((2 * kk, LANES), F32), pltpu.VMEM((2 * kk, LANES), jnp.int32),
                        pltpu.VMEM((kk, LANES), F32), pltpu.VMEM((kk, LANES), jnp.int32),
                        pltpu.VMEM((nlh, hk, LANES), jnp.int32), pltpu.VMEM((nlh, hk, LANES), F32)],
        compiler_params=pltpu.CompilerParams(
            dimension_semantics=("parallel",), vmem_limit_bytes=VMEM_LIMIT),
        name="route",
    )(h, wq, keys, flat, cmask)


PEER_SC_SHARE_PCT = 46
PEER_TC_FIRST_PCT = 49
SC_WORKERS = 32
SC_LANES = 16
SC_CHUNK = 32
SC_DBLK = 16
SC_EBLK = 4
SC_JUNROLL = 8


def _sc_mesh():
    return plsc.VectorSubcoreMesh(core_axis_name="c", subcore_axis_name="s")


def _sc_worker():
    return lax.axis_index("s") * 2 + lax.axis_index("c")


def _sc_pre(ids, h, u_tab, row0):
    ts, hk = ids.shape
    d = h.shape[1]
    per_w = ts // SC_WORKERS
    nch = hk // SC_CHUNK

    def body(ids_hbm, h_hbm, u_hbm, out_hbm, idx_v, h_v, rows_a, rows_b, pre_v, sem_a, sem_b):
        base = _sc_worker() * per_w
        lane = lax.iota(jnp.int32, SC_LANES)
        bufs = (rows_a, rows_b)
        sems = (sem_a, sem_b)

        def gather(c):
            return pltpu.async_copy(u_hbm.at[idx_v.at[pl.ds(c * SC_CHUNK, SC_CHUNK)]], bufs[c % 2], sems[c % 2])

        def token(i, carry):
            t = base + i
            pltpu.sync_copy(ids_hbm.at[t], idx_v)
            pltpu.sync_copy(h_hbm.at[row0 + t], h_v)
            cps = {0: gather(0)}
            for c in range(nch):
                if c + 1 < nch:
                    cps[c + 1] = gather(c + 1)
                cps[c].wait()
                rows = bufs[c % 2]
                for g in range(SC_CHUNK // SC_LANES):
                    def experts(q, vec, g=g, rows=rows):
                        e0 = g * SC_LANES + q * SC_EBLK

                        def span(jb, accs, rows=rows, e0=e0):
                            accs = list(accs)
                            for jj in range(SC_JUNROLL):
                                sl = pl.ds((jb * SC_JUNROLL + jj) * SC_LANES, SC_LANES)
                                hv = h_v[sl]
                                for i in range(SC_EBLK):
                                    accs[i] = accs[i] + rows[e0 + i, sl] * hv
                            return tuple(accs)
                        accs = lax.fori_loop(0, d // (SC_LANES * SC_JUNROLL), span,
                                             tuple(jnp.zeros((SC_LANES,), F32) for _ in range(SC_EBLK)))
                        for i in range(SC_EBLK):
                            vec = jnp.where(lane == q * SC_EBLK + i, jnp.sum(accs[i]), vec)
                        return vec
                    vec = lax.fori_loop(0, SC_LANES // SC_EBLK, experts, jnp.zeros((SC_LANES,), F32))
                    pre_v[pl.ds(c * SC_CHUNK + g * SC_LANES, SC_LANES)] = vec
            pltpu.sync_copy(pre_v, out_hbm.at[t])
            return carry

        lax.fori_loop(0, per_w, token, 0)

    return pl.kernel(
        body, out_type=jax.ShapeDtypeStruct((ts, hk), F32), mesh=_sc_mesh(),
        scratch_types=[pltpu.VMEM((hk,), jnp.int32), pltpu.VMEM((d,), F32),
                       pltpu.VMEM((SC_CHUNK, d), F32), pltpu.VMEM((SC_CHUNK, d), F32),
                       pltpu.VMEM((hk,), F32), pltpu.SemaphoreType.DMA, pltpu.SemaphoreType.DMA],
        compiler_params=pltpu.CompilerParams(needs_layout_passes=False),
        name="sc_pre",
    )(ids, h, u_tab)


def _sc_out(ids, act, v_tab):
    ts, hk = ids.shape
    d = v_tab.shape[1]
    per_w = ts // SC_WORKERS
    nch = hk // SC_CHUNK

    def body(ids_hbm, act_hbm, v_hbm, out_hbm, idx_v, act_v, rows_a, rows_b, y_v, sem_a, sem_b):
        base = _sc_worker() * per_w
        lane = lax.iota(jnp.int32, SC_LANES)
        bufs = (rows_a, rows_b)
        sems = (sem_a, sem_b)

        def gather(c):
            return pltpu.async_copy(v_hbm.at[idx_v.at[pl.ds(c * SC_CHUNK, SC_CHUNK)]], bufs[c % 2], sems[c % 2])

        def token(i, carry):
            t = base + i
            pltpu.sync_copy(ids_hbm.at[t], idx_v)
            pltpu.sync_copy(act_hbm.at[t], act_v)
            for j in range(d // SC_LANES):
                y_v[pl.ds(j * SC_LANES, SC_LANES)] = jnp.zeros((SC_LANES,), F32)
            cps = {0: gather(0)}
            for c in range(nch):
                if c + 1 < nch:
                    cps[c + 1] = gather(c + 1)
                cps[c].wait()
                rows = bufs[c % 2]
                for db in range(d // (SC_DBLK * SC_LANES)):
                    def expert(r, accs, c=c, rows=rows, db=db):
                        a = plsc.load_gather(act_v, [jnp.zeros((SC_LANES,), jnp.int32) + (c * SC_CHUNK + r)])
                        return tuple(
                            accs[j] + a * rows[r, pl.ds((db * SC_DBLK + j) * SC_LANES, SC_LANES)]
                            for j in range(SC_DBLK))
                    accs = lax.fori_loop(0, SC_CHUNK, expert,
                                         tuple(jnp.zeros((SC_LANES,), F32) for _ in range(SC_DBLK)))
                    for j in range(SC_DBLK):
                        plsc.addupdate(y_v.at[pl.ds((db * SC_DBLK + j) * SC_LANES, SC_LANES)], accs[j])
            pltpu.sync_copy(y_v, out_hbm.at[t])
            return carry

        lax.fori_loop(0, per_w, token, 0)

    return pl.kernel(
        body, out_type=jax.ShapeDtypeStruct((ts, d), F32), mesh=_sc_mesh(),
        scratch_types=[pltpu.VMEM((hk,), jnp.int32), pltpu.VMEM((hk,), F32),
                       pltpu.VMEM((SC_CHUNK, d), F32), pltpu.VMEM((SC_CHUNK, d), F32),
                       pltpu.VMEM((d,), F32), pltpu.SemaphoreType.DMA, pltpu.SemaphoreType.DMA],
        compiler_params=pltpu.CompilerParams(needs_layout_passes=False),
        name="sc_out",
    )(ids, act, v_tab)


def _act_kernel(pre_ref, gate_ref, after_ref, o_ref):
    del after_ref
    pre = pre_ref[...]
    o_ref[...] = 0.5 * pre * (1.0 + lax.erf(pre * (2.0 ** -0.5))) * gate_ref[...]


def _act(pre, gate, after):
    t, hk = pre.shape
    tm = 256
    assert t % tm == 0
    spec = pl.BlockSpec((tm, hk), lambda i: (i, 0))
    return pl.pallas_call(
        _act_kernel, grid=(t // tm,),
        in_specs=[spec, spec, pl.BlockSpec((8, LANES), lambda i: (0, 0))], out_specs=spec,
        out_shape=jax.ShapeDtypeStruct((t, hk), F32),
        compiler_params=pltpu.CompilerParams(dimension_semantics=("parallel",)),
        name="peer_act",
    )(pre, gate, after)


def _ln2_kernel(h_ref, y_ref, g_ref, b_ref, prev_ref, o_ref, *, alpha):
    del prev_ref
    o_ref[...] = _layernorm(alpha * h_ref[...] + y_ref[...], g_ref[...], b_ref[...])


def _ln2(h, y, g2, b2, alpha, row0, out_prev):
    t, d = y.shape
    tm = 256
    assert t % tm == 0 and row0 % tm == 0
    g0 = row0 // tm
    spec_full = pl.BlockSpec((tm, d), lambda i: (g0 + i, 0))
    vec = pl.BlockSpec((1, d), lambda i: (0, 0))
    return pl.pallas_call(
        functools.partial(_ln2_kernel, alpha=alpha), grid=(t // tm,),
        in_specs=[spec_full, pl.BlockSpec((tm, d), lambda i: (i, 0)), vec, vec,
                  pl.BlockSpec(memory_space=pl.ANY)],
        out_specs=spec_full,
        out_shape=jax.ShapeDtypeStruct(out_prev.shape, F32),
        input_output_aliases={4: 0},
        compiler_params=pltpu.CompilerParams(dimension_semantics=("parallel",)),
        name="peer_ln2",
    )(h, y, g2, b2, out_prev)


def _peer_kernel(h_cur_ref, h_nxt_ref, wq_ref, keys_ref, flat_ref, cmask_ref, uv_hbm, g2_ref, b2_ref, *rest,
                 tt, alpha, has_prev):
    (o_ref, q_scr, s_scr, i_scr, best_scr, eh_scr, eall_scr, idv_scr, ids_smem, gate_scr,
     buf_a, buf_b, sem, idsem, y_scr) = rest[1:] if has_prev else rest
    _peer_body(h_cur_ref, h_nxt_ref, wq_ref, keys_ref, flat_ref, cmask_ref, uv_hbm, g2_ref, b2_ref, o_ref,
               q_scr, s_scr, i_scr, best_scr, eh_scr, eall_scr, idv_scr, ids_smem, gate_scr,
               buf_a, buf_b, sem, idsem, y_scr, tt=tt, alpha=alpha)


def _peer_body(h_cur_ref, h_nxt_ref, wq_ref, keys_ref, flat_ref, cmask_ref, uv_hbm, g2_ref, b2_ref, o_ref,
               q_scr, s_scr, i_scr, best_scr, eh_scr, eall_scr, idv_scr, ids_smem, gate_scr,
               buf_a, buf_b, sem, idsem, y_scr, *, tt, alpha):
    s = pl.program_id(0)
    tg, d = h_cur_ref.shape
    nk, kk = PEER_NKEYS, PEER_TOPK
    hk = PEER_HEADS * kk
    nlh = tg // LANES
    nsub = tg // (2 * tt)
    assert nsub == PEER_HEADS * nlh
    nslab, sub = uv_hbm.shape[1], uv_hbm.shape[2]
    half = nslab // 2
    bufs = (buf_a, buf_b)
    last = pl.num_programs(0) - 1
    rslot = s % 3
    pslot = (s + 2) % 3
    eslot = (s + 1) % 3

    def issue(idslot, row0, slot):
        for t in range(tt):
            for k in range(hk):
                e = ids_smem[idslot, row0 + t, k]
                pltpu.make_async_copy(uv_hbm.at[e], bufs[slot].at[:, pl.ds((t * hk + k) * sub, sub), :],
                                      sem.at[slot]).start(priority=k % 2)

    def wait(slot):
        pltpu.make_async_copy(bufs[slot], bufs[slot], sem.at[slot]).wait()

    q = _dot(h_nxt_ref[...].astype(BF16), wq_ref[...])
    for j in range(2 * PEER_HEADS):
        for lh in range(nlh):
            q_scr[j, lh] = q[lh * LANES:(lh + 1) * LANES, j * LANES:(j + 1) * LANES].astype(BF16)
    flat = flat_ref[...]
    cmask = cmask_ref[...]

    def route_piece(piece):
        hd = piece // nlh
        lh = piece % nlh
        gates, ids = _route_head(keys_ref, q_scr, hd, lh, (s_scr, i_scr, best_scr, eh_scr), flat, cmask)
        r0 = pl.multiple_of(hd * kk, kk)
        gate_scr[rslot, lh, pl.ds(r0, kk), :] = gates
        eall_scr[lh, pl.ds(r0, kk), :] = ids

    lane = lax.broadcasted_iota(jnp.int32, (hk, LANES), 1)

    def compute(row0, slot):
        buf = bufs[slot]
        gt = gate_scr[eslot, row0 // LANES]
        lane0 = row0 % LANES
        for t in range(tt):
            hrow = h_cur_ref[pl.ds(row0 + t, 1), :]

            def rows(a, c):
                return buf[a, pl.ds(t * hk * sub + c, hk, stride=sub), :]

            part = None
            for j in range(half):
                for c in range(sub):
                    seg = j * sub + c
                    term = rows(j, c) * hrow[:, seg * LANES:(seg + 1) * LANES]
                    part = term if part is None else part + term
            pre = jnp.sum(part, axis=-1, keepdims=True)
            gate = jnp.sum(jnp.where(lane == lane0 + t, gt, 0.0), axis=-1, keepdims=True)
            act = 0.5 * pre * (1.0 + lax.erf(pre * (2.0 ** -0.5))) * gate
            yrow = slot * tt + t
            for j in range(half):
                for c in range(sub):
                    seg = j * sub + c
                    y_scr[yrow:yrow + 1, seg * LANES:(seg + 1) * LANES] = jnp.sum(
                        act * rows(half + j, c), axis=0, keepdims=True)

    def substep(j, carry):
        row0 = pl.multiple_of(j * 2 * tt, 2 * tt)
        issue(eslot, row0 + tt, 1)
        route_piece(j)
        wait(0)
        compute(row0, 0)
        wait(1)
        wrap = j == nsub - 1
        issue(jnp.where(wrap, pslot, eslot), jnp.where(wrap, 0, row0 + 2 * tt), 0)
        compute(row0 + tt, 1)
        z = alpha * h_cur_ref[pl.ds(row0, 2 * tt), :] + y_scr[...]
        o_ref[pl.ds(row0, 2 * tt), :] = _layernorm(z, g2_ref[...], b2_ref[...])
        return carry

    @pl.when(s < 2)
    def _():
        o_ref[...] = jnp.zeros_like(o_ref)

        def piece(p, carry):
            route_piece(p)
            return carry
        lax.fori_loop(0, nsub, piece, 0)

    @pl.when(s == 1)
    def _():
        issue(pslot, 0, 0)

    @pl.when(s >= 2)
    def _():
        lax.fori_loop(0, nsub, substep, 0)

    @pl.when(s == last)
    def _():
        wait(0)

    for lh in range(nlh):
        idv_scr[lh * LANES:(lh + 1) * LANES, :] = eall_scr[lh].T
    publish = pltpu.make_async_copy(idv_scr, ids_smem.at[rslot], idsem)
    publish.start()
    publish.wait()


def _peer(h1, wq, keys, uv_tab, g2, b2, alpha, row0=0, t=None, out_prev=None):
    d = h1.shape[1]
    t = h1.shape[0] if t is None else t
    tg = 256
    tt = 8
    ngrp = t // tg
    g0 = row0 // tg
    has_prev = out_prev is not None
    nq = wq.shape[1]
    kk = PEER_TOPK
    hk = PEER_HEADS * kk
    nlh = tg // LANES
    nslab, sub = uv_tab.shape[1], uv_tab.shape[2]
    flat, cmask = _candidate_tables()
    const = lambda shape: pl.BlockSpec(shape, lambda i: (0,) * len(shape))
    prev_specs = [pl.BlockSpec(memory_space=pl.ANY)] if has_prev else []
    prev_args = [out_prev] if has_prev else []
    return pl.pallas_call(
        functools.partial(_peer_kernel, tt=tt, alpha=alpha, has_prev=has_prev),
        grid=(ngrp + 2,),
        in_specs=[pl.BlockSpec((tg, d), lambda i: (g0 + jnp.maximum(i - 2, 0), 0)),
                  pl.BlockSpec((tg, d), lambda i: (g0 + jnp.minimum(i, ngrp - 1), 0)),
                  const((d, nq)), const(keys.shape), const(flat.shape), const(cmask.shape),
                  pl.BlockSpec(memory_space=pl.ANY),
                  const((1, d)), const((1, d))] + prev_specs,
        out_specs=pl.BlockSpec((tg, d), lambda i: (g0 + jnp.maximum(i - 2, 0), 0)),
        out_shape=jax.ShapeDtypeStruct(h1.shape, F32),
        input_output_aliases={9: 0} if has_prev else {},
        scratch_shapes=[pltpu.VMEM((2 * PEER_HEADS, nlh, LANES, LANES), BF16),
                        pltpu.VMEM((2 * kk, LANES), F32), pltpu.VMEM((2 * kk, LANES), jnp.int32),
                        pltpu.VMEM((kk, LANES), F32), pltpu.VMEM((kk, LANES), jnp.int32),
                        pltpu.VMEM((nlh, hk, LANES), jnp.int32),
                        pltpu.VMEM((tg, hk), jnp.int32),
                        pltpu.SMEM((3, tg, hk), jnp.int32),
                        pltpu.VMEM((3, nlh, hk, LANES), F32),
                        pltpu.VMEM((nslab, tt * hk * sub, LANES), F32),
                        pltpu.VMEM((nslab, tt * hk * sub, LANES), F32),
                        pltpu.SemaphoreType.DMA((2,)), pltpu.SemaphoreType.DMA,
                        pltpu.VMEM((2 * tt, d), F32)],
        compiler_params=pltpu.CompilerParams(
            dimension_semantics=("arbitrary",), vmem_limit_bytes=VMEM_LIMIT),
        name="peer",
    )(h1, h1, wq, keys, flat, cmask, uv_tab, g2, b2, *prev_args)


def _layer(h, w_in, conv_w, a_log, dt_bias, norm_w, f_bias, w_out_gdn, w_out_fox, w_o, ln1_g, ln1_b,
           peer_wq, peer_keys, peer_u, peer_v, ln2_g, ln2_b, alpha):
    b, s, d = h.shape
    t = b * s
    qk = GDN_HEADS * GDN_DK
    fw = FX_HEADS * FX_DH
    o_gz = 4 * qk
    o_ga = o_gz
    o_fq = o_ga + 2 * GDN_HEADS
    o_ff = o_fq + 3 * fw
    o_gate = o_ff + FX_HEADS
    w_big = jnp.concatenate([w_in[:, o_gate:], w_in[:, :o_gz]], axis=1).astype(BF16)
    pad = LANES - FX_DH
    w_fox = jnp.pad(w_in[:, o_fq:o_ff].reshape(d, 3 * FX_HEADS, FX_DH), ((0, 0), (0, 0), (0, pad)))
    w_fox = w_fox.reshape(d, 3 * FX_HEADS * LANES).astype(BF16)
    w_out_fox_p = jnp.pad(w_out_fox.reshape(FX_HEADS, FX_DH, d), ((0, 0), (0, pad), (0, 0)))
    w_out_fox_p = w_out_fox_p.reshape(FX_HEADS * LANES, d).astype(BF16)
    n_small = 2 * GDN_HEADS + FX_HEADS
    w_small = jnp.concatenate([w_in[:, o_ga:o_fq], w_in[:, o_ff:o_gate],
                               jnp.zeros((d, LANES - n_small), F32)], axis=1)
    params = jnp.zeros((8, LANES), F32)
    params = params.at[0, :GDN_HEADS].set(a_log).at[1, :GDN_HEADS].set(dt_bias)
    params = params.at[2, 2 * GDN_HEADS:n_small].set(f_bias)

    x2 = h.reshape(t, d)
    proj = _in_proj(x2, w_big, F32, "in_proj")
    pf = _in_proj(x2, w_fox, BF16, "in_proj_fox")
    gexp, bexp, c, ct = _prep(h, w_small, params)
    proj3 = proj.reshape(b, s, proj.shape[1])
    gdn0 = 2 * d // LANES
    oa = _gdn(proj3, gexp, bexp, conv_w, norm_w.reshape(1, LANES),
              (gdn0, gdn0 + GDN_HEADS, gdn0 + 2 * GDN_HEADS, gdn0 + 3 * GDN_HEADS))
    ob = _fox(pf.reshape(b, s, pf.shape[1]), c, ct)
    h1 = _mix(oa.reshape(t, qk), ob.reshape(t, FX_HEADS * LANES), proj, x2,
              w_out_gdn.astype(BF16), w_out_fox_p, w_o.astype(BF16),
              ln1_g.reshape(1, d), ln1_b.reshape(1, d), alpha)
    keys = peer_keys.reshape(2 * PEER_HEADS, PEER_NKEYS, peer_keys.shape[-1]).astype(BF16)
    ne = peer_u.shape[0]
    uslabs = d // (PEER_SUB * LANES)
    uv_tab = jnp.concatenate([peer_u.reshape(ne, uslabs, PEER_SUB, LANES),
                              peer_v.reshape(ne, uslabs, PEER_SUB, LANES)], axis=1)
    wq = peer_wq.astype(BF16)
    g2, b2 = ln2_g.reshape(1, d), ln2_b.reshape(1, d)
    t_sc = (t * PEER_SC_SHARE_PCT // 100) // 256 * 256
    t_tc = t - t_sc
    if t_sc:
        t_s1 = (t_sc // 2) // 256 * 256
        ids_1, gate_1 = _route(h1, wq, keys, t_tc, t_s1)
        pre_1 = _sc_pre(ids_1, h1, peer_u, t_tc)
        ids_2, gate_2 = _route(h1, wq, keys, t_tc + t_s1, t_sc - t_s1)
        pre_2 = _sc_pre(ids_2, h1, peer_u, t_tc + t_s1)
        ids_sc = jnp.concatenate([ids_1, ids_2], axis=0)
        gate_sc = jnp.concatenate([gate_1, gate_2], axis=0)
        pre = jnp.concatenate([pre_1, pre_2], axis=0)
        t_a = (t_tc * PEER_TC_FIRST_PCT // 100) // 256 * 256
        out = _peer(h1, wq, keys, uv_tab, g2, b2, alpha, 0, t_a)
        y_sc = _sc_out(ids_sc, _act(pre, gate_sc, out), peer_v)
        out = _peer(h1, wq, keys, uv_tab, g2, b2, alpha, t_a, t_tc - t_a, out)
        out = _ln2(h1, y_sc, g2, b2, alpha, t_tc, out)
    else:
        out = _peer(h1, wq, keys, uv_tab, g2, b2, alpha)
    return out.reshape(b, s, d)


def kernel(x, w_in, gdn_conv_w, gdn_a_log, gdn_dt_bias, gdn_norm_w, fox_f_bias, w_out_gdn, w_out_fox, w_o,
           ln1_g, ln1_b, peer_wq, peer_keys, peer_u, peer_v, ln2_g, ln2_b):
    depth = w_in.shape[0]
    alpha = (2.0 * depth) ** 0.25
    params = (w_in, gdn_conv_w, gdn_a_log, gdn_dt_bias, gdn_norm_w, fox_f_bias, w_out_gdn, w_out_fox, w_o,
              ln1_g, ln1_b, peer_wq, peer_keys, peer_u, peer_v, ln2_g, ln2_b)

    def layer_slice(p, l):
        return p.reshape(p.shape[1:]) if depth == 1 else p[l]

    h = x
    for l in range(depth):
        h = _layer(h, *(layer_slice(p, l) for p in params), alpha)
    return h
```

```python
import functools

import jax
import jax.numpy as jnp
from jax import lax
from jax.experimental import pallas as pl
from jax.experimental.pallas import tpu as pltpu
from jax.experimental.pallas import tpu_sc as plsc

F32 = jnp.float32
BF16 = jnp.bfloat16
HI = lax.Precision.HIGHEST

LANES = 128
CHUNK = 64
GDN_HEADS = 4
GDN_DK = 128
FX_HEADS = 8
FX_DH = 64
PEER_HEADS = 8
PEER_NKEYS = 128
PEER_TOPK = 16
PEER_SUB = 4
PEER_GROUP = 256
LN_EPS = 1e-5
VMEM_LIMIT = 48 * 1024 * 1024


def _dot(a, b, prec=None):
    return jnp.dot(a, b, preferred_element_type=F32, precision=prec)


def _dot_nt(a, b, prec=None):
    return lax.dot_general(a, b, (((1,), (1,)), ((), ())), preferred_element_type=F32, precision=prec)


def _dot_tn(a, b, prec=None):
    return lax.dot_general(a, b, (((0,), (0,)), ((), ())), preferred_element_type=F32, precision=prec)


def _softplus(x):
    return jnp.maximum(x, 0.0) + jnp.log1p(jnp.exp(-jnp.abs(x)))


def _layernorm(z, g, b):
    mu = jnp.mean(z, axis=-1, keepdims=True)
    zc = z - mu
    var = jnp.mean(zc * zc, axis=-1, keepdims=True)
    return zc * lax.rsqrt(var + LN_EPS) * g + b


def _mm_kernel(x_ref, w_ref, o_ref):
    o_ref[...] = _dot(x_ref[...].astype(BF16), w_ref[...]).astype(o_ref.dtype)


def _in_proj(x2, w_big, out_dtype, name):
    t, d = x2.shape
    n = w_big.shape[1]
    tm = min(1024, t)
    tn = 512
    return pl.pallas_call(
        _mm_kernel,
        grid=(t // tm, n // tn),
        in_specs=[pl.BlockSpec((tm, d), lambda i, j: (i, 0)),
                  pl.BlockSpec((d, tn), lambda i, j: (0, j))],
        out_specs=pl.BlockSpec((tm, tn), lambda i, j: (i, j)),
        out_shape=jax.ShapeDtypeStruct((t, n), out_dtype),
        compiler_params=pltpu.CompilerParams(
            dimension_semantics=("parallel", "parallel"), vmem_limit_bytes=VMEM_LIMIT),
        name=name,
    )(x2, w_big)


def _prep_kernel(x_ref, w_ref, par_ref, gexp_ref, bexp_ref, c_ref, ct_ref, carry_ref):
    ts = x_ref.shape[1]

    @pl.when(pl.program_id(1) == 0)
    def _():
        carry_ref[...] = jnp.zeros_like(carry_ref)

    small = _dot(x_ref[0], w_ref[...], HI)
    a_log = par_ref[0:1, :]
    dt_bias = par_ref[1:2, :]
    f_bias = par_ref[2:3, :]
    g = -jnp.exp(a_log) * _softplus(small + dt_bias)
    beta = jax.nn.sigmoid(small)
    lane = lax.broadcasted_iota(jnp.int32, (ts, LANES), 1)
    log_f = jnp.where((lane >= 8) & (lane < 16), -_softplus(-(small + f_bias)), 0.0)
    row = lax.broadcasted_iota(jnp.int32, (ts, ts), 0)
    col = lax.broadcasted_iota(jnp.int32, (ts, ts), 1)
    tril = (row >= col).astype(F32)
    c = _dot(tril, log_f, HI) + carry_ref[...]
    carry_ref[...] = c[ts - 1:ts, :]
    c_ref[0] = c
    ct_ref[0] = c.T[8:16, :]
    gexp_ref[0] = jnp.concatenate(
        [jnp.broadcast_to(g[:, h:h + 1], (ts, LANES)) for h in range(GDN_HEADS)], axis=1)
    bexp_ref[0] = jnp.concatenate(
        [jnp.broadcast_to(beta[:, GDN_HEADS + h:GDN_HEADS + h + 1], (ts, LANES)) for h in range(GDN_HEADS)], axis=1)


def _prep(x, w_small, params):
    b, s, d = x.shape
    ts = min(512, s)
    hw = GDN_HEADS * LANES
    return pl.pallas_call(
        _prep_kernel,
        grid=(b, s // ts),
        in_specs=[pl.BlockSpec((1, ts, d), lambda i, j: (i, j, 0)),
                  pl.BlockSpec((d, LANES), lambda i, j: (0, 0)),
                  pl.BlockSpec((8, LANES), lambda i, j: (0, 0))],
        out_specs=[pl.BlockSpec((1, ts, hw), lambda i, j: (i, j, 0)),
                   pl.BlockSpec((1, ts, hw), lambda i, j: (i, j, 0)),
                   pl.BlockSpec((1, ts, LANES), lambda i, j: (i, j, 0)),
                   pl.BlockSpec((1, 8, ts), lambda i, j: (i, 0, j))],
        out_shape=[jax.ShapeDtypeStruct((b, s, hw), F32),
                   jax.ShapeDtypeStruct((b, s, hw), F32),
                   jax.ShapeDtypeStruct((b, s, LANES), F32),
                   jax.ShapeDtypeStruct((b, 8, s), F32)],
        scratch_shapes=[pltpu.VMEM((1, LANES), F32)],
        compiler_params=pltpu.CompilerParams(
            dimension_semantics=("parallel", "arbitrary"), vmem_limit_bytes=VMEM_LIMIT),
        name="prep",
    )(x, w_small, params)


def _each(fn, *lists):
    return [fn(*args) for args in zip(*lists)]


def _unit_lower_inverse(ms, masks):
    eye, blk16, blk32 = masks
    hi = lambda a, b: _dot(a, b, HI)
    n1 = _each(lambda m: -jnp.where(blk16, m, 0.0), ms)
    l1 = _each(lambda m: jnp.where(blk32 & jnp.logical_not(blk16), m, 0.0), ms)
    l2 = _each(lambda m: jnp.where(blk32, 0.0, m), ms)
    n2 = _each(hi, n1, n1)
    yield
    p = _each(lambda a, b: hi(eye + a, eye + b), n1, n2)
    n4 = _each(hi, n2, n2)
    yield
    p = _each(lambda a, b: hi(a, eye + b), p, n4)
    n8 = _each(hi, n4, n4)
    yield
    d_inv = _each(lambda a, b: hi(a, eye + b), p, n8)
    yield
    dl = _each(hi, d_inv, l1)
    yield
    a32 = _each(lambda a, b: a - hi(b, a), d_inv, dl)
    yield
    al = _each(hi, a32, l2)
    yield
    return _each(lambda a, b: a - hi(b, a), a32, al)


def _emit_zipped(main, side):
    live = [main, side]
    while live:
        for gen in list(live):
            try:
                next(gen)
            except StopIteration:
                live.remove(gen)


def _gdn_kernel(q_ref, k_ref, v_ref, z_ref, g_ref, b_ref, cwq_ref, cwk_ref, cwv_ref, nw_ref,
                o_ref, qn, kn, vn, st, sol_s, qk_s, qg_s, kd_s, gl_s):
    s = q_ref.shape[1]
    c = CHUNK
    nh = q_ref.shape[2] // LANES
    row = lax.broadcasted_iota(jnp.int32, (s, LANES), 0)

    def conv_silu(x, w):
        y = x * w[3:4, :]
        for sh in (1, 2, 3):
            xs = jnp.where(row >= sh, pltpu.roll(x, sh, axis=0), 0.0)
            y = y + xs * w[3 - sh:4 - sh, :]
        return y * jax.nn.sigmoid(y)

    def l2norm(x):
        return x * lax.rsqrt(jnp.sum(x * x, axis=-1, keepdims=True) + 1e-6)

    for j in range(nh):
        hs = slice(j * LANES, (j + 1) * LANES)
        qn[:, hs] = l2norm(conv_silu(q_ref[0, :, hs], cwq_ref[:, hs])) * (GDN_DK ** -0.5)
        kn[:, hs] = l2norm(conv_silu(k_ref[0, :, hs], cwk_ref[:, hs]))
        vn[:, hs] = conv_silu(v_ref[0, :, hs], cwv_ref[:, hs])
    st[...] = jnp.zeros_like(st)

    ri = lax.broadcasted_iota(jnp.int32, (c, c), 0)
    ci = lax.broadcasted_iota(jnp.int32, (c, c), 1)
    tril = ri >= ci
    strict = ri > ci
    t_inc = tril.astype(F32)
    eye = (ri == ci).astype(F32)
    blk16 = (ri >> 4) == (ci >> 4)
    blk32 = (ri >> 5) == (ci >> 5)
    l2 = lax.broadcasted_iota(jnp.int32, (c, 2 * LANES), 0)
    j2 = lax.broadcasted_iota(jnp.int32, (c, 2 * LANES), 1)
    ux = jnp.where((j2 >= c) | (l2 > j2), 1.0, 0.0).astype(F32)
    nw = nw_ref[...]

    heads = list(range(nh))
    lanes_of = [slice(j * LANES, (j + 1) * LANES) for j in heads]

    def local_stage(ns):
        pairs = [(i, j) for i in range(len(ns)) for j in heads]
        rows = [pl.ds(pl.multiple_of(n * c, c), c) for n in ns]
        q = [qn[rows[i], lanes_of[j]] for i, j in pairs]
        k = [kn[rows[i], lanes_of[j]] for i, j in pairs]
        v = [vn[rows[i], lanes_of[j]] for i, j in pairs]
        gb = [g_ref[0, rows[i], lanes_of[j]] for i, j in pairs]
        bb = [b_ref[0, rows[i], lanes_of[j]] for i, j in pairs]
        d = _each(lambda g: _dot(t_inc, jnp.concatenate([g, g], axis=1) * ux, HI), gb)
        kb = _each(lambda a, b: a * b, k, bb)
        kk = _each(lambda a, b: _dot_nt(a, b, HI), kb, k)
        qk = _each(lambda a, b: _dot_nt(a.astype(BF16), b.astype(BF16)), q, k)
        yield
        gc = [x[:, LANES:] for x in d]
        decay = [jnp.where(tril, jnp.exp(x[:, :c]), 0.0) for x in d]
        m = _each(lambda a, b: jnp.where(strict, a * b, 0.0), kk, decay)
        a_inv = yield from _unit_lower_inverse(m, (eye, blk16, blk32))
        yield
        egc = _each(jnp.exp, gc)
        rhs = _each(lambda vv, b, kbb, e: jnp.concatenate([vv * b, kbb * e], axis=1), v, bb, kb, egc)
        sol = _each(lambda a, r: _dot(a, r, HI), a_inv, rhs)
        gl = [x[c - 1:c, :] for x in gc]
        yield
        for p, (i, j) in enumerate(pairs):
            n = ns[i]
            qk_s[n, j] = (qk[p] * decay[p]).astype(BF16)
            qg_s[n, j] = (q[p] * egc[p]).astype(BF16)
            kd_s[n, j] = (k[p] * jnp.exp(gl[p] - gc[p])).astype(BF16)
            gl_s[n, j] = gl[p]
            sol_s[n, j] = sol[p]

    def state_stage(ns):
        for n in ns:
            rows = pl.ds(pl.multiple_of(n * c, c), c)
            state = [st[j] for j in heads]
            state_b = [x.astype(BF16) for x in state]
            v_new = [sol_s[n, j, :, :LANES] - _dot(sol_s[n, j, :, LANES:].astype(BF16), state_b[j])
                     for j in heads]
            yield
            v_new_b = [x.astype(BF16) for x in v_new]
            o = [_dot(qg_s[n, j], state_b[j]) + _dot(qk_s[n, j], v_new_b[j]) for j in heads]
            new_state = [state[j] * jnp.exp(gl_s[n, j]) + _dot_tn(kd_s[n, j], v_new_b[j]) for j in heads]
            yield
            for j in heads:
                st[j] = new_state[j]
                on = o[j] * lax.rsqrt(jnp.mean(o[j] * o[j], axis=-1, keepdims=True) + 1e-6) * nw
                z = z_ref[0, rows, lanes_of[j]]
                o_ref[0, rows, lanes_of[j]] = on * (z * jax.nn.sigmoid(z))
            yield

    nchunks = s // c
    group = 4
    ngroups = nchunks // group

    def chunks_of(gidx):
        return [gidx * group + i for i in range(group)]

    def run(gen):
        for _ in gen:
            pass

    run(local_stage(chunks_of(jnp.int32(0))))

    def both(gidx, carry):
        _emit_zipped(local_stage(chunks_of(gidx)), state_stage(chunks_of(gidx - 1)))
        return carry

    lax.fori_loop(1, ngroups, both, 0)
    run(state_stage(chunks_of(jnp.int32(ngroups - 1))))


def _gdn(proj3, gexp, bexp, conv_w, norm_w, cols):
    b, s, _ = proj3.shape
    nh = 2
    nc = s // CHUNK
    wd = nh * LANES
    cq, ck, cv, cz = (c0 // nh for c0 in cols)

    def blk(c0):
        return pl.BlockSpec((1, s, wd), lambda i, h: (i, 0, c0 + h))

    def cw(c0):
        return pl.BlockSpec((conv_w.shape[0], wd), lambda i, h: (0, c0 + h))

    head = pl.BlockSpec((1, s, wd), lambda i, h: (i, 0, h))
    return pl.pallas_call(
        _gdn_kernel,
        grid=(b, GDN_HEADS // nh),
        in_specs=[blk(cq), blk(ck), blk(cv), blk(cz), head, head,
                  cw(0), cw(GDN_HEADS // nh), cw(2 * GDN_HEADS // nh),
                  pl.BlockSpec((1, LANES), lambda i, h: (0, 0))],
        out_specs=head,
        out_shape=jax.ShapeDtypeStruct((b, s, GDN_HEADS * LANES), F32),
        scratch_shapes=[pltpu.VMEM((s, wd), F32), pltpu.VMEM((s, wd), F32),
                        pltpu.VMEM((s, wd), F32), pltpu.VMEM((nh, GDN_DK, LANES), F32),
                        pltpu.VMEM((nc, nh, CHUNK, 2 * LANES), F32), pltpu.VMEM((nc, nh, CHUNK, CHUNK), BF16),
                        pltpu.VMEM((nc, nh, CHUNK, LANES), BF16), pltpu.VMEM((nc, nh, CHUNK, LANES), BF16),
                        pltpu.VMEM((nc, nh, 1, LANES), F32)],
        compiler_params=pltpu.CompilerParams(
            dimension_semantics=("parallel", "parallel"), vmem_limit_bytes=VMEM_LIMIT),
        name="gdn",
    )(proj3, proj3, proj3, proj3, gexp, bexp, conv_w, conv_w, conv_w, norm_w)


def _fox_kernel(q_ref, k_ref, v_ref, c_ref, ct_ref, o_ref, *, tk):
    tq = q_ref.shape[1]
    nj = q_ref.shape[2] // LANES
    g = pl.program_id(1)
    qi = pl.program_id(2)
    q = q_ref[0]
    cblk = c_ref[0]
    lane = lax.broadcasted_iota(jnp.int32, (tq, LANES), 1)
    qpos = qi * tq + lax.broadcasted_iota(jnp.int32, (tq, tk), 0)
    kofs = lax.broadcasted_iota(jnp.int32, (tq, tk), 1)
    heads = list(range(nj))
    lanes_of = [slice(j * LANES, (j + 1) * LANES) for j in heads]
    ccol = [jnp.sum(jnp.where(lane == 8 + g * nj + j, cblk, 0.0), axis=-1, keepdims=True) for j in heads]
    qh = [q[:, hs] * jnp.asarray(FX_DH ** -0.5, BF16) for hs in lanes_of]

    def body(kv, carry):
        k0 = pl.multiple_of(kv * tk, tk)
        causal = qpos >= k0 + kofs
        kblk = k_ref[0, pl.ds(k0, tk), :]
        vblk = v_ref[0, pl.ds(k0, tk), :]
        sc = [_dot_nt(qh[j], kblk[:, lanes_of[j]]) for j in heads]
        crow = [ct_ref[0, pl.ds(g * nj + j, 1), pl.ds(k0, tk)] for j in heads]
        sc = [jnp.where(causal, sc[j] + ccol[j] - crow[j], -1e30) for j in heads]
        m_new = [jnp.maximum(carry[j][0], jnp.max(sc[j], axis=-1, keepdims=True)) for j in heads]
        a = [jnp.exp(carry[j][0] - m_new[j]) for j in heads]
        p = [jnp.exp(sc[j] - m_new[j]) for j in heads]
        l = [a[j] * carry[j][1] + jnp.sum(p[j], axis=-1, keepdims=True) for j in heads]
        acc = [a[j] * carry[j][2] + _dot(p[j].astype(BF16), vblk[:, lanes_of[j]]) for j in heads]
        return tuple((m_new[j], l[j], acc[j]) for j in heads)

    init = tuple((jnp.full((tq, 1), -1e30, F32), jnp.zeros((tq, 1), F32), jnp.zeros((tq, LANES), F32))
                 for _ in heads)
    nkv = (qi * tq + tq - 1) // tk + 1
    res = lax.fori_loop(0, nkv, body, init)
    o_ref[0] = jnp.concatenate([acc / l for _, l, acc in res], axis=1)


def _fox(pf3, c, ct):
    b, s, _ = pf3.shape
    wd = 4 * LANES
    tq = min(128, s)
    tk = min(256, s)
    ngrp = FX_HEADS * LANES // wd
    return pl.pallas_call(
        functools.partial(_fox_kernel, tk=tk),
        grid=(b, ngrp, s // tq),
        in_specs=[pl.BlockSpec((1, tq, wd), lambda i, h, t: (i, t, h)),
                  pl.BlockSpec((1, s, wd), lambda i, h, t: (i, 0, ngrp + h)),
                  pl.BlockSpec((1, s, wd), lambda i, h, t: (i, 0, 2 * ngrp + h)),
                  pl.BlockSpec((1, tq, LANES), lambda i, h, t: (i, t, 0)),
                  pl.BlockSpec((1, 8, s), lambda i, h, t: (i, 0, 0))],
        out_specs=pl.BlockSpec((1, tq, wd), lambda i, h, t: (i, t, h)),
        out_shape=jax.ShapeDtypeStruct((b, s, ngrp * wd), F32),
        compiler_params=pltpu.CompilerParams(
            dimension_semantics=("parallel", "parallel", "parallel"), vmem_limit_bytes=VMEM_LIMIT),
        name="fox",
    )(pf3, pf3, pf3, c, ct)


def _mix_kernel(oa_ref, ob_ref, ga_ref, gb_ref, x_ref, wa_ref, wb_ref, wo_ref, g1_ref, b1_ref, o_ref, *, alpha):
    ya = _dot(oa_ref[...].astype(BF16), wa_ref[...])
    yb = _dot(ob_ref[...].astype(BF16), wb_ref[...])
    mix = jax.nn.sigmoid(ga_ref[...]) * ya + jax.nn.sigmoid(gb_ref[...]) * yb
    z = alpha * x_ref[...] + _dot(mix.astype(BF16), wo_ref[...])
    o_ref[...] = _layernorm(z, g1_ref[...], b1_ref[...])


def _mix(oa, ob, proj, x2, wa, wb, wo, g1, b1, alpha):
    t, d = x2.shape
    tm = min(512, t)
    w, w2 = oa.shape[1], ob.shape[1]
    full = lambda r, c: pl.BlockSpec((r, c), lambda i: (0, 0))
    return pl.pallas_call(
        functools.partial(_mix_kernel, alpha=alpha),
        grid=(t // tm,),
        in_specs=[pl.BlockSpec((tm, w), lambda i: (i, 0)),
                  pl.BlockSpec((tm, w2), lambda i: (i, 0)),
                  pl.BlockSpec((tm, d), lambda i: (i, 0)),
                  pl.BlockSpec((tm, d), lambda i: (i, 1)),
                  pl.BlockSpec((tm, d), lambda i: (i, 0)),
                  full(w, d), full(w2, d), full(d, d), full(1, d), full(1, d)],
        out_specs=pl.BlockSpec((tm, d), lambda i: (i, 0)),
        out_shape=jax.ShapeDtypeStruct((t, d), F32),
        compiler_params=pltpu.CompilerParams(
            dimension_semantics=("parallel",), vmem_limit_bytes=VMEM_LIMIT),
        name="mix",
    )(oa, ob, proj, proj, x2, wa, wb, wo, g1, b1)


def _route_head(keys_ref, q_scr, hd, lh, scr, flat, cmask):
    s_scr, i_scr, best_scr, eh_scr = scr
    nk, kk = PEER_NKEYS, PEER_TOPK
    iota_k = lax.broadcasted_iota(jnp.int32, (nk, LANES), 0)
    neg = jnp.float32(-jnp.inf)
    for p in range(2):
        vals = _dot_nt(keys_ref[hd * 2 + p], q_scr[hd * 2 + p, lh])
        for r in range(kk):
            m = jnp.max(vals, axis=0, keepdims=True)
            am = jnp.min(jnp.where(vals == m, iota_k, nk), axis=0, keepdims=True)
            s_scr[p * kk + r:p * kk + r + 1, :] = m
            i_scr[p * kk + r:p * kk + r + 1, :] = am
            vals = jnp.where(iota_k == am, neg, vals)
    s1 = s_scr[kk:kk + 8, :]
    i1 = i_scr[kk:kk + 8, :]
    cand = [s_scr[0:1, :] + s_scr[kk:2 * kk, :]]
    cidx = [i_scr[0:1, :] * nk + i_scr[kk:2 * kk, :]]
    for a in range(1, 8):
        cand.append(s_scr[a:a + 1, :] + s1)
        cidx.append(i_scr[a:a + 1, :] * nk + i1)
    cand.append(s_scr[8:kk, :] + s_scr[kk:kk + 1, :])
    cidx.append(i_scr[8:kk, :] * nk + i_scr[kk:kk + 1, :])
    vals = jnp.concatenate(cand, axis=0) + cmask
    cidx = jnp.concatenate(cidx, axis=0)
    for r in range(kk):
        m = jnp.max(vals, axis=0, keepdims=True)
        am = jnp.min(jnp.where(vals == m, flat, 2 * kk * kk), axis=0, keepdims=True)
        sel = flat == am
        best_scr[r:r + 1, :] = m
        eh_scr[r:r + 1, :] = jnp.max(jnp.where(sel, cidx, -1), axis=0, keepdims=True)
        vals = jnp.where(sel, neg, vals)
    bs = best_scr[...]
    ex = jnp.exp(bs - bs[0:1, :])
    return ex / jnp.sum(ex, axis=0, keepdims=True), eh_scr[...]


def _candidate_tables():
    kk = PEER_TOPK
    pairs = [(0, bb) for bb in range(kk)]
    for a in range(1, 8):
        pairs += [(a, bb) for bb in range(8)]
    pairs += [(a, 0) for a in range(8, kk)]
    real = [(a + 1) * (bb + 1) <= kk for a, bb in pairs]
    flat = [a * kk + bb if ok else kk * kk + r for r, ((a, bb), ok) in enumerate(zip(pairs, real))]
    flat = jnp.broadcast_to(jnp.asarray(flat, jnp.int32)[:, None], (len(pairs), LANES))
    cmask = jnp.broadcast_to(jnp.asarray([0.0 if ok else -jnp.inf for ok in real], F32)[:, None],
                             (len(pairs), LANES))
    return flat, cmask


def _route_kernel(h_ref, wq_ref, keys_ref, flat_ref, cmask_ref, e_ref, g_ref,
                  q_scr, s_scr, i_scr, best_scr, eh_scr, eall_scr, gall_scr):
    tg = h_ref.shape[0]
    kk = PEER_TOPK
    nlh = tg // LANES
    q = _dot(h_ref[...].astype(BF16), wq_ref[...])
    for j in range(2 * PEER_HEADS):
        for lh in range(nlh):
            q_scr[j, lh] = q[lh * LANES:(lh + 1) * LANES, j * LANES:(j + 1) * LANES].astype(BF16)
    flat = flat_ref[...]
    cmask = cmask_ref[...]

    def piece(p, carry):
        hd = p // nlh
        lh = p % nlh
        gates, ids = _route_head(keys_ref, q_scr, hd, lh, (s_scr, i_scr, best_scr, eh_scr), flat, cmask)
        r0 = pl.multiple_of(hd * kk, kk)
        gall_scr[lh, pl.ds(r0, kk), :] = gates
        eall_scr[lh, pl.ds(r0, kk), :] = ids
        return carry

    lax.fori_loop(0, PEER_HEADS * nlh, piece, 0)
    for lh in range(nlh):
        e_ref[lh * LANES:(lh + 1) * LANES, :] = eall_scr[lh].T
        g_ref[lh * LANES:(lh + 1) * LANES, :] = gall_scr[lh].T


def _route(h, wq, keys, row0, t):
    d = h.shape[1]
    tg = PEER_GROUP
    g0 = row0 // tg
    nq = wq.shape[1]
    kk = PEER_TOPK
    hk = PEER_HEADS * kk
    nlh = tg // LANES
    flat, cmask = _candidate_tables()
    const = lambda shape: pl.BlockSpec(shape, lambda i: (0,) * len(shape))
    return pl.pallas_call(
        _route_kernel,
        grid=(t // tg,),
        in_specs=[pl.BlockSpec((tg, d), lambda i: (g0 + i, 0)),
                  const((d, nq)), const(keys.shape), const(flat.shape), const(cmask.shape)],
        out_specs=[pl.BlockSpec((tg, hk), lambda i: (i, 0)), pl.BlockSpec((tg, hk), lambda i: (i, 0))],
        out_shape=[jax.ShapeDtypeStruct((t, hk), jnp.int32), jax.ShapeDtypeStruct((t, hk), F32)],
        scratch_shapes=[pltpu.VMEM((2 * PEER_HEADS, nlh, LANES, LANES), BF16),
                        pltpu.VMEM((2 * kk, LANES), F32), pltpu.VMEM((2 * kk, LANES), jnp.int32),
                        pltpu.VMEM((kk, LANES), F32), pltpu.VMEM((kk, LANES), jnp.int32),
                        pltpu.VMEM((nlh, hk, LANES), jnp.int32), pltpu.VMEM((nlh, hk, LANES), F32)],
        compiler_params=pltpu.CompilerParams(
            dimension_semantics=("parallel",), vmem_limit_bytes=VMEM_LIMIT),
        name="route",
    )(h, wq, keys, flat, cmask)


PEER_SC_SHARE_PCT = 46
PEER_TC_FIRST_PCT = 49
SC_WORKERS = 32
SC_LANES = 16
SC_CHUNK = 32
SC_DBLK = 16
SC_EBLK = 4
SC_JUNROLL = 8


def _sc_mesh():
    return plsc.VectorSubcoreMesh(core_axis_name="c", subcore_axis_name="s")


def _sc_worker():
    return lax.axis_index("s") * 2 + lax.axis_index("c")


def _sc_pre(ids, h, u_tab, row0):
    ts, hk = ids.shape
    d = h.shape[1]
    per_w = ts // SC_WORKERS
    nch = hk // SC_CHUNK

    def body(ids_hbm, h_hbm, u_hbm, out_hbm, idx_v, h_v, rows_a, rows_b, pre_v, sem_a, sem_b):
        base = _sc_worker() * per_w
        lane = lax.iota(jnp.int32, SC_LANES)
        bufs = (rows_a, rows_b)
        sems = (sem_a, sem_b)

        def gather(c):
            return pltpu.async_copy(u_hbm.at[idx_v.at[pl.ds(c * SC_CHUNK, SC_CHUNK)]], bufs[c % 2], sems[c % 2])

        def token(i, carry):
            t = base + i
            pltpu.sync_copy(ids_hbm.at[t], idx_v)
            pltpu.sync_copy(h_hbm.at[row0 + t], h_v)
            cps = {0: gather(0)}
            for c in range(nch):
                if c + 1 < nch:
                    cps[c + 1] = gather(c + 1)
                cps[c].wait()
                rows = bufs[c % 2]
                for g in range(SC_CHUNK // SC_LANES):
                    def experts(q, vec, g=g, rows=rows):
                        e0 = g * SC_LANES + q * SC_EBLK

                        def span(jb, accs, rows=rows, e0=e0):
                            accs = list(accs)
                            for jj in range(SC_JUNROLL):
                                sl = pl.ds((jb * SC_JUNROLL + jj) * SC_LANES, SC_LANES)
                                hv = h_v[sl]
                                for i in range(SC_EBLK):
                                    accs[i] = accs[i] + rows[e0 + i, sl] * hv
                            return tuple(accs)
                        accs = lax.fori_loop(0, d // (SC_LANES * SC_JUNROLL), span,
                                             tuple(jnp.zeros((SC_LANES,), F32) for _ in range(SC_EBLK)))
                        for i in range(SC_EBLK):
                            vec = jnp.where(lane == q * SC_EBLK + i, jnp.sum(accs[i]), vec)
                        return vec
                    vec = lax.fori_loop(0, SC_LANES // SC_EBLK, experts, jnp.zeros((SC_LANES,), F32))
                    pre_v[pl.ds(c * SC_CHUNK + g * SC_LANES, SC_LANES)] = vec
            pltpu.sync_copy(pre_v, out_hbm.at[t])
            return carry

        lax.fori_loop(0, per_w, token, 0)

    return pl.kernel(
        body, out_type=jax.ShapeDtypeStruct((ts, hk), F32), mesh=_sc_mesh(),
        scratch_types=[pltpu.VMEM((hk,), jnp.int32), pltpu.VMEM((d,), F32),
                       pltpu.VMEM((SC_CHUNK, d), F32), pltpu.VMEM((SC_CHUNK, d), F32),
                       pltpu.VMEM((hk,), F32), pltpu.SemaphoreType.DMA, pltpu.SemaphoreType.DMA],
        compiler_params=pltpu.CompilerParams(needs_layout_passes=False),
        name="sc_pre",
    )(ids, h, u_tab)


def _sc_out(ids, act, v_tab):
    ts, hk = ids.shape
    d = v_tab.shape[1]
    per_w = ts // SC_WORKERS
    nch = hk // SC_CHUNK

    def body(ids_hbm, act_hbm, v_hbm, out_hbm, idx_v, act_v, rows_a, rows_b, y_v, sem_a, sem_b):
        base = _sc_worker() * per_w
        lane = lax.iota(jnp.int32, SC_LANES)
        bufs = (rows_a, rows_b)
        sems = (sem_a, sem_b)

        def gather(c):
            return pltpu.async_copy(v_hbm.at[idx_v.at[pl.ds(c * SC_CHUNK, SC_CHUNK)]], bufs[c % 2], sems[c % 2])

        def token(i, carry):
            t = base + i
            pltpu.sync_copy(ids_hbm.at[t], idx_v)
            pltpu.sync_copy(act_hbm.at[t], act_v)
            for j in range(d // SC_LANES):
                y_v[pl.ds(j * SC_LANES, SC_LANES)] = jnp.zeros((SC_LANES,), F32)
            cps = {0: gather(0)}
            for c in range(nch):
                if c + 1 < nch:
                    cps[c + 1] = gather(c + 1)
                cps[c].wait()
                rows = bufs[c % 2]
                for db in range(d // (SC_DBLK * SC_LANES)):
                    def expert(r, accs, c=c, rows=rows, db=db):
                        a = plsc.load_gather(act_v, [jnp.zeros((SC_LANES,), jnp.int32) + (c * SC_CHUNK + r)])
                        return tuple(
                            accs[j] + a * rows[r, pl.ds((db * SC_DBLK + j) * SC_LANES, SC_LANES)]
                            for j in range(SC_DBLK))
                    accs = lax.fori_loop(0, SC_CHUNK, expert,
                                         tuple(jnp.zeros((SC_LANES,), F32) for _ in range(SC_DBLK)))
                    for j in range(SC_DBLK):
                        plsc.addupdate(y_v.at[pl.ds((db * SC_DBLK + j) * SC_LANES, SC_LANES)], accs[j])
            pltpu.sync_copy(y_v, out_hbm.at[t])
            return carry

        lax.fori_loop(0, per_w, token, 0)

    return pl.kernel(
        body, out_type=jax.ShapeDtypeStruct((ts, d), F32), mesh=_sc_mesh(),
        scratch_types=[pltpu.VMEM((hk,), jnp.int32), pltpu.VMEM((hk,), F32),
                       pltpu.VMEM((SC_CHUNK, d), F32), pltpu.VMEM((SC_CHUNK, d), F32),
                       pltpu.VMEM((d,), F32), pltpu.SemaphoreType.DMA, pltpu.SemaphoreType.DMA],
        compiler_params=pltpu.CompilerParams(needs_layout_passes=False),
        name="sc_out",
    )(ids, act, v_tab)


def _act_kernel(pre_ref, gate_ref, after_ref, o_ref):
    del after_ref
    pre = pre_ref[...]
    o_ref[...] = 0.5 * pre * (1.0 + lax.erf(pre * (2.0 ** -0.5))) * gate_ref[...]


def _act(pre, gate, after):
    t, hk = pre.shape
    tm = PEER_GROUP
    assert t % tm == 0
    spec = pl.BlockSpec((tm, hk), lambda i: (i, 0))
    return pl.pallas_call(
        _act_kernel, grid=(t // tm,),
        in_specs=[spec, spec, pl.BlockSpec((8, LANES), lambda i: (0, 0))], out_specs=spec,
        out_shape=jax.ShapeDtypeStruct((t, hk), F32),
        compiler_params=pltpu.CompilerParams(dimension_semantics=("parallel",)),
        name="peer_act",
    )(pre, gate, after)


def _ln2_kernel(h_ref, y_ref, g_ref, b_ref, prev_ref, o_ref, *, alpha):
    del prev_ref
    o_ref[...] = _layernorm(alpha * h_ref[...] + y_ref[...], g_ref[...], b_ref[...])


def _ln2(h, y, g2, b2, alpha, row0, out_prev):
    t, d = y.shape
    tm = PEER_GROUP
    assert t % tm == 0 and row0 % tm == 0
    g0 = row0 // tm
    spec_full = pl.BlockSpec((tm, d), lambda i: (g0 + i, 0))
    vec = pl.BlockSpec((1, d), lambda i: (0, 0))
    return pl.pallas_call(
        functools.partial(_ln2_kernel, alpha=alpha), grid=(t // tm,),
        in_specs=[spec_full, pl.BlockSpec((tm, d), lambda i: (i, 0)), vec, vec,
                  pl.BlockSpec(memory_space=pl.ANY)],
        out_specs=spec_full,
        out_shape=jax.ShapeDtypeStruct(out_prev.shape, F32),
        input_output_aliases={4: 0},
        compiler_params=pltpu.CompilerParams(dimension_semantics=("parallel",)),
        name="peer_ln2",
    )(h, y, g2, b2, out_prev)


def _peer_kernel(h_cur_ref, h_nxt_ref, wq_ref, keys_ref, flat_ref, cmask_ref, uv_hbm, g2_ref, b2_ref, *rest,
                 tt, alpha, has_prev):
    (o_ref, q_scr, s_scr, i_scr, best_scr, eh_scr, eall_scr, idv_scr, ids_smem, gate_scr,
     buf_a, buf_b, sem, idsem, y_scr) = rest[1:] if has_prev else rest
    _peer_body(h_cur_ref, h_nxt_ref, wq_ref, keys_ref, flat_ref, cmask_ref, uv_hbm, g2_ref, b2_ref, o_ref,
               q_scr, s_scr, i_scr, best_scr, eh_scr, eall_scr, idv_scr, ids_smem, gate_scr,
               buf_a, buf_b, sem, idsem, y_scr, tt=tt, alpha=alpha)


def _peer_body(h_cur_ref, h_nxt_ref, wq_ref, keys_ref, flat_ref, cmask_ref, uv_hbm, g2_ref, b2_ref, o_ref,
               q_scr, s_scr, i_scr, best_scr, eh_scr, eall_scr, idv_scr, ids_smem, gate_scr,
               buf_a, buf_b, sem, idsem, y_scr, *, tt, alpha):
    s = pl.program_id(0)
    tg, d = h_cur_ref.shape
    nk, kk = PEER_NKEYS, PEER_TOPK
    hk = PEER_HEADS * kk
    nlh = tg // LANES
    nsub = tg // (2 * tt)
    assert nsub == PEER_HEADS * nlh
    nslab, sub = uv_hbm.shape[1], uv_hbm.shape[2]
    half = nslab // 2
    bufs = (buf_a, buf_b)
    last = pl.num_programs(0) - 1
    rslot = s % 3
    pslot = (s + 2) % 3
    eslot = (s + 1) % 3

    def issue(idslot, row0, slot):
        for t in range(tt):
            for k in range(hk):
                e = ids_smem[idslot, row0 + t, k]
                pltpu.make_async_copy(uv_hbm.at[e], bufs[slot].at[:, pl.ds((t * hk + k) * sub, sub), :],
                                      sem.at[slot]).start(priority=k % 2)

    def wait(slot):
        pltpu.make_async_copy(bufs[slot], bufs[slot], sem.at[slot]).wait()

    q = _dot(h_nxt_ref[...].astype(BF16), wq_ref[...])
    for j in range(2 * PEER_HEADS):
        for lh in range(nlh):
            q_scr[j, lh] = q[lh * LANES:(lh + 1) * LANES, j * LANES:(j + 1) * LANES].astype(BF16)
    flat = flat_ref[...]
    cmask = cmask_ref[...]

    def route_piece(piece):
        hd = piece // nlh
        lh = piece % nlh
        gates, ids = _route_head(keys_ref, q_scr, hd, lh, (s_scr, i_scr, best_scr, eh_scr), flat, cmask)
        r0 = pl.multiple_of(hd * kk, kk)
        gate_scr[rslot, lh, pl.ds(r0, kk), :] = gates
        eall_scr[lh, pl.ds(r0, kk), :] = ids

    lane = lax.broadcasted_iota(jnp.int32, (hk, LANES), 1)

    def compute(row0, slot):
        buf = bufs[slot]
        gt = gate_scr[eslot, row0 // LANES]
        lane0 = row0 % LANES
        for t in range(tt):
            hrow = h_cur_ref[pl.ds(row0 + t, 1), :]

            def rows(a, c):
                return buf[a, pl.ds(t * hk * sub + c, hk, stride=sub), :]

            part = None
            for j in range(half):
                for c in range(sub):
                    seg = j * sub + c
                    term = rows(j, c) * hrow[:, seg * LANES:(seg + 1) * LANES]
                    part = term if part is None else part + term
            pre = jnp.sum(part, axis=-1, keepdims=True)
            gate = jnp.sum(jnp.where(lane == lane0 + t, gt, 0.0), axis=-1, keepdims=True)
            act = 0.5 * pre * (1.0 + lax.erf(pre * (2.0 ** -0.5))) * gate
            yrow = slot * tt + t
            for j in range(half):
                for c in range(sub):
                    seg = j * sub + c
                    y_scr[yrow:yrow + 1, seg * LANES:(seg + 1) * LANES] = jnp.sum(
                        act * rows(half + j, c), axis=0, keepdims=True)

    def substep(j, carry):
        row0 = pl.multiple_of(j * 2 * tt, 2 * tt)
        issue(eslot, row0 + tt, 1)
        route_piece(j)
        wait(0)
        compute(row0, 0)
        wait(1)
        wrap = j == nsub - 1
        issue(jnp.where(wrap, pslot, eslot), jnp.where(wrap, 0, row0 + 2 * tt), 0)
        compute(row0 + tt, 1)
        z = alpha * h_cur_ref[pl.ds(row0, 2 * tt), :] + y_scr[...]
        o_ref[pl.ds(row0, 2 * tt), :] = _layernorm(z, g2_ref[...], b2_ref[...])
        return carry

    @pl.when(s < 2)
    def _():
        o_ref[...] = jnp.zeros_like(o_ref)

        def piece(p, carry):
            route_piece(p)
            return carry
        lax.fori_loop(0, nsub, piece, 0)

    @pl.when(s == 1)
    def _():
        issue(pslot, 0, 0)

    @pl.when(s >= 2)
    def _():
        lax.fori_loop(0, nsub, substep, 0)

    @pl.when(s == last)
    def _():
        wait(0)

    for lh in range(nlh):
        idv_scr[lh * LANES:(lh + 1) * LANES, :] = eall_scr[lh].T
    publish = pltpu.make_async_copy(idv_scr, ids_smem.at[rslot], idsem)
    publish.start()
    publish.wait()


def _peer(h1, wq, keys, uv_tab, g2, b2, alpha, row0=0, t=None, out_prev=None):
    d = h1.shape[1]
    t = h1.shape[0] if t is None else t
    tg = PEER_GROUP
    tt = 8
    assert t % tg == 0 and row0 % tg == 0
    ngrp = t // tg
    g0 = row0 // tg
    has_prev = out_prev is not None
    nq = wq.shape[1]
    kk = PEER_TOPK
    hk = PEER_HEADS * kk
    nlh = tg // LANES
    nslab, sub = uv_tab.shape[1], uv_tab.shape[2]
    flat, cmask = _candidate_tables()
    const = lambda shape: pl.BlockSpec(shape, lambda i: (0,) * len(shape))
    prev_specs = [pl.BlockSpec(memory_space=pl.ANY)] if has_prev else []
    prev_args = [out_prev] if has_prev else []
    return pl.pallas_call(
        functools.partial(_peer_kernel, tt=tt, alpha=alpha, has_prev=has_prev),
        grid=(ngrp + 2,),
        in_specs=[pl.BlockSpec((tg, d), lambda i: (g0 + jnp.maximum(i - 2, 0), 0)),
                  pl.BlockSpec((tg, d), lambda i: (g0 + jnp.minimum(i, ngrp - 1), 0)),
                  const((d, nq)), const(keys.shape), const(flat.shape), const(cmask.shape),
                  pl.BlockSpec(memory_space=pl.ANY),
                  const((1, d)), const((1, d))] + prev_specs,
        out_specs=pl.BlockSpec((tg, d), lambda i: (g0 + jnp.maximum(i - 2, 0), 0)),
        out_shape=jax.ShapeDtypeStruct(h1.shape, F32),
        input_output_aliases={9: 0} if has_prev else {},
        scratch_shapes=[pltpu.VMEM((2 * PEER_HEADS, nlh, LANES, LANES), BF16),
                        pltpu.VMEM((2 * kk, LANES), F32), pltpu.VMEM((2 * kk, LANES), jnp.int32),
                        pltpu.VMEM((kk, LANES), F32), pltpu.VMEM((kk, LANES), jnp.int32),
                        pltpu.VMEM((nlh, hk, LANES), jnp.int32),
                        pltpu.VMEM((tg, hk), jnp.int32),
                        pltpu.SMEM((3, tg, hk), jnp.int32),
                        pltpu.VMEM((3, nlh, hk, LANES), F32),
                        pltpu.VMEM((nslab, tt * hk * sub, LANES), F32),
                        pltpu.VMEM((nslab, tt * hk * sub, LANES), F32),
                        pltpu.SemaphoreType.DMA((2,)), pltpu.SemaphoreType.DMA,
                        pltpu.VMEM((2 * tt, d), F32)],
        compiler_params=pltpu.CompilerParams(
            dimension_semantics=("arbitrary",), vmem_limit_bytes=VMEM_LIMIT),
        name="peer",
    )(h1, h1, wq, keys, flat, cmask, uv_tab, g2, b2, *prev_args)


def _layer(h, w_in, conv_w, a_log, dt_bias, norm_w, f_bias, w_out_gdn, w_out_fox, w_o, ln1_g, ln1_b,
           peer_wq, peer_keys, peer_u, peer_v, ln2_g, ln2_b, alpha):
    b, s, d = h.shape
    t = b * s
    qk = GDN_HEADS * GDN_DK
    fw = FX_HEADS * FX_DH
    o_gz = 4 * qk
    o_ga = o_gz
    o_fq = o_ga + 2 * GDN_HEADS
    o_ff = o_fq + 3 * fw
    o_gate = o_ff + FX_HEADS
    w_big = jnp.concatenate([w_in[:, o_gate:], w_in[:, :o_gz]], axis=1).astype(BF16)
    pad = LANES - FX_DH
    w_fox = jnp.pad(w_in[:, o_fq:o_ff].reshape(d, 3 * FX_HEADS, FX_DH), ((0, 0), (0, 0), (0, pad)))
    w_fox = w_fox.reshape(d, 3 * FX_HEADS * LANES).astype(BF16)
    w_out_fox_p = jnp.pad(w_out_fox.reshape(FX_HEADS, FX_DH, d), ((0, 0), (0, pad), (0, 0)))
    w_out_fox_p = w_out_fox_p.reshape(FX_HEADS * LANES, d).astype(BF16)
    n_small = 2 * GDN_HEADS + FX_HEADS
    w_small = jnp.concatenate([w_in[:, o_ga:o_fq], w_in[:, o_ff:o_gate],
                               jnp.zeros((d, LANES - n_small), F32)], axis=1)
    params = jnp.zeros((8, LANES), F32)
    params = params.at[0, :GDN_HEADS].set(a_log).at[1, :GDN_HEADS].set(dt_bias)
    params = params.at[2, 2 * GDN_HEADS:n_small].set(f_bias)

    x2 = h.reshape(t, d)
    proj = _in_proj(x2, w_big, F32, "in_proj")
    pf = _in_proj(x2, w_fox, BF16, "in_proj_fox")
    gexp, bexp, c, ct = _prep(h, w_small, params)
    proj3 = proj.reshape(b, s, proj.shape[1])
    gdn0 = 2 * d // LANES
    oa = _gdn(proj3, gexp, bexp, conv_w, norm_w.reshape(1, LANES),
              (gdn0, gdn0 + GDN_HEADS, gdn0 + 2 * GDN_HEADS, gdn0 + 3 * GDN_HEADS))
    ob = _fox(pf.reshape(b, s, pf.shape[1]), c, ct)
    h1 = _mix(oa.reshape(t, qk), ob.reshape(t, FX_HEADS * LANES), proj, x2,
              w_out_gdn.astype(BF16), w_out_fox_p, w_o.astype(BF16),
              ln1_g.reshape(1, d), ln1_b.reshape(1, d), alpha)
    keys = peer_keys.reshape(2 * PEER_HEADS, PEER_NKEYS, peer_keys.shape[-1]).astype(BF16)
    ne = peer_u.shape[0]
    uslabs = d // (PEER_SUB * LANES)
    uv_tab = jnp.concatenate([peer_u.reshape(ne, uslabs, PEER_SUB, LANES),
                              peer_v.reshape(ne, uslabs, PEER_SUB, LANES)], axis=1)
    wq = peer_wq.astype(BF16)
    g2, b2 = ln2_g.reshape(1, d), ln2_b.reshape(1, d)
    grp = PEER_GROUP
    t_sc = (t * PEER_SC_SHARE_PCT // 100) // (2 * grp) * (2 * grp)
    t_tc = t - t_sc
    if t_sc:
        t_s1 = t_sc // 2
        ids_1, gate_1 = _route(h1, wq, keys, t_tc, t_s1)
        pre_1 = _sc_pre(ids_1, h1, peer_u, t_tc)
        ids_2, gate_2 = _route(h1, wq, keys, t_tc + t_s1, t_sc - t_s1)
        pre_2 = _sc_pre(ids_2, h1, peer_u, t_tc + t_s1)
        ids_sc = jnp.concatenate([ids_1, ids_2], axis=0)
        gate_sc = jnp.concatenate([gate_1, gate_2], axis=0)
        pre = jnp.concatenate([pre_1, pre_2], axis=0)
        t_a = (t_tc * PEER_TC_FIRST_PCT // 100) // grp * grp
        out = _peer(h1, wq, keys, uv_tab, g2, b2, alpha, 0, t_a)
        y_sc = _sc_out(ids_sc, _act(pre, gate_sc, out), peer_v)
        out = _peer(h1, wq, keys, uv_tab, g2, b2, alpha, t_a, t_tc - t_a, out)
        out = _ln2(h1, y_sc, g2, b2, alpha, t_tc, out)
    else:
        out = _peer(h1, wq, keys, uv_tab, g2, b2, alpha)
    return out.reshape(b, s, d)


def kernel(x, w_in, gdn_conv_w, gdn_a_log, gdn_dt_bias, gdn_norm_w, fox_f_bias, w_out_gdn, w_out_fox, w_o,
           ln1_g, ln1_b, peer_wq, peer_keys, peer_u, peer_v, ln2_g, ln2_b):
    depth = w_in.shape[0]
    alpha = (2.0 * depth) ** 0.25
    params = (w_in, gdn_conv_w, gdn_a_log, gdn_dt_bias, gdn_norm_w, fox_f_bias, w_out_gdn, w_out_fox, w_o,
              ln1_g, ln1_b, peer_wq, peer_keys, peer_u, peer_v, ln2_g, ln2_b)

    def layer_slice(p, l):
        return p.reshape(p.shape[1:]) if depth == 1 else p[l]

    h = x
    for l in range(depth):
        h = _layer(h, *(layer_slice(p, l) for p in params), alpha)
    return h
```

```python
import functools

import jax
import jax.numpy as jnp
from jax import lax
from jax.experimental import pallas as pl
from jax.experimental.pallas import tpu as pltpu
from jax.experimental.pallas import tpu_sc as plsc

F32 = jnp.float32
BF16 = jnp.bfloat16
HI = lax.Precision.HIGHEST

LANES = 128
CHUNK = 64
GDN_HEADS = 4
GDN_DK = 128
FX_HEADS = 8
FX_DH = 64
PEER_HEADS = 8
PEER_NKEYS = 128
PEER_TOPK = 16
PEER_SUB = 4
PEER_GROUP = 256
LN_EPS = 1e-5
VMEM_LIMIT = 48 * 1024 * 1024


def _dot(a, b, prec=None):
    return jnp.dot(a, b, preferred_element_type=F32, precision=prec)


def _dot_nt(a, b, prec=None):
    return lax.dot_general(a, b, (((1,), (1,)), ((), ())), preferred_element_type=F32, precision=prec)


def _dot_tn(a, b, prec=None):
    return lax.dot_general(a, b, (((0,), (0,)), ((), ())), preferred_element_type=F32, precision=prec)


def _softplus(x):
    return jnp.maximum(x, 0.0) + jnp.log1p(jnp.exp(-jnp.abs(x)))


def _layernorm(z, g, b):
    mu = jnp.mean(z, axis=-1, keepdims=True)
    zc = z - mu
    var = jnp.mean(zc * zc, axis=-1, keepdims=True)
    return zc * lax.rsqrt(var + LN_EPS) * g + b


def _mm_kernel(x_ref, w_ref, o_ref):
    o_ref[...] = _dot(x_ref[...].astype(BF16), w_ref[...]).astype(o_ref.dtype)


def _in_proj(x2, w_big, out_dtype, name):
    t, d = x2.shape
    n = w_big.shape[1]
    tm = min(1024, t)
    tn = 512
    return pl.pallas_call(
        _mm_kernel,
        grid=(t // tm, n // tn),
        in_specs=[pl.BlockSpec((tm, d), lambda i, j: (i, 0)),
                  pl.BlockSpec((d, tn), lambda i, j: (0, j))],
        out_specs=pl.BlockSpec((tm, tn), lambda i, j: (i, j)),
        out_shape=jax.ShapeDtypeStruct((t, n), out_dtype),
        compiler_params=pltpu.CompilerParams(
            dimension_semantics=("parallel", "parallel"), vmem_limit_bytes=VMEM_LIMIT),
        name=name,
    )(x2, w_big)


def _prep_kernel(x_ref, w_ref, par_ref, gexp_ref, bexp_ref, c_ref, ct_ref, carry_ref):
    ts = x_ref.shape[1]

    @pl.when(pl.program_id(1) == 0)
    def _():
        carry_ref[...] = jnp.zeros_like(carry_ref)

    small = _dot(x_ref[0], w_ref[...], HI)
    a_log = par_ref[0:1, :]
    dt_bias = par_ref[1:2, :]
    f_bias = par_ref[2:3, :]
    g = -jnp.exp(a_log) * _softplus(small + dt_bias)
    beta = jax.nn.sigmoid(small)
    lane = lax.broadcasted_iota(jnp.int32, (ts, LANES), 1)
    log_f = jnp.where((lane >= 8) & (lane < 16), -_softplus(-(small + f_bias)), 0.0)
    row = lax.broadcasted_iota(jnp.int32, (ts, ts), 0)
    col = lax.broadcasted_iota(jnp.int32, (ts, ts), 1)
    tril = (row >= col).astype(F32)
    c = _dot(tril, log_f, HI) + carry_ref[...]
    carry_ref[...] = c[ts - 1:ts, :]
    c_ref[0] = c
    ct_ref[0] = c.T[8:16, :]
    gexp_ref[0] = jnp.concatenate(
        [jnp.broadcast_to(g[:, h:h + 1], (ts, LANES)) for h in range(GDN_HEADS)], axis=1)
    bexp_ref[0] = jnp.concatenate(
        [jnp.broadcast_to(beta[:, GDN_HEADS + h:GDN_HEADS + h + 1], (ts, LANES)) for h in range(GDN_HEADS)], axis=1)


def _prep(x, w_small, params):
    b, s, d = x.shape
    ts = min(512, s)
    hw = GDN_HEADS * LANES
    return pl.pallas_call(
        _prep_kernel,
        grid=(b, s // ts),
        in_specs=[pl.BlockSpec((1, ts, d), lambda i, j: (i, j, 0)),
                  pl.BlockSpec((d, LANES), lambda i, j: (0, 0)),
                  pl.BlockSpec((8, LANES), lambda i, j: (0, 0))],
        out_specs=[pl.BlockSpec((1, ts, hw), lambda i, j: (i, j, 0)),
                   pl.BlockSpec((1, ts, hw), lambda i, j: (i, j, 0)),
                   pl.BlockSpec((1, ts, LANES), lambda i, j: (i, j, 0)),
                   pl.BlockSpec((1, 8, ts), lambda i, j: (i, 0, j))],
        out_shape=[jax.ShapeDtypeStruct((b, s, hw), F32),
                   jax.ShapeDtypeStruct((b, s, hw), F32),
                   jax.ShapeDtypeStruct((b, s, LANES), F32),
                   jax.ShapeDtypeStruct((b, 8, s), F32)],
        scratch_shapes=[pltpu.VMEM((1, LANES), F32)],
        compiler_params=pltpu.CompilerParams(
            dimension_semantics=("parallel", "arbitrary"), vmem_limit_bytes=VMEM_LIMIT),
        name="prep",
    )(x, w_small, params)


def _each(fn, *lists):
    return [fn(*args) for args in zip(*lists)]


def _unit_lower_inverse(ms, masks):
    eye, blk16, blk32 = masks
    hi = lambda a, b: _dot(a, b, HI)
    n1 = _each(lambda m: -jnp.where(blk16, m, 0.0), ms)
    l1 = _each(lambda m: jnp.where(blk32 & jnp.logical_not(blk16), m, 0.0), ms)
    l2 = _each(lambda m: jnp.where(blk32, 0.0, m), ms)
    n2 = _each(hi, n1, n1)
    yield
    p = _each(lambda a, b: hi(eye + a, eye + b), n1, n2)
    n4 = _each(hi, n2, n2)
    yield
    p = _each(lambda a, b: hi(a, eye + b), p, n4)
    n8 = _each(hi, n4, n4)
    yield
    d_inv = _each(lambda a, b: hi(a, eye + b), p, n8)
    yield
    dl = _each(hi, d_inv, l1)
    yield
    a32 = _each(lambda a, b: a - hi(b, a), d_inv, dl)
    yield
    al = _each(hi, a32, l2)
    yield
    return _each(lambda a, b: a - hi(b, a), a32, al)


def _emit_zipped(main, side):
    live = [main, side]
    while live:
        for gen in list(live):
            try:
                next(gen)
            except StopIteration:
                live.remove(gen)


def _gdn_kernel(q_ref, k_ref, v_ref, z_ref, g_ref, b_ref, cwq_ref, cwk_ref, cwv_ref, nw_ref,
                o_ref, qn, kn, vn, st, sol_s, qk_s, qg_s, kd_s, gl_s):
    s = q_ref.shape[1]
    c = CHUNK
    nh = q_ref.shape[2] // LANES
    row = lax.broadcasted_iota(jnp.int32, (s, LANES), 0)

    def conv_silu(x, w):
        y = x * w[3:4, :]
        for sh in (1, 2, 3):
            xs = jnp.where(row >= sh, pltpu.roll(x, sh, axis=0), 0.0)
            y = y + xs * w[3 - sh:4 - sh, :]
        return y * jax.nn.sigmoid(y)

    def l2norm(x):
        return x * lax.rsqrt(jnp.sum(x * x, axis=-1, keepdims=True) + 1e-6)

    for j in range(nh):
        hs = slice(j * LANES, (j + 1) * LANES)
        qn[:, hs] = l2norm(conv_silu(q_ref[0, :, hs], cwq_ref[:, hs])) * (GDN_DK ** -0.5)
        kn[:, hs] = l2norm(conv_silu(k_ref[0, :, hs], cwk_ref[:, hs]))
        vn[:, hs] = conv_silu(v_ref[0, :, hs], cwv_ref[:, hs])
    st[...] = jnp.zeros_like(st)

    ri = lax.broadcasted_iota(jnp.int32, (c, c), 0)
    ci = lax.broadcasted_iota(jnp.int32, (c, c), 1)
    tril = ri >= ci
    strict = ri > ci
    t_inc = tril.astype(F32)
    eye = (ri == ci).astype(F32)
    blk16 = (ri >> 4) == (ci >> 4)
    blk32 = (ri >> 5) == (ci >> 5)
    l2 = lax.broadcasted_iota(jnp.int32, (c, 2 * LANES), 0)
    j2 = lax.broadcasted_iota(jnp.int32, (c, 2 * LANES), 1)
    ux = jnp.where((j2 >= c) | (l2 > j2), 1.0, 0.0).astype(F32)
    nw = nw_ref[...]

    heads = list(range(nh))
    lanes_of = [slice(j * LANES, (j + 1) * LANES) for j in heads]

    def local_stage(ns):
        pairs = [(i, j) for i in range(len(ns)) for j in heads]
        rows = [pl.ds(pl.multiple_of(n * c, c), c) for n in ns]
        q = [qn[rows[i], lanes_of[j]] for i, j in pairs]
        k = [kn[rows[i], lanes_of[j]] for i, j in pairs]
        v = [vn[rows[i], lanes_of[j]] for i, j in pairs]
        gb = [g_ref[0, rows[i], lanes_of[j]] for i, j in pairs]
        bb = [b_ref[0, rows[i], lanes_of[j]] for i, j in pairs]
        d = _each(lambda g: _dot(t_inc, jnp.concatenate([g, g], axis=1) * ux, HI), gb)
        kb = _each(lambda a, b: a * b, k, bb)
        kk = _each(lambda a, b: _dot_nt(a, b, HI), kb, k)
        qk = _each(lambda a, b: _dot_nt(a.astype(BF16), b.astype(BF16)), q, k)
        yield
        gc = [x[:, LANES:] for x in d]
        decay = [jnp.where(tril, jnp.exp(x[:, :c]), 0.0) for x in d]
        m = _each(lambda a, b: jnp.where(strict, a * b, 0.0), kk, decay)
        a_inv = yield from _unit_lower_inverse(m, (eye, blk16, blk32))
        yield
        egc = _each(jnp.exp, gc)
        rhs = _each(lambda vv, b, kbb, e: jnp.concatenate([vv * b, kbb * e], axis=1), v, bb, kb, egc)
        sol = _each(lambda a, r: _dot(a, r, HI), a_inv, rhs)
        gl = [x[c - 1:c, :] for x in gc]
        yield
        for p, (i, j) in enumerate(pairs):
            n = ns[i]
            qk_s[n, j] = (qk[p] * decay[p]).astype(BF16)
            qg_s[n, j] = (q[p] * egc[p]).astype(BF16)
            kd_s[n, j] = (k[p] * jnp.exp(gl[p] - gc[p])).astype(BF16)
            gl_s[n, j] = gl[p]
            sol_s[n, j] = sol[p]

    def state_stage(ns):
        for n in ns:
            rows = pl.ds(pl.multiple_of(n * c, c), c)
            state = [st[j] for j in heads]
            state_b = [x.astype(BF16) for x in state]
            v_new = [sol_s[n, j, :, :LANES] - _dot(sol_s[n, j, :, LANES:].astype(BF16), state_b[j])
                     for j in heads]
            yield
            v_new_b = [x.astype(BF16) for x in v_new]
            o = [_dot(qg_s[n, j], state_b[j]) + _dot(qk_s[n, j], v_new_b[j]) for j in heads]
            new_state = [state[j] * jnp.exp(gl_s[n, j]) + _dot_tn(kd_s[n, j], v_new_b[j]) for j in heads]
            yield
            for j in heads:
                st[j] = new_state[j]
                on = o[j] * lax.rsqrt(jnp.mean(o[j] * o[j], axis=-1, keepdims=True) + 1e-6) * nw
                z = z_ref[0, rows, lanes_of[j]]
                o_ref[0, rows, lanes_of[j]] = on * (z * jax.nn.sigmoid(z))
            yield

    nchunks = s // c
    group = 4
    ngroups = nchunks // group

    def chunks_of(gidx):
        return [gidx * group + i for i in range(group)]

    def run(gen):
        for _ in gen:
            pass

    run(local_stage(chunks_of(jnp.int32(0))))

    def both(gidx, carry):
        _emit_zipped(local_stage(chunks_of(gidx)), state_stage(chunks_of(gidx - 1)))
        return carry

    lax.fori_loop(1, ngroups, both, 0)
    run(state_stage(chunks_of(jnp.int32(ngroups - 1))))


def _gdn(proj3, gexp, bexp, conv_w, norm_w, cols):
    b, s, _ = proj3.shape
    nh = 2
    nc = s // CHUNK
    wd = nh * LANES
    cq, ck, cv, cz = (c0 // nh for c0 in cols)

    def blk(c0):
        return pl.BlockSpec((1, s, wd), lambda i, h: (i, 0, c0 + h))

    def cw(c0):
        return pl.BlockSpec((conv_w.shape[0], wd), lambda i, h: (0, c0 + h))

    head = pl.BlockSpec((1, s, wd), lambda i, h: (i, 0, h))
    return pl.pallas_call(
        _gdn_kernel,
        grid=(b, GDN_HEADS // nh),
        in_specs=[blk(cq), blk(ck), blk(cv), blk(cz), head, head,
                  cw(0), cw(GDN_HEADS // nh), cw(2 * GDN_HEADS // nh),
                  pl.BlockSpec((1, LANES), lambda i, h: (0, 0))],
        out_specs=head,
        out_shape=jax.ShapeDtypeStruct((b, s, GDN_HEADS * LANES), F32),
        scratch_shapes=[pltpu.VMEM((s, wd), F32), pltpu.VMEM((s, wd), F32),
                        pltpu.VMEM((s, wd), F32), pltpu.VMEM((nh, GDN_DK, LANES), F32),
                        pltpu.VMEM((nc, nh, CHUNK, 2 * LANES), F32), pltpu.VMEM((nc, nh, CHUNK, CHUNK), BF16),
                        pltpu.VMEM((nc, nh, CHUNK, LANES), BF16), pltpu.VMEM((nc, nh, CHUNK, LANES), BF16),
                        pltpu.VMEM((nc, nh, 1, LANES), F32)],
        compiler_params=pltpu.CompilerParams(
            dimension_semantics=("parallel", "parallel"), vmem_limit_bytes=VMEM_LIMIT),
        name="gdn",
    )(proj3, proj3, proj3, proj3, gexp, bexp, conv_w, conv_w, conv_w, norm_w)


def _fox_kernel(q_ref, k_ref, v_ref, c_ref, ct_ref, o_ref, *, tk):
    tq = q_ref.shape[1]
    nj = q_ref.shape[2] // LANES
    g = pl.program_id(1)
    qi = pl.program_id(2)
    q = q_ref[0]
    cblk = c_ref[0]
    lane = lax.broadcasted_iota(jnp.int32, (tq, LANES), 1)
    qpos = qi * tq + lax.broadcasted_iota(jnp.int32, (tq, tk), 0)
    kofs = lax.broadcasted_iota(jnp.int32, (tq, tk), 1)
    heads = list(range(nj))
    lanes_of = [slice(j * LANES, (j + 1) * LANES) for j in heads]
    ccol = [jnp.sum(jnp.where(lane == 8 + g * nj + j, cblk, 0.0), axis=-1, keepdims=True) for j in heads]
    qh = [q[:, hs] * jnp.asarray(FX_DH ** -0.5, BF16) for hs in lanes_of]

    def body(kv, carry):
        k0 = pl.multiple_of(kv * tk, tk)
        causal = qpos >= k0 + kofs
        kblk = k_ref[0, pl.ds(k0, tk), :]
        vblk = v_ref[0, pl.ds(k0, tk), :]
        sc = [_dot_nt(qh[j], kblk[:, lanes_of[j]]) for j in heads]
        crow = [ct_ref[0, pl.ds(g * nj + j, 1), pl.ds(k0, tk)] for j in heads]
        sc = [jnp.where(causal, sc[j] + ccol[j] - crow[j], -1e30) for j in heads]
        m_new = [jnp.maximum(carry[j][0], jnp.max(sc[j], axis=-1, keepdims=True)) for j in heads]
        a = [jnp.exp(carry[j][0] - m_new[j]) for j in heads]
        p = [jnp.exp(sc[j] - m_new[j]) for j in heads]
        l = [a[j] * carry[j][1] + jnp.sum(p[j], axis=-1, keepdims=True) for j in heads]
        acc = [a[j] * carry[j][2] + _dot(p[j].astype(BF16), vblk[:, lanes_of[j]]) for j in heads]
        return tuple((m_new[j], l[j], acc[j]) for j in heads)

    init = tuple((jnp.full((tq, 1), -1e30, F32), jnp.zeros((tq, 1), F32), jnp.zeros((tq, LANES), F32))
                 for _ in heads)
    nkv = (qi * tq + tq - 1) // tk + 1
    res = lax.fori_loop(0, nkv, body, init)
    o_ref[0] = jnp.concatenate([acc / l for _, l, acc in res], axis=1)


def _fox(pf3, c, ct):
    b, s, _ = pf3.shape
    wd = 4 * LANES
    tq = min(128, s)
    tk = min(256, s)
    ngrp = FX_HEADS * LANES // wd
    return pl.pallas_call(
        functools.partial(_fox_kernel, tk=tk),
        grid=(b, ngrp, s // tq),
        in_specs=[pl.BlockSpec((1, tq, wd), lambda i, h, t: (i, t, h)),
                  pl.BlockSpec((1, s, wd), lambda i, h, t: (i, 0, ngrp + h)),
                  pl.BlockSpec((1, s, wd), lambda i, h, t: (i, 0, 2 * ngrp + h)),
                  pl.BlockSpec((1, tq, LANES), lambda i, h, t: (i, t, 0)),
                  pl.BlockSpec((1, 8, s), lambda i, h, t: (i, 0, 0))],
        out_specs=pl.BlockSpec((1, tq, wd), lambda i, h, t: (i, t, h)),
        out_shape=jax.ShapeDtypeStruct((b, s, ngrp * wd), F32),
        compiler_params=pltpu.CompilerParams(
            dimension_semantics=("parallel", "parallel", "parallel"), vmem_limit_bytes=VMEM_LIMIT),
        name="fox",
    )(pf3, pf3, pf3, c, ct)


def _mix_kernel(oa_ref, ob_ref, ga_ref, gb_ref, x_ref, wa_ref, wb_ref, wo_ref, g1_ref, b1_ref, o_ref, *, alpha):
    ya = _dot(oa_ref[...].astype(BF16), wa_ref[...])
    yb = _dot(ob_ref[...].astype(BF16), wb_ref[...])
    mix = jax.nn.sigmoid(ga_ref[...]) * ya + jax.nn.sigmoid(gb_ref[...]) * yb
    z = alpha * x_ref[...] + _dot(mix.astype(BF16), wo_ref[...])
    o_ref[...] = _layernorm(z, g1_ref[...], b1_ref[...])


def _mix(oa, ob, proj, x2, wa, wb, wo, g1, b1, alpha):
    t, d = x2.shape
    tm = min(512, t)
    w, w2 = oa.shape[1], ob.shape[1]
    full = lambda r, c: pl.BlockSpec((r, c), lambda i: (0, 0))
    return pl.pallas_call(
        functools.partial(_mix_kernel, alpha=alpha),
        grid=(t // tm,),
        in_specs=[pl.BlockSpec((tm, w), lambda i: (i, 0)),
                  pl.BlockSpec((tm, w2), lambda i: (i, 0)),
                  pl.BlockSpec((tm, d), lambda i: (i, 0)),
                  pl.BlockSpec((tm, d), lambda i: (i, 1)),
                  pl.BlockSpec((tm, d), lambda i: (i, 0)),
                  full(w, d), full(w2, d), full(d, d), full(1, d), full(1, d)],
        out_specs=pl.BlockSpec((tm, d), lambda i: (i, 0)),
        out_shape=jax.ShapeDtypeStruct((t, d), F32),
        compiler_params=pltpu.CompilerParams(
            dimension_semantics=("parallel",), vmem_limit_bytes=VMEM_LIMIT),
        name="mix",
    )(oa, ob, proj, proj, x2, wa, wb, wo, g1, b1)


def _route_head(keys_ref, q_scr, hd, lh, scr, flat, cmask):
    s_scr, i_scr, best_scr, eh_scr = scr
    nk, kk = PEER_NKEYS, PEER_TOPK
    iota_k = lax.broadcasted_iota(jnp.int32, (nk, LANES), 0)
    neg = jnp.float32(-jnp.inf)
    for p in range(2):
        vals = _dot_nt(keys_ref[hd * 2 + p], q_scr[hd * 2 + p, lh])
        for r in range(kk):
            m = jnp.max(vals, axis=0, keepdims=True)
            am = jnp.min(jnp.where(vals == m, iota_k, nk), axis=0, keepdims=True)
            s_scr[p * kk + r:p * kk + r + 1, :] = m
            i_scr[p * kk + r:p * kk + r + 1, :] = am
            vals = jnp.where(iota_k == am, neg, vals)
    s1 = s_scr[kk:kk + 8, :]
    i1 = i_scr[kk:kk + 8, :]
    cand = [s_scr[0:1, :] + s_scr[kk:2 * kk, :]]
    cidx = [i_scr[0:1, :] * nk + i_scr[kk:2 * kk, :]]
    for a in range(1, 8):
        cand.append(s_scr[a:a + 1, :] + s1)
        cidx.append(i_scr[a:a + 1, :] * nk + i1)
    cand.append(s_scr[8:kk, :] + s_scr[kk:kk + 1, :])
    cidx.append(i_scr[8:kk, :] * nk + i_scr[kk:kk + 1, :])
    vals = jnp.concatenate(cand, axis=0) + cmask
    cidx = jnp.concatenate(cidx, axis=0)
    for r in range(kk):
        m = jnp.max(vals, axis=0, keepdims=True)
        am = jnp.min(jnp.where(vals == m, flat, 2 * kk * kk), axis=0, keepdims=True)
        sel = flat == am
        best_scr[r:r + 1, :] = m
        eh_scr[r:r + 1, :] = jnp.max(jnp.where(sel, cidx, -1), axis=0, keepdims=True)
        vals = jnp.where(sel, neg, vals)
    bs = best_scr[...]
    ex = jnp.exp(bs - bs[0:1, :])
    return ex / jnp.sum(ex, axis=0, keepdims=True), eh_scr[...]


def _candidate_tables():
    kk = PEER_TOPK
    pairs = [(0, bb) for bb in range(kk)]
    for a in range(1, 8):
        pairs += [(a, bb) for bb in range(8)]
    pairs += [(a, 0) for a in range(8, kk)]
    real = [(a + 1) * (bb + 1) <= kk for a, bb in pairs]
    flat = [a * kk + bb if ok else kk * kk + r for r, ((a, bb), ok) in enumerate(zip(pairs, real))]
    flat = jnp.broadcast_to(jnp.asarray(flat, jnp.int32)[:, None], (len(pairs), LANES))
    cmask = jnp.broadcast_to(jnp.asarray([0.0 if ok else -jnp.inf for ok in real], F32)[:, None],
                             (len(pairs), LANES))
    return flat, cmask


def _route_kernel(h_ref, wq_ref, keys_ref, flat_ref, cmask_ref, e_ref, g_ref,
                  q_scr, s_scr, i_scr, best_scr, eh_scr, eall_scr, gall_scr):
    tg = h_ref.shape[0]
    kk = PEER_TOPK
    nlh = tg // LANES
    q = _dot(h_ref[...].astype(BF16), wq_ref[...])
    for j in range(2 * PEER_HEADS):
        for lh in range(nlh):
            q_scr[j, lh] = q[lh * LANES:(lh + 1) * LANES, j * LANES:(j + 1) * LANES].astype(BF16)
    flat = flat_ref[...]
    cmask = cmask_ref[...]

    def piece(p, carry):
        hd = p // nlh
        lh = p % nlh
        gates, ids = _route_head(keys_ref, q_scr, hd, lh, (s_scr, i_scr, best_scr, eh_scr), flat, cmask)
        r0 = pl.multiple_of(hd * kk, kk)
        gall_scr[lh, pl.ds(r0, kk), :] = gates
        eall_scr[lh, pl.ds(r0, kk), :] = ids
        return carry

    lax.fori_loop(0, PEER_HEADS * nlh, piece, 0)
    for lh in range(nlh):
        e_ref[lh * LANES:(lh + 1) * LANES, :] = eall_scr[lh].T
        g_ref[lh * LANES:(lh + 1) * LANES, :] = gall_scr[lh].T


def _route(h, wq, keys, row0, t):
    d = h.shape[1]
    tg = PEER_GROUP
    g0 = row0 // tg
    nq = wq.shape[1]
    kk = PEER_TOPK
    hk = PEER_HEADS * kk
    nlh = tg // LANES
    flat, cmask = _candidate_tables()
    const = lambda shape: pl.BlockSpec(shape, lambda i: (0,) * len(shape))
    return pl.pallas_call(
        _route_kernel,
        grid=(t // tg,),
        in_specs=[pl.BlockSpec((tg, d), lambda i: (g0 + i, 0)),
                  const((d, nq)), const(keys.shape), const(flat.shape), const(cmask.shape)],
        out_specs=[pl.BlockSpec((tg, hk), lambda i: (i, 0)), pl.BlockSpec((tg, hk), lambda i: (i, 0))],
        out_shape=[jax.ShapeDtypeStruct((t, hk), jnp.int32), jax.ShapeDtypeStruct((t, hk), F32)],
        scratch_shapes=[pltpu.VMEM((2 * PEER_HEADS, nlh, LANES, LANES), BF16),
                        pltpu.VMEM((2 * kk, LANES), F32), pltpu.VMEM((2 * kk, LANES), jnp.int32),
                        pltpu.VMEM((kk, LANES), F32), pltpu.VMEM((kk, LANES), jnp.int32),
                        pltpu.VMEM((nlh, hk, LANES), jnp.int32), pltpu.VMEM((nlh, hk, LANES), F32)],
        compiler_params=pltpu.CompilerParams(
            dimension_semantics=("parallel",), vmem_limit_bytes=VMEM_LIMIT),
        name="route",
    )(h, wq, keys, flat, cmask)


PEER_SC_SHARE_PCT = 47
PEER_TC_FIRST_PCT = 49
SC_WORKERS = 32
SC_LANES = 16
SC_CHUNK = 32
SC_DBLK = 16
SC_EBLK = 4
SC_JUNROLL = 8


def _sc_mesh():
    return plsc.VectorSubcoreMesh(core_axis_name="c", subcore_axis_name="s")


def _sc_worker():
    return lax.axis_index("s") * 2 + lax.axis_index("c")


def _sc_pre(ids, h, u_tab, row0):
    ts, hk = ids.shape
    d = h.shape[1]
    per_w = ts // SC_WORKERS
    nch = hk // SC_CHUNK

    def body(ids_hbm, h_hbm, u_hbm, out_hbm, idx_v, h_v, rows_a, rows_b, pre_v, sem_a, sem_b):
        base = _sc_worker() * per_w
        lane = lax.iota(jnp.int32, SC_LANES)
        bufs = (rows_a, rows_b)
        sems = (sem_a, sem_b)

        def gather(c):
            return pltpu.async_copy(u_hbm.at[idx_v.at[pl.ds(c * SC_CHUNK, SC_CHUNK)]], bufs[c % 2], sems[c % 2])

        def token(i, carry):
            t = base + i
            pltpu.sync_copy(ids_hbm.at[t], idx_v)
            pltpu.sync_copy(h_hbm.at[row0 + t], h_v)
            cps = {0: gather(0)}
            for c in range(nch):
                if c + 1 < nch:
                    cps[c + 1] = gather(c + 1)
                cps[c].wait()
                rows = bufs[c % 2]
                for g in range(SC_CHUNK // SC_LANES):
                    def experts(q, vec, g=g, rows=rows):
                        e0 = g * SC_LANES + q * SC_EBLK

                        def span(jb, accs, rows=rows, e0=e0):
                            accs = list(accs)
                            for jj in range(SC_JUNROLL):
                                sl = pl.ds((jb * SC_JUNROLL + jj) * SC_LANES, SC_LANES)
                                hv = h_v[sl]
                                for i in range(SC_EBLK):
                                    accs[i] = accs[i] + rows[e0 + i, sl] * hv
                            return tuple(accs)
                        accs = lax.fori_loop(0, d // (SC_LANES * SC_JUNROLL), span,
                                             tuple(jnp.zeros((SC_LANES,), F32) for _ in range(SC_EBLK)))
                        for i in range(SC_EBLK):
                            vec = jnp.where(lane == q * SC_EBLK + i, jnp.sum(accs[i]), vec)
                        return vec
                    vec = lax.fori_loop(0, SC_LANES // SC_EBLK, experts, jnp.zeros((SC_LANES,), F32))
                    pre_v[pl.ds(c * SC_CHUNK + g * SC_LANES, SC_LANES)] = vec
            pltpu.sync_copy(pre_v, out_hbm.at[t])
            return carry

        lax.fori_loop(0, per_w, token, 0)

    return pl.kernel(
        body, out_type=jax.ShapeDtypeStruct((ts, hk), F32), mesh=_sc_mesh(),
        scratch_types=[pltpu.VMEM((hk,), jnp.int32), pltpu.VMEM((d,), F32),
                       pltpu.VMEM((SC_CHUNK, d), F32), pltpu.VMEM((SC_CHUNK, d), F32),
                       pltpu.VMEM((hk,), F32), pltpu.SemaphoreType.DMA, pltpu.SemaphoreType.DMA],
        compiler_params=pltpu.CompilerParams(needs_layout_passes=False),
        name="sc_pre",
    )(ids, h, u_tab)


def _sc_out(ids, act, v_tab):
    ts, hk = ids.shape
    d = v_tab.shape[1]
    per_w = ts // SC_WORKERS
    nch = hk // SC_CHUNK

    def body(ids_hbm, act_hbm, v_hbm, out_hbm, idx_v, act_v, rows_a, rows_b, y_v, sem_a, sem_b):
        base = _sc_worker() * per_w
        bufs = (rows_a, rows_b)
        sems = (sem_a, sem_b)

        def gather(c):
            return pltpu.async_copy(v_hbm.at[idx_v.at[pl.ds(c * SC_CHUNK, SC_CHUNK)]], bufs[c % 2], sems[c % 2])

        def token(i, carry):
            t = base + i
            pltpu.sync_copy(ids_hbm.at[t], idx_v)
            pltpu.sync_copy(act_hbm.at[t], act_v)
            for j in range(d // SC_LANES):
                y_v[pl.ds(j * SC_LANES, SC_LANES)] = jnp.zeros((SC_LANES,), F32)
            cps = {0: gather(0)}
            for c in range(nch):
                if c + 1 < nch:
                    cps[c + 1] = gather(c + 1)
                cps[c].wait()
                rows = bufs[c % 2]
                for db in range(d // (SC_DBLK * SC_LANES)):
                    def expert(r, accs, c=c, rows=rows, db=db):
                        a = plsc.load_gather(act_v, [jnp.zeros((SC_LANES,), jnp.int32) + (c * SC_CHUNK + r)])
                        return tuple(
                            accs[j] + a * rows[r, pl.ds((db * SC_DBLK + j) * SC_LANES, SC_LANES)]
                            for j in range(SC_DBLK))
                    accs = lax.fori_loop(0, SC_CHUNK, expert,
                                         tuple(jnp.zeros((SC_LANES,), F32) for _ in range(SC_DBLK)))
                    for j in range(SC_DBLK):
                        plsc.addupdate(y_v.at[pl.ds((db * SC_DBLK + j) * SC_LANES, SC_LANES)], accs[j])
            pltpu.sync_copy(y_v, out_hbm.at[t])
            return carry

        lax.fori_loop(0, per_w, token, 0)

    return pl.kernel(
        body, out_type=jax.ShapeDtypeStruct((ts, d), F32), mesh=_sc_mesh(),
        scratch_types=[pltpu.VMEM((hk,), jnp.int32), pltpu.VMEM((hk,), F32),
                       pltpu.VMEM((SC_CHUNK, d), F32), pltpu.VMEM((SC_CHUNK, d), F32),
                       pltpu.VMEM((d,), F32), pltpu.SemaphoreType.DMA, pltpu.SemaphoreType.DMA],
        compiler_params=pltpu.CompilerParams(needs_layout_passes=False),
        name="sc_out",
    )(ids, act, v_tab)


def _act_kernel(pre_ref, gate_ref, after_ref, o_ref):
    del after_ref
    pre = pre_ref[...]
    o_ref[...] = 0.5 * pre * (1.0 + lax.erf(pre * (2.0 ** -0.5))) * gate_ref[...]


def _act(pre, gate, after):
    t, hk = pre.shape
    tm = PEER_GROUP
    assert t % tm == 0
    spec = pl.BlockSpec((tm, hk), lambda i: (i, 0))
    return pl.pallas_call(
        _act_kernel, grid=(t // tm,),
        in_specs=[spec, spec, pl.BlockSpec((8, LANES), lambda i: (0, 0))], out_specs=spec,
        out_shape=jax.ShapeDtypeStruct((t, hk), F32),
        compiler_params=pltpu.CompilerParams(dimension_semantics=("parallel",)),
        name="peer_act",
    )(pre, gate, after)


def _ln2_kernel(h_ref, y_ref, g_ref, b_ref, prev_ref, o_ref, *, alpha):
    del prev_ref
    o_ref[...] = _layernorm(alpha * h_ref[...] + y_ref[...], g_ref[...], b_ref[...])


def _ln2(h, y, g2, b2, alpha, row0, out_prev):
    t, d = y.shape
    tm = PEER_GROUP
    assert t % tm == 0 and row0 % tm == 0
    g0 = row0 // tm
    spec_full = pl.BlockSpec((tm, d), lambda i: (g0 + i, 0))
    vec = pl.BlockSpec((1, d), lambda i: (0, 0))
    return pl.pallas_call(
        functools.partial(_ln2_kernel, alpha=alpha), grid=(t // tm,),
        in_specs=[spec_full, pl.BlockSpec((tm, d), lambda i: (i, 0)), vec, vec,
                  pl.BlockSpec(memory_space=pl.ANY)],
        out_specs=spec_full,
        out_shape=jax.ShapeDtypeStruct(out_prev.shape, F32),
        input_output_aliases={4: 0},
        compiler_params=pltpu.CompilerParams(dimension_semantics=("parallel",)),
        name="peer_ln2",
    )(h, y, g2, b2, out_prev)


def _peer_kernel(h_cur_ref, h_nxt_ref, wq_ref, keys_ref, flat_ref, cmask_ref, uv_hbm, g2_ref, b2_ref, *rest,
                 tt, alpha, has_prev):
    (o_ref, q_scr, s_scr, i_scr, best_scr, eh_scr, eall_scr, idv_scr, ids_smem, gate_scr,
     buf_a, buf_b, sem, idsem, y_scr) = rest[1:] if has_prev else rest
    _peer_body(h_cur_ref, h_nxt_ref, wq_ref, keys_ref, flat_ref, cmask_ref, uv_hbm, g2_ref, b2_ref, o_ref,
               q_scr, s_scr, i_scr, best_scr, eh_scr, eall_scr, idv_scr, ids_smem, gate_scr,
               buf_a, buf_b, sem, idsem, y_scr, tt=tt, alpha=alpha)


def _peer_body(h_cur_ref, h_nxt_ref, wq_ref, keys_ref, flat_ref, cmask_ref, uv_hbm, g2_ref, b2_ref, o_ref,
               q_scr, s_scr, i_scr, best_scr, eh_scr, eall_scr, idv_scr, ids_smem, gate_scr,
               buf_a, buf_b, sem, idsem, y_scr, *, tt, alpha):
    s = pl.program_id(0)
    tg, d = h_cur_ref.shape
    nk, kk = PEER_NKEYS, PEER_TOPK
    hk = PEER_HEADS * kk
    nlh = tg // LANES
    nsub = tg // (2 * tt)
    assert nsub == PEER_HEADS * nlh
    nslab, sub = uv_hbm.shape[1], uv_hbm.shape[2]
    half = nslab // 2
    bufs = (buf_a, buf_b)
    last = pl.num_programs(0) - 1
    rslot = s % 3
    pslot = (s + 2) % 3
    eslot = (s + 1) % 3

    def issue(idslot, row0, slot):
        for t in range(tt):
            for k in range(hk):
                e = ids_smem[idslot, row0 + t, k]
                pltpu.make_async_copy(uv_hbm.at[e], bufs[slot].at[:, pl.ds((t * hk + k) * sub, sub), :],
                                      sem.at[slot]).start(priority=k % 2)

    def wait(slot):
        pltpu.make_async_copy(bufs[slot], bufs[slot], sem.at[slot]).wait()

    q = _dot(h_nxt_ref[...].astype(BF16), wq_ref[...])
    for j in range(2 * PEER_HEADS):
        for lh in range(nlh):
            q_scr[j, lh] = q[lh * LANES:(lh + 1) * LANES, j * LANES:(j + 1) * LANES].astype(BF16)
    flat = flat_ref[...]
    cmask = cmask_ref[...]

    def route_piece(piece):
        hd = piece // nlh
        lh = piece % nlh
        gates, ids = _route_head(keys_ref, q_scr, hd, lh, (s_scr, i_scr, best_scr, eh_scr), flat, cmask)
        r0 = pl.multiple_of(hd * kk, kk)
        gate_scr[rslot, lh, pl.ds(r0, kk), :] = gates
        eall_scr[lh, pl.ds(r0, kk), :] = ids

    lane = lax.broadcasted_iota(jnp.int32, (hk, LANES), 1)

    def compute(row0, slot):
        buf = bufs[slot]
        gt = gate_scr[eslot, row0 // LANES]
        lane0 = row0 % LANES
        for t in range(tt):
            hrow = h_cur_ref[pl.ds(row0 + t, 1), :]

            def rows(a, c):
                return buf[a, pl.ds(t * hk * sub + c, hk, stride=sub), :]

            part = None
            for j in range(half):
                for c in range(sub):
                    seg = j * sub + c
                    term = rows(j, c) * hrow[:, seg * LANES:(seg + 1) * LANES]
                    part = term if part is None else part + term
            pre = jnp.sum(part, axis=-1, keepdims=True)
            gate = jnp.sum(jnp.where(lane == lane0 + t, gt, 0.0), axis=-1, keepdims=True)
            act = 0.5 * pre * (1.0 + lax.erf(pre * (2.0 ** -0.5))) * gate
            yrow = slot * tt + t
            for j in range(half):
                for c in range(sub):
                    seg = j * sub + c
                    y_scr[yrow:yrow + 1, seg * LANES:(seg + 1) * LANES] = jnp.sum(
                        act * rows(half + j, c), axis=0, keepdims=True)

    def substep(j, carry):
        row0 = pl.multiple_of(j * 2 * tt, 2 * tt)
        issue(eslot, row0 + tt, 1)
        route_piece(j)
        wait(0)
        compute(row0, 0)
        wait(1)
        wrap = j == nsub - 1
        issue(jnp.where(wrap, pslot, eslot), jnp.where(wrap, 0, row0 + 2 * tt), 0)
        compute(row0 + tt, 1)
        z = alpha * h_cur_ref[pl.ds(row0, 2 * tt), :] + y_scr[...]
        o_ref[pl.ds(row0, 2 * tt), :] = _layernorm(z, g2_ref[...], b2_ref[...])
        return carry

    @pl.when(s < 2)
    def _():
        o_ref[...] = jnp.zeros_like(o_ref)

        def piece(p, carry):
            route_piece(p)
            return carry
        lax.fori_loop(0, nsub, piece, 0)

    @pl.when(s == 1)
    def _():
        issue(pslot, 0, 0)

    @pl.when(s >= 2)
    def _():
        lax.fori_loop(0, nsub, substep, 0)

    @pl.when(s == last)
    def _():
        wait(0)

    for lh in range(nlh):
        idv_scr[lh * LANES:(lh + 1) * LANES, :] = eall_scr[lh].T
    publish = pltpu.make_async_copy(idv_scr, ids_smem.at[rslot], idsem)
    publish.start()
    publish.wait()


def _peer(h1, wq, keys, uv_tab, g2, b2, alpha, row0=0, t=None, out_prev=None):
    d = h1.shape[1]
    t = h1.shape[0] if t is None else t
    tg = PEER_GROUP
    tt = 8
    assert t % tg == 0 and row0 % tg == 0
    ngrp = t // tg
    g0 = row0 // tg
    has_prev = out_prev is not None
    nq = wq.shape[1]
    kk = PEER_TOPK
    hk = PEER_HEADS * kk
    nlh = tg // LANES
    nslab, sub = uv_tab.shape[1], uv_tab.shape[2]
    flat, cmask = _candidate_tables()
    const = lambda shape: pl.BlockSpec(shape, lambda i: (0,) * len(shape))
    prev_specs = [pl.BlockSpec(memory_space=pl.ANY)] if has_prev else []
    prev_args = [out_prev] if has_prev else []
    return pl.pallas_call(
        functools.partial(_peer_kernel, tt=tt, alpha=alpha, has_prev=has_prev),
        grid=(ngrp + 2,),
        in_specs=[pl.BlockSpec((tg, d), lambda i: (g0 + jnp.maximum(i - 2, 0), 0)),
                  pl.BlockSpec((tg, d), lambda i: (g0 + jnp.minimum(i, ngrp - 1), 0)),
                  const((d, nq)), const(keys.shape), const(flat.shape), const(cmask.shape),
                  pl.BlockSpec(memory_space=pl.ANY),
                  const((1, d)), const((1, d))] + prev_specs,
        out_specs=pl.BlockSpec((tg, d), lambda i: (g0 + jnp.maximum(i - 2, 0), 0)),
        out_shape=jax.ShapeDtypeStruct(h1.shape, F32),
        input_output_aliases={9: 0} if has_prev else {},
        scratch_shapes=[pltpu.VMEM((2 * PEER_HEADS, nlh, LANES, LANES), BF16),
                        pltpu.VMEM((2 * kk, LANES), F32), pltpu.VMEM((2 * kk, LANES), jnp.int32),
                        pltpu.VMEM((kk, LANES), F32), pltpu.VMEM((kk, LANES), jnp.int32),
                        pltpu.VMEM((nlh, hk, LANES), jnp.int32),
                        pltpu.VMEM((tg, hk), jnp.int32),
                        pltpu.SMEM((3, tg, hk), jnp.int32),
                        pltpu.VMEM((3, nlh, hk, LANES), F32),
                        pltpu.VMEM((nslab, tt * hk * sub, LANES), F32),
                        pltpu.VMEM((nslab, tt * hk * sub, LANES), F32),
                        pltpu.SemaphoreType.DMA((2,)), pltpu.SemaphoreType.DMA,
                        pltpu.VMEM((2 * tt, d), F32)],
        compiler_params=pltpu.CompilerParams(
            dimension_semantics=("arbitrary",), vmem_limit_bytes=VMEM_LIMIT),
        name="peer",
    )(h1, h1, wq, keys, flat, cmask, uv_tab, g2, b2, *prev_args)


def _layer(h, w_in, conv_w, a_log, dt_bias, norm_w, f_bias, w_out_gdn, w_out_fox, w_o, ln1_g, ln1_b,
           peer_wq, peer_keys, peer_u, peer_v, ln2_g, ln2_b, alpha):
    b, s, d = h.shape
    t = b * s
    qk = GDN_HEADS * GDN_DK
    fw = FX_HEADS * FX_DH
    o_gz = 4 * qk
    o_ga = o_gz
    o_fq = o_ga + 2 * GDN_HEADS
    o_ff = o_fq + 3 * fw
    o_gate = o_ff + FX_HEADS
    w_big = jnp.concatenate([w_in[:, o_gate:], w_in[:, :o_gz]], axis=1).astype(BF16)
    pad = LANES - FX_DH
    w_fox = jnp.pad(w_in[:, o_fq:o_ff].reshape(d, 3 * FX_HEADS, FX_DH), ((0, 0), (0, 0), (0, pad)))
    w_fox = w_fox.reshape(d, 3 * FX_HEADS * LANES).astype(BF16)
    w_out_fox_p = jnp.pad(w_out_fox.reshape(FX_HEADS, FX_DH, d), ((0, 0), (0, pad), (0, 0)))
    w_out_fox_p = w_out_fox_p.reshape(FX_HEADS * LANES, d).astype(BF16)
    n_small = 2 * GDN_HEADS + FX_HEADS
    w_small = jnp.concatenate([w_in[:, o_ga:o_fq], w_in[:, o_ff:o_gate],
                               jnp.zeros((d, LANES - n_small), F32)], axis=1)
    params = jnp.zeros((8, LANES), F32)
    params = params.at[0, :GDN_HEADS].set(a_log).at[1, :GDN_HEADS].set(dt_bias)
    params = params.at[2, 2 * GDN_HEADS:n_small].set(f_bias)

    x2 = h.reshape(t, d)
    proj = _in_proj(x2, w_big, F32, "in_proj")
    pf = _in_proj(x2, w_fox, BF16, "in_proj_fox")
    gexp, bexp, c, ct = _prep(h, w_small, params)
    proj3 = proj.reshape(b, s, proj.shape[1])
    gdn0 = 2 * d // LANES
    oa = _gdn(proj3, gexp, bexp, conv_w, norm_w.reshape(1, LANES),
              (gdn0, gdn0 + GDN_HEADS, gdn0 + 2 * GDN_HEADS, gdn0 + 3 * GDN_HEADS))
    ob = _fox(pf.reshape(b, s, pf.shape[1]), c, ct)
    h1 = _mix(oa.reshape(t, qk), ob.reshape(t, FX_HEADS * LANES), proj, x2,
              w_out_gdn.astype(BF16), w_out_fox_p, w_o.astype(BF16),
              ln1_g.reshape(1, d), ln1_b.reshape(1, d), alpha)
    keys = peer_keys.reshape(2 * PEER_HEADS, PEER_NKEYS, peer_keys.shape[-1]).astype(BF16)
    ne = peer_u.shape[0]
    uslabs = d // (PEER_SUB * LANES)
    uv_tab = jnp.concatenate([peer_u.reshape(ne, uslabs, PEER_SUB, LANES),
                              peer_v.reshape(ne, uslabs, PEER_SUB, LANES)], axis=1)
    wq = peer_wq.astype(BF16)
    g2, b2 = ln2_g.reshape(1, d), ln2_b.reshape(1, d)
    grp = PEER_GROUP
    t_sc = (t * PEER_SC_SHARE_PCT // 100) // (2 * grp) * (2 * grp)
    t_tc = t - t_sc
    if t_sc:
        t_s1 = t_sc // 2
        ids_1, gate_1 = _route(h1, wq, keys, t_tc, t_s1)
        pre_1 = _sc_pre(ids_1, h1, peer_u, t_tc)
        ids_2, gate_2 = _route(h1, wq, keys, t_tc + t_s1, t_sc - t_s1)
        pre_2 = _sc_pre(ids_2, h1, peer_u, t_tc + t_s1)
        ids_sc = jnp.concatenate([ids_1, ids_2], axis=0)
        gate_sc = jnp.concatenate([gate_1, gate_2], axis=0)
        pre = jnp.concatenate([pre_1, pre_2], axis=0)
        t_a = (t_tc * PEER_TC_FIRST_PCT // 100) // grp * grp
        out = _peer(h1, wq, keys, uv_tab, g2, b2, alpha, 0, t_a)
        y_sc = _sc_out(ids_sc, _act(pre, gate_sc, out), peer_v)
        out = _peer(h1, wq, keys, uv_tab, g2, b2, alpha, t_a, t_tc - t_a, out)
        out = _ln2(h1, y_sc, g2, b2, alpha, t_tc, out)
    else:
        out = _peer(h1, wq, keys, uv_tab, g2, b2, alpha)
    return out.reshape(b, s, d)


def kernel(x, w_in, gdn_conv_w, gdn_a_log, gdn_dt_bias, gdn_norm_w, fox_f_bias, w_out_gdn, w_out_fox, w_o,
           ln1_g, ln1_b, peer_wq, peer_keys, peer_u, peer_v, ln2_g, ln2_b):
    depth = w_in.shape[0]
    alpha = (2.0 * depth) ** 0.25
    params = (w_in, gdn_conv_w, gdn_a_log, gdn_dt_bias, gdn_norm_w, fox_f_bias, w_out_gdn, w_out_fox, w_o,
              ln1_g, ln1_b, peer_wq, peer_keys, peer_u, peer_v, ln2_g, ln2_b)

    def layer_slice(p, l):
        return p.reshape(p.shape[1:]) if depth == 1 else p[l]

    h = x
    for l in range(depth):
        h = _layer(h, *(layer_slice(p, l) for p in params), alpha)
    return h
```

```python
import functools

import jax
import jax.numpy as jnp
from jax import lax
from jax.experimental import pallas as pl
from jax.experimental.pallas import tpu as pltpu
from jax.experimental.pallas import tpu_sc as plsc

F32 = jnp.float32
BF16 = jnp.bfloat16
HI = lax.Precision.HIGHEST

LANES = 128
CHUNK = 64
GDN_HEADS = 4
GDN_DK = 128
FX_HEADS = 8
FX_DH = 64
PEER_HEADS = 8
PEER_NKEYS = 128
PEER_TOPK = 16
PEER_SUB = 4
PEER_GROUP = 256
LN_EPS = 1e-5
VMEM_LIMIT = 48 * 1024 * 1024


def _dot(a, b, prec=None):
    return jnp.dot(a, b, preferred_element_type=F32, precision=prec)


def _dot_nt(a, b, prec=None):
    return lax.dot_general(a, b, (((1,), (1,)), ((), ())), preferred_element_type=F32, precision=prec)


def _dot_tn(a, b, prec=None):
    return lax.dot_general(a, b, (((0,), (0,)), ((), ())), preferred_element_type=F32, precision=prec)


def _softplus(x):
    return jnp.maximum(x, 0.0) + jnp.log1p(jnp.exp(-jnp.abs(x)))


def _layernorm(z, g, b):
    mu = jnp.mean(z, axis=-1, keepdims=True)
    zc = z - mu
    var = jnp.mean(zc * zc, axis=-1, keepdims=True)
    return zc * lax.rsqrt(var + LN_EPS) * g + b


def _mm_kernel(x_ref, w_ref, o_ref):
    o_ref[...] = _dot(x_ref[...].astype(BF16), w_ref[...]).astype(o_ref.dtype)


def _in_proj(x2, w_big, out_dtype, name):
    t, d = x2.shape
    n = w_big.shape[1]
    tm = min(1024, t)
    tn = 512
    return pl.pallas_call(
        _mm_kernel,
        grid=(t // tm, n // tn),
        in_specs=[pl.BlockSpec((tm, d), lambda i, j: (i, 0)),
                  pl.BlockSpec((d, tn), lambda i, j: (0, j))],
        out_specs=pl.BlockSpec((tm, tn), lambda i, j: (i, j)),
        out_shape=jax.ShapeDtypeStruct((t, n), out_dtype),
        compiler_params=pltpu.CompilerParams(
            dimension_semantics=("parallel", "parallel"), vmem_limit_bytes=VMEM_LIMIT),
        name=name,
    )(x2, w_big)


def _prep_kernel(x_ref, w_ref, par_ref, gexp_ref, bexp_ref, c_ref, ct_ref, carry_ref):
    ts = x_ref.shape[1]

    @pl.when(pl.program_id(1) == 0)
    def _():
        carry_ref[...] = jnp.zeros_like(carry_ref)

    small = _dot(x_ref[0], w_ref[...], HI)
    a_log = par_ref[0:1, :]
    dt_bias = par_ref[1:2, :]
    f_bias = par_ref[2:3, :]
    g = -jnp.exp(a_log) * _softplus(small + dt_bias)
    beta = jax.nn.sigmoid(small)
    lane = lax.broadcasted_iota(jnp.int32, (ts, LANES), 1)
    log_f = jnp.where((lane >= 8) & (lane < 16), -_softplus(-(small + f_bias)), 0.0)
    row = lax.broadcasted_iota(jnp.int32, (ts, ts), 0)
    col = lax.broadcasted_iota(jnp.int32, (ts, ts), 1)
    tril = (row >= col).astype(F32)
    c = _dot(tril, log_f, HI) + carry_ref[...]
    carry_ref[...] = c[ts - 1:ts, :]
    c_ref[0] = c
    ct_ref[0] = c.T[8:16, :]
    gexp_ref[0] = jnp.concatenate(
        [jnp.broadcast_to(g[:, h:h + 1], (ts, LANES)) for h in range(GDN_HEADS)], axis=1)
    bexp_ref[0] = jnp.concatenate(
        [jnp.broadcast_to(beta[:, GDN_HEADS + h:GDN_HEADS + h + 1], (ts, LANES)) for h in range(GDN_HEADS)], axis=1)


def _prep(x, w_small, params):
    b, s, d = x.shape
    ts = min(512, s)
    hw = GDN_HEADS * LANES
    return pl.pallas_call(
        _prep_kernel,
        grid=(b, s // ts),
        in_specs=[pl.BlockSpec((1, ts, d), lambda i, j: (i, j, 0)),
                  pl.BlockSpec((d, LANES), lambda i, j: (0, 0)),
                  pl.BlockSpec((8, LANES), lambda i, j: (0, 0))],
        out_specs=[pl.BlockSpec((1, ts, hw), lambda i, j: (i, j, 0)),
                   pl.BlockSpec((1, ts, hw), lambda i, j: (i, j, 0)),
                   pl.BlockSpec((1, ts, LANES), lambda i, j: (i, j, 0)),
                   pl.BlockSpec((1, 8, ts), lambda i, j: (i, 0, j))],
        out_shape=[jax.ShapeDtypeStruct((b, s, hw), F32),
                   jax.ShapeDtypeStruct((b, s, hw), F32),
                   jax.ShapeDtypeStruct((b, s, LANES), F32),
                   jax.ShapeDtypeStruct((b, 8, s), F32)],
        scratch_shapes=[pltpu.VMEM((1, LANES), F32)],
        compiler_params=pltpu.CompilerParams(
            dimension_semantics=("parallel", "arbitrary"), vmem_limit_bytes=VMEM_LIMIT),
        name="prep",
    )(x, w_small, params)


def _each(fn, *lists):
    return [fn(*args) for args in zip(*lists)]


def _unit_lower_inverse(ms, masks):
    eye, blk16, blk32 = masks
    hi = lambda a, b: _dot(a, b, HI)
    n1 = _each(lambda m: -jnp.where(blk16, m, 0.0), ms)
    l1 = _each(lambda m: jnp.where(blk32 & jnp.logical_not(blk16), m, 0.0), ms)
    l2 = _each(lambda m: jnp.where(blk32, 0.0, m), ms)
    n2 = _each(hi, n1, n1)
    yield
    p = _each(lambda a, b: hi(eye + a, eye + b), n1, n2)
    n4 = _each(hi, n2, n2)
    yield
    p = _each(lambda a, b: hi(a, eye + b), p, n4)
    n8 = _each(hi, n4, n4)
    yield
    d_inv = _each(lambda a, b: hi(a, eye + b), p, n8)
    yield
    dl = _each(hi, d_inv, l1)
    yield
    a32 = _each(lambda a, b: a - hi(b, a), d_inv, dl)
    yield
    al = _each(hi, a32, l2)
    yield
    return _each(lambda a, b: a - hi(b, a), a32, al)


def _emit_zipped(main, side):
    live = [main, side]
    while live:
        for gen in list(live):
            try:
                next(gen)
            except StopIteration:
                live.remove(gen)


def _gdn_kernel(q_ref, k_ref, v_ref, z_ref, g_ref, b_ref, cwq_ref, cwk_ref, cwv_ref, nw_ref,
                o_ref, qn, kn, vn, st, sol_s, qk_s, qg_s, kd_s, gl_s):
    s = q_ref.shape[1]
    c = CHUNK
    nh = q_ref.shape[2] // LANES
    row = lax.broadcasted_iota(jnp.int32, (s, LANES), 0)

    def conv_silu(x, w):
        y = x * w[3:4, :]
        for sh in (1, 2, 3):
            xs = jnp.where(row >= sh, pltpu.roll(x, sh, axis=0), 0.0)
            y = y + xs * w[3 - sh:4 - sh, :]
        return y * jax.nn.sigmoid(y)

    def l2norm(x):
        return x * lax.rsqrt(jnp.sum(x * x, axis=-1, keepdims=True) + 1e-6)

    for j in range(nh):
        hs = slice(j * LANES, (j + 1) * LANES)
        qn[:, hs] = l2norm(conv_silu(q_ref[0, :, hs], cwq_ref[:, hs])) * (GDN_DK ** -0.5)
        kn[:, hs] = l2norm(conv_silu(k_ref[0, :, hs], cwk_ref[:, hs]))
        vn[:, hs] = conv_silu(v_ref[0, :, hs], cwv_ref[:, hs])
    st[...] = jnp.zeros_like(st)

    ri = lax.broadcasted_iota(jnp.int32, (c, c), 0)
    ci = lax.broadcasted_iota(jnp.int32, (c, c), 1)
    tril = ri >= ci
    strict = ri > ci
    t_inc = tril.astype(F32)
    eye = (ri == ci).astype(F32)
    blk16 = (ri >> 4) == (ci >> 4)
    blk32 = (ri >> 5) == (ci >> 5)
    l2 = lax.broadcasted_iota(jnp.int32, (c, 2 * LANES), 0)
    j2 = lax.broadcasted_iota(jnp.int32, (c, 2 * LANES), 1)
    ux = jnp.where((j2 >= c) | (l2 > j2), 1.0, 0.0).astype(F32)
    nw = nw_ref[...]

    heads = list(range(nh))
    lanes_of = [slice(j * LANES, (j + 1) * LANES) for j in heads]

    def local_stage(ns):
        pairs = [(i, j) for i in range(len(ns)) for j in heads]
        rows = [pl.ds(pl.multiple_of(n * c, c), c) for n in ns]
        q = [qn[rows[i], lanes_of[j]] for i, j in pairs]
        k = [kn[rows[i], lanes_of[j]] for i, j in pairs]
        v = [vn[rows[i], lanes_of[j]] for i, j in pairs]
        gb = [g_ref[0, rows[i], lanes_of[j]] for i, j in pairs]
        bb = [b_ref[0, rows[i], lanes_of[j]] for i, j in pairs]
        d = _each(lambda g: _dot(t_inc, jnp.concatenate([g, g], axis=1) * ux, HI), gb)
        kb = _each(lambda a, b: a * b, k, bb)
        kk = _each(lambda a, b: _dot_nt(a, b, HI), kb, k)
        qk = _each(lambda a, b: _dot_nt(a.astype(BF16), b.astype(BF16)), q, k)
        yield
        gc = [x[:, LANES:] for x in d]
        decay = [jnp.where(tril, jnp.exp(x[:, :c]), 0.0) for x in d]
        m = _each(lambda a, b: jnp.where(strict, a * b, 0.0), kk, decay)
        a_inv = yield from _unit_lower_inverse(m, (eye, blk16, blk32))
        yield
        egc = _each(jnp.exp, gc)
        rhs = _each(lambda vv, b, kbb, e: jnp.concatenate([vv * b, kbb * e], axis=1), v, bb, kb, egc)
        sol = _each(lambda a, r: _dot(a, r, HI), a_inv, rhs)
        gl = [x[c - 1:c, :] for x in gc]
        yield
        for p, (i, j) in enumerate(pairs):
            n = ns[i]
            qk_s[n, j] = (qk[p] * decay[p]).astype(BF16)
            qg_s[n, j] = (q[p] * egc[p]).astype(BF16)
            kd_s[n, j] = (k[p] * jnp.exp(gl[p] - gc[p])).astype(BF16)
            gl_s[n, j] = gl[p]
            sol_s[n, j] = sol[p]

    def state_stage(ns):
        for n in ns:
            rows = pl.ds(pl.multiple_of(n * c, c), c)
            state = [st[j] for j in heads]
            state_b = [x.astype(BF16) for x in state]
            v_new = [sol_s[n, j, :, :LANES] - _dot(sol_s[n, j, :, LANES:].astype(BF16), state_b[j])
                     for j in heads]
            yield
            v_new_b = [x.astype(BF16) for x in v_new]
            o = [_dot(qg_s[n, j], state_b[j]) + _dot(qk_s[n, j], v_new_b[j]) for j in heads]
            new_state = [state[j] * jnp.exp(gl_s[n, j]) + _dot_tn(kd_s[n, j], v_new_b[j]) for j in heads]
            yield
            for j in heads:
                st[j] = new_state[j]
                on = o[j] * lax.rsqrt(jnp.mean(o[j] * o[j], axis=-1, keepdims=True) + 1e-6) * nw
                z = z_ref[0, rows, lanes_of[j]]
                o_ref[0, rows, lanes_of[j]] = on * (z * jax.nn.sigmoid(z))
            yield

    nchunks = s // c
    group = 4
    ngroups = nchunks // group

    def chunks_of(gidx):
        return [gidx * group + i for i in range(group)]

    def run(gen):
        for _ in gen:
            pass

    run(local_stage(chunks_of(jnp.int32(0))))

    def both(gidx, carry):
        _emit_zipped(local_stage(chunks_of(gidx)), state_stage(chunks_of(gidx - 1)))
        return carry

    lax.fori_loop(1, ngroups, both, 0)
    run(state_stage(chunks_of(jnp.int32(ngroups - 1))))


def _gdn(proj3, gexp, bexp, conv_w, norm_w, cols):
    b, s, _ = proj3.shape
    nh = 2
    nc = s // CHUNK
    wd = nh * LANES
    cq, ck, cv, cz = (c0 // nh for c0 in cols)

    def blk(c0):
        return pl.BlockSpec((1, s, wd), lambda i, h: (i, 0, c0 + h))

    def cw(c0):
        return pl.BlockSpec((conv_w.shape[0], wd), lambda i, h: (0, c0 + h))

    head = pl.BlockSpec((1, s, wd), lambda i, h: (i, 0, h))
    return pl.pallas_call(
        _gdn_kernel,
        grid=(b, GDN_HEADS // nh),
        in_specs=[blk(cq), blk(ck), blk(cv), blk(cz), head, head,
                  cw(0), cw(GDN_HEADS // nh), cw(2 * GDN_HEADS // nh),
                  pl.BlockSpec((1, LANES), lambda i, h: (0, 0))],
        out_specs=head,
        out_shape=jax.ShapeDtypeStruct((b, s, GDN_HEADS * LANES), F32),
        scratch_shapes=[pltpu.VMEM((s, wd), F32), pltpu.VMEM((s, wd), F32),
                        pltpu.VMEM((s, wd), F32), pltpu.VMEM((nh, GDN_DK, LANES), F32),
                        pltpu.VMEM((nc, nh, CHUNK, 2 * LANES), F32), pltpu.VMEM((nc, nh, CHUNK, CHUNK), BF16),
                        pltpu.VMEM((nc, nh, CHUNK, LANES), BF16), pltpu.VMEM((nc, nh, CHUNK, LANES), BF16),
                        pltpu.VMEM((nc, nh, 1, LANES), F32)],
        compiler_params=pltpu.CompilerParams(
            dimension_semantics=("parallel", "parallel"), vmem_limit_bytes=VMEM_LIMIT),
        name="gdn",
    )(proj3, proj3, proj3, proj3, gexp, bexp, conv_w, conv_w, conv_w, norm_w)


def _fox_kernel(q_ref, k_ref, v_ref, c_ref, ct_ref, o_ref, *, tk):
    tq = q_ref.shape[1]
    nj = q_ref.shape[2] // LANES
    g = pl.program_id(1)
    qi = pl.program_id(2)
    q = q_ref[0]
    cblk = c_ref[0]
    lane = lax.broadcasted_iota(jnp.int32, (tq, LANES), 1)
    qpos = qi * tq + lax.broadcasted_iota(jnp.int32, (tq, tk), 0)
    kofs = lax.broadcasted_iota(jnp.int32, (tq, tk), 1)
    heads = list(range(nj))
    lanes_of = [slice(j * LANES, (j + 1) * LANES) for j in heads]
    ccol = [jnp.sum(jnp.where(lane == 8 + g * nj + j, cblk, 0.0), axis=-1, keepdims=True) for j in heads]
    qh = [q[:, hs] * jnp.asarray(FX_DH ** -0.5, BF16) for hs in lanes_of]

    def body(kv, carry):
        k0 = pl.multiple_of(kv * tk, tk)
        causal = qpos >= k0 + kofs
        kblk = k_ref[0, pl.ds(k0, tk), :]
        vblk = v_ref[0, pl.ds(k0, tk), :]
        sc = [_dot_nt(qh[j], kblk[:, lanes_of[j]]) for j in heads]
        crow = [ct_ref[0, pl.ds(g * nj + j, 1), pl.ds(k0, tk)] for j in heads]
        sc = [jnp.where(causal, sc[j] + ccol[j] - crow[j], -1e30) for j in heads]
        m_new = [jnp.maximum(carry[j][0], jnp.max(sc[j], axis=-1, keepdims=True)) for j in heads]
        a = [jnp.exp(carry[j][0] - m_new[j]) for j in heads]
        p = [jnp.exp(sc[j] - m_new[j]) for j in heads]
        l = [a[j] * carry[j][1] + jnp.sum(p[j], axis=-1, keepdims=True) for j in heads]
        acc = [a[j] * carry[j][2] + _dot(p[j].astype(BF16), vblk[:, lanes_of[j]]) for j in heads]
        return tuple((m_new[j], l[j], acc[j]) for j in heads)

    init = tuple((jnp.full((tq, 1), -1e30, F32), jnp.zeros((tq, 1), F32), jnp.zeros((tq, LANES), F32))
                 for _ in heads)
    nkv = (qi * tq + tq - 1) // tk + 1
    res = lax.fori_loop(0, nkv, body, init)
    o_ref[0] = jnp.concatenate([acc / l for _, l, acc in res], axis=1)


def _fox(pf3, c, ct):
    b, s, _ = pf3.shape
    wd = 4 * LANES
    tq = min(128, s)
    tk = min(256, s)
    ngrp = FX_HEADS * LANES // wd
    return pl.pallas_call(
        functools.partial(_fox_kernel, tk=tk),
        grid=(b, ngrp, s // tq),
        in_specs=[pl.BlockSpec((1, tq, wd), lambda i, h, t: (i, t, h)),
                  pl.BlockSpec((1, s, wd), lambda i, h, t: (i, 0, ngrp + h)),
                  pl.BlockSpec((1, s, wd), lambda i, h, t: (i, 0, 2 * ngrp + h)),
                  pl.BlockSpec((1, tq, LANES), lambda i, h, t: (i, t, 0)),
                  pl.BlockSpec((1, 8, s), lambda i, h, t: (i, 0, 0))],
        out_specs=pl.BlockSpec((1, tq, wd), lambda i, h, t: (i, t, h)),
        out_shape=jax.ShapeDtypeStruct((b, s, ngrp * wd), F32),
        compiler_params=pltpu.CompilerParams(
            dimension_semantics=("parallel", "parallel", "parallel"), vmem_limit_bytes=VMEM_LIMIT),
        name="fox",
    )(pf3, pf3, pf3, c, ct)


def _mix_kernel(oa_ref, ob_ref, ga_ref, gb_ref, x_ref, wa_ref, wb_ref, wo_ref, g1_ref, b1_ref, o_ref, *, alpha):
    ya = _dot(oa_ref[...].astype(BF16), wa_ref[...])
    yb = _dot(ob_ref[...].astype(BF16), wb_ref[...])
    mix = jax.nn.sigmoid(ga_ref[...]) * ya + jax.nn.sigmoid(gb_ref[...]) * yb
    z = alpha * x_ref[...] + _dot(mix.astype(BF16), wo_ref[...])
    o_ref[...] = _layernorm(z, g1_ref[...], b1_ref[...])


def _mix(oa, ob, proj, x2, wa, wb, wo, g1, b1, alpha):
    t, d = x2.shape
    tm = min(512, t)
    w, w2 = oa.shape[1], ob.shape[1]
    full = lambda r, c: pl.BlockSpec((r, c), lambda i: (0, 0))
    return pl.pallas_call(
        functools.partial(_mix_kernel, alpha=alpha),
        grid=(t // tm,),
        in_specs=[pl.BlockSpec((tm, w), lambda i: (i, 0)),
                  pl.BlockSpec((tm, w2), lambda i: (i, 0)),
                  pl.BlockSpec((tm, d), lambda i: (i, 0)),
                  pl.BlockSpec((tm, d), lambda i: (i, 1)),
                  pl.BlockSpec((tm, d), lambda i: (i, 0)),
                  full(w, d), full(w2, d), full(d, d), full(1, d), full(1, d)],
        out_specs=pl.BlockSpec((tm, d), lambda i: (i, 0)),
        out_shape=jax.ShapeDtypeStruct((t, d), F32),
        compiler_params=pltpu.CompilerParams(
            dimension_semantics=("parallel",), vmem_limit_bytes=VMEM_LIMIT),
        name="mix",
    )(oa, ob, proj, proj, x2, wa, wb, wo, g1, b1)


def _route_head(keys_ref, q_scr, hd, lh, scr, flat, cmask):
    s_scr, i_scr, best_scr, eh_scr = scr
    nk, kk = PEER_NKEYS, PEER_TOPK
    iota_k = lax.broadcasted_iota(jnp.int32, (nk, LANES), 0)
    neg = jnp.float32(-jnp.inf)
    for p in range(2):
        vals = _dot_nt(keys_ref[hd * 2 + p], q_scr[hd * 2 + p, lh])
        for r in range(kk):
            m = jnp.max(vals, axis=0, keepdims=True)
            am = jnp.min(jnp.where(vals == m, iota_k, nk), axis=0, keepdims=True)
            s_scr[p * kk + r:p * kk + r + 1, :] = m
            i_scr[p * kk + r:p * kk + r + 1, :] = am
            vals = jnp.where(iota_k == am, neg, vals)
    s1 = s_scr[kk:kk + 8, :]
    i1 = i_scr[kk:kk + 8, :]
    cand = [s_scr[0:1, :] + s_scr[kk:2 * kk, :]]
    cidx = [i_scr[0:1, :] * nk + i_scr[kk:2 * kk, :]]
    for a in range(1, 8):
        cand.append(s_scr[a:a + 1, :] + s1)
        cidx.append(i_scr[a:a + 1, :] * nk + i1)
    cand.append(s_scr[8:kk, :] + s_scr[kk:kk + 1, :])
    cidx.append(i_scr[8:kk, :] * nk + i_scr[kk:kk + 1, :])
    vals = jnp.concatenate(cand, axis=0) + cmask
    cidx = jnp.concatenate(cidx, axis=0)
    for r in range(kk):
        m = jnp.max(vals, axis=0, keepdims=True)
        am = jnp.min(jnp.where(vals == m, flat, 2 * kk * kk), axis=0, keepdims=True)
        sel = flat == am
        best_scr[r:r + 1, :] = m
        eh_scr[r:r + 1, :] = jnp.max(jnp.where(sel, cidx, -1), axis=0, keepdims=True)
        vals = jnp.where(sel, neg, vals)
    bs = best_scr[...]
    ex = jnp.exp(bs - bs[0:1, :])
    return ex / jnp.sum(ex, axis=0, keepdims=True), eh_scr[...]


def _candidate_tables():
    kk = PEER_TOPK
    pairs = [(0, bb) for bb in range(kk)]
    for a in range(1, 8):
        pairs += [(a, bb) for bb in range(8)]
    pairs += [(a, 0) for a in range(8, kk)]
    real = [(a + 1) * (bb + 1) <= kk for a, bb in pairs]
    flat = [a * kk + bb if ok else kk * kk + r for r, ((a, bb), ok) in enumerate(zip(pairs, real))]
    flat = jnp.broadcast_to(jnp.asarray(flat, jnp.int32)[:, None], (len(pairs), LANES))
    cmask = jnp.broadcast_to(jnp.asarray([0.0 if ok else -jnp.inf for ok in real], F32)[:, None],
                             (len(pairs), LANES))
    return flat, cmask


def _route_kernel(h_ref, wq_ref, keys_ref, flat_ref, cmask_ref, e_ref, g_ref,
                  q_scr, s_scr, i_scr, best_scr, eh_scr, eall_scr, gall_scr):
    tg = h_ref.shape[0]
    kk = PEER_TOPK
    nlh = tg // LANES
    q = _dot(h_ref[...].astype(BF16), wq_ref[...])
    for j in range(2 * PEER_HEADS):
        for lh in range(nlh):
            q_scr[j, lh] = q[lh * LANES:(lh + 1) * LANES, j * LANES:(j + 1) * LANES].astype(BF16)
    flat = flat_ref[...]
    cmask = cmask_ref[...]

    def piece(p, carry):
        hd = p // nlh
        lh = p % nlh
        gates, ids = _route_head(keys_ref, q_scr, hd, lh, (s_scr, i_scr, best_scr, eh_scr), flat, cmask)
        r0 = pl.multiple_of(hd * kk, kk)
        gall_scr[lh, pl.ds(r0, kk), :] = gates
        eall_scr[lh, pl.ds(r0, kk), :] = ids
        return carry

    lax.fori_loop(0, PEER_HEADS * nlh, piece, 0)
    for lh in range(nlh):
        e_ref[lh * LANES:(lh + 1) * LANES, :] = eall_scr[lh].T
        g_ref[lh * LANES:(lh + 1) * LANES, :] = gall_scr[lh].T


def _route(h, wq, keys, row0, t):
    d = h.shape[1]
    tg = PEER_GROUP
    g0 = row0 // tg
    nq = wq.shape[1]
    kk = PEER_TOPK
    hk = PEER_HEADS * kk
    nlh = tg // LANES
    flat, cmask = _candidate_tables()
    const = lambda shape: pl.BlockSpec(shape, lambda i: (0,) * len(shape))
    return pl.pallas_call(
        _route_kernel,
        grid=(t // tg,),
        in_specs=[pl.BlockSpec((tg, d), lambda i: (g0 + i, 0)),
                  const((d, nq)), const(keys.shape), const(flat.shape), const(cmask.shape)],
        out_specs=[pl.BlockSpec((tg, hk), lambda i: (i, 0)), pl.BlockSpec((tg, hk), lambda i: (i, 0))],
        out_shape=[jax.ShapeDtypeStruct((t, hk), jnp.int32), jax.ShapeDtypeStruct((t, hk), F32)],
        scratch_shapes=[pltpu.VMEM((2 * PEER_HEADS, nlh, LANES, LANES), BF16),
                        pltpu.VMEM((2 * kk, LANES), F32), pltpu.VMEM((2 * kk, LANES), jnp.int32),
                        pltpu.VMEM((kk, LANES), F32), pltpu.VMEM((kk, LANES), jnp.int32),
                        pltpu.VMEM((nlh, hk, LANES), jnp.int32), pltpu.VMEM((nlh, hk, LANES), F32)],
        compiler_params=pltpu.CompilerParams(
            dimension_semantics=("parallel",), vmem_limit_bytes=VMEM_LIMIT),
        name="route",
    )(h, wq, keys, flat, cmask)


PEER_SC_SHARE_PCT = 47
PEER_TC_FIRST_PCT = 49
SC_WORKERS = 32
SC_LANES = 16
SC_CHUNK = 32
SC_DBLK = 16
SC_EBLK = 4
SC_JUNROLL = 8


def _sc_mesh():
    return plsc.VectorSubcoreMesh(core_axis_name="c", subcore_axis_name="s")


def _sc_worker():
    return lax.axis_index("s") * 2 + lax.axis_index("c")


def _sc_pre(ids, h, u_tab, row0):
    ts, hk = ids.shape
    d = h.shape[1]
    per_w = ts // SC_WORKERS
    nch = hk // SC_CHUNK

    def body(ids_hbm, h_hbm, u_hbm, out_hbm, idx_a, idx_b, h_a, h_b, rows_a, rows_b, pre_v,
             sem_a, sem_b, tsem_a, tsem_b):
        base = _sc_worker() * per_w
        lane = lax.iota(jnp.int32, SC_LANES)
        bufs = (rows_a, rows_b)
        sems = (sem_a, sem_b)
        idxs, hs, tsems = (idx_a, idx_b), (h_a, h_b), (tsem_a, tsem_b)

        def token_copies(t, s):
            return (pltpu.make_async_copy(ids_hbm.at[t], idxs[s], tsems[s]),
                    pltpu.make_async_copy(h_hbm.at[row0 + t], hs[s], tsems[s]))

        def prefetch(t, s):
            for cp in token_copies(t, s):
                cp.start()

        def arrived(t, s):
            for cp in token_copies(t, s):
                cp.wait()

        def process(t, s):
            idx_v, h_v = idxs[s], hs[s]

            def gather(c):
                return pltpu.async_copy(u_hbm.at[idx_v.at[pl.ds(c * SC_CHUNK, SC_CHUNK)]], bufs[c % 2], sems[c % 2])

            cps = {0: gather(0)}
            for c in range(nch):
                if c + 1 < nch:
                    cps[c + 1] = gather(c + 1)
                cps[c].wait()
                rows = bufs[c % 2]
                for g in range(SC_CHUNK // SC_LANES):
                    def experts(q, vec, g=g, rows=rows):
                        e0 = g * SC_LANES + q * SC_EBLK

                        def span(jb, accs, rows=rows, e0=e0):
                            accs = list(accs)
                            for jj in range(SC_JUNROLL):
                                sl = pl.ds((jb * SC_JUNROLL + jj) * SC_LANES, SC_LANES)
                                hv = h_v[sl]
                                for i in range(SC_EBLK):
                                    accs[i] = accs[i] + rows[e0 + i, sl] * hv
                            return tuple(accs)
                        accs = lax.fori_loop(0, d // (SC_LANES * SC_JUNROLL), span,
                                             tuple(jnp.zeros((SC_LANES,), F32) for _ in range(SC_EBLK)))
                        for i in range(SC_EBLK):
                            vec = jnp.where(lane == q * SC_EBLK + i, jnp.sum(accs[i]), vec)
                        return vec
                    vec = lax.fori_loop(0, SC_LANES // SC_EBLK, experts, jnp.zeros((SC_LANES,), F32))
                    pre_v[pl.ds(c * SC_CHUNK + g * SC_LANES, SC_LANES)] = vec
            pltpu.sync_copy(pre_v, out_hbm.at[t])

        last = base + per_w - 1
        prefetch(base, 0)

        def pair(i, carry):
            t0 = base + 2 * i
            prefetch(t0 + 1, 1)
            arrived(t0, 0)
            process(t0, 0)
            nxt = jnp.minimum(t0 + 2, last)
            prefetch(nxt, 0)
            arrived(t0 + 1, 1)
            process(t0 + 1, 1)
            return carry

        lax.fori_loop(0, per_w // 2, pair, 0)
        arrived(last, 0)

    assert per_w % 2 == 0
    return pl.kernel(
        body, out_type=jax.ShapeDtypeStruct((ts, hk), F32), mesh=_sc_mesh(),
        scratch_types=[pltpu.VMEM((hk,), jnp.int32), pltpu.VMEM((hk,), jnp.int32),
                       pltpu.VMEM((d,), F32), pltpu.VMEM((d,), F32),
                       pltpu.VMEM((SC_CHUNK, d), F32), pltpu.VMEM((SC_CHUNK, d), F32),
                       pltpu.VMEM((hk,), F32), pltpu.SemaphoreType.DMA, pltpu.SemaphoreType.DMA,
                       pltpu.SemaphoreType.DMA, pltpu.SemaphoreType.DMA],
        compiler_params=pltpu.CompilerParams(needs_layout_passes=False),
        name="sc_pre",
    )(ids, h, u_tab)


def _sc_out(ids, act, v_tab):
    ts, hk = ids.shape
    d = v_tab.shape[1]
    per_w = ts // SC_WORKERS
    nch = hk // SC_CHUNK

    def body(ids_hbm, act_hbm, v_hbm, out_hbm, idx_v, act_v, rows_a, rows_b, y_v, sem_a, sem_b):
        base = _sc_worker() * per_w
        bufs = (rows_a, rows_b)
        sems = (sem_a, sem_b)

        def gather(c):
            return pltpu.async_copy(v_hbm.at[idx_v.at[pl.ds(c * SC_CHUNK, SC_CHUNK)]], bufs[c % 2], sems[c % 2])

        def token(i, carry):
            t = base + i
            pltpu.sync_copy(ids_hbm.at[t], idx_v)
            pltpu.sync_copy(act_hbm.at[t], act_v)
            for j in range(d // SC_LANES):
                y_v[pl.ds(j * SC_LANES, SC_LANES)] = jnp.zeros((SC_LANES,), F32)
            cps = {0: gather(0)}
            for c in range(nch):
                if c + 1 < nch:
                    cps[c + 1] = gather(c + 1)
                cps[c].wait()
                rows = bufs[c % 2]
                for db in range(d // (SC_DBLK * SC_LANES)):
                    def expert(r, accs, c=c, rows=rows, db=db):
                        a = plsc.load_gather(act_v, [jnp.zeros((SC_LANES,), jnp.int32) + (c * SC_CHUNK + r)])
                        return tuple(
                            accs[j] + a * rows[r, pl.ds((db * SC_DBLK + j) * SC_LANES, SC_LANES)]
                            for j in range(SC_DBLK))
                    accs = lax.fori_loop(0, SC_CHUNK, expert,
                                         tuple(jnp.zeros((SC_LANES,), F32) for _ in range(SC_DBLK)))
                    for j in range(SC_DBLK):
                        plsc.addupdate(y_v.at[pl.ds((db * SC_DBLK + j) * SC_LANES, SC_LANES)], accs[j])
            pltpu.sync_copy(y_v, out_hbm.at[t])
            return carry

        lax.fori_loop(0, per_w, token, 0)

    return pl.kernel(
        body, out_type=jax.ShapeDtypeStruct((ts, d), F32), mesh=_sc_mesh(),
        scratch_types=[pltpu.VMEM((hk,), jnp.int32), pltpu.VMEM((hk,), F32),
                       pltpu.VMEM((SC_CHUNK, d), F32), pltpu.VMEM((SC_CHUNK, d), F32),
                       pltpu.VMEM((d,), F32), pltpu.SemaphoreType.DMA, pltpu.SemaphoreType.DMA],
        compiler_params=pltpu.CompilerParams(needs_layout_passes=False),
        name="sc_out",
    )(ids, act, v_tab)


def _act_kernel(pre_ref, gate_ref, after_ref, o_ref):
    del after_ref
    pre = pre_ref[...]
    o_ref[...] = 0.5 * pre * (1.0 + lax.erf(pre * (2.0 ** -0.5))) * gate_ref[...]


def _act(pre, gate, after):
    t, hk = pre.shape
    tm = PEER_GROUP
    assert t % tm == 0
    spec = pl.BlockSpec((tm, hk), lambda i: (i, 0))
    return pl.pallas_call(
        _act_kernel, grid=(t // tm,),
        in_specs=[spec, spec, pl.BlockSpec((8, LANES), lambda i: (0, 0))], out_specs=spec,
        out_shape=jax.ShapeDtypeStruct((t, hk), F32),
        compiler_params=pltpu.CompilerParams(dimension_semantics=("parallel",)),
        name="peer_act",
    )(pre, gate, after)


def _ln2_kernel(h_ref, y_ref, g_ref, b_ref, prev_ref, o_ref, *, alpha):
    del prev_ref
    o_ref[...] = _layernorm(alpha * h_ref[...] + y_ref[...], g_ref[...], b_ref[...])


def _ln2(h, y, g2, b2, alpha, row0, out_prev):
    t, d = y.shape
    tm = PEER_GROUP
    assert t % tm == 0 and row0 % tm == 0
    g0 = row0 // tm
    spec_full = pl.BlockSpec((tm, d), lambda i: (g0 + i, 0))
    vec = pl.BlockSpec((1, d), lambda i: (0, 0))
    return pl.pallas_call(
        functools.partial(_ln2_kernel, alpha=alpha), grid=(t // tm,),
        in_specs=[spec_full, pl.BlockSpec((tm, d), lambda i: (i, 0)), vec, vec,
                  pl.BlockSpec(memory_space=pl.ANY)],
        out_specs=spec_full,
        out_shape=jax.ShapeDtypeStruct(out_prev.shape, F32),
        input_output_aliases={4: 0},
        compiler_params=pltpu.CompilerParams(dimension_semantics=("parallel",)),
        name="peer_ln2",
    )(h, y, g2, b2, out_prev)


def _peer_kernel(h_cur_ref, h_nxt_ref, wq_ref, keys_ref, flat_ref, cmask_ref, uv_hbm, g2_ref, b2_ref, *rest,
                 tt, alpha, has_prev):
    (o_ref, q_scr, s_scr, i_scr, best_scr, eh_scr, eall_scr, idv_scr, ids_smem, gate_scr,
     buf_a, buf_b, sem, idsem, y_scr) = rest[1:] if has_prev else rest
    _peer_body(h_cur_ref, h_nxt_ref, wq_ref, keys_ref, flat_ref, cmask_ref, uv_hbm, g2_ref, b2_ref, o_ref,
               q_scr, s_scr, i_scr, best_scr, eh_scr, eall_scr, idv_scr, ids_smem, gate_scr,
               buf_a, buf_b, sem, idsem, y_scr, tt=tt, alpha=alpha)


def _peer_body(h_cur_ref, h_nxt_ref, wq_ref, keys_ref, flat_ref, cmask_ref, uv_hbm, g2_ref, b2_ref, o_ref,
               q_scr, s_scr, i_scr, best_scr, eh_scr, eall_scr, idv_scr, ids_smem, gate_scr,
               buf_a, buf_b, sem, idsem, y_scr, *, tt, alpha):
    s = pl.program_id(0)
    tg, d = h_cur_ref.shape
    nk, kk = PEER_NKEYS, PEER_TOPK
    hk = PEER_HEADS * kk
    nlh = tg // LANES
    nsub = tg // (2 * tt)
    assert nsub == PEER_HEADS * nlh
    nslab, sub = uv_hbm.shape[1], uv_hbm.shape[2]
    half = nslab // 2
    bufs = (buf_a, buf_b)
    last = pl.num_programs(0) - 1
    rslot = s % 3
    pslot = (s + 2) % 3
    eslot = (s + 1) % 3

    def issue(idslot, row0, slot):
        for t in range(tt):
            for k in range(hk):
                e = ids_smem[idslot, row0 + t, k]
                pltpu.make_async_copy(uv_hbm.at[e], bufs[slot].at[:, pl.ds((t * hk + k) * sub, sub), :],
                                      sem.at[slot]).start(priority=k % 2)

    def wait(slot):
        pltpu.make_async_copy(bufs[slot], bufs[slot], sem.at[slot]).wait()

    q = _dot(h_nxt_ref[...].astype(BF16), wq_ref[...])
    for j in range(2 * PEER_HEADS):
        for lh in range(nlh):
            q_scr[j, lh] = q[lh * LANES:(lh + 1) * LANES, j * LANES:(j + 1) * LANES].astype(BF16)
    flat = flat_ref[...]
    cmask = cmask_ref[...]

    def route_piece(piece):
        hd = piece // nlh
        lh = piece % nlh
        gates, ids = _route_head(keys_ref, q_scr, hd, lh, (s_scr, i_scr, best_scr, eh_scr), flat, cmask)
        r0 = pl.multiple_of(hd * kk, kk)
        gate_scr[rslot, lh, pl.ds(r0, kk), :] = gates
        eall_scr[lh, pl.ds(r0, kk), :] = ids

    lane = lax.broadcasted_iota(jnp.int32, (hk, LANES), 1)

    def compute(row0, slot):
        buf = bufs[slot]
        gt = gate_scr[eslot, row0 // LANES]
        lane0 = row0 % LANES
        for t in range(tt):
            hrow = h_cur_ref[pl.ds(row0 + t, 1), :]

            def rows(a, c):
                return buf[a, pl.ds(t * hk * sub + c, hk, stride=sub), :]

            part = None
            for j in range(half):
                for c in range(sub):
                    seg = j * sub + c
                    term = rows(j, c) * hrow[:, seg * LANES:(seg + 1) * LANES]
                    part = term if part is None else part + term
            pre = jnp.sum(part, axis=-1, keepdims=True)
            gate = jnp.sum(jnp.where(lane == lane0 + t, gt, 0.0), axis=-1, keepdims=True)
            act = 0.5 * pre * (1.0 + lax.erf(pre * (2.0 ** -0.5))) * gate
            yrow = slot * tt + t
            for j in range(half):
                for c in range(sub):
                    seg = j * sub + c
                    y_scr[yrow:yrow + 1, seg * LANES:(seg + 1) * LANES] = jnp.sum(
                        act * rows(half + j, c), axis=0, keepdims=True)

    def substep(j, carry):
        row0 = pl.multiple_of(j * 2 * tt, 2 * tt)
        issue(eslot, row0 + tt, 1)
        route_piece(j)
        wait(0)
        compute(row0, 0)
        wait(1)
        wrap = j == nsub - 1
        issue(jnp.where(wrap, pslot, eslot), jnp.where(wrap, 0, row0 + 2 * tt), 0)
        compute(row0 + tt, 1)
        z = alpha * h_cur_ref[pl.ds(row0, 2 * tt), :] + y_scr[...]
        o_ref[pl.ds(row0, 2 * tt), :] = _layernorm(z, g2_ref[...], b2_ref[...])
        return carry

    @pl.when(s < 2)
    def _():
        o_ref[...] = jnp.zeros_like(o_ref)

        def piece(p, carry):
            route_piece(p)
            return carry
        lax.fori_loop(0, nsub, piece, 0)

    @pl.when(s == 1)
    def _():
        issue(pslot, 0, 0)

    @pl.when(s >= 2)
    def _():
        lax.fori_loop(0, nsub, substep, 0)

    @pl.when(s == last)
    def _():
        wait(0)

    for lh in range(nlh):
        idv_scr[lh * LANES:(lh + 1) * LANES, :] = eall_scr[lh].T
    publish = pltpu.make_async_copy(idv_scr, ids_smem.at[rslot], idsem)
    publish.start()
    publish.wait()


def _peer(h1, wq, keys, uv_tab, g2, b2, alpha, row0=0, t=None, out_prev=None):
    d = h1.shape[1]
    t = h1.shape[0] if t is None else t
    tg = PEER_GROUP
    tt = 8
    assert t % tg == 0 and row0 % tg == 0
    ngrp = t // tg
    g0 = row0 // tg
    has_prev = out_prev is not None
    nq = wq.shape[1]
    kk = PEER_TOPK
    hk = PEER_HEADS * kk
    nlh = tg // LANES
    nslab, sub = uv_tab.shape[1], uv_tab.shape[2]
    flat, cmask = _candidate_tables()
    const = lambda shape: pl.BlockSpec(shape, lambda i: (0,) * len(shape))
    prev_specs = [pl.BlockSpec(memory_space=pl.ANY)] if has_prev else []
    prev_args = [out_prev] if has_prev else []
    return pl.pallas_call(
        functools.partial(_peer_kernel, tt=tt, alpha=alpha, has_prev=has_prev),
        grid=(ngrp + 2,),
        in_specs=[pl.BlockSpec((tg, d), lambda i: (g0 + jnp.maximum(i - 2, 0), 0)),
                  pl.BlockSpec((tg, d), lambda i: (g0 + jnp.minimum(i, ngrp - 1), 0)),
                  const((d, nq)), const(keys.shape), const(flat.shape), const(cmask.shape),
                  pl.BlockSpec(memory_space=pl.ANY),
                  const((1, d)), const((1, d))] + prev_specs,
        out_specs=pl.BlockSpec((tg, d), lambda i: (g0 + jnp.maximum(i - 2, 0), 0)),
        out_shape=jax.ShapeDtypeStruct(h1.shape, F32),
        input_output_aliases={9: 0} if has_prev else {},
        scratch_shapes=[pltpu.VMEM((2 * PEER_HEADS, nlh, LANES, LANES), BF16),
                        pltpu.VMEM((2 * kk, LANES), F32), pltpu.VMEM((2 * kk, LANES), jnp.int32),
                        pltpu.VMEM((kk, LANES), F32), pltpu.VMEM((kk, LANES), jnp.int32),
                        pltpu.VMEM((nlh, hk, LANES), jnp.int32),
                        pltpu.VMEM((tg, hk), jnp.int32),
                        pltpu.SMEM((3, tg, hk), jnp.int32),
                        pltpu.VMEM((3, nlh, hk, LANES), F32),
                        pltpu.VMEM((nslab, tt * hk * sub, LANES), F32),
                        pltpu.VMEM((nslab, tt * hk * sub, LANES), F32),
                        pltpu.SemaphoreType.DMA((2,)), pltpu.SemaphoreType.DMA,
                        pltpu.VMEM((2 * tt, d), F32)],
        compiler_params=pltpu.CompilerParams(
            dimension_semantics=("arbitrary",), vmem_limit_bytes=VMEM_LIMIT),
        name="peer",
    )(h1, h1, wq, keys, flat, cmask, uv_tab, g2, b2, *prev_args)


def _layer(h, w_in, conv_w, a_log, dt_bias, norm_w, f_bias, w_out_gdn, w_out_fox, w_o, ln1_g, ln1_b,
           peer_wq, peer_keys, peer_u, peer_v, ln2_g, ln2_b, alpha):
    b, s, d = h.shape
    t = b * s
    qk = GDN_HEADS * GDN_DK
    fw = FX_HEADS * FX_DH
    o_gz = 4 * qk
    o_ga = o_gz
    o_fq = o_ga + 2 * GDN_HEADS
    o_ff = o_fq + 3 * fw
    o_gate = o_ff + FX_HEADS
    w_big = jnp.concatenate([w_in[:, o_gate:], w_in[:, :o_gz]], axis=1).astype(BF16)
    pad = LANES - FX_DH
    w_fox = jnp.pad(w_in[:, o_fq:o_ff].reshape(d, 3 * FX_HEADS, FX_DH), ((0, 0), (0, 0), (0, pad)))
    w_fox = w_fox.reshape(d, 3 * FX_HEADS * LANES).astype(BF16)
    w_out_fox_p = jnp.pad(w_out_fox.reshape(FX_HEADS, FX_DH, d), ((0, 0), (0, pad), (0, 0)))
    w_out_fox_p = w_out_fox_p.reshape(FX_HEADS * LANES, d).astype(BF16)
    n_small = 2 * GDN_HEADS + FX_HEADS
    w_small = jnp.concatenate([w_in[:, o_ga:o_fq], w_in[:, o_ff:o_gate],
                               jnp.zeros((d, LANES - n_small), F32)], axis=1)
    params = jnp.zeros((8, LANES), F32)
    params = params.at[0, :GDN_HEADS].set(a_log).at[1, :GDN_HEADS].set(dt_bias)
    params = params.at[2, 2 * GDN_HEADS:n_small].set(f_bias)

    x2 = h.reshape(t, d)
    proj = _in_proj(x2, w_big, F32, "in_proj")
    pf = _in_proj(x2, w_fox, BF16, "in_proj_fox")
    gexp, bexp, c, ct = _prep(h, w_small, params)
    proj3 = proj.reshape(b, s, proj.shape[1])
    gdn0 = 2 * d // LANES
    oa = _gdn(proj3, gexp, bexp, conv_w, norm_w.reshape(1, LANES),
              (gdn0, gdn0 + GDN_HEADS, gdn0 + 2 * GDN_HEADS, gdn0 + 3 * GDN_HEADS))
    ob = _fox(pf.reshape(b, s, pf.shape[1]), c, ct)
    h1 = _mix(oa.reshape(t, qk), ob.reshape(t, FX_HEADS * LANES), proj, x2,
              w_out_gdn.astype(BF16), w_out_fox_p, w_o.astype(BF16),
              ln1_g.reshape(1, d), ln1_b.reshape(1, d), alpha)
    keys = peer_keys.reshape(2 * PEER_HEADS, PEER_NKEYS, peer_keys.shape[-1]).astype(BF16)
    ne = peer_u.shape[0]
    uslabs = d // (PEER_SUB * LANES)
    uv_tab = jnp.concatenate([peer_u.reshape(ne, uslabs, PEER_SUB, LANES),
                              peer_v.reshape(ne, uslabs, PEER_SUB, LANES)], axis=1)
    wq = peer_wq.astype(BF16)
    g2, b2 = ln2_g.reshape(1, d), ln2_b.reshape(1, d)
    grp = PEER_GROUP
    t_sc = (t * PEER_SC_SHARE_PCT // 100) // (2 * grp) * (2 * grp)
    t_tc = t - t_sc
    if t_sc:
        t_s1 = t_sc // 2
        ids_1, gate_1 = _route(h1, wq, keys, t_tc, t_s1)
        pre_1 = _sc_pre(ids_1, h1, peer_u, t_tc)
        ids_2, gate_2 = _route(h1, wq, keys, t_tc + t_s1, t_sc - t_s1)
        pre_2 = _sc_pre(ids_2, h1, peer_u, t_tc + t_s1)
        ids_sc = jnp.concatenate([ids_1, ids_2], axis=0)
        gate_sc = jnp.concatenate([gate_1, gate_2], axis=0)
        pre = jnp.concatenate([pre_1, pre_2], axis=0)
        t_a = (t_tc * PEER_TC_FIRST_PCT // 100) // grp * grp
        out = _peer(h1, wq, keys, uv_tab, g2, b2, alpha, 0, t_a)
        y_sc = _sc_out(ids_sc, _act(pre, gate_sc, out), peer_v)
        out = _peer(h1, wq, keys, uv_tab, g2, b2, alpha, t_a, t_tc - t_a, out)
        out = _ln2(h1, y_sc, g2, b2, alpha, t_tc, out)
    else:
        out = _peer(h1, wq, keys, uv_tab, g2, b2, alpha)
    return out.reshape(b, s, d)


def kernel(x, w_in, gdn_conv_w, gdn_a_log, gdn_dt_bias, gdn_norm_w, fox_f_bias, w_out_gdn, w_out_fox, w_o,
           ln1_g, ln1_b, peer_wq, peer_keys, peer_u, peer_v, ln2_g, ln2_b):
    depth = w_in.shape[0]
    alpha = (2.0 * depth) ** 0.25
    params = (w_in, gdn_conv_w, gdn_a_log, gdn_dt_bias, gdn_norm_w, fox_f_bias, w_out_gdn, w_out_fox, w_o,
              ln1_g, ln1_b, peer_wq, peer_keys, peer_u, peer_v, ln2_g, ln2_b)

    def layer_slice(p, l):
        return p.reshape(p.shape[1:]) if depth == 1 else p[l]

    h = x
    for l in range(depth):
        h = _layer(h, *(layer_slice(p, l) for p in params), alpha)
    return h
```

```python
import functools

import jax
import jax.numpy as jnp
from jax import lax
from jax.experimental import pallas as pl
from jax.experimental.pallas import tpu as pltpu
from jax.experimental.pallas import tpu_sc as plsc

F32 = jnp.float32
BF16 = jnp.bfloat16
HI = lax.Precision.HIGHEST

LANES = 128
CHUNK = 64
GDN_HEADS = 4
GDN_DK = 128
FX_HEADS = 8
FX_DH = 64
PEER_HEADS = 8
PEER_NKEYS = 128
PEER_TOPK = 16
PEER_SUB = 4
PEER_GROUP = 256
LN_EPS = 1e-5
VMEM_LIMIT = 48 * 1024 * 1024


def _dot(a, b, prec=None):
    return jnp.dot(a, b, preferred_element_type=F32, precision=prec)


def _dot_nt(a, b, prec=None):
    return lax.dot_general(a, b, (((1,), (1,)), ((), ())), preferred_element_type=F32, precision=prec)


def _dot_tn(a, b, prec=None):
    return lax.dot_general(a, b, (((0,), (0,)), ((), ())), preferred_element_type=F32, precision=prec)


def _softplus(x):
    return jnp.maximum(x, 0.0) + jnp.log1p(jnp.exp(-jnp.abs(x)))


def _layernorm(z, g, b):
    mu = jnp.mean(z, axis=-1, keepdims=True)
    zc = z - mu
    var = jnp.mean(zc * zc, axis=-1, keepdims=True)
    return zc * lax.rsqrt(var + LN_EPS) * g + b


def _mm_kernel(x_ref, w_ref, o_ref):
    o_ref[...] = _dot(x_ref[...].astype(BF16), w_ref[...]).astype(o_ref.dtype)


def _in_proj(x2, w_big, out_dtype, name):
    t, d = x2.shape
    n = w_big.shape[1]
    tm = min(1024, t)
    tn = 512
    return pl.pallas_call(
        _mm_kernel,
        grid=(t // tm, n // tn),
        in_specs=[pl.BlockSpec((tm, d), lambda i, j: (i, 0)),
                  pl.BlockSpec((d, tn), lambda i, j: (0, j))],
        out_specs=pl.BlockSpec((tm, tn), lambda i, j: (i, j)),
        out_shape=jax.ShapeDtypeStruct((t, n), out_dtype),
        compiler_params=pltpu.CompilerParams(
            dimension_semantics=("parallel", "parallel"), vmem_limit_bytes=VMEM_LIMIT),
        name=name,
    )(x2, w_big)


def _prep_kernel(x_ref, w_ref, par_ref, gexp_ref, bexp_ref, c_ref, ct_ref, carry_ref):
    ts = x_ref.shape[1]

    @pl.when(pl.program_id(1) == 0)
    def _():
        carry_ref[...] = jnp.zeros_like(carry_ref)

    small = _dot(x_ref[0], w_ref[...], HI)
    a_log = par_ref[0:1, :]
    dt_bias = par_ref[1:2, :]
    f_bias = par_ref[2:3, :]
    g = -jnp.exp(a_log) * _softplus(small + dt_bias)
    beta = jax.nn.sigmoid(small)
    lane = lax.broadcasted_iota(jnp.int32, (ts, LANES), 1)
    log_f = jnp.where((lane >= 8) & (lane < 16), -_softplus(-(small + f_bias)), 0.0)
    row = lax.broadcasted_iota(jnp.int32, (ts, ts), 0)
    col = lax.broadcasted_iota(jnp.int32, (ts, ts), 1)
    tril = (row >= col).astype(F32)
    c = _dot(tril, log_f, HI) + carry_ref[...]
    carry_ref[...] = c[ts - 1:ts, :]
    c_ref[0] = c
    ct_ref[0] = c.T[8:16, :]
    gexp_ref[0] = jnp.concatenate(
        [jnp.broadcast_to(g[:, h:h + 1], (ts, LANES)) for h in range(GDN_HEADS)], axis=1)
    bexp_ref[0] = jnp.concatenate(
        [jnp.broadcast_to(beta[:, GDN_HEADS + h:GDN_HEADS + h + 1], (ts, LANES)) for h in range(GDN_HEADS)], axis=1)


def _prep(x, w_small, params):
    b, s, d = x.shape
    ts = min(512, s)
    hw = GDN_HEADS * LANES
    return pl.pallas_call(
        _prep_kernel,
        grid=(b, s // ts),
        in_specs=[pl.BlockSpec((1, ts, d), lambda i, j: (i, j, 0)),
                  pl.BlockSpec((d, LANES), lambda i, j: (0, 0)),
                  pl.BlockSpec((8, LANES), lambda i, j: (0, 0))],
        out_specs=[pl.BlockSpec((1, ts, hw), lambda i, j: (i, j, 0)),
                   pl.BlockSpec((1, ts, hw), lambda i, j: (i, j, 0)),
                   pl.BlockSpec((1, ts, LANES), lambda i, j: (i, j, 0)),
                   pl.BlockSpec((1, 8, ts), lambda i, j: (i, 0, j))],
        out_shape=[jax.ShapeDtypeStruct((b, s, hw), F32),
                   jax.ShapeDtypeStruct((b, s, hw), F32),
                   jax.ShapeDtypeStruct((b, s, LANES), F32),
                   jax.ShapeDtypeStruct((b, 8, s), F32)],
        scratch_shapes=[pltpu.VMEM((1, LANES), F32)],
        compiler_params=pltpu.CompilerParams(
            dimension_semantics=("parallel", "arbitrary"), vmem_limit_bytes=VMEM_LIMIT),
        name="prep",
    )(x, w_small, params)


def _each(fn, *lists):
    return [fn(*args) for args in zip(*lists)]


def _unit_lower_inverse(ms, masks):
    eye, blk16, blk32 = masks
    hi = lambda a, b: _dot(a, b, HI)
    n1 = _each(lambda m: -jnp.where(blk16, m, 0.0), ms)
    l1 = _each(lambda m: jnp.where(blk32 & jnp.logical_not(blk16), m, 0.0), ms)
    l2 = _each(lambda m: jnp.where(blk32, 0.0, m), ms)
    n2 = _each(hi, n1, n1)
    yield
    p = _each(lambda a, b: hi(eye + a, eye + b), n1, n2)
    n4 = _each(hi, n2, n2)
    yield
    p = _each(lambda a, b: hi(a, eye + b), p, n4)
    n8 = _each(hi, n4, n4)
    yield
    d_inv = _each(lambda a, b: hi(a, eye + b), p, n8)
    yield
    dl = _each(hi, d_inv, l1)
    yield
    a32 = _each(lambda a, b: a - hi(b, a), d_inv, dl)
    yield
    al = _each(hi, a32, l2)
    yield
    return _each(lambda a, b: a - hi(b, a), a32, al)


def _emit_zipped(main, side):
    live = [main, side]
    while live:
        for gen in list(live):
            try:
                next(gen)
            except StopIteration:
                live.remove(gen)


def _gdn_kernel(q_ref, k_ref, v_ref, z_ref, g_ref, b_ref, cwq_ref, cwk_ref, cwv_ref, nw_ref,
                o_ref, qn, kn, vn, st, sol_s, qk_s, qg_s, kd_s, gl_s):
    s = q_ref.shape[1]
    c = CHUNK
    nh = q_ref.shape[2] // LANES
    row = lax.broadcasted_iota(jnp.int32, (s, LANES), 0)

    def conv_silu(x, w):
        y = x * w[3:4, :]
        for sh in (1, 2, 3):
            xs = jnp.where(row >= sh, pltpu.roll(x, sh, axis=0), 0.0)
            y = y + xs * w[3 - sh:4 - sh, :]
        return y * jax.nn.sigmoid(y)

    def l2norm(x):
        return x * lax.rsqrt(jnp.sum(x * x, axis=-1, keepdims=True) + 1e-6)

    for j in range(nh):
        hs = slice(j * LANES, (j + 1) * LANES)
        qn[:, hs] = l2norm(conv_silu(q_ref[0, :, hs], cwq_ref[:, hs])) * (GDN_DK ** -0.5)
        kn[:, hs] = l2norm(conv_silu(k_ref[0, :, hs], cwk_ref[:, hs]))
        vn[:, hs] = conv_silu(v_ref[0, :, hs], cwv_ref[:, hs])
    st[...] = jnp.zeros_like(st)

    ri = lax.broadcasted_iota(jnp.int32, (c, c), 0)
    ci = lax.broadcasted_iota(jnp.int32, (c, c), 1)
    tril = ri >= ci
    strict = ri > ci
    t_inc = tril.astype(F32)
    eye = (ri == ci).astype(F32)
    blk16 = (ri >> 4) == (ci >> 4)
    blk32 = (ri >> 5) == (ci >> 5)
    l2 = lax.broadcasted_iota(jnp.int32, (c, 2 * LANES), 0)
    j2 = lax.broadcasted_iota(jnp.int32, (c, 2 * LANES), 1)
    ux = jnp.where((j2 >= c) | (l2 > j2), 1.0, 0.0).astype(F32)
    nw = nw_ref[...]

    heads = list(range(nh))
    lanes_of = [slice(j * LANES, (j + 1) * LANES) for j in heads]

    def local_stage(ns):
        pairs = [(i, j) for i in range(len(ns)) for j in heads]
        rows = [pl.ds(pl.multiple_of(n * c, c), c) for n in ns]
        q = [qn[rows[i], lanes_of[j]] for i, j in pairs]
        k = [kn[rows[i], lanes_of[j]] for i, j in pairs]
        v = [vn[rows[i], lanes_of[j]] for i, j in pairs]
        gb = [g_ref[0, rows[i], lanes_of[j]] for i, j in pairs]
        bb = [b_ref[0, rows[i], lanes_of[j]] for i, j in pairs]
        d = _each(lambda g: _dot(t_inc, jnp.concatenate([g, g], axis=1) * ux, HI), gb)
        kb = _each(lambda a, b: a * b, k, bb)
        kk = _each(lambda a, b: _dot_nt(a, b, HI), kb, k)
        qk = _each(lambda a, b: _dot_nt(a.astype(BF16), b.astype(BF16)), q, k)
        yield
        gc = [x[:, LANES:] for x in d]
        decay = [jnp.where(tril, jnp.exp(x[:, :c]), 0.0) for x in d]
        m = _each(lambda a, b: jnp.where(strict, a * b, 0.0), kk, decay)
        a_inv = yield from _unit_lower_inverse(m, (eye, blk16, blk32))
        yield
        egc = _each(jnp.exp, gc)
        rhs = _each(lambda vv, b, kbb, e: jnp.concatenate([vv * b, kbb * e], axis=1), v, bb, kb, egc)
        sol = _each(lambda a, r: _dot(a, r, HI), a_inv, rhs)
        gl = [x[c - 1:c, :] for x in gc]
        yield
        for p, (i, j) in enumerate(pairs):
            n = ns[i]
            qk_s[n, j] = (qk[p] * decay[p]).astype(BF16)
            qg_s[n, j] = (q[p] * egc[p]).astype(BF16)
            kd_s[n, j] = (k[p] * jnp.exp(gl[p] - gc[p])).astype(BF16)
            gl_s[n, j] = gl[p]
            sol_s[n, j] = sol[p]

    def state_stage(ns):
        for n in ns:
            rows = pl.ds(pl.multiple_of(n * c, c), c)
            state = [st[j] for j in heads]
            state_b = [x.astype(BF16) for x in state]
            v_new = [sol_s[n, j, :, :LANES] - _dot(sol_s[n, j, :, LANES:].astype(BF16), state_b[j])
                     for j in heads]
            yield
            v_new_b = [x.astype(BF16) for x in v_new]
            o = [_dot(qg_s[n, j], state_b[j]) + _dot(qk_s[n, j], v_new_b[j]) for j in heads]
            new_state = [state[j] * jnp.exp(gl_s[n, j]) + _dot_tn(kd_s[n, j], v_new_b[j]) for j in heads]
            yield
            for j in heads:
                st[j] = new_state[j]
                on = o[j] * lax.rsqrt(jnp.mean(o[j] * o[j], axis=-1, keepdims=True) + 1e-6) * nw
                z = z_ref[0, rows, lanes_of[j]]
                o_ref[0, rows, lanes_of[j]] = on * (z * jax.nn.sigmoid(z))
            yield

    nchunks = s // c
    group = 4
    ngroups = nchunks // group

    def chunks_of(gidx):
        return [gidx * group + i for i in range(group)]

    def run(gen):
        for _ in gen:
            pass

    run(local_stage(chunks_of(jnp.int32(0))))

    def both(gidx, carry):
        _emit_zipped(local_stage(chunks_of(gidx)), state_stage(chunks_of(gidx - 1)))
        return carry

    lax.fori_loop(1, ngroups, both, 0)
    run(state_stage(chunks_of(jnp.int32(ngroups - 1))))


def _gdn(proj3, gexp, bexp, conv_w, norm_w, cols):
    b, s, _ = proj3.shape
    nh = 2
    nc = s // CHUNK
    wd = nh * LANES
    cq, ck, cv, cz = (c0 // nh for c0 in cols)

    def blk(c0):
        return pl.BlockSpec((1, s, wd), lambda i, h: (i, 0, c0 + h))

    def cw(c0):
        return pl.BlockSpec((conv_w.shape[0], wd), lambda i, h: (0, c0 + h))

    head = pl.BlockSpec((1, s, wd), lambda i, h: (i, 0, h))
    return pl.pallas_call(
        _gdn_kernel,
        grid=(b, GDN_HEADS // nh),
        in_specs=[blk(cq), blk(ck), blk(cv), blk(cz), head, head,
                  cw(0), cw(GDN_HEADS // nh), cw(2 * GDN_HEADS // nh),
                  pl.BlockSpec((1, LANES), lambda i, h: (0, 0))],
        out_specs=head,
        out_shape=jax.ShapeDtypeStruct((b, s, GDN_HEADS * LANES), F32),
        scratch_shapes=[pltpu.VMEM((s, wd), F32), pltpu.VMEM((s, wd), F32),
                        pltpu.VMEM((s, wd), F32), pltpu.VMEM((nh, GDN_DK, LANES), F32),
                        pltpu.VMEM((nc, nh, CHUNK, 2 * LANES), F32), pltpu.VMEM((nc, nh, CHUNK, CHUNK), BF16),
                        pltpu.VMEM((nc, nh, CHUNK, LANES), BF16), pltpu.VMEM((nc, nh, CHUNK, LANES), BF16),
                        pltpu.VMEM((nc, nh, 1, LANES), F32)],
        compiler_params=pltpu.CompilerParams(
            dimension_semantics=("parallel", "parallel"), vmem_limit_bytes=VMEM_LIMIT),
        name="gdn",
    )(proj3, proj3, proj3, proj3, gexp, bexp, conv_w, conv_w, conv_w, norm_w)


def _fox_kernel(q_ref, k_ref, v_ref, c_ref, ct_ref, o_ref, *, tk):
    tq = q_ref.shape[1]
    nj = q_ref.shape[2] // LANES
    g = pl.program_id(1)
    qi = pl.program_id(2)
    q = q_ref[0]
    cblk = c_ref[0]
    lane = lax.broadcasted_iota(jnp.int32, (tq, LANES), 1)
    qpos = qi * tq + lax.broadcasted_iota(jnp.int32, (tq, tk), 0)
    kofs = lax.broadcasted_iota(jnp.int32, (tq, tk), 1)
    heads = list(range(nj))
    lanes_of = [slice(j * LANES, (j + 1) * LANES) for j in heads]
    ccol = [jnp.sum(jnp.where(lane == 8 + g * nj + j, cblk, 0.0), axis=-1, keepdims=True) for j in heads]
    qh = [q[:, hs] * jnp.asarray(FX_DH ** -0.5, BF16) for hs in lanes_of]

    def body(kv, carry):
        k0 = pl.multiple_of(kv * tk, tk)
        causal = qpos >= k0 + kofs
        kblk = k_ref[0, pl.ds(k0, tk), :]
        vblk = v_ref[0, pl.ds(k0, tk), :]
        sc = [_dot_nt(qh[j], kblk[:, lanes_of[j]]) for j in heads]
        crow = [ct_ref[0, pl.ds(g * nj + j, 1), pl.ds(k0, tk)] for j in heads]
        sc = [jnp.where(causal, sc[j] + ccol[j] - crow[j], -1e30) for j in heads]
        m_new = [jnp.maximum(carry[j][0], jnp.max(sc[j], axis=-1, keepdims=True)) for j in heads]
        a = [jnp.exp(carry[j][0] - m_new[j]) for j in heads]
        p = [jnp.exp(sc[j] - m_new[j]) for j in heads]
        l = [a[j] * carry[j][1] + jnp.sum(p[j], axis=-1, keepdims=True) for j in heads]
        acc = [a[j] * carry[j][2] + _dot(p[j].astype(BF16), vblk[:, lanes_of[j]]) for j in heads]
        return tuple((m_new[j], l[j], acc[j]) for j in heads)

    init = tuple((jnp.full((tq, 1), -1e30, F32), jnp.zeros((tq, 1), F32), jnp.zeros((tq, LANES), F32))
                 for _ in heads)
    nkv = (qi * tq + tq - 1) // tk + 1
    res = lax.fori_loop(0, nkv, body, init)
    o_ref[0] = jnp.concatenate([acc / l for _, l, acc in res], axis=1)


def _fox(pf3, c, ct):
    b, s, _ = pf3.shape
    wd = 4 * LANES
    tq = min(128, s)
    tk = min(256, s)
    ngrp = FX_HEADS * LANES // wd
    return pl.pallas_call(
        functools.partial(_fox_kernel, tk=tk),
        grid=(b, ngrp, s // tq),
        in_specs=[pl.BlockSpec((1, tq, wd), lambda i, h, t: (i, t, h)),
                  pl.BlockSpec((1, s, wd), lambda i, h, t: (i, 0, ngrp + h)),
                  pl.BlockSpec((1, s, wd), lambda i, h, t: (i, 0, 2 * ngrp + h)),
                  pl.BlockSpec((1, tq, LANES), lambda i, h, t: (i, t, 0)),
                  pl.BlockSpec((1, 8, s), lambda i, h, t: (i, 0, 0))],
        out_specs=pl.BlockSpec((1, tq, wd), lambda i, h, t: (i, t, h)),
        out_shape=jax.ShapeDtypeStruct((b, s, ngrp * wd), F32),
        compiler_params=pltpu.CompilerParams(
            dimension_semantics=("parallel", "parallel", "parallel"), vmem_limit_bytes=VMEM_LIMIT),
        name="fox",
    )(pf3, pf3, pf3, c, ct)


def _mix_kernel(oa_ref, ob_ref, ga_ref, gb_ref, x_ref, wa_ref, wb_ref, wo_ref, g1_ref, b1_ref, o_ref, *, alpha):
    ya = _dot(oa_ref[...].astype(BF16), wa_ref[...])
    yb = _dot(ob_ref[...].astype(BF16), wb_ref[...])
    mix = jax.nn.sigmoid(ga_ref[...]) * ya + jax.nn.sigmoid(gb_ref[...]) * yb
    z = alpha * x_ref[...] + _dot(mix.astype(BF16), wo_ref[...])
    o_ref[...] = _layernorm(z, g1_ref[...], b1_ref[...])


def _mix(oa, ob, proj, x2, wa, wb, wo, g1, b1, alpha):
    t, d = x2.shape
    tm = min(512, t)
    w, w2 = oa.shape[1], ob.shape[1]
    full = lambda r, c: pl.BlockSpec((r, c), lambda i: (0, 0))
    return pl.pallas_call(
        functools.partial(_mix_kernel, alpha=alpha),
        grid=(t // tm,),
        in_specs=[pl.BlockSpec((tm, w), lambda i: (i, 0)),
                  pl.BlockSpec((tm, w2), lambda i: (i, 0)),
                  pl.BlockSpec((tm, d), lambda i: (i, 0)),
                  pl.BlockSpec((tm, d), lambda i: (i, 1)),
                  pl.BlockSpec((tm, d), lambda i: (i, 0)),
                  full(w, d), full(w2, d), full(d, d), full(1, d), full(1, d)],
        out_specs=pl.BlockSpec((tm, d), lambda i: (i, 0)),
        out_shape=jax.ShapeDtypeStruct((t, d), F32),
        compiler_params=pltpu.CompilerParams(
            dimension_semantics=("parallel",), vmem_limit_bytes=VMEM_LIMIT),
        name="mix",
    )(oa, ob, proj, proj, x2, wa, wb, wo, g1, b1)


def _route_head(keys_ref, q_scr, hd, lh, scr, flat, cmask):
    s_scr, i_scr, best_scr, eh_scr = scr
    nk, kk = PEER_NKEYS, PEER_TOPK
    iota_k = lax.broadcasted_iota(jnp.int32, (nk, LANES), 0)
    neg = jnp.float32(-jnp.inf)
    for p in range(2):
        vals = _dot_nt(keys_ref[hd * 2 + p], q_scr[hd * 2 + p, lh])
        for r in range(kk):
            m = jnp.max(vals, axis=0, keepdims=True)
            am = jnp.min(jnp.where(vals == m, iota_k, nk), axis=0, keepdims=True)
            s_scr[p * kk + r:p * kk + r + 1, :] = m
            i_scr[p * kk + r:p * kk + r + 1, :] = am
            vals = jnp.where(iota_k == am, neg, vals)
    s1 = s_scr[kk:kk + 8, :]
    i1 = i_scr[kk:kk + 8, :]
    cand = [s_scr[0:1, :] + s_scr[kk:2 * kk, :]]
    cidx = [i_scr[0:1, :] * nk + i_scr[kk:2 * kk, :]]
    for a in range(1, 8):
        cand.append(s_scr[a:a + 1, :] + s1)
        cidx.append(i_scr[a:a + 1, :] * nk + i1)
    cand.append(s_scr[8:kk, :] + s_scr[kk:kk + 1, :])
    cidx.append(i_scr[8:kk, :] * nk + i_scr[kk:kk + 1, :])
    vals = jnp.concatenate(cand, axis=0) + cmask
    cidx = jnp.concatenate(cidx, axis=0)
    for r in range(kk):
        m = jnp.max(vals, axis=0, keepdims=True)
        am = jnp.min(jnp.where(vals == m, flat, 2 * kk * kk), axis=0, keepdims=True)
        sel = flat == am
        best_scr[r:r + 1, :] = m
        eh_scr[r:r + 1, :] = jnp.max(jnp.where(sel, cidx, -1), axis=0, keepdims=True)
        vals = jnp.where(sel, neg, vals)
    bs = best_scr[...]
    ex = jnp.exp(bs - bs[0:1, :])
    return ex / jnp.sum(ex, axis=0, keepdims=True), eh_scr[...]


def _candidate_tables():
    kk = PEER_TOPK
    pairs = [(0, bb) for bb in range(kk)]
    for a in range(1, 8):
        pairs += [(a, bb) for bb in range(8)]
    pairs += [(a, 0) for a in range(8, kk)]
    real = [(a + 1) * (bb + 1) <= kk for a, bb in pairs]
    flat = [a * kk + bb if ok else kk * kk + r for r, ((a, bb), ok) in enumerate(zip(pairs, real))]
    flat = jnp.broadcast_to(jnp.asarray(flat, jnp.int32)[:, None], (len(pairs), LANES))
    cmask = jnp.broadcast_to(jnp.asarray([0.0 if ok else -jnp.inf for ok in real], F32)[:, None],
                             (len(pairs), LANES))
    return flat, cmask


def _route_kernel(h_ref, wq_ref, keys_ref, flat_ref, cmask_ref, e_ref, g_ref,
                  q_scr, s_scr, i_scr, best_scr, eh_scr, eall_scr, gall_scr):
    tg = h_ref.shape[0]
    kk = PEER_TOPK
    nlh = tg // LANES
    q = _dot(h_ref[...].astype(BF16), wq_ref[...])
    for j in range(2 * PEER_HEADS):
        for lh in range(nlh):
            q_scr[j, lh] = q[lh * LANES:(lh + 1) * LANES, j * LANES:(j + 1) * LANES].astype(BF16)
    flat = flat_ref[...]
    cmask = cmask_ref[...]

    def piece(p, carry):
        hd = p // nlh
        lh = p % nlh
        gates, ids = _route_head(keys_ref, q_scr, hd, lh, (s_scr, i_scr, best_scr, eh_scr), flat, cmask)
        r0 = pl.multiple_of(hd * kk, kk)
        gall_scr[lh, pl.ds(r0, kk), :] = gates
        eall_scr[lh, pl.ds(r0, kk), :] = ids
        return carry

    lax.fori_loop(0, PEER_HEADS * nlh, piece, 0)
    for lh in range(nlh):
        e_ref[lh * LANES:(lh + 1) * LANES, :] = eall_scr[lh].T
        g_ref[lh * LANES:(lh + 1) * LANES, :] = gall_scr[lh].T


def _route(h, wq, keys, row0, t):
    d = h.shape[1]
    tg = PEER_GROUP
    g0 = row0 // tg
    nq = wq.shape[1]
    kk = PEER_TOPK
    hk = PEER_HEADS * kk
    nlh = tg // LANES
    flat, cmask = _candidate_tables()
    const = lambda shape: pl.BlockSpec(shape, lambda i: (0,) * len(shape))
    return pl.pallas_call(
        _route_kernel,
        grid=(t // tg,),
        in_specs=[pl.BlockSpec((tg, d), lambda i: (g0 + i, 0)),
                  const((d, nq)), const(keys.shape), const(flat.shape), const(cmask.shape)],
        out_specs=[pl.BlockSpec((tg, hk), lambda i: (i, 0)), pl.BlockSpec((tg, hk), lambda i: (i, 0))],
        out_shape=[jax.ShapeDtypeStruct((t, hk), jnp.int32), jax.ShapeDtypeStruct((t, hk), F32)],
        scratch_shapes=[pltpu.VMEM((2 * PEER_HEADS, nlh, LANES, LANES), BF16),
                        pltpu.VMEM((2 * kk, LANES), F32), pltpu.VMEM((2 * kk, LANES), jnp.int32),
                        pltpu.VMEM((kk, LANES), F32), pltpu.VMEM((kk, LANES), jnp.int32),
                        pltpu.VMEM((nlh, hk, LANES), jnp.int32), pltpu.VMEM((nlh, hk, LANES), F32)],
        compiler_params=pltpu.CompilerParams(
            dimension_semantics=("parallel",), vmem_limit_bytes=VMEM_LIMIT),
        name="route",
    )(h, wq, keys, flat, cmask)


PEER_SC_SHARE_PCT = 50
PEER_TC_FIRST_PCT = 47
SC_WORKERS = 32
SC_LANES = 16
SC_CHUNK = 32
SC_DBLK = 16
SC_EBLK = 4
SC_JUNROLL = 8


def _sc_mesh():
    return plsc.VectorSubcoreMesh(core_axis_name="c", subcore_axis_name="s")


def _sc_worker():
    return lax.axis_index("s") * 2 + lax.axis_index("c")


def _sc_pre(ids, h, u_tab, row0):
    ts, hk = ids.shape
    d = h.shape[1]
    per_w = ts // SC_WORKERS
    nch = hk // SC_CHUNK

    def body(ids_hbm, h_hbm, u_hbm, out_hbm, idx_a, idx_b, h_a, h_b, rows_a, rows_b, pre_v,
             sem_a, sem_b, tsem_a, tsem_b):
        base = _sc_worker() * per_w
        lane = lax.iota(jnp.int32, SC_LANES)
        bufs = (rows_a, rows_b)
        sems = (sem_a, sem_b)
        idxs, hs, tsems = (idx_a, idx_b), (h_a, h_b), (tsem_a, tsem_b)

        def token_copies(t, s):
            return (pltpu.make_async_copy(ids_hbm.at[t], idxs[s], tsems[s]),
                    pltpu.make_async_copy(h_hbm.at[row0 + t], hs[s], tsems[s]))

        def prefetch(t, s):
            for cp in token_copies(t, s):
                cp.start()

        def arrived(t, s):
            for cp in token_copies(t, s):
                cp.wait()

        def process(t, s):
            idx_v, h_v = idxs[s], hs[s]

            def gather(c):
                return pltpu.async_copy(u_hbm.at[idx_v.at[pl.ds(c * SC_CHUNK, SC_CHUNK)]], bufs[c % 2], sems[c % 2])

            cps = {0: gather(0)}
            for c in range(nch):
                if c + 1 < nch:
                    cps[c + 1] = gather(c + 1)
                cps[c].wait()
                rows = bufs[c % 2]
                for g in range(SC_CHUNK // SC_LANES):
                    def experts(q, vec, g=g, rows=rows):
                        e0 = g * SC_LANES + q * SC_EBLK

                        def span(jb, accs, rows=rows, e0=e0):
                            accs = list(accs)
                            for jj in range(SC_JUNROLL):
                                sl = pl.ds((jb * SC_JUNROLL + jj) * SC_LANES, SC_LANES)
                                hv = h_v[sl]
                                for i in range(SC_EBLK):
                                    accs[i] = accs[i] + rows[e0 + i, sl] * hv
                            return tuple(accs)
                        accs = lax.fori_loop(0, d // (SC_LANES * SC_JUNROLL), span,
                                             tuple(jnp.zeros((SC_LANES,), F32) for _ in range(SC_EBLK)))
                        for i in range(SC_EBLK):
                            vec = jnp.where(lane == q * SC_EBLK + i, jnp.sum(accs[i]), vec)
                        return vec
                    vec = lax.fori_loop(0, SC_LANES // SC_EBLK, experts, jnp.zeros((SC_LANES,), F32))
                    pre_v[pl.ds(c * SC_CHUNK + g * SC_LANES, SC_LANES)] = vec
            pltpu.sync_copy(pre_v, out_hbm.at[t])

        last = base + per_w - 1
        prefetch(base, 0)

        def pair(i, carry):
            t0 = base + 2 * i
            prefetch(t0 + 1, 1)
            arrived(t0, 0)
            process(t0, 0)
            nxt = jnp.minimum(t0 + 2, last)
            prefetch(nxt, 0)
            arrived(t0 + 1, 1)
            process(t0 + 1, 1)
            return carry

        lax.fori_loop(0, per_w // 2, pair, 0)
        arrived(last, 0)

    assert per_w % 2 == 0
    return pl.kernel(
        body, out_type=jax.ShapeDtypeStruct((ts, hk), F32), mesh=_sc_mesh(),
        scratch_types=[pltpu.VMEM((hk,), jnp.int32), pltpu.VMEM((hk,), jnp.int32),
                       pltpu.VMEM((d,), F32), pltpu.VMEM((d,), F32),
                       pltpu.VMEM((SC_CHUNK, d), F32), pltpu.VMEM((SC_CHUNK, d), F32),
                       pltpu.VMEM((hk,), F32), pltpu.SemaphoreType.DMA, pltpu.SemaphoreType.DMA,
                       pltpu.SemaphoreType.DMA, pltpu.SemaphoreType.DMA],
        compiler_params=pltpu.CompilerParams(needs_layout_passes=False),
        name="sc_pre",
    )(ids, h, u_tab)


def _sc_out(ids, act, v_tab):
    ts, hk = ids.shape
    d = v_tab.shape[1]
    per_w = ts // SC_WORKERS
    nch = hk // SC_CHUNK

    def body(ids_hbm, act_hbm, v_hbm, out_hbm, idx_a, idx_b, act_a, act_b, rows_a, rows_b, y_v,
             sem_a, sem_b, tsem_a, tsem_b):
        base = _sc_worker() * per_w
        bufs = (rows_a, rows_b)
        sems = (sem_a, sem_b)
        idxs, acts, tsems = (idx_a, idx_b), (act_a, act_b), (tsem_a, tsem_b)

        def token_copies(t, s):
            return (pltpu.make_async_copy(ids_hbm.at[t], idxs[s], tsems[s]),
                    pltpu.make_async_copy(act_hbm.at[t], acts[s], tsems[s]))

        def prefetch(t, s):
            for cp in token_copies(t, s):
                cp.start()

        def arrived(t, s):
            for cp in token_copies(t, s):
                cp.wait()

        def process(t, s):
            idx_v, act_v = idxs[s], acts[s]

            def gather(c):
                return pltpu.async_copy(v_hbm.at[idx_v.at[pl.ds(c * SC_CHUNK, SC_CHUNK)]], bufs[c % 2], sems[c % 2])

            for j in range(d // SC_LANES):
                y_v[pl.ds(j * SC_LANES, SC_LANES)] = jnp.zeros((SC_LANES,), F32)
            cps = {0: gather(0)}
            for c in range(nch):
                if c + 1 < nch:
                    cps[c + 1] = gather(c + 1)
                cps[c].wait()
                rows = bufs[c % 2]
                for db in range(d // (SC_DBLK * SC_LANES)):
                    def expert(r, accs, c=c, rows=rows, db=db):
                        a = plsc.load_gather(act_v, [jnp.zeros((SC_LANES,), jnp.int32) + (c * SC_CHUNK + r)])
                        return tuple(
                            accs[j] + a * rows[r, pl.ds((db * SC_DBLK + j) * SC_LANES, SC_LANES)]
                            for j in range(SC_DBLK))
                    accs = lax.fori_loop(0, SC_CHUNK, expert,
                                         tuple(jnp.zeros((SC_LANES,), F32) for _ in range(SC_DBLK)))
                    for j in range(SC_DBLK):
                        plsc.addupdate(y_v.at[pl.ds((db * SC_DBLK + j) * SC_LANES, SC_LANES)], accs[j])
            pltpu.sync_copy(y_v, out_hbm.at[t])

        last = base + per_w - 1
        prefetch(base, 0)

        def pair(i, carry):
            t0 = base + 2 * i
            prefetch(t0 + 1, 1)
            arrived(t0, 0)
            process(t0, 0)
            nxt = jnp.minimum(t0 + 2, last)
            prefetch(nxt, 0)
            arrived(t0 + 1, 1)
            process(t0 + 1, 1)
            return carry

        lax.fori_loop(0, per_w // 2, pair, 0)
        arrived(last, 0)

    assert per_w % 2 == 0
    return pl.kernel(
        body, out_type=jax.ShapeDtypeStruct((ts, d), F32), mesh=_sc_mesh(),
        scratch_types=[pltpu.VMEM((hk,), jnp.int32), pltpu.VMEM((hk,), jnp.int32),
                       pltpu.VMEM((hk,), F32), pltpu.VMEM((hk,), F32),
                       pltpu.VMEM((SC_CHUNK, d), F32), pltpu.VMEM((SC_CHUNK, d), F32),
                       pltpu.VMEM((d,), F32), pltpu.SemaphoreType.DMA, pltpu.SemaphoreType.DMA,
                       pltpu.SemaphoreType.DMA, pltpu.SemaphoreType.DMA],
        compiler_params=pltpu.CompilerParams(needs_layout_passes=False),
        name="sc_out",
    )(ids, act, v_tab)


def _act_kernel(pre_ref, gate_ref, after_ref, o_ref):
    del after_ref
    pre = pre_ref[...]
    o_ref[...] = 0.5 * pre * (1.0 + lax.erf(pre * (2.0 ** -0.5))) * gate_ref[...]


def _act(pre, gate, after):
    t, hk = pre.shape
    tm = PEER_GROUP
    assert t % tm == 0
    spec = pl.BlockSpec((tm, hk), lambda i: (i, 0))
    return pl.pallas_call(
        _act_kernel, grid=(t // tm,),
        in_specs=[spec, spec, pl.BlockSpec((8, LANES), lambda i: (0, 0))], out_specs=spec,
        out_shape=jax.ShapeDtypeStruct((t, hk), F32),
        compiler_params=pltpu.CompilerParams(dimension_semantics=("parallel",)),
        name="peer_act",
    )(pre, gate, after)


def _ln2_kernel(h_ref, y_ref, g_ref, b_ref, prev_ref, o_ref, *, alpha):
    del prev_ref
    o_ref[...] = _layernorm(alpha * h_ref[...] + y_ref[...], g_ref[...], b_ref[...])


def _ln2(h, y, g2, b2, alpha, row0, out_prev):
    t, d = y.shape
    tm = PEER_GROUP
    assert t % tm == 0 and row0 % tm == 0
    g0 = row0 // tm
    spec_full = pl.BlockSpec((tm, d), lambda i: (g0 + i, 0))
    vec = pl.BlockSpec((1, d), lambda i: (0, 0))
    return pl.pallas_call(
        functools.partial(_ln2_kernel, alpha=alpha), grid=(t // tm,),
        in_specs=[spec_full, pl.BlockSpec((tm, d), lambda i: (i, 0)), vec, vec,
                  pl.BlockSpec(memory_space=pl.ANY)],
        out_specs=spec_full,
        out_shape=jax.ShapeDtypeStruct(out_prev.shape, F32),
        input_output_aliases={4: 0},
        compiler_params=pltpu.CompilerParams(dimension_semantics=("parallel",)),
        name="peer_ln2",
    )(h, y, g2, b2, out_prev)


def _peer_kernel(h_cur_ref, h_nxt_ref, wq_ref, keys_ref, flat_ref, cmask_ref, uv_hbm, g2_ref, b2_ref, *rest,
                 tt, alpha, has_prev):
    (o_ref, q_scr, s_scr, i_scr, best_scr, eh_scr, eall_scr, idv_scr, ids_smem, gate_scr,
     buf_a, buf_b, sem, idsem, y_scr) = rest[1:] if has_prev else rest
    _peer_body(h_cur_ref, h_nxt_ref, wq_ref, keys_ref, flat_ref, cmask_ref, uv_hbm, g2_ref, b2_ref, o_ref,
               q_scr, s_scr, i_scr, best_scr, eh_scr, eall_scr, idv_scr, ids_smem, gate_scr,
               buf_a, buf_b, sem, idsem, y_scr, tt=tt, alpha=alpha)


def _peer_body(h_cur_ref, h_nxt_ref, wq_ref, keys_ref, flat_ref, cmask_ref, uv_hbm, g2_ref, b2_ref, o_ref,
               q_scr, s_scr, i_scr, best_scr, eh_scr, eall_scr, idv_scr, ids_smem, gate_scr,
               buf_a, buf_b, sem, idsem, y_scr, *, tt, alpha):
    s = pl.program_id(0)
    tg, d = h_cur_ref.shape
    nk, kk = PEER_NKEYS, PEER_TOPK
    hk = PEER_HEADS * kk
    nlh = tg // LANES
    nsub = tg // (2 * tt)
    assert nsub == PEER_HEADS * nlh
    nslab, sub = uv_hbm.shape[1], uv_hbm.shape[2]
    half = nslab // 2
    bufs = (buf_a, buf_b)
    last = pl.num_programs(0) - 1
    rslot = s % 3
    pslot = (s + 2) % 3
    eslot = (s + 1) % 3

    def issue(idslot, row0, slot):
        for t in range(tt):
            for k in range(hk):
                e = ids_smem[idslot, row0 + t, k]
                pltpu.make_async_copy(uv_hbm.at[e], bufs[slot].at[:, pl.ds((t * hk + k) * sub, sub), :],
                                      sem.at[slot]).start(priority=k % 2)

    def wait(slot):
        pltpu.make_async_copy(bufs[slot], bufs[slot], sem.at[slot]).wait()

    q = _dot(h_nxt_ref[...].astype(BF16), wq_ref[...])
    for j in range(2 * PEER_HEADS):
        for lh in range(nlh):
            q_scr[j, lh] = q[lh * LANES:(lh + 1) * LANES, j * LANES:(j + 1) * LANES].astype(BF16)
    flat = flat_ref[...]
    cmask = cmask_ref[...]

    def route_piece(piece):
        hd = piece // nlh
        lh = piece % nlh
        gates, ids = _route_head(keys_ref, q_scr, hd, lh, (s_scr, i_scr, best_scr, eh_scr), flat, cmask)
        r0 = pl.multiple_of(hd * kk, kk)
        gate_scr[rslot, lh, pl.ds(r0, kk), :] = gates
        eall_scr[lh, pl.ds(r0, kk), :] = ids

    lane = lax.broadcasted_iota(jnp.int32, (hk, LANES), 1)

    def compute(row0, slot):
        buf = bufs[slot]
        gt = gate_scr[eslot, row0 // LANES]
        lane0 = row0 % LANES
        for t in range(tt):
            hrow = h_cur_ref[pl.ds(row0 + t, 1), :]

            def rows(a, c):
                return buf[a, pl.ds(t * hk * sub + c, hk, stride=sub), :]

            part = None
            for j in range(half):
                for c in range(sub):
                    seg = j * sub + c
                    term = rows(j, c) * hrow[:, seg * LANES:(seg + 1) * LANES]
                    part = term if part is None else part + term
            pre = jnp.sum(part, axis=-1, keepdims=True)
            gate = jnp.sum(jnp.where(lane == lane0 + t, gt, 0.0), axis=-1, keepdims=True)
            act = 0.5 * pre * (1.0 + lax.erf(pre * (2.0 ** -0.5))) * gate
            yrow = slot * tt + t
            for j in range(half):
                for c in range(sub):
                    seg = j * sub + c
                    y_scr[yrow:yrow + 1, seg * LANES:(seg + 1) * LANES] = jnp.sum(
                        act * rows(half + j, c), axis=0, keepdims=True)

    def substep(j, carry):
        row0 = pl.multiple_of(j * 2 * tt, 2 * tt)
        issue(eslot, row0 + tt, 1)
        route_piece(j)
        wait(0)
        compute(row0, 0)
        wait(1)
        wrap = j == nsub - 1
        issue(jnp.where(wrap, pslot, eslot), jnp.where(wrap, 0, row0 + 2 * tt), 0)
        compute(row0 + tt, 1)
        z = alpha * h_cur_ref[pl.ds(row0, 2 * tt), :] + y_scr[...]
        o_ref[pl.ds(row0, 2 * tt), :] = _layernorm(z, g2_ref[...], b2_ref[...])
        return carry

    @pl.when(s < 2)
    def _():
        o_ref[...] = jnp.zeros_like(o_ref)

        def piece(p, carry):
            route_piece(p)
            return carry
        lax.fori_loop(0, nsub, piece, 0)

    @pl.when(s == 1)
    def _():
        issue(pslot, 0, 0)

    @pl.when(s >= 2)
    def _():
        lax.fori_loop(0, nsub, substep, 0)

    @pl.when(s == last)
    def _():
        wait(0)

    for lh in range(nlh):
        idv_scr[lh * LANES:(lh + 1) * LANES, :] = eall_scr[lh].T
    publish = pltpu.make_async_copy(idv_scr, ids_smem.at[rslot], idsem)
    publish.start()
    publish.wait()


def _peer(h1, wq, keys, uv_tab, g2, b2, alpha, row0=0, t=None, out_prev=None):
    d = h1.shape[1]
    t = h1.shape[0] if t is None else t
    tg = PEER_GROUP
    tt = 8
    assert t % tg == 0 and row0 % tg == 0
    ngrp = t // tg
    g0 = row0 // tg
    has_prev = out_prev is not None
    nq = wq.shape[1]
    kk = PEER_TOPK
    hk = PEER_HEADS * kk
    nlh = tg // LANES
    nslab, sub = uv_tab.shape[1], uv_tab.shape[2]
    flat, cmask = _candidate_tables()
    const = lambda shape: pl.BlockSpec(shape, lambda i: (0,) * len(shape))
    prev_specs = [pl.BlockSpec(memory_space=pl.ANY)] if has_prev else []
    prev_args = [out_prev] if has_prev else []
    return pl.pallas_call(
        functools.partial(_peer_kernel, tt=tt, alpha=alpha, has_prev=has_prev),
        grid=(ngrp + 2,),
        in_specs=[pl.BlockSpec((tg, d), lambda i: (g0 + jnp.maximum(i - 2, 0), 0)),
                  pl.BlockSpec((tg, d), lambda i: (g0 + jnp.minimum(i, ngrp - 1), 0)),
                  const((d, nq)), const(keys.shape), const(flat.shape), const(cmask.shape),
                  pl.BlockSpec(memory_space=pl.ANY),
                  const((1, d)), const((1, d))] + prev_specs,
        out_specs=pl.BlockSpec((tg, d), lambda i: (g0 + jnp.maximum(i - 2, 0), 0)),
        out_shape=jax.ShapeDtypeStruct(h1.shape, F32),
        input_output_aliases={9: 0} if has_prev else {},
        scratch_shapes=[pltpu.VMEM((2 * PEER_HEADS, nlh, LANES, LANES), BF16),
                        pltpu.VMEM((2 * kk, LANES), F32), pltpu.VMEM((2 * kk, LANES), jnp.int32),
                        pltpu.VMEM((kk, LANES), F32), pltpu.VMEM((kk, LANES), jnp.int32),
                        pltpu.VMEM((nlh, hk, LANES), jnp.int32),
                        pltpu.VMEM((tg, hk), jnp.int32),
                        pltpu.SMEM((3, tg, hk), jnp.int32),
                        pltpu.VMEM((3, nlh, hk, LANES), F32),
                        pltpu.VMEM((nslab, tt * hk * sub, LANES), F32),
                        pltpu.VMEM((nslab, tt * hk * sub, LANES), F32),
                        pltpu.SemaphoreType.DMA((2,)), pltpu.SemaphoreType.DMA,
                        pltpu.VMEM((2 * tt, d), F32)],
        compiler_params=pltpu.CompilerParams(
            dimension_semantics=("arbitrary",), vmem_limit_bytes=VMEM_LIMIT),
        name="peer",
    )(h1, h1, wq, keys, flat, cmask, uv_tab, g2, b2, *prev_args)


def _layer(h, w_in, conv_w, a_log, dt_bias, norm_w, f_bias, w_out_gdn, w_out_fox, w_o, ln1_g, ln1_b,
           peer_wq, peer_keys, peer_u, peer_v, ln2_g, ln2_b, alpha):
    b, s, d = h.shape
    t = b * s
    qk = GDN_HEADS * GDN_DK
    fw = FX_HEADS * FX_DH
    o_gz = 4 * qk
    o_ga = o_gz
    o_fq = o_ga + 2 * GDN_HEADS
    o_ff = o_fq + 3 * fw
    o_gate = o_ff + FX_HEADS
    w_big = jnp.concatenate([w_in[:, o_gate:], w_in[:, :o_gz]], axis=1).astype(BF16)
    pad = LANES - FX_DH
    w_fox = jnp.pad(w_in[:, o_fq:o_ff].reshape(d, 3 * FX_HEADS, FX_DH), ((0, 0), (0, 0), (0, pad)))
    w_fox = w_fox.reshape(d, 3 * FX_HEADS * LANES).astype(BF16)
    w_out_fox_p = jnp.pad(w_out_fox.reshape(FX_HEADS, FX_DH, d), ((0, 0), (0, pad), (0, 0)))
    w_out_fox_p = w_out_fox_p.reshape(FX_HEADS * LANES, d).astype(BF16)
    n_small = 2 * GDN_HEADS + FX_HEADS
    w_small = jnp.concatenate([w_in[:, o_ga:o_fq], w_in[:, o_ff:o_gate],
                               jnp.zeros((d, LANES - n_small), F32)], axis=1)
    params = jnp.zeros((8, LANES), F32)
    params = params.at[0, :GDN_HEADS].set(a_log).at[1, :GDN_HEADS].set(dt_bias)
    params = params.at[2, 2 * GDN_HEADS:n_small].set(f_bias)

    x2 = h.reshape(t, d)
    proj = _in_proj(x2, w_big, F32, "in_proj")
    pf = _in_proj(x2, w_fox, BF16, "in_proj_fox")
    gexp, bexp, c, ct = _prep(h, w_small, params)
    proj3 = proj.reshape(b, s, proj.shape[1])
    gdn0 = 2 * d // LANES
    oa = _gdn(proj3, gexp, bexp, conv_w, norm_w.reshape(1, LANES),
              (gdn0, gdn0 + GDN_HEADS, gdn0 + 2 * GDN_HEADS, gdn0 + 3 * GDN_HEADS))
    ob = _fox(pf.reshape(b, s, pf.shape[1]), c, ct)
    h1 = _mix(oa.reshape(t, qk), ob.reshape(t, FX_HEADS * LANES), proj, x2,
              w_out_gdn.astype(BF16), w_out_fox_p, w_o.astype(BF16),
              ln1_g.reshape(1, d), ln1_b.reshape(1, d), alpha)
    keys = peer_keys.reshape(2 * PEER_HEADS, PEER_NKEYS, peer_keys.shape[-1]).astype(BF16)
    ne = peer_u.shape[0]
    uslabs = d // (PEER_SUB * LANES)
    uv_tab = jnp.concatenate([peer_u.reshape(ne, uslabs, PEER_SUB, LANES),
                              peer_v.reshape(ne, uslabs, PEER_SUB, LANES)], axis=1)
    wq = peer_wq.astype(BF16)
    g2, b2 = ln2_g.reshape(1, d), ln2_b.reshape(1, d)
    grp = PEER_GROUP
    t_sc = (t * PEER_SC_SHARE_PCT // 100) // (2 * grp) * (2 * grp)
    t_tc = t - t_sc
    if t_sc:
        t_s1 = t_sc // 2
        ids_1, gate_1 = _route(h1, wq, keys, t_tc, t_s1)
        pre_1 = _sc_pre(ids_1, h1, peer_u, t_tc)
        ids_2, gate_2 = _route(h1, wq, keys, t_tc + t_s1, t_sc - t_s1)
        pre_2 = _sc_pre(ids_2, h1, peer_u, t_tc + t_s1)
        ids_sc = jnp.concatenate([ids_1, ids_2], axis=0)
        gate_sc = jnp.concatenate([gate_1, gate_2], axis=0)
        pre = jnp.concatenate([pre_1, pre_2], axis=0)
        t_a = (t_tc * PEER_TC_FIRST_PCT // 100) // grp * grp
        out = _peer(h1, wq, keys, uv_tab, g2, b2, alpha, 0, t_a)
        y_sc = _sc_out(ids_sc, _act(pre, gate_sc, out), peer_v)
        out = _peer(h1, wq, keys, uv_tab, g2, b2, alpha, t_a, t_tc - t_a, out)
        out = _ln2(h1, y_sc, g2, b2, alpha, t_tc, out)
    else:
        out = _peer(h1, wq, keys, uv_tab, g2, b2, alpha)
    return out.reshape(b, s, d)


def kernel(x, w_in, gdn_conv_w, gdn_a_log, gdn_dt_bias, gdn_norm_w, fox_f_bias, w_out_gdn, w_out_fox, w_o,
           ln1_g, ln1_b, peer_wq, peer_keys, peer_u, peer_v, ln2_g, ln2_b):
    depth = w_in.shape[0]
    alpha = (2.0 * depth) ** 0.25
    params = (w_in, gdn_conv_w, gdn_a_log, gdn_dt_bias, gdn_norm_w, fox_f_bias, w_out_gdn, w_out_fox, w_o,
              ln1_g, ln1_b, peer_wq, peer_keys, peer_u, peer_v, ln2_g, ln2_b)

    def layer_slice(p, l):
        return p.reshape(p.shape[1:]) if depth == 1 else p[l]

    h = x
    for l in range(depth):
        h = _layer(h, *(layer_slice(p, l) for p in params), alpha)
    return h
```
